```python
import jax
import jax.numpy as jnp
from jax import lax
import numpy as np

D_MODEL = 1024
BATCH = 8
SEQ = 2048
DEPTH = 2

CTX_LEN = 256
GRID_W = 64
N_EVEN = (DEPTH + 1) // 2
N_ODD = DEPTH // 2
EPS = 1e-6
CONV_K = 4
MIX_WIDTH = D_MODEL
RG_WIDTH = MIX_WIDTH // 2
RG_BLOCKS = 8
RG_BLOCK = RG_WIDTH // RG_BLOCKS
RG_C = 8.0
DN_HEADS = 4
DN_DK = MIX_WIDTH // (2 * DN_HEADS)
DN_DV = MIX_WIDTH // (2 * DN_HEADS)
DN_CHUNK = 64
HG_HEADS = 4
HG_DF = 128
HG_DI = MIX_WIDTH // (2 * HG_HEADS)
GLA_HEADS = 4
GLA_DK = 64
GLA_DV = MIX_WIDTH // (2 * GLA_HEADS)
GLA_RANK = 16
GLA_GATE_NORM = 16.0
GLA_CHUNK = 32
D_FF = 2816
N_EXPERTS = 8
TOP_K = 2
D_EXPERT = 2816

E_SIZES = (RG_WIDTH, RG_WIDTH, DN_HEADS * (2 * DN_DK + DN_DV), DN_HEADS * DN_DV, 2 * DN_HEADS, 2 * DN_HEADS)
E_IN = sum(E_SIZES)
O_SIZES = (HG_HEADS * HG_DF, HG_HEADS * HG_DF, HG_HEADS * HG_DF, HG_HEADS * HG_DI, HG_HEADS * HG_DI,
           GLA_HEADS * GLA_DK, GLA_HEADS * GLA_DK, GLA_HEADS * GLA_DV, GLA_HEADS * GLA_DV, 2 * GLA_RANK)
O_IN = sum(O_SIZES)

kernel_name = 'bidir_hybrid_rglru_deltanet_hgrn2_gla_moe'


def rmsnorm(x, g):
    xf = x.astype(jnp.float32)
    y = xf * lax.rsqrt(jnp.mean(jnp.square(xf), axis=-1, keepdims=True) + EPS)
    return (y * g.astype(jnp.float32)).astype(x.dtype)


def modulate(h, g, shift, scale):
    return rmsnorm(h, g) * (1.0 + scale) + shift


def split_cols(u, sizes):
    out, start = [], 0
    for s in sizes:
        out.append(u[..., start:start + s])
        start += s
    return out


def to_heads(t, n):
    b, l, _ = t.shape
    return t.reshape(b, l, n, -1).transpose(0, 2, 1, 3)


def from_heads(t):
    b, n, l, d = t.shape
    return t.transpose(0, 2, 1, 3).reshape(b, l, n * d)


def l2norm(t):
    return t * lax.rsqrt(jnp.sum(t * t, axis=-1, keepdims=True) + EPS)


def gated_rmsnorm(o, z, g):
    o = o * lax.rsqrt(jnp.mean(o * o, axis=-1, keepdims=True) + EPS) * g
    return o * jax.nn.silu(z)


def flip_seq(t, d, axis):
    return jnp.flip(t, axis=axis) if d == 1 else t


def dwconv_centred(x, w, b=None):
    k, ch = w.shape
    y = lax.conv_general_dilated(x, w[:, None, :].astype(x.dtype), (1,), [(k // 2, k - 1 - k // 2)],
                                 dimension_numbers=('NWC', 'WIO', 'NWC'), feature_group_count=ch)
    return y if b is None else y + b


def rglru_scan(u, gate_w, gate_b, lam, h0):
    b, l, w = u.shape
    gates = jnp.einsum('blhi,ghij->gblhj', u.reshape(b, l, RG_BLOCKS, RG_BLOCK), gate_w).reshape(2, b, l, w)
    gates = gates + gate_b[:, None, None, :]
    r, i = jax.nn.sigmoid(gates[0]), jax.nn.sigmoid(gates[1])
    log_a = -RG_C * r * jax.nn.softplus(-lam)
    a = jnp.exp(log_a)
    xin = jnp.sqrt(-jnp.expm1(2.0 * log_a)) * (i * u)
    xin = xin.at[:, 0].add(a[:, 0] * h0)

    def comb(left, right):
        return left[0] * right[0], right[0] * left[1] + right[1]

    return lax.associative_scan(comb, (a, xin), axis=1)[1]


def rglru_bidir(u_lat, u_ctx, gate_w, gate_b, lam):
    h0 = jnp.zeros((u_ctx.shape[0], u_ctx.shape[2]), jnp.float32)
    y_lat, y_ctx = 0.0, 0.0
    for d in range(2):
        hc = rglru_scan(flip_seq(u_ctx, d, 1), gate_w[d], gate_b[d], lam[d], h0)
        hl = rglru_scan(flip_seq(u_lat, d, 1), gate_w[d], gate_b[d], lam[d], hc[:, -1])
        y_ctx = y_ctx + flip_seq(hc, d, 1)
        y_lat = y_lat + flip_seq(hl, d, 1)
    return y_lat, y_ctx


def gated_delta_chunk(q, k, v, g, beta, s0):
    b, h, l, dk = k.shape
    dv = v.shape[-1]
    n = l // DN_CHUNK
    rs = lambda t: t.reshape(b, h, n, DN_CHUNK, *t.shape[3:])
    q, k, v, g, beta = rs(q), rs(k), rs(v), rs(g), rs(beta)
    gc = jnp.cumsum(g, axis=-1)
    incl = jnp.tril(jnp.ones((DN_CHUNK, DN_CHUNK), bool))
    strict = jnp.tril(jnp.ones((DN_CHUNK, DN_CHUNK), bool), -1)
    decay = jnp.exp(jnp.where(incl, gc[..., :, None] - gc[..., None, :], -jnp.inf))
    kb = k * beta[..., None]
    lower = jnp.where(strict, jnp.einsum('bhnid,bhnjd->bhnij', kb, k) * decay, 0.0)
    eye = jnp.eye(DN_CHUNK, dtype=k.dtype)
    rhs = jnp.concatenate([v * beta[..., None], kb * jnp.exp(gc)[..., None]], axis=-1)
    sol = lax.linalg.triangular_solve(eye + lower, rhs, left_side=True, lower=True, unit_diagonal=True)
    u, w = sol[..., :dv], sol[..., dv:]
    a_qk = jnp.einsum('bhnid,bhnjd->bhnij', q, k) * decay
    q_dec = q * jnp.exp(gc)[..., None]
    k_tail = k * jnp.exp(gc[..., -1:] - gc)[..., None]
    g_tot = jnp.exp(gc[..., -1])

    def step(s, xs):
        u_n, w_n, q_n, a_n, k_n, gt_n = xs
        v_new = u_n - jnp.einsum('bhck,bhkv->bhcv', w_n, s)
        o_n = jnp.einsum('bhck,bhkv->bhcv', q_n, s) + jnp.einsum('bhcj,bhjv->bhcv', a_n, v_new)
        s = s * gt_n[..., None, None] + jnp.einsum('bhck,bhcv->bhkv', k_n, v_new)
        return s, o_n

    xs = tuple(jnp.moveaxis(t, 2, 0) for t in (u, w, q_dec, a_qk, k_tail, g_tot))
    s_fin, o = lax.scan(step, s0, xs)
    return jnp.moveaxis(o, 0, 2).reshape(b, h, l, dv), s_fin


def gla_chunk(q, k, v, logd, s0):
    b, h, l, dk = k.shape
    dv = v.shape[-1]
    n = l // GLA_CHUNK
    rs = lambda t: t.reshape(b, h, n, GLA_CHUNK, t.shape[-1])
    q, k, v, logd = rs(q), rs(k), rs(v), rs(logd)
    bc = jnp.cumsum(logd, axis=-2)
    q_dec = q * jnp.exp(bc)
    incl = jnp.tril(jnp.ones((GLA_CHUNK, GLA_CHUNK), bool))
    a = jnp.where(incl, jnp.einsum('bhnid,bhnjd->bhnij', q_dec, k * jnp.exp(-bc)), 0.0)
    o_intra = jnp.einsum('bhnij,bhnjv->bhniv', a, v)
    k_tail = k * jnp.exp(bc[..., -1:, :] - bc)
    d_tot = jnp.exp(bc[..., -1, :])

    def step(s, xs):
        q_n, k_n, v_n, d_n = xs
        o_n = jnp.einsum('bhck,bhkv->bhcv', q_n, s)
        s = s * d_n[..., None] + jnp.einsum('bhck,bhcv->bhkv', k_n, v_n)
        return s, o_n

    xs = tuple(jnp.moveaxis(t, 2, 0) for t in (q_dec, k_tail, v, d_tot))
    s_fin, o_inter = lax.scan(step, s0, xs)
    o = o_intra + jnp.moveaxis(o_inter, 0, 2)
    return o.reshape(b, h, l, dv), s_fin


def gla_final_state(k, v, logd):
    bc = jnp.cumsum(logd, axis=2)
    return jnp.einsum('bhlk,bhlv->bhkv', k * jnp.exp(bc[:, :, -1:] - bc), v)


def bidir_scan(chunk_fn, state_fn, lat_in, ctx_in, s0, ctx_out):
    y_lat, y_ctx = 0.0, 0.0
    for d in range(2):
        c_args = [flip_seq(t, d, 2) for t in ctx_in[d]]
        if ctx_out:
            o_c, s_c = chunk_fn(*c_args, s0)
            y_ctx = y_ctx + flip_seq(o_c, d, 2)
        else:
            s_c = state_fn(*c_args)
        o_l, _ = chunk_fn(*[flip_seq(t, d, 2) for t in lat_in[d]], s_c)
        y_lat = y_lat + flip_seq(o_l, d, 2)
    return y_lat, (y_ctx if ctx_out else None)


def even_mixer(xl, xc, w_in, w_out, a_conv_w, a_conv_b, a_gate_w, a_gate_b, a_lambda,
               b_conv_w, b_a_log, b_dt_bias, b_norm_g, ctx_out):
    f32 = jnp.float32

    def prep(xs):
        bsz, l, _ = xs.shape
        ax, ay, qkv, z, bb, ba = split_cols(xs @ w_in, E_SIZES)
        ua = dwconv_centred(ax, a_conv_w, a_conv_b).astype(f32)
        qkv = jax.nn.silu(dwconv_centred(qkv, b_conv_w).astype(f32))
        q, k, v = split_cols(qkv, (DN_HEADS * DN_DK, DN_HEADS * DN_DK, DN_HEADS * DN_DV))
        q = l2norm(to_heads(q, DN_HEADS)) * DN_DK ** -0.5
        k = l2norm(to_heads(k, DN_HEADS))
        v = to_heads(v, DN_HEADS)
        beta = jax.nn.sigmoid(bb.astype(f32)).reshape(bsz, l, 2, DN_HEADS).transpose(2, 0, 3, 1)
        g = -jnp.exp(b_a_log.astype(f32)) * jax.nn.softplus(ba.astype(f32).reshape(bsz, l, 2, DN_HEADS) + b_dt_bias)
        g = g.transpose(2, 0, 3, 1)
        return ua, ay, [(q, k, v, g[d], beta[d]) for d in range(2)], z

    ua_l, ay_l, dn_l, z_l = prep(xl)
    ua_c, ay_c, dn_c, z_c = prep(xc)
    ha_l, ha_c = rglru_bidir(ua_l, ua_c, a_gate_w, a_gate_b, a_lambda)
    s0 = jnp.zeros((xc.shape[0], DN_HEADS, DN_DK, DN_DV), f32)
    ob_l, ob_c = bidir_scan(gated_delta_chunk, lambda *t: gated_delta_chunk(*t, s0)[1], dn_l, dn_c, s0, ctx_out)

    def merge(ha, ay, ob, z, dtype):
        ya = ha * jax.nn.gelu(ay.astype(f32))
        yb = from_heads(gated_rmsnorm(ob, to_heads(z.astype(f32), DN_HEADS), b_norm_g))
        return jnp.concatenate([ya, yb], axis=-1).astype(dtype) @ w_out

    y_l = merge(ha_l, ay_l, ob_l, z_l, xl.dtype)
    y_c = merge(ha_c, ay_c, ob_c, z_c, xc.dtype) if ctx_out else None
    return y_l, y_c


def odd_mixer(xl, xc, w_in, w_out, lb, c_norm_g, d_gate_w2, d_gate_b2, d_norm_g, ctx_out):
    f32 = jnp.float32
    lb = lb.astype(f32)

    def prep(xs):
        bsz, l, _ = xs.shape
        cq, cff, cfb, ci, cg, dq, dk, dv, dg, dlr = split_cols((xs @ w_in).astype(f32), O_SIZES)
        q_c = to_heads(jax.nn.silu(cq), HG_HEADS) * HG_DF ** -0.5
        v_c = to_heads(ci, HG_HEADS)
        hg_in = []
        for fl in (cff, cfb):
            log_f = jnp.log(lb + (1.0 - lb) * jax.nn.sigmoid(fl))
            k_c = (1.0 - lb) * jax.nn.sigmoid(-fl)
            hg_in.append((q_c, to_heads(k_c, HG_HEADS), v_c, to_heads(log_f, HG_HEADS)))
        q_d = to_heads(dq, GLA_HEADS) * GLA_DK ** -0.5
        k_d = to_heads(dk, GLA_HEADS)
        v_d = to_heads(dv, GLA_HEADS)
        lr = dlr.reshape(bsz, l, 2, GLA_RANK)
        logd = jax.nn.log_sigmoid(jnp.einsum('blgr,grk->gblk', lr, d_gate_w2) + d_gate_b2[:, None, None, :]) / GLA_GATE_NORM
        gla_in = [(q_d, k_d, v_d, to_heads(logd[d], GLA_HEADS)) for d in range(2)]
        return hg_in, gla_in, cg, dg

    hg_l, gl_l, cg_l, dg_l = prep(xl)
    hg_c, gl_c, cg_c, dg_c = prep(xc)
    state_fn = lambda q, k, v, logd: gla_final_state(k, v, logd)
    s0_c = jnp.zeros((xc.shape[0], HG_HEADS, HG_DF, HG_DI), f32)
    s0_d = jnp.zeros((xc.shape[0], GLA_HEADS, GLA_DK, GLA_DV), f32)
    yc_l, yc_c = bidir_scan(gla_chunk, state_fn, hg_l, hg_c, s0_c, ctx_out)
    yd_l, yd_c = bidir_scan(gla_chunk, state_fn, gl_l, gl_c, s0_d, ctx_out)

    def merge(yc, yd, cg, dg, dtype):
        oc = from_heads(gated_rmsnorm(yc, to_heads(cg, HG_HEADS), c_norm_g))
        od = from_heads(gated_rmsnorm(yd, to_heads(dg, GLA_HEADS), d_norm_g))
        return jnp.concatenate([oc, od], axis=-1).astype(dtype) @ w_out

    y_l = merge(yc_l, yd_l, cg_l, dg_l, xl.dtype)
    y_c = merge(yc_c, yd_c, cg_c, dg_c, xc.dtype) if ctx_out else None
    return y_l, y_c


def swiglu(xs, w_gate, w_up, w_down):
    return (jax.nn.silu(xs @ w_gate) * (xs @ w_up)) @ w_down


def moe_swiglu(xs, router_w, router_b, w_gate, w_up, w_down):
    logits = (xs @ router_w).astype(jnp.float32) + router_b.astype(jnp.float32)
    top_val, top_idx = lax.top_k(logits, TOP_K)
    top_p = jax.nn.softmax(top_val, axis=-1)
    gates = jnp.sum(jax.nn.one_hot(top_idx, N_EXPERTS, dtype=jnp.float32) * top_p[..., None], axis=-2).astype(xs.dtype)
    y = jnp.zeros_like(xs)
    for e in range(N_EXPERTS):
        y = y + gates[..., e:e + 1] * swiglu(xs, w_gate[e], w_up[e], w_down[e])
    return y


def setup_inputs(seed: int = 0) -> dict:
    key = jax.random.key(seed)
    keys = list(jax.random.split(key, 48))

    def nrm(shape, std=1.0):
        return std * jax.random.normal(keys.pop(), shape, jnp.float32)

    def unif(shape, lo, hi):
        return jax.random.uniform(keys.pop(), shape, jnp.float32, lo, hi)

    d = D_MODEL
    a_pow = unif((N_EVEN, 2, RG_WIDTH), 0.9, 0.999) ** (1.0 / RG_C)
    dt = jnp.exp(unif((N_EVEN, 2, DN_HEADS), float(np.log(1e-3)), float(np.log(1e-1))))
    a_log = jnp.log(unif((N_EVEN, 2, DN_HEADS), 1.0, 16.0))
    return {
        'x': nrm((BATCH, SEQ, d)),
        'c': nrm((BATCH, d)),
        'ctx': nrm((BATCH, CTX_LEN, d)),
        'c_ctx': nrm((d,)),
        'ada_w': nrm((DEPTH, d, 6 * d), 0.5 * d ** -0.5),
        'ada_b': nrm((DEPTH, 6 * d), 0.02),
        'norm_mix_g': 1.0 + nrm((DEPTH, d), 0.05),
        'norm_ffn_g': 1.0 + nrm((DEPTH, d), 0.05),
        'final_norm_g': 1.0 + nrm((d,), 0.05),
        'e_w_in': nrm((N_EVEN, d, E_IN), d ** -0.5),
        'e_w_out': nrm((N_EVEN, MIX_WIDTH, d), MIX_WIDTH ** -0.5),
        'e_a_conv_w': nrm((N_EVEN, CONV_K, RG_WIDTH), CONV_K ** -0.5),
        'e_a_conv_b': nrm((N_EVEN, RG_WIDTH), 0.02),
        'e_a_gate_w': nrm((N_EVEN, 2, 2, RG_BLOCKS, RG_BLOCK, RG_BLOCK), RG_BLOCK ** -0.5),
        'e_a_gate_b': nrm((N_EVEN, 2, 2, RG_WIDTH), 0.02),
        'e_a_lambda': jnp.log(a_pow) - jnp.log1p(-a_pow),
        'e_b_conv_w': nrm((N_EVEN, CONV_K, DN_HEADS * (2 * DN_DK + DN_DV)), CONV_K ** -0.5),
        'e_b_a_log': a_log,
        'e_b_dt_bias': dt + jnp.log(-jnp.expm1(-dt)),
        'e_b_norm_g': 1.0 + nrm((N_EVEN, DN_DV), 0.05),
        'e_ffn_w_gate': nrm((N_EVEN, d, D_FF), d ** -0.5),
        'e_ffn_w_up': nrm((N_EVEN, d, D_FF), d ** -0.5),
        'e_ffn_w_down': nrm((N_EVEN, D_FF, d), D_FF ** -0.5),
        'o_w_in': nrm((N_ODD, d, O_IN), d ** -0.5),
        'o_w_out': nrm((N_ODD, MIX_WIDTH, d), MIX_WIDTH ** -0.5),
        'o_lb_logits': nrm((DEPTH, HG_HEADS * HG_DF), 0.1),
        'o_c_norm_g': 1.0 + nrm((N_ODD, HG_DI), 0.05),
        'o_d_gate_w2': nrm((N_ODD, 2, GLA_RANK, GLA_HEADS * GLA_DK), GLA_RANK ** -0.5),
        'o_d_gate_b2': nrm((N_ODD, 2, GLA_HEADS * GLA_DK), 0.02),
        'o_d_norm_g': 1.0 + nrm((N_ODD, GLA_DV), 0.05),
        'o_router_w': nrm((N_ODD, d, N_EXPERTS), d ** -0.5),
        'o_router_b': nrm((N_ODD, N_EXPERTS), 0.01),
        'o_moe_w_gate': nrm((N_ODD, N_EXPERTS, d, D_EXPERT), d ** -0.5),
        'o_moe_w_up': nrm((N_ODD, N_EXPERTS, d, D_EXPERT), d ** -0.5),
        'o_moe_w_down': nrm((N_ODD, N_EXPERTS, D_EXPERT, d), D_EXPERT ** -0.5),
    }


def reference(x, c, ctx, c_ctx, ada_w, ada_b, norm_mix_g, norm_ffn_g, final_norm_g,
              e_w_in, e_w_out, e_a_conv_w, e_a_conv_b, e_a_gate_w, e_a_gate_b, e_a_lambda,
              e_b_conv_w, e_b_a_log, e_b_dt_bias, e_b_norm_g, e_ffn_w_gate, e_ffn_w_up, e_ffn_w_down,
              o_w_in, o_w_out, o_lb_logits, o_c_norm_g, o_d_gate_w2, o_d_gate_b2, o_d_norm_g,
              o_router_w, o_router_b, o_moe_w_gate, o_moe_w_up, o_moe_w_down):
    bsz, seq, dm = x.shape
    rows = seq // GRID_W
    lb_w = jax.nn.softmax(o_lb_logits.astype(jnp.float32), axis=0)
    lower_bounds = jnp.cumsum(lb_w, axis=0) - lb_w[0]
    s_lat = jax.nn.silu(c)
    s_ctx = jax.nn.silu(c_ctx)
    h_lat, h_ctx = x, ctx
    for l in range(DEPTH):
        j = l // 2
        ctx_out = l < DEPTH - 1
        ml = jnp.split((s_lat @ ada_w[l] + ada_b[l])[:, None, :], 6, axis=-1)
        mc = jnp.split(s_ctx @ ada_w[l] + ada_b[l], 6, axis=-1)
        xl = modulate(h_lat, norm_mix_g[l], ml[0], ml[1])
        xc = modulate(h_ctx, norm_mix_g[l], mc[0], mc[1])
        if l % 2 == 0:
            yl, yc = even_mixer(xl, xc, e_w_in[j], e_w_out[j], e_a_conv_w[j], e_a_conv_b[j], e_a_gate_w[j],
                                e_a_gate_b[j], e_a_lambda[j], e_b_conv_w[j], e_b_a_log[j], e_b_dt_bias[j],
                                e_b_norm_g[j], ctx_out)
        else:
            xl_cm = xl.reshape(bsz, rows, GRID_W, dm).swapaxes(1, 2).reshape(bsz, seq, dm)
            yl_cm, yc = odd_mixer(xl_cm, xc, o_w_in[j], o_w_out[j], lower_bounds[l], o_c_norm_g[j],
                                  o_d_gate_w2[j], o_d_gate_b2[j], o_d_norm_g[j], ctx_out)
            yl = yl_cm.reshape(bsz, GRID_W, rows, dm).swapaxes(1, 2).reshape(bsz, seq, dm)
        h_lat = h_lat + ml[2] * yl

        def channel_mixer(xs):
            if l % 2 == 0:
                return swiglu(xs, e_ffn_w_gate[j], e_ffn_w_up[j], e_ffn_w_down[j])
            return moe_swiglu(xs, o_router_w[j], o_router_b[j], o_moe_w_gate[j], o_moe_w_up[j], o_moe_w_down[j])

        h_lat = h_lat + ml[5] * channel_mixer(modulate(h_lat, norm_ffn_g[l], ml[3], ml[4]))
        if ctx_out:
            h_ctx = h_ctx + mc[2] * yc
            h_ctx = h_ctx + mc[5] * channel_mixer(modulate(h_ctx, norm_ffn_g[l], mc[3], mc[4]))
    return rmsnorm(h_lat, final_norm_g)
```

```python
import functools

import jax
import jax.numpy as jnp
from jax import lax
from jax.experimental import pallas as pl
from jax.experimental.pallas import tpu as pltpu

F32 = jnp.float32
BF16 = jnp.bfloat16
HI = lax.Precision.HIGHEST

EPS = 1e-6
D_MODEL = 1024
GRID_W = 64
CONV_K = 4
RG_WIDTH = 512
RG_BLOCK = 64
RG_C = 8.0
DN_HEADS = 4
DN_D = 128
DN_CHUNK = 64
HG_HEADS = 4
HG_D = 128
GLA_HEADS = 4
GLA_DK = 64
GLA_DV = 128
GLA_RANK = 16
GLA_GATE_NORM = 16.0
D_FF = 2816
N_EXPERTS = 8

E_IN = 3088
E_IN_PAD = 3200
O_IN = 4128
O_IN_PAD = 4224
SEG = 512
N_SEG = 11
S_HQ, S_HV, S_HK0, S_HLF0, S_HK1, S_HLF1, S_GV, S_GQK, S_CG, S_DG, S_GLD = range(N_SEG)

V7X_VMEM_BYTES = 64 * 1024 * 1024
VMEM_HEADROOM_BYTES = 8 * 1024 * 1024
MOE_ROW_BLOCK = 128
MOE_FF_SPLIT = 2


def _vmem(nbytes):
    return int(min(V7X_VMEM_BYTES - VMEM_HEADROOM_BYTES, nbytes))


def _params(sem, vmem_bytes):
    return pltpu.CompilerParams(dimension_semantics=sem, vmem_limit_bytes=_vmem(vmem_bytes))


def _sigmoid(x):
    return jax.nn.sigmoid(x)


def _silu(x):
    return x * jax.nn.sigmoid(x)


def _softplus(x):
    return jnp.maximum(x, 0.0) + jnp.log1p(jnp.exp(-jnp.abs(x)))


def _gelu_tanh(x):
    return 0.5 * x * (1.0 + jnp.tanh(0.7978845608028654 * (x + 0.044715 * (x * x * x))))


def _normmod(x, g, shift, scale):
    y = x * lax.rsqrt(jnp.mean(x * x, axis=-1, keepdims=True) + EPS)
    return (y * g) * (1.0 + scale) + shift


def _dot(a, b):
    return jnp.dot(a, b, preferred_element_type=F32)


def _dot_nt(a, b):
    return lax.dot_general(a, b, (((1,), (1,)), ((), ())), preferred_element_type=F32)


def _dot_tn(a, b):
    return lax.dot_general(a, b, (((0,), (0,)), ((), ())), preferred_element_type=F32)


def _dot_hi(a, b):
    return jnp.dot(a, b, precision=HI, preferred_element_type=F32)


def _const_spec(shape):
    nd = len(shape)
    return pl.BlockSpec(shape, lambda *_: (0,) * nd, pipeline_mode=pl.Buffered(1))


def _ada_kernel(cv_ref, w_ref, b_ref, o_ref):
    s = _silu(cv_ref[...]).astype(BF16)
    o_ref[...] = _dot(s, w_ref[...].astype(BF16)) + b_ref[...]


def _ada(c, c_ctx, ada_w, ada_b):
    depth, d, n6 = ada_w.shape
    bsz = c.shape[0]
    rows = 16
    cv = jnp.zeros((rows, d), F32).at[:bsz].set(c).at[bsz].set(c_ctx)
    tn = 1536
    return pl.pallas_call(
        _ada_kernel,
        grid=(depth, n6 // tn),
        in_specs=[pl.BlockSpec((rows, d), lambda l, j: (0, 0)),
                  pl.BlockSpec((None, d, tn), lambda l, j: (l, 0, j)),
                  pl.BlockSpec((None, 1, tn), lambda l, j: (l, 0, j))],
        out_specs=pl.BlockSpec((None, rows, tn), lambda l, j: (l, 0, j)),
        out_shape=jax.ShapeDtypeStruct((depth, rows, n6), F32),
        compiler_params=_params(("arbitrary", "arbitrary"), 32 << 20),
        name="ada_mod",
    )(cv, ada_w, ada_b.reshape(depth, 1, n6))


def _modtab(mods_l, bsz):
    m = mods_l.reshape(mods_l.shape[0], 6, D_MODEL)
    lat = m[:bsz]
    ctx = jnp.broadcast_to(m[bsz][None], (bsz, 6, D_MODEL))
    return jnp.stack([ctx, lat], axis=1)


def _inproj0_kernel(hp_ref, hm_ref, hn_ref, mod_ref, g_ref, w_ref, acw_ref, acb_ref, bcw_ref, gpar_ref,
                    ua_ref, gay_ref, q_ref, k_ref, v_ref, sz_ref, gb_ref, u_scr, *, tm, ctx_tiles, n_tiles):
    t = pl.program_id(1)
    x = jnp.concatenate([hp_ref[...], hm_ref[...], hn_ref[...]], axis=0)
    xm = _normmod(x, g_ref[...], mod_ref[0:1, :], mod_ref[1:2, :]).astype(BF16)
    u_scr[...] = _dot(xm, w_ref[...])

    @pl.when(jnp.logical_or(t == 0, t == ctx_tiles))
    def _():
        u_scr[0:8, :] = jnp.zeros((8, E_IN_PAD), F32)

    @pl.when(jnp.logical_or(t == ctx_tiles - 1, t == n_tiles - 1))
    def _():
        u_scr[tm + 8:tm + 16, :] = jnp.zeros((8, E_IN_PAD), F32)

    def conv(c0, width, w_ref_, w0):
        acc = u_scr[6:6 + tm, c0:c0 + width] * w_ref_[0:1, w0:w0 + width]
        for j in range(1, CONV_K):
            acc = acc + u_scr[6 + j:6 + j + tm, c0:c0 + width] * w_ref_[j:j + 1, w0:w0 + width]
        return acc

    for grp in range(RG_WIDTH // 128):
        c0 = grp * 128
        ua_ref[:, c0:c0 + 128] = conv(c0, 128, acw_ref, c0) + acb_ref[0:1, c0:c0 + 128]
    gay_ref[...] = _gelu_tanh(u_scr[8:8 + tm, 512:1024])

    for grp in range(3 * DN_HEADS):
        c0 = grp * 128
        y = _silu(conv(1024 + c0, 128, bcw_ref, c0))
        if grp < 2 * DN_HEADS:
            y = y * lax.rsqrt(jnp.sum(y * y, axis=-1, keepdims=True) + EPS)
        if grp < DN_HEADS:
            q_ref[:, c0:c0 + 128] = y * (DN_D ** -0.5)
        elif grp < 2 * DN_HEADS:
            k_ref[:, c0 - 512:c0 - 384] = y
        else:
            v_ref[:, c0 - 1024:c0 - 896] = y
    sz_ref[...] = _silu(u_scr[8:8 + tm, 2560:3072])

    xg = u_scr[8:8 + tm, 3072:3200]
    lane = lax.broadcasted_iota(jnp.int32, xg.shape, 1)
    g = -jnp.exp(gpar_ref[0:1, :]) * _softplus(xg + gpar_ref[1:2, :])
    gb_ref[...] = jnp.where(lane < 2 * DN_HEADS, _sigmoid(xg), g)


def _inproj0(h, modtab, g, w_pad, acw, acb, bcw, gpar, *, tm, ctx_len):
    bsz, t_all, d = h.shape
    n_tiles = t_all // tm
    ctx_tiles = ctx_len // tm
    tb = tm // 8
    kern = functools.partial(_inproj0_kernel, tm=tm, ctx_tiles=ctx_tiles, n_tiles=n_tiles)
    tok = lambda w: jax.ShapeDtypeStruct((bsz, t_all, w), F32)
    tok_spec = lambda w: pl.BlockSpec((None, tm, w), lambda b, t: (b, t, 0))
    return pl.pallas_call(
        kern,
        grid=(bsz, n_tiles),
        in_specs=[
            pl.BlockSpec((None, 8, d), lambda b, t: (b, jnp.maximum(t * tb - 1, 0), 0)),
            pl.BlockSpec((None, tm, d), lambda b, t: (b, t, 0)),
            pl.BlockSpec((None, 8, d), lambda b, t: (b, jnp.minimum((t + 1) * tb, t_all // 8 - 1), 0)),
            pl.BlockSpec((None, None, 6, d), lambda b, t: (b, jnp.where(t >= ctx_tiles, 1, 0), 0, 0)),
            _const_spec((1, d)),
            _const_spec((d, E_IN_PAD)),
            _const_spec((CONV_K, RG_WIDTH)),
            _const_spec((1, RG_WIDTH)),
            _const_spec((CONV_K, 3 * DN_HEADS * DN_D)),
            _const_spec((2, 128)),
        ],
        out_specs=[pl.BlockSpec((tm, RG_WIDTH), lambda b, t: (t, b)),
                   tok_spec(512), tok_spec(512), tok_spec(512), tok_spec(512), tok_spec(512), tok_spec(128)],
        out_shape=[jax.ShapeDtypeStruct((t_all, bsz * RG_WIDTH), F32),
                   tok(512), tok(512), tok(512), tok(512), tok(512), tok(128)],
        scratch_shapes=[pltpu.VMEM((tm + 16, E_IN_PAD), F32)],
        compiler_params=_params(("arbitrary", "arbitrary"), 40 << 20),
        name="l0_inproj",
    )(h, h, h, modtab, g, w_pad, acw, acb, bcw, gpar)


def _rglru_kernel(uf_ref, ub_ref, wg_ref, gbias_ref, lam_ref, hf_ref, hb_ref,
                  af_scr, xf_scr, ab_scr, xb_scr, h_scr, *, tt, bsz):
    s = pl.program_id(0)

    @pl.when(s == 0)
    def _():
        h_scr[...] = jnp.zeros_like(h_scr)

    def gates(u_ref, d, a_scr, x_scr):
        x = u_ref[...].reshape(tt * bsz, RG_WIDTH)
        xb = x.astype(BF16)
        for half in range(2):
            c0 = half * 256
            xh = xb[:, c0:c0 + 256]
            r = _sigmoid(_dot(xh, wg_ref[d, 0, half]) + gbias_ref[2 * d:2 * d + 1, c0:c0 + 256])
            i = _sigmoid(_dot(xh, wg_ref[d, 1, half]) + gbias_ref[2 * d + 1:2 * d + 2, c0:c0 + 256])
            log_a = (-RG_C) * r * _softplus(-lam_ref[d:d + 1, c0:c0 + 256])
            a = jnp.exp(log_a)
            mult = jnp.sqrt(-jnp.tanh(log_a) * (a * a + 1.0))
            xin = mult * (i * x[:, c0:c0 + 256])
            a_scr[:, :, c0:c0 + 256] = a.reshape(tt, bsz, 256)
            x_scr[:, :, c0:c0 + 256] = xin.reshape(tt, bsz, 256)

    gates(uf_ref, 0, af_scr, xf_scr)
    gates(ub_ref, 1, ab_scr, xb_scr)

    def step(t, carry):
        hf, hb = carry
        hf = af_scr[t] * hf + xf_scr[t]
        hf_ref[t] = hf
        tb = tt - 1 - t
        hb = ab_scr[tb] * hb + xb_scr[tb]
        hb_ref[tb] = hb
        return hf, hb

    hf, hb = lax.fori_loop(0, tt, step, (h_scr[0], h_scr[1]), unroll=8)
    h_scr[0] = hf
    h_scr[1] = hb


def _rglru(ua3, wg, gbias, lam, *, tt, ctx_len):
    t_all, bsz, w = ua3.shape
    n_steps = t_all // tt
    nc = ctx_len // tt

    def bwd(s):
        return jnp.where(s < nc, nc - 1 - s, n_steps + nc - 1 - s)

    blk = (tt, bsz, w)
    kern = functools.partial(_rglru_kernel, tt=tt, bsz=bsz)
    return pl.pallas_call(
        kern,
        grid=(n_steps,),
        in_specs=[pl.BlockSpec(blk, lambda s: (s, 0, 0)),
                  pl.BlockSpec(blk, lambda s: (bwd(s), 0, 0)),
                  _const_spec(wg.shape), _const_spec(gbias.shape), _const_spec(lam.shape)],
        out_specs=[pl.BlockSpec(blk, lambda s: (s, 0, 0)),
                   pl.BlockSpec(blk, lambda s: (bwd(s), 0, 0))],
        out_shape=[jax.ShapeDtypeStruct(ua3.shape, F32)] * 2,
        scratch_shapes=[pltpu.VMEM(blk, F32)] * 4 + [pltpu.VMEM((2, bsz, w), F32)],
        compiler_params=_params(("arbitrary",), 40 << 20),
        name="l0_rglru",
    )(ua3, ua3, wg, gbias, lam)


def _delta_kernel(q_ref, k_ref, v_ref, gb_ref, o_ref, s_scr):
    d = pl.program_id(0)
    s = pl.program_id(2)
    c = DN_CHUNK

    @pl.when(s == 0)
    def _():
        s_scr[...] = jnp.zeros_like(s_scr)

    sgn = 1 - 2 * d
    row = lax.broadcasted_iota(jnp.int32, (c, c), 0)
    col = lax.broadcasted_iota(jnp.int32, (c, c), 1)
    dlt = (row - col) * sgn
    incl = dlt >= 0
    strict = dlt > 0
    m_incl = jnp.where(incl, 1.0, 0.0)
    m_incl_t = jnp.where(dlt <= 0, 1.0, 0.0)
    ones = jnp.ones((c, c), F32)
    m_stack = jnp.concatenate([m_incl, ones], axis=0)
    eye = jnp.where(row == col, 1.0, 0.0)

    gb = gb_ref[...]
    for h in range(DN_HEADS):
        lo = h * DN_D
        q = q_ref[:, lo:lo + DN_D]
        k = k_ref[:, lo:lo + DN_D]
        v = v_ref[:, lo:lo + DN_D]
        beta = jnp.where(d == 0, gb[:, h:h + 1], gb[:, DN_HEADS + h:DN_HEADS + h + 1])
        g = jnp.where(d == 0, gb[:, 2 * DN_HEADS + h:2 * DN_HEADS + h + 1],
                      gb[:, 3 * DN_HEADS + h:3 * DN_HEADS + h + 1])
        g_wide = jnp.broadcast_to(g, (c, DN_D))
        cs = _dot_hi(m_stack, g_wide)
        gc = cs[0:c]
        gtot = cs[c:c + 1]
        gc_row = _dot_hi(ones, g_wide[:, 0:c] * m_incl_t)
        decay = jnp.where(incl, jnp.exp(jnp.minimum(gc[:, 0:c] - gc_row, 0.0)), 0.0)
        e_gc = jnp.exp(gc)
        kb = k * beta
        kbf = k.astype(BF16)
        lmat = jnp.where(strict, _dot_nt(kb.astype(BF16), kbf) * decay, 0.0)
        neg = -lmat
        t_inv = eye + neg
        p = _dot_hi(neg, neg)
        n_sq = max(1, (c - 1).bit_length() - 1)
        for it in range(n_sq):
            tp = _dot_hi(jnp.concatenate([t_inv, p], axis=0), p)
            t_inv = t_inv + tp[0:c]
            if it + 1 < n_sq:
                p = tp[c:2 * c]
        rhs = jnp.concatenate([v * beta, kb * e_gc], axis=1).astype(BF16)
        sol = _dot(t_inv.astype(BF16), rhs)
        u = sol[:, 0:DN_D]
        w = sol[:, DN_D:2 * DN_D]
        a_qk = _dot_nt(q.astype(BF16), kbf) * decay
        q_dec = q * e_gc
        k_tail = k * jnp.exp(gtot - gc)
        st = s_scr[h]
        stb = st.astype(BF16)
        v_new = u - _dot(w.astype(BF16), stb)
        vnb = v_new.astype(BF16)
        o = _dot(q_dec.astype(BF16), stb) + _dot(a_qk.astype(BF16), vnb)
        o_ref[:, lo:lo + DN_D] = o
        s_scr[h] = st * jnp.exp(gtot) + _dot_tn(k_tail.astype(BF16), vnb)


def _delta(q, k, v, gb, *, ctx_len):
    bsz, t_all, w = q.shape
    c = DN_CHUNK
    n_steps = t_all // c
    nc = ctx_len // c

    def chunk(d, s):
        return jnp.where(d == 0, s, jnp.where(s < nc, nc - 1 - s, n_steps + nc - 1 - s))

    tok = lambda width: pl.BlockSpec((None, c, width), lambda d, b, s: (b, chunk(d, s), 0))
    return pl.pallas_call(
        _delta_kernel,
        grid=(2, bsz, n_steps),
        in_specs=[tok(w), tok(w), tok(w), tok(128)],
        out_specs=pl.BlockSpec((None, None, c, w), lambda d, b, s: (d, b, chunk(d, s), 0)),
        out_shape=jax.ShapeDtypeStruct((2, bsz, t_all, w), F32),
        scratch_shapes=[pltpu.VMEM((DN_HEADS, DN_D, DN_D), F32)],
        compiler_params=_params(("arbitrary", "arbitrary", "arbitrary"), 32 << 20),
        name="l0_deltanet",
    )(q, k, v, gb)


def _head_norm(y, g):
    return y * lax.rsqrt(jnp.mean(y * y, axis=-1, keepdims=True) + EPS) * g


def _outproj0_kernel(hf_ref, hb_ref, gay_ref, o0_ref, o1_ref, sz_ref, h_ref, mod_ref, ng_ref, w_ref, out_ref):
    parts = [((hf_ref[...] + hb_ref[...]) * gay_ref[...]).astype(BF16)]
    for hd in range(DN_HEADS):
        lo = hd * DN_D
        ob = o0_ref[:, lo:lo + DN_D] + o1_ref[:, lo:lo + DN_D]
        parts.append((_head_norm(ob, ng_ref[...]) * sz_ref[:, lo:lo + DN_D]).astype(BF16))
    y = _dot(jnp.concatenate(parts, axis=-1), w_ref[...])
    out_ref[...] = h_ref[...] + mod_ref[2:3, :] * y


def _outproj0(hf2, hb2, gay, o, sz, h, modtab, ng, w, *, tm, ctx_len):
    bsz, t_all, d = h.shape
    ctx_tiles = ctx_len // tm
    tmaj = pl.BlockSpec((tm, RG_WIDTH), lambda b, t: (t, b))
    tok = lambda width: pl.BlockSpec((None, tm, width), lambda b, t: (b, t, 0))
    return pl.pallas_call(
        _outproj0_kernel,
        grid=(bsz, t_all // tm),
        in_specs=[tmaj, tmaj, tok(512),
                  pl.BlockSpec((None, None, tm, 512), lambda b, t: (0, b, t, 0)),
                  pl.BlockSpec((None, None, tm, 512), lambda b, t: (1, b, t, 0)),
                  tok(512), tok(d),
                  pl.BlockSpec((None, None, 6, d), lambda b, t: (b, jnp.where(t >= ctx_tiles, 1, 0), 0, 0)),
                  _const_spec((1, DN_D)), _const_spec((d, d))],
        out_specs=tok(d),
        out_shape=jax.ShapeDtypeStruct(h.shape, F32),
        compiler_params=_params(("arbitrary", "arbitrary"), 32 << 20),
        name="l0_outproj",
    )(hf2, hb2, gay, o, o, sz, h, modtab, ng, w)


def _ffn_kernel(h_ref, mod_ref, g_ref, wg_ref, wu_ref, wd_ref, out_ref, *, n_chunks):
    x = h_ref[...]
    xm = _normmod(x, g_ref[...], mod_ref[3:4, :], mod_ref[4:5, :]).astype(BF16)
    cw = D_FF // n_chunks
    acc = jnp.zeros(x.shape, F32)
    for ci in range(n_chunks):
        c0 = ci * cw
        gate = _dot(xm, wg_ref[:, c0:c0 + cw])
        up = _dot(xm, wu_ref[:, c0:c0 + cw])
        acc = acc + _dot((_silu(gate) * up).astype(BF16), wd_ref[c0:c0 + cw, :])
    out_ref[...] = x + mod_ref[5:6, :] * acc


def _ffn(h, modtab, g, wg, wu, wd, *, tm, ctx_len):
    bsz, t_all, d = h.shape
    ctx_tiles = ctx_len // tm
    tok = pl.BlockSpec((None, tm, d), lambda b, t: (b, t, 0))
    return pl.pallas_call(
        functools.partial(_ffn_kernel, n_chunks=2),
        grid=(bsz, t_all // tm),
        in_specs=[tok,
                  pl.BlockSpec((None, None, 6, d), lambda b, t: (b, jnp.where(t >= ctx_tiles, 1, 0), 0, 0)),
                  _const_spec((1, d)), _const_spec((d, D_FF)), _const_spec((d, D_FF)), _const_spec((D_FF, d))],
        out_specs=tok,
        out_shape=jax.ShapeDtypeStruct(h.shape, F32),
        compiler_params=_params(("arbitrary", "arbitrary"), 44 << 20),
        name="l0_ffn",
    )(h, modtab, g, wg, wu, wd)


def _inproj1_kernel(h_ref, mod_ref, g_ref, w_ref, lbl_ref, wlr_ref, b2_ref, p_ref, u_scr, *, layer):
    x = h_ref[...]
    xm = _normmod(x, g_ref[...], mod_ref[0:1, :], mod_ref[1:2, :]).astype(BF16)
    u_scr[...] = _dot(xm, w_ref[...])

    lg = lbl_ref[...]
    ex = jnp.exp(lg - jnp.max(lg, axis=0, keepdims=True))
    lbw = ex / jnp.sum(ex, axis=0, keepdims=True)
    lb = jnp.sum(lbw[1:layer + 1], axis=0, keepdims=True)

    def put(seg, off, val):
        p_ref[:, seg * SEG + off:seg * SEG + off + val.shape[1]] = val

    for grp in range(4):
        c0 = grp * 128
        put(S_HQ, c0, _silu(u_scr[:, c0:c0 + 128]) * (HG_D ** -0.5))
        put(S_HV, c0, u_scr[:, 1536 + c0:1536 + c0 + 128])
        lbg = lb[:, c0:c0 + 128]
        for dr, (sk, sf) in enumerate(((S_HK0, S_HLF0), (S_HK1, S_HLF1))):
            fl = u_scr[:, 512 + dr * 512 + c0:512 + dr * 512 + c0 + 128]
            put(sf, c0, jnp.log(lbg + (1.0 - lbg) * _sigmoid(fl)))
            put(sk, c0, (1.0 - lbg) * _sigmoid(-fl))
        put(S_CG, c0, _silu(u_scr[:, 2048 + c0:2048 + c0 + 128]))
        put(S_GV, c0, u_scr[:, 3072 + c0:3072 + c0 + 128])
        put(S_DG, c0, _silu(u_scr[:, 3584 + c0:3584 + c0 + 128]))
    put(S_GQK, 0, u_scr[:, 2560:2816] * (GLA_DK ** -0.5))
    put(S_GQK, 256, u_scr[:, 2816:3072])
    lr = u_scr[:, 4096:4224]
    put(S_GLD, 0, -_softplus(-(_dot_hi(lr, wlr_ref[...]) + b2_ref[...])) * (1.0 / GLA_GATE_NORM))


def _inproj1(h, modtab, g, w_pad, lbl, wlr, b2, *, tm, ctx_len, layer):
    bsz, t_all, d = h.shape
    n_tiles = t_all // tm
    ctx_tiles = ctx_len // tm
    lat_tiles = n_tiles - ctx_tiles
    f1 = N_SEG * SEG
    return pl.pallas_call(
        functools.partial(_inproj1_kernel, layer=layer),
        grid=(bsz, n_tiles),
        in_specs=[pl.BlockSpec((None, tm, d), lambda b, t: (b, t, 0)),
                  pl.BlockSpec((None, None, 6, d), lambda b, t: (b, jnp.where(t >= ctx_tiles, 1, 0), 0, 0)),
                  _const_spec((1, d)), _const_spec((d, O_IN_PAD)), _const_spec(lbl.shape),
                  _const_spec((128, SEG)), _const_spec((1, SEG))],
        out_specs=pl.BlockSpec((None, tm, f1),
                               lambda b, t: (b, jnp.where(t < ctx_tiles, lat_tiles + t, t - ctx_tiles), 0)),
        out_shape=jax.ShapeDtypeStruct((bsz, t_all, f1), F32),
        scratch_shapes=[pltpu.VMEM((tm, O_IN_PAD), F32)],
        compiler_params=_params(("arbitrary", "arbitrary"), 44 << 20),
        name="l1_inproj",
    )(h, modtab, g, w_pad, lbl, wlr, b2)


def _gla_masks(c, d):
    sgn = 1 - 2 * d
    row = lax.broadcasted_iota(jnp.int32, (c, c), 0)
    col = lax.broadcasted_iota(jnp.int32, (c, c), 1)
    incl = (row - col) * sgn >= 0
    m_stack = jnp.concatenate([jnp.where(incl, 1.0, 0.0), jnp.ones((c, c), F32)], axis=0)
    return incl, m_stack


def _gla_chunk(q, k, v, ld, st, incl, m_stack, want_out):
    c = k.shape[0]
    cs = _dot_hi(m_stack, ld)
    bc = cs[0:c]
    btot = cs[c:c + 1]
    vb = v.astype(BF16)
    o = None
    if want_out:
        qd = (q * jnp.exp(bc)).astype(BF16)
        kn = (k * jnp.exp(-bc)).astype(BF16)
        a = jnp.where(incl, _dot_nt(qd, kn), 0.0).astype(BF16)
        o = _dot(a, vb) + _dot_nt(qd, st.astype(BF16))
    kt = (k * jnp.exp(btot - bc)).astype(BF16)
    return o, st * jnp.exp(btot) + _dot_tn(vb, kt)


def _mix1_ctx_kernel(hv_ref, hk_ref, hlf_ref, gv_ref, gqk_ref, gld_ref, sh_ref, sg_ref, *, c):
    d = pl.program_id(0)
    s = pl.program_id(2)

    @pl.when(s == 0)
    def _():
        sh_ref[...] = jnp.zeros_like(sh_ref)
        sg_ref[...] = jnp.zeros_like(sg_ref)

    incl, m_stack = _gla_masks(c, d)
    for hd in range(HG_HEADS):
        lo = hd * HG_D
        _, st = _gla_chunk(None, hk_ref[:, lo:lo + HG_D], hv_ref[:, lo:lo + HG_D], hlf_ref[:, lo:lo + HG_D],
                           sh_ref[hd], incl, m_stack, False)
        sh_ref[hd] = st
    for hd in range(GLA_HEADS):
        lo = hd * GLA_DK
        ld = jnp.where(d == 0, gld_ref[:, lo:lo + GLA_DK], gld_ref[:, 256 + lo:256 + lo + GLA_DK])
        _, st = _gla_chunk(None, gqk_ref[:, 256 + lo:256 + lo + GLA_DK], gv_ref[:, hd * GLA_DV:(hd + 1) * GLA_DV],
                           ld, sg_ref[hd], incl, m_stack, False)
        sg_ref[hd] = st


def _mix1_ctx(p1, *, seq, ctx_len, c):
    bsz = p1.shape[0]
    n_steps = ctx_len // c
    base = seq // c

    def seg(idx_fn):
        return pl.BlockSpec((None, c, SEG),
                            lambda d, b, s: (b, base + jnp.where(d == 0, s, n_steps - 1 - s), idx_fn(d)))

    return pl.pallas_call(
        functools.partial(_mix1_ctx_kernel, c=c),
        grid=(2, bsz, n_steps),
        in_specs=[seg(lambda d: S_HV), seg(lambda d: S_HK0 + 2 * d), seg(lambda d: S_HLF0 + 2 * d),
                  seg(lambda d: S_GV), seg(lambda d: S_GQK), seg(lambda d: S_GLD)],
        out_specs=[pl.BlockSpec((None, None, HG_HEADS, HG_D, HG_D), lambda d, b, s: (d, b, 0, 0, 0)),
                   pl.BlockSpec((None, None, GLA_HEADS, GLA_DV, GLA_DK), lambda d, b, s: (d, b, 0, 0, 0))],
        out_shape=[jax.ShapeDtypeStruct((2, bsz, HG_HEADS, HG_D, HG_D), F32),
                   jax.ShapeDtypeStruct((2, bsz, GLA_HEADS, GLA_DV, GLA_DK), F32)],
        compiler_params=_params(("arbitrary", "arbitrary", "arbitrary"), 32 << 20),
        name="l1_ctx_state",
    )(p1, p1, p1, p1, p1, p1)


def _mix1_lat_kernel(hq_ref, hv_ref, hk_ref, hlf_ref, gv_ref, gqk_ref, gld_ref, sh0_ref, sg0_ref, o_ref,
                     sh_scr, sg_scr, *, c):
    d = pl.program_id(0)
    s = pl.program_id(2)

    @pl.when(s == 0)
    def _():
        sh_scr[...] = sh0_ref[...]
        sg_scr[...] = sg0_ref[...]

    incl, m_stack = _gla_masks(c, d)
    for hd in range(HG_HEADS):
        lo = hd * HG_D
        o, st = _gla_chunk(hq_ref[:, lo:lo + HG_D], hk_ref[:, lo:lo + HG_D], hv_ref[:, lo:lo + HG_D],
                           hlf_ref[:, lo:lo + HG_D], sh_scr[hd], incl, m_stack, True)
        o_ref[:, lo:lo + HG_D] = o
        sh_scr[hd] = st
    for hd in range(GLA_HEADS):
        lo = hd * GLA_DK
        ld = jnp.where(d == 0, gld_ref[:, lo:lo + GLA_DK], gld_ref[:, 256 + lo:256 + lo + GLA_DK])
        o, st = _gla_chunk(gqk_ref[:, lo:lo + GLA_DK], gqk_ref[:, 256 + lo:256 + lo + GLA_DK],
                           gv_ref[:, hd * GLA_DV:(hd + 1) * GLA_DV], ld, sg_scr[hd], incl, m_stack, True)
        o_ref[:, 512 + hd * GLA_DV:512 + (hd + 1) * GLA_DV] = o
        sg_scr[hd] = st


def _mix1_lat(p1, sh0, sg0, *, seq):
    bsz, t_all, f1 = p1.shape
    rows = seq // GRID_W
    p1v = p1.reshape(bsz, t_all // GRID_W, GRID_W * f1)

    def seg(idx_fn):
        return pl.BlockSpec((None, rows, SEG),
                            lambda d, b, s: (b, 0, jnp.where(d == 0, s, GRID_W - 1 - s) * N_SEG + idx_fn(d)))

    out = pl.pallas_call(
        functools.partial(_mix1_lat_kernel, c=rows),
        grid=(2, bsz, GRID_W),
        in_specs=[seg(lambda d: S_HQ), seg(lambda d: S_HV), seg(lambda d: S_HK0 + 2 * d),
                  seg(lambda d: S_HLF0 + 2 * d), seg(lambda d: S_GV), seg(lambda d: S_GQK), seg(lambda d: S_GLD),
                  pl.BlockSpec((None, None, HG_HEADS, HG_D, HG_D), lambda d, b, s: (d, b, 0, 0, 0)),
                  pl.BlockSpec((None, None, GLA_HEADS, GLA_DV, GLA_DK), lambda d, b, s: (d, b, 0, 0, 0))],
        out_specs=pl.BlockSpec((None, None, rows, 2 * SEG),
                               lambda d, b, s: (d, b, 0, jnp.where(d == 0, s, GRID_W - 1 - s))),
        out_shape=jax.ShapeDtypeStruct((2, bsz, rows, GRID_W * 2 * SEG), F32),
        scratch_shapes=[pltpu.VMEM((HG_HEADS, HG_D, HG_D), F32), pltpu.VMEM((GLA_HEADS, GLA_DV, GLA_DK), F32)],
        compiler_params=_params(("arbitrary", "arbitrary", "arbitrary"), 32 << 20),
        name="l1_scan",
    )(p1v, p1v, p1v, p1v, p1v, p1v, p1v, sh0, sg0)
    return out.reshape(2, bsz, seq, 2 * SEG)


def _outproj1_kernel(o0_ref, o1_ref, gate_ref, h_ref, mod_ref, cng_ref, dng_ref, w_ref, out_ref):
    parts = []
    for hd in range(HG_HEADS + GLA_HEADS):
        lo = hd * 128
        y = o0_ref[:, lo:lo + 128] + o1_ref[:, lo:lo + 128]
        ng = cng_ref[...] if hd < HG_HEADS else dng_ref[...]
        parts.append((_head_norm(y, ng) * gate_ref[:, lo:lo + 128]).astype(BF16))
    y = _dot(jnp.concatenate(parts, axis=-1), w_ref[...])
    out_ref[...] = h_ref[...] + mod_ref[2:3, :] * y


def _outproj1(o, p1, h, modtab, cng, dng, w, *, tm, ctx_len):
    _, bsz, seq, d = o.shape
    ctx_tiles = ctx_len // tm
    return pl.pallas_call(
        _outproj1_kernel,
        grid=(bsz, seq // tm),
        in_specs=[pl.BlockSpec((None, None, tm, d), lambda b, t: (0, b, t, 0)),
                  pl.BlockSpec((None, None, tm, d), lambda b, t: (1, b, t, 0)),
                  pl.BlockSpec((None, tm, 2 * SEG), lambda b, t: (b, t, S_CG // 2)),
                  pl.BlockSpec((None, tm, d), lambda b, t: (b, ctx_tiles + t, 0)),
                  pl.BlockSpec((None, None, 6, d), lambda b, t: (b, 1, 0, 0)),
                  _const_spec((1, 128)), _const_spec((1, 128)), _const_spec((d, d))],
        out_specs=pl.BlockSpec((None, tm, d), lambda b, t: (b, t, 0)),
        out_shape=jax.ShapeDtypeStruct((bsz, seq, d), F32),
        compiler_params=_params(("arbitrary", "arbitrary"), 32 << 20),
        name="l1_outproj",
    )(o, o, p1, h, modtab, cng, dng, w)


def _moe_kernel(h_ref, mod_ref, ng_ref, fg_ref, rwt_ref, rb_ref, wg_ref, wu_ref, wd_ref, out_ref,
                xn_scr, route_scr, xg_scr, oa_scr, *, tk):
    e = pl.program_id(1)
    f = pl.program_id(2)
    n_f = pl.num_programs(2)
    n_e = pl.num_programs(1)
    rb = MOE_ROW_BLOCK

    @pl.when(jnp.logical_and(e == 0, f == 0))
    def _route():
        xm = _normmod(h_ref[...], ng_ref[...], mod_ref[3:4, :], mod_ref[4:5, :])
        xn_scr[...] = xm.astype(BF16)
        lg = lax.dot_general(rwt_ref[...], xm, (((1,), (1,)), ((), ())), precision=HI,
                             preferred_element_type=F32) + rb_ref[...]
        eidx = lax.broadcasted_iota(jnp.int32, lg.shape, 0)
        m1 = jnp.max(lg, axis=0, keepdims=True)
        i1 = jnp.min(jnp.where(lg == m1, eidx, N_EXPERTS), axis=0, keepdims=True)
        lg2 = jnp.where(eidx == i1, -jnp.inf, lg)
        m2 = jnp.max(lg2, axis=0, keepdims=True)
        i2 = jnp.min(jnp.where(lg2 == m2, eidx, N_EXPERTS), axis=0, keepdims=True)
        ex = jnp.exp(m2 - m1)
        p1 = 1.0 / (1.0 + ex)
        p2 = ex * p1
        gate = jnp.where(eidx == i1, p1, 0.0) + jnp.where(eidx == i2, p2, 0.0)
        sel = jnp.where(eidx == i1, 1.0, 0.0) + jnp.where(eidx == i2, 1.0, 0.0)
        lane = lax.broadcasted_iota(jnp.int32, lg.shape, 1)
        cum = sel
        sh = 1
        while sh < tk:
            cum = cum + jnp.where(lane >= sh, pltpu.roll(cum, sh, 1), 0.0)
            sh *= 2
        route_scr[0] = gate
        route_scr[1] = sel
        route_scr[2] = cum - 1.0
        out_ref[...] = jnp.zeros_like(out_ref)

    sel_row = route_scr[1, pl.ds(e, 1), :]
    pos_row = route_scr[2, pl.ds(e, 1), :]
    count = jnp.max((pos_row + 1.0) * sel_row).astype(jnp.int32)
    n_blk = (count + rb - 1) // rb

    def onehot(blk):
        rid = (lax.broadcasted_iota(jnp.int32, (rb, tk), 0) + blk * rb).astype(F32)
        return jnp.where(pos_row == rid, sel_row, 0.0)

    @pl.when(f == 0)
    def _gather():
        def body(blk, carry):
            r0 = pl.multiple_of(blk * rb, rb)
            xg_scr[pl.ds(r0, rb), :] = _dot(onehot(blk).astype(BF16), xn_scr[...]).astype(BF16)
            return carry
        lax.fori_loop(0, n_blk, body, 0)

    def ffn_body(blk, carry):
        r0 = pl.multiple_of(blk * rb, rb)
        x = xg_scr[pl.ds(r0, rb), :]
        act = (_silu(_dot(x, wg_ref[...])) * _dot(x, wu_ref[...])).astype(BF16)
        part = _dot(act, wd_ref[...])

        @pl.when(f == 0)
        def _():
            oa_scr[pl.ds(r0, rb), :] = part

        @pl.when(f != 0)
        def _():
            oa_scr[pl.ds(r0, rb), :] = oa_scr[pl.ds(r0, rb), :] + part
        return carry
    lax.fori_loop(0, n_blk, ffn_body, 0)

    @pl.when(f == n_f - 1)
    def _scatter():
        gate_row = route_scr[0, pl.ds(e, 1), :]

        def body(blk, carry):
            r0 = pl.multiple_of(blk * rb, rb)
            p = onehot(blk)
            gcol = jnp.sum(p * gate_row, axis=1, keepdims=True)
            og = (oa_scr[pl.ds(r0, rb), :] * gcol).astype(BF16)
            out_ref[...] = out_ref[...] + _dot_tn(p.astype(BF16), og)
            return carry
        lax.fori_loop(0, n_blk, body, 0)

    @pl.when(jnp.logical_and(e == n_e - 1, f == n_f - 1))
    def _final():
        h3 = h_ref[...] + mod_ref[5:6, :] * out_ref[...]
        out_ref[...] = h3 * lax.rsqrt(jnp.mean(h3 * h3, axis=-1, keepdims=True) + EPS) * fg_ref[...]


def _moe(h, modtab, ng, fg, rwt, rb, wg, wu, wd, *, tk):
    bsz, seq, d = h.shape
    tpb = seq // tk
    fw = D_FF // MOE_FF_SPLIT
    tok = lambda i, e, f: (i // tpb, i % tpb, 0)
    return pl.pallas_call(
        functools.partial(_moe_kernel, tk=tk),
        grid=(bsz * tpb, N_EXPERTS, MOE_FF_SPLIT),
        in_specs=[pl.BlockSpec((None, tk, d), tok, pipeline_mode=pl.Buffered(1)),
                  pl.BlockSpec((None, None, 6, d), lambda i, e, f: (i // tpb, 1, 0, 0)),
                  _const_spec((1, d)), _const_spec((1, d)), _const_spec((N_EXPERTS, d)), _const_spec((N_EXPERTS, 1)),
                  pl.BlockSpec((None, d, fw), lambda i, e, f: (e, 0, f)),
                  pl.BlockSpec((None, d, fw), lambda i, e, f: (e, 0, f)),
                  pl.BlockSpec((None, fw, d), lambda i, e, f: (e, f, 0))],
        out_specs=pl.BlockSpec((None, tk, d), tok),
        out_shape=jax.ShapeDtypeStruct(h.shape, F32),
        scratch_shapes=[pltpu.VMEM((tk, d), BF16), pltpu.VMEM((3, N_EXPERTS, tk), F32),
                        pltpu.VMEM((tk, d), BF16), pltpu.VMEM((tk, d), F32)],
        compiler_params=_params(("arbitrary", "arbitrary", "arbitrary"), 56 << 20),
        name="l1_moe",
    )(h, modtab, ng, fg, rwt, rb, wg, wu, wd)


def _block_diag_gate(gate_w):
    w = gate_w.reshape(2, 2, 2, 4, RG_BLOCK, RG_BLOCK)
    eye = jnp.eye(4, dtype=gate_w.dtype)
    return jnp.einsum('dghbij,bc->dghbicj', w, eye).reshape(2, 2, 2, 256, 256)


def _pad_cols(w, n):
    return jnp.pad(w, ((0, 0), (0, n - w.shape[1])))


def _layer0(h, modtab, norm_mix_g, norm_ffn_g, e_w_in, e_w_out, e_a_conv_w, e_a_conv_b, e_a_gate_w, e_a_gate_b,
            e_a_lambda, e_b_conv_w, e_b_a_log, e_b_dt_bias, e_b_norm_g, e_ffn_w_gate, e_ffn_w_up, e_ffn_w_down,
            *, tm, tt, ctx_len):
    bsz, t_all, d = h.shape
    w_in = _pad_cols(e_w_in, E_IN_PAD).astype(BF16)
    gpar = jnp.zeros((2, 128), F32)
    gpar = gpar.at[0, 2 * DN_HEADS:4 * DN_HEADS].set(e_b_a_log.reshape(-1))
    gpar = gpar.at[1, 2 * DN_HEADS:4 * DN_HEADS].set(e_b_dt_bias.reshape(-1))
    ua, gay, q, k, v, sz, gb = _inproj0(h, modtab, norm_mix_g.reshape(1, d), w_in, e_a_conv_w,
                                        e_a_conv_b.reshape(1, -1), e_b_conv_w, gpar, tm=tm, ctx_len=ctx_len)
    wg = _block_diag_gate(e_a_gate_w).astype(BF16)
    hf, hb = _rglru(ua.reshape(t_all, bsz, RG_WIDTH), wg, e_a_gate_b.reshape(4, RG_WIDTH), e_a_lambda,
                    tt=tt, ctx_len=ctx_len)
    o = _delta(q, k, v, gb, ctx_len=ctx_len)
    h = _outproj0(hf.reshape(t_all, bsz * RG_WIDTH), hb.reshape(t_all, bsz * RG_WIDTH), gay, o, sz, h, modtab,
                  e_b_norm_g.reshape(1, -1), e_w_out.astype(BF16), tm=tm, ctx_len=ctx_len)
    return _ffn(h, modtab, norm_ffn_g.reshape(1, d), e_ffn_w_gate.astype(BF16), e_ffn_w_up.astype(BF16),
                e_ffn_w_down.astype(BF16), tm=tm, ctx_len=ctx_len)


def _layer1(h, modtab, norm_mix_g, norm_ffn_g, final_norm_g, o_w_in, o_w_out, o_lb_logits, o_c_norm_g, o_d_gate_w2,
            o_d_gate_b2, o_d_norm_g, o_router_w, o_router_b, o_moe_w_gate, o_moe_w_up, o_moe_w_down,
            *, tm, tk, ctx_len, layer):
    bsz, t_all, d = h.shape
    seq = t_all - ctx_len
    w_in = _pad_cols(o_w_in, O_IN_PAD).astype(BF16)
    wlr = jnp.zeros((128, SEG), F32)
    wlr = wlr.at[0:GLA_RANK, 0:256].set(o_d_gate_w2[0]).at[GLA_RANK:2 * GLA_RANK, 256:512].set(o_d_gate_w2[1])
    p1 = _inproj1(h, modtab, norm_mix_g.reshape(1, d), w_in, o_lb_logits, wlr, o_d_gate_b2.reshape(1, SEG),
                  tm=tm, ctx_len=ctx_len, layer=layer)
    rows = seq // GRID_W
    sh0, sg0 = _mix1_ctx(p1, seq=seq, ctx_len=ctx_len, c=rows)
    o = _mix1_lat(p1, sh0, sg0, seq=seq)
    h2 = _outproj1(o, p1, h, modtab, o_c_norm_g.reshape(1, -1), o_d_norm_g.reshape(1, -1), o_w_out.astype(BF16),
                   tm=tm, ctx_len=ctx_len)
    return _moe(h2, modtab, norm_ffn_g.reshape(1, d), final_norm_g.reshape(1, d), o_router_w.T,
                o_router_b.reshape(N_EXPERTS, 1), o_moe_w_gate.astype(BF16), o_moe_w_up.astype(BF16),
                o_moe_w_down.astype(BF16), tk=tk)


def kernel(x, c, ctx, c_ctx, ada_w, ada_b, norm_mix_g, norm_ffn_g, final_norm_g, e_w_in, e_w_out, e_a_conv_w, e_a_conv_b, e_a_gate_w, e_a_gate_b, e_a_lambda, e_b_conv_w, e_b_a_log, e_b_dt_bias, e_b_norm_g, e_ffn_w_gate, e_ffn_w_up, e_ffn_w_down, o_w_in, o_w_out, o_lb_logits, o_c_norm_g, o_d_gate_w2, o_d_gate_b2, o_d_norm_g, o_router_w, o_router_b, o_moe_w_gate, o_moe_w_up, o_moe_w_down):
    bsz, seq, d = x.shape
    ctx_len = ctx.shape[1]
    assert bsz == 8 and d == D_MODEL and ada_w.shape[0] == 2
    tm = min(256, ctx_len)
    tt = min(128, ctx_len)
    tk = min(1024, seq)
    assert ctx_len % tm == 0 and seq % tm == 0 and ctx_len % DN_CHUNK == 0 and seq % DN_CHUNK == 0
    assert seq % GRID_W == 0 and ctx_len % (seq // GRID_W) == 0 and (seq // GRID_W) % 8 == 0 and seq % tk == 0

    mods = _ada(c, c_ctx, ada_w, ada_b)
    h = jnp.concatenate([ctx, x], axis=1)
    h = _layer0(h, _modtab(mods[0], bsz), norm_mix_g[0], norm_ffn_g[0], e_w_in[0], e_w_out[0], e_a_conv_w[0],
                e_a_conv_b[0], e_a_gate_w[0], e_a_gate_b[0], e_a_lambda[0], e_b_conv_w[0], e_b_a_log[0],
                e_b_dt_bias[0], e_b_norm_g[0], e_ffn_w_gate[0], e_ffn_w_up[0], e_ffn_w_down[0],
                tm=tm, tt=tt, ctx_len=ctx_len)
    return _layer1(h, _modtab(mods[1], bsz), norm_mix_g[1], norm_ffn_g[1], final_norm_g, o_w_in[0], o_w_out[0],
                   o_lb_logits, o_c_norm_g[0], o_d_gate_w2[0], o_d_gate_b2[0], o_d_norm_g[0], o_router_w[0],
                   o_router_b[0], o_moe_w_gate[0], o_moe_w_up[0], o_moe_w_down[0],
                   tm=tm, tk=tk, ctx_len=ctx_len, layer=1)
```

```python
import functools

import jax
import jax.numpy as jnp
from jax import lax
from jax.experimental import pallas as pl
from jax.experimental.pallas import tpu as pltpu

F32 = jnp.float32
BF16 = jnp.bfloat16
HI = lax.Precision.HIGHEST

EPS = 1e-6
D_MODEL = 1024
GRID_W = 64
CONV_K = 4
RG_WIDTH = 512
RG_BLOCK = 64
RG_C = 8.0
DN_HEADS = 4
DN_D = 128
DN_CHUNK = 64
HG_HEADS = 4
HG_D = 128
GLA_HEADS = 4
GLA_DK = 64
GLA_DV = 128
GLA_RANK = 16
GLA_GATE_NORM = 16.0
MIX1_CHUNK = 64
SCAN_ROWS = 128
D_FF = 2816
N_EXPERTS = 8

E_IN_PAD = 3200
O_IN_PAD = 4224
SEG = 512
N_SEG = 11
S_HQ, S_HV, S_HK0, S_HLF0, S_HK1, S_HLF1, S_GV, S_GQK, S_CG, S_DG, S_GLD = range(N_SEG)

V7X_VMEM_BYTES = 64 * 1024 * 1024
VMEM_HEADROOM_BYTES = 8 * 1024 * 1024
MOE_ROW_BLOCK = 128
MOE_FF_SPLIT = 2


def _vmem(nbytes):
    return int(min(V7X_VMEM_BYTES - VMEM_HEADROOM_BYTES, nbytes))


def _params(sem, vmem_bytes):
    return pltpu.CompilerParams(dimension_semantics=sem, vmem_limit_bytes=_vmem(vmem_bytes))


def _sigmoid(x):
    return jax.nn.sigmoid(x)


def _silu(x):
    return x * jax.nn.sigmoid(x)


def _softplus(x):
    return jnp.maximum(x, 0.0) + jnp.log1p(jnp.exp(-jnp.abs(x)))


def _gelu_tanh(x):
    return 0.5 * x * (1.0 + jnp.tanh(0.7978845608028654 * (x + 0.044715 * (x * x * x))))


def _normmod(x, g, shift, scale):
    y = x * lax.rsqrt(jnp.mean(x * x, axis=-1, keepdims=True) + EPS)
    return (y * g) * (1.0 + scale) + shift


def _dot(a, b):
    return jnp.dot(a, b, preferred_element_type=F32)


def _dot_nt(a, b):
    return lax.dot_general(a, b, (((1,), (1,)), ((), ())), preferred_element_type=F32)


def _dot_tn(a, b):
    return lax.dot_general(a, b, (((0,), (0,)), ((), ())), preferred_element_type=F32)


def _dot_hi(a, b):
    return jnp.dot(a, b, precision=HI, preferred_element_type=F32)


def _const_spec(shape):
    nd = len(shape)
    return pl.BlockSpec(shape, lambda *_: (0,) * nd, pipeline_mode=pl.Buffered(1))


def _scan_masks(c, d):
    row = lax.broadcasted_iota(jnp.int32, (c, c), 0)
    col = lax.broadcasted_iota(jnp.int32, (c, c), 1)
    dlt = row - col if d == 0 else col - row
    return dlt >= 0, dlt > 0, dlt <= 0, row == col


def _ada_kernel(cv_ref, w_ref, b_ref, o_ref):
    s = _silu(cv_ref[...]).astype(BF16)
    o_ref[...] = _dot(s, w_ref[...].astype(BF16)) + b_ref[...]


def _ada(c, c_ctx, ada_w, ada_b):
    depth, d, n6 = ada_w.shape
    bsz = c.shape[0]
    rows = 16
    cv = jnp.zeros((rows, d), F32).at[:bsz].set(c).at[bsz].set(c_ctx)
    tn = 1536
    return pl.pallas_call(
        _ada_kernel,
        grid=(depth, n6 // tn),
        in_specs=[pl.BlockSpec((rows, d), lambda l, j: (0, 0)),
                  pl.BlockSpec((None, d, tn), lambda l, j: (l, 0, j)),
                  pl.BlockSpec((None, 1, tn), lambda l, j: (l, 0, j))],
        out_specs=pl.BlockSpec((None, rows, tn), lambda l, j: (l, 0, j)),
        out_shape=jax.ShapeDtypeStruct((depth, rows, n6), F32),
        compiler_params=_params(("arbitrary", "arbitrary"), 32 << 20),
        name="ada_mod",
    )(cv, ada_w, ada_b.reshape(depth, 1, n6))


def _modtab(mods_l, bsz):
    m = mods_l.reshape(mods_l.shape[0], 6, D_MODEL)
    lat = m[:bsz]
    ctx = jnp.broadcast_to(m[bsz][None], (bsz, 6, D_MODEL))
    return jnp.stack([ctx, lat], axis=1)


def _inproj0_kernel(hp_ref, hm_ref, hn_ref, mod_ref, g_ref, w_ref, acw_ref, acb_ref, bcw_ref, gpar_ref,
                    ua_ref, gay_ref, q_ref, k_ref, v_ref, sz_ref, gb_ref, u_scr, *, tm, ctx_tiles, n_tiles):
    t = pl.program_id(1)
    x = jnp.concatenate([hp_ref[...], hm_ref[...], hn_ref[...]], axis=0)
    xm = _normmod(x, g_ref[...], mod_ref[0:1, :], mod_ref[1:2, :]).astype(BF16)
    u_scr[...] = _dot(xm, w_ref[...])

    @pl.when(jnp.logical_or(t == 0, t == ctx_tiles))
    def _():
        u_scr[0:8, :] = jnp.zeros((8, E_IN_PAD), F32)

    @pl.when(jnp.logical_or(t == ctx_tiles - 1, t == n_tiles - 1))
    def _():
        u_scr[tm + 8:tm + 16, :] = jnp.zeros((8, E_IN_PAD), F32)

    def conv(c0, width, w_ref_, w0):
        acc = u_scr[6:6 + tm, c0:c0 + width] * w_ref_[0:1, w0:w0 + width]
        for j in range(1, CONV_K):
            acc = acc + u_scr[6 + j:6 + j + tm, c0:c0 + width] * w_ref_[j:j + 1, w0:w0 + width]
        return acc

    for grp in range(RG_WIDTH // 128):
        c0 = grp * 128
        ua_ref[:, c0:c0 + 128] = conv(c0, 128, acw_ref, c0) + acb_ref[0:1, c0:c0 + 128]
    gay_ref[...] = _gelu_tanh(u_scr[8:8 + tm, 512:1024])

    for grp in range(3 * DN_HEADS):
        c0 = grp * 128
        y = _silu(conv(1024 + c0, 128, bcw_ref, c0))
        if grp < 2 * DN_HEADS:
            y = y * lax.rsqrt(jnp.sum(y * y, axis=-1, keepdims=True) + EPS)
        if grp < DN_HEADS:
            q_ref[:, c0:c0 + 128] = y * (DN_D ** -0.5)
        elif grp < 2 * DN_HEADS:
            k_ref[:, c0 - 512:c0 - 384] = y
        else:
            v_ref[:, c0 - 1024:c0 - 896] = y
    sz_ref[...] = _silu(u_scr[8:8 + tm, 2560:3072])

    xg = u_scr[8:8 + tm, 3072:3200]
    lane = lax.broadcasted_iota(jnp.int32, xg.shape, 1)
    g = -jnp.exp(gpar_ref[0:1, :]) * _softplus(xg + gpar_ref[1:2, :])
    gb_ref[...] = jnp.where(lane < 2 * DN_HEADS, _sigmoid(xg), g)


def _inproj0(h, modtab, g, w_pad, acw, acb, bcw, gpar, *, tm, ctx_len):
    bsz, t_all, d = h.shape
    n_tiles = t_all // tm
    ctx_tiles = ctx_len // tm
    tb = tm // 8
    kern = functools.partial(_inproj0_kernel, tm=tm, ctx_tiles=ctx_tiles, n_tiles=n_tiles)
    tok = lambda w: jax.ShapeDtypeStruct((bsz, t_all, w), F32)
    tok_spec = lambda w: pl.BlockSpec((None, tm, w), lambda b, t: (b, t, 0))
    return pl.pallas_call(
        kern,
        grid=(bsz, n_tiles),
        in_specs=[
            pl.BlockSpec((None, 8, d), lambda b, t: (b, jnp.maximum(t * tb - 1, 0), 0)),
            pl.BlockSpec((None, tm, d), lambda b, t: (b, t, 0)),
            pl.BlockSpec((None, 8, d), lambda b, t: (b, jnp.minimum((t + 1) * tb, t_all // 8 - 1), 0)),
            pl.BlockSpec((None, None, 6, d), lambda b, t: (b, jnp.where(t >= ctx_tiles, 1, 0), 0, 0)),
            _const_spec((1, d)),
            _const_spec((d, E_IN_PAD)),
            _const_spec((CONV_K, RG_WIDTH)),
            _const_spec((1, RG_WIDTH)),
            _const_spec((CONV_K, 3 * DN_HEADS * DN_D)),
            _const_spec((2, 128)),
        ],
        out_specs=[tok_spec(512), tok_spec(512), tok_spec(512), tok_spec(512), tok_spec(512), tok_spec(512),
                   tok_spec(128)],
        out_shape=[tok(512), tok(512), tok(512), tok(512), tok(512), tok(512), tok(128)],
        scratch_shapes=[pltpu.VMEM((tm + 16, E_IN_PAD), F32)],
        compiler_params=_params(("arbitrary", "arbitrary"), 40 << 20),
        name="l0_inproj",
    )(h, h, h, modtab, g, w_pad, acw, acb, bcw, gpar)


def _rglru_kernel(uf_ref, ub_ref, wg_ref, gbias_ref, lam_ref, hf_ref, hb_ref,
                  af_scr, xf_scr, ab_scr, xb_scr, h_scr, *, tt, bsz):
    s = pl.program_id(0)

    @pl.when(s == 0)
    def _():
        h_scr[...] = jnp.zeros_like(h_scr)

    def gates(u_ref, d, a_scr, x_scr):
        x = u_ref[...].reshape(tt * bsz, RG_WIDTH)
        xb = x.astype(BF16)
        for half in range(2):
            c0 = half * 256
            xh = xb[:, c0:c0 + 256]
            r = _sigmoid(_dot(xh, wg_ref[d, 0, half]) + gbias_ref[2 * d:2 * d + 1, c0:c0 + 256])
            i = _sigmoid(_dot(xh, wg_ref[d, 1, half]) + gbias_ref[2 * d + 1:2 * d + 2, c0:c0 + 256])
            log_a = (-RG_C) * r * _softplus(-lam_ref[d:d + 1, c0:c0 + 256])
            a = jnp.exp(log_a)
            mult = jnp.sqrt(-jnp.tanh(log_a) * (a * a + 1.0))
            xin = mult * (i * x[:, c0:c0 + 256])
            a_scr[:, :, c0:c0 + 256] = a.reshape(tt, bsz, 256)
            x_scr[:, :, c0:c0 + 256] = xin.reshape(tt, bsz, 256)

    gates(uf_ref, 0, af_scr, xf_scr)
    gates(ub_ref, 1, ab_scr, xb_scr)

    def step(t, carry):
        hf, hb = carry
        hf = af_scr[t] * hf + xf_scr[t]
        hf_ref[t] = hf
        tb = tt - 1 - t
        hb = ab_scr[tb] * hb + xb_scr[tb]
        hb_ref[tb] = hb
        return hf, hb

    hf, hb = lax.fori_loop(0, tt, step, (h_scr[0], h_scr[1]), unroll=8)
    h_scr[0] = hf
    h_scr[1] = hb


def _rglru(ua3, wg, gbias, lam, *, tt, ctx_len):
    t_all, bsz, w = ua3.shape
    n_steps = t_all // tt
    nc = ctx_len // tt

    def bwd(s):
        return jnp.where(s < nc, nc - 1 - s, n_steps + nc - 1 - s)

    blk = (tt, bsz, w)
    kern = functools.partial(_rglru_kernel, tt=tt, bsz=bsz)
    return pl.pallas_call(
        kern,
        grid=(n_steps,),
        in_specs=[pl.BlockSpec(blk, lambda s: (s, 0, 0)),
                  pl.BlockSpec(blk, lambda s: (bwd(s), 0, 0)),
                  _const_spec(wg.shape), _const_spec(gbias.shape), _const_spec(lam.shape)],
        out_specs=[pl.BlockSpec(blk, lambda s: (s, 0, 0)),
                   pl.BlockSpec(blk, lambda s: (bwd(s), 0, 0))],
        out_shape=[jax.ShapeDtypeStruct(ua3.shape, F32)] * 2,
        scratch_shapes=[pltpu.VMEM(blk, F32)] * 4 + [pltpu.VMEM((2, bsz, w), F32)],
        compiler_params=_params(("arbitrary",), 40 << 20),
        name="l0_rglru",
    )(ua3, ua3, wg, gbias, lam)


def _delta_kernel(qf_ref, kf_ref, vf_ref, gf_ref, qb_ref, kb_ref, vb_ref, gb_ref, of_ref, ob_ref, s_scr, *, n_sub):
    c = DN_CHUNK

    @pl.when(pl.program_id(1) == 0)
    def _():
        s_scr[...] = jnp.zeros_like(s_scr)

    dir_refs = ((qf_ref, kf_ref, vf_ref, gf_ref, of_ref), (qb_ref, kb_ref, vb_ref, gb_ref, ob_ref))
    masks = [_scan_masks(c, d) for d in range(2)]
    eye = jnp.where(masks[0][3], 1.0, 0.0)

    cums = {}
    for d in range(2):
        incl, _, incl_t, _ = masks[d]
        m_incl = jnp.where(incl, 1.0, 0.0)
        m_incl_t = jnp.where(incl_t, 1.0, 0.0)
        for ci in range(n_sub):
            g_all = dir_refs[d][3][ci * c:(ci + 1) * c, :]
            gc_all = _dot_hi(m_incl, g_all)
            gct_all = lax.dot_general(g_all, m_incl_t, (((0,), (0,)), ((), ())), precision=HI,
                                      preferred_element_type=F32)
            cums[d, ci] = (g_all, gc_all, gct_all)

    chains = []
    for d in range(2):
        q_ref, k_ref, v_ref, _, _ = dir_refs[d]
        incl, strict, _, _ = masks[d]
        last = c - 1 if d == 0 else 0
        for ci in range(n_sub):
            g_all, gc_all, gct_all = cums[d, ci]
            rs = slice(ci * c, (ci + 1) * c)
            for h in range(DN_HEADS):
                hs = slice(h * DN_D, (h + 1) * DN_D)
                lane = 2 * DN_HEADS + d * DN_HEADS + h
                ch = dict(d=d, ci=ci, h=h, rs=rs, hs=hs, incl=incl, strict=strict)
                ch["beta"] = g_all[:, d * DN_HEADS + h:d * DN_HEADS + h + 1]
                gc = jnp.broadcast_to(gc_all[:, lane:lane + 1], (c, DN_D))
                gc_row = jnp.broadcast_to(gct_all[lane:lane + 1, :], (c, c))
                ch["gc"] = gc
                ch["gtot"] = gc[last:last + 1, :]
                ch["decay"] = jnp.where(incl, jnp.exp(jnp.minimum(gc[:, 0:c] - gc_row, 0.0)), 0.0)
                ch["e_gc"] = jnp.exp(gc)
                ch["q"] = q_ref[rs, hs]
                ch["k"] = k_ref[rs, hs]
                ch["v"] = v_ref[rs, hs]
                chains.append(ch)

    for ch in chains:
        ch["kb"] = ch["k"] * ch["beta"]
        qk = _dot_nt(jnp.concatenate([ch["kb"], ch["q"]], axis=0).astype(BF16), ch["k"].astype(BF16))
        ch["neg"] = -jnp.where(ch["strict"], qk[0:c] * ch["decay"], 0.0)
        ch["a_qk"] = (qk[c:2 * c] * ch["decay"]).astype(BF16)
    for ch in chains:
        negb = ch["neg"].astype(BF16)
        ch["t"] = eye + ch["neg"]
        ch["p"] = _dot(negb, negb)
    n_sq = max(1, (c - 1).bit_length() - 1)
    for it in range(n_sq):
        for ch in chains:
            tp = _dot(jnp.concatenate([ch["t"], ch["p"]], axis=0).astype(BF16), ch["p"].astype(BF16))
            ch["t"] = ch["t"] + tp[0:c]
            ch["p"] = tp[c:2 * c]
    for ch in chains:
        rhs = jnp.concatenate([ch["v"] * ch["beta"], ch["kb"] * ch["e_gc"]], axis=1).astype(BF16)
        sol = _dot(ch["t"].astype(BF16), rhs)
        ch["u"] = sol[:, 0:DN_D]
        ch["wq"] = jnp.concatenate([sol[:, DN_D:2 * DN_D], ch["q"] * ch["e_gc"]], axis=0).astype(BF16)
        ch["k_tail"] = (ch["k"] * jnp.exp(ch["gtot"] - ch["gc"])).astype(BF16)

    by_key = {(ch["d"], ch["ci"], ch["h"]): ch for ch in chains}
    for step in range(n_sub):
        live = [by_key[d, step if d == 0 else n_sub - 1 - step, h] for d in range(2) for h in range(DN_HEADS)]
        for ch in live:
            ch["st"] = s_scr[ch["d"], ch["h"]]
            ch["ws"] = _dot(ch["wq"], ch["st"].astype(BF16))
        for ch in live:
            vnb = (ch["u"] - ch["ws"][0:c]).astype(BF16)
            o = ch["ws"][c:2 * c] + _dot(ch["a_qk"], vnb)
            dir_refs[ch["d"]][4][ch["rs"], ch["hs"]] = o
            s_scr[ch["d"], ch["h"]] = ch["st"] * jnp.exp(ch["gtot"]) + _dot_tn(ch["k_tail"], vnb)


def _delta(q, k, v, gb, *, ctx_len, rows):
    bsz, t_all, w = q.shape
    n_steps = t_all // rows
    nc = ctx_len // rows

    def bwd(s):
        return jnp.where(s < nc, nc - 1 - s, n_steps + nc - 1 - s)

    fwd_spec = lambda width: pl.BlockSpec((None, rows, width), lambda b, s: (b, s, 0))
    bwd_spec = lambda width: pl.BlockSpec((None, rows, width), lambda b, s: (b, bwd(s), 0))
    return pl.pallas_call(
        functools.partial(_delta_kernel, n_sub=rows // DN_CHUNK),
        grid=(bsz, n_steps),
        in_specs=[fwd_spec(w), fwd_spec(w), fwd_spec(w), fwd_spec(128),
                  bwd_spec(w), bwd_spec(w), bwd_spec(w), bwd_spec(128)],
        out_specs=[fwd_spec(w), bwd_spec(w)],
        out_shape=[jax.ShapeDtypeStruct((bsz, t_all, w), F32)] * 2,
        scratch_shapes=[pltpu.VMEM((2, DN_HEADS, DN_D, DN_D), F32)],
        compiler_params=_params(("arbitrary", "arbitrary"), 32 << 20),
        name="l0_deltanet",
    )(q, k, v, gb, q, k, v, gb)


def _head_norm(y, g):
    return y * lax.rsqrt(jnp.mean(y * y, axis=-1, keepdims=True) + EPS) * g


def _outproj0_kernel(ha_ref, gay_ref, o0_ref, o1_ref, sz_ref, h_ref, mod_ref, ng_ref, w_ref, out_ref):
    parts = [(ha_ref[...] * gay_ref[...]).astype(BF16)]
    for hd in range(DN_HEADS):
        lo = hd * DN_D
        ob = o0_ref[:, lo:lo + DN_D] + o1_ref[:, lo:lo + DN_D]
        parts.append((_head_norm(ob, ng_ref[...]) * sz_ref[:, lo:lo + DN_D]).astype(BF16))
    y = _dot(jnp.concatenate(parts, axis=-1), w_ref[...])
    out_ref[...] = h_ref[...] + mod_ref[2:3, :] * y


def _outproj0(ha, gay, o0, o1, sz, h, modtab, ng, w, *, tm, ctx_len):
    bsz, t_all, d = h.shape
    ctx_tiles = ctx_len // tm
    tok = lambda width: pl.BlockSpec((None, tm, width), lambda b, t: (b, t, 0))
    return pl.pallas_call(
        _outproj0_kernel,
        grid=(bsz, t_all // tm),
        in_specs=[tok(512), tok(512), tok(512), tok(512), tok(512), tok(d),
                  pl.BlockSpec((None, None, 6, d), lambda b, t: (b, jnp.where(t >= ctx_tiles, 1, 0), 0, 0)),
                  _const_spec((1, DN_D)), _const_spec((d, d))],
        out_specs=tok(d),
        out_shape=jax.ShapeDtypeStruct(h.shape, F32),
        compiler_params=_params(("arbitrary", "arbitrary"), 32 << 20),
        name="l0_outproj",
    )(ha, gay, o0, o1, sz, h, modtab, ng, w)


def _ffn_kernel(h_ref, mod_ref, g_ref, wg_ref, wu_ref, wd_ref, out_ref, *, n_chunks):
    x = h_ref[...]
    xm = _normmod(x, g_ref[...], mod_ref[3:4, :], mod_ref[4:5, :]).astype(BF16)
    cw = D_FF // n_chunks
    acc = jnp.zeros(x.shape, F32)
    for ci in range(n_chunks):
        c0 = ci * cw
        gate = _dot(xm, wg_ref[:, c0:c0 + cw])
        up = _dot(xm, wu_ref[:, c0:c0 + cw])
        acc = acc + _dot((_silu(gate) * up).astype(BF16), wd_ref[c0:c0 + cw, :])
    out_ref[...] = x + mod_ref[5:6, :] * acc


def _ffn(h, modtab, g, wg, wu, wd, *, tm, ctx_len):
    bsz, t_all, d = h.shape
    ctx_tiles = ctx_len // tm
    tok = pl.BlockSpec((None, tm, d), lambda b, t: (b, t, 0))
    return pl.pallas_call(
        functools.partial(_ffn_kernel, n_chunks=2),
        grid=(bsz, t_all // tm),
        in_specs=[tok,
                  pl.BlockSpec((None, None, 6, d), lambda b, t: (b, jnp.where(t >= ctx_tiles, 1, 0), 0, 0)),
                  _const_spec((1, d)), _const_spec((d, D_FF)), _const_spec((d, D_FF)), _const_spec((D_FF, d))],
        out_specs=tok,
        out_shape=jax.ShapeDtypeStruct(h.shape, F32),
        compiler_params=_params(("arbitrary", "arbitrary"), 44 << 20),
        name="l0_ffn",
    )(h, modtab, g, wg, wu, wd)


def _inproj1_kernel(h_ref, mod_ref, g_ref, w_ref, lbl_ref, wlr_ref, b2_ref, p_ref, u_scr, *, layer):
    x = h_ref[...]
    xm = _normmod(x, g_ref[...], mod_ref[0:1, :], mod_ref[1:2, :]).astype(BF16)
    u_scr[...] = _dot(xm, w_ref[...])

    lg = lbl_ref[...]
    ex = jnp.exp(lg - jnp.max(lg, axis=0, keepdims=True))
    lbw = ex / jnp.sum(ex, axis=0, keepdims=True)
    lb = jnp.sum(lbw[1:layer + 1], axis=0, keepdims=True)

    def put(seg, off, val):
        p_ref[:, seg * SEG + off:seg * SEG + off + val.shape[1]] = val

    for grp in range(4):
        c0 = grp * 128
        put(S_HQ, c0, _silu(u_scr[:, c0:c0 + 128]) * (HG_D ** -0.5))
        put(S_HV, c0, u_scr[:, 1536 + c0:1536 + c0 + 128])
        lbg = lb[:, c0:c0 + 128]
        for dr, (sk, sf) in enumerate(((S_HK0, S_HLF0), (S_HK1, S_HLF1))):
            fl = u_scr[:, 512 + dr * 512 + c0:512 + dr * 512 + c0 + 128]
            put(sf, c0, jnp.log(lbg + (1.0 - lbg) * _sigmoid(fl)))
            put(sk, c0, (1.0 - lbg) * _sigmoid(-fl))
        put(S_CG, c0, _silu(u_scr[:, 2048 + c0:2048 + c0 + 128]))
        put(S_GV, c0, u_scr[:, 3072 + c0:3072 + c0 + 128])
        put(S_DG, c0, _silu(u_scr[:, 3584 + c0:3584 + c0 + 128]))
    put(S_GQK, 0, u_scr[:, 2560:2816] * (GLA_DK ** -0.5))
    put(S_GQK, 256, u_scr[:, 2816:3072])
    lr = u_scr[:, 4096:4224]
    put(S_GLD, 0, -_softplus(-(_dot_hi(lr, wlr_ref[...]) + b2_ref[...])) * (1.0 / GLA_GATE_NORM))


def _inproj1(h, row0, n_rows, modtab, seg, g, w_pad, lbl, wlr, b2, *, tm, layer):
    bsz, _, d = h.shape
    t0 = row0 // tm
    f1 = N_SEG * SEG
    return pl.pallas_call(
        functools.partial(_inproj1_kernel, layer=layer),
        grid=(bsz, n_rows // tm),
        in_specs=[pl.BlockSpec((None, tm, d), lambda b, t: (b, t0 + t, 0)),
                  pl.BlockSpec((None, None, 6, d), lambda b, t: (b, seg, 0, 0)),
                  _const_spec((1, d)), _const_spec((d, O_IN_PAD)), _const_spec(lbl.shape),
                  _const_spec((128, SEG)), _const_spec((1, SEG))],
        out_specs=pl.BlockSpec((None, tm, f1), lambda b, t: (b, t, 0)),
        out_shape=jax.ShapeDtypeStruct((bsz, n_rows, f1), F32),
        scratch_shapes=[pltpu.VMEM((tm, O_IN_PAD), F32)],
        compiler_params=_params(("arbitrary", "arbitrary"), 44 << 20),
        name="l1_inproj",
    )(h, modtab, g, w_pad, lbl, wlr, b2)


def _gla_stream(d, q_all, k_all, ld_all, v_all, st_ref, o_ref, o_lane0, r0, n_heads, dk, dv, incl, m_incl):
    c = k_all.shape[0]
    mid = c // 2 - 1 if d == 0 else c // 2
    last = c - 1 if d == 0 else 0
    bc = _dot_hi(m_incl, ld_all)
    m = bc[mid:mid + 1]
    btot = bc[last:last + 1]
    kn = k_all * jnp.exp(m - bc)
    it = dict(d=d, r0=r0, st_ref=st_ref, o_ref=o_ref, o_lane0=o_lane0, n_heads=n_heads, dk=dk, dv=dv, incl=incl,
              c=c, kt=(kn * jnp.exp(btot - m)).astype(BF16), dec=jnp.exp(btot), v=v_all.astype(BF16),
              want_out=q_all is not None)
    if q_all is not None:
        qe = q_all * jnp.exp(bc)
        it.update(qd=(qe * jnp.exp(-m)).astype(BF16), qe=qe.astype(BF16), knb=kn.astype(BF16))
    return it


def _gla_intra(it):
    dk = it["dk"]
    it["a"] = [jnp.where(it["incl"], _dot_nt(it["qd"][:, hd * dk:(hd + 1) * dk], it["knb"][:, hd * dk:(hd + 1) * dk]),
                         0.0).astype(BF16) for hd in range(it["n_heads"])]


def _gla_advance(it):
    d, dk, dv, c, st_ref = it["d"], it["dk"], it["dv"], it["c"], it["st_ref"]
    sts = [st_ref[d, hd] for hd in range(it["n_heads"])]
    if it["want_out"]:
        for hd in range(it["n_heads"]):
            v = it["v"][:, hd * dv:(hd + 1) * dv]
            o = _dot(it["a"][hd], v) + _dot_nt(it["qe"][:, hd * dk:(hd + 1) * dk], sts[hd].astype(BF16))
            it["o_ref"][it["r0"]:it["r0"] + c, it["o_lane0"] + hd * dv:it["o_lane0"] + (hd + 1) * dv] = o
    for hd in range(it["n_heads"]):
        ks = slice(hd * dk, (hd + 1) * dk)
        st_ref[d, hd] = sts[hd] * it["dec"][:, ks] + _dot_tn(it["v"][:, hd * dv:(hd + 1) * dv], it["kt"][:, ks])


def _mix1_body(dirs, n_sub, sh_ref, sg_ref):
    c = MIX1_CHUNK
    gw = GLA_HEADS * GLA_DK
    prepared = {}
    for d, (hq_ref, hv_ref, hk_ref, hlf_ref, gv_ref, gqk_ref, gld_ref, o_ref) in enumerate(dirs):
        incl = _scan_masks(c, d)[0]
        m_incl = jnp.where(incl, 1.0, 0.0)
        for ci in range(n_sub):
            r0 = ci * c
            rs = slice(r0, r0 + c)
            prepared[d, ci, 0] = _gla_stream(
                d, None if hq_ref is None else hq_ref[rs, :], hk_ref[rs, :], hlf_ref[rs, :], hv_ref[rs, :],
                sh_ref, o_ref, 0, r0, HG_HEADS, HG_D, HG_D, incl, m_incl)
            prepared[d, ci, 1] = _gla_stream(
                d, None if hq_ref is None else gqk_ref[rs, 0:gw], gqk_ref[rs, gw:2 * gw],
                gld_ref[rs, d * gw:(d + 1) * gw], gv_ref[rs, :],
                sg_ref, o_ref, HG_HEADS * HG_D, r0, GLA_HEADS, GLA_DK, GLA_DV, incl, m_incl)
    for it in prepared.values():
        if it["want_out"]:
            _gla_intra(it)
    for step in range(n_sub):
        for d in range(2):
            for stream in range(2):
                _gla_advance(prepared[d, step if d == 0 else n_sub - 1 - step, stream])


def _mix1_ctx_kernel(*refs, n_sub):
    fwd, bwd, (sh_ref, sg_ref) = refs[0:6], refs[6:12], refs[12:14]

    @pl.when(pl.program_id(1) == 0)
    def _():
        sh_ref[...] = jnp.zeros_like(sh_ref)
        sg_ref[...] = jnp.zeros_like(sg_ref)

    _mix1_body([(None,) + tuple(r) + (None,) for r in (fwd, bwd)], n_sub, sh_ref, sg_ref)


def _mix1_lat_kernel(*refs, n_sub):
    fwd, bwd = refs[0:7], refs[7:14]
    sh0_ref, sg0_ref, of_ref, ob_ref, sh_scr, sg_scr = refs[14:20]

    @pl.when(pl.program_id(1) == 0)
    def _():
        sh_scr[...] = sh0_ref[...]
        sg_scr[...] = sg0_ref[...]

    _mix1_body([tuple(fwd) + (of_ref,), tuple(bwd) + (ob_ref,)], n_sub, sh_scr, sg_scr)


def _mix1_specs(rows, n_steps, segs_of_dir):
    specs = []
    for d in range(2):
        blk = (lambda b, s: s) if d == 0 else (lambda b, s: n_steps - 1 - s)
        for sg in segs_of_dir(d):
            specs.append(pl.BlockSpec((None, rows, SEG), lambda b, s, blk=blk, sg=sg: (b, blk(b, s), sg)))
    return specs


_SH_SHAPE = (2, HG_HEADS, HG_D, HG_D)
_SG_SHAPE = (2, GLA_HEADS, GLA_DV, GLA_DK)


def _mix1_ctx(p1c, *, rows):
    bsz, ctx_len, _ = p1c.shape
    n_steps = ctx_len // rows
    segs = lambda d: (S_HV, S_HK0 + 2 * d, S_HLF0 + 2 * d, S_GV, S_GQK, S_GLD)
    state = lambda shape: pl.BlockSpec((None,) + shape, lambda b, s: (b, 0, 0, 0, 0))
    return pl.pallas_call(
        functools.partial(_mix1_ctx_kernel, n_sub=rows // MIX1_CHUNK),
        grid=(bsz, n_steps),
        in_specs=_mix1_specs(rows, n_steps, segs),
        out_specs=[state(_SH_SHAPE), state(_SG_SHAPE)],
        out_shape=[jax.ShapeDtypeStruct((bsz,) + _SH_SHAPE, F32), jax.ShapeDtypeStruct((bsz,) + _SG_SHAPE, F32)],
        compiler_params=_params(("arbitrary", "arbitrary"), 32 << 20),
        name="l1_ctx_state",
    )(*([p1c] * 12))


def _mix1_lat(p1l, sh0, sg0, *, rows):
    bsz, seq, _ = p1l.shape
    n_steps = seq // rows
    segs = lambda d: (S_HQ, S_HV, S_HK0 + 2 * d, S_HLF0 + 2 * d, S_GV, S_GQK, S_GLD)
    state = lambda shape: pl.BlockSpec((None,) + shape, lambda b, s: (b, 0, 0, 0, 0))
    ow = HG_HEADS * HG_D + GLA_HEADS * GLA_DV
    return pl.pallas_call(
        functools.partial(_mix1_lat_kernel, n_sub=rows // MIX1_CHUNK),
        grid=(bsz, n_steps),
        in_specs=_mix1_specs(rows, n_steps, segs) + [state(_SH_SHAPE), state(_SG_SHAPE)],
        out_specs=[pl.BlockSpec((None, rows, ow), lambda b, s: (b, s, 0)),
                   pl.BlockSpec((None, rows, ow), lambda b, s: (b, n_steps - 1 - s, 0))],
        out_shape=[jax.ShapeDtypeStruct((bsz, seq, ow), F32)] * 2,
        scratch_shapes=[pltpu.VMEM(_SH_SHAPE, F32), pltpu.VMEM(_SG_SHAPE, F32)],
        compiler_params=_params(("arbitrary", "arbitrary"), 32 << 20),
        name="l1_scan",
    )(*([p1l] * 14), sh0, sg0)


def _outproj1_kernel(o0_ref, o1_ref, gate_ref, h_ref, mod_ref, cng_ref, dng_ref, w_ref, out_ref):
    parts = []
    for hd in range(HG_HEADS + GLA_HEADS):
        lo = hd * 128
        y = o0_ref[:, lo:lo + 128] + o1_ref[:, lo:lo + 128]
        ng = cng_ref[...] if hd < HG_HEADS else dng_ref[...]
        parts.append((_head_norm(y, ng) * gate_ref[:, lo:lo + 128]).astype(BF16))
    y = _dot(jnp.concatenate(parts, axis=-1), w_ref[...])
    out_ref[...] = h_ref[...] + mod_ref[2:3, :] * y


def _outproj1(o0, o1, p1l, h, modtab, cng, dng, w, *, tm):
    bsz, seq, d = h.shape
    tok = pl.BlockSpec((None, tm, d), lambda b, t: (b, t, 0))
    return pl.pallas_call(
        _outproj1_kernel,
        grid=(bsz, seq // tm),
        in_specs=[tok, tok,
                  pl.BlockSpec((None, tm, 2 * SEG), lambda b, t: (b, t, S_CG // 2)),
                  tok,
                  pl.BlockSpec((None, None, 6, d), lambda b, t: (b, 1, 0, 0)),
                  _const_spec((1, 128)), _const_spec((1, 128)), _const_spec((d, d))],
        out_specs=tok,
        out_shape=jax.ShapeDtypeStruct((bsz, seq, d), F32),
        compiler_params=_params(("arbitrary", "arbitrary"), 32 << 20),
        name="l1_outproj",
    )(o0, o1, p1l, h, modtab, cng, dng, w)


def _moe_kernel(h_ref, mod_ref, ng_ref, fg_ref, rwt_ref, rb_ref, wg_ref, wu_ref, wd_ref, out_ref,
                xn_scr, route_scr, xg_scr, oa_scr, *, tk):
    e = pl.program_id(1)
    f = pl.program_id(2)
    n_f = pl.num_programs(2)
    n_e = pl.num_programs(1)
    rb = MOE_ROW_BLOCK

    @pl.when(jnp.logical_and(e == 0, f == 0))
    def _route():
        xm = _normmod(h_ref[...], ng_ref[...], mod_ref[3:4, :], mod_ref[4:5, :])
        xn_scr[...] = xm.astype(BF16)
        lg = lax.dot_general(rwt_ref[...], xm, (((1,), (1,)), ((), ())), precision=HI,
                             preferred_element_type=F32) + rb_ref[...]
        eidx = lax.broadcasted_iota(jnp.int32, lg.shape, 0).astype(F32)
        m1 = jnp.max(lg, axis=0, keepdims=True)
        i1 = jnp.min(jnp.where(lg == m1, eidx, float(N_EXPERTS)), axis=0, keepdims=True)
        lg2 = jnp.where(eidx == i1, -jnp.inf, lg)
        m2 = jnp.max(lg2, axis=0, keepdims=True)
        i2 = jnp.min(jnp.where(lg2 == m2, eidx, float(N_EXPERTS)), axis=0, keepdims=True)
        ex = jnp.exp(m2 - m1)
        p1 = 1.0 / (1.0 + ex)
        p2 = ex * p1
        gate = jnp.where(eidx == i1, p1, 0.0) + jnp.where(eidx == i2, p2, 0.0)
        sel = jnp.where(eidx == i1, 1.0, 0.0) + jnp.where(eidx == i2, 1.0, 0.0)
        lane = lax.broadcasted_iota(jnp.int32, lg.shape, 1)
        cum = sel
        sh = 1
        while sh < tk:
            cum = cum + jnp.where(lane >= sh, pltpu.roll(cum, sh, 1), 0.0)
            sh *= 2
        route_scr[0] = gate
        route_scr[1] = sel
        route_scr[2] = cum - 1.0
        out_ref[...] = jnp.zeros_like(out_ref)

    sel_row = route_scr[1, pl.ds(e, 1), :]
    pos_row = route_scr[2, pl.ds(e, 1), :]
    count = jnp.max((pos_row + 1.0) * sel_row).astype(jnp.int32)
    n_blk = (count + rb - 1) // rb

    def onehot(blk):
        rid = (lax.broadcasted_iota(jnp.int32, (rb, tk), 0) + blk * rb).astype(F32)
        return jnp.where(pos_row == rid, sel_row, 0.0)

    @pl.when(f == 0)
    def _gather():
        def body(blk, carry):
            r0 = pl.multiple_of(blk * rb, rb)
            xg_scr[pl.ds(r0, rb), :] = _dot(onehot(blk).astype(BF16), xn_scr[...]).astype(BF16)
            return carry
        lax.fori_loop(0, n_blk, body, 0)

    def ffn_body(blk, carry):
        r0 = pl.multiple_of(blk * rb, rb)
        x = xg_scr[pl.ds(r0, rb), :]
        act = (_silu(_dot(x, wg_ref[...])) * _dot(x, wu_ref[...])).astype(BF16)
        part = _dot(act, wd_ref[...])

        @pl.when(f == 0)
        def _():
            oa_scr[pl.ds(r0, rb), :] = part

        @pl.when(f != 0)
        def _():
            oa_scr[pl.ds(r0, rb), :] = oa_scr[pl.ds(r0, rb), :] + part
        return carry
    lax.fori_loop(0, n_blk, ffn_body, 0)

    @pl.when(f == n_f - 1)
    def _scatter():
        gate_row = route_scr[0, pl.ds(e, 1), :]

        def body(blk, carry):
            r0 = pl.multiple_of(blk * rb, rb)
            p = onehot(blk)
            gcol = jnp.sum(p * gate_row, axis=1, keepdims=True)
            og = (oa_scr[pl.ds(r0, rb), :] * gcol).astype(BF16)
            out_ref[...] = out_ref[...] + _dot_tn(p.astype(BF16), og)
            return carry
        lax.fori_loop(0, n_blk, body, 0)

    @pl.when(jnp.logical_and(e == n_e - 1, f == n_f - 1))
    def _final():
        h3 = h_ref[...] + mod_ref[5:6, :] * out_ref[...]
        out_ref[...] = h3 * lax.rsqrt(jnp.mean(h3 * h3, axis=-1, keepdims=True) + EPS) * fg_ref[...]


def _moe(h, modtab, ng, fg, rwt, rb, wg, wu, wd, *, tk):
    bsz, seq, d = h.shape
    tpb = seq // tk
    fw = D_FF // MOE_FF_SPLIT
    tok = lambda i, e, f: (i // tpb, i % tpb, 0)
    return pl.pallas_call(
        functools.partial(_moe_kernel, tk=tk),
        grid=(bsz * tpb, N_EXPERTS, MOE_FF_SPLIT),
        in_specs=[pl.BlockSpec((None, tk, d), tok, pipeline_mode=pl.Buffered(1)),
                  pl.BlockSpec((None, None, 6, d), lambda i, e, f: (i // tpb, 1, 0, 0)),
                  _const_spec((1, d)), _const_spec((1, d)), _const_spec((N_EXPERTS, d)), _const_spec((N_EXPERTS, 1)),
                  pl.BlockSpec((None, d, fw), lambda i, e, f: (e, 0, f)),
                  pl.BlockSpec((None, d, fw), lambda i, e, f: (e, 0, f)),
                  pl.BlockSpec((None, fw, d), lambda i, e, f: (e, f, 0))],
        out_specs=pl.BlockSpec((None, tk, d), tok),
        out_shape=jax.ShapeDtypeStruct(h.shape, F32),
        scratch_shapes=[pltpu.VMEM((tk, d), BF16), pltpu.VMEM((3, N_EXPERTS, tk), F32),
                        pltpu.VMEM((tk, d), BF16), pltpu.VMEM((tk, d), F32)],
        compiler_params=_params(("arbitrary", "arbitrary", "arbitrary"), 56 << 20),
        name="l1_moe",
    )(h, modtab, ng, fg, rwt, rb, wg, wu, wd)


def _block_diag_gate(gate_w):
    w = gate_w.reshape(2, 2, 2, 4, RG_BLOCK, RG_BLOCK)
    eye = jnp.eye(4, dtype=gate_w.dtype)
    return jnp.einsum('dghbij,bc->dghbicj', w, eye).reshape(2, 2, 2, 256, 256)


def _pad_cols(w, n):
    return jnp.pad(w, ((0, 0), (0, n - w.shape[1])))


def _layer0(h, modtab, norm_mix_g, norm_ffn_g, e_w_in, e_w_out, e_a_conv_w, e_a_conv_b, e_a_gate_w, e_a_gate_b,
            e_a_lambda, e_b_conv_w, e_b_a_log, e_b_dt_bias, e_b_norm_g, e_ffn_w_gate, e_ffn_w_up, e_ffn_w_down,
            *, tm, tt, ctx_len):
    bsz, t_all, d = h.shape
    w_in = _pad_cols(e_w_in, E_IN_PAD).astype(BF16)
    gpar = jnp.zeros((2, 128), F32)
    gpar = gpar.at[0, 2 * DN_HEADS:4 * DN_HEADS].set(e_b_a_log.reshape(-1))
    gpar = gpar.at[1, 2 * DN_HEADS:4 * DN_HEADS].set(e_b_dt_bias.reshape(-1))
    ua, gay, q, k, v, sz, gb = _inproj0(h, modtab, norm_mix_g.reshape(1, d), w_in, e_a_conv_w,
                                        e_a_conv_b.reshape(1, -1), e_b_conv_w, gpar, tm=tm, ctx_len=ctx_len)
    wg = _block_diag_gate(e_a_gate_w).astype(BF16)
    hf, hb = _rglru(jnp.transpose(ua, (1, 0, 2)), wg, e_a_gate_b.reshape(4, RG_WIDTH), e_a_lambda,
                    tt=tt, ctx_len=ctx_len)
    ha = jnp.transpose(hf + hb, (1, 0, 2))
    o0, o1 = _delta(q, k, v, gb, ctx_len=ctx_len, rows=SCAN_ROWS)
    h = _outproj0(ha, gay, o0, o1, sz, h, modtab, e_b_norm_g.reshape(1, -1), e_w_out.astype(BF16),
                  tm=tm, ctx_len=ctx_len)
    return _ffn(h, modtab, norm_ffn_g.reshape(1, d), e_ffn_w_gate.astype(BF16), e_ffn_w_up.astype(BF16),
                e_ffn_w_down.astype(BF16), tm=tm, ctx_len=ctx_len)


def _layer1(h, modtab, norm_mix_g, norm_ffn_g, final_norm_g, o_w_in, o_w_out, o_lb_logits, o_c_norm_g, o_d_gate_w2,
            o_d_gate_b2, o_d_norm_g, o_router_w, o_router_b, o_moe_w_gate, o_moe_w_up, o_moe_w_down,
            *, tm, tk, ctx_len, layer):
    bsz, t_all, d = h.shape
    seq = t_all - ctx_len
    rows = seq // GRID_W
    hl = h[:, ctx_len:, :].reshape(bsz, rows, GRID_W, d).swapaxes(1, 2).reshape(bsz, seq, d)
    w_in = _pad_cols(o_w_in, O_IN_PAD).astype(BF16)
    wlr = jnp.zeros((128, SEG), F32)
    wlr = wlr.at[0:GLA_RANK, 0:256].set(o_d_gate_w2[0]).at[GLA_RANK:2 * GLA_RANK, 256:512].set(o_d_gate_w2[1])
    proj = functools.partial(_inproj1, g=norm_mix_g.reshape(1, d), w_pad=w_in, lbl=o_lb_logits, wlr=wlr,
                             b2=o_d_gate_b2.reshape(1, SEG), tm=tm, layer=layer)
    p1c = proj(h, 0, ctx_len, modtab, 0)
    p1l = proj(hl, 0, seq, modtab, 1)
    sh0, sg0 = _mix1_ctx(p1c, rows=SCAN_ROWS)
    o0, o1 = _mix1_lat(p1l, sh0, sg0, rows=SCAN_ROWS)
    h2 = _outproj1(o0, o1, p1l, hl, modtab, o_c_norm_g.reshape(1, -1), o_d_norm_g.reshape(1, -1),
                   o_w_out.astype(BF16), tm=tm)
    return _moe(h2, modtab, norm_ffn_g.reshape(1, d), final_norm_g.reshape(1, d), o_router_w.T,
                o_router_b.reshape(N_EXPERTS, 1), o_moe_w_gate.astype(BF16), o_moe_w_up.astype(BF16),
                o_moe_w_down.astype(BF16), tk=tk)


def kernel(x, c, ctx, c_ctx, ada_w, ada_b, norm_mix_g, norm_ffn_g, final_norm_g, e_w_in, e_w_out, e_a_conv_w, e_a_conv_b, e_a_gate_w, e_a_gate_b, e_a_lambda, e_b_conv_w, e_b_a_log, e_b_dt_bias, e_b_norm_g, e_ffn_w_gate, e_ffn_w_up, e_ffn_w_down, o_w_in, o_w_out, o_lb_logits, o_c_norm_g, o_d_gate_w2, o_d_gate_b2, o_d_norm_g, o_router_w, o_router_b, o_moe_w_gate, o_moe_w_up, o_moe_w_down):
    bsz, seq, d = x.shape
    ctx_len = ctx.shape[1]
    assert bsz == 8 and d == D_MODEL and ada_w.shape[0] == 2
    tm = min(256, ctx_len)
    tt = min(128, ctx_len)
    tk = min(1024, seq)
    assert ctx_len % tm == 0 and seq % tm == 0 and ctx_len % SCAN_ROWS == 0 and seq % SCAN_ROWS == 0
    assert seq % GRID_W == 0 and seq % tk == 0

    mods = _ada(c, c_ctx, ada_w, ada_b)
    h = jnp.concatenate([ctx, x], axis=1)
    h = _layer0(h, _modtab(mods[0], bsz), norm_mix_g[0], norm_ffn_g[0], e_w_in[0], e_w_out[0], e_a_conv_w[0],
                e_a_conv_b[0], e_a_gate_w[0], e_a_gate_b[0], e_a_lambda[0], e_b_conv_w[0], e_b_a_log[0],
                e_b_dt_bias[0], e_b_norm_g[0], e_ffn_w_gate[0], e_ffn_w_up[0], e_ffn_w_down[0],
                tm=tm, tt=tt, ctx_len=ctx_len)
    out_cm = _layer1(h, _modtab(mods[1], bsz), norm_mix_g[1], norm_ffn_g[1], final_norm_g, o_w_in[0], o_w_out[0],
                     o_lb_logits, o_c_norm_g[0], o_d_gate_w2[0], o_d_gate_b2[0], o_d_norm_g[0], o_router_w[0],
                     o_router_b[0], o_moe_w_gate[0], o_moe_w_up[0], o_moe_w_down[0],
                     tm=tm, tk=tk, ctx_len=ctx_len, layer=1)
    rows = seq // GRID_W
    return out_cm.reshape(bsz, GRID_W, rows, d).swapaxes(1, 2).reshape(bsz, seq, d)
```

```python
import functools

import jax
import jax.numpy as jnp
from jax import lax
from jax.experimental import pallas as pl
from jax.experimental.pallas import tpu as pltpu

F32 = jnp.float32
BF16 = jnp.bfloat16
HI = lax.Precision.HIGHEST

EPS = 1e-6
D_MODEL = 1024
GRID_W = 64
CONV_K = 4
RG_WIDTH = 512
RG_BLOCK = 64
RG_C = 8.0
DN_HEADS = 4
DN_D = 128
DN_CHUNK = 64
HG_HEADS = 4
HG_D = 128
GLA_HEADS = 4
GLA_DK = 64
GLA_DV = 128
GLA_RANK = 16
GLA_GATE_NORM = 16.0
MIX1_CHUNK = 64
SCAN_ROWS = 256
D_FF = 2816
N_EXPERTS = 8

E_IN_PAD = 3200
O_IN_PAD = 4224
SEG = 512
N_SEG = 11
S_HQ, S_HV, S_HK0, S_HLF0, S_HK1, S_HLF1, S_GV, S_GQK, S_CG, S_DG, S_GLD = range(N_SEG)

V7X_VMEM_BYTES = 64 * 1024 * 1024
VMEM_HEADROOM_BYTES = 8 * 1024 * 1024
MOE_ROW_BLOCK = 128
MOE_GATHER_ROWS = 256
MOE_FF_SPLIT = 2


def _vmem(nbytes):
    return int(min(V7X_VMEM_BYTES - VMEM_HEADROOM_BYTES, nbytes))


def _params(sem, vmem_bytes):
    return pltpu.CompilerParams(dimension_semantics=sem, vmem_limit_bytes=_vmem(vmem_bytes))


def _sigmoid(x):
    return jax.nn.sigmoid(x)


def _sigmoid_tanh(x):
    return 0.5 * jnp.tanh(0.5 * x) + 0.5


def _silu(x):
    return x * jax.nn.sigmoid(x)


def _softplus(x):
    return jnp.maximum(x, 0.0) + jnp.log1p(jnp.exp(-jnp.abs(x)))


def _gelu_tanh(x):
    return 0.5 * x * (1.0 + jnp.tanh(0.7978845608028654 * (x + 0.044715 * (x * x * x))))


def _normmod(x, g, shift, scale):
    y = x * lax.rsqrt(jnp.mean(x * x, axis=-1, keepdims=True) + EPS)
    return (y * g) * (1.0 + scale) + shift


def _dot(a, b):
    return jnp.dot(a, b, preferred_element_type=F32)


def _dot_nt(a, b):
    return lax.dot_general(a, b, (((1,), (1,)), ((), ())), preferred_element_type=F32)


def _dot_tn(a, b):
    return lax.dot_general(a, b, (((0,), (0,)), ((), ())), preferred_element_type=F32)


def _dot_hi(a, b):
    return jnp.dot(a, b, precision=HI, preferred_element_type=F32)


def _const_spec(shape):
    nd = len(shape)
    return pl.BlockSpec(shape, lambda *_: (0,) * nd, pipeline_mode=pl.Buffered(1))


def _scan_masks(c, d):
    row = lax.broadcasted_iota(jnp.int32, (c, c), 0)
    col = lax.broadcasted_iota(jnp.int32, (c, c), 1)
    dlt = row - col if d == 0 else col - row
    return dlt >= 0, dlt > 0, dlt <= 0, row == col


def _ada_kernel(cv_ref, w_ref, b_ref, o_ref):
    s = _silu(cv_ref[...]).astype(BF16)
    o_ref[...] = _dot(s, w_ref[...].astype(BF16)) + b_ref[...]


def _ada(c, c_ctx, ada_w, ada_b):
    depth, d, n6 = ada_w.shape
    bsz = c.shape[0]
    rows = 16
    cv = jnp.zeros((rows, d), F32).at[:bsz].set(c).at[bsz].set(c_ctx)
    tn = 1536
    return pl.pallas_call(
        _ada_kernel,
        grid=(depth, n6 // tn),
        in_specs=[pl.BlockSpec((rows, d), lambda l, j: (0, 0)),
                  pl.BlockSpec((None, d, tn), lambda l, j: (l, 0, j)),
                  pl.BlockSpec((None, 1, tn), lambda l, j: (l, 0, j))],
        out_specs=pl.BlockSpec((None, rows, tn), lambda l, j: (l, 0, j)),
        out_shape=jax.ShapeDtypeStruct((depth, rows, n6), F32),
        compiler_params=_params(("arbitrary", "arbitrary"), 32 << 20),
        name="ada_mod",
    )(cv, ada_w, ada_b.reshape(depth, 1, n6))


def _modtab(mods_l, bsz):
    m = mods_l.reshape(mods_l.shape[0], 6, D_MODEL)
    lat = m[:bsz]
    ctx = jnp.broadcast_to(m[bsz][None], (bsz, 6, D_MODEL))
    return jnp.stack([ctx, lat], axis=1)


def _inproj0_kernel(hp_ref, hm_ref, hn_ref, mod_ref, g_ref, w_ref, acw_ref, acb_ref, bcw_ref, gpar_ref,
                    ua_ref, gay_ref, q_ref, k_ref, v_ref, sz_ref, gb_ref, u_scr, *, tm, ctx_tiles, n_tiles):
    t = pl.program_id(1)
    x = jnp.concatenate([hp_ref[...], hm_ref[...], hn_ref[...]], axis=0)
    xm = _normmod(x, g_ref[...], mod_ref[0:1, :], mod_ref[1:2, :]).astype(BF16)
    seg_first = jnp.logical_or(t == 0, t == ctx_tiles)
    seg_last = jnp.logical_or(t == ctx_tiles - 1, t == n_tiles - 1)

    def project(c0, width, conv_input):
        u_scr[:, c0:c0 + width] = _dot(xm, w_ref[:, c0:c0 + width])
        if conv_input:
            u_scr[0:8, c0:c0 + width] = jnp.where(seg_first, 0.0, u_scr[0:8, c0:c0 + width])
            u_scr[tm + 8:tm + 16, c0:c0 + width] = jnp.where(seg_last, 0.0, u_scr[tm + 8:tm + 16, c0:c0 + width])

    def conv(c0, width, w_ref_, w0):
        acc = u_scr[6:6 + tm, c0:c0 + width] * w_ref_[0:1, w0:w0 + width]
        for j in range(1, CONV_K):
            acc = acc + u_scr[6 + j:6 + j + tm, c0:c0 + width] * w_ref_[j:j + 1, w0:w0 + width]
        return acc

    project(0, RG_WIDTH, True)
    for grp in range(RG_WIDTH // 128):
        c0 = grp * 128
        ua_ref[:, c0:c0 + 128] = conv(c0, 128, acw_ref, c0) + acb_ref[0:1, c0:c0 + 128]
    project(512, 512, False)
    gay_ref[...] = _gelu_tanh(u_scr[8:8 + tm, 512:1024])

    for grp in range(3 * DN_HEADS):
        c0 = grp * 128
        if grp % DN_HEADS == 0:
            project(1024 + c0, DN_HEADS * DN_D, True)
        y = _silu(conv(1024 + c0, 128, bcw_ref, c0))
        if grp < 2 * DN_HEADS:
            y = y * lax.rsqrt(jnp.sum(y * y, axis=-1, keepdims=True) + EPS)
        if grp < DN_HEADS:
            q_ref[:, c0:c0 + 128] = y * (DN_D ** -0.5)
        elif grp < 2 * DN_HEADS:
            k_ref[:, c0 - 512:c0 - 384] = y
        else:
            v_ref[:, c0 - 1024:c0 - 896] = y
    project(2560, 512, False)
    sz_ref[...] = _silu(u_scr[8:8 + tm, 2560:3072])

    project(3072, 128, False)
    xg = u_scr[8:8 + tm, 3072:3200]
    lane = lax.broadcasted_iota(jnp.int32, xg.shape, 1)
    g = -jnp.exp(gpar_ref[0:1, :]) * _softplus(xg + gpar_ref[1:2, :])
    gb_ref[...] = jnp.where(lane < 2 * DN_HEADS, _sigmoid(xg), g)


def _inproj0(h, modtab, g, w_pad, acw, acb, bcw, gpar, *, tm, ctx_len):
    bsz, t_all, d = h.shape
    n_tiles = t_all // tm
    ctx_tiles = ctx_len // tm
    tb = tm // 8
    kern = functools.partial(_inproj0_kernel, tm=tm, ctx_tiles=ctx_tiles, n_tiles=n_tiles)
    tok = lambda w: jax.ShapeDtypeStruct((bsz, t_all, w), F32)
    tok_spec = lambda w: pl.BlockSpec((None, tm, w), lambda b, t: (b, t, 0))
    return pl.pallas_call(
        kern,
        grid=(bsz, n_tiles),
        in_specs=[
            pl.BlockSpec((None, 8, d), lambda b, t: (b, jnp.maximum(t * tb - 1, 0), 0)),
            pl.BlockSpec((None, tm, d), lambda b, t: (b, t, 0)),
            pl.BlockSpec((None, 8, d), lambda b, t: (b, jnp.minimum((t + 1) * tb, t_all // 8 - 1), 0)),
            pl.BlockSpec((None, None, 6, d), lambda b, t: (b, jnp.where(t >= ctx_tiles, 1, 0), 0, 0)),
            _const_spec((1, d)),
            _const_spec((d, E_IN_PAD)),
            _const_spec((CONV_K, RG_WIDTH)),
            _const_spec((1, RG_WIDTH)),
            _const_spec((CONV_K, 3 * DN_HEADS * DN_D)),
            _const_spec((2, 128)),
        ],
        out_specs=[tok_spec(512), tok_spec(512), tok_spec(512), tok_spec(512), tok_spec(512), tok_spec(512),
                   tok_spec(128)],
        out_shape=[tok(512), tok(512), tok(512), tok(512), tok(512), tok(512), tok(128)],
        scratch_shapes=[pltpu.VMEM((tm + 16, E_IN_PAD), F32)],
        compiler_params=_params(("arbitrary", "arbitrary"), 40 << 20),
        name="l0_inproj",
    )(h, h, h, modtab, g, w_pad, acw, acb, bcw, gpar)


def _rglru_kernel(uf_ref, ub_ref, wg_ref, gbias_ref, lam_ref, hf_ref, hb_ref,
                  af_scr, xf_scr, ab_scr, xb_scr, h_scr, *, tt, bsz):
    s = pl.program_id(0)

    @pl.when(s == 0)
    def _():
        h_scr[...] = jnp.zeros_like(h_scr)

    def gates(u_ref, d, a_scr, x_scr):
        x = u_ref[...].reshape(tt * bsz, RG_WIDTH)
        xb = x.astype(BF16)
        for half in range(2):
            c0 = half * 256
            xh = xb[:, c0:c0 + 256]
            r = _sigmoid_tanh(_dot(xh, wg_ref[d, 0, half]) + gbias_ref[2 * d:2 * d + 1, c0:c0 + 256])
            i = _sigmoid_tanh(_dot(xh, wg_ref[d, 1, half]) + gbias_ref[2 * d + 1:2 * d + 2, c0:c0 + 256])
            log_a = (-RG_C) * r * _softplus(-lam_ref[d:d + 1, c0:c0 + 256])
            a = jnp.exp(log_a)
            mult = jnp.sqrt(-jnp.tanh(log_a) * (a * a + 1.0))
            xin = mult * (i * x[:, c0:c0 + 256])
            a_scr[:, :, c0:c0 + 256] = a.reshape(tt, bsz, 256)
            x_scr[:, :, c0:c0 + 256] = xin.reshape(tt, bsz, 256)

    gates(uf_ref, 0, af_scr, xf_scr)
    gates(ub_ref, 1, ab_scr, xb_scr)

    def step(t, carry):
        hf, hb = carry
        hf = af_scr[t] * hf + xf_scr[t]
        hf_ref[t] = hf
        tb = tt - 1 - t
        hb = ab_scr[tb] * hb + xb_scr[tb]
        hb_ref[tb] = hb
        return hf, hb

    hf, hb = lax.fori_loop(0, tt, step, (h_scr[0], h_scr[1]), unroll=8)
    h_scr[0] = hf
    h_scr[1] = hb


def _rglru(ua3, wg, gbias, lam, *, tt, ctx_len):
    t_all, bsz, w = ua3.shape
    n_steps = t_all // tt
    nc = ctx_len // tt

    def bwd(s):
        return jnp.where(s < nc, nc - 1 - s, n_steps + nc - 1 - s)

    blk = (tt, bsz, w)
    kern = functools.partial(_rglru_kernel, tt=tt, bsz=bsz)
    return pl.pallas_call(
        kern,
        grid=(n_steps,),
        in_specs=[pl.BlockSpec(blk, lambda s: (s, 0, 0)),
                  pl.BlockSpec(blk, lambda s: (bwd(s), 0, 0)),
                  _const_spec(wg.shape), _const_spec(gbias.shape), _const_spec(lam.shape)],
        out_specs=[pl.BlockSpec(blk, lambda s: (s, 0, 0)),
                   pl.BlockSpec(blk, lambda s: (bwd(s), 0, 0))],
        out_shape=[jax.ShapeDtypeStruct(ua3.shape, F32)] * 2,
        scratch_shapes=[pltpu.VMEM(blk, F32)] * 4 + [pltpu.VMEM((2, bsz, w), F32)],
        compiler_params=_params(("arbitrary",), 40 << 20),
        name="l0_rglru",
    )(ua3, ua3, wg, gbias, lam)


def _delta_kernel(qf_ref, kf_ref, vf_ref, gf_ref, qb_ref, kb_ref, vb_ref, gb_ref, of_ref, ob_ref, s_scr, *, n_sub):
    c = DN_CHUNK

    @pl.when(pl.program_id(1) == 0)
    def _():
        s_scr[...] = jnp.zeros_like(s_scr)

    dir_refs = ((qf_ref, kf_ref, vf_ref, gf_ref, of_ref), (qb_ref, kb_ref, vb_ref, gb_ref, ob_ref))
    masks = [_scan_masks(c, d) for d in range(2)]
    eye = jnp.where(masks[0][3], 1.0, 0.0)

    cums = {}
    for d in range(2):
        incl, _, incl_t, _ = masks[d]
        m_incl = jnp.where(incl, 1.0, 0.0)
        m_incl_t = jnp.where(incl_t, 1.0, 0.0)
        for ci in range(n_sub):
            g_all = dir_refs[d][3][ci * c:(ci + 1) * c, :]
            gc_all = _dot_hi(m_incl, g_all)
            gct_all = lax.dot_general(g_all, m_incl_t, (((0,), (0,)), ((), ())), precision=HI,
                                      preferred_element_type=F32)
            cums[d, ci] = (g_all, gc_all, gct_all)

    chains = []
    for d in range(2):
        q_ref, k_ref, v_ref, _, _ = dir_refs[d]
        incl, strict, _, _ = masks[d]
        last = c - 1 if d == 0 else 0
        for ci in range(n_sub):
            g_all, gc_all, gct_all = cums[d, ci]
            rs = slice(ci * c, (ci + 1) * c)
            for h in range(DN_HEADS):
                hs = slice(h * DN_D, (h + 1) * DN_D)
                lane = 2 * DN_HEADS + d * DN_HEADS + h
                ch = dict(d=d, ci=ci, h=h, rs=rs, hs=hs, incl=incl, strict=strict)
                ch["beta"] = g_all[:, d * DN_HEADS + h:d * DN_HEADS + h + 1]
                gc = jnp.broadcast_to(gc_all[:, lane:lane + 1], (c, DN_D))
                gc_row = jnp.broadcast_to(gct_all[lane:lane + 1, :], (c, c))
                ch["gc"] = gc
                ch["gtot"] = gc[last:last + 1, :]
                ch["decay"] = jnp.where(incl, jnp.exp(jnp.minimum(gc[:, 0:c] - gc_row, 0.0)), 0.0)
                ch["e_gc"] = jnp.exp(gc)
                ch["q"] = q_ref[rs, hs]
                ch["k"] = k_ref[rs, hs]
                ch["v"] = v_ref[rs, hs]
                chains.append(ch)

    for ch in chains:
        ch["kb"] = ch["k"] * ch["beta"]
        qk = _dot_nt(jnp.concatenate([ch["kb"], ch["q"]], axis=0).astype(BF16), ch["k"].astype(BF16))
        ch["neg"] = -jnp.where(ch["strict"], qk[0:c] * ch["decay"], 0.0)
        ch["a_qk"] = (qk[c:2 * c] * ch["decay"]).astype(BF16)
    for ch in chains:
        negb = ch["neg"].astype(BF16)
        ch["t"] = eye + ch["neg"]
        ch["p"] = _dot(negb, negb)
    n_sq = max(1, (c - 1).bit_length() - 1)
    for it in range(n_sq):
        for ch in chains:
            tp = _dot(jnp.concatenate([ch["t"], ch["p"]], axis=0).astype(BF16), ch["p"].astype(BF16))
            ch["t"] = ch["t"] + tp[0:c]
            ch["p"] = tp[c:2 * c]
    for ch in chains:
        rhs = jnp.concatenate([ch["v"] * ch["beta"], ch["kb"] * ch["e_gc"]], axis=1).astype(BF16)
        sol = _dot(ch["t"].astype(BF16), rhs)
        ch["u"] = sol[:, 0:DN_D]
        ch["wq"] = jnp.concatenate([sol[:, DN_D:2 * DN_D], ch["q"] * ch["e_gc"]], axis=0).astype(BF16)
        ch["k_tail"] = (ch["k"] * jnp.exp(ch["gtot"] - ch["gc"])).astype(BF16)

    by_key = {(ch["d"], ch["ci"], ch["h"]): ch for ch in chains}
    for step in range(n_sub):
        live = [by_key[d, step if d == 0 else n_sub - 1 - step, h] for d in range(2) for h in range(DN_HEADS)]
        for ch in live:
            ch["st"] = s_scr[ch["d"], ch["h"]]
            ch["ws"] = _dot(ch["wq"], ch["st"].astype(BF16))
        for ch in live:
            vnb = (ch["u"] - ch["ws"][0:c]).astype(BF16)
            o = ch["ws"][c:2 * c] + _dot(ch["a_qk"], vnb)
            dir_refs[ch["d"]][4][ch["rs"], ch["hs"]] = o
            s_scr[ch["d"], ch["h"]] = ch["st"] * jnp.exp(ch["gtot"]) + _dot_tn(ch["k_tail"], vnb)


def _delta(q, k, v, gb, *, ctx_len, rows):
    bsz, t_all, w = q.shape
    n_steps = t_all // rows
    nc = ctx_len // rows

    def bwd(s):
        return jnp.where(s < nc, nc - 1 - s, n_steps + nc - 1 - s)

    fwd_spec = lambda width: pl.BlockSpec((None, rows, width), lambda b, s: (b, s, 0))
    bwd_spec = lambda width: pl.BlockSpec((None, rows, width), lambda b, s: (b, bwd(s), 0))
    return pl.pallas_call(
        functools.partial(_delta_kernel, n_sub=rows // DN_CHUNK),
        grid=(bsz, n_steps),
        in_specs=[fwd_spec(w), fwd_spec(w), fwd_spec(w), fwd_spec(128),
                  bwd_spec(w), bwd_spec(w), bwd_spec(w), bwd_spec(128)],
        out_specs=[fwd_spec(w), bwd_spec(w)],
        out_shape=[jax.ShapeDtypeStruct((bsz, t_all, w), F32)] * 2,
        scratch_shapes=[pltpu.VMEM((2, DN_HEADS, DN_D, DN_D), F32)],
        compiler_params=_params(("arbitrary", "arbitrary"), 32 << 20),
        name="l0_deltanet",
    )(q, k, v, gb, q, k, v, gb)


def _head_norm(y, g):
    return y * lax.rsqrt(jnp.mean(y * y, axis=-1, keepdims=True) + EPS) * g


def _outproj0_kernel(ha_ref, gay_ref, o0_ref, o1_ref, sz_ref, h_ref, mod_ref, ng_ref, w_ref, out_ref):
    parts = [(ha_ref[...] * gay_ref[...]).astype(BF16)]
    for hd in range(DN_HEADS):
        lo = hd * DN_D
        ob = o0_ref[:, lo:lo + DN_D] + o1_ref[:, lo:lo + DN_D]
        parts.append((_head_norm(ob, ng_ref[...]) * sz_ref[:, lo:lo + DN_D]).astype(BF16))
    y = _dot(jnp.concatenate(parts, axis=-1), w_ref[...])
    out_ref[...] = h_ref[...] + mod_ref[2:3, :] * y


def _outproj0(ha, gay, o0, o1, sz, h, modtab, ng, w, *, tm, ctx_len):
    bsz, t_all, d = h.shape
    ctx_tiles = ctx_len // tm
    tok = lambda width: pl.BlockSpec((None, tm, width), lambda b, t: (b, t, 0))
    return pl.pallas_call(
        _outproj0_kernel,
        grid=(bsz, t_all // tm),
        in_specs=[tok(512), tok(512), tok(512), tok(512), tok(512), tok(d),
                  pl.BlockSpec((None, None, 6, d), lambda b, t: (b, jnp.where(t >= ctx_tiles, 1, 0), 0, 0)),
                  _const_spec((1, DN_D)), _const_spec((d, d))],
        out_specs=tok(d),
        out_shape=jax.ShapeDtypeStruct(h.shape, F32),
        compiler_params=_params(("arbitrary", "arbitrary"), 32 << 20),
        name="l0_outproj",
    )(ha, gay, o0, o1, sz, h, modtab, ng, w)


def _ffn_kernel(h_ref, mod_ref, g_ref, wg_ref, wu_ref, wd_ref, out_ref, *, n_chunks):
    x = h_ref[...]
    xm = _normmod(x, g_ref[...], mod_ref[3:4, :], mod_ref[4:5, :]).astype(BF16)
    cw = D_FF // n_chunks
    acc = jnp.zeros(x.shape, F32)
    for ci in range(n_chunks):
        c0 = ci * cw
        gate = _dot(xm, wg_ref[:, c0:c0 + cw])
        up = _dot(xm, wu_ref[:, c0:c0 + cw])
        acc = acc + _dot((_silu(gate) * up).astype(BF16), wd_ref[c0:c0 + cw, :])
    out_ref[...] = x + mod_ref[5:6, :] * acc


def _ffn(h, modtab, g, wg, wu, wd, *, tm, ctx_len):
    bsz, t_all, d = h.shape
    ctx_tiles = ctx_len // tm
    tok = pl.BlockSpec((None, tm, d), lambda b, t: (b, t, 0))
    return pl.pallas_call(
        functools.partial(_ffn_kernel, n_chunks=2),
        grid=(bsz, t_all // tm),
        in_specs=[tok,
                  pl.BlockSpec((None, None, 6, d), lambda b, t: (b, jnp.where(t >= ctx_tiles, 1, 0), 0, 0)),
                  _const_spec((1, d)), _const_spec((d, D_FF)), _const_spec((d, D_FF)), _const_spec((D_FF, d))],
        out_specs=tok,
        out_shape=jax.ShapeDtypeStruct(h.shape, F32),
        compiler_params=_params(("arbitrary", "arbitrary"), 44 << 20),
        name="l0_ffn",
    )(h, modtab, g, wg, wu, wd)


def _inproj1_kernel(h_ref, mod_ref, g_ref, w_ref, lbl_ref, wlr_ref, b2_ref, p_ref, u_scr, *, layer):
    x = h_ref[...]
    xm = _normmod(x, g_ref[...], mod_ref[0:1, :], mod_ref[1:2, :]).astype(BF16)

    lg = lbl_ref[...]
    ex = jnp.exp(lg - jnp.max(lg, axis=0, keepdims=True))
    lbw = ex / jnp.sum(ex, axis=0, keepdims=True)
    lb = jnp.sum(lbw[1:layer + 1], axis=0, keepdims=True)

    def project(c0, width):
        u_scr[:, c0:c0 + width] = _dot(xm, w_ref[:, c0:c0 + width])

    def put(seg, off, val):
        p_ref[:, seg * SEG + off:seg * SEG + off + val.shape[1]] = val

    def groups(fn):
        for grp in range(SEG // 128):
            fn(grp * 128)

    project(0, SEG)
    groups(lambda c0: put(S_HQ, c0, _silu(u_scr[:, c0:c0 + 128]) * (HG_D ** -0.5)))
    for dr, (sk, sf) in enumerate(((S_HK0, S_HLF0), (S_HK1, S_HLF1))):
        project(512 + dr * 512, SEG)

        def forget(c0, dr=dr, sk=sk, sf=sf):
            lbg = lb[:, c0:c0 + 128]
            fl = u_scr[:, 512 + dr * 512 + c0:512 + dr * 512 + c0 + 128]
            put(sf, c0, jnp.log(lbg + (1.0 - lbg) * _sigmoid(fl)))
            put(sk, c0, (1.0 - lbg) * _sigmoid(-fl))
        groups(forget)
    project(1536, SEG)
    groups(lambda c0: put(S_HV, c0, u_scr[:, 1536 + c0:1536 + c0 + 128]))
    project(2048, SEG)
    groups(lambda c0: put(S_CG, c0, _silu(u_scr[:, 2048 + c0:2048 + c0 + 128])))
    project(2560, SEG)
    put(S_GQK, 0, u_scr[:, 2560:2816] * (GLA_DK ** -0.5))
    put(S_GQK, 256, u_scr[:, 2816:3072])
    project(3072, SEG)
    groups(lambda c0: put(S_GV, c0, u_scr[:, 3072 + c0:3072 + c0 + 128]))
    project(3584, SEG)
    groups(lambda c0: put(S_DG, c0, _silu(u_scr[:, 3584 + c0:3584 + c0 + 128])))
    project(4096, 128)
    lr = u_scr[:, 4096:4224]
    put(S_GLD, 0, -_softplus(-(_dot_hi(lr, wlr_ref[...]) + b2_ref[...])) * (1.0 / GLA_GATE_NORM))


def _inproj1(h, row0, n_rows, modtab, seg, g, w_pad, lbl, wlr, b2, *, tm, layer):
    bsz, _, d = h.shape
    t0 = row0 // tm
    f1 = N_SEG * SEG
    return pl.pallas_call(
        functools.partial(_inproj1_kernel, layer=layer),
        grid=(bsz, n_rows // tm),
        in_specs=[pl.BlockSpec((None, tm, d), lambda b, t: (b, t0 + t, 0)),
                  pl.BlockSpec((None, None, 6, d), lambda b, t: (b, seg, 0, 0)),
                  _const_spec((1, d)), _const_spec((d, O_IN_PAD)), _const_spec(lbl.shape),
                  _const_spec((128, SEG)), _const_spec((1, SEG))],
        out_specs=pl.BlockSpec((None, tm, f1), lambda b, t: (b, t, 0)),
        out_shape=jax.ShapeDtypeStruct((bsz, n_rows, f1), F32),
        scratch_shapes=[pltpu.VMEM((tm, O_IN_PAD), F32)],
        compiler_params=_params(("arbitrary", "arbitrary"), 44 << 20),
        name="l1_inproj",
    )(h, modtab, g, w_pad, lbl, wlr, b2)


def _gla_stream(d, q_all, k_all, ld_all, v_all, st_ref, o_ref, o_lane0, r0, n_heads, dk, dv, incl, m_incl):
    c = k_all.shape[0]
    mid = c // 2 - 1 if d == 0 else c // 2
    last = c - 1 if d == 0 else 0
    bc = _dot_hi(m_incl, ld_all)
    m = bc[mid:mid + 1]
    btot = bc[last:last + 1]
    kn = k_all * jnp.exp(m - bc)
    it = dict(d=d, r0=r0, st_ref=st_ref, o_ref=o_ref, o_lane0=o_lane0, n_heads=n_heads, dk=dk, dv=dv, incl=incl,
              c=c, kt=(kn * jnp.exp(btot - m)).astype(BF16), dec=jnp.exp(btot), v=v_all.astype(BF16),
              want_out=q_all is not None)
    if q_all is not None:
        qe = q_all * jnp.exp(bc)
        it.update(qd=(qe * jnp.exp(-m)).astype(BF16), qe=qe.astype(BF16), knb=kn.astype(BF16))
    return it


def _gla_intra(it):
    dk = it["dk"]
    it["a"] = [jnp.where(it["incl"], _dot_nt(it["qd"][:, hd * dk:(hd + 1) * dk], it["knb"][:, hd * dk:(hd + 1) * dk]),
                         0.0).astype(BF16) for hd in range(it["n_heads"])]


def _gla_advance(it):
    d, dk, dv, c, st_ref = it["d"], it["dk"], it["dv"], it["c"], it["st_ref"]
    sts = [st_ref[d, hd] for hd in range(it["n_heads"])]
    if it["want_out"]:
        for hd in range(it["n_heads"]):
            v = it["v"][:, hd * dv:(hd + 1) * dv]
            o = _dot(it["a"][hd], v) + _dot_nt(it["qe"][:, hd * dk:(hd + 1) * dk], sts[hd].astype(BF16))
            it["o_ref"][it["r0"]:it["r0"] + c, it["o_lane0"] + hd * dv:it["o_lane0"] + (hd + 1) * dv] = o
    for hd in range(it["n_heads"]):
        ks = slice(hd * dk, (hd + 1) * dk)
        st_ref[d, hd] = sts[hd] * it["dec"][:, ks] + _dot_tn(it["v"][:, hd * dv:(hd + 1) * dv], it["kt"][:, ks])


def _mix1_body(dirs, n_sub, sh_ref, sg_ref):
    c = MIX1_CHUNK
    gw = GLA_HEADS * GLA_DK
    prepared = {}
    for d, (hq_ref, hv_ref, hk_ref, hlf_ref, gv_ref, gqk_ref, gld_ref, o_ref) in enumerate(dirs):
        incl = _scan_masks(c, d)[0]
        m_incl = jnp.where(incl, 1.0, 0.0)
        for ci in range(n_sub):
            r0 = ci * c
            rs = slice(r0, r0 + c)
            prepared[d, ci, 0] = _gla_stream(
                d, None if hq_ref is None else hq_ref[rs, :], hk_ref[rs, :], hlf_ref[rs, :], hv_ref[rs, :],
                sh_ref, o_ref, 0, r0, HG_HEADS, HG_D, HG_D, incl, m_incl)
            prepared[d, ci, 1] = _gla_stream(
                d, None if hq_ref is None else gqk_ref[rs, 0:gw], gqk_ref[rs, gw:2 * gw],
                gld_ref[rs, d * gw:(d + 1) * gw], gv_ref[rs, :],
                sg_ref, o_ref, HG_HEADS * HG_D, r0, GLA_HEADS, GLA_DK, GLA_DV, incl, m_incl)
    for it in prepared.values():
        if it["want_out"]:
            _gla_intra(it)
    for step in range(n_sub):
        for d in range(2):
            for stream in range(2):
                _gla_advance(prepared[d, step if d == 0 else n_sub - 1 - step, stream])


def _mix1_ctx_kernel(*refs, n_sub):
    fwd, bwd, (sh_ref, sg_ref) = refs[0:6], refs[6:12], refs[12:14]

    @pl.when(pl.program_id(1) == 0)
    def _():
        sh_ref[...] = jnp.zeros_like(sh_ref)
        sg_ref[...] = jnp.zeros_like(sg_ref)

    _mix1_body([(None,) + tuple(r) + (None,) for r in (fwd, bwd)], n_sub, sh_ref, sg_ref)


def _mix1_lat_kernel(*refs, n_sub):
    fwd, bwd = refs[0:7], refs[7:14]
    sh0_ref, sg0_ref, of_ref, ob_ref, sh_scr, sg_scr = refs[14:20]

    @pl.when(pl.program_id(1) == 0)
    def _():
        sh_scr[...] = sh0_ref[...]
        sg_scr[...] = sg0_ref[...]

    _mix1_body([tuple(fwd) + (of_ref,), tuple(bwd) + (ob_ref,)], n_sub, sh_scr, sg_scr)


def _mix1_specs(rows, n_steps, segs_of_dir):
    specs = []
    for d in range(2):
        blk = (lambda b, s: s) if d == 0 else (lambda b, s: n_steps - 1 - s)
        for sg in segs_of_dir(d):
            specs.append(pl.BlockSpec((None, rows, SEG), lambda b, s, blk=blk, sg=sg: (b, blk(b, s), sg)))
    return specs


_SH_SHAPE = (2, HG_HEADS, HG_D, HG_D)
_SG_SHAPE = (2, GLA_HEADS, GLA_DV, GLA_DK)


def _mix1_ctx(p1c, *, rows):
    bsz, ctx_len, _ = p1c.shape
    n_steps = ctx_len // rows
    segs = lambda d: (S_HV, S_HK0 + 2 * d, S_HLF0 + 2 * d, S_GV, S_GQK, S_GLD)
    state = lambda shape: pl.BlockSpec((None,) + shape, lambda b, s: (b, 0, 0, 0, 0))
    return pl.pallas_call(
        functools.partial(_mix1_ctx_kernel, n_sub=rows // MIX1_CHUNK),
        grid=(bsz, n_steps),
        in_specs=_mix1_specs(rows, n_steps, segs),
        out_specs=[state(_SH_SHAPE), state(_SG_SHAPE)],
        out_shape=[jax.ShapeDtypeStruct((bsz,) + _SH_SHAPE, F32), jax.ShapeDtypeStruct((bsz,) + _SG_SHAPE, F32)],
        compiler_params=_params(("arbitrary", "arbitrary"), 32 << 20),
        name="l1_ctx_state",
    )(*([p1c] * 12))


def _mix1_lat(p1l, sh0, sg0, *, rows):
    bsz, seq, _ = p1l.shape
    n_steps = seq // rows
    segs = lambda d: (S_HQ, S_HV, S_HK0 + 2 * d, S_HLF0 + 2 * d, S_GV, S_GQK, S_GLD)
    state = lambda shape: pl.BlockSpec((None,) + shape, lambda b, s: (b, 0, 0, 0, 0))
    ow = HG_HEADS * HG_D + GLA_HEADS * GLA_DV
    return pl.pallas_call(
        functools.partial(_mix1_lat_kernel, n_sub=rows // MIX1_CHUNK),
        grid=(bsz, n_steps),
        in_specs=_mix1_specs(rows, n_steps, segs) + [state(_SH_SHAPE), state(_SG_SHAPE)],
        out_specs=[pl.BlockSpec((None, rows, ow), lambda b, s: (b, s, 0)),
                   pl.BlockSpec((None, rows, ow), lambda b, s: (b, n_steps - 1 - s, 0))],
        out_shape=[jax.ShapeDtypeStruct((bsz, seq, ow), F32)] * 2,
        scratch_shapes=[pltpu.VMEM(_SH_SHAPE, F32), pltpu.VMEM(_SG_SHAPE, F32)],
        compiler_params=_params(("arbitrary", "arbitrary"), 32 << 20),
        name="l1_scan",
    )(*([p1l] * 14), sh0, sg0)


def _outproj1_kernel(o0_ref, o1_ref, gate_ref, h_ref, mod_ref, cng_ref, dng_ref, w_ref, out_ref):
    parts = []
    for hd in range(HG_HEADS + GLA_HEADS):
        lo = hd * 128
        y = o0_ref[:, lo:lo + 128] + o1_ref[:, lo:lo + 128]
        ng = cng_ref[...] if hd < HG_HEADS else dng_ref[...]
        parts.append((_head_norm(y, ng) * gate_ref[:, lo:lo + 128]).astype(BF16))
    y = _dot(jnp.concatenate(parts, axis=-1), w_ref[...])
    out_ref[...] = h_ref[...] + mod_ref[2:3, :] * y


def _outproj1(o0, o1, p1l, h, modtab, cng, dng, w, *, tm):
    bsz, seq, d = h.shape
    tok = pl.BlockSpec((None, tm, d), lambda b, t: (b, t, 0))
    return pl.pallas_call(
        _outproj1_kernel,
        grid=(bsz, seq // tm),
        in_specs=[tok, tok,
                  pl.BlockSpec((None, tm, 2 * SEG), lambda b, t: (b, t, S_CG // 2)),
                  tok,
                  pl.BlockSpec((None, None, 6, d), lambda b, t: (b, 1, 0, 0)),
                  _const_spec((1, 128)), _const_spec((1, 128)), _const_spec((d, d))],
        out_specs=tok,
        out_shape=jax.ShapeDtypeStruct((bsz, seq, d), F32),
        compiler_params=_params(("arbitrary", "arbitrary"), 32 << 20),
        name="l1_outproj",
    )(o0, o1, p1l, h, modtab, cng, dng, w)


def _moe_kernel(h_ref, mod_ref, ng_ref, fg_ref, rwt_ref, rb_ref, wg_ref, wu_ref, wd_ref, out_ref,
                xn_scr, slot_scr, seg_scr, xg_scr, oa_scr, *, tk):
    e = pl.program_id(1)
    f = pl.program_id(2)
    n_f = pl.num_programs(2)
    n_e = pl.num_programs(1)
    rb = MOE_ROW_BLOCK
    gr = MOE_GATHER_ROWS

    def onehot(r0, rows, val_a, val_b):
        rid = (lax.broadcasted_iota(jnp.int32, (rows, tk), 0) + r0).astype(F32)
        return jnp.where(rid == slot_scr[0:1, :], val_a, 0.0) + jnp.where(rid == slot_scr[1:2, :], val_b, 0.0)

    def n_chunks():
        total = jnp.max(seg_scr[0, N_EXPERTS - 1:N_EXPERTS, :] + seg_scr[1, N_EXPERTS - 1:N_EXPERTS, :])
        return (total.astype(jnp.int32) + gr - 1) // gr

    @pl.when(jnp.logical_and(e == 0, f == 0))
    def _route():
        xm = _normmod(h_ref[...], ng_ref[...], mod_ref[3:4, :], mod_ref[4:5, :])
        xn_scr[...] = xm.astype(BF16)
        lg = lax.dot_general(rwt_ref[...], xm, (((1,), (1,)), ((), ())), precision=HI,
                             preferred_element_type=F32) + rb_ref[...]
        eidx = lax.broadcasted_iota(jnp.int32, lg.shape, 0).astype(F32)
        m1 = jnp.max(lg, axis=0, keepdims=True)
        i1 = jnp.min(jnp.where(lg == m1, eidx, float(N_EXPERTS)), axis=0, keepdims=True)
        lg2 = jnp.where(eidx == i1, -jnp.inf, lg)
        m2 = jnp.max(lg2, axis=0, keepdims=True)
        i2 = jnp.min(jnp.where(lg2 == m2, eidx, float(N_EXPERTS)), axis=0, keepdims=True)
        ex = jnp.exp(m2 - m1)
        p1 = 1.0 / (1.0 + ex)
        sel = jnp.where(eidx == i1, 1.0, 0.0) + jnp.where(eidx == i2, 1.0, 0.0)
        lane = lax.broadcasted_iota(jnp.int32, lg.shape, 1)
        cum = sel
        sh = 1
        while sh < tk:
            cum = cum + jnp.where(lane >= sh, pltpu.roll(cum, sh, 1), 0.0)
            sh *= 2
        padded = jnp.floor((cum[:, tk - 1:tk] + (rb - 1.0)) * (1.0 / rb)) * rb
        padded = jnp.broadcast_to(padded, (N_EXPERTS, 128))
        er = lax.broadcasted_iota(jnp.int32, (N_EXPERTS, N_EXPERTS), 0)
        ec = lax.broadcasted_iota(jnp.int32, (N_EXPERTS, N_EXPERTS), 1)
        off = _dot_hi(jnp.where(er > ec, 1.0, 0.0), padded)
        slot = off[:, 0:1] + cum - 1.0
        slot_scr[0:1, :] = jnp.sum(jnp.where(eidx == i1, slot, 0.0), axis=0, keepdims=True)
        slot_scr[1:2, :] = jnp.sum(jnp.where(eidx == i2, slot, 0.0), axis=0, keepdims=True)
        slot_scr[2:3, :] = p1
        slot_scr[3:4, :] = ex * p1
        seg_scr[0] = off
        seg_scr[1] = padded

        def gather(ci, carry):
            r0 = pl.multiple_of(ci * gr, gr)
            xg_scr[pl.ds(r0, gr), :] = _dot(onehot(r0, gr, 1.0, 1.0).astype(BF16), xn_scr[...]).astype(BF16)
            return carry
        lax.fori_loop(0, n_chunks(), gather, 0)

    off_e = jnp.max(seg_scr[0, pl.ds(e, 1), :]).astype(jnp.int32)
    n_blk = jnp.max(seg_scr[1, pl.ds(e, 1), :]).astype(jnp.int32) // rb

    def ffn(r_loc, rows):
        x = xg_scr[pl.ds(pl.multiple_of(off_e + r_loc, rb), rows), :]
        act = (_silu(_dot(x, wg_ref[...])) * _dot(x, wu_ref[...])).astype(BF16)
        part = _dot(act, wd_ref[...])

        @pl.when(f == 0)
        def _():
            oa_scr[pl.ds(r_loc, rows), :] = part

        @pl.when(f != 0)
        def _():
            oa_scr[pl.ds(r_loc, rows), :] = oa_scr[pl.ds(r_loc, rows), :] + part

    def ffn_pair(j, carry):
        ffn(pl.multiple_of(j * (2 * rb), 2 * rb), 2 * rb)
        return carry
    lax.fori_loop(0, n_blk // 2, ffn_pair, 0)

    @pl.when(n_blk % 2 == 1)
    def _():
        ffn(pl.multiple_of((n_blk - 1) * rb, rb), rb)

    @pl.when(f == n_f - 1)
    def _gate():
        def body(blk, carry):
            r_loc = pl.multiple_of(blk * rb, rb)
            r0 = pl.multiple_of(off_e + r_loc, rb)
            gcol = jnp.sum(onehot(r0, rb, slot_scr[2:3, :], slot_scr[3:4, :]), axis=1, keepdims=True)
            xg_scr[pl.ds(r0, rb), :] = (oa_scr[pl.ds(r_loc, rb), :] * gcol).astype(BF16)
            return carry
        lax.fori_loop(0, n_blk, body, 0)

    @pl.when(jnp.logical_and(e == n_e - 1, f == n_f - 1))
    def _final():
        out_ref[...] = jnp.zeros_like(out_ref)

        def scatter(ci, carry):
            r0 = pl.multiple_of(ci * gr, gr)
            out_ref[...] = out_ref[...] + _dot_tn(onehot(r0, gr, 1.0, 1.0).astype(BF16), xg_scr[pl.ds(r0, gr), :])
            return carry
        lax.fori_loop(0, n_chunks(), scatter, 0)
        h3 = h_ref[...] + mod_ref[5:6, :] * out_ref[...]
        out_ref[...] = h3 * lax.rsqrt(jnp.mean(h3 * h3, axis=-1, keepdims=True) + EPS) * fg_ref[...]


def _moe(h, modtab, ng, fg, rwt, rb, wg, wu, wd, *, tk):
    bsz, seq, d = h.shape
    tpb = seq // tk
    fw = D_FF // MOE_FF_SPLIT
    tok = lambda i, e, f: (i // tpb, i % tpb, 0)
    return pl.pallas_call(
        functools.partial(_moe_kernel, tk=tk),
        grid=(bsz * tpb, N_EXPERTS, MOE_FF_SPLIT),
        in_specs=[pl.BlockSpec((None, tk, d), tok, pipeline_mode=pl.Buffered(1)),
                  pl.BlockSpec((None, None, 6, d), lambda i, e, f: (i // tpb, 1, 0, 0)),
                  _const_spec((1, d)), _const_spec((1, d)), _const_spec((N_EXPERTS, d)), _const_spec((N_EXPERTS, 1)),
                  pl.BlockSpec((None, d, fw), lambda i, e, f: (e, 0, f)),
                  pl.BlockSpec((None, d, fw), lambda i, e, f: (e, 0, f)),
                  pl.BlockSpec((None, fw, d), lambda i, e, f: (e, f, 0))],
        out_specs=pl.BlockSpec((None, tk, d), tok),
        out_shape=jax.ShapeDtypeStruct(h.shape, F32),
        scratch_shapes=[pltpu.VMEM((tk, d), BF16), pltpu.VMEM((8, tk), F32), pltpu.VMEM((2, N_EXPERTS, 128), F32),
                        pltpu.VMEM((2 * tk + N_EXPERTS * MOE_ROW_BLOCK, d), BF16), pltpu.VMEM((tk, d), F32)],
        compiler_params=_params(("arbitrary", "arbitrary", "arbitrary"), 56 << 20),
        name="l1_moe",
    )(h, modtab, ng, fg, rwt, rb, wg, wu, wd)


def _block_diag_gate(gate_w):
    w = gate_w.reshape(2, 2, 2, 4, RG_BLOCK, RG_BLOCK)
    eye = jnp.eye(4, dtype=gate_w.dtype)
    return jnp.einsum('dghbij,bc->dghbicj', w, eye).reshape(2, 2, 2, 256, 256)


def _pad_cols(w, n):
    return jnp.pad(w, ((0, 0), (0, n - w.shape[1])))


def _layer0(h, modtab, norm_mix_g, norm_ffn_g, e_w_in, e_w_out, e_a_conv_w, e_a_conv_b, e_a_gate_w, e_a_gate_b,
            e_a_lambda, e_b_conv_w, e_b_a_log, e_b_dt_bias, e_b_norm_g, e_ffn_w_gate, e_ffn_w_up, e_ffn_w_down,
            *, tm, tt, ctx_len):
    bsz, t_all, d = h.shape
    w_in = _pad_cols(e_w_in, E_IN_PAD).astype(BF16)
    gpar = jnp.zeros((2, 128), F32)
    gpar = gpar.at[0, 2 * DN_HEADS:4 * DN_HEADS].set(e_b_a_log.reshape(-1))
    gpar = gpar.at[1, 2 * DN_HEADS:4 * DN_HEADS].set(e_b_dt_bias.reshape(-1))
    ua, gay, q, k, v, sz, gb = _inproj0(h, modtab, norm_mix_g.reshape(1, d), w_in, e_a_conv_w,
                                        e_a_conv_b.reshape(1, -1), e_b_conv_w, gpar, tm=tm, ctx_len=ctx_len)
    wg = _block_diag_gate(e_a_gate_w).astype(BF16)
    hf, hb = _rglru(jnp.transpose(ua, (1, 0, 2)), wg, e_a_gate_b.reshape(4, RG_WIDTH), e_a_lambda,
                    tt=tt, ctx_len=ctx_len)
    ha = jnp.transpose(hf + hb, (1, 0, 2))
    o0, o1 = _delta(q, k, v, gb, ctx_len=ctx_len, rows=SCAN_ROWS)
    h = _outproj0(ha, gay, o0, o1, sz, h, modtab, e_b_norm_g.reshape(1, -1), e_w_out.astype(BF16),
                  tm=tm, ctx_len=ctx_len)
    return _ffn(h, modtab, norm_ffn_g.reshape(1, d), e_ffn_w_gate.astype(BF16), e_ffn_w_up.astype(BF16),
                e_ffn_w_down.astype(BF16), tm=tm, ctx_len=ctx_len)


def _layer1(h, modtab, norm_mix_g, norm_ffn_g, final_norm_g, o_w_in, o_w_out, o_lb_logits, o_c_norm_g, o_d_gate_w2,
            o_d_gate_b2, o_d_norm_g, o_router_w, o_router_b, o_moe_w_gate, o_moe_w_up, o_moe_w_down,
            *, tm, tk, ctx_len, layer):
    bsz, t_all, d = h.shape
    seq = t_all - ctx_len
    rows = seq // GRID_W
    hl = h[:, ctx_len:, :].reshape(bsz, rows, GRID_W, d).swapaxes(1, 2).reshape(bsz, seq, d)
    w_in = _pad_cols(o_w_in, O_IN_PAD).astype(BF16)
    wlr = jnp.zeros((128, SEG), F32)
    wlr = wlr.at[0:GLA_RANK, 0:256].set(o_d_gate_w2[0]).at[GLA_RANK:2 * GLA_RANK, 256:512].set(o_d_gate_w2[1])
    proj = functools.partial(_inproj1, g=norm_mix_g.reshape(1, d), w_pad=w_in, lbl=o_lb_logits, wlr=wlr,
                             b2=o_d_gate_b2.reshape(1, SEG), tm=tm, layer=layer)
    p1c = proj(h, 0, ctx_len, modtab, 0)
    p1l = proj(hl, 0, seq, modtab, 1)
    sh0, sg0 = _mix1_ctx(p1c, rows=SCAN_ROWS)
    o0, o1 = _mix1_lat(p1l, sh0, sg0, rows=SCAN_ROWS)
    h2 = _outproj1(o0, o1, p1l, hl, modtab, o_c_norm_g.reshape(1, -1), o_d_norm_g.reshape(1, -1),
                   o_w_out.astype(BF16), tm=tm)
    return _moe(h2, modtab, norm_ffn_g.reshape(1, d), final_norm_g.reshape(1, d), o_router_w.T,
                o_router_b.reshape(N_EXPERTS, 1), o_moe_w_gate.astype(BF16), o_moe_w_up.astype(BF16),
                o_moe_w_down.astype(BF16), tk=tk)


def kernel(x, c, ctx, c_ctx, ada_w, ada_b, norm_mix_g, norm_ffn_g, final_norm_g, e_w_in, e_w_out, e_a_conv_w, e_a_conv_b, e_a_gate_w, e_a_gate_b, e_a_lambda, e_b_conv_w, e_b_a_log, e_b_dt_bias, e_b_norm_g, e_ffn_w_gate, e_ffn_w_up, e_ffn_w_down, o_w_in, o_w_out, o_lb_logits, o_c_norm_g, o_d_gate_w2, o_d_gate_b2, o_d_norm_g, o_router_w, o_router_b, o_moe_w_gate, o_moe_w_up, o_moe_w_down):
    bsz, seq, d = x.shape
    ctx_len = ctx.shape[1]
    assert bsz == 8 and d == D_MODEL and ada_w.shape[0] == 2
    tm = min(256, ctx_len)
    tt = min(128, ctx_len)
    tk = min(1024, seq)
    assert ctx_len % tm == 0 and seq % tm == 0 and ctx_len % SCAN_ROWS == 0 and seq % SCAN_ROWS == 0
    assert seq % GRID_W == 0 and seq % tk == 0

    mods = _ada(c, c_ctx, ada_w, ada_b)
    h = jnp.concatenate([ctx, x], axis=1)
    h = _layer0(h, _modtab(mods[0], bsz), norm_mix_g[0], norm_ffn_g[0], e_w_in[0], e_w_out[0], e_a_conv_w[0],
                e_a_conv_b[0], e_a_gate_w[0], e_a_gate_b[0], e_a_lambda[0], e_b_conv_w[0], e_b_a_log[0],
                e_b_dt_bias[0], e_b_norm_g[0], e_ffn_w_gate[0], e_ffn_w_up[0], e_ffn_w_down[0],
                tm=tm, tt=tt, ctx_len=ctx_len)
    out_cm = _layer1(h, _modtab(mods[1], bsz), norm_mix_g[1], norm_ffn_g[1], final_norm_g, o_w_in[0], o_w_out[0],
                     o_lb_logits, o_c_norm_g[0], o_d_gate_w2[0], o_d_gate_b2[0], o_d_norm_g[0], o_router_w[0],
                     o_router_b[0], o_moe_w_gate[0], o_moe_w_up[0], o_moe_w_down[0],
                     tm=tm, tk=tk, ctx_len=ctx_len, layer=1)
    rows = seq // GRID_W
    return out_cm.reshape(bsz, GRID_W, rows, d).swapaxes(1, 2).reshape(bsz, seq, d)
```

```python
import functools

import jax
import jax.numpy as jnp
from jax import lax
from jax.experimental import pallas as pl
from jax.experimental.pallas import tpu as pltpu

F32 = jnp.float32
BF16 = jnp.bfloat16
HI = lax.Precision.HIGHEST

EPS = 1e-6
D_MODEL = 1024
GRID_W = 64
CONV_K = 4
RG_WIDTH = 512
RG_BLOCK = 64
RG_C = 8.0
DN_HEADS = 4
DN_D = 128
DN_CHUNK = 64
HG_HEADS = 4
HG_D = 128
GLA_HEADS = 4
GLA_DK = 64
GLA_DV = 128
GLA_RANK = 16
GLA_GATE_NORM = 16.0
MIX1_CHUNK = 64
SCAN_ROWS = 256
D_FF = 2816
N_EXPERTS = 8

E_IN_PAD = 3200
O_IN_PAD = 4224
SEG = 512
N_SEG = 11
S_HQ, S_HV, S_HK0, S_HLF0, S_HK1, S_HLF1, S_GV, S_GQK, S_CG, S_DG, S_GLD = range(N_SEG)

V7X_VMEM_BYTES = 64 * 1024 * 1024
VMEM_HEADROOM_BYTES = 8 * 1024 * 1024
MOE_PIECE = 16
MOE_BLOCK = 256
MOE_GATHER_ROWS = 256


def _vmem(nbytes):
    return int(min(V7X_VMEM_BYTES - VMEM_HEADROOM_BYTES, nbytes))


def _params(sem, vmem_bytes):
    return pltpu.CompilerParams(dimension_semantics=sem, vmem_limit_bytes=_vmem(vmem_bytes))


def _sigmoid(x):
    return jax.nn.sigmoid(x)


def _sigmoid_tanh(x):
    return 0.5 * jnp.tanh(0.5 * x) + 0.5


def _silu(x):
    return x * jax.nn.sigmoid(x)


def _softplus(x):
    return jnp.maximum(x, 0.0) + jnp.log1p(jnp.exp(-jnp.abs(x)))


def _gelu_tanh(x):
    return 0.5 * x * (1.0 + jnp.tanh(0.7978845608028654 * (x + 0.044715 * (x * x * x))))


def _normmod(x, g, shift, scale):
    y = x * lax.rsqrt(jnp.mean(x * x, axis=-1, keepdims=True) + EPS)
    return (y * g) * (1.0 + scale) + shift


def _dot(a, b):
    return jnp.dot(a, b, preferred_element_type=F32)


def _dot_nt(a, b):
    return lax.dot_general(a, b, (((1,), (1,)), ((), ())), preferred_element_type=F32)


def _dot_tn(a, b):
    return lax.dot_general(a, b, (((0,), (0,)), ((), ())), preferred_element_type=F32)


def _dot_hi(a, b):
    return jnp.dot(a, b, precision=HI, preferred_element_type=F32)


def _const_spec(shape):
    nd = len(shape)
    return pl.BlockSpec(shape, lambda *_: (0,) * nd, pipeline_mode=pl.Buffered(1))


def _scan_masks(c, d):
    row = lax.broadcasted_iota(jnp.int32, (c, c), 0)
    col = lax.broadcasted_iota(jnp.int32, (c, c), 1)
    dlt = row - col if d == 0 else col - row
    return dlt >= 0, dlt > 0, dlt <= 0, row == col


def _ada_kernel(cv_ref, w_ref, b_ref, o_ref):
    s = _silu(cv_ref[...]).astype(BF16)
    o_ref[...] = _dot(s, w_ref[...].astype(BF16)) + b_ref[...]


def _ada(c, c_ctx, ada_w, ada_b):
    depth, d, n6 = ada_w.shape
    bsz = c.shape[0]
    rows = 16
    cv = jnp.zeros((rows, d), F32).at[:bsz].set(c).at[bsz].set(c_ctx)
    tn = 1536
    return pl.pallas_call(
        _ada_kernel,
        grid=(depth, n6 // tn),
        in_specs=[pl.BlockSpec((rows, d), lambda l, j: (0, 0)),
                  pl.BlockSpec((None, d, tn), lambda l, j: (l, 0, j)),
                  pl.BlockSpec((None, 1, tn), lambda l, j: (l, 0, j))],
        out_specs=pl.BlockSpec((None, rows, tn), lambda l, j: (l, 0, j)),
        out_shape=jax.ShapeDtypeStruct((depth, rows, n6), F32),
        compiler_params=_params(("arbitrary", "arbitrary"), 32 << 20),
        name="ada_mod",
    )(cv, ada_w, ada_b.reshape(depth, 1, n6))


def _modtab(mods_l, bsz):
    m = mods_l.reshape(mods_l.shape[0], 6, D_MODEL)
    lat = m[:bsz]
    ctx = jnp.broadcast_to(m[bsz][None], (bsz, 6, D_MODEL))
    return jnp.stack([ctx, lat], axis=1)


def _inproj0_kernel(hp_ref, hm_ref, hn_ref, mod_ref, g_ref, w_ref, acw_ref, acb_ref, bcw_ref, gpar_ref,
                    ua_ref, gay_ref, q_ref, k_ref, v_ref, sz_ref, gb_ref, u_scr, *, tm, ctx_tiles, n_tiles):
    t = pl.program_id(1)
    x = jnp.concatenate([hp_ref[...], hm_ref[...], hn_ref[...]], axis=0)
    xm = _normmod(x, g_ref[...], mod_ref[0:1, :], mod_ref[1:2, :]).astype(BF16)
    seg_first = jnp.logical_or(t == 0, t == ctx_tiles)
    seg_last = jnp.logical_or(t == ctx_tiles - 1, t == n_tiles - 1)

    def project(c0, width, conv_input):
        u_scr[:, c0:c0 + width] = _dot(xm, w_ref[:, c0:c0 + width])
        if conv_input:
            u_scr[0:8, c0:c0 + width] = jnp.where(seg_first, 0.0, u_scr[0:8, c0:c0 + width])
            u_scr[tm + 8:tm + 16, c0:c0 + width] = jnp.where(seg_last, 0.0, u_scr[tm + 8:tm + 16, c0:c0 + width])

    def conv(c0, width, w_ref_, w0):
        acc = u_scr[6:6 + tm, c0:c0 + width] * w_ref_[0:1, w0:w0 + width]
        for j in range(1, CONV_K):
            acc = acc + u_scr[6 + j:6 + j + tm, c0:c0 + width] * w_ref_[j:j + 1, w0:w0 + width]
        return acc

    project(0, RG_WIDTH, True)
    for grp in range(RG_WIDTH // 128):
        c0 = grp * 128
        ua_ref[:, c0:c0 + 128] = conv(c0, 128, acw_ref, c0) + acb_ref[0:1, c0:c0 + 128]
    project(512, 512, False)
    gay_ref[...] = _gelu_tanh(u_scr[8:8 + tm, 512:1024])

    for grp in range(3 * DN_HEADS):
        c0 = grp * 128
        if grp % DN_HEADS == 0:
            project(1024 + c0, DN_HEADS * DN_D, True)
        y = _silu(conv(1024 + c0, 128, bcw_ref, c0))
        if grp < 2 * DN_HEADS:
            y = y * lax.rsqrt(jnp.sum(y * y, axis=-1, keepdims=True) + EPS)
        if grp < DN_HEADS:
            q_ref[:, c0:c0 + 128] = y * (DN_D ** -0.5)
        elif grp < 2 * DN_HEADS:
            k_ref[:, c0 - 512:c0 - 384] = y
        else:
            v_ref[:, c0 - 1024:c0 - 896] = y
    project(2560, 512, False)
    sz_ref[...] = _silu(u_scr[8:8 + tm, 2560:3072])

    project(3072, 128, False)
    xg = u_scr[8:8 + tm, 3072:3200]
    lane = lax.broadcasted_iota(jnp.int32, xg.shape, 1)
    g = -jnp.exp(gpar_ref[0:1, :]) * _softplus(xg + gpar_ref[1:2, :])
    gb_ref[...] = jnp.where(lane < 2 * DN_HEADS, _sigmoid(xg), g)


def _inproj0(h, modtab, g, w_pad, acw, acb, bcw, gpar, *, tm, ctx_len):
    bsz, t_all, d = h.shape
    n_tiles = t_all // tm
    ctx_tiles = ctx_len // tm
    tb = tm // 8
    kern = functools.partial(_inproj0_kernel, tm=tm, ctx_tiles=ctx_tiles, n_tiles=n_tiles)
    tok = lambda w: jax.ShapeDtypeStruct((bsz, t_all, w), F32)
    tok_spec = lambda w: pl.BlockSpec((None, tm, w), lambda b, t: (b, t, 0))
    return pl.pallas_call(
        kern,
        grid=(bsz, n_tiles),
        in_specs=[
            pl.BlockSpec((None, 8, d), lambda b, t: (b, jnp.maximum(t * tb - 1, 0), 0)),
            pl.BlockSpec((None, tm, d), lambda b, t: (b, t, 0)),
            pl.BlockSpec((None, 8, d), lambda b, t: (b, jnp.minimum((t + 1) * tb, t_all // 8 - 1), 0)),
            pl.BlockSpec((None, None, 6, d), lambda b, t: (b, jnp.where(t >= ctx_tiles, 1, 0), 0, 0)),
            _const_spec((1, d)),
            _const_spec((d, E_IN_PAD)),
            _const_spec((CONV_K, RG_WIDTH)),
            _const_spec((1, RG_WIDTH)),
            _const_spec((CONV_K, 3 * DN_HEADS * DN_D)),
            _const_spec((2, 128)),
        ],
        out_specs=[tok_spec(512), tok_spec(512), tok_spec(512), tok_spec(512), tok_spec(512), tok_spec(512),
                   tok_spec(128)],
        out_shape=[tok(512), tok(512), tok(512), tok(512), tok(512), tok(512), tok(128)],
        scratch_shapes=[pltpu.VMEM((tm + 16, E_IN_PAD), F32)],
        compiler_params=_params(("arbitrary", "arbitrary"), 40 << 20),
        name="l0_inproj",
    )(h, h, h, modtab, g, w_pad, acw, acb, bcw, gpar)


def _rglru_kernel(uf_ref, ub_ref, wg_ref, gbias_ref, lam_ref, hf_ref, hb_ref,
                  af_scr, xf_scr, ab_scr, xb_scr, h_scr, *, tt, bsz):
    s = pl.program_id(0)

    @pl.when(s == 0)
    def _():
        h_scr[...] = jnp.zeros_like(h_scr)

    def gates(u_ref, d, a_scr, x_scr):
        x = u_ref[...].reshape(tt * bsz, RG_WIDTH)
        xb = x.astype(BF16)
        for half in range(2):
            c0 = half * 256
            xh = xb[:, c0:c0 + 256]
            r = _sigmoid_tanh(_dot(xh, wg_ref[d, 0, half]) + gbias_ref[2 * d:2 * d + 1, c0:c0 + 256])
            i = _sigmoid_tanh(_dot(xh, wg_ref[d, 1, half]) + gbias_ref[2 * d + 1:2 * d + 2, c0:c0 + 256])
            log_a = (-RG_C) * r * _softplus(-lam_ref[d:d + 1, c0:c0 + 256])
            a = jnp.exp(log_a)
            mult = jnp.sqrt(-jnp.tanh(log_a) * (a * a + 1.0))
            xin = mult * (i * x[:, c0:c0 + 256])
            a_scr[:, :, c0:c0 + 256] = a.reshape(tt, bsz, 256)
            x_scr[:, :, c0:c0 + 256] = xin.reshape(tt, bsz, 256)

    gates(uf_ref, 0, af_scr, xf_scr)
    gates(ub_ref, 1, ab_scr, xb_scr)

    def step(t, carry):
        hf, hb = carry
        hf = af_scr[t] * hf + xf_scr[t]
        hf_ref[t] = hf
        tb = tt - 1 - t
        hb = ab_scr[tb] * hb + xb_scr[tb]
        hb_ref[tb] = hb
        return hf, hb

    hf, hb = lax.fori_loop(0, tt, step, (h_scr[0], h_scr[1]), unroll=8)
    h_scr[0] = hf
    h_scr[1] = hb


def _rglru(ua3, wg, gbias, lam, *, tt, ctx_len):
    t_all, bsz, w = ua3.shape
    n_steps = t_all // tt
    nc = ctx_len // tt

    def bwd(s):
        return jnp.where(s < nc, nc - 1 - s, n_steps + nc - 1 - s)

    blk = (tt, bsz, w)
    kern = functools.partial(_rglru_kernel, tt=tt, bsz=bsz)
    return pl.pallas_call(
        kern,
        grid=(n_steps,),
        in_specs=[pl.BlockSpec(blk, lambda s: (s, 0, 0)),
                  pl.BlockSpec(blk, lambda s: (bwd(s), 0, 0)),
                  _const_spec(wg.shape), _const_spec(gbias.shape), _const_spec(lam.shape)],
        out_specs=[pl.BlockSpec(blk, lambda s: (s, 0, 0)),
                   pl.BlockSpec(blk, lambda s: (bwd(s), 0, 0))],
        out_shape=[jax.ShapeDtypeStruct(ua3.shape, F32)] * 2,
        scratch_shapes=[pltpu.VMEM(blk, F32)] * 4 + [pltpu.VMEM((2, bsz, w), F32)],
        compiler_params=_params(("arbitrary",), 40 << 20),
        name="l0_rglru",
    )(ua3, ua3, wg, gbias, lam)


def _delta_kernel(qf_ref, kf_ref, vf_ref, gf_ref, qb_ref, kb_ref, vb_ref, gb_ref, of_ref, ob_ref, s_scr, *, n_sub):
    c = DN_CHUNK

    @pl.when(pl.program_id(1) == 0)
    def _():
        s_scr[...] = jnp.zeros_like(s_scr)

    dir_refs = ((qf_ref, kf_ref, vf_ref, gf_ref, of_ref), (qb_ref, kb_ref, vb_ref, gb_ref, ob_ref))
    masks = [_scan_masks(c, d) for d in range(2)]
    eye = jnp.where(masks[0][3], 1.0, 0.0)

    cums = {}
    for d in range(2):
        incl, _, incl_t, _ = masks[d]
        m_incl = jnp.where(incl, 1.0, 0.0)
        m_incl_t = jnp.where(incl_t, 1.0, 0.0)
        for ci in range(n_sub):
            g_all = dir_refs[d][3][ci * c:(ci + 1) * c, :]
            gc_all = _dot_hi(m_incl, g_all)
            gct_all = lax.dot_general(g_all, m_incl_t, (((0,), (0,)), ((), ())), precision=HI,
                                      preferred_element_type=F32)
            cums[d, ci] = (g_all, gc_all, gct_all)

    chains = []
    for d in range(2):
        q_ref, k_ref, v_ref, _, _ = dir_refs[d]
        incl, strict, _, _ = masks[d]
        last = c - 1 if d == 0 else 0
        for ci in range(n_sub):
            g_all, gc_all, gct_all = cums[d, ci]
            rs = slice(ci * c, (ci + 1) * c)
            for h in range(DN_HEADS):
                hs = slice(h * DN_D, (h + 1) * DN_D)
                lane = 2 * DN_HEADS + d * DN_HEADS + h
                ch = dict(d=d, ci=ci, h=h, rs=rs, hs=hs, incl=incl, strict=strict)
                ch["beta"] = g_all[:, d * DN_HEADS + h:d * DN_HEADS + h + 1]
                gc = jnp.broadcast_to(gc_all[:, lane:lane + 1], (c, DN_D))
                gc_row = jnp.broadcast_to(gct_all[lane:lane + 1, :], (c, c))
                ch["gc"] = gc
                ch["gtot"] = gc[last:last + 1, :]
                ch["decay"] = jnp.where(incl, jnp.exp(jnp.minimum(gc[:, 0:c] - gc_row, 0.0)), 0.0)
                ch["e_gc"] = jnp.exp(gc)
                ch["q"] = q_ref[rs, hs]
                ch["k"] = k_ref[rs, hs]
                ch["v"] = v_ref[rs, hs]
                chains.append(ch)

    for ch in chains:
        ch["kb"] = ch["k"] * ch["beta"]
        qk = _dot_nt(jnp.concatenate([ch["kb"], ch["q"]], axis=0).astype(BF16), ch["k"].astype(BF16))
        ch["neg"] = -jnp.where(ch["strict"], qk[0:c] * ch["decay"], 0.0)
        ch["a_qk"] = (qk[c:2 * c] * ch["decay"]).astype(BF16)
    for ch in chains:
        negb = ch["neg"].astype(BF16)
        ch["t"] = eye + ch["neg"]
        ch["p"] = _dot(negb, negb)
    n_sq = max(1, (c - 1).bit_length() - 1)
    for it in range(n_sq):
        for ch in chains:
            tp = _dot(jnp.concatenate([ch["t"], ch["p"]], axis=0).astype(BF16), ch["p"].astype(BF16))
            ch["t"] = ch["t"] + tp[0:c]
            ch["p"] = tp[c:2 * c]
    for ch in chains:
        rhs = jnp.concatenate([ch["v"] * ch["beta"], ch["kb"] * ch["e_gc"]], axis=1).astype(BF16)
        sol = _dot(ch["t"].astype(BF16), rhs)
        ch["u"] = sol[:, 0:DN_D]
        ch["wq"] = jnp.concatenate([sol[:, DN_D:2 * DN_D], ch["q"] * ch["e_gc"]], axis=0).astype(BF16)
        ch["k_tail"] = (ch["k"] * jnp.exp(ch["gtot"] - ch["gc"])).astype(BF16)

    by_key = {(ch["d"], ch["ci"], ch["h"]): ch for ch in chains}
    for step in range(n_sub):
        live = [by_key[d, step if d == 0 else n_sub - 1 - step, h] for d in range(2) for h in range(DN_HEADS)]
        for ch in live:
            ch["st"] = s_scr[ch["d"], ch["h"]]
            ch["ws"] = _dot(ch["wq"], ch["st"].astype(BF16))
        for ch in live:
            vnb = (ch["u"] - ch["ws"][0:c]).astype(BF16)
            o = ch["ws"][c:2 * c] + _dot(ch["a_qk"], vnb)
            dir_refs[ch["d"]][4][ch["rs"], ch["hs"]] = o
            s_scr[ch["d"], ch["h"]] = ch["st"] * jnp.exp(ch["gtot"]) + _dot_tn(ch["k_tail"], vnb)


def _delta(q, k, v, gb, *, ctx_len, rows):
    bsz, t_all, w = q.shape
    n_steps = t_all // rows
    nc = ctx_len // rows

    def bwd(s):
        return jnp.where(s < nc, nc - 1 - s, n_steps + nc - 1 - s)

    fwd_spec = lambda width: pl.BlockSpec((None, rows, width), lambda b, s: (b, s, 0))
    bwd_spec = lambda width: pl.BlockSpec((None, rows, width), lambda b, s: (b, bwd(s), 0))
    return pl.pallas_call(
        functools.partial(_delta_kernel, n_sub=rows // DN_CHUNK),
        grid=(bsz, n_steps),
        in_specs=[fwd_spec(w), fwd_spec(w), fwd_spec(w), fwd_spec(128),
                  bwd_spec(w), bwd_spec(w), bwd_spec(w), bwd_spec(128)],
        out_specs=[fwd_spec(w), bwd_spec(w)],
        out_shape=[jax.ShapeDtypeStruct((bsz, t_all, w), F32)] * 2,
        scratch_shapes=[pltpu.VMEM((2, DN_HEADS, DN_D, DN_D), F32)],
        compiler_params=_params(("arbitrary", "arbitrary"), 32 << 20),
        name="l0_deltanet",
    )(q, k, v, gb, q, k, v, gb)


def _head_norm(y, g):
    return y * lax.rsqrt(jnp.mean(y * y, axis=-1, keepdims=True) + EPS) * g


def _outproj0_kernel(ha_ref, gay_ref, o0_ref, o1_ref, sz_ref, h_ref, mod_ref, ng_ref, w_ref, out_ref):
    parts = [(ha_ref[...] * gay_ref[...]).astype(BF16)]
    for hd in range(DN_HEADS):
        lo = hd * DN_D
        ob = o0_ref[:, lo:lo + DN_D] + o1_ref[:, lo:lo + DN_D]
        parts.append((_head_norm(ob, ng_ref[...]) * sz_ref[:, lo:lo + DN_D]).astype(BF16))
    y = _dot(jnp.concatenate(parts, axis=-1), w_ref[...])
    out_ref[...] = h_ref[...] + mod_ref[2:3, :] * y


def _outproj0(ha, gay, o0, o1, sz, h, modtab, ng, w, *, tm, ctx_len):
    bsz, t_all, d = h.shape
    ctx_tiles = ctx_len // tm
    tok = lambda width: pl.BlockSpec((None, tm, width), lambda b, t: (b, t, 0))
    return pl.pallas_call(
        _outproj0_kernel,
        grid=(bsz, t_all // tm),
        in_specs=[tok(512), tok(512), tok(512), tok(512), tok(512), tok(d),
                  pl.BlockSpec((None, None, 6, d), lambda b, t: (b, jnp.where(t >= ctx_tiles, 1, 0), 0, 0)),
                  _const_spec((1, DN_D)), _const_spec((d, d))],
        out_specs=tok(d),
        out_shape=jax.ShapeDtypeStruct(h.shape, F32),
        compiler_params=_params(("arbitrary", "arbitrary"), 32 << 20),
        name="l0_outproj",
    )(ha, gay, o0, o1, sz, h, modtab, ng, w)


def _ffn_kernel(h_ref, mod_ref, g_ref, wg_ref, wu_ref, wd_ref, out_ref, *, n_chunks):
    x = h_ref[...]
    xm = _normmod(x, g_ref[...], mod_ref[3:4, :], mod_ref[4:5, :]).astype(BF16)
    cw = D_FF // n_chunks
    acc = jnp.zeros(x.shape, F32)
    for ci in range(n_chunks):
        c0 = ci * cw
        gate = _dot(xm, wg_ref[:, c0:c0 + cw])
        up = _dot(xm, wu_ref[:, c0:c0 + cw])
        acc = acc + _dot((_silu(gate) * up).astype(BF16), wd_ref[c0:c0 + cw, :])
    out_ref[...] = x + mod_ref[5:6, :] * acc


def _ffn(h, modtab, g, wg, wu, wd, *, tm, ctx_len):
    bsz, t_all, d = h.shape
    ctx_tiles = ctx_len // tm
    tok = pl.BlockSpec((None, tm, d), lambda b, t: (b, t, 0))
    return pl.pallas_call(
        functools.partial(_ffn_kernel, n_chunks=2),
        grid=(bsz, t_all // tm),
        in_specs=[tok,
                  pl.BlockSpec((None, None, 6, d), lambda b, t: (b, jnp.where(t >= ctx_tiles, 1, 0), 0, 0)),
                  _const_spec((1, d)), _const_spec((d, D_FF)), _const_spec((d, D_FF)), _const_spec((D_FF, d))],
        out_specs=tok,
        out_shape=jax.ShapeDtypeStruct(h.shape, F32),
        compiler_params=_params(("arbitrary", "arbitrary"), 44 << 20),
        name="l0_ffn",
    )(h, modtab, g, wg, wu, wd)


def _inproj1_kernel(h_ref, mod_ref, g_ref, w_ref, lbl_ref, wlr_ref, b2_ref, p_ref, u_scr, *, layer):
    x = h_ref[...]
    xm = _normmod(x, g_ref[...], mod_ref[0:1, :], mod_ref[1:2, :]).astype(BF16)

    lg = lbl_ref[...]
    ex = jnp.exp(lg - jnp.max(lg, axis=0, keepdims=True))
    lbw = ex / jnp.sum(ex, axis=0, keepdims=True)
    lb = jnp.sum(lbw[1:layer + 1], axis=0, keepdims=True)

    def project(c0, width):
        u_scr[:, c0:c0 + width] = _dot(xm, w_ref[:, c0:c0 + width])

    def put(seg, off, val):
        p_ref[:, seg * SEG + off:seg * SEG + off + val.shape[1]] = val

    def groups(fn):
        for grp in range(SEG // 128):
            fn(grp * 128)

    project(0, SEG)
    groups(lambda c0: put(S_HQ, c0, _silu(u_scr[:, c0:c0 + 128]) * (HG_D ** -0.5)))
    for dr, (sk, sf) in enumerate(((S_HK0, S_HLF0), (S_HK1, S_HLF1))):
        project(512 + dr * 512, SEG)

        def forget(c0, dr=dr, sk=sk, sf=sf):
            lbg = lb[:, c0:c0 + 128]
            fl = u_scr[:, 512 + dr * 512 + c0:512 + dr * 512 + c0 + 128]
            put(sf, c0, jnp.log(lbg + (1.0 - lbg) * _sigmoid(fl)))
            put(sk, c0, (1.0 - lbg) * _sigmoid(-fl))
        groups(forget)
    project(1536, SEG)
    groups(lambda c0: put(S_HV, c0, u_scr[:, 1536 + c0:1536 + c0 + 128]))
    project(2048, SEG)
    groups(lambda c0: put(S_CG, c0, _silu(u_scr[:, 2048 + c0:2048 + c0 + 128])))
    project(2560, SEG)
    put(S_GQK, 0, u_scr[:, 2560:2816] * (GLA_DK ** -0.5))
    put(S_GQK, 256, u_scr[:, 2816:3072])
    project(3072, SEG)
    groups(lambda c0: put(S_GV, c0, u_scr[:, 3072 + c0:3072 + c0 + 128]))
    project(3584, SEG)
    groups(lambda c0: put(S_DG, c0, _silu(u_scr[:, 3584 + c0:3584 + c0 + 128])))
    project(4096, 128)
    lr = u_scr[:, 4096:4224]
    put(S_GLD, 0, -_softplus(-(_dot_hi(lr, wlr_ref[...]) + b2_ref[...])) * (1.0 / GLA_GATE_NORM))


def _inproj1(h, row0, n_rows, modtab, seg, g, w_pad, lbl, wlr, b2, *, tm, layer):
    bsz, _, d = h.shape
    t0 = row0 // tm
    f1 = N_SEG * SEG
    return pl.pallas_call(
        functools.partial(_inproj1_kernel, layer=layer),
        grid=(bsz, n_rows // tm),
        in_specs=[pl.BlockSpec((None, tm, d), lambda b, t: (b, t0 + t, 0)),
                  pl.BlockSpec((None, None, 6, d), lambda b, t: (b, seg, 0, 0)),
                  _const_spec((1, d)), _const_spec((d, O_IN_PAD)), _const_spec(lbl.shape),
                  _const_spec((128, SEG)), _const_spec((1, SEG))],
        out_specs=pl.BlockSpec((None, tm, f1), lambda b, t: (b, t, 0)),
        out_shape=jax.ShapeDtypeStruct((bsz, n_rows, f1), F32),
        scratch_shapes=[pltpu.VMEM((tm, O_IN_PAD), F32)],
        compiler_params=_params(("arbitrary", "arbitrary"), 44 << 20),
        name="l1_inproj",
    )(h, modtab, g, w_pad, lbl, wlr, b2)


def _gla_stream(d, q_all, k_all, ld_all, v_all, st_ref, o_ref, o_lane0, r0, n_heads, dk, dv, incl, m_incl):
    c = k_all.shape[0]
    mid = c // 2 - 1 if d == 0 else c // 2
    last = c - 1 if d == 0 else 0
    bc = _dot_hi(m_incl, ld_all)
    m = bc[mid:mid + 1]
    btot = bc[last:last + 1]
    kn = k_all * jnp.exp(m - bc)
    it = dict(d=d, r0=r0, st_ref=st_ref, o_ref=o_ref, o_lane0=o_lane0, n_heads=n_heads, dk=dk, dv=dv, incl=incl,
              c=c, kt=(kn * jnp.exp(btot - m)).astype(BF16), dec=jnp.exp(btot), v=v_all.astype(BF16),
              want_out=q_all is not None)
    if q_all is not None:
        qe = q_all * jnp.exp(bc)
        it.update(qd=(qe * jnp.exp(-m)).astype(BF16), qe=qe.astype(BF16), knb=kn.astype(BF16))
    return it


def _gla_intra(it):
    dk = it["dk"]
    it["a"] = [jnp.where(it["incl"], _dot_nt(it["qd"][:, hd * dk:(hd + 1) * dk], it["knb"][:, hd * dk:(hd + 1) * dk]),
                         0.0).astype(BF16) for hd in range(it["n_heads"])]


def _gla_advance(it):
    d, dk, dv, c, st_ref = it["d"], it["dk"], it["dv"], it["c"], it["st_ref"]
    sts = [st_ref[d, hd] for hd in range(it["n_heads"])]
    if it["want_out"]:
        for hd in range(it["n_heads"]):
            v = it["v"][:, hd * dv:(hd + 1) * dv]
            o = _dot(it["a"][hd], v) + _dot_nt(it["qe"][:, hd * dk:(hd + 1) * dk], sts[hd].astype(BF16))
            it["o_ref"][it["r0"]:it["r0"] + c, it["o_lane0"] + hd * dv:it["o_lane0"] + (hd + 1) * dv] = o
    for hd in range(it["n_heads"]):
        ks = slice(hd * dk, (hd + 1) * dk)
        st_ref[d, hd] = sts[hd] * it["dec"][:, ks] + _dot_tn(it["v"][:, hd * dv:(hd + 1) * dv], it["kt"][:, ks])


def _mix1_body(dirs, n_sub, sh_ref, sg_ref):
    c = MIX1_CHUNK
    gw = GLA_HEADS * GLA_DK
    prepared = {}
    for d, (hq_ref, hv_ref, hk_ref, hlf_ref, gv_ref, gqk_ref, gld_ref, o_ref) in enumerate(dirs):
        incl = _scan_masks(c, d)[0]
        m_incl = jnp.where(incl, 1.0, 0.0)
        for ci in range(n_sub):
            r0 = ci * c
            rs = slice(r0, r0 + c)
            prepared[d, ci, 0] = _gla_stream(
                d, None if hq_ref is None else hq_ref[rs, :], hk_ref[rs, :], hlf_ref[rs, :], hv_ref[rs, :],
                sh_ref, o_ref, 0, r0, HG_HEADS, HG_D, HG_D, incl, m_incl)
            prepared[d, ci, 1] = _gla_stream(
                d, None if hq_ref is None else gqk_ref[rs, 0:gw], gqk_ref[rs, gw:2 * gw],
                gld_ref[rs, d * gw:(d + 1) * gw], gv_ref[rs, :],
                sg_ref, o_ref, HG_HEADS * HG_D, r0, GLA_HEADS, GLA_DK, GLA_DV, incl, m_incl)
    for it in prepared.values():
        if it["want_out"]:
            _gla_intra(it)
    for step in range(n_sub):
        for d in range(2):
            for stream in range(2):
                _gla_advance(prepared[d, step if d == 0 else n_sub - 1 - step, stream])


def _mix1_ctx_kernel(*refs, n_sub):
    fwd, bwd, (sh_ref, sg_ref) = refs[0:6], refs[6:12], refs[12:14]

    @pl.when(pl.program_id(1) == 0)
    def _():
        sh_ref[...] = jnp.zeros_like(sh_ref)
        sg_ref[...] = jnp.zeros_like(sg_ref)

    _mix1_body([(None,) + tuple(r) + (None,) for r in (fwd, bwd)], n_sub, sh_ref, sg_ref)


def _mix1_lat_kernel(*refs, n_sub):
    fwd, bwd = refs[0:7], refs[7:14]
    sh0_ref, sg0_ref, of_ref, ob_ref, sh_scr, sg_scr = refs[14:20]

    @pl.when(pl.program_id(1) == 0)
    def _():
        sh_scr[...] = sh0_ref[...]
        sg_scr[...] = sg0_ref[...]

    _mix1_body([tuple(fwd) + (of_ref,), tuple(bwd) + (ob_ref,)], n_sub, sh_scr, sg_scr)


def _mix1_specs(rows, n_steps, segs_of_dir):
    specs = []
    for d in range(2):
        blk = (lambda b, s: s) if d == 0 else (lambda b, s: n_steps - 1 - s)
        for sg in segs_of_dir(d):
            specs.append(pl.BlockSpec((None, rows, SEG), lambda b, s, blk=blk, sg=sg: (b, blk(b, s), sg)))
    return specs


_SH_SHAPE = (2, HG_HEADS, HG_D, HG_D)
_SG_SHAPE = (2, GLA_HEADS, GLA_DV, GLA_DK)


def _mix1_ctx(p1c, *, rows):
    bsz, ctx_len, _ = p1c.shape
    n_steps = ctx_len // rows
    segs = lambda d: (S_HV, S_HK0 + 2 * d, S_HLF0 + 2 * d, S_GV, S_GQK, S_GLD)
    state = lambda shape: pl.BlockSpec((None,) + shape, lambda b, s: (b, 0, 0, 0, 0))
    return pl.pallas_call(
        functools.partial(_mix1_ctx_kernel, n_sub=rows // MIX1_CHUNK),
        grid=(bsz, n_steps),
        in_specs=_mix1_specs(rows, n_steps, segs),
        out_specs=[state(_SH_SHAPE), state(_SG_SHAPE)],
        out_shape=[jax.ShapeDtypeStruct((bsz,) + _SH_SHAPE, F32), jax.ShapeDtypeStruct((bsz,) + _SG_SHAPE, F32)],
        compiler_params=_params(("arbitrary", "arbitrary"), 32 << 20),
        name="l1_ctx_state",
    )(*([p1c] * 12))


def _mix1_lat(p1l, sh0, sg0, *, rows):
    bsz, seq, _ = p1l.shape
    n_steps = seq // rows
    segs = lambda d: (S_HQ, S_HV, S_HK0 + 2 * d, S_HLF0 + 2 * d, S_GV, S_GQK, S_GLD)
    state = lambda shape: pl.BlockSpec((None,) + shape, lambda b, s: (b, 0, 0, 0, 0))
    ow = HG_HEADS * HG_D + GLA_HEADS * GLA_DV
    return pl.pallas_call(
        functools.partial(_mix1_lat_kernel, n_sub=rows // MIX1_CHUNK),
        grid=(bsz, n_steps),
        in_specs=_mix1_specs(rows, n_steps, segs) + [state(_SH_SHAPE), state(_SG_SHAPE)],
        out_specs=[pl.BlockSpec((None, rows, ow), lambda b, s: (b, s, 0)),
                   pl.BlockSpec((None, rows, ow), lambda b, s: (b, n_steps - 1 - s, 0))],
        out_shape=[jax.ShapeDtypeStruct((bsz, seq, ow), F32)] * 2,
        scratch_shapes=[pltpu.VMEM(_SH_SHAPE, F32), pltpu.VMEM(_SG_SHAPE, F32)],
        compiler_params=_params(("arbitrary", "arbitrary"), 32 << 20),
        name="l1_scan",
    )(*([p1l] * 14), sh0, sg0)


def _outproj1_kernel(o0_ref, o1_ref, gate_ref, h_ref, mod_ref, cng_ref, dng_ref, w_ref, out_ref):
    parts = []
    for hd in range(HG_HEADS + GLA_HEADS):
        lo = hd * 128
        y = o0_ref[:, lo:lo + 128] + o1_ref[:, lo:lo + 128]
        ng = cng_ref[...] if hd < HG_HEADS else dng_ref[...]
        parts.append((_head_norm(y, ng) * gate_ref[:, lo:lo + 128]).astype(BF16))
    y = _dot(jnp.concatenate(parts, axis=-1), w_ref[...])
    out_ref[...] = h_ref[...] + mod_ref[2:3, :] * y


def _outproj1(o0, o1, p1l, h, modtab, cng, dng, w, *, tm):
    bsz, seq, d = h.shape
    tok = pl.BlockSpec((None, tm, d), lambda b, t: (b, t, 0))
    return pl.pallas_call(
        _outproj1_kernel,
        grid=(bsz, seq // tm),
        in_specs=[tok, tok,
                  pl.BlockSpec((None, tm, 2 * SEG), lambda b, t: (b, t, S_CG // 2)),
                  tok,
                  pl.BlockSpec((None, None, 6, d), lambda b, t: (b, 1, 0, 0)),
                  _const_spec((1, 128)), _const_spec((1, 128)), _const_spec((d, d))],
        out_specs=tok,
        out_shape=jax.ShapeDtypeStruct((bsz, seq, d), F32),
        compiler_params=_params(("arbitrary", "arbitrary"), 32 << 20),
        name="l1_outproj",
    )(o0, o1, p1l, h, modtab, cng, dng, w)


def _moe_route_kernel(h_ref, mod_ref, ng_ref, rwt_ref, rb_ref, slot_ref, seg_ref, xs_hbm,
                      xn_scr, xg_scr, zero_scr, base_smem, sem, *, tk):
    i = pl.program_id(0)
    pc = MOE_PIECE
    gr = MOE_GATHER_ROWS

    @pl.when(i == 0)
    def _():
        for e in range(N_EXPERTS):
            base_smem[e] = 0
        zero_scr[...] = jnp.zeros_like(zero_scr)

    xm = _normmod(h_ref[...], ng_ref[...], mod_ref[3:4, :], mod_ref[4:5, :])
    xn_scr[...] = xm.astype(BF16)
    lg = lax.dot_general(rwt_ref[...], xm, (((1,), (1,)), ((), ())), precision=HI,
                         preferred_element_type=F32) + rb_ref[...]
    eidx = lax.broadcasted_iota(jnp.int32, lg.shape, 0).astype(F32)
    m1 = jnp.max(lg, axis=0, keepdims=True)
    i1 = jnp.min(jnp.where(lg == m1, eidx, float(N_EXPERTS)), axis=0, keepdims=True)
    lg2 = jnp.where(eidx == i1, -jnp.inf, lg)
    m2 = jnp.max(lg2, axis=0, keepdims=True)
    i2 = jnp.min(jnp.where(lg2 == m2, eidx, float(N_EXPERTS)), axis=0, keepdims=True)
    ex = jnp.exp(m2 - m1)
    p1 = 1.0 / (1.0 + ex)
    sel = jnp.where(eidx == i1, 1.0, 0.0) + jnp.where(eidx == i2, 1.0, 0.0)
    lane = lax.broadcasted_iota(jnp.int32, lg.shape, 1)
    cum = sel
    sh = 1
    while sh < tk:
        cum = cum + jnp.where(lane >= sh, pltpu.roll(cum, sh, 1), 0.0)
        sh *= 2
    padded = jnp.floor((cum[:, tk - 1:tk] + (pc - 1.0)) * (1.0 / pc)) * pc
    padded = jnp.broadcast_to(padded, (N_EXPERTS, 128))
    er = lax.broadcasted_iota(jnp.int32, (N_EXPERTS, N_EXPERTS), 0)
    ec = lax.broadcasted_iota(jnp.int32, (N_EXPERTS, N_EXPERTS), 1)
    off = _dot_hi(jnp.where(er > ec, 1.0, 0.0), padded)
    slot = off[:, 0:1] + cum - 1.0
    slot_a = jnp.sum(jnp.where(eidx == i1, slot, 0.0), axis=0, keepdims=True)
    slot_b = jnp.sum(jnp.where(eidx == i2, slot, 0.0), axis=0, keepdims=True)
    slot_ref[...] = jnp.concatenate([slot_a, slot_b, p1, ex * p1, jnp.zeros((4, tk), F32)], axis=0)

    total = jnp.max(off[N_EXPERTS - 1:N_EXPERTS, :] + padded[N_EXPERTS - 1:N_EXPERTS, :]).astype(jnp.int32)

    def gather(ci, carry):
        r0 = pl.multiple_of(ci * gr, gr)
        rid = (lax.broadcasted_iota(jnp.int32, (gr, tk), 0) + r0).astype(F32)
        p = jnp.where(rid == slot_a, 1.0, 0.0) + jnp.where(rid == slot_b, 1.0, 0.0)
        xg_scr[pl.ds(r0, gr), :] = _dot(p.astype(BF16), xn_scr[...]).astype(BF16)
        return carry
    lax.fori_loop(0, (total + gr - 1) // gr, gather, 0)

    erow = lax.broadcasted_iota(jnp.int32, (N_EXPERTS, 128), 0)
    base_vec = jnp.zeros((N_EXPERTS, 128), F32)
    segs = []
    for e in range(N_EXPERTS):
        off_e = jnp.max(off[e:e + 1, :]).astype(jnp.int32)
        len_e = jnp.max(padded[e:e + 1, :]).astype(jnp.int32)
        base_e = base_smem[e]
        base_vec = jnp.where(erow == e, base_e.astype(F32), base_vec)
        segs.append((e, off_e, len_e, base_e))
    seg_ref[0] = off
    seg_ref[1] = padded
    seg_ref[2] = base_vec

    def seg_copy(e, off_e, base_e, p):
        return pltpu.make_async_copy(
            xg_scr.at[pl.ds(pl.multiple_of(off_e + p * pc, pc), pc), :],
            xs_hbm.at[e, pl.ds(pl.multiple_of(base_e + p * pc, pc), pc), :], sem)

    for e, off_e, len_e, base_e in segs:
        def start(p, carry, e=e, off_e=off_e, base_e=base_e):
            seg_copy(e, off_e, base_e, p).start()
            return carry
        lax.fori_loop(0, len_e // pc, start, 0)
    for e, off_e, len_e, base_e in segs:
        def wait(p, carry, e=e, off_e=off_e, base_e=base_e):
            seg_copy(e, off_e, base_e, p).wait()
            return carry
        lax.fori_loop(0, len_e // pc, wait, 0)
        base_smem[e] = base_e + len_e

    @pl.when(i == pl.num_programs(0) - 1)
    def _():
        def tail_copy(e, p):
            end = base_smem[e]
            return pltpu.make_async_copy(zero_scr, xs_hbm.at[e, pl.ds(pl.multiple_of(end + p * pc, pc), pc), :], sem)

        def n_tail(e):
            rem = lax.rem(base_smem[e], MOE_BLOCK)
            return jnp.where(rem == 0, 0, MOE_BLOCK - rem) // pc

        for e in range(N_EXPERTS):
            def start(p, carry, e=e):
                tail_copy(e, p).start()
                return carry
            lax.fori_loop(0, n_tail(e), start, 0)
        for e in range(N_EXPERTS):
            def wait(p, carry, e=e):
                tail_copy(e, p).wait()
                return carry
            lax.fori_loop(0, n_tail(e), wait, 0)


def _moe_ffn_kernel(eid_ref, blk_ref, nv_ref, x_ref, wg_ref, wu_ref, wd_ref, o_ref, *, n_chunks):
    del eid_ref, blk_ref

    @pl.when(pl.program_id(0) < nv_ref[0])
    def _():
        x = x_ref[...]
        cw = D_FF // n_chunks
        acc = jnp.zeros(x.shape, F32)
        for ci in range(n_chunks):
            c0 = ci * cw
            act = (_silu(_dot(x, wg_ref[:, c0:c0 + cw])) * _dot(x, wu_ref[:, c0:c0 + cw])).astype(BF16)
            acc = acc + _dot(act, wd_ref[c0:c0 + cw, :])
        o_ref[...] = acc.astype(BF16)


def _moe_combine_kernel(base_ref, len_ref, off_ref, h_ref, mod_ref, fg_ref, slot_ref, og_hbm, out_ref,
                        og_scr, sem, *, tk):
    i = pl.program_id(0)
    pc = MOE_PIECE
    gr = MOE_GATHER_ROWS
    n_rows = og_scr.shape[0]

    def seg_copy(e, p):
        return pltpu.make_async_copy(
            og_hbm.at[e, pl.ds(pl.multiple_of(base_ref[i * N_EXPERTS + e] + p * pc, pc), pc), :],
            og_scr.at[pl.ds(pl.multiple_of(off_ref[i * N_EXPERTS + e] + p * pc, pc), pc), :], sem)

    for e in range(N_EXPERTS):
        def start(p, carry, e=e):
            seg_copy(e, p).start()
            return carry
        lax.fori_loop(0, len_ref[i * N_EXPERTS + e] // pc, start, 0)

    last = i * N_EXPERTS + N_EXPERTS - 1
    total = off_ref[last] + len_ref[last]

    def clear(p, carry):
        og_scr[pl.ds(pl.multiple_of(p * pc, pc), pc), :] = jnp.zeros((pc, og_scr.shape[1]), BF16)
        return carry
    lax.fori_loop(total // pc, n_rows // pc, clear, 0)

    for e in range(N_EXPERTS):
        def wait(p, carry, e=e):
            seg_copy(e, p).wait()
            return carry
        lax.fori_loop(0, len_ref[i * N_EXPERTS + e] // pc, wait, 0)

    out_ref[...] = jnp.zeros_like(out_ref)

    def scatter(ci, carry):
        r0 = pl.multiple_of(ci * gr, gr)
        rid = (lax.broadcasted_iota(jnp.int32, (gr, tk), 0) + r0).astype(F32)
        pa = jnp.where(rid == slot_ref[0:1, :], 1.0, 0.0)
        pb = jnp.where(rid == slot_ref[1:2, :], 1.0, 0.0)
        gcol = jnp.sum(pa * slot_ref[2:3, :] + pb * slot_ref[3:4, :], axis=1, keepdims=True)
        og = (og_scr[pl.ds(r0, gr), :].astype(F32) * gcol).astype(BF16)
        out_ref[...] = out_ref[...] + _dot_tn((pa + pb).astype(BF16), og)
        return carry
    lax.fori_loop(0, (total + gr - 1) // gr, scatter, 0)
    h3 = h_ref[...] + mod_ref[5:6, :] * out_ref[...]
    out_ref[...] = h3 * lax.rsqrt(jnp.mean(h3 * h3, axis=-1, keepdims=True) + EPS) * fg_ref[...]


def _moe_block_table(seg, n_blocks):
    ends = (seg[-1, 2, :, 0] + seg[-1, 1, :, 0]).astype(jnp.int32)
    nblk = (ends + MOE_BLOCK - 1) // MOE_BLOCK
    cum = jnp.cumsum(nblk)
    n_valid = cum[-1]
    g = jnp.minimum(jnp.arange(n_blocks, dtype=jnp.int32), n_valid - 1)
    eid = jnp.sum((g[:, None] >= cum[None, :]).astype(jnp.int32), axis=1)
    blk = g - (cum - nblk)[eid]
    return eid, blk, n_valid.reshape(1)


def _moe(h, modtab, ng, fg, rwt, rb, wg, wu, wd, *, tk):
    bsz, seq, d = h.shape
    tpb = seq // tk
    n_tiles = bsz * tpb
    n_tok = bsz * seq
    tile_rows = -(-(2 * tk + N_EXPERTS * MOE_PIECE) // MOE_GATHER_ROWS) * MOE_GATHER_ROWS
    cap = -(-(n_tok + n_tiles * MOE_PIECE) // MOE_BLOCK) * MOE_BLOCK
    n_blocks = -(-(2 * n_tok + n_tiles * N_EXPERTS * MOE_PIECE) // MOE_BLOCK) + N_EXPERTS
    tok = lambda i, *_: (i // tpb, i % tpb, 0)
    mod = lambda i, *_: (i // tpb, 1, 0, 0)

    slots, seg, xs = pl.pallas_call(
        functools.partial(_moe_route_kernel, tk=tk),
        grid=(n_tiles,),
        in_specs=[pl.BlockSpec((None, tk, d), tok), pl.BlockSpec((None, None, 6, d), mod),
                  _const_spec((1, d)), _const_spec((N_EXPERTS, d)), _const_spec((N_EXPERTS, 1))],
        out_specs=[pl.BlockSpec((None, 8, tk), lambda i: (i, 0, 0)),
                   pl.BlockSpec((None, 3, N_EXPERTS, 128), lambda i: (i, 0, 0, 0)),
                   pl.BlockSpec(memory_space=pl.ANY)],
        out_shape=[jax.ShapeDtypeStruct((n_tiles, 8, tk), F32),
                   jax.ShapeDtypeStruct((n_tiles, 3, N_EXPERTS, 128), F32),
                   jax.ShapeDtypeStruct((N_EXPERTS, cap, d), BF16)],
        scratch_shapes=[pltpu.VMEM((tk, d), BF16), pltpu.VMEM((tile_rows, d), BF16), pltpu.VMEM((MOE_PIECE, d), BF16),
                        pltpu.SMEM((N_EXPERTS,), jnp.int32), pltpu.SemaphoreType.DMA(())],
        compiler_params=_params(("arbitrary",), 40 << 20),
        name="l1_moe_route",
    )(h, modtab, ng, rwt, rb)

    eid, blk, n_valid = _moe_block_table(seg, n_blocks)
    x_spec = pl.BlockSpec((None, MOE_BLOCK, d), lambda g, eid, blk, nv: (eid[g], blk[g], 0))
    og = pl.pallas_call(
        functools.partial(_moe_ffn_kernel, n_chunks=2),
        grid_spec=pltpu.PrefetchScalarGridSpec(
            num_scalar_prefetch=3, grid=(n_blocks,),
            in_specs=[x_spec,
                      pl.BlockSpec((None, d, D_FF), lambda g, eid, blk, nv: (eid[g], 0, 0)),
                      pl.BlockSpec((None, d, D_FF), lambda g, eid, blk, nv: (eid[g], 0, 0)),
                      pl.BlockSpec((None, D_FF, d), lambda g, eid, blk, nv: (eid[g], 0, 0))],
            out_specs=x_spec),
        out_shape=jax.ShapeDtypeStruct((N_EXPERTS, cap, d), BF16),
        compiler_params=_params(("arbitrary",), 52 << 20),
        name="l1_moe_experts",
    )(eid, blk, n_valid, xs, wg, wu, wd)

    tab = lambda k: seg[:, k, :, 0].astype(jnp.int32).reshape(-1)
    return pl.pallas_call(
        functools.partial(_moe_combine_kernel, tk=tk),
        grid_spec=pltpu.PrefetchScalarGridSpec(
            num_scalar_prefetch=3, grid=(n_tiles,),
            in_specs=[pl.BlockSpec((None, tk, d), tok), pl.BlockSpec((None, None, 6, d), mod),
                      pl.BlockSpec((1, d), lambda i, *_: (0, 0)),
                      pl.BlockSpec((None, 8, tk), lambda i, *_: (i, 0, 0)),
                      pl.BlockSpec(memory_space=pl.ANY)],
            out_specs=pl.BlockSpec((None, tk, d), tok),
            scratch_shapes=[pltpu.VMEM((tile_rows, d), BF16), pltpu.SemaphoreType.DMA(())]),
        out_shape=jax.ShapeDtypeStruct(h.shape, F32),
        compiler_params=_params(("arbitrary",), 40 << 20),
        name="l1_moe_combine",
    )(tab(2), tab(1), tab(0), h, modtab, fg, slots, og)


def _block_diag_gate(gate_w):
    w = gate_w.reshape(2, 2, 2, 4, RG_BLOCK, RG_BLOCK)
    eye = jnp.eye(4, dtype=gate_w.dtype)
    return jnp.einsum('dghbij,bc->dghbicj', w, eye).reshape(2, 2, 2, 256, 256)


def _pad_cols(w, n):
    return jnp.pad(w, ((0, 0), (0, n - w.shape[1])))


def _layer0(h, modtab, norm_mix_g, norm_ffn_g, e_w_in, e_w_out, e_a_conv_w, e_a_conv_b, e_a_gate_w, e_a_gate_b,
            e_a_lambda, e_b_conv_w, e_b_a_log, e_b_dt_bias, e_b_norm_g, e_ffn_w_gate, e_ffn_w_up, e_ffn_w_down,
            *, tm, tt, ctx_len):
    bsz, t_all, d = h.shape
    w_in = _pad_cols(e_w_in, E_IN_PAD).astype(BF16)
    gpar = jnp.zeros((2, 128), F32)
    gpar = gpar.at[0, 2 * DN_HEADS:4 * DN_HEADS].set(e_b_a_log.reshape(-1))
    gpar = gpar.at[1, 2 * DN_HEADS:4 * DN_HEADS].set(e_b_dt_bias.reshape(-1))
    ua, gay, q, k, v, sz, gb = _inproj0(h, modtab, norm_mix_g.reshape(1, d), w_in, e_a_conv_w,
                                        e_a_conv_b.reshape(1, -1), e_b_conv_w, gpar, tm=tm, ctx_len=ctx_len)
    wg = _block_diag_gate(e_a_gate_w).astype(BF16)
    hf, hb = _rglru(jnp.transpose(ua, (1, 0, 2)), wg, e_a_gate_b.reshape(4, RG_WIDTH), e_a_lambda,
                    tt=tt, ctx_len=ctx_len)
    ha = jnp.transpose(hf + hb, (1, 0, 2))
    o0, o1 = _delta(q, k, v, gb, ctx_len=ctx_len, rows=SCAN_ROWS)
    h = _outproj0(ha, gay, o0, o1, sz, h, modtab, e_b_norm_g.reshape(1, -1), e_w_out.astype(BF16),
                  tm=tm, ctx_len=ctx_len)
    return _ffn(h, modtab, norm_ffn_g.reshape(1, d), e_ffn_w_gate.astype(BF16), e_ffn_w_up.astype(BF16),
                e_ffn_w_down.astype(BF16), tm=tm, ctx_len=ctx_len)


def _layer1(h, modtab, norm_mix_g, norm_ffn_g, final_norm_g, o_w_in, o_w_out, o_lb_logits, o_c_norm_g, o_d_gate_w2,
            o_d_gate_b2, o_d_norm_g, o_router_w, o_router_b, o_moe_w_gate, o_moe_w_up, o_moe_w_down,
            *, tm, tk, ctx_len, layer):
    bsz, t_all, d = h.shape
    seq = t_all - ctx_len
    rows = seq // GRID_W
    hl = h[:, ctx_len:, :].reshape(bsz, rows, GRID_W, d).swapaxes(1, 2).reshape(bsz, seq, d)
    w_in = _pad_cols(o_w_in, O_IN_PAD).astype(BF16)
    wlr = jnp.zeros((128, SEG), F32)
    wlr = wlr.at[0:GLA_RANK, 0:256].set(o_d_gate_w2[0]).at[GLA_RANK:2 * GLA_RANK, 256:512].set(o_d_gate_w2[1])
    proj = functools.partial(_inproj1, g=norm_mix_g.reshape(1, d), w_pad=w_in, lbl=o_lb_logits, wlr=wlr,
                             b2=o_d_gate_b2.reshape(1, SEG), tm=tm, layer=layer)
    p1c = proj(h, 0, ctx_len, modtab, 0)
    p1l = proj(hl, 0, seq, modtab, 1)
    sh0, sg0 = _mix1_ctx(p1c, rows=SCAN_ROWS)
    o0, o1 = _mix1_lat(p1l, sh0, sg0, rows=SCAN_ROWS)
    h2 = _outproj1(o0, o1, p1l, hl, modtab, o_c_norm_g.reshape(1, -1), o_d_norm_g.reshape(1, -1),
                   o_w_out.astype(BF16), tm=tm)
    return _moe(h2, modtab, norm_ffn_g.reshape(1, d), final_norm_g.reshape(1, d), o_router_w.T,
                o_router_b.reshape(N_EXPERTS, 1), o_moe_w_gate.astype(BF16), o_moe_w_up.astype(BF16),
                o_moe_w_down.astype(BF16), tk=tk)


def kernel(x, c, ctx, c_ctx, ada_w, ada_b, norm_mix_g, norm_ffn_g, final_norm_g, e_w_in, e_w_out, e_a_conv_w, e_a_conv_b, e_a_gate_w, e_a_gate_b, e_a_lambda, e_b_conv_w, e_b_a_log, e_b_dt_bias, e_b_norm_g, e_ffn_w_gate, e_ffn_w_up, e_ffn_w_down, o_w_in, o_w_out, o_lb_logits, o_c_norm_g, o_d_gate_w2, o_d_gate_b2, o_d_norm_g, o_router_w, o_router_b, o_moe_w_gate, o_moe_w_up, o_moe_w_down):
    bsz, seq, d = x.shape
    ctx_len = ctx.shape[1]
    assert bsz == 8 and d == D_MODEL and ada_w.shape[0] == 2
    tm = min(256, ctx_len)
    tt = min(128, ctx_len)
    tk = min(1024, seq)
    assert ctx_len % tm == 0 and seq % tm == 0 and ctx_len % SCAN_ROWS == 0 and seq % SCAN_ROWS == 0
    assert seq % GRID_W == 0 and seq % tk == 0

    mods = _ada(c, c_ctx, ada_w, ada_b)
    h = jnp.concatenate([ctx, x], axis=1)
    h = _layer0(h, _modtab(mods[0], bsz), norm_mix_g[0], norm_ffn_g[0], e_w_in[0], e_w_out[0], e_a_conv_w[0],
                e_a_conv_b[0], e_a_gate_w[0], e_a_gate_b[0], e_a_lambda[0], e_b_conv_w[0], e_b_a_log[0],
                e_b_dt_bias[0], e_b_norm_g[0], e_ffn_w_gate[0], e_ffn_w_up[0], e_ffn_w_down[0],
                tm=tm, tt=tt, ctx_len=ctx_len)
    out_cm = _layer1(h, _modtab(mods[1], bsz), norm_mix_g[1], norm_ffn_g[1], final_norm_g, o_w_in[0], o_w_out[0],
                     o_lb_logits, o_c_norm_g[0], o_d_gate_w2[0], o_d_gate_b2[0], o_d_norm_g[0], o_router_w[0],
                     o_router_b[0], o_moe_w_gate[0], o_moe_w_up[0], o_moe_w_down[0],
                     tm=tm, tk=tk, ctx_len=ctx_len, layer=1)
    rows = seq // GRID_W
    return out_cm.reshape(bsz, GRID_W, rows, d).swapaxes(1, 2).reshape(bsz, seq, d)
```

```python
import functools

import jax
import jax.numpy as jnp
from jax import lax
from jax.experimental import pallas as pl
from jax.experimental.pallas import tpu as pltpu

F32 = jnp.float32
BF16 = jnp.bfloat16
HI = lax.Precision.HIGHEST

EPS = 1e-6
D_MODEL = 1024
GRID_W = 64
CONV_K = 4
RG_WIDTH = 512
RG_BLOCK = 64
RG_C = 8.0
DN_HEADS = 4
DN_D = 128
DN_CHUNK = 64
HG_HEADS = 4
HG_D = 128
GLA_HEADS = 4
GLA_DK = 64
GLA_DV = 128
GLA_RANK = 16
GLA_GATE_NORM = 16.0
MIX1_CHUNK = 64
SCAN_ROWS = 256
D_FF = 2816
N_EXPERTS = 8

E_IN_PAD = 3200
O_IN_PAD = 4224
SEG = 512
S_HQ, S_HV, S_HK0, S_HK1, S_GV, S_GQK, S_CG, S_DG = [(0, i) for i in range(8)]
S_HLF0, S_HLF1, S_GLD = [(1, i) for i in range(3)]
N_SEG = (8, 3)
P1_DTYPES = (BF16, F32)

V7X_VMEM_BYTES = 64 * 1024 * 1024
VMEM_HEADROOM_BYTES = 8 * 1024 * 1024
MOE_PIECE = 16
MOE_BLOCK = 512
MOE_GATHER_ROWS = 256


def _vmem(nbytes):
    return int(min(V7X_VMEM_BYTES - VMEM_HEADROOM_BYTES, nbytes))


def _params(sem, vmem_bytes):
    return pltpu.CompilerParams(dimension_semantics=sem, vmem_limit_bytes=_vmem(vmem_bytes))


def _sigmoid(x):
    return jax.nn.sigmoid(x)


def _sigmoid_tanh(x):
    return 0.5 * jnp.tanh(0.5 * x) + 0.5


def _silu(x):
    return x * jax.nn.sigmoid(x)


def _softplus(x):
    return jnp.maximum(x, 0.0) + jnp.log1p(jnp.exp(-jnp.abs(x)))


def _gelu_tanh(x):
    return 0.5 * x * (1.0 + jnp.tanh(0.7978845608028654 * (x + 0.044715 * (x * x * x))))


def _normmod(x, g, shift, scale):
    y = x * lax.rsqrt(jnp.mean(x * x, axis=-1, keepdims=True) + EPS)
    return (y * g) * (1.0 + scale) + shift


def _dot(a, b):
    return jnp.dot(a, b, preferred_element_type=F32)


def _dot_nt(a, b):
    return lax.dot_general(a, b, (((1,), (1,)), ((), ())), preferred_element_type=F32)


def _dot_tn(a, b):
    return lax.dot_general(a, b, (((0,), (0,)), ((), ())), preferred_element_type=F32)


def _dot_hi(a, b):
    return jnp.dot(a, b, precision=HI, preferred_element_type=F32)


def _const_spec(shape):
    nd = len(shape)
    return pl.BlockSpec(shape, lambda *_: (0,) * nd, pipeline_mode=pl.Buffered(1))


def _scan_masks(c, d):
    row = lax.broadcasted_iota(jnp.int32, (c, c), 0)
    col = lax.broadcasted_iota(jnp.int32, (c, c), 1)
    dlt = row - col if d == 0 else col - row
    return dlt >= 0, dlt > 0, dlt <= 0, row == col


def _ada_kernel(cv_ref, w_ref, b_ref, o_ref):
    s = _silu(cv_ref[...]).astype(BF16)
    o_ref[...] = _dot(s, w_ref[...].astype(BF16)) + b_ref[...]


def _ada(c, c_ctx, ada_w, ada_b):
    depth, d, n6 = ada_w.shape
    bsz = c.shape[0]
    rows = 16
    cv = jnp.zeros((rows, d), F32).at[:bsz].set(c).at[bsz].set(c_ctx)
    tn = 1536
    return pl.pallas_call(
        _ada_kernel,
        grid=(depth, n6 // tn),
        in_specs=[pl.BlockSpec((rows, d), lambda l, j: (0, 0)),
                  pl.BlockSpec((None, d, tn), lambda l, j: (l, 0, j)),
                  pl.BlockSpec((None, 1, tn), lambda l, j: (l, 0, j))],
        out_specs=pl.BlockSpec((None, rows, tn), lambda l, j: (l, 0, j)),
        out_shape=jax.ShapeDtypeStruct((depth, rows, n6), F32),
        compiler_params=_params(("arbitrary", "arbitrary"), 32 << 20),
        name="ada_mod",
    )(cv, ada_w, ada_b.reshape(depth, 1, n6))


def _modtab(mods_l, bsz):
    m = mods_l.reshape(mods_l.shape[0], 6, D_MODEL)
    lat = m[:bsz]
    ctx = jnp.broadcast_to(m[bsz][None], (bsz, 6, D_MODEL))
    return jnp.stack([ctx, lat], axis=1)


def _inproj0_kernel(hp_ref, hm_ref, hn_ref, mod_ref, g_ref, w_ref, acw_ref, acb_ref, bcw_ref, gpar_ref,
                    ua_ref, gay_ref, q_ref, k_ref, v_ref, sz_ref, gb_ref, u_scr, *, tm, ctx_tiles, n_tiles):
    t = pl.program_id(1)
    x = jnp.concatenate([hp_ref[...], hm_ref[...], hn_ref[...]], axis=0)
    xm = _normmod(x, g_ref[...], mod_ref[0:1, :], mod_ref[1:2, :]).astype(BF16)
    seg_first = jnp.logical_or(t == 0, t == ctx_tiles)
    seg_last = jnp.logical_or(t == ctx_tiles - 1, t == n_tiles - 1)

    def project(c0, width, conv_input):
        u_scr[:, c0:c0 + width] = _dot(xm, w_ref[:, c0:c0 + width])
        if conv_input:
            u_scr[0:8, c0:c0 + width] = jnp.where(seg_first, 0.0, u_scr[0:8, c0:c0 + width])
            u_scr[tm + 8:tm + 16, c0:c0 + width] = jnp.where(seg_last, 0.0, u_scr[tm + 8:tm + 16, c0:c0 + width])

    def conv(c0, width, w_ref_, w0):
        acc = u_scr[6:6 + tm, c0:c0 + width] * w_ref_[0:1, w0:w0 + width]
        for j in range(1, CONV_K):
            acc = acc + u_scr[6 + j:6 + j + tm, c0:c0 + width] * w_ref_[j:j + 1, w0:w0 + width]
        return acc

    project(0, RG_WIDTH, True)
    for grp in range(RG_WIDTH // 128):
        c0 = grp * 128
        ua_ref[:, c0:c0 + 128] = conv(c0, 128, acw_ref, c0) + acb_ref[0:1, c0:c0 + 128]
    project(512, 512, False)
    gay_ref[...] = _gelu_tanh(u_scr[8:8 + tm, 512:1024]).astype(BF16)

    for grp in range(3 * DN_HEADS):
        c0 = grp * 128
        if grp % DN_HEADS == 0:
            project(1024 + c0, DN_HEADS * DN_D, True)
        y = _silu(conv(1024 + c0, 128, bcw_ref, c0))
        if grp < 2 * DN_HEADS:
            y = y * lax.rsqrt(jnp.sum(y * y, axis=-1, keepdims=True) + EPS)
        if grp < DN_HEADS:
            q_ref[:, c0:c0 + 128] = (y * (DN_D ** -0.5)).astype(BF16)
        elif grp < 2 * DN_HEADS:
            k_ref[:, c0 - 512:c0 - 384] = y.astype(BF16)
        else:
            v_ref[:, c0 - 1024:c0 - 896] = y.astype(BF16)
    project(2560, 512, False)
    sz_ref[...] = _silu(u_scr[8:8 + tm, 2560:3072]).astype(BF16)

    project(3072, 128, False)
    xg = u_scr[8:8 + tm, 3072:3200]
    lane = lax.broadcasted_iota(jnp.int32, xg.shape, 1)
    g = -jnp.exp(gpar_ref[0:1, :]) * _softplus(xg + gpar_ref[1:2, :])
    gb_ref[...] = jnp.where(lane < 2 * DN_HEADS, _sigmoid(xg), g)


def _inproj0(h, modtab, g, w_pad, acw, acb, bcw, gpar, *, tm, ctx_len):
    bsz, t_all, d = h.shape
    n_tiles = t_all // tm
    ctx_tiles = ctx_len // tm
    tb = tm // 8
    kern = functools.partial(_inproj0_kernel, tm=tm, ctx_tiles=ctx_tiles, n_tiles=n_tiles)
    tok = lambda w, dt=BF16: jax.ShapeDtypeStruct((bsz, t_all, w), dt)
    tok_spec = lambda w: pl.BlockSpec((None, tm, w), lambda b, t: (b, t, 0))
    return pl.pallas_call(
        kern,
        grid=(bsz, n_tiles),
        in_specs=[
            pl.BlockSpec((None, 8, d), lambda b, t: (b, jnp.maximum(t * tb - 1, 0), 0)),
            pl.BlockSpec((None, tm, d), lambda b, t: (b, t, 0)),
            pl.BlockSpec((None, 8, d), lambda b, t: (b, jnp.minimum((t + 1) * tb, t_all // 8 - 1), 0)),
            pl.BlockSpec((None, None, 6, d), lambda b, t: (b, jnp.where(t >= ctx_tiles, 1, 0), 0, 0)),
            _const_spec((1, d)),
            _const_spec((d, E_IN_PAD)),
            _const_spec((CONV_K, RG_WIDTH)),
            _const_spec((1, RG_WIDTH)),
            _const_spec((CONV_K, 3 * DN_HEADS * DN_D)),
            _const_spec((2, 128)),
        ],
        out_specs=[tok_spec(512), tok_spec(512), tok_spec(512), tok_spec(512), tok_spec(512), tok_spec(512),
                   tok_spec(128)],
        out_shape=[tok(512, F32), tok(512), tok(512), tok(512), tok(512), tok(512), tok(128, F32)],
        scratch_shapes=[pltpu.VMEM((tm + 16, E_IN_PAD), F32)],
        compiler_params=_params(("arbitrary", "arbitrary"), 40 << 20),
        name="l0_inproj",
    )(h, h, h, modtab, g, w_pad, acw, acb, bcw, gpar)


def _rglru_kernel(uf_ref, ub_ref, wg_ref, gbias_ref, lam_ref, hf_ref, hb_ref,
                  af_scr, xf_scr, ab_scr, xb_scr, h_scr, *, tt, bsz):
    s = pl.program_id(0)

    @pl.when(s == 0)
    def _():
        h_scr[...] = jnp.zeros_like(h_scr)

    def gates(u_ref, d, a_scr, x_scr):
        x = u_ref[...].reshape(tt * bsz, RG_WIDTH)
        xb = x.astype(BF16)
        for half in range(2):
            c0 = half * 256
            xh = xb[:, c0:c0 + 256]
            r = _sigmoid_tanh(_dot(xh, wg_ref[d, 0, half]) + gbias_ref[2 * d:2 * d + 1, c0:c0 + 256])
            i = _sigmoid_tanh(_dot(xh, wg_ref[d, 1, half]) + gbias_ref[2 * d + 1:2 * d + 2, c0:c0 + 256])
            log_a = (-RG_C) * r * _softplus(-lam_ref[d:d + 1, c0:c0 + 256])
            a = jnp.exp(log_a)
            mult = jnp.sqrt(-jnp.tanh(log_a) * (a * a + 1.0))
            xin = mult * (i * x[:, c0:c0 + 256])
            a_scr[:, :, c0:c0 + 256] = a.reshape(tt, bsz, 256)
            x_scr[:, :, c0:c0 + 256] = xin.reshape(tt, bsz, 256)

    gates(uf_ref, 0, af_scr, xf_scr)
    gates(ub_ref, 1, ab_scr, xb_scr)

    def step(t, carry):
        hf, hb = carry
        hf = af_scr[t] * hf + xf_scr[t]
        hf_ref[t] = hf
        tb = tt - 1 - t
        hb = ab_scr[tb] * hb + xb_scr[tb]
        hb_ref[tb] = hb
        return hf, hb

    hf, hb = lax.fori_loop(0, tt, step, (h_scr[0], h_scr[1]), unroll=8)
    h_scr[0] = hf
    h_scr[1] = hb


def _rglru(ua3, wg, gbias, lam, *, tt, ctx_len):
    t_all, bsz, w = ua3.shape
    n_steps = t_all // tt
    nc = ctx_len // tt

    def bwd(s):
        return jnp.where(s < nc, nc - 1 - s, n_steps + nc - 1 - s)

    blk = (tt, bsz, w)
    kern = functools.partial(_rglru_kernel, tt=tt, bsz=bsz)
    return pl.pallas_call(
        kern,
        grid=(n_steps,),
        in_specs=[pl.BlockSpec(blk, lambda s: (s, 0, 0)),
                  pl.BlockSpec(blk, lambda s: (bwd(s), 0, 0)),
                  _const_spec(wg.shape), _const_spec(gbias.shape), _const_spec(lam.shape)],
        out_specs=[pl.BlockSpec(blk, lambda s: (s, 0, 0)),
                   pl.BlockSpec(blk, lambda s: (bwd(s), 0, 0))],
        out_shape=[jax.ShapeDtypeStruct(ua3.shape, F32)] * 2,
        scratch_shapes=[pltpu.VMEM(blk, F32)] * 4 + [pltpu.VMEM((2, bsz, w), F32)],
        compiler_params=_params(("arbitrary",), 40 << 20),
        name="l0_rglru",
    )(ua3, ua3, wg, gbias, lam)


def _delta_kernel(qf_ref, kf_ref, vf_ref, gf_ref, qb_ref, kb_ref, vb_ref, gb_ref, of_ref, ob_ref, s_scr, *, n_sub):
    c = DN_CHUNK

    @pl.when(pl.program_id(1) == 0)
    def _():
        s_scr[...] = jnp.zeros_like(s_scr)

    dir_refs = ((qf_ref, kf_ref, vf_ref, gf_ref, of_ref), (qb_ref, kb_ref, vb_ref, gb_ref, ob_ref))
    masks = [_scan_masks(c, d) for d in range(2)]
    eye = jnp.where(masks[0][3], 1.0, 0.0)

    cums = {}
    for d in range(2):
        incl, _, incl_t, _ = masks[d]
        m_incl = jnp.where(incl, 1.0, 0.0)
        m_incl_t = jnp.where(incl_t, 1.0, 0.0)
        for ci in range(n_sub):
            g_all = dir_refs[d][3][ci * c:(ci + 1) * c, :]
            gc_all = _dot_hi(m_incl, g_all)
            gct_all = lax.dot_general(g_all, m_incl_t, (((0,), (0,)), ((), ())), precision=HI,
                                      preferred_element_type=F32)
            cums[d, ci] = (g_all, gc_all, gct_all)

    chains = []
    for d in range(2):
        q_ref, k_ref, v_ref, _, _ = dir_refs[d]
        incl, strict, _, _ = masks[d]
        last = c - 1 if d == 0 else 0
        for ci in range(n_sub):
            g_all, gc_all, gct_all = cums[d, ci]
            rs = slice(ci * c, (ci + 1) * c)
            for h in range(DN_HEADS):
                hs = slice(h * DN_D, (h + 1) * DN_D)
                lane = 2 * DN_HEADS + d * DN_HEADS + h
                ch = dict(d=d, ci=ci, h=h, rs=rs, hs=hs, incl=incl, strict=strict)
                ch["beta"] = g_all[:, d * DN_HEADS + h:d * DN_HEADS + h + 1]
                gc = jnp.broadcast_to(gc_all[:, lane:lane + 1], (c, DN_D))
                gc_row = jnp.broadcast_to(gct_all[lane:lane + 1, :], (c, c))
                ch["gc"] = gc
                ch["gtot"] = gc[last:last + 1, :]
                ch["decay"] = jnp.where(incl, jnp.exp(jnp.minimum(gc[:, 0:c] - gc_row, 0.0)), 0.0)
                ch["e_gc"] = jnp.exp(gc)
                ch["q"] = q_ref[rs, hs].astype(F32)
                ch["k"] = k_ref[rs, hs].astype(F32)
                ch["v"] = v_ref[rs, hs].astype(F32)
                chains.append(ch)

    for ch in chains:
        ch["kb"] = ch["k"] * ch["beta"]
        qk = _dot_nt(jnp.concatenate([ch["kb"], ch["q"]], axis=0).astype(BF16), ch["k"].astype(BF16))
        ch["neg"] = -jnp.where(ch["strict"], qk[0:c] * ch["decay"], 0.0)
        ch["a_qk"] = (qk[c:2 * c] * ch["decay"]).astype(BF16)
    for ch in chains:
        negb = ch["neg"].astype(BF16)
        ch["t"] = eye + ch["neg"]
        ch["p"] = _dot(negb, negb)
    n_sq = max(1, (c - 1).bit_length() - 1)
    for it in range(n_sq):
        for ch in chains:
            tp = _dot(jnp.concatenate([ch["t"], ch["p"]], axis=0).astype(BF16), ch["p"].astype(BF16))
            ch["t"] = ch["t"] + tp[0:c]
            ch["p"] = tp[c:2 * c]
    for ch in chains:
        rhs = jnp.concatenate([ch["v"] * ch["beta"], ch["kb"] * ch["e_gc"]], axis=1).astype(BF16)
        sol = _dot(ch["t"].astype(BF16), rhs)
        ch["u"] = sol[:, 0:DN_D]
        ch["wq"] = jnp.concatenate([sol[:, DN_D:2 * DN_D], ch["q"] * ch["e_gc"]], axis=0).astype(BF16)
        ch["k_tail"] = (ch["k"] * jnp.exp(ch["gtot"] - ch["gc"])).astype(BF16)

    by_key = {(ch["d"], ch["ci"], ch["h"]): ch for ch in chains}
    for step in range(n_sub):
        live = [by_key[d, step if d == 0 else n_sub - 1 - step, h] for d in range(2) for h in range(DN_HEADS)]
        for ch in live:
            ch["st"] = s_scr[ch["d"], ch["h"]]
            ch["ws"] = _dot(ch["wq"], ch["st"].astype(BF16))
        for ch in live:
            vnb = (ch["u"] - ch["ws"][0:c]).astype(BF16)
            o = ch["ws"][c:2 * c] + _dot(ch["a_qk"], vnb)
            dir_refs[ch["d"]][4][ch["rs"], ch["hs"]] = o
            s_scr[ch["d"], ch["h"]] = ch["st"] * jnp.exp(ch["gtot"]) + _dot_tn(ch["k_tail"], vnb)


def _delta(q, k, v, gb, *, ctx_len, rows):
    bsz, t_all, w = q.shape
    n_steps = t_all // rows
    nc = ctx_len // rows

    def bwd(s):
        return jnp.where(s < nc, nc - 1 - s, n_steps + nc - 1 - s)

    fwd_spec = lambda width: pl.BlockSpec((None, rows, width), lambda b, s: (b, s, 0))
    bwd_spec = lambda width: pl.BlockSpec((None, rows, width), lambda b, s: (b, bwd(s), 0))
    return pl.pallas_call(
        functools.partial(_delta_kernel, n_sub=rows // DN_CHUNK),
        grid=(bsz, n_steps),
        in_specs=[fwd_spec(w), fwd_spec(w), fwd_spec(w), fwd_spec(128),
                  bwd_spec(w), bwd_spec(w), bwd_spec(w), bwd_spec(128)],
        out_specs=[fwd_spec(w), bwd_spec(w)],
        out_shape=[jax.ShapeDtypeStruct((bsz, t_all, w), F32)] * 2,
        scratch_shapes=[pltpu.VMEM((2, DN_HEADS, DN_D, DN_D), F32)],
        compiler_params=_params(("arbitrary", "arbitrary"), 32 << 20),
        name="l0_deltanet",
    )(q, k, v, gb, q, k, v, gb)


def _head_norm(y, g):
    return y * lax.rsqrt(jnp.mean(y * y, axis=-1, keepdims=True) + EPS) * g


def _l0_tail_kernel(ha_ref, gay_ref, o0_ref, o1_ref, sz_ref, h_ref, mod_ref, ng_ref, wo_ref, g_ref, wg_ref, wu_ref,
                    wd_ref, hc_ref, hl_ref, *, n_chunks, ctx_tiles):
    parts = [(ha_ref[...] * gay_ref[...].astype(F32)).astype(BF16)]
    for hd in range(DN_HEADS):
        lo = hd * DN_D
        ob = o0_ref[:, lo:lo + DN_D] + o1_ref[:, lo:lo + DN_D]
        parts.append((_head_norm(ob, ng_ref[...]) * sz_ref[:, lo:lo + DN_D].astype(F32)).astype(BF16))
    x = h_ref[...] + mod_ref[2:3, :] * _dot(jnp.concatenate(parts, axis=-1), wo_ref[...])

    xm = _normmod(x, g_ref[...], mod_ref[3:4, :], mod_ref[4:5, :]).astype(BF16)
    cw = D_FF // n_chunks
    acc = jnp.zeros(x.shape, F32)
    for ci in range(n_chunks):
        c0 = ci * cw
        gate = _dot(xm, wg_ref[:, c0:c0 + cw])
        up = _dot(xm, wu_ref[:, c0:c0 + cw])
        acc = acc + _dot((_silu(gate) * up).astype(BF16), wd_ref[c0:c0 + cw, :])
    out = x + mod_ref[5:6, :] * acc

    t = pl.program_id(1)

    @pl.when(t < ctx_tiles)
    def _():
        hc_ref[...] = out

    @pl.when(t >= ctx_tiles)
    def _():
        hl_ref[...] = out


def _l0_tail(ha, gay, o0, o1, sz, h, modtab, ng, wo, g, wg, wu, wd, *, tm, ctx_len):
    bsz, t_all, d = h.shape
    ctx_tiles = ctx_len // tm
    tok = lambda width: pl.BlockSpec((None, tm, width), lambda b, t: (b, t, 0))
    return pl.pallas_call(
        functools.partial(_l0_tail_kernel, n_chunks=2, ctx_tiles=ctx_tiles),
        grid=(bsz, t_all // tm),
        in_specs=[tok(512), tok(512), tok(512), tok(512), tok(512), tok(d),
                  pl.BlockSpec((None, None, 6, d), lambda b, t: (b, jnp.where(t >= ctx_tiles, 1, 0), 0, 0)),
                  _const_spec((1, DN_D)), _const_spec((d, d)),
                  _const_spec((1, d)), _const_spec((d, D_FF)), _const_spec((d, D_FF)), _const_spec((D_FF, d))],
        out_specs=[pl.BlockSpec((None, tm, d), lambda b, t: (b, jnp.minimum(t, ctx_tiles - 1), 0)),
                   pl.BlockSpec((None, tm, d), lambda b, t: (b, jnp.maximum(t - ctx_tiles, 0), 0))],
        out_shape=[jax.ShapeDtypeStruct((bsz, ctx_len, d), F32), jax.ShapeDtypeStruct((bsz, t_all - ctx_len, d), F32)],
        compiler_params=_params(("arbitrary", "arbitrary"), 48 << 20),
        name="l0_tail",
    )(ha, gay, o0, o1, sz, h, modtab, ng, wo, g, wg, wu, wd)


def _inproj1_kernel(h_ref, mod_ref, g_ref, w_ref, lbl_ref, wlr_ref, b2_ref, pa_ref, pb_ref, u_scr, *, layer):
    x = h_ref[...]
    xm = _normmod(x, g_ref[...], mod_ref[0:1, :], mod_ref[1:2, :]).astype(BF16)

    lg = lbl_ref[...]
    ex = jnp.exp(lg - jnp.max(lg, axis=0, keepdims=True))
    lbw = ex / jnp.sum(ex, axis=0, keepdims=True)
    lb = jnp.sum(lbw[1:layer + 1], axis=0, keepdims=True)

    def project(c0, width):
        u_scr[:, c0:c0 + width] = _dot(xm, w_ref[:, c0:c0 + width])

    def put(seg, off, val):
        ref = (pa_ref, pb_ref)[seg[0]]
        ref[:, seg[1] * SEG + off:seg[1] * SEG + off + val.shape[1]] = val.astype(ref.dtype)

    def groups(fn):
        for grp in range(SEG // 128):
            fn(grp * 128)

    project(0, SEG)
    groups(lambda c0: put(S_HQ, c0, _silu(u_scr[:, c0:c0 + 128]) * (HG_D ** -0.5)))
    for dr, (sk, sf) in enumerate(((S_HK0, S_HLF0), (S_HK1, S_HLF1))):
        project(512 + dr * 512, SEG)

        def forget(c0, dr=dr, sk=sk, sf=sf):
            lbg = lb[:, c0:c0 + 128]
            fl = u_scr[:, 512 + dr * 512 + c0:512 + dr * 512 + c0 + 128]
            put(sf, c0, jnp.log(lbg + (1.0 - lbg) * _sigmoid(fl)))
            put(sk, c0, (1.0 - lbg) * _sigmoid(-fl))
        groups(forget)
    project(1536, SEG)
    groups(lambda c0: put(S_HV, c0, u_scr[:, 1536 + c0:1536 + c0 + 128]))
    project(2048, SEG)
    groups(lambda c0: put(S_CG, c0, _silu(u_scr[:, 2048 + c0:2048 + c0 + 128])))
    project(2560, SEG)
    put(S_GQK, 0, u_scr[:, 2560:2816] * (GLA_DK ** -0.5))
    put(S_GQK, 256, u_scr[:, 2816:3072])
    project(3072, SEG)
    groups(lambda c0: put(S_GV, c0, u_scr[:, 3072 + c0:3072 + c0 + 128]))
    project(3584, SEG)
    groups(lambda c0: put(S_DG, c0, _silu(u_scr[:, 3584 + c0:3584 + c0 + 128])))
    project(4096, 128)
    lr = u_scr[:, 4096:4224]
    put(S_GLD, 0, -_softplus(-(_dot_hi(lr, wlr_ref[...]) + b2_ref[...])) * (1.0 / GLA_GATE_NORM))


def _inproj1(h, row0, n_rows, modtab, seg, g, w_pad, lbl, wlr, b2, *, tm, layer):
    bsz, _, d = h.shape
    t0 = row0 // tm
    return pl.pallas_call(
        functools.partial(_inproj1_kernel, layer=layer),
        grid=(bsz, n_rows // tm),
        in_specs=[pl.BlockSpec((None, tm, d), lambda b, t: (b, t0 + t, 0)),
                  pl.BlockSpec((None, None, 6, d), lambda b, t: (b, seg, 0, 0)),
                  _const_spec((1, d)), _const_spec((d, O_IN_PAD)), _const_spec(lbl.shape),
                  _const_spec((128, SEG)), _const_spec((1, SEG))],
        out_specs=[pl.BlockSpec((None, tm, n * SEG), lambda b, t: (b, t, 0)) for n in N_SEG],
        out_shape=[jax.ShapeDtypeStruct((bsz, n_rows, n * SEG), dt) for n, dt in zip(N_SEG, P1_DTYPES)],
        scratch_shapes=[pltpu.VMEM((tm, O_IN_PAD), F32)],
        compiler_params=_params(("arbitrary", "arbitrary"), 44 << 20),
        name="l1_inproj",
    )(h, modtab, g, w_pad, lbl, wlr, b2)


def _gla_stream(d, q_all, k_all, ld_all, v_all, st_ref, o_ref, o_lane0, r0, n_heads, dk, dv, incl, m_incl):
    c = k_all.shape[0]
    mid = c // 2 - 1 if d == 0 else c // 2
    last = c - 1 if d == 0 else 0
    bc = _dot_hi(m_incl, ld_all)
    m = bc[mid:mid + 1]
    btot = bc[last:last + 1]
    kn = k_all.astype(F32) * jnp.exp(m - bc)
    it = dict(d=d, r0=r0, st_ref=st_ref, o_ref=o_ref, o_lane0=o_lane0, n_heads=n_heads, dk=dk, dv=dv, incl=incl,
              c=c, kt=(kn * jnp.exp(btot - m)).astype(BF16), dec=jnp.exp(btot), v=v_all.astype(BF16),
              want_out=q_all is not None)
    if q_all is not None:
        qe = q_all.astype(F32) * jnp.exp(bc)
        it.update(qd=(qe * jnp.exp(-m)).astype(BF16), qe=qe.astype(BF16), knb=kn.astype(BF16))
    return it


def _gla_intra(it):
    dk = it["dk"]
    it["a"] = [jnp.where(it["incl"], _dot_nt(it["qd"][:, hd * dk:(hd + 1) * dk], it["knb"][:, hd * dk:(hd + 1) * dk]),
                         0.0).astype(BF16) for hd in range(it["n_heads"])]


def _gla_advance(it):
    d, dk, dv, c, st_ref = it["d"], it["dk"], it["dv"], it["c"], it["st_ref"]
    sts = [st_ref[d, hd] for hd in range(it["n_heads"])]
    if it["want_out"]:
        for hd in range(it["n_heads"]):
            v = it["v"][:, hd * dv:(hd + 1) * dv]
            o = _dot(it["a"][hd], v) + _dot_nt(it["qe"][:, hd * dk:(hd + 1) * dk], sts[hd].astype(BF16))
            it["o_ref"][it["r0"]:it["r0"] + c, it["o_lane0"] + hd * dv:it["o_lane0"] + (hd + 1) * dv] = o
    for hd in range(it["n_heads"]):
        ks = slice(hd * dk, (hd + 1) * dk)
        st_ref[d, hd] = sts[hd] * it["dec"][:, ks] + _dot_tn(it["v"][:, hd * dv:(hd + 1) * dv], it["kt"][:, ks])


def _mix1_body(dirs, n_sub, sh_ref, sg_ref):
    c = MIX1_CHUNK
    gw = GLA_HEADS * GLA_DK
    prepared = {}
    for d, (hq_ref, hv_ref, hk_ref, hlf_ref, gv_ref, gqk_ref, gld_ref, o_ref) in enumerate(dirs):
        incl = _scan_masks(c, d)[0]
        m_incl = jnp.where(incl, 1.0, 0.0)
        for ci in range(n_sub):
            r0 = ci * c
            rs = slice(r0, r0 + c)
            prepared[d, ci, 0] = _gla_stream(
                d, None if hq_ref is None else hq_ref[rs, :], hk_ref[rs, :], hlf_ref[rs, :], hv_ref[rs, :],
                sh_ref, o_ref, 0, r0, HG_HEADS, HG_D, HG_D, incl, m_incl)
            prepared[d, ci, 1] = _gla_stream(
                d, None if hq_ref is None else gqk_ref[rs, 0:gw], gqk_ref[rs, gw:2 * gw],
                gld_ref[rs, d * gw:(d + 1) * gw], gv_ref[rs, :],
                sg_ref, o_ref, HG_HEADS * HG_D, r0, GLA_HEADS, GLA_DK, GLA_DV, incl, m_incl)
    for it in prepared.values():
        if it["want_out"]:
            _gla_intra(it)
    for step in range(n_sub):
        for d in range(2):
            for stream in range(2):
                _gla_advance(prepared[d, step if d == 0 else n_sub - 1 - step, stream])


def _mix1_ctx_kernel(*refs, n_sub):
    fwd, bwd, (sh_ref, sg_ref) = refs[0:6], refs[6:12], refs[12:14]

    @pl.when(pl.program_id(1) == 0)
    def _():
        sh_ref[...] = jnp.zeros_like(sh_ref)
        sg_ref[...] = jnp.zeros_like(sg_ref)

    _mix1_body([(None,) + tuple(r) + (None,) for r in (fwd, bwd)], n_sub, sh_ref, sg_ref)


def _mix1_lat_kernel(*refs, n_sub):
    fwd, bwd = refs[0:7], refs[7:14]
    sh0_ref, sg0_ref, of_ref, ob_ref, sh_scr, sg_scr = refs[14:20]

    @pl.when(pl.program_id(1) == 0)
    def _():
        sh_scr[...] = sh0_ref[...]
        sg_scr[...] = sg0_ref[...]

    _mix1_body([tuple(fwd) + (of_ref,), tuple(bwd) + (ob_ref,)], n_sub, sh_scr, sg_scr)


def _mix1_specs(p1, rows, n_steps, segs_of_dir):
    specs, args = [], []
    for d in range(2):
        blk = (lambda b, s: s) if d == 0 else (lambda b, s: n_steps - 1 - s)
        for arr, sg in segs_of_dir(d):
            specs.append(pl.BlockSpec((None, rows, SEG), lambda b, s, blk=blk, sg=sg: (b, blk(b, s), sg)))
            args.append(p1[arr])
    return specs, args


_SH_SHAPE = (2, HG_HEADS, HG_D, HG_D)
_SG_SHAPE = (2, GLA_HEADS, GLA_DV, GLA_DK)


def _mix1_ctx(p1c, *, rows):
    bsz, ctx_len, _ = p1c[0].shape
    n_steps = ctx_len // rows
    segs = lambda d: (S_HV, (S_HK0, S_HK1)[d], (S_HLF0, S_HLF1)[d], S_GV, S_GQK, S_GLD)
    specs, args = _mix1_specs(p1c, rows, n_steps, segs)
    state = lambda shape: pl.BlockSpec((None,) + shape, lambda b, s: (b, 0, 0, 0, 0))
    return pl.pallas_call(
        functools.partial(_mix1_ctx_kernel, n_sub=rows // MIX1_CHUNK),
        grid=(bsz, n_steps),
        in_specs=specs,
        out_specs=[state(_SH_SHAPE), state(_SG_SHAPE)],
        out_shape=[jax.ShapeDtypeStruct((bsz,) + _SH_SHAPE, F32), jax.ShapeDtypeStruct((bsz,) + _SG_SHAPE, F32)],
        compiler_params=_params(("arbitrary", "arbitrary"), 32 << 20),
        name="l1_ctx_state",
    )(*args)


def _mix1_lat(p1l, sh0, sg0, *, rows):
    bsz, seq, _ = p1l[0].shape
    n_steps = seq // rows
    segs = lambda d: (S_HQ, S_HV, (S_HK0, S_HK1)[d], (S_HLF0, S_HLF1)[d], S_GV, S_GQK, S_GLD)
    specs, args = _mix1_specs(p1l, rows, n_steps, segs)
    state = lambda shape: pl.BlockSpec((None,) + shape, lambda b, s: (b, 0, 0, 0, 0))
    ow = HG_HEADS * HG_D + GLA_HEADS * GLA_DV
    return pl.pallas_call(
        functools.partial(_mix1_lat_kernel, n_sub=rows // MIX1_CHUNK),
        grid=(bsz, n_steps),
        in_specs=specs + [state(_SH_SHAPE), state(_SG_SHAPE)],
        out_specs=[pl.BlockSpec((None, rows, ow), lambda b, s: (b, s, 0)),
                   pl.BlockSpec((None, rows, ow), lambda b, s: (b, n_steps - 1 - s, 0))],
        out_shape=[jax.ShapeDtypeStruct((bsz, seq, ow), F32)] * 2,
        scratch_shapes=[pltpu.VMEM(_SH_SHAPE, F32), pltpu.VMEM(_SG_SHAPE, F32)],
        compiler_params=_params(("arbitrary", "arbitrary"), 32 << 20),
        name="l1_scan",
    )(*args, sh0, sg0)


def _outproj1_kernel(o0_ref, o1_ref, gate_ref, h_ref, mod_ref, cng_ref, dng_ref, w_ref, out_ref):
    parts = []
    for hd in range(HG_HEADS + GLA_HEADS):
        lo = hd * 128
        y = o0_ref[:, lo:lo + 128] + o1_ref[:, lo:lo + 128]
        ng = cng_ref[...] if hd < HG_HEADS else dng_ref[...]
        parts.append((_head_norm(y, ng) * gate_ref[:, lo:lo + 128].astype(F32)).astype(BF16))
    y = _dot(jnp.concatenate(parts, axis=-1), w_ref[...])
    out_ref[...] = h_ref[...] + mod_ref[2:3, :] * y


def _outproj1(o0, o1, p1l, h, modtab, cng, dng, w, *, tm):
    bsz, seq, d = h.shape
    tok = pl.BlockSpec((None, tm, d), lambda b, t: (b, t, 0))
    return pl.pallas_call(
        _outproj1_kernel,
        grid=(bsz, seq // tm),
        in_specs=[tok, tok,
                  pl.BlockSpec((None, tm, 2 * SEG), lambda b, t: (b, t, S_CG[1] // 2)),
                  tok,
                  pl.BlockSpec((None, None, 6, d), lambda b, t: (b, 1, 0, 0)),
                  _const_spec((1, 128)), _const_spec((1, 128)), _const_spec((d, d))],
        out_specs=tok,
        out_shape=jax.ShapeDtypeStruct((bsz, seq, d), F32),
        compiler_params=_params(("arbitrary", "arbitrary"), 32 << 20),
        name="l1_outproj",
    )(o0, o1, p1l[S_CG[0]], h, modtab, cng, dng, w)


def _moe_route_kernel(h_ref, mod_ref, ng_ref, rwt_ref, rb_ref, slot_ref, seg_ref, xs_hbm,
                      xn_scr, xg_scr, zero_scr, base_smem, sem, *, tk):
    i = pl.program_id(0)
    pc = MOE_PIECE
    gr = MOE_GATHER_ROWS

    @pl.when(i == 0)
    def _():
        for e in range(N_EXPERTS):
            base_smem[e] = 0
        zero_scr[...] = jnp.zeros_like(zero_scr)

    xm = _normmod(h_ref[...], ng_ref[...], mod_ref[3:4, :], mod_ref[4:5, :])
    xn_scr[...] = xm.astype(BF16)
    lg = lax.dot_general(rwt_ref[...], xm, (((1,), (1,)), ((), ())), precision=HI,
                         preferred_element_type=F32) + rb_ref[...]
    eidx = lax.broadcasted_iota(jnp.int32, lg.shape, 0).astype(F32)
    m1 = jnp.max(lg, axis=0, keepdims=True)
    i1 = jnp.min(jnp.where(lg == m1, eidx, float(N_EXPERTS)), axis=0, keepdims=True)
    lg2 = jnp.where(eidx == i1, -jnp.inf, lg)
    m2 = jnp.max(lg2, axis=0, keepdims=True)
    i2 = jnp.min(jnp.where(lg2 == m2, eidx, float(N_EXPERTS)), axis=0, keepdims=True)
    ex = jnp.exp(m2 - m1)
    p1 = 1.0 / (1.0 + ex)
    sel = jnp.where(eidx == i1, 1.0, 0.0) + jnp.where(eidx == i2, 1.0, 0.0)
    lane = lax.broadcasted_iota(jnp.int32, lg.shape, 1)
    cum = sel
    sh = 1
    while sh < tk:
        cum = cum + jnp.where(lane >= sh, pltpu.roll(cum, sh, 1), 0.0)
        sh *= 2
    padded = jnp.floor((cum[:, tk - 1:tk] + (pc - 1.0)) * (1.0 / pc)) * pc
    padded = jnp.broadcast_to(padded, (N_EXPERTS, 128))
    er = lax.broadcasted_iota(jnp.int32, (N_EXPERTS, N_EXPERTS), 0)
    ec = lax.broadcasted_iota(jnp.int32, (N_EXPERTS, N_EXPERTS), 1)
    off = _dot_hi(jnp.where(er > ec, 1.0, 0.0), padded)
    slot = off[:, 0:1] + cum - 1.0
    slot_a = jnp.sum(jnp.where(eidx == i1, slot, 0.0), axis=0, keepdims=True)
    slot_b = jnp.sum(jnp.where(eidx == i2, slot, 0.0), axis=0, keepdims=True)
    slot_ref[...] = jnp.concatenate([slot_a, slot_b, p1, ex * p1, jnp.zeros((4, tk), F32)], axis=0)

    total = jnp.max(off[N_EXPERTS - 1:N_EXPERTS, :] + padded[N_EXPERTS - 1:N_EXPERTS, :]).astype(jnp.int32)

    def gather(ci, carry):
        r0 = pl.multiple_of(ci * gr, gr)
        rid = (lax.broadcasted_iota(jnp.int32, (gr, tk), 0) + r0).astype(F32)
        p = jnp.where(rid == slot_a, 1.0, 0.0) + jnp.where(rid == slot_b, 1.0, 0.0)
        xg_scr[pl.ds(r0, gr), :] = _dot(p.astype(BF16), xn_scr[...]).astype(BF16)
        return carry
    lax.fori_loop(0, (total + gr - 1) // gr, gather, 0)

    erow = lax.broadcasted_iota(jnp.int32, (N_EXPERTS, 128), 0)
    base_vec = jnp.zeros((N_EXPERTS, 128), F32)
    segs = []
    for e in range(N_EXPERTS):
        off_e = jnp.max(off[e:e + 1, :]).astype(jnp.int32)
        len_e = jnp.max(padded[e:e + 1, :]).astype(jnp.int32)
        base_e = base_smem[e]
        base_vec = jnp.where(erow == e, base_e.astype(F32), base_vec)
        segs.append((e, off_e, len_e, base_e))
    seg_ref[0] = off
    seg_ref[1] = padded
    seg_ref[2] = base_vec

    def seg_copy(e, off_e, base_e, p):
        return pltpu.make_async_copy(
            xg_scr.at[pl.ds(pl.multiple_of(off_e + p * pc, pc), pc), :],
            xs_hbm.at[e, pl.ds(pl.multiple_of(base_e + p * pc, pc), pc), :], sem)

    for e, off_e, len_e, base_e in segs:
        def start(p, carry, e=e, off_e=off_e, base_e=base_e):
            seg_copy(e, off_e, base_e, p).start()
            return carry
        lax.fori_loop(0, len_e // pc, start, 0)
    for e, off_e, len_e, base_e in segs:
        def wait(p, carry, e=e, off_e=off_e, base_e=base_e):
            seg_copy(e, off_e, base_e, p).wait()
            return carry
        lax.fori_loop(0, len_e // pc, wait, 0)
        base_smem[e] = base_e + len_e

    @pl.when(i == pl.num_programs(0) - 1)
    def _():
        def tail_copy(e, p):
            end = base_smem[e]
            return pltpu.make_async_copy(zero_scr, xs_hbm.at[e, pl.ds(pl.multiple_of(end + p * pc, pc), pc), :], sem)

        def n_tail(e):
            rem = lax.rem(base_smem[e], MOE_BLOCK)
            return jnp.where(rem == 0, 0, MOE_BLOCK - rem) // pc

        for e in range(N_EXPERTS):
            def start(p, carry, e=e):
                tail_copy(e, p).start()
                return carry
            lax.fori_loop(0, n_tail(e), start, 0)
        for e in range(N_EXPERTS):
            def wait(p, carry, e=e):
                tail_copy(e, p).wait()
                return carry
            lax.fori_loop(0, n_tail(e), wait, 0)


def _moe_ffn_kernel(eid_ref, blk_ref, nv_ref, x_ref, wg_ref, wu_ref, wd_ref, o_ref, *, n_chunks):
    del eid_ref, blk_ref

    @pl.when(pl.program_id(0) < nv_ref[0])
    def _():
        x = x_ref[...]
        cw = D_FF // n_chunks
        acc = jnp.zeros(x.shape, F32)
        for ci in range(n_chunks):
            c0 = ci * cw
            act = (_silu(_dot(x, wg_ref[:, c0:c0 + cw])) * _dot(x, wu_ref[:, c0:c0 + cw])).astype(BF16)
            acc = acc + _dot(act, wd_ref[c0:c0 + cw, :])
        o_ref[...] = acc.astype(BF16)


def _moe_combine_kernel(base_ref, len_ref, off_ref, h_ref, mod_ref, fg_ref, slot_ref, og_hbm, out_ref,
                        og_scr, sem, *, tk):
    i = pl.program_id(0)
    pc = MOE_PIECE
    gr = MOE_GATHER_ROWS
    n_rows = og_scr.shape[0]

    def seg_copy(e, p):
        return pltpu.make_async_copy(
            og_hbm.at[e, pl.ds(pl.multiple_of(base_ref[i * N_EXPERTS + e] + p * pc, pc), pc), :],
            og_scr.at[pl.ds(pl.multiple_of(off_ref[i * N_EXPERTS + e] + p * pc, pc), pc), :], sem)

    for e in range(N_EXPERTS):
        def start(p, carry, e=e):
            seg_copy(e, p).start()
            return carry
        lax.fori_loop(0, len_ref[i * N_EXPERTS + e] // pc, start, 0)

    last = i * N_EXPERTS + N_EXPERTS - 1
    total = off_ref[last] + len_ref[last]

    def clear(p, carry):
        og_scr[pl.ds(pl.multiple_of(p * pc, pc), pc), :] = jnp.zeros((pc, og_scr.shape[1]), BF16)
        return carry
    lax.fori_loop(total // pc, n_rows // pc, clear, 0)

    for e in range(N_EXPERTS):
        def wait(p, carry, e=e):
            seg_copy(e, p).wait()
            return carry
        lax.fori_loop(0, len_ref[i * N_EXPERTS + e] // pc, wait, 0)

    out_ref[...] = jnp.zeros_like(out_ref)

    def scatter(ci, carry):
        r0 = pl.multiple_of(ci * gr, gr)
        rid = (lax.broadcasted_iota(jnp.int32, (gr, tk), 0) + r0).astype(F32)
        pa = jnp.where(rid == slot_ref[0:1, :], 1.0, 0.0)
        pb = jnp.where(rid == slot_ref[1:2, :], 1.0, 0.0)
        gcol = jnp.sum(pa * slot_ref[2:3, :] + pb * slot_ref[3:4, :], axis=1, keepdims=True)
        og = (og_scr[pl.ds(r0, gr), :].astype(F32) * gcol).astype(BF16)
        out_ref[...] = out_ref[...] + _dot_tn((pa + pb).astype(BF16), og)
        return carry
    lax.fori_loop(0, (total + gr - 1) // gr, scatter, 0)
    h3 = h_ref[...] + mod_ref[5:6, :] * out_ref[...]
    out_ref[...] = h3 * lax.rsqrt(jnp.mean(h3 * h3, axis=-1, keepdims=True) + EPS) * fg_ref[...]


def _moe_block_table(seg, n_blocks):
    ends = (seg[-1, 2, :, 0] + seg[-1, 1, :, 0]).astype(jnp.int32)
    nblk = (ends + MOE_BLOCK - 1) // MOE_BLOCK
    cum = jnp.cumsum(nblk)
    n_valid = cum[-1]
    g = jnp.minimum(jnp.arange(n_blocks, dtype=jnp.int32), n_valid - 1)
    eid = jnp.sum((g[:, None] >= cum[None, :]).astype(jnp.int32), axis=1)
    blk = g - (cum - nblk)[eid]
    return eid, blk, n_valid.reshape(1)


def _moe(h, modtab, ng, fg, rwt, rb, wg, wu, wd, *, tk):
    bsz, seq, d = h.shape
    tpb = seq // tk
    n_tiles = bsz * tpb
    n_tok = bsz * seq
    tile_rows = -(-(2 * tk + N_EXPERTS * MOE_PIECE) // MOE_GATHER_ROWS) * MOE_GATHER_ROWS
    cap = -(-(n_tok + n_tiles * MOE_PIECE) // MOE_BLOCK) * MOE_BLOCK
    n_blocks = -(-(2 * n_tok + n_tiles * N_EXPERTS * MOE_PIECE) // MOE_BLOCK) + N_EXPERTS
    tok = lambda i, *_: (i // tpb, i % tpb, 0)
    mod = lambda i, *_: (i // tpb, 1, 0, 0)

    slots, seg, xs = pl.pallas_call(
        functools.partial(_moe_route_kernel, tk=tk),
        grid=(n_tiles,),
        in_specs=[pl.BlockSpec((None, tk, d), tok), pl.BlockSpec((None, None, 6, d), mod),
                  _const_spec((1, d)), _const_spec((N_EXPERTS, d)), _const_spec((N_EXPERTS, 1))],
        out_specs=[pl.BlockSpec((None, 8, tk), lambda i: (i, 0, 0)),
                   pl.BlockSpec((None, 3, N_EXPERTS, 128), lambda i: (i, 0, 0, 0)),
                   pl.BlockSpec(memory_space=pl.ANY)],
        out_shape=[jax.ShapeDtypeStruct((n_tiles, 8, tk), F32),
                   jax.ShapeDtypeStruct((n_tiles, 3, N_EXPERTS, 128), F32),
                   jax.ShapeDtypeStruct((N_EXPERTS, cap, d), BF16)],
        scratch_shapes=[pltpu.VMEM((tk, d), BF16), pltpu.VMEM((tile_rows, d), BF16), pltpu.VMEM((MOE_PIECE, d), BF16),
                        pltpu.SMEM((N_EXPERTS,), jnp.int32), pltpu.SemaphoreType.DMA(())],
        compiler_params=_params(("arbitrary",), 40 << 20),
        name="l1_moe_route",
    )(h, modtab, ng, rwt, rb)

    eid, blk, n_valid = _moe_block_table(seg, n_blocks)
    x_spec = pl.BlockSpec((None, MOE_BLOCK, d), lambda g, eid, blk, nv: (eid[g], blk[g], 0))
    og = pl.pallas_call(
        functools.partial(_moe_ffn_kernel, n_chunks=2),
        grid_spec=pltpu.PrefetchScalarGridSpec(
            num_scalar_prefetch=3, grid=(n_blocks,),
            in_specs=[x_spec,
                      pl.BlockSpec((None, d, D_FF), lambda g, eid, blk, nv: (eid[g], 0, 0)),
                      pl.BlockSpec((None, d, D_FF), lambda g, eid, blk, nv: (eid[g], 0, 0)),
                      pl.BlockSpec((None, D_FF, d), lambda g, eid, blk, nv: (eid[g], 0, 0))],
            out_specs=x_spec),
        out_shape=jax.ShapeDtypeStruct((N_EXPERTS, cap, d), BF16),
        compiler_params=_params(("arbitrary",), 52 << 20),
        name="l1_moe_experts",
    )(eid, blk, n_valid, xs, wg, wu, wd)

    tab = lambda k: seg[:, k, :, 0].astype(jnp.int32).reshape(-1)
    return pl.pallas_call(
        functools.partial(_moe_combine_kernel, tk=tk),
        grid_spec=pltpu.PrefetchScalarGridSpec(
            num_scalar_prefetch=3, grid=(n_tiles,),
            in_specs=[pl.BlockSpec((None, tk, d), tok), pl.BlockSpec((None, None, 6, d), mod),
                      pl.BlockSpec((1, d), lambda i, *_: (0, 0)),
                      pl.BlockSpec((None, 8, tk), lambda i, *_: (i, 0, 0)),
                      pl.BlockSpec(memory_space=pl.ANY)],
            out_specs=pl.BlockSpec((None, tk, d), tok),
            scratch_shapes=[pltpu.VMEM((tile_rows, d), BF16), pltpu.SemaphoreType.DMA(())]),
        out_shape=jax.ShapeDtypeStruct(h.shape, F32),
        compiler_params=_params(("arbitrary",), 40 << 20),
        name="l1_moe_combine",
    )(tab(2), tab(1), tab(0), h, modtab, fg, slots, og)


def _block_diag_gate(gate_w):
    w = gate_w.reshape(2, 2, 2, 4, RG_BLOCK, RG_BLOCK)
    eye = jnp.eye(4, dtype=gate_w.dtype)
    return jnp.einsum('dghbij,bc->dghbicj', w, eye).reshape(2, 2, 2, 256, 256)


def _pad_cols(w, n):
    return jnp.pad(w, ((0, 0), (0, n - w.shape[1])))


def _layer0(h, modtab, norm_mix_g, norm_ffn_g, e_w_in, e_w_out, e_a_conv_w, e_a_conv_b, e_a_gate_w, e_a_gate_b,
            e_a_lambda, e_b_conv_w, e_b_a_log, e_b_dt_bias, e_b_norm_g, e_ffn_w_gate, e_ffn_w_up, e_ffn_w_down,
            *, tm, tt, ctx_len):
    bsz, t_all, d = h.shape
    w_in = _pad_cols(e_w_in, E_IN_PAD).astype(BF16)
    gpar = jnp.zeros((2, 128), F32)
    gpar = gpar.at[0, 2 * DN_HEADS:4 * DN_HEADS].set(e_b_a_log.reshape(-1))
    gpar = gpar.at[1, 2 * DN_HEADS:4 * DN_HEADS].set(e_b_dt_bias.reshape(-1))
    ua, gay, q, k, v, sz, gb = _inproj0(h, modtab, norm_mix_g.reshape(1, d), w_in, e_a_conv_w,
                                        e_a_conv_b.reshape(1, -1), e_b_conv_w, gpar, tm=tm, ctx_len=ctx_len)
    wg = _block_diag_gate(e_a_gate_w).astype(BF16)
    hf, hb = _rglru(jnp.transpose(ua, (1, 0, 2)), wg, e_a_gate_b.reshape(4, RG_WIDTH), e_a_lambda,
                    tt=tt, ctx_len=ctx_len)
    ha = jnp.transpose(hf + hb, (1, 0, 2))
    o0, o1 = _delta(q, k, v, gb, ctx_len=ctx_len, rows=SCAN_ROWS)
    return _l0_tail(ha, gay, o0, o1, sz, h, modtab, e_b_norm_g.reshape(1, -1), e_w_out.astype(BF16),
                    norm_ffn_g.reshape(1, d), e_ffn_w_gate.astype(BF16), e_ffn_w_up.astype(BF16),
                    e_ffn_w_down.astype(BF16), tm=tm, ctx_len=ctx_len)


def _layer1(hc, hl, modtab, norm_mix_g, norm_ffn_g, final_norm_g, o_w_in, o_w_out, o_lb_logits, o_c_norm_g,
            o_d_gate_w2, o_d_gate_b2, o_d_norm_g, o_router_w, o_router_b, o_moe_w_gate, o_moe_w_up, o_moe_w_down,
            *, tm, tk, layer):
    bsz, seq, d = hl.shape
    ctx_len = hc.shape[1]
    rows = seq // GRID_W
    hl = hl.reshape(bsz, rows, GRID_W, d).swapaxes(1, 2).reshape(bsz, seq, d)
    w_in = _pad_cols(o_w_in, O_IN_PAD).astype(BF16)
    wlr = jnp.zeros((128, SEG), F32)
    wlr = wlr.at[0:GLA_RANK, 0:256].set(o_d_gate_w2[0]).at[GLA_RANK:2 * GLA_RANK, 256:512].set(o_d_gate_w2[1])
    proj = functools.partial(_inproj1, g=norm_mix_g.reshape(1, d), w_pad=w_in, lbl=o_lb_logits, wlr=wlr,
                             b2=o_d_gate_b2.reshape(1, SEG), layer=layer)
    p1c = proj(hc, 0, ctx_len, modtab, 0, tm=tm)
    p1l = proj(hl, 0, seq, modtab, 1, tm=2 * tm)
    sh0, sg0 = _mix1_ctx(p1c, rows=SCAN_ROWS)
    o0, o1 = _mix1_lat(p1l, sh0, sg0, rows=SCAN_ROWS)
    h2 = _outproj1(o0, o1, p1l, hl, modtab, o_c_norm_g.reshape(1, -1), o_d_norm_g.reshape(1, -1),
                   o_w_out.astype(BF16), tm=tm)
    return _moe(h2, modtab, norm_ffn_g.reshape(1, d), final_norm_g.reshape(1, d), o_router_w.T,
                o_router_b.reshape(N_EXPERTS, 1), o_moe_w_gate.astype(BF16), o_moe_w_up.astype(BF16),
                o_moe_w_down.astype(BF16), tk=tk)


def kernel(x, c, ctx, c_ctx, ada_w, ada_b, norm_mix_g, norm_ffn_g, final_norm_g, e_w_in, e_w_out, e_a_conv_w, e_a_conv_b, e_a_gate_w, e_a_gate_b, e_a_lambda, e_b_conv_w, e_b_a_log, e_b_dt_bias, e_b_norm_g, e_ffn_w_gate, e_ffn_w_up, e_ffn_w_down, o_w_in, o_w_out, o_lb_logits, o_c_norm_g, o_d_gate_w2, o_d_gate_b2, o_d_norm_g, o_router_w, o_router_b, o_moe_w_gate, o_moe_w_up, o_moe_w_down):
    bsz, seq, d = x.shape
    ctx_len = ctx.shape[1]
    assert bsz == 8 and d == D_MODEL and ada_w.shape[0] == 2
    tm = min(256, ctx_len)
    tt = min(128, ctx_len)
    tk = min(1024, seq)
    assert ctx_len % tm == 0 and seq % tm == 0 and ctx_len % SCAN_ROWS == 0 and seq % SCAN_ROWS == 0
    assert seq % GRID_W == 0 and seq % tk == 0

    mods = _ada(c, c_ctx, ada_w, ada_b)
    h = jnp.concatenate([ctx, x], axis=1)
    hc, hl = _layer0(h, _modtab(mods[0], bsz), norm_mix_g[0], norm_ffn_g[0], e_w_in[0], e_w_out[0], e_a_conv_w[0],
                     e_a_conv_b[0], e_a_gate_w[0], e_a_gate_b[0], e_a_lambda[0], e_b_conv_w[0], e_b_a_log[0],
                     e_b_dt_bias[0], e_b_norm_g[0], e_ffn_w_gate[0], e_ffn_w_up[0], e_ffn_w_down[0],
                     tm=tm, tt=tt, ctx_len=ctx_len)
    out_cm = _layer1(hc, hl, _modtab(mods[1], bsz), norm_mix_g[1], norm_ffn_g[1], final_norm_g, o_w_in[0],
                     o_w_out[0], o_lb_logits, o_c_norm_g[0], o_d_gate_w2[0], o_d_gate_b2[0], o_d_norm_g[0],
                     o_router_w[0], o_router_b[0], o_moe_w_gate[0], o_moe_w_up[0], o_moe_w_down[0],
                     tm=tm, tk=tk, layer=1)
    rows = seq // GRID_W
    return out_cm.reshape(bsz, GRID_W, rows, d).swapaxes(1, 2).reshape(bsz, seq, d)
```

```python
import functools

import jax
import jax.numpy as jnp
from jax import lax
from jax.experimental import pallas as pl
from jax.experimental.pallas import tpu as pltpu

F32 = jnp.float32
BF16 = jnp.bfloat16
HI = lax.Precision.HIGHEST

EPS = 1e-6
D_MODEL = 1024
GRID_W = 64
CONV_K = 4
RG_WIDTH = 512
RG_BLOCK = 64
RG_C = 8.0
DN_HEADS = 4
DN_D = 128
DN_CHUNK = 64
HG_HEADS = 4
HG_D = 128
GLA_HEADS = 4
GLA_DK = 64
GLA_DV = 128
GLA_RANK = 16
GLA_GATE_NORM = 16.0
MIX1_CHUNK = 64
SCAN_ROWS = 256
D_FF = 2816
N_EXPERTS = 8

E_IN_PAD = 3200
O_IN_PAD = 4224
SEG = 512
S_HQ, S_HV, S_HK0, S_HK1, S_GV, S_GQK, S_CG, S_DG = [(0, i) for i in range(8)]
S_HLF0, S_HLF1, S_GLD = [(1, i) for i in range(3)]
N_SEG = (8, 3)
P1_DTYPES = (BF16, F32)

V7X_VMEM_BYTES = 64 * 1024 * 1024
VMEM_HEADROOM_BYTES = 8 * 1024 * 1024
MOE_PIECE = 16
MOE_BLOCK = 512
MOE_GATHER_ROWS = 256


def _vmem(nbytes):
    return int(min(V7X_VMEM_BYTES - VMEM_HEADROOM_BYTES, nbytes))


def _params(sem, vmem_bytes):
    return pltpu.CompilerParams(dimension_semantics=sem, vmem_limit_bytes=_vmem(vmem_bytes))


def _sigmoid(x):
    return jax.nn.sigmoid(x)


def _sigmoid_tanh(x):
    return 0.5 * jnp.tanh(0.5 * x) + 0.5


def _silu(x):
    return x * jax.nn.sigmoid(x)


def _softplus(x):
    return jnp.maximum(x, 0.0) + jnp.log1p(jnp.exp(-jnp.abs(x)))


def _gelu_tanh(x):
    return 0.5 * x * (1.0 + jnp.tanh(0.7978845608028654 * (x + 0.044715 * (x * x * x))))


def _normmod(x, g, shift, scale):
    y = x * lax.rsqrt(jnp.mean(x * x, axis=-1, keepdims=True) + EPS)
    return (y * g) * (1.0 + scale) + shift


def _dot(a, b):
    return jnp.dot(a, b, preferred_element_type=F32)


def _dot_nt(a, b):
    return lax.dot_general(a, b, (((1,), (1,)), ((), ())), preferred_element_type=F32)


def _dot_tn(a, b):
    return lax.dot_general(a, b, (((0,), (0,)), ((), ())), preferred_element_type=F32)


def _dot_hi(a, b):
    return jnp.dot(a, b, precision=HI, preferred_element_type=F32)


def _const_spec(shape):
    nd = len(shape)
    return pl.BlockSpec(shape, lambda *_: (0,) * nd, pipeline_mode=pl.Buffered(1))


def _scan_masks(c, d):
    row = lax.broadcasted_iota(jnp.int32, (c, c), 0)
    col = lax.broadcasted_iota(jnp.int32, (c, c), 1)
    dlt = row - col if d == 0 else col - row
    return dlt >= 0, dlt > 0, dlt <= 0, row == col


def _ada_kernel(cv_ref, w_ref, b_ref, o_ref):
    s = _silu(cv_ref[...]).astype(BF16)
    o_ref[...] = _dot(s, w_ref[...].astype(BF16)) + b_ref[...]


def _ada(c, c_ctx, ada_w, ada_b):
    depth, d, n6 = ada_w.shape
    bsz = c.shape[0]
    rows = 16
    cv = jnp.zeros((rows, d), F32).at[:bsz].set(c).at[bsz].set(c_ctx)
    tn = 1536
    return pl.pallas_call(
        _ada_kernel,
        grid=(depth, n6 // tn),
        in_specs=[pl.BlockSpec((rows, d), lambda l, j: (0, 0)),
                  pl.BlockSpec((None, d, tn), lambda l, j: (l, 0, j)),
                  pl.BlockSpec((None, 1, tn), lambda l, j: (l, 0, j))],
        out_specs=pl.BlockSpec((None, rows, tn), lambda l, j: (l, 0, j)),
        out_shape=jax.ShapeDtypeStruct((depth, rows, n6), F32),
        compiler_params=_params(("arbitrary", "arbitrary"), 32 << 20),
        name="ada_mod",
    )(cv, ada_w, ada_b.reshape(depth, 1, n6))


def _modtab(mods_l, bsz):
    m = mods_l.reshape(mods_l.shape[0], 6, D_MODEL)
    lat = m[:bsz]
    ctx = jnp.broadcast_to(m[bsz][None], (bsz, 6, D_MODEL))
    return jnp.stack([ctx, lat], axis=1)


def _inproj0_kernel(cp_ref, cm_ref, cn_ref, xp_ref, xm_ref, xn_ref, mod_ref, g_ref, w_ref, acw_ref, acb_ref, bcw_ref,
                    gpar_ref, ua_ref, gay_ref, q_ref, k_ref, v_ref, sz_ref, gb_ref, u_scr, *, tm, ctx_tiles, n_tiles):
    t = pl.program_id(1)
    pick = lambda c_ref, x_ref: jnp.where(t < ctx_tiles, c_ref[...], x_ref[...])
    x = jnp.concatenate([pick(cp_ref, xp_ref), pick(cm_ref, xm_ref), pick(cn_ref, xn_ref)], axis=0)
    xm = _normmod(x, g_ref[...], mod_ref[0:1, :], mod_ref[1:2, :]).astype(BF16)
    seg_first = jnp.logical_or(t == 0, t == ctx_tiles)
    seg_last = jnp.logical_or(t == ctx_tiles - 1, t == n_tiles - 1)

    def project(c0, width, conv_input):
        u_scr[:, c0:c0 + width] = _dot(xm, w_ref[:, c0:c0 + width])
        if conv_input:
            u_scr[0:8, c0:c0 + width] = jnp.where(seg_first, 0.0, u_scr[0:8, c0:c0 + width])
            u_scr[tm + 8:tm + 16, c0:c0 + width] = jnp.where(seg_last, 0.0, u_scr[tm + 8:tm + 16, c0:c0 + width])

    def conv(c0, width, w_ref_, w0):
        acc = u_scr[6:6 + tm, c0:c0 + width] * w_ref_[0:1, w0:w0 + width]
        for j in range(1, CONV_K):
            acc = acc + u_scr[6 + j:6 + j + tm, c0:c0 + width] * w_ref_[j:j + 1, w0:w0 + width]
        return acc

    project(0, RG_WIDTH, True)
    for grp in range(RG_WIDTH // 128):
        c0 = grp * 128
        ua_ref[:, c0:c0 + 128] = conv(c0, 128, acw_ref, c0) + acb_ref[0:1, c0:c0 + 128]
    project(512, 512, False)
    gay_ref[...] = _gelu_tanh(u_scr[8:8 + tm, 512:1024]).astype(BF16)

    for grp in range(3 * DN_HEADS):
        c0 = grp * 128
        if grp % DN_HEADS == 0:
            project(1024 + c0, DN_HEADS * DN_D, True)
        y = _silu(conv(1024 + c0, 128, bcw_ref, c0))
        if grp < 2 * DN_HEADS:
            y = y * lax.rsqrt(jnp.sum(y * y, axis=-1, keepdims=True) + EPS)
        if grp < DN_HEADS:
            q_ref[:, c0:c0 + 128] = (y * (DN_D ** -0.5)).astype(BF16)
        elif grp < 2 * DN_HEADS:
            k_ref[:, c0 - 512:c0 - 384] = y.astype(BF16)
        else:
            v_ref[:, c0 - 1024:c0 - 896] = y.astype(BF16)
    project(2560, 512, False)
    sz_ref[...] = _silu(u_scr[8:8 + tm, 2560:3072]).astype(BF16)

    project(3072, 128, False)
    xg = u_scr[8:8 + tm, 3072:3200]
    lane = lax.broadcasted_iota(jnp.int32, xg.shape, 1)
    g = -jnp.exp(gpar_ref[0:1, :]) * _softplus(xg + gpar_ref[1:2, :])
    gb_ref[...] = jnp.where(lane < 2 * DN_HEADS, _sigmoid(xg), g)


def _row_specs(tm, d, ctx_tiles, ctx_len, seq, halo):
    tb = tm // 8

    def specs(n_rows, tile_of):
        main = pl.BlockSpec((None, tm, d), lambda b, t: (b, jnp.clip(tile_of(t), 0, n_rows // tm - 1), 0))
        if not halo:
            return [main]
        prev = pl.BlockSpec((None, 8, d), lambda b, t: (b, jnp.clip(tile_of(t) * tb - 1, 0, n_rows // 8 - 1), 0))
        nxt = pl.BlockSpec((None, 8, d), lambda b, t: (b, jnp.clip((tile_of(t) + 1) * tb, 0, n_rows // 8 - 1), 0))
        return [prev, main, nxt]

    return specs(ctx_len, lambda t: t) + specs(seq, lambda t: t - ctx_tiles)


def _inproj0(ctx, x, modtab, g, w_pad, acw, acb, bcw, gpar, *, tm):
    bsz, ctx_len, d = ctx.shape
    seq = x.shape[1]
    t_all = ctx_len + seq
    n_tiles = t_all // tm
    ctx_tiles = ctx_len // tm
    kern = functools.partial(_inproj0_kernel, tm=tm, ctx_tiles=ctx_tiles, n_tiles=n_tiles)
    tok = lambda w, dt=BF16: jax.ShapeDtypeStruct((bsz, t_all, w), dt)
    tok_spec = lambda w: pl.BlockSpec((None, tm, w), lambda b, t: (b, t, 0))
    return pl.pallas_call(
        kern,
        grid=(bsz, n_tiles),
        in_specs=_row_specs(tm, d, ctx_tiles, ctx_len, seq, True) + [
            pl.BlockSpec((None, None, 6, d), lambda b, t: (b, jnp.where(t >= ctx_tiles, 1, 0), 0, 0)),
            _const_spec((1, d)),
            _const_spec((d, E_IN_PAD)),
            _const_spec((CONV_K, RG_WIDTH)),
            _const_spec((1, RG_WIDTH)),
            _const_spec((CONV_K, 3 * DN_HEADS * DN_D)),
            _const_spec((2, 128)),
        ],
        out_specs=[tok_spec(512), tok_spec(512), tok_spec(512), tok_spec(512), tok_spec(512), tok_spec(512),
                   tok_spec(128)],
        out_shape=[tok(512, F32), tok(512), tok(512), tok(512), tok(512), tok(512), tok(128, F32)],
        scratch_shapes=[pltpu.VMEM((tm + 16, E_IN_PAD), F32)],
        compiler_params=_params(("arbitrary", "arbitrary"), 40 << 20),
        name="l0_inproj",
    )(ctx, ctx, ctx, x, x, x, modtab, g, w_pad, acw, acb, bcw, gpar)


def _rglru_kernel(uf_ref, ub_ref, wg_ref, gbias_ref, lam_ref, hf_ref, hb_ref,
                  af_scr, xf_scr, ab_scr, xb_scr, h_scr, *, tt, bsz):
    s = pl.program_id(0)

    @pl.when(s == 0)
    def _():
        h_scr[...] = jnp.zeros_like(h_scr)

    def gates(u_ref, d, a_scr, x_scr):
        x = u_ref[...].reshape(tt * bsz, RG_WIDTH)
        xb = x.astype(BF16)
        for half in range(2):
            c0 = half * 256
            xh = xb[:, c0:c0 + 256]
            r = _sigmoid_tanh(_dot(xh, wg_ref[d, 0, half]) + gbias_ref[2 * d:2 * d + 1, c0:c0 + 256])
            i = _sigmoid_tanh(_dot(xh, wg_ref[d, 1, half]) + gbias_ref[2 * d + 1:2 * d + 2, c0:c0 + 256])
            log_a = (-RG_C) * r * _softplus(-lam_ref[d:d + 1, c0:c0 + 256])
            a = jnp.exp(log_a)
            mult = jnp.sqrt(-jnp.tanh(log_a) * (a * a + 1.0))
            xin = mult * (i * x[:, c0:c0 + 256])
            a_scr[:, :, c0:c0 + 256] = a.reshape(tt, bsz, 256)
            x_scr[:, :, c0:c0 + 256] = xin.reshape(tt, bsz, 256)

    gates(uf_ref, 0, af_scr, xf_scr)
    gates(ub_ref, 1, ab_scr, xb_scr)

    def step(t, carry):
        hf, hb = carry
        hf = af_scr[t] * hf + xf_scr[t]
        hf_ref[t] = hf
        tb = tt - 1 - t
        hb = ab_scr[tb] * hb + xb_scr[tb]
        hb_ref[tb] = hb
        return hf, hb

    hf, hb = lax.fori_loop(0, tt, step, (h_scr[0], h_scr[1]), unroll=8)
    h_scr[0] = hf
    h_scr[1] = hb


def _rglru(ua3, wg, gbias, lam, *, tt, ctx_len):
    t_all, bsz, w = ua3.shape
    n_steps = t_all // tt
    nc = ctx_len // tt

    def bwd(s):
        return jnp.where(s < nc, nc - 1 - s, n_steps + nc - 1 - s)

    blk = (tt, bsz, w)
    kern = functools.partial(_rglru_kernel, tt=tt, bsz=bsz)
    return pl.pallas_call(
        kern,
        grid=(n_steps,),
        in_specs=[pl.BlockSpec(blk, lambda s: (s, 0, 0)),
                  pl.BlockSpec(blk, lambda s: (bwd(s), 0, 0)),
                  _const_spec(wg.shape), _const_spec(gbias.shape), _const_spec(lam.shape)],
        out_specs=[pl.BlockSpec(blk, lambda s: (s, 0, 0)),
                   pl.BlockSpec(blk, lambda s: (bwd(s), 0, 0))],
        out_shape=[jax.ShapeDtypeStruct(ua3.shape, F32)] * 2,
        scratch_shapes=[pltpu.VMEM(blk, F32)] * 4 + [pltpu.VMEM((2, bsz, w), F32)],
        compiler_params=_params(("arbitrary",), 40 << 20),
        name="l0_rglru",
    )(ua3, ua3, wg, gbias, lam)


def _delta_kernel(qf_ref, kf_ref, vf_ref, gf_ref, qb_ref, kb_ref, vb_ref, gb_ref, of_ref, ob_ref, s_scr, *, n_sub):
    c = DN_CHUNK

    @pl.when(pl.program_id(1) == 0)
    def _():
        s_scr[...] = jnp.zeros_like(s_scr)

    dir_refs = ((qf_ref, kf_ref, vf_ref, gf_ref, of_ref), (qb_ref, kb_ref, vb_ref, gb_ref, ob_ref))
    masks = [_scan_masks(c, d) for d in range(2)]
    eye = jnp.where(masks[0][3], 1.0, 0.0)

    cums = {}
    for d in range(2):
        incl, _, incl_t, _ = masks[d]
        m_incl = jnp.where(incl, 1.0, 0.0)
        m_incl_t = jnp.where(incl_t, 1.0, 0.0)
        for ci in range(n_sub):
            g_all = dir_refs[d][3][ci * c:(ci + 1) * c, :]
            gc_all = _dot_hi(m_incl, g_all)
            gct_all = lax.dot_general(g_all, m_incl_t, (((0,), (0,)), ((), ())), precision=HI,
                                      preferred_element_type=F32)
            cums[d, ci] = (g_all, gc_all, gct_all)

    chains = []
    for d in range(2):
        q_ref, k_ref, v_ref, _, _ = dir_refs[d]
        incl, strict, _, _ = masks[d]
        last = c - 1 if d == 0 else 0
        for ci in range(n_sub):
            g_all, gc_all, gct_all = cums[d, ci]
            rs = slice(ci * c, (ci + 1) * c)
            for h in range(DN_HEADS):
                hs = slice(h * DN_D, (h + 1) * DN_D)
                lane = 2 * DN_HEADS + d * DN_HEADS + h
                ch = dict(d=d, ci=ci, h=h, rs=rs, hs=hs, incl=incl, strict=strict)
                ch["beta"] = g_all[:, d * DN_HEADS + h:d * DN_HEADS + h + 1]
                gc = jnp.broadcast_to(gc_all[:, lane:lane + 1], (c, DN_D))
                gc_row = jnp.broadcast_to(gct_all[lane:lane + 1, :], (c, c))
                ch["gc"] = gc
                ch["gtot"] = gc[last:last + 1, :]
                ch["decay"] = jnp.where(incl, jnp.exp(jnp.minimum(gc[:, 0:c] - gc_row, 0.0)), 0.0)
                ch["e_gc"] = jnp.exp(gc)
                ch["q"] = q_ref[rs, hs].astype(F32)
                ch["k"] = k_ref[rs, hs].astype(F32)
                ch["v"] = v_ref[rs, hs].astype(F32)
                chains.append(ch)

    for ch in chains:
        ch["kb"] = ch["k"] * ch["beta"]
        qk = _dot_nt(jnp.concatenate([ch["kb"], ch["q"]], axis=0).astype(BF16), ch["k"].astype(BF16))
        ch["neg"] = -jnp.where(ch["strict"], qk[0:c] * ch["decay"], 0.0)
        ch["a_qk"] = (qk[c:2 * c] * ch["decay"]).astype(BF16)
    for ch in chains:
        negb = ch["neg"].astype(BF16)
        ch["t"] = eye + ch["neg"]
        ch["p"] = _dot(negb, negb)
    n_sq = max(1, (c - 1).bit_length() - 1)
    for it in range(n_sq):
        for ch in chains:
            tp = _dot(jnp.concatenate([ch["t"], ch["p"]], axis=0).astype(BF16), ch["p"].astype(BF16))
            ch["t"] = ch["t"] + tp[0:c]
            ch["p"] = tp[c:2 * c]
    for ch in chains:
        rhs = jnp.concatenate([ch["v"] * ch["beta"], ch["kb"] * ch["e_gc"]], axis=1).astype(BF16)
        sol = _dot(ch["t"].astype(BF16), rhs)
        ch["u"] = sol[:, 0:DN_D]
        ch["wq"] = jnp.concatenate([sol[:, DN_D:2 * DN_D], ch["q"] * ch["e_gc"]], axis=0).astype(BF16)
        ch["k_tail"] = (ch["k"] * jnp.exp(ch["gtot"] - ch["gc"])).astype(BF16)

    by_key = {(ch["d"], ch["ci"], ch["h"]): ch for ch in chains}
    for step in range(n_sub):
        live = [by_key[d, step if d == 0 else n_sub - 1 - step, h] for d in range(2) for h in range(DN_HEADS)]
        for ch in live:
            ch["st"] = s_scr[ch["d"], ch["h"]]
            ch["ws"] = _dot(ch["wq"], ch["st"].astype(BF16))
        for ch in live:
            vnb = (ch["u"] - ch["ws"][0:c]).astype(BF16)
            o = ch["ws"][c:2 * c] + _dot(ch["a_qk"], vnb)
            dir_refs[ch["d"]][4][ch["rs"], ch["hs"]] = o
            s_scr[ch["d"], ch["h"]] = ch["st"] * jnp.exp(ch["gtot"]) + _dot_tn(ch["k_tail"], vnb)


def _delta(q, k, v, gb, *, ctx_len, rows):
    bsz, t_all, w = q.shape
    n_steps = t_all // rows
    nc = ctx_len // rows

    def bwd(s):
        return jnp.where(s < nc, nc - 1 - s, n_steps + nc - 1 - s)

    fwd_spec = lambda width: pl.BlockSpec((None, rows, width), lambda b, s: (b, s, 0))
    bwd_spec = lambda width: pl.BlockSpec((None, rows, width), lambda b, s: (b, bwd(s), 0))
    return pl.pallas_call(
        functools.partial(_delta_kernel, n_sub=rows // DN_CHUNK),
        grid=(bsz, n_steps),
        in_specs=[fwd_spec(w), fwd_spec(w), fwd_spec(w), fwd_spec(128),
                  bwd_spec(w), bwd_spec(w), bwd_spec(w), bwd_spec(128)],
        out_specs=[fwd_spec(w), bwd_spec(w)],
        out_shape=[jax.ShapeDtypeStruct((bsz, t_all, w), F32)] * 2,
        scratch_shapes=[pltpu.VMEM((2, DN_HEADS, DN_D, DN_D), F32)],
        compiler_params=_params(("arbitrary", "arbitrary"), 32 << 20),
        name="l0_deltanet",
    )(q, k, v, gb, q, k, v, gb)


def _head_norm(y, g):
    return y * lax.rsqrt(jnp.mean(y * y, axis=-1, keepdims=True) + EPS) * g


def _l0_tail_kernel(ha_ref, gay_ref, o0_ref, o1_ref, sz_ref, ctx_ref, x_ref, mod_ref, ng_ref, wo_ref, g_ref, wg_ref,
                    wu_ref, wd_ref, hc_ref, hl_ref, *, n_chunks, ctx_tiles):
    t = pl.program_id(1)
    parts = [(ha_ref[...] * gay_ref[...].astype(F32)).astype(BF16)]
    for hd in range(DN_HEADS):
        lo = hd * DN_D
        ob = o0_ref[:, lo:lo + DN_D] + o1_ref[:, lo:lo + DN_D]
        parts.append((_head_norm(ob, ng_ref[...]) * sz_ref[:, lo:lo + DN_D].astype(F32)).astype(BF16))
    h = jnp.where(t < ctx_tiles, ctx_ref[...], x_ref[...])
    x = h + mod_ref[2:3, :] * _dot(jnp.concatenate(parts, axis=-1), wo_ref[...])

    xm = _normmod(x, g_ref[...], mod_ref[3:4, :], mod_ref[4:5, :]).astype(BF16)
    cw = D_FF // n_chunks
    acc = jnp.zeros(x.shape, F32)
    for ci in range(n_chunks):
        c0 = ci * cw
        gate = _dot(xm, wg_ref[:, c0:c0 + cw])
        up = _dot(xm, wu_ref[:, c0:c0 + cw])
        acc = acc + _dot((_silu(gate) * up).astype(BF16), wd_ref[c0:c0 + cw, :])
    out = x + mod_ref[5:6, :] * acc

    @pl.when(t < ctx_tiles)
    def _():
        hc_ref[...] = out

    @pl.when(t >= ctx_tiles)
    def _():
        hl_ref[...] = out


def _l0_tail(ha, gay, o0, o1, sz, ctx, x, modtab, ng, wo, g, wg, wu, wd, *, tm):
    bsz, ctx_len, d = ctx.shape
    seq = x.shape[1]
    t_all = ctx_len + seq
    ctx_tiles = ctx_len // tm
    tok = lambda width: pl.BlockSpec((None, tm, width), lambda b, t: (b, t, 0))
    return pl.pallas_call(
        functools.partial(_l0_tail_kernel, n_chunks=2, ctx_tiles=ctx_tiles),
        grid=(bsz, t_all // tm),
        in_specs=[tok(512), tok(512), tok(512), tok(512), tok(512)] + _row_specs(tm, d, ctx_tiles, ctx_len, seq, False) + [
                  pl.BlockSpec((None, None, 6, d), lambda b, t: (b, jnp.where(t >= ctx_tiles, 1, 0), 0, 0)),
                  _const_spec((1, DN_D)), _const_spec((d, d)),
                  _const_spec((1, d)), _const_spec((d, D_FF)), _const_spec((d, D_FF)), _const_spec((D_FF, d))],
        out_specs=[pl.BlockSpec((None, tm, d), lambda b, t: (b, jnp.minimum(t, ctx_tiles - 1), 0)),
                   pl.BlockSpec((None, tm, d), lambda b, t: (b, jnp.maximum(t - ctx_tiles, 0), 0))],
        out_shape=[jax.ShapeDtypeStruct((bsz, ctx_len, d), F32), jax.ShapeDtypeStruct((bsz, t_all - ctx_len, d), F32)],
        compiler_params=_params(("arbitrary", "arbitrary"), 48 << 20),
        name="l0_tail",
    )(ha, gay, o0, o1, sz, ctx, x, modtab, ng, wo, g, wg, wu, wd)


def _inproj1_kernel(h_ref, mod_ref, g_ref, w_ref, lbl_ref, wlr_ref, b2_ref, pa_ref, pb_ref, u_scr, *, layer):
    x = h_ref[...]
    xm = _normmod(x, g_ref[...], mod_ref[0:1, :], mod_ref[1:2, :]).astype(BF16)

    lg = lbl_ref[...]
    ex = jnp.exp(lg - jnp.max(lg, axis=0, keepdims=True))
    lbw = ex / jnp.sum(ex, axis=0, keepdims=True)
    lb = jnp.sum(lbw[1:layer + 1], axis=0, keepdims=True)

    def project(c0, width):
        u_scr[:, c0:c0 + width] = _dot(xm, w_ref[:, c0:c0 + width])

    def put(seg, off, val):
        ref = (pa_ref, pb_ref)[seg[0]]
        ref[:, seg[1] * SEG + off:seg[1] * SEG + off + val.shape[1]] = val.astype(ref.dtype)

    def groups(fn):
        for grp in range(SEG // 128):
            fn(grp * 128)

    project(0, SEG)
    groups(lambda c0: put(S_HQ, c0, _silu(u_scr[:, c0:c0 + 128]) * (HG_D ** -0.5)))
    for dr, (sk, sf) in enumerate(((S_HK0, S_HLF0), (S_HK1, S_HLF1))):
        project(512 + dr * 512, SEG)

        def forget(c0, dr=dr, sk=sk, sf=sf):
            lbg = lb[:, c0:c0 + 128]
            fl = u_scr[:, 512 + dr * 512 + c0:512 + dr * 512 + c0 + 128]
            put(sf, c0, jnp.log(lbg + (1.0 - lbg) * _sigmoid(fl)))
            put(sk, c0, (1.0 - lbg) * _sigmoid(-fl))
        groups(forget)
    project(1536, SEG)
    groups(lambda c0: put(S_HV, c0, u_scr[:, 1536 + c0:1536 + c0 + 128]))
    project(2048, SEG)
    groups(lambda c0: put(S_CG, c0, _silu(u_scr[:, 2048 + c0:2048 + c0 + 128])))
    project(2560, SEG)
    put(S_GQK, 0, u_scr[:, 2560:2816] * (GLA_DK ** -0.5))
    put(S_GQK, 256, u_scr[:, 2816:3072])
    project(3072, SEG)
    groups(lambda c0: put(S_GV, c0, u_scr[:, 3072 + c0:3072 + c0 + 128]))
    project(3584, SEG)
    groups(lambda c0: put(S_DG, c0, _silu(u_scr[:, 3584 + c0:3584 + c0 + 128])))
    project(4096, 128)
    lr = u_scr[:, 4096:4224]
    put(S_GLD, 0, -_softplus(-(_dot_hi(lr, wlr_ref[...]) + b2_ref[...])) * (1.0 / GLA_GATE_NORM))


def _inproj1(h, row0, n_rows, modtab, seg, g, w_pad, lbl, wlr, b2, *, tm, layer):
    bsz, _, d = h.shape
    t0 = row0 // tm
    return pl.pallas_call(
        functools.partial(_inproj1_kernel, layer=layer),
        grid=(bsz, n_rows // tm),
        in_specs=[pl.BlockSpec((None, tm, d), lambda b, t: (b, t0 + t, 0)),
                  pl.BlockSpec((None, None, 6, d), lambda b, t: (b, seg, 0, 0)),
                  _const_spec((1, d)), _const_spec((d, O_IN_PAD)), _const_spec(lbl.shape),
                  _const_spec((128, SEG)), _const_spec((1, SEG))],
        out_specs=[pl.BlockSpec((None, tm, n * SEG), lambda b, t: (b, t, 0)) for n in N_SEG],
        out_shape=[jax.ShapeDtypeStruct((bsz, n_rows, n * SEG), dt) for n, dt in zip(N_SEG, P1_DTYPES)],
        scratch_shapes=[pltpu.VMEM((tm, O_IN_PAD), F32)],
        compiler_params=_params(("arbitrary", "arbitrary"), 44 << 20),
        name="l1_inproj",
    )(h, modtab, g, w_pad, lbl, wlr, b2)


def _gla_stream(d, q_all, k_all, ld_all, v_all, st_ref, o_ref, o_lane0, r0, n_heads, dk, dv, incl, m_incl):
    c = k_all.shape[0]
    mid = c // 2 - 1 if d == 0 else c // 2
    last = c - 1 if d == 0 else 0
    bc = _dot_hi(m_incl, ld_all)
    m = bc[mid:mid + 1]
    btot = bc[last:last + 1]
    kn = k_all.astype(F32) * jnp.exp(m - bc)
    it = dict(d=d, r0=r0, st_ref=st_ref, o_ref=o_ref, o_lane0=o_lane0, n_heads=n_heads, dk=dk, dv=dv, incl=incl,
              c=c, kt=(kn * jnp.exp(btot - m)).astype(BF16), dec=jnp.exp(btot), v=v_all.astype(BF16),
              want_out=q_all is not None)
    if q_all is not None:
        qe = q_all.astype(F32) * jnp.exp(bc)
        it.update(qd=(qe * jnp.exp(-m)).astype(BF16), qe=qe.astype(BF16), knb=kn.astype(BF16))
    return it


def _gla_intra(it):
    dk = it["dk"]
    it["a"] = [jnp.where(it["incl"], _dot_nt(it["qd"][:, hd * dk:(hd + 1) * dk], it["knb"][:, hd * dk:(hd + 1) * dk]),
                         0.0).astype(BF16) for hd in range(it["n_heads"])]


def _gla_advance(it):
    d, dk, dv, c, st_ref = it["d"], it["dk"], it["dv"], it["c"], it["st_ref"]
    sts = [st_ref[d, hd] for hd in range(it["n_heads"])]
    if it["want_out"]:
        for hd in range(it["n_heads"]):
            v = it["v"][:, hd * dv:(hd + 1) * dv]
            o = _dot(it["a"][hd], v) + _dot_nt(it["qe"][:, hd * dk:(hd + 1) * dk], sts[hd].astype(BF16))
            it["o_ref"][it["r0"]:it["r0"] + c, it["o_lane0"] + hd * dv:it["o_lane0"] + (hd + 1) * dv] = o
    for hd in range(it["n_heads"]):
        ks = slice(hd * dk, (hd + 1) * dk)
        st_ref[d, hd] = sts[hd] * it["dec"][:, ks] + _dot_tn(it["v"][:, hd * dv:(hd + 1) * dv], it["kt"][:, ks])


def _mix1_body(dirs, n_sub, sh_ref, sg_ref):
    c = MIX1_CHUNK
    gw = GLA_HEADS * GLA_DK
    prepared = {}
    for d, (hq_ref, hv_ref, hk_ref, hlf_ref, gv_ref, gqk_ref, gld_ref, o_ref) in enumerate(dirs):
        incl = _scan_masks(c, d)[0]
        m_incl = jnp.where(incl, 1.0, 0.0)
        for ci in range(n_sub):
            r0 = ci * c
            rs = slice(r0, r0 + c)
            prepared[d, ci, 0] = _gla_stream(
                d, None if hq_ref is None else hq_ref[rs, :], hk_ref[rs, :], hlf_ref[rs, :], hv_ref[rs, :],
                sh_ref, o_ref, 0, r0, HG_HEADS, HG_D, HG_D, incl, m_incl)
            prepared[d, ci, 1] = _gla_stream(
                d, None if hq_ref is None else gqk_ref[rs, 0:gw], gqk_ref[rs, gw:2 * gw],
                gld_ref[rs, d * gw:(d + 1) * gw], gv_ref[rs, :],
                sg_ref, o_ref, HG_HEADS * HG_D, r0, GLA_HEADS, GLA_DK, GLA_DV, incl, m_incl)
    for it in prepared.values():
        if it["want_out"]:
            _gla_intra(it)
    for step in range(n_sub):
        for d in range(2):
            for stream in range(2):
                _gla_advance(prepared[d, step if d == 0 else n_sub - 1 - step, stream])


def _mix1_ctx_kernel(*refs, n_sub):
    fwd, bwd, (sh_ref, sg_ref) = refs[0:6], refs[6:12], refs[12:14]

    @pl.when(pl.program_id(1) == 0)
    def _():
        sh_ref[...] = jnp.zeros_like(sh_ref)
        sg_ref[...] = jnp.zeros_like(sg_ref)

    _mix1_body([(None,) + tuple(r) + (None,) for r in (fwd, bwd)], n_sub, sh_ref, sg_ref)


def _mix1_lat_kernel(*refs, n_sub):
    fwd, bwd = refs[0:7], refs[7:14]
    sh0_ref, sg0_ref, of_ref, ob_ref, sh_scr, sg_scr = refs[14:20]

    @pl.when(pl.program_id(1) == 0)
    def _():
        sh_scr[...] = sh0_ref[...]
        sg_scr[...] = sg0_ref[...]

    _mix1_body([tuple(fwd) + (of_ref,), tuple(bwd) + (ob_ref,)], n_sub, sh_scr, sg_scr)


def _mix1_specs(p1, rows, n_steps, segs_of_dir):
    specs, args = [], []
    for d in range(2):
        blk = (lambda b, s: s) if d == 0 else (lambda b, s: n_steps - 1 - s)
        for arr, sg in segs_of_dir(d):
            specs.append(pl.BlockSpec((None, rows, SEG), lambda b, s, blk=blk, sg=sg: (b, blk(b, s), sg)))
            args.append(p1[arr])
    return specs, args


_SH_SHAPE = (2, HG_HEADS, HG_D, HG_D)
_SG_SHAPE = (2, GLA_HEADS, GLA_DV, GLA_DK)


def _mix1_ctx(p1c, *, rows):
    bsz, ctx_len, _ = p1c[0].shape
    n_steps = ctx_len // rows
    segs = lambda d: (S_HV, (S_HK0, S_HK1)[d], (S_HLF0, S_HLF1)[d], S_GV, S_GQK, S_GLD)
    specs, args = _mix1_specs(p1c, rows, n_steps, segs)
    state = lambda shape: pl.BlockSpec((None,) + shape, lambda b, s: (b, 0, 0, 0, 0))
    return pl.pallas_call(
        functools.partial(_mix1_ctx_kernel, n_sub=rows // MIX1_CHUNK),
        grid=(bsz, n_steps),
        in_specs=specs,
        out_specs=[state(_SH_SHAPE), state(_SG_SHAPE)],
        out_shape=[jax.ShapeDtypeStruct((bsz,) + _SH_SHAPE, F32), jax.ShapeDtypeStruct((bsz,) + _SG_SHAPE, F32)],
        compiler_params=_params(("arbitrary", "arbitrary"), 32 << 20),
        name="l1_ctx_state",
    )(*args)


def _mix1_lat(p1l, sh0, sg0, *, rows):
    bsz, seq, _ = p1l[0].shape
    n_steps = seq // rows
    segs = lambda d: (S_HQ, S_HV, (S_HK0, S_HK1)[d], (S_HLF0, S_HLF1)[d], S_GV, S_GQK, S_GLD)
    specs, args = _mix1_specs(p1l, rows, n_steps, segs)
    state = lambda shape: pl.BlockSpec((None,) + shape, lambda b, s: (b, 0, 0, 0, 0))
    ow = HG_HEADS * HG_D + GLA_HEADS * GLA_DV
    return pl.pallas_call(
        functools.partial(_mix1_lat_kernel, n_sub=rows // MIX1_CHUNK),
        grid=(bsz, n_steps),
        in_specs=specs + [state(_SH_SHAPE), state(_SG_SHAPE)],
        out_specs=[pl.BlockSpec((None, rows, ow), lambda b, s: (b, s, 0)),
                   pl.BlockSpec((None, rows, ow), lambda b, s: (b, n_steps - 1 - s, 0))],
        out_shape=[jax.ShapeDtypeStruct((bsz, seq, ow), F32)] * 2,
        scratch_shapes=[pltpu.VMEM(_SH_SHAPE, F32), pltpu.VMEM(_SG_SHAPE, F32)],
        compiler_params=_params(("arbitrary", "arbitrary"), 32 << 20),
        name="l1_scan",
    )(*args, sh0, sg0)


def _outproj1_kernel(o0_ref, o1_ref, gate_ref, h_ref, mod_ref, cng_ref, dng_ref, w_ref, out_ref):
    parts = []
    for hd in range(HG_HEADS + GLA_HEADS):
        lo = hd * 128
        y = o0_ref[:, lo:lo + 128] + o1_ref[:, lo:lo + 128]
        ng = cng_ref[...] if hd < HG_HEADS else dng_ref[...]
        parts.append((_head_norm(y, ng) * gate_ref[:, lo:lo + 128].astype(F32)).astype(BF16))
    y = _dot(jnp.concatenate(parts, axis=-1), w_ref[...])
    out_ref[...] = h_ref[...] + mod_ref[2:3, :] * y


def _outproj1(o0, o1, p1l, h, modtab, cng, dng, w, *, tm):
    bsz, seq, d = h.shape
    tok = pl.BlockSpec((None, tm, d), lambda b, t: (b, t, 0))
    return pl.pallas_call(
        _outproj1_kernel,
        grid=(bsz, seq // tm),
        in_specs=[tok, tok,
                  pl.BlockSpec((None, tm, 2 * SEG), lambda b, t: (b, t, S_CG[1] // 2)),
                  tok,
                  pl.BlockSpec((None, None, 6, d), lambda b, t: (b, 1, 0, 0)),
                  _const_spec((1, 128)), _const_spec((1, 128)), _const_spec((d, d))],
        out_specs=tok,
        out_shape=jax.ShapeDtypeStruct((bsz, seq, d), F32),
        compiler_params=_params(("arbitrary", "arbitrary"), 32 << 20),
        name="l1_outproj",
    )(o0, o1, p1l[S_CG[0]], h, modtab, cng, dng, w)


def _moe_route_kernel(h_ref, mod_ref, ng_ref, rwt_ref, rb_ref, slot_ref, seg_ref, xs_hbm,
                      xn_scr, xg_scr, zero_scr, base_smem, sem, *, tk):
    i = pl.program_id(0)
    pc = MOE_PIECE
    gr = MOE_GATHER_ROWS

    @pl.when(i == 0)
    def _():
        for e in range(N_EXPERTS):
            base_smem[e] = 0
        zero_scr[...] = jnp.zeros_like(zero_scr)

    xm = _normmod(h_ref[...], ng_ref[...], mod_ref[3:4, :], mod_ref[4:5, :])
    xn_scr[...] = xm.astype(BF16)
    lg = lax.dot_general(rwt_ref[...], xm, (((1,), (1,)), ((), ())), precision=HI,
                         preferred_element_type=F32) + rb_ref[...]
    eidx = lax.broadcasted_iota(jnp.int32, lg.shape, 0).astype(F32)
    m1 = jnp.max(lg, axis=0, keepdims=True)
    i1 = jnp.min(jnp.where(lg == m1, eidx, float(N_EXPERTS)), axis=0, keepdims=True)
    lg2 = jnp.where(eidx == i1, -jnp.inf, lg)
    m2 = jnp.max(lg2, axis=0, keepdims=True)
    i2 = jnp.min(jnp.where(lg2 == m2, eidx, float(N_EXPERTS)), axis=0, keepdims=True)
    ex = jnp.exp(m2 - m1)
    p1 = 1.0 / (1.0 + ex)
    sel = jnp.where(eidx == i1, 1.0, 0.0) + jnp.where(eidx == i2, 1.0, 0.0)
    lane = lax.broadcasted_iota(jnp.int32, lg.shape, 1)
    cum = sel
    sh = 1
    while sh < tk:
        cum = cum + jnp.where(lane >= sh, pltpu.roll(cum, sh, 1), 0.0)
        sh *= 2
    padded = jnp.floor((cum[:, tk - 1:tk] + (pc - 1.0)) * (1.0 / pc)) * pc
    padded = jnp.broadcast_to(padded, (N_EXPERTS, 128))
    er = lax.broadcasted_iota(jnp.int32, (N_EXPERTS, N_EXPERTS), 0)
    ec = lax.broadcasted_iota(jnp.int32, (N_EXPERTS, N_EXPERTS), 1)
    off = _dot_hi(jnp.where(er > ec, 1.0, 0.0), padded)
    slot = off[:, 0:1] + cum - 1.0
    slot_a = jnp.sum(jnp.where(eidx == i1, slot, 0.0), axis=0, keepdims=True)
    slot_b = jnp.sum(jnp.where(eidx == i2, slot, 0.0), axis=0, keepdims=True)
    slot_ref[...] = jnp.concatenate([slot_a, slot_b, p1, ex * p1, jnp.zeros((4, tk), F32)], axis=0)

    total = jnp.max(off[N_EXPERTS - 1:N_EXPERTS, :] + padded[N_EXPERTS - 1:N_EXPERTS, :]).astype(jnp.int32)

    def gather(ci, carry):
        r0 = pl.multiple_of(ci * gr, gr)
        rid = (lax.broadcasted_iota(jnp.int32, (gr, tk), 0) + r0).astype(F32)
        p = jnp.where(rid == slot_a, 1.0, 0.0) + jnp.where(rid == slot_b, 1.0, 0.0)
        xg_scr[pl.ds(r0, gr), :] = _dot(p.astype(BF16), xn_scr[...]).astype(BF16)
        return carry
    lax.fori_loop(0, (total + gr - 1) // gr, gather, 0)

    erow = lax.broadcasted_iota(jnp.int32, (N_EXPERTS, 128), 0)
    base_vec = jnp.zeros((N_EXPERTS, 128), F32)
    segs = []
    for e in range(N_EXPERTS):
        off_e = jnp.max(off[e:e + 1, :]).astype(jnp.int32)
        len_e = jnp.max(padded[e:e + 1, :]).astype(jnp.int32)
        base_e = base_smem[e]
        base_vec = jnp.where(erow == e, base_e.astype(F32), base_vec)
        segs.append((e, off_e, len_e, base_e))
    seg_ref[0] = off
    seg_ref[1] = padded
    seg_ref[2] = base_vec

    def seg_copy(e, off_e, base_e, p):
        return pltpu.make_async_copy(
            xg_scr.at[pl.ds(pl.multiple_of(off_e + p * pc, pc), pc), :],
            xs_hbm.at[e, pl.ds(pl.multiple_of(base_e + p * pc, pc), pc), :], sem)

    for e, off_e, len_e, base_e in segs:
        def start(p, carry, e=e, off_e=off_e, base_e=base_e):
            seg_copy(e, off_e, base_e, p).start()
            return carry
        lax.fori_loop(0, len_e // pc, start, 0)
    for e, off_e, len_e, base_e in segs:
        def wait(p, carry, e=e, off_e=off_e, base_e=base_e):
            seg_copy(e, off_e, base_e, p).wait()
            return carry
        lax.fori_loop(0, len_e // pc, wait, 0)
        base_smem[e] = base_e + len_e

    @pl.when(i == pl.num_programs(0) - 1)
    def _():
        def tail_copy(e, p):
            end = base_smem[e]
            return pltpu.make_async_copy(zero_scr, xs_hbm.at[e, pl.ds(pl.multiple_of(end + p * pc, pc), pc), :], sem)

        def n_tail(e):
            rem = lax.rem(base_smem[e], MOE_BLOCK)
            return jnp.where(rem == 0, 0, MOE_BLOCK - rem) // pc

        for e in range(N_EXPERTS):
            def start(p, carry, e=e):
                tail_copy(e, p).start()
                return carry
            lax.fori_loop(0, n_tail(e), start, 0)
        for e in range(N_EXPERTS):
            def wait(p, carry, e=e):
                tail_copy(e, p).wait()
                return carry
            lax.fori_loop(0, n_tail(e), wait, 0)


def _moe_ffn_kernel(eid_ref, blk_ref, nv_ref, x_ref, wg_ref, wu_ref, wd_ref, o_ref, *, n_chunks):
    del eid_ref, blk_ref

    @pl.when(pl.program_id(0) < nv_ref[0])
    def _():
        x = x_ref[...]
        cw = D_FF // n_chunks
        acc = jnp.zeros(x.shape, F32)
        for ci in range(n_chunks):
            c0 = ci * cw
            act = (_silu(_dot(x, wg_ref[:, c0:c0 + cw])) * _dot(x, wu_ref[:, c0:c0 + cw])).astype(BF16)
            acc = acc + _dot(act, wd_ref[c0:c0 + cw, :])
        o_ref[...] = acc.astype(BF16)


def _moe_combine_kernel(base_ref, len_ref, off_ref, h_ref, mod_ref, fg_ref, slot_ref, og_hbm, out_ref,
                        og_scr, sem, *, tk):
    i = pl.program_id(0)
    pc = MOE_PIECE
    gr = MOE_GATHER_ROWS
    n_rows = og_scr.shape[0]

    def seg_copy(e, p):
        return pltpu.make_async_copy(
            og_hbm.at[e, pl.ds(pl.multiple_of(base_ref[i * N_EXPERTS + e] + p * pc, pc), pc), :],
            og_scr.at[pl.ds(pl.multiple_of(off_ref[i * N_EXPERTS + e] + p * pc, pc), pc), :], sem)

    for e in range(N_EXPERTS):
        def start(p, carry, e=e):
            seg_copy(e, p).start()
            return carry
        lax.fori_loop(0, len_ref[i * N_EXPERTS + e] // pc, start, 0)

    last = i * N_EXPERTS + N_EXPERTS - 1
    total = off_ref[last] + len_ref[last]

    def clear(p, carry):
        og_scr[pl.ds(pl.multiple_of(p * pc, pc), pc), :] = jnp.zeros((pc, og_scr.shape[1]), BF16)
        return carry
    lax.fori_loop(total // pc, n_rows // pc, clear, 0)

    for e in range(N_EXPERTS):
        def wait(p, carry, e=e):
            seg_copy(e, p).wait()
            return carry
        lax.fori_loop(0, len_ref[i * N_EXPERTS + e] // pc, wait, 0)

    out_ref[...] = jnp.zeros_like(out_ref)

    def scatter(ci, carry):
        r0 = pl.multiple_of(ci * gr, gr)
        rid = (lax.broadcasted_iota(jnp.int32, (gr, tk), 0) + r0).astype(F32)
        pa = jnp.where(rid == slot_ref[0:1, :], 1.0, 0.0)
        pb = jnp.where(rid == slot_ref[1:2, :], 1.0, 0.0)
        gcol = jnp.sum(pa * slot_ref[2:3, :] + pb * slot_ref[3:4, :], axis=1, keepdims=True)
        og = (og_scr[pl.ds(r0, gr), :].astype(F32) * gcol).astype(BF16)
        out_ref[...] = out_ref[...] + _dot_tn((pa + pb).astype(BF16), og)
        return carry
    lax.fori_loop(0, (total + gr - 1) // gr, scatter, 0)
    h3 = h_ref[...] + mod_ref[5:6, :] * out_ref[...]
    out_ref[...] = h3 * lax.rsqrt(jnp.mean(h3 * h3, axis=-1, keepdims=True) + EPS) * fg_ref[...]


def _moe_block_table(seg, n_blocks):
    ends = (seg[-1, 2, :, 0] + seg[-1, 1, :, 0]).astype(jnp.int32)
    nblk = (ends + MOE_BLOCK - 1) // MOE_BLOCK
    cum = jnp.cumsum(nblk)
    n_valid = cum[-1]
    g = jnp.minimum(jnp.arange(n_blocks, dtype=jnp.int32), n_valid - 1)
    eid = jnp.sum((g[:, None] >= cum[None, :]).astype(jnp.int32), axis=1)
    blk = g - (cum - nblk)[eid]
    return eid, blk, n_valid.reshape(1)


def _moe(h, modtab, ng, fg, rwt, rb, wg, wu, wd, *, tk):
    bsz, seq, d = h.shape
    tpb = seq // tk
    n_tiles = bsz * tpb
    n_tok = bsz * seq
    tile_rows = -(-(2 * tk + N_EXPERTS * MOE_PIECE) // MOE_GATHER_ROWS) * MOE_GATHER_ROWS
    cap = -(-(n_tok + n_tiles * MOE_PIECE) // MOE_BLOCK) * MOE_BLOCK
    n_blocks = -(-(2 * n_tok + n_tiles * N_EXPERTS * MOE_PIECE) // MOE_BLOCK) + N_EXPERTS
    tok = lambda i, *_: (i // tpb, i % tpb, 0)
    mod = lambda i, *_: (i // tpb, 1, 0, 0)

    slots, seg, xs = pl.pallas_call(
        functools.partial(_moe_route_kernel, tk=tk),
        grid=(n_tiles,),
        in_specs=[pl.BlockSpec((None, tk, d), tok), pl.BlockSpec((None, None, 6, d), mod),
                  _const_spec((1, d)), _const_spec((N_EXPERTS, d)), _const_spec((N_EXPERTS, 1))],
        out_specs=[pl.BlockSpec((None, 8, tk), lambda i: (i, 0, 0)),
                   pl.BlockSpec((None, 3, N_EXPERTS, 128), lambda i: (i, 0, 0, 0)),
                   pl.BlockSpec(memory_space=pl.ANY)],
        out_shape=[jax.ShapeDtypeStruct((n_tiles, 8, tk), F32),
                   jax.ShapeDtypeStruct((n_tiles, 3, N_EXPERTS, 128), F32),
                   jax.ShapeDtypeStruct((N_EXPERTS, cap, d), BF16)],
        scratch_shapes=[pltpu.VMEM((tk, d), BF16), pltpu.VMEM((tile_rows, d), BF16), pltpu.VMEM((MOE_PIECE, d), BF16),
                        pltpu.SMEM((N_EXPERTS,), jnp.int32), pltpu.SemaphoreType.DMA(())],
        compiler_params=_params(("arbitrary",), 40 << 20),
        name="l1_moe_route",
    )(h, modtab, ng, rwt, rb)

    eid, blk, n_valid = _moe_block_table(seg, n_blocks)
    x_spec = pl.BlockSpec((None, MOE_BLOCK, d), lambda g, eid, blk, nv: (eid[g], blk[g], 0))
    og = pl.pallas_call(
        functools.partial(_moe_ffn_kernel, n_chunks=11),
        grid_spec=pltpu.PrefetchScalarGridSpec(
            num_scalar_prefetch=3, grid=(n_blocks,),
            in_specs=[x_spec,
                      pl.BlockSpec((None, d, D_FF), lambda g, eid, blk, nv: (eid[g], 0, 0)),
                      pl.BlockSpec((None, d, D_FF), lambda g, eid, blk, nv: (eid[g], 0, 0)),
                      pl.BlockSpec((None, D_FF, d), lambda g, eid, blk, nv: (eid[g], 0, 0))],
            out_specs=x_spec),
        out_shape=jax.ShapeDtypeStruct((N_EXPERTS, cap, d), BF16),
        compiler_params=_params(("arbitrary",), 52 << 20),
        name="l1_moe_experts",
    )(eid, blk, n_valid, xs, wg, wu, wd)

    tab = lambda k: seg[:, k, :, 0].astype(jnp.int32).reshape(-1)
    return pl.pallas_call(
        functools.partial(_moe_combine_kernel, tk=tk),
        grid_spec=pltpu.PrefetchScalarGridSpec(
            num_scalar_prefetch=3, grid=(n_tiles,),
            in_specs=[pl.BlockSpec((None, tk, d), tok), pl.BlockSpec((None, None, 6, d), mod),
                      pl.BlockSpec((1, d), lambda i, *_: (0, 0)),
                      pl.BlockSpec((None, 8, tk), lambda i, *_: (i, 0, 0)),
                      pl.BlockSpec(memory_space=pl.ANY)],
            out_specs=pl.BlockSpec((None, tk, d), tok),
            scratch_shapes=[pltpu.VMEM((tile_rows, d), BF16), pltpu.SemaphoreType.DMA(())]),
        out_shape=jax.ShapeDtypeStruct(h.shape, F32),
        compiler_params=_params(("arbitrary",), 40 << 20),
        name="l1_moe_combine",
    )(tab(2), tab(1), tab(0), h, modtab, fg, slots, og)


def _block_diag_gate(gate_w):
    w = gate_w.reshape(2, 2, 2, 4, RG_BLOCK, RG_BLOCK)
    eye = jnp.eye(4, dtype=gate_w.dtype)
    return jnp.einsum('dghbij,bc->dghbicj', w, eye).reshape(2, 2, 2, 256, 256)


def _pad_cols(w, n):
    return jnp.pad(w, ((0, 0), (0, n - w.shape[1])))


def _layer0(ctx, x, modtab, norm_mix_g, norm_ffn_g, e_w_in, e_w_out, e_a_conv_w, e_a_conv_b, e_a_gate_w, e_a_gate_b,
            e_a_lambda, e_b_conv_w, e_b_a_log, e_b_dt_bias, e_b_norm_g, e_ffn_w_gate, e_ffn_w_up, e_ffn_w_down,
            *, tm, tt):
    bsz, ctx_len, d = ctx.shape
    w_in = _pad_cols(e_w_in, E_IN_PAD).astype(BF16)
    gpar = jnp.zeros((2, 128), F32)
    gpar = gpar.at[0, 2 * DN_HEADS:4 * DN_HEADS].set(e_b_a_log.reshape(-1))
    gpar = gpar.at[1, 2 * DN_HEADS:4 * DN_HEADS].set(e_b_dt_bias.reshape(-1))
    ua, gay, q, k, v, sz, gb = _inproj0(ctx, x, modtab, norm_mix_g.reshape(1, d), w_in, e_a_conv_w,
                                        e_a_conv_b.reshape(1, -1), e_b_conv_w, gpar, tm=tm)
    wg = _block_diag_gate(e_a_gate_w).astype(BF16)
    hf, hb = _rglru(jnp.transpose(ua, (1, 0, 2)), wg, e_a_gate_b.reshape(4, RG_WIDTH), e_a_lambda,
                    tt=tt, ctx_len=ctx_len)
    ha = jnp.transpose(hf + hb, (1, 0, 2))
    o0, o1 = _delta(q, k, v, gb, ctx_len=ctx_len, rows=SCAN_ROWS)
    return _l0_tail(ha, gay, o0, o1, sz, ctx, x, modtab, e_b_norm_g.reshape(1, -1), e_w_out.astype(BF16),
                    norm_ffn_g.reshape(1, d), e_ffn_w_gate.astype(BF16), e_ffn_w_up.astype(BF16),
                    e_ffn_w_down.astype(BF16), tm=tm)


def _layer1(hc, hl, modtab, norm_mix_g, norm_ffn_g, final_norm_g, o_w_in, o_w_out, o_lb_logits, o_c_norm_g,
            o_d_gate_w2, o_d_gate_b2, o_d_norm_g, o_router_w, o_router_b, o_moe_w_gate, o_moe_w_up, o_moe_w_down,
            *, tm, tk, layer):
    bsz, seq, d = hl.shape
    ctx_len = hc.shape[1]
    rows = seq // GRID_W
    hl = hl.reshape(bsz, rows, GRID_W, d).swapaxes(1, 2).reshape(bsz, seq, d)
    w_in = _pad_cols(o_w_in, O_IN_PAD).astype(BF16)
    wlr = jnp.zeros((128, SEG), F32)
    wlr = wlr.at[0:GLA_RANK, 0:256].set(o_d_gate_w2[0]).at[GLA_RANK:2 * GLA_RANK, 256:512].set(o_d_gate_w2[1])
    proj = functools.partial(_inproj1, g=norm_mix_g.reshape(1, d), w_pad=w_in, lbl=o_lb_logits, wlr=wlr,
                             b2=o_d_gate_b2.reshape(1, SEG), layer=layer)
    p1c = proj(hc, 0, ctx_len, modtab, 0, tm=tm)
    p1l = proj(hl, 0, seq, modtab, 1, tm=2 * tm)
    sh0, sg0 = _mix1_ctx(p1c, rows=SCAN_ROWS)
    o0, o1 = _mix1_lat(p1l, sh0, sg0, rows=SCAN_ROWS)
    h2 = _outproj1(o0, o1, p1l, hl, modtab, o_c_norm_g.reshape(1, -1), o_d_norm_g.reshape(1, -1),
                   o_w_out.astype(BF16), tm=tm)
    return _moe(h2, modtab, norm_ffn_g.reshape(1, d), final_norm_g.reshape(1, d), o_router_w.T,
                o_router_b.reshape(N_EXPERTS, 1), o_moe_w_gate.astype(BF16), o_moe_w_up.astype(BF16),
                o_moe_w_down.astype(BF16), tk=tk)


def kernel(x, c, ctx, c_ctx, ada_w, ada_b, norm_mix_g, norm_ffn_g, final_norm_g, e_w_in, e_w_out, e_a_conv_w, e_a_conv_b, e_a_gate_w, e_a_gate_b, e_a_lambda, e_b_conv_w, e_b_a_log, e_b_dt_bias, e_b_norm_g, e_ffn_w_gate, e_ffn_w_up, e_ffn_w_down, o_w_in, o_w_out, o_lb_logits, o_c_norm_g, o_d_gate_w2, o_d_gate_b2, o_d_norm_g, o_router_w, o_router_b, o_moe_w_gate, o_moe_w_up, o_moe_w_down):
    bsz, seq, d = x.shape
    ctx_len = ctx.shape[1]
    assert bsz == 8 and d == D_MODEL and ada_w.shape[0] == 2
    tm = min(256, ctx_len)
    tt = min(128, ctx_len)
    tk = min(512, seq)
    assert ctx_len % tm == 0 and seq % tm == 0 and ctx_len % SCAN_ROWS == 0 and seq % SCAN_ROWS == 0
    assert seq % GRID_W == 0 and seq % tk == 0

    mods = _ada(c, c_ctx, ada_w, ada_b)
    hc, hl = _layer0(ctx, x, _modtab(mods[0], bsz), norm_mix_g[0], norm_ffn_g[0], e_w_in[0], e_w_out[0],
                     e_a_conv_w[0], e_a_conv_b[0], e_a_gate_w[0], e_a_gate_b[0], e_a_lambda[0], e_b_conv_w[0],
                     e_b_a_log[0], e_b_dt_bias[0], e_b_norm_g[0], e_ffn_w_gate[0], e_ffn_w_up[0], e_ffn_w_down[0],
                     tm=tm, tt=tt)
    out_cm = _layer1(hc, hl, _modtab(mods[1], bsz), norm_mix_g[1], norm_ffn_g[1], final_norm_g, o_w_in[0],
                     o_w_out[0], o_lb_logits, o_c_norm_g[0], o_d_gate_w2[0], o_d_gate_b2[0], o_d_norm_g[0],
                     o_router_w[0], o_router_b[0], o_moe_w_gate[0], o_moe_w_up[0], o_moe_w_down[0],
                     tm=tm, tk=tk, layer=1)
    rows = seq // GRID_W
    return out_cm.reshape(bsz, GRID_W, rows, d).swapaxes(1, 2).reshape(bsz, seq, d)
```

```python
import functools

import jax
import jax.numpy as jnp
from jax import lax
from jax.experimental import pallas as pl
from jax.experimental.pallas import tpu as pltpu

F32 = jnp.float32
BF16 = jnp.bfloat16
HI = lax.Precision.HIGHEST

EPS = 1e-6
D_MODEL = 1024
GRID_W = 64
CONV_K = 4
RG_WIDTH = 512
RG_BLOCK = 64
RG_C = 8.0
DN_HEADS = 4
DN_D = 128
DN_CHUNK = 64
HG_HEADS = 4
HG_D = 128
GLA_HEADS = 4
GLA_DK = 64
GLA_DV = 128
GLA_RANK = 16
GLA_GATE_NORM = 16.0
MIX1_CHUNK = 64
SCAN_ROWS = 256
D_FF = 2816
N_EXPERTS = 8

E_IN_PAD = 3200
O_IN_PAD = 4224
SEG = 512
S_HQ, S_HV, S_HK0, S_HK1, S_GV, S_GQK, S_CG, S_DG = [(0, i) for i in range(8)]
S_HLF0, S_HLF1, S_GLD = [(1, i) for i in range(3)]
N_SEG = (8, 3)
P1_DTYPES = (BF16, F32)

V7X_VMEM_BYTES = 64 * 1024 * 1024
VMEM_HEADROOM_BYTES = 8 * 1024 * 1024
MOE_PIECE = 16
MOE_SEG_PIECES = (64, MOE_PIECE)
MOE_BLOCK = 512
MOE_GATHER_ROWS = 256


def _vmem(nbytes):
    return int(min(V7X_VMEM_BYTES - VMEM_HEADROOM_BYTES, nbytes))


def _params(sem, vmem_bytes):
    return pltpu.CompilerParams(dimension_semantics=sem, vmem_limit_bytes=_vmem(vmem_bytes))


def _sigmoid(x):
    return jax.nn.sigmoid(x)


def _sigmoid_tanh(x):
    return 0.5 * jnp.tanh(0.5 * x) + 0.5


def _silu(x):
    return x * jax.nn.sigmoid(x)


def _softplus(x):
    return jnp.maximum(x, 0.0) + jnp.log1p(jnp.exp(-jnp.abs(x)))


def _gelu_tanh(x):
    return 0.5 * x * (1.0 + jnp.tanh(0.7978845608028654 * (x + 0.044715 * (x * x * x))))


def _normmod(x, g, shift, scale):
    y = x * lax.rsqrt(jnp.mean(x * x, axis=-1, keepdims=True) + EPS)
    return (y * g) * (1.0 + scale) + shift


def _dot(a, b):
    return jnp.dot(a, b, preferred_element_type=F32)


def _dot_nt(a, b):
    return lax.dot_general(a, b, (((1,), (1,)), ((), ())), preferred_element_type=F32)


def _dot_tn(a, b):
    return lax.dot_general(a, b, (((0,), (0,)), ((), ())), preferred_element_type=F32)


def _dot_hi(a, b):
    return jnp.dot(a, b, precision=HI, preferred_element_type=F32)


def _const_spec(shape):
    nd = len(shape)
    return pl.BlockSpec(shape, lambda *_: (0,) * nd, pipeline_mode=pl.Buffered(1))


def _scan_masks(c, d):
    row = lax.broadcasted_iota(jnp.int32, (c, c), 0)
    col = lax.broadcasted_iota(jnp.int32, (c, c), 1)
    dlt = row - col if d == 0 else col - row
    return dlt >= 0, dlt > 0, dlt <= 0, row == col


def _ada_kernel(cv_ref, w_ref, b_ref, o_ref):
    s = _silu(cv_ref[...]).astype(BF16)
    o_ref[...] = _dot(s, w_ref[...].astype(BF16)) + b_ref[...]


def _ada(c, c_ctx, ada_w, ada_b):
    depth, d, n6 = ada_w.shape
    bsz = c.shape[0]
    rows = 16
    cv = jnp.zeros((rows, d), F32).at[:bsz].set(c).at[bsz].set(c_ctx)
    tn = 1536
    return pl.pallas_call(
        _ada_kernel,
        grid=(depth, n6 // tn),
        in_specs=[pl.BlockSpec((rows, d), lambda l, j: (0, 0)),
                  pl.BlockSpec((None, d, tn), lambda l, j: (l, 0, j)),
                  pl.BlockSpec((None, 1, tn), lambda l, j: (l, 0, j))],
        out_specs=pl.BlockSpec((None, rows, tn), lambda l, j: (l, 0, j)),
        out_shape=jax.ShapeDtypeStruct((depth, rows, n6), F32),
        compiler_params=_params(("arbitrary", "arbitrary"), 32 << 20),
        name="ada_mod",
    )(cv, ada_w, ada_b.reshape(depth, 1, n6))


def _modtab(mods_l, bsz):
    m = mods_l.reshape(mods_l.shape[0], 6, D_MODEL)
    lat = m[:bsz]
    ctx = jnp.broadcast_to(m[bsz][None], (bsz, 6, D_MODEL))
    return jnp.stack([ctx, lat], axis=1)


def _inproj0_kernel(cp_ref, cm_ref, cn_ref, xp_ref, xm_ref, xn_ref, mod_ref, g_ref, w_ref, acw_ref, acb_ref, bcw_ref,
                    gpar_ref, ua_ref, gay_ref, q_ref, k_ref, v_ref, sz_ref, gb_ref, u_scr, *, tm, ctx_tiles, n_tiles):
    t = pl.program_id(1)
    pick = lambda c_ref, x_ref: jnp.where(t < ctx_tiles, c_ref[...], x_ref[...])
    x = jnp.concatenate([pick(cp_ref, xp_ref), pick(cm_ref, xm_ref), pick(cn_ref, xn_ref)], axis=0)
    xm = _normmod(x, g_ref[...], mod_ref[0:1, :], mod_ref[1:2, :]).astype(BF16)
    seg_first = jnp.logical_or(t == 0, t == ctx_tiles)
    seg_last = jnp.logical_or(t == ctx_tiles - 1, t == n_tiles - 1)

    def project(c0, width, conv_input):
        u_scr[:, c0:c0 + width] = _dot(xm, w_ref[:, c0:c0 + width])
        if conv_input:
            u_scr[0:8, c0:c0 + width] = jnp.where(seg_first, 0.0, u_scr[0:8, c0:c0 + width])
            u_scr[tm + 8:tm + 16, c0:c0 + width] = jnp.where(seg_last, 0.0, u_scr[tm + 8:tm + 16, c0:c0 + width])

    def conv(c0, width, w_ref_, w0):
        acc = u_scr[6:6 + tm, c0:c0 + width] * w_ref_[0:1, w0:w0 + width]
        for j in range(1, CONV_K):
            acc = acc + u_scr[6 + j:6 + j + tm, c0:c0 + width] * w_ref_[j:j + 1, w0:w0 + width]
        return acc

    project(0, RG_WIDTH, True)
    for grp in range(RG_WIDTH // 128):
        c0 = grp * 128
        ua_ref[:, c0:c0 + 128] = conv(c0, 128, acw_ref, c0) + acb_ref[0:1, c0:c0 + 128]
    project(512, 512, False)
    gay_ref[...] = _gelu_tanh(u_scr[8:8 + tm, 512:1024]).astype(BF16)

    for grp in range(3 * DN_HEADS):
        c0 = grp * 128
        if grp % DN_HEADS == 0:
            project(1024 + c0, DN_HEADS * DN_D, True)
        y = _silu(conv(1024 + c0, 128, bcw_ref, c0))
        if grp < 2 * DN_HEADS:
            y = y * lax.rsqrt(jnp.sum(y * y, axis=-1, keepdims=True) + EPS)
        if grp < DN_HEADS:
            q_ref[:, c0:c0 + 128] = (y * (DN_D ** -0.5)).astype(BF16)
        elif grp < 2 * DN_HEADS:
            k_ref[:, c0 - 512:c0 - 384] = y.astype(BF16)
        else:
            v_ref[:, c0 - 1024:c0 - 896] = y.astype(BF16)
    project(2560, 512, False)
    sz_ref[...] = _silu(u_scr[8:8 + tm, 2560:3072]).astype(BF16)

    project(3072, 128, False)
    xg = u_scr[8:8 + tm, 3072:3200]
    lane = lax.broadcasted_iota(jnp.int32, xg.shape, 1)
    g = -jnp.exp(gpar_ref[0:1, :]) * _softplus(xg + gpar_ref[1:2, :])
    gb_ref[...] = jnp.where(lane < 2 * DN_HEADS, _sigmoid(xg), g)


def _row_specs(tm, d, ctx_tiles, ctx_len, seq, halo):
    tb = tm // 8

    def specs(n_rows, tile_of):
        main = pl.BlockSpec((None, tm, d), lambda b, t: (b, jnp.clip(tile_of(t), 0, n_rows // tm - 1), 0))
        if not halo:
            return [main]
        prev = pl.BlockSpec((None, 8, d), lambda b, t: (b, jnp.clip(tile_of(t) * tb - 1, 0, n_rows // 8 - 1), 0))
        nxt = pl.BlockSpec((None, 8, d), lambda b, t: (b, jnp.clip((tile_of(t) + 1) * tb, 0, n_rows // 8 - 1), 0))
        return [prev, main, nxt]

    return specs(ctx_len, lambda t: t) + specs(seq, lambda t: t - ctx_tiles)


def _inproj0(ctx, x, modtab, g, w_pad, acw, acb, bcw, gpar, *, tm):
    bsz, ctx_len, d = ctx.shape
    seq = x.shape[1]
    t_all = ctx_len + seq
    n_tiles = t_all // tm
    ctx_tiles = ctx_len // tm
    kern = functools.partial(_inproj0_kernel, tm=tm, ctx_tiles=ctx_tiles, n_tiles=n_tiles)
    tok = lambda w, dt=BF16: jax.ShapeDtypeStruct((bsz, t_all, w), dt)
    tok_spec = lambda w: pl.BlockSpec((None, tm, w), lambda b, t: (b, t, 0))
    return pl.pallas_call(
        kern,
        grid=(bsz, n_tiles),
        in_specs=_row_specs(tm, d, ctx_tiles, ctx_len, seq, True) + [
            pl.BlockSpec((None, None, 6, d), lambda b, t: (b, jnp.where(t >= ctx_tiles, 1, 0), 0, 0)),
            _const_spec((1, d)),
            _const_spec((d, E_IN_PAD)),
            _const_spec((CONV_K, RG_WIDTH)),
            _const_spec((1, RG_WIDTH)),
            _const_spec((CONV_K, 3 * DN_HEADS * DN_D)),
            _const_spec((2, 128)),
        ],
        out_specs=[tok_spec(512), tok_spec(512), tok_spec(512), tok_spec(512), tok_spec(512), tok_spec(512),
                   tok_spec(128)],
        out_shape=[tok(512, F32), tok(512), tok(512), tok(512), tok(512), tok(512), tok(128, F32)],
        scratch_shapes=[pltpu.VMEM((tm + 16, E_IN_PAD), F32)],
        compiler_params=_params(("arbitrary", "arbitrary"), 40 << 20),
        name="l0_inproj",
    )(ctx, ctx, ctx, x, x, x, modtab, g, w_pad, acw, acb, bcw, gpar)


def _rglru_kernel(uf_ref, ub_ref, wg_ref, gbias_ref, lam_ref, hf_ref, hb_ref,
                  af_scr, xf_scr, ab_scr, xb_scr, h_scr, *, tt, bsz):
    s = pl.program_id(0)

    @pl.when(s == 0)
    def _():
        h_scr[...] = jnp.zeros_like(h_scr)

    def gates(u_ref, d, a_scr, x_scr):
        x = u_ref[...].reshape(tt * bsz, RG_WIDTH)
        xb = x.astype(BF16)
        for half in range(2):
            c0 = half * 256
            xh = xb[:, c0:c0 + 256]
            r = _sigmoid_tanh(_dot(xh, wg_ref[d, 0, half]) + gbias_ref[2 * d:2 * d + 1, c0:c0 + 256])
            i = _sigmoid_tanh(_dot(xh, wg_ref[d, 1, half]) + gbias_ref[2 * d + 1:2 * d + 2, c0:c0 + 256])
            log_a = (-RG_C) * r * _softplus(-lam_ref[d:d + 1, c0:c0 + 256])
            a = jnp.exp(log_a)
            mult = jnp.sqrt(-jnp.tanh(log_a) * (a * a + 1.0))
            xin = mult * (i * x[:, c0:c0 + 256])
            a_scr[:, :, c0:c0 + 256] = a.reshape(tt, bsz, 256)
            x_scr[:, :, c0:c0 + 256] = xin.reshape(tt, bsz, 256)

    gates(uf_ref, 0, af_scr, xf_scr)
    gates(ub_ref, 1, ab_scr, xb_scr)

    def step(t, carry):
        hf, hb = carry
        hf = af_scr[t] * hf + xf_scr[t]
        hf_ref[t] = hf
        tb = tt - 1 - t
        hb = ab_scr[tb] * hb + xb_scr[tb]
        hb_ref[tb] = hb
        return hf, hb

    hf, hb = lax.fori_loop(0, tt, step, (h_scr[0], h_scr[1]), unroll=8)
    h_scr[0] = hf
    h_scr[1] = hb


def _rglru(ua3, wg, gbias, lam, *, tt, ctx_len):
    t_all, bsz, w = ua3.shape
    n_steps = t_all // tt
    nc = ctx_len // tt

    def bwd(s):
        return jnp.where(s < nc, nc - 1 - s, n_steps + nc - 1 - s)

    blk = (tt, bsz, w)
    kern = functools.partial(_rglru_kernel, tt=tt, bsz=bsz)
    return pl.pallas_call(
        kern,
        grid=(n_steps,),
        in_specs=[pl.BlockSpec(blk, lambda s: (s, 0, 0)),
                  pl.BlockSpec(blk, lambda s: (bwd(s), 0, 0)),
                  _const_spec(wg.shape), _const_spec(gbias.shape), _const_spec(lam.shape)],
        out_specs=[pl.BlockSpec(blk, lambda s: (s, 0, 0)),
                   pl.BlockSpec(blk, lambda s: (bwd(s), 0, 0))],
        out_shape=[jax.ShapeDtypeStruct(ua3.shape, F32)] * 2,
        scratch_shapes=[pltpu.VMEM(blk, F32)] * 4 + [pltpu.VMEM((2, bsz, w), F32)],
        compiler_params=_params(("arbitrary",), 40 << 20),
        name="l0_rglru",
    )(ua3, ua3, wg, gbias, lam)


def _delta_kernel(qf_ref, kf_ref, vf_ref, gf_ref, qb_ref, kb_ref, vb_ref, gb_ref, of_ref, ob_ref, s_scr, *, n_sub):
    c = DN_CHUNK

    @pl.when(pl.program_id(1) == 0)
    def _():
        s_scr[...] = jnp.zeros_like(s_scr)

    dir_refs = ((qf_ref, kf_ref, vf_ref, gf_ref, of_ref), (qb_ref, kb_ref, vb_ref, gb_ref, ob_ref))
    masks = [_scan_masks(c, d) for d in range(2)]
    eye = jnp.where(masks[0][3], 1.0, 0.0)

    cums = {}
    for d in range(2):
        incl, _, incl_t, _ = masks[d]
        m_incl = jnp.where(incl, 1.0, 0.0)
        m_incl_t = jnp.where(incl_t, 1.0, 0.0)
        for ci in range(n_sub):
            g_all = dir_refs[d][3][ci * c:(ci + 1) * c, :]
            gc_all = _dot_hi(m_incl, g_all)
            gct_all = lax.dot_general(g_all, m_incl_t, (((0,), (0,)), ((), ())), precision=HI,
                                      preferred_element_type=F32)
            cums[d, ci] = (g_all, gc_all, gct_all)

    chains = []
    for d in range(2):
        q_ref, k_ref, v_ref, _, _ = dir_refs[d]
        incl, strict, _, _ = masks[d]
        last = c - 1 if d == 0 else 0
        for ci in range(n_sub):
            g_all, gc_all, gct_all = cums[d, ci]
            rs = slice(ci * c, (ci + 1) * c)
            for h in range(DN_HEADS):
                hs = slice(h * DN_D, (h + 1) * DN_D)
                lane = 2 * DN_HEADS + d * DN_HEADS + h
                ch = dict(d=d, ci=ci, h=h, rs=rs, hs=hs, incl=incl, strict=strict)
                ch["beta"] = g_all[:, d * DN_HEADS + h:d * DN_HEADS + h + 1]
                gc = jnp.broadcast_to(gc_all[:, lane:lane + 1], (c, DN_D))
                gc_row = jnp.broadcast_to(gct_all[lane:lane + 1, :], (c, c))
                ch["gc"] = gc
                ch["gtot"] = gc[last:last + 1, :]
                ch["decay"] = jnp.where(incl, jnp.exp(jnp.minimum(gc[:, 0:c] - gc_row, 0.0)), 0.0)
                ch["e_gc"] = jnp.exp(gc)
                ch["q"] = q_ref[rs, hs].astype(F32)
                ch["k"] = k_ref[rs, hs].astype(F32)
                ch["v"] = v_ref[rs, hs].astype(F32)
                chains.append(ch)

    for ch in chains:
        ch["kb"] = ch["k"] * ch["beta"]
        qk = _dot_nt(jnp.concatenate([ch["kb"], ch["q"]], axis=0).astype(BF16), ch["k"].astype(BF16))
        ch["neg"] = -jnp.where(ch["strict"], qk[0:c] * ch["decay"], 0.0)
        ch["a_qk"] = (qk[c:2 * c] * ch["decay"]).astype(BF16)
    for ch in chains:
        negb = ch["neg"].astype(BF16)
        ch["t"] = eye + ch["neg"]
        ch["p"] = _dot(negb, negb)
    n_sq = max(1, (c - 1).bit_length() - 1)
    for it in range(n_sq):
        for ch in chains:
            tp = _dot(jnp.concatenate([ch["t"], ch["p"]], axis=0).astype(BF16), ch["p"].astype(BF16))
            ch["t"] = ch["t"] + tp[0:c]
            ch["p"] = tp[c:2 * c]
    for ch in chains:
        rhs = jnp.concatenate([ch["v"] * ch["beta"], ch["kb"] * ch["e_gc"]], axis=1).astype(BF16)
        sol = _dot(ch["t"].astype(BF16), rhs)
        ch["u"] = sol[:, 0:DN_D]
        ch["wq"] = jnp.concatenate([sol[:, DN_D:2 * DN_D], ch["q"] * ch["e_gc"]], axis=0).astype(BF16)
        ch["k_tail"] = (ch["k"] * jnp.exp(ch["gtot"] - ch["gc"])).astype(BF16)

    by_key = {(ch["d"], ch["ci"], ch["h"]): ch for ch in chains}
    for step in range(n_sub):
        live = [by_key[d, step if d == 0 else n_sub - 1 - step, h] for d in range(2) for h in range(DN_HEADS)]
        for ch in live:
            ch["st"] = s_scr[ch["d"], ch["h"]]
            ch["ws"] = _dot(ch["wq"], ch["st"].astype(BF16))
        for ch in live:
            vnb = (ch["u"] - ch["ws"][0:c]).astype(BF16)
            o = ch["ws"][c:2 * c] + _dot(ch["a_qk"], vnb)
            dir_refs[ch["d"]][4][ch["rs"], ch["hs"]] = o.astype(BF16)
            s_scr[ch["d"], ch["h"]] = ch["st"] * jnp.exp(ch["gtot"]) + _dot_tn(ch["k_tail"], vnb)


def _delta(q, k, v, gb, *, ctx_len, rows):
    bsz, t_all, w = q.shape
    n_steps = t_all // rows
    nc = ctx_len // rows

    def bwd(s):
        return jnp.where(s < nc, nc - 1 - s, n_steps + nc - 1 - s)

    fwd_spec = lambda width: pl.BlockSpec((None, rows, width), lambda b, s: (b, s, 0))
    bwd_spec = lambda width: pl.BlockSpec((None, rows, width), lambda b, s: (b, bwd(s), 0))
    return pl.pallas_call(
        functools.partial(_delta_kernel, n_sub=rows // DN_CHUNK),
        grid=(bsz, n_steps),
        in_specs=[fwd_spec(w), fwd_spec(w), fwd_spec(w), fwd_spec(128),
                  bwd_spec(w), bwd_spec(w), bwd_spec(w), bwd_spec(128)],
        out_specs=[fwd_spec(w), bwd_spec(w)],
        out_shape=[jax.ShapeDtypeStruct((bsz, t_all, w), BF16)] * 2,
        scratch_shapes=[pltpu.VMEM((2, DN_HEADS, DN_D, DN_D), F32)],
        compiler_params=_params(("arbitrary", "arbitrary"), 32 << 20),
        name="l0_deltanet",
    )(q, k, v, gb, q, k, v, gb)


def _head_norm(y, g):
    return y * lax.rsqrt(jnp.mean(y * y, axis=-1, keepdims=True) + EPS) * g


def _l0_tail_kernel(ha_ref, gay_ref, o0_ref, o1_ref, sz_ref, ctx_ref, x_ref, mod_ref, ng_ref, wo_ref, g_ref, wg_ref,
                    wu_ref, wd_ref, hc_ref, hl_ref, *, n_chunks, ctx_tiles):
    t = pl.program_id(1)
    tm = ha_ref.shape[0]
    halves = [dict(rs=slice(i * tm // 2, (i + 1) * tm // 2)) for i in range(2)]
    for hv in halves:
        rs = hv["rs"]
        parts = [(ha_ref[rs, :] * gay_ref[rs, :].astype(F32)).astype(BF16)]
        for hd in range(DN_HEADS):
            lo = hd * DN_D
            ob = o0_ref[rs, lo:lo + DN_D].astype(F32) + o1_ref[rs, lo:lo + DN_D].astype(F32)
            parts.append((_head_norm(ob, ng_ref[...]) * sz_ref[rs, lo:lo + DN_D].astype(F32)).astype(BF16))
        hv["ycat"] = jnp.concatenate(parts, axis=-1)
    for hv in halves:
        h = jnp.where(t < ctx_tiles, ctx_ref[hv["rs"], :], x_ref[hv["rs"], :])
        hv["x"] = h + mod_ref[2:3, :] * _dot(hv["ycat"], wo_ref[...])
    for hv in halves:
        hv["xm"] = _normmod(hv["x"], g_ref[...], mod_ref[3:4, :], mod_ref[4:5, :]).astype(BF16)
        hv["acc"] = jnp.zeros(hv["x"].shape, F32)
    cw = D_FF // n_chunks
    for ci in range(n_chunks):
        c0 = ci * cw
        for hv in halves:
            hv["act"] = (_silu(_dot(hv["xm"], wg_ref[:, c0:c0 + cw])) * _dot(hv["xm"], wu_ref[:, c0:c0 + cw])).astype(BF16)
        for hv in halves:
            hv["acc"] = hv["acc"] + _dot(hv["act"], wd_ref[c0:c0 + cw, :])
    out = jnp.concatenate([hv["x"] + mod_ref[5:6, :] * hv["acc"] for hv in halves], axis=0)

    @pl.when(t < ctx_tiles)
    def _():
        hc_ref[...] = out

    @pl.when(t >= ctx_tiles)
    def _():
        hl_ref[...] = out


def _l0_tail(ha, gay, o0, o1, sz, ctx, x, modtab, ng, wo, g, wg, wu, wd, *, tm):
    bsz, ctx_len, d = ctx.shape
    seq = x.shape[1]
    t_all = ctx_len + seq
    ctx_tiles = ctx_len // tm
    tok = lambda width: pl.BlockSpec((None, tm, width), lambda b, t: (b, t, 0))
    return pl.pallas_call(
        functools.partial(_l0_tail_kernel, n_chunks=2, ctx_tiles=ctx_tiles),
        grid=(bsz, t_all // tm),
        in_specs=[tok(512), tok(512), tok(512), tok(512), tok(512)] + _row_specs(tm, d, ctx_tiles, ctx_len, seq, False) + [
                  pl.BlockSpec((None, None, 6, d), lambda b, t: (b, jnp.where(t >= ctx_tiles, 1, 0), 0, 0)),
                  _const_spec((1, DN_D)), _const_spec((d, d)),
                  _const_spec((1, d)), _const_spec((d, D_FF)), _const_spec((d, D_FF)), _const_spec((D_FF, d))],
        out_specs=[pl.BlockSpec((None, tm, d), lambda b, t: (b, jnp.minimum(t, ctx_tiles - 1), 0)),
                   pl.BlockSpec((None, tm, d), lambda b, t: (b, jnp.maximum(t - ctx_tiles, 0), 0))],
        out_shape=[jax.ShapeDtypeStruct((bsz, ctx_len, d), F32), jax.ShapeDtypeStruct((bsz, t_all - ctx_len, d), F32)],
        compiler_params=_params(("arbitrary", "arbitrary"), 48 << 20),
        name="l0_tail",
    )(ha, gay, o0, o1, sz, ctx, x, modtab, ng, wo, g, wg, wu, wd)


def _inproj1_kernel(h_ref, mod_ref, g_ref, w_ref, lbl_ref, wlr_ref, b2_ref, pa_ref, pb_ref, u_scr, *, layer):
    x = h_ref[...]
    xm = _normmod(x, g_ref[...], mod_ref[0:1, :], mod_ref[1:2, :]).astype(BF16)

    lg = lbl_ref[...]
    ex = jnp.exp(lg - jnp.max(lg, axis=0, keepdims=True))
    lbw = ex / jnp.sum(ex, axis=0, keepdims=True)
    lb = jnp.sum(lbw[1:layer + 1], axis=0, keepdims=True)

    def project(c0, width):
        u_scr[:, c0:c0 + width] = _dot(xm, w_ref[:, c0:c0 + width])

    def put(seg, off, val):
        ref = (pa_ref, pb_ref)[seg[0]]
        ref[:, seg[1] * SEG + off:seg[1] * SEG + off + val.shape[1]] = val.astype(ref.dtype)

    def groups(fn):
        for grp in range(SEG // 128):
            fn(grp * 128)

    project(0, SEG)
    groups(lambda c0: put(S_HQ, c0, _silu(u_scr[:, c0:c0 + 128]) * (HG_D ** -0.5)))
    for dr, (sk, sf) in enumerate(((S_HK0, S_HLF0), (S_HK1, S_HLF1))):
        project(512 + dr * 512, SEG)

        def forget(c0, dr=dr, sk=sk, sf=sf):
            lbg = lb[:, c0:c0 + 128]
            fl = u_scr[:, 512 + dr * 512 + c0:512 + dr * 512 + c0 + 128]
            put(sf, c0, jnp.log(lbg + (1.0 - lbg) * _sigmoid(fl)))
            put(sk, c0, (1.0 - lbg) * _sigmoid(-fl))
        groups(forget)
    project(1536, SEG)
    groups(lambda c0: put(S_HV, c0, u_scr[:, 1536 + c0:1536 + c0 + 128]))
    project(2048, SEG)
    groups(lambda c0: put(S_CG, c0, _silu(u_scr[:, 2048 + c0:2048 + c0 + 128])))
    project(2560, SEG)
    put(S_GQK, 0, u_scr[:, 2560:2816] * (GLA_DK ** -0.5))
    put(S_GQK, 256, u_scr[:, 2816:3072])
    project(3072, SEG)
    groups(lambda c0: put(S_GV, c0, u_scr[:, 3072 + c0:3072 + c0 + 128]))
    project(3584, SEG)
    groups(lambda c0: put(S_DG, c0, _silu(u_scr[:, 3584 + c0:3584 + c0 + 128])))
    project(4096, 128)
    lr = u_scr[:, 4096:4224]
    put(S_GLD, 0, -_softplus(-(_dot_hi(lr, wlr_ref[...]) + b2_ref[...])) * (1.0 / GLA_GATE_NORM))


def _inproj1(h, row0, n_rows, modtab, seg, g, w_pad, lbl, wlr, b2, *, tm, layer):
    bsz, _, d = h.shape
    t0 = row0 // tm
    return pl.pallas_call(
        functools.partial(_inproj1_kernel, layer=layer),
        grid=(bsz, n_rows // tm),
        in_specs=[pl.BlockSpec((None, tm, d), lambda b, t: (b, t0 + t, 0)),
                  pl.BlockSpec((None, None, 6, d), lambda b, t: (b, seg, 0, 0)),
                  _const_spec((1, d)), _const_spec((d, O_IN_PAD)), _const_spec(lbl.shape),
                  _const_spec((128, SEG)), _const_spec((1, SEG))],
        out_specs=[pl.BlockSpec((None, tm, n * SEG), lambda b, t: (b, t, 0)) for n in N_SEG],
        out_shape=[jax.ShapeDtypeStruct((bsz, n_rows, n * SEG), dt) for n, dt in zip(N_SEG, P1_DTYPES)],
        scratch_shapes=[pltpu.VMEM((tm, O_IN_PAD), F32)],
        compiler_params=_params(("arbitrary", "arbitrary"), 44 << 20),
        name="l1_inproj",
    )(h, modtab, g, w_pad, lbl, wlr, b2)


def _gla_stream(d, q_all, k_all, ld_all, v_all, st_ref, o_ref, o_lane0, r0, n_heads, dk, dv, incl, m_incl):
    c = k_all.shape[0]
    mid = c // 2 - 1 if d == 0 else c // 2
    last = c - 1 if d == 0 else 0
    bc = _dot_hi(m_incl, ld_all)
    m = bc[mid:mid + 1]
    btot = bc[last:last + 1]
    kn = k_all.astype(F32) * jnp.exp(m - bc)
    it = dict(d=d, r0=r0, st_ref=st_ref, o_ref=o_ref, o_lane0=o_lane0, n_heads=n_heads, dk=dk, dv=dv, incl=incl,
              c=c, kt=(kn * jnp.exp(btot - m)).astype(BF16), dec=jnp.exp(btot), v=v_all.astype(BF16),
              want_out=q_all is not None)
    if q_all is not None:
        qe = q_all.astype(F32) * jnp.exp(bc)
        it.update(qd=(qe * jnp.exp(-m)).astype(BF16), qe=qe.astype(BF16), knb=kn.astype(BF16))
    return it


def _gla_intra(it):
    dk = it["dk"]
    it["a"] = [jnp.where(it["incl"], _dot_nt(it["qd"][:, hd * dk:(hd + 1) * dk], it["knb"][:, hd * dk:(hd + 1) * dk]),
                         0.0).astype(BF16) for hd in range(it["n_heads"])]


def _gla_advance(it):
    d, dk, dv, c, st_ref = it["d"], it["dk"], it["dv"], it["c"], it["st_ref"]
    sts = [st_ref[d, hd] for hd in range(it["n_heads"])]
    if it["want_out"]:
        for hd in range(it["n_heads"]):
            v = it["v"][:, hd * dv:(hd + 1) * dv]
            o = _dot(it["a"][hd], v) + _dot_nt(it["qe"][:, hd * dk:(hd + 1) * dk], sts[hd].astype(BF16))
            it["o_ref"][it["r0"]:it["r0"] + c, it["o_lane0"] + hd * dv:it["o_lane0"] + (hd + 1) * dv] = o.astype(BF16)
    for hd in range(it["n_heads"]):
        ks = slice(hd * dk, (hd + 1) * dk)
        st_ref[d, hd] = sts[hd] * it["dec"][:, ks] + _dot_tn(it["v"][:, hd * dv:(hd + 1) * dv], it["kt"][:, ks])


def _mix1_body(dirs, n_sub, sh_ref, sg_ref):
    c = MIX1_CHUNK
    gw = GLA_HEADS * GLA_DK
    prepared = {}
    for d, (hq_ref, hv_ref, hk_ref, hlf_ref, gv_ref, gqk_ref, gld_ref, o_ref) in enumerate(dirs):
        incl = _scan_masks(c, d)[0]
        m_incl = jnp.where(incl, 1.0, 0.0)
        for ci in range(n_sub):
            r0 = ci * c
            rs = slice(r0, r0 + c)
            prepared[d, ci, 0] = _gla_stream(
                d, None if hq_ref is None else hq_ref[rs, :], hk_ref[rs, :], hlf_ref[rs, :], hv_ref[rs, :],
                sh_ref, o_ref, 0, r0, HG_HEADS, HG_D, HG_D, incl, m_incl)
            prepared[d, ci, 1] = _gla_stream(
                d, None if hq_ref is None else gqk_ref[rs, 0:gw], gqk_ref[rs, gw:2 * gw],
                gld_ref[rs, d * gw:(d + 1) * gw], gv_ref[rs, :],
                sg_ref, o_ref, HG_HEADS * HG_D, r0, GLA_HEADS, GLA_DK, GLA_DV, incl, m_incl)
    for it in prepared.values():
        if it["want_out"]:
            _gla_intra(it)
    for step in range(n_sub):
        for d in range(2):
            for stream in range(2):
                _gla_advance(prepared[d, step if d == 0 else n_sub - 1 - step, stream])


def _mix1_ctx_kernel(*refs, n_sub):
    fwd, bwd, (sh_ref, sg_ref) = refs[0:6], refs[6:12], refs[12:14]

    @pl.when(pl.program_id(1) == 0)
    def _():
        sh_ref[...] = jnp.zeros_like(sh_ref)
        sg_ref[...] = jnp.zeros_like(sg_ref)

    _mix1_body([(None,) + tuple(r) + (None,) for r in (fwd, bwd)], n_sub, sh_ref, sg_ref)


def _mix1_lat_kernel(*refs, n_sub):
    fwd, bwd = refs[0:7], refs[7:14]
    sh0_ref, sg0_ref, of_ref, ob_ref, sh_scr, sg_scr = refs[14:20]

    @pl.when(pl.program_id(1) == 0)
    def _():
        sh_scr[...] = sh0_ref[...]
        sg_scr[...] = sg0_ref[...]

    _mix1_body([tuple(fwd) + (of_ref,), tuple(bwd) + (ob_ref,)], n_sub, sh_scr, sg_scr)


def _mix1_specs(p1, rows, n_steps, segs_of_dir):
    specs, args = [], []
    for d in range(2):
        blk = (lambda b, s: s) if d == 0 else (lambda b, s: n_steps - 1 - s)
        for arr, sg in segs_of_dir(d):
            specs.append(pl.BlockSpec((None, rows, SEG), lambda b, s, blk=blk, sg=sg: (b, blk(b, s), sg)))
            args.append(p1[arr])
    return specs, args


_SH_SHAPE = (2, HG_HEADS, HG_D, HG_D)
_SG_SHAPE = (2, GLA_HEADS, GLA_DV, GLA_DK)


def _mix1_ctx(p1c, *, rows):
    bsz, ctx_len, _ = p1c[0].shape
    n_steps = ctx_len // rows
    segs = lambda d: (S_HV, (S_HK0, S_HK1)[d], (S_HLF0, S_HLF1)[d], S_GV, S_GQK, S_GLD)
    specs, args = _mix1_specs(p1c, rows, n_steps, segs)
    state = lambda shape: pl.BlockSpec((None,) + shape, lambda b, s: (b, 0, 0, 0, 0))
    return pl.pallas_call(
        functools.partial(_mix1_ctx_kernel, n_sub=rows // MIX1_CHUNK),
        grid=(bsz, n_steps),
        in_specs=specs,
        out_specs=[state(_SH_SHAPE), state(_SG_SHAPE)],
        out_shape=[jax.ShapeDtypeStruct((bsz,) + _SH_SHAPE, F32), jax.ShapeDtypeStruct((bsz,) + _SG_SHAPE, F32)],
        compiler_params=_params(("arbitrary", "arbitrary"), 32 << 20),
        name="l1_ctx_state",
    )(*args)


def _mix1_lat(p1l, sh0, sg0, *, rows):
    bsz, seq, _ = p1l[0].shape
    n_steps = seq // rows
    segs = lambda d: (S_HQ, S_HV, (S_HK0, S_HK1)[d], (S_HLF0, S_HLF1)[d], S_GV, S_GQK, S_GLD)
    specs, args = _mix1_specs(p1l, rows, n_steps, segs)
    state = lambda shape: pl.BlockSpec((None,) + shape, lambda b, s: (b, 0, 0, 0, 0))
    ow = HG_HEADS * HG_D + GLA_HEADS * GLA_DV
    return pl.pallas_call(
        functools.partial(_mix1_lat_kernel, n_sub=rows // MIX1_CHUNK),
        grid=(bsz, n_steps),
        in_specs=specs + [state(_SH_SHAPE), state(_SG_SHAPE)],
        out_specs=[pl.BlockSpec((None, rows, ow), lambda b, s: (b, s, 0)),
                   pl.BlockSpec((None, rows, ow), lambda b, s: (b, n_steps - 1 - s, 0))],
        out_shape=[jax.ShapeDtypeStruct((bsz, seq, ow), BF16)] * 2,
        scratch_shapes=[pltpu.VMEM(_SH_SHAPE, F32), pltpu.VMEM(_SG_SHAPE, F32)],
        compiler_params=_params(("arbitrary", "arbitrary"), 32 << 20),
        name="l1_scan",
    )(*args, sh0, sg0)


def _outproj1_kernel(o0_ref, o1_ref, gate_ref, h_ref, mod_ref, cng_ref, dng_ref, w_ref, out_ref):
    parts = []
    for hd in range(HG_HEADS + GLA_HEADS):
        lo = hd * 128
        y = o0_ref[:, lo:lo + 128].astype(F32) + o1_ref[:, lo:lo + 128].astype(F32)
        ng = cng_ref[...] if hd < HG_HEADS else dng_ref[...]
        parts.append((_head_norm(y, ng) * gate_ref[:, lo:lo + 128].astype(F32)).astype(BF16))
    y = _dot(jnp.concatenate(parts, axis=-1), w_ref[...])
    out_ref[...] = h_ref[...] + mod_ref[2:3, :] * y


def _outproj1(o0, o1, p1l, h, modtab, cng, dng, w, *, tm):
    bsz, seq, d = h.shape
    tok = pl.BlockSpec((None, tm, d), lambda b, t: (b, t, 0))
    return pl.pallas_call(
        _outproj1_kernel,
        grid=(bsz, seq // tm),
        in_specs=[tok, tok,
                  pl.BlockSpec((None, tm, 2 * SEG), lambda b, t: (b, t, S_CG[1] // 2)),
                  tok,
                  pl.BlockSpec((None, None, 6, d), lambda b, t: (b, 1, 0, 0)),
                  _const_spec((1, 128)), _const_spec((1, 128)), _const_spec((d, d))],
        out_specs=tok,
        out_shape=jax.ShapeDtypeStruct((bsz, seq, d), F32),
        compiler_params=_params(("arbitrary", "arbitrary"), 32 << 20),
        name="l1_outproj",
    )(o0, o1, p1l[S_CG[0]], h, modtab, cng, dng, w)


def _for_pieces(length, fn):
    done = 0
    for rows in MOE_SEG_PIECES:
        n = (length - done) // rows

        def body(p, carry, rows=rows, done=done):
            fn(done + p * rows, rows)
            return carry
        lax.fori_loop(0, n, body, 0)
        done = done + n * rows


def _moe_route_kernel(h_ref, mod_ref, ng_ref, rwt_ref, rb_ref, slot_ref, seg_ref, xs_hbm,
                      xn_scr, xg_scr, zero_scr, base_smem, sem, *, tk):
    i = pl.program_id(0)
    pc = MOE_PIECE
    gr = MOE_GATHER_ROWS

    @pl.when(i == 0)
    def _():
        for e in range(N_EXPERTS):
            base_smem[e] = 0
        zero_scr[...] = jnp.zeros_like(zero_scr)

    xm = _normmod(h_ref[...], ng_ref[...], mod_ref[3:4, :], mod_ref[4:5, :])
    xn_scr[...] = xm.astype(BF16)
    lg = lax.dot_general(rwt_ref[...], xm, (((1,), (1,)), ((), ())), precision=HI,
                         preferred_element_type=F32) + rb_ref[...]
    eidx = lax.broadcasted_iota(jnp.int32, lg.shape, 0).astype(F32)
    m1 = jnp.max(lg, axis=0, keepdims=True)
    i1 = jnp.min(jnp.where(lg == m1, eidx, float(N_EXPERTS)), axis=0, keepdims=True)
    lg2 = jnp.where(eidx == i1, -jnp.inf, lg)
    m2 = jnp.max(lg2, axis=0, keepdims=True)
    i2 = jnp.min(jnp.where(lg2 == m2, eidx, float(N_EXPERTS)), axis=0, keepdims=True)
    ex = jnp.exp(m2 - m1)
    p1 = 1.0 / (1.0 + ex)
    sel = jnp.where(eidx == i1, 1.0, 0.0) + jnp.where(eidx == i2, 1.0, 0.0)
    lane = lax.broadcasted_iota(jnp.int32, lg.shape, 1)
    cum = sel
    sh = 1
    while sh < tk:
        cum = cum + jnp.where(lane >= sh, pltpu.roll(cum, sh, 1), 0.0)
        sh *= 2
    padded = jnp.floor((cum[:, tk - 1:tk] + (pc - 1.0)) * (1.0 / pc)) * pc
    padded = jnp.broadcast_to(padded, (N_EXPERTS, 128))
    er = lax.broadcasted_iota(jnp.int32, (N_EXPERTS, N_EXPERTS), 0)
    ec = lax.broadcasted_iota(jnp.int32, (N_EXPERTS, N_EXPERTS), 1)
    off = _dot_hi(jnp.where(er > ec, 1.0, 0.0), padded)
    slot = off[:, 0:1] + cum - 1.0
    slot_a = jnp.sum(jnp.where(eidx == i1, slot, 0.0), axis=0, keepdims=True)
    slot_b = jnp.sum(jnp.where(eidx == i2, slot, 0.0), axis=0, keepdims=True)
    slot_ref[...] = jnp.concatenate([slot_a, slot_b, p1, ex * p1, jnp.zeros((4, tk), F32)], axis=0)

    total = jnp.max(off[N_EXPERTS - 1:N_EXPERTS, :] + padded[N_EXPERTS - 1:N_EXPERTS, :]).astype(jnp.int32)

    def gather(ci, carry):
        r0 = pl.multiple_of(ci * gr, gr)
        rid = (lax.broadcasted_iota(jnp.int32, (gr, tk), 0) + r0).astype(F32)
        p = jnp.where(rid == slot_a, 1.0, 0.0) + jnp.where(rid == slot_b, 1.0, 0.0)
        xg_scr[pl.ds(r0, gr), :] = _dot(p.astype(BF16), xn_scr[...]).astype(BF16)
        return carry
    lax.fori_loop(0, (total + gr - 1) // gr, gather, 0)

    erow = lax.broadcasted_iota(jnp.int32, (N_EXPERTS, 128), 0)
    base_vec = jnp.zeros((N_EXPERTS, 128), F32)
    segs = []
    for e in range(N_EXPERTS):
        off_e = jnp.max(off[e:e + 1, :]).astype(jnp.int32)
        len_e = jnp.max(padded[e:e + 1, :]).astype(jnp.int32)
        base_e = base_smem[e]
        base_vec = jnp.where(erow == e, base_e.astype(F32), base_vec)
        segs.append((e, off_e, len_e, base_e))
    seg_ref[0] = off
    seg_ref[1] = padded
    seg_ref[2] = base_vec

    def seg_copy(e, off_e, base_e, r, rows):
        return pltpu.make_async_copy(
            xg_scr.at[pl.ds(pl.multiple_of(off_e + r, pc), rows), :],
            xs_hbm.at[e, pl.ds(pl.multiple_of(base_e + r, pc), rows), :], sem)

    for e, off_e, len_e, base_e in segs:
        _for_pieces(len_e, lambda r, rows, e=e, off_e=off_e, base_e=base_e: seg_copy(e, off_e, base_e, r, rows).start())
    for e, off_e, len_e, base_e in segs:
        _for_pieces(len_e, lambda r, rows, e=e, off_e=off_e, base_e=base_e: seg_copy(e, off_e, base_e, r, rows).wait())
        base_smem[e] = base_e + len_e

    @pl.when(i == pl.num_programs(0) - 1)
    def _():
        def tail_copy(e, p):
            end = base_smem[e]
            return pltpu.make_async_copy(zero_scr, xs_hbm.at[e, pl.ds(pl.multiple_of(end + p * pc, pc), pc), :], sem)

        def n_tail(e):
            rem = lax.rem(base_smem[e], MOE_BLOCK)
            return jnp.where(rem == 0, 0, MOE_BLOCK - rem) // pc

        for e in range(N_EXPERTS):
            def start(p, carry, e=e):
                tail_copy(e, p).start()
                return carry
            lax.fori_loop(0, n_tail(e), start, 0)
        for e in range(N_EXPERTS):
            def wait(p, carry, e=e):
                tail_copy(e, p).wait()
                return carry
            lax.fori_loop(0, n_tail(e), wait, 0)


def _moe_ffn_kernel(eid_ref, blk_ref, nv_ref, x_ref, wg_ref, wu_ref, wd_ref, o_ref, *, n_chunks):
    del eid_ref, blk_ref

    @pl.when(pl.program_id(0) < nv_ref[0])
    def _():
        x = x_ref[...]
        cw = D_FF // n_chunks
        acc = jnp.zeros(x.shape, F32)
        for ci in range(n_chunks):
            c0 = ci * cw
            act = (_silu(_dot(x, wg_ref[:, c0:c0 + cw])) * _dot(x, wu_ref[:, c0:c0 + cw])).astype(BF16)
            acc = acc + _dot(act, wd_ref[c0:c0 + cw, :])
        o_ref[...] = acc.astype(BF16)


def _moe_combine_kernel(base_ref, len_ref, off_ref, h_ref, mod_ref, fg_ref, slot_ref, og_hbm, out_ref,
                        og_scr, sem, *, tk):
    i = pl.program_id(0)
    pc = MOE_PIECE
    gr = MOE_GATHER_ROWS
    n_rows = og_scr.shape[0]

    def seg_copy(e, r, rows):
        return pltpu.make_async_copy(
            og_hbm.at[e, pl.ds(pl.multiple_of(base_ref[i * N_EXPERTS + e] + r, pc), rows), :],
            og_scr.at[pl.ds(pl.multiple_of(off_ref[i * N_EXPERTS + e] + r, pc), rows), :], sem)

    for e in range(N_EXPERTS):
        _for_pieces(len_ref[i * N_EXPERTS + e], lambda r, rows, e=e: seg_copy(e, r, rows).start())

    last = i * N_EXPERTS + N_EXPERTS - 1
    total = off_ref[last] + len_ref[last]

    def clear(p, carry):
        og_scr[pl.ds(pl.multiple_of(p * pc, pc), pc), :] = jnp.zeros((pc, og_scr.shape[1]), BF16)
        return carry
    lax.fori_loop(total // pc, n_rows // pc, clear, 0)

    for e in range(N_EXPERTS):
        _for_pieces(len_ref[i * N_EXPERTS + e], lambda r, rows, e=e: seg_copy(e, r, rows).wait())

    out_ref[...] = jnp.zeros_like(out_ref)

    def scatter(ci, carry):
        r0 = pl.multiple_of(ci * gr, gr)
        rid = (lax.broadcasted_iota(jnp.int32, (gr, tk), 0) + r0).astype(F32)
        pa = jnp.where(rid == slot_ref[0:1, :], 1.0, 0.0)
        pb = jnp.where(rid == slot_ref[1:2, :], 1.0, 0.0)
        gcol = jnp.sum(pa * slot_ref[2:3, :] + pb * slot_ref[3:4, :], axis=1, keepdims=True)
        og = (og_scr[pl.ds(r0, gr), :].astype(F32) * gcol).astype(BF16)
        out_ref[...] = out_ref[...] + _dot_tn((pa + pb).astype(BF16), og)
        return carry
    lax.fori_loop(0, (total + gr - 1) // gr, scatter, 0)
    h3 = h_ref[...] + mod_ref[5:6, :] * out_ref[...]
    out_ref[...] = h3 * lax.rsqrt(jnp.mean(h3 * h3, axis=-1, keepdims=True) + EPS) * fg_ref[...]


def _moe_block_table(seg, n_blocks):
    ends = (seg[-1, 2, :, 0] + seg[-1, 1, :, 0]).astype(jnp.int32)
    nblk = (ends + MOE_BLOCK - 1) // MOE_BLOCK
    cum = jnp.cumsum(nblk)
    n_valid = cum[-1]
    g = jnp.minimum(jnp.arange(n_blocks, dtype=jnp.int32), n_valid - 1)
    eid = jnp.sum((g[:, None] >= cum[None, :]).astype(jnp.int32), axis=1)
    blk = g - (cum - nblk)[eid]
    return eid, blk, n_valid.reshape(1)


def _moe(h, modtab, ng, fg, rwt, rb, wg, wu, wd, *, tk):
    bsz, seq, d = h.shape
    tpb = seq // tk
    n_tiles = bsz * tpb
    n_tok = bsz * seq
    tile_rows = -(-(2 * tk + N_EXPERTS * MOE_PIECE) // MOE_GATHER_ROWS) * MOE_GATHER_ROWS
    cap = -(-(n_tok + n_tiles * MOE_PIECE) // MOE_BLOCK) * MOE_BLOCK
    n_blocks = -(-(2 * n_tok + n_tiles * N_EXPERTS * MOE_PIECE) // MOE_BLOCK) + N_EXPERTS
    tok = lambda i, *_: (i // tpb, i % tpb, 0)
    mod = lambda i, *_: (i // tpb, 1, 0, 0)

    slots, seg, xs = pl.pallas_call(
        functools.partial(_moe_route_kernel, tk=tk),
        grid=(n_tiles,),
        in_specs=[pl.BlockSpec((None, tk, d), tok), pl.BlockSpec((None, None, 6, d), mod),
                  _const_spec((1, d)), _const_spec((N_EXPERTS, d)), _const_spec((N_EXPERTS, 1))],
        out_specs=[pl.BlockSpec((None, 8, tk), lambda i: (i, 0, 0)),
                   pl.BlockSpec((None, 3, N_EXPERTS, 128), lambda i: (i, 0, 0, 0)),
                   pl.BlockSpec(memory_space=pl.ANY)],
        out_shape=[jax.ShapeDtypeStruct((n_tiles, 8, tk), F32),
                   jax.ShapeDtypeStruct((n_tiles, 3, N_EXPERTS, 128), F32),
                   jax.ShapeDtypeStruct((N_EXPERTS, cap, d), BF16)],
        scratch_shapes=[pltpu.VMEM((tk, d), BF16), pltpu.VMEM((tile_rows, d), BF16), pltpu.VMEM((MOE_PIECE, d), BF16),
                        pltpu.SMEM((N_EXPERTS,), jnp.int32), pltpu.SemaphoreType.DMA(())],
        compiler_params=_params(("arbitrary",), 40 << 20),
        name="l1_moe_route",
    )(h, modtab, ng, rwt, rb)

    eid, blk, n_valid = _moe_block_table(seg, n_blocks)
    x_spec = pl.BlockSpec((None, MOE_BLOCK, d), lambda g, eid, blk, nv: (eid[g], blk[g], 0))
    og = pl.pallas_call(
        functools.partial(_moe_ffn_kernel, n_chunks=11),
        grid_spec=pltpu.PrefetchScalarGridSpec(
            num_scalar_prefetch=3, grid=(n_blocks,),
            in_specs=[x_spec,
                      pl.BlockSpec((None, d, D_FF), lambda g, eid, blk, nv: (eid[g], 0, 0)),
                      pl.BlockSpec((None, d, D_FF), lambda g, eid, blk, nv: (eid[g], 0, 0)),
                      pl.BlockSpec((None, D_FF, d), lambda g, eid, blk, nv: (eid[g], 0, 0))],
            out_specs=x_spec),
        out_shape=jax.ShapeDtypeStruct((N_EXPERTS, cap, d), BF16),
        compiler_params=_params(("arbitrary",), 52 << 20),
        name="l1_moe_experts",
    )(eid, blk, n_valid, xs, wg, wu, wd)

    tab = lambda k: seg[:, k, :, 0].astype(jnp.int32).reshape(-1)
    return pl.pallas_call(
        functools.partial(_moe_combine_kernel, tk=tk),
        grid_spec=pltpu.PrefetchScalarGridSpec(
            num_scalar_prefetch=3, grid=(n_tiles,),
            in_specs=[pl.BlockSpec((None, tk, d), tok), pl.BlockSpec((None, None, 6, d), mod),
                      pl.BlockSpec((1, d), lambda i, *_: (0, 0)),
                      pl.BlockSpec((None, 8, tk), lambda i, *_: (i, 0, 0)),
                      pl.BlockSpec(memory_space=pl.ANY)],
            out_specs=pl.BlockSpec((None, tk, d), tok),
            scratch_shapes=[pltpu.VMEM((tile_rows, d), BF16), pltpu.SemaphoreType.DMA(())]),
        out_shape=jax.ShapeDtypeStruct(h.shape, F32),
        compiler_params=_params(("arbitrary",), 40 << 20),
        name="l1_moe_combine",
    )(tab(2), tab(1), tab(0), h, modtab, fg, slots, og)


def _block_diag_gate(gate_w):
    w = gate_w.reshape(2, 2, 2, 4, RG_BLOCK, RG_BLOCK)
    eye = jnp.eye(4, dtype=gate_w.dtype)
    return jnp.einsum('dghbij,bc->dghbicj', w, eye).reshape(2, 2, 2, 256, 256)


def _pad_cols(w, n):
    return jnp.pad(w, ((0, 0), (0, n - w.shape[1])))


def _layer0(ctx, x, modtab, norm_mix_g, norm_ffn_g, e_w_in, e_w_out, e_a_conv_w, e_a_conv_b, e_a_gate_w, e_a_gate_b,
            e_a_lambda, e_b_conv_w, e_b_a_log, e_b_dt_bias, e_b_norm_g, e_ffn_w_gate, e_ffn_w_up, e_ffn_w_down,
            *, tm, tt):
    bsz, ctx_len, d = ctx.shape
    w_in = _pad_cols(e_w_in, E_IN_PAD).astype(BF16)
    gpar = jnp.zeros((2, 128), F32)
    gpar = gpar.at[0, 2 * DN_HEADS:4 * DN_HEADS].set(e_b_a_log.reshape(-1))
    gpar = gpar.at[1, 2 * DN_HEADS:4 * DN_HEADS].set(e_b_dt_bias.reshape(-1))
    ua, gay, q, k, v, sz, gb = _inproj0(ctx, x, modtab, norm_mix_g.reshape(1, d), w_in, e_a_conv_w,
                                        e_a_conv_b.reshape(1, -1), e_b_conv_w, gpar, tm=tm)
    wg = _block_diag_gate(e_a_gate_w).astype(BF16)
    hf, hb = _rglru(jnp.transpose(ua, (1, 0, 2)), wg, e_a_gate_b.reshape(4, RG_WIDTH), e_a_lambda,
                    tt=tt, ctx_len=ctx_len)
    ha = jnp.transpose(hf + hb, (1, 0, 2))
    o0, o1 = _delta(q, k, v, gb, ctx_len=ctx_len, rows=SCAN_ROWS)
    return _l0_tail(ha, gay, o0, o1, sz, ctx, x, modtab, e_b_norm_g.reshape(1, -1), e_w_out.astype(BF16),
                    norm_ffn_g.reshape(1, d), e_ffn_w_gate.astype(BF16), e_ffn_w_up.astype(BF16),
                    e_ffn_w_down.astype(BF16), tm=tm)


def _layer1(hc, hl, modtab, norm_mix_g, norm_ffn_g, final_norm_g, o_w_in, o_w_out, o_lb_logits, o_c_norm_g,
            o_d_gate_w2, o_d_gate_b2, o_d_norm_g, o_router_w, o_router_b, o_moe_w_gate, o_moe_w_up, o_moe_w_down,
            *, tm, tk, layer):
    bsz, seq, d = hl.shape
    ctx_len = hc.shape[1]
    rows = seq // GRID_W
    hl = hl.reshape(bsz, rows, GRID_W, d).swapaxes(1, 2).reshape(bsz, seq, d)
    w_in = _pad_cols(o_w_in, O_IN_PAD).astype(BF16)
    wlr = jnp.zeros((128, SEG), F32)
    wlr = wlr.at[0:GLA_RANK, 0:256].set(o_d_gate_w2[0]).at[GLA_RANK:2 * GLA_RANK, 256:512].set(o_d_gate_w2[1])
    proj = functools.partial(_inproj1, g=norm_mix_g.reshape(1, d), w_pad=w_in, lbl=o_lb_logits, wlr=wlr,
                             b2=o_d_gate_b2.reshape(1, SEG), layer=layer)
    p1c = proj(hc, 0, ctx_len, modtab, 0, tm=tm)
    p1l = proj(hl, 0, seq, modtab, 1, tm=2 * tm)
    sh0, sg0 = _mix1_ctx(p1c, rows=SCAN_ROWS)
    o0, o1 = _mix1_lat(p1l, sh0, sg0, rows=SCAN_ROWS)
    h2 = _outproj1(o0, o1, p1l, hl, modtab, o_c_norm_g.reshape(1, -1), o_d_norm_g.reshape(1, -1),
                   o_w_out.astype(BF16), tm=tm)
    return _moe(h2, modtab, norm_ffn_g.reshape(1, d), final_norm_g.reshape(1, d), o_router_w.T,
                o_router_b.reshape(N_EXPERTS, 1), o_moe_w_gate.astype(BF16), o_moe_w_up.astype(BF16),
                o_moe_w_down.astype(BF16), tk=tk)


def kernel(x, c, ctx, c_ctx, ada_w, ada_b, norm_mix_g, norm_ffn_g, final_norm_g, e_w_in, e_w_out, e_a_conv_w, e_a_conv_b, e_a_gate_w, e_a_gate_b, e_a_lambda, e_b_conv_w, e_b_a_log, e_b_dt_bias, e_b_norm_g, e_ffn_w_gate, e_ffn_w_up, e_ffn_w_down, o_w_in, o_w_out, o_lb_logits, o_c_norm_g, o_d_gate_w2, o_d_gate_b2, o_d_norm_g, o_router_w, o_router_b, o_moe_w_gate, o_moe_w_up, o_moe_w_down):
    bsz, seq, d = x.shape
    ctx_len = ctx.shape[1]
    assert bsz == 8 and d == D_MODEL and ada_w.shape[0] == 2
    tm = min(256, ctx_len)
    tt = min(128, ctx_len)
    tk = min(512, seq)
    assert ctx_len % tm == 0 and seq % tm == 0 and ctx_len % SCAN_ROWS == 0 and seq % SCAN_ROWS == 0
    assert seq % GRID_W == 0 and seq % tk == 0

    mods = _ada(c, c_ctx, ada_w, ada_b)
    hc, hl = _layer0(ctx, x, _modtab(mods[0], bsz), norm_mix_g[0], norm_ffn_g[0], e_w_in[0], e_w_out[0],
                     e_a_conv_w[0], e_a_conv_b[0], e_a_gate_w[0], e_a_gate_b[0], e_a_lambda[0], e_b_conv_w[0],
                     e_b_a_log[0], e_b_dt_bias[0], e_b_norm_g[0], e_ffn_w_gate[0], e_ffn_w_up[0], e_ffn_w_down[0],
                     tm=tm, tt=tt)
    out_cm = _layer1(hc, hl, _modtab(mods[1], bsz), norm_mix_g[1], norm_ffn_g[1], final_norm_g, o_w_in[0],
                     o_w_out[0], o_lb_logits, o_c_norm_g[0], o_d_gate_w2[0], o_d_gate_b2[0], o_d_norm_g[0],
                     o_router_w[0], o_router_b[0], o_moe_w_gate[0], o_moe_w_up[0], o_moe_w_down[0],
                     tm=tm, tk=tk, layer=1)
    rows = seq // GRID_W
    return out_cm.reshape(bsz, GRID_W, rows, d).swapaxes(1, 2).reshape(bsz, seq, d)
```

```python
import functools

import jax
import jax.numpy as jnp
from jax import lax
from jax.experimental import pallas as pl
from jax.experimental.pallas import tpu as pltpu

F32 = jnp.float32
BF16 = jnp.bfloat16
HI = lax.Precision.HIGHEST

EPS = 1e-6
D_MODEL = 1024
GRID_W = 64
CONV_K = 4
RG_WIDTH = 512
RG_BLOCK = 64
RG_C = 8.0
DN_HEADS = 4
DN_D = 128
DN_CHUNK = 64
HG_HEADS = 4
HG_D = 128
GLA_HEADS = 4
GLA_DK = 64
GLA_DV = 128
GLA_RANK = 16
GLA_GATE_NORM = 16.0
MIX1_CHUNK = 64
SCAN_ROWS = 256
D_FF = 2816
N_EXPERTS = 8

E_IN_PAD = 3200
O_IN_PAD = 4224
SEG = 512
S_HQ, S_HV, S_HK0, S_HK1, S_GV, S_GQK, S_CG, S_DG = [(0, i) for i in range(8)]
S_HLF0, S_HLF1, S_GLD = [(1, i) for i in range(3)]
N_SEG = (8, 3)
P1_DTYPES = (BF16, F32)

V7X_VMEM_BYTES = 64 * 1024 * 1024
VMEM_HEADROOM_BYTES = 8 * 1024 * 1024
MOE_PIECE = 16
MOE_SEG_PIECES = (64, MOE_PIECE)
MOE_BLOCK = 512
MOE_GATHER_ROWS = 256


def _vmem(nbytes):
    return int(min(V7X_VMEM_BYTES - VMEM_HEADROOM_BYTES, nbytes))


def _params(sem, vmem_bytes):
    return pltpu.CompilerParams(dimension_semantics=sem, vmem_limit_bytes=_vmem(vmem_bytes))


def _sigmoid(x):
    return jax.nn.sigmoid(x)


def _sigmoid_tanh(x):
    return 0.5 * jnp.tanh(0.5 * x) + 0.5


def _silu(x):
    return x * jax.nn.sigmoid(x)


def _softplus(x):
    return jnp.maximum(x, 0.0) + jnp.log1p(jnp.exp(-jnp.abs(x)))


def _gelu_tanh(x):
    return 0.5 * x * (1.0 + jnp.tanh(0.7978845608028654 * (x + 0.044715 * (x * x * x))))


def _normmod(x, g, shift, scale):
    y = x * lax.rsqrt(jnp.mean(x * x, axis=-1, keepdims=True) + EPS)
    return (y * g) * (1.0 + scale) + shift


def _dot(a, b):
    return jnp.dot(a, b, preferred_element_type=F32)


def _dot_nt(a, b):
    return lax.dot_general(a, b, (((1,), (1,)), ((), ())), preferred_element_type=F32)


def _dot_tn(a, b):
    return lax.dot_general(a, b, (((0,), (0,)), ((), ())), preferred_element_type=F32)


def _dot_hi(a, b):
    return jnp.dot(a, b, precision=HI, preferred_element_type=F32)


def _split3(x):
    hi = x.astype(BF16)
    r1 = x - hi.astype(F32)
    mid = r1.astype(BF16)
    return hi, mid, (r1 - mid.astype(F32)).astype(BF16)


def _dot_hilo(a, b):
    ah = a.astype(BF16)
    al = (a - ah.astype(F32)).astype(BF16)
    bh = b.astype(BF16)
    bl = (b - bh.astype(F32)).astype(BF16)
    return _dot(ah, bh) + (_dot(ah, bl) + _dot(al, bh))


def _mask_dot(mask, x):
    mb = mask.astype(BF16)
    hi, mid, lo = _split3(x)
    return _dot(mb, hi) + (_dot(mb, mid) + _dot(mb, lo))


def _mask_dot_tn(x, mask):
    mb = mask.astype(BF16)
    hi, mid, lo = _split3(x)
    return _dot_tn(hi, mb) + (_dot_tn(mid, mb) + _dot_tn(lo, mb))


def _const_spec(shape):
    nd = len(shape)
    return pl.BlockSpec(shape, lambda *_: (0,) * nd, pipeline_mode=pl.Buffered(1))


def _scan_masks(c, d):
    row = lax.broadcasted_iota(jnp.int32, (c, c), 0)
    col = lax.broadcasted_iota(jnp.int32, (c, c), 1)
    dlt = row - col if d == 0 else col - row
    return dlt >= 0, dlt > 0, dlt <= 0, row == col


def _ada_kernel(cv_ref, w_ref, b_ref, o_ref):
    s = _silu(cv_ref[...]).astype(BF16)
    o_ref[...] = _dot(s, w_ref[...].astype(BF16)) + b_ref[...]


def _ada(c, c_ctx, ada_w, ada_b):
    depth, d, n6 = ada_w.shape
    bsz = c.shape[0]
    rows = 16
    cv = jnp.zeros((rows, d), F32).at[:bsz].set(c).at[bsz].set(c_ctx)
    tn = 1536
    return pl.pallas_call(
        _ada_kernel,
        grid=(depth, n6 // tn),
        in_specs=[pl.BlockSpec((rows, d), lambda l, j: (0, 0)),
                  pl.BlockSpec((None, d, tn), lambda l, j: (l, 0, j)),
                  pl.BlockSpec((None, 1, tn), lambda l, j: (l, 0, j))],
        out_specs=pl.BlockSpec((None, rows, tn), lambda l, j: (l, 0, j)),
        out_shape=jax.ShapeDtypeStruct((depth, rows, n6), F32),
        compiler_params=_params(("arbitrary", "arbitrary"), 32 << 20),
        name="ada_mod",
    )(cv, ada_w, ada_b.reshape(depth, 1, n6))


def _modtab(mods_l, bsz):
    m = mods_l.reshape(mods_l.shape[0], 6, D_MODEL)
    lat = m[:bsz]
    ctx = jnp.broadcast_to(m[bsz][None], (bsz, 6, D_MODEL))
    return jnp.stack([ctx, lat], axis=1)


def _inproj0_kernel(cp_ref, cm_ref, cn_ref, xp_ref, xm_ref, xn_ref, mod_ref, g_ref, w_ref, acw_ref, acb_ref, bcw_ref,
                    gpar_ref, ua_ref, gay_ref, q_ref, k_ref, v_ref, sz_ref, gb_ref, u_scr, *, tm, ctx_tiles, n_tiles):
    t = pl.program_id(1)
    pick = lambda c_ref, x_ref: jnp.where(t < ctx_tiles, c_ref[...], x_ref[...])
    x = jnp.concatenate([pick(cp_ref, xp_ref), pick(cm_ref, xm_ref), pick(cn_ref, xn_ref)], axis=0)
    xm = _normmod(x, g_ref[...], mod_ref[0:1, :], mod_ref[1:2, :]).astype(BF16)
    seg_first = jnp.logical_or(t == 0, t == ctx_tiles)
    seg_last = jnp.logical_or(t == ctx_tiles - 1, t == n_tiles - 1)

    def project(c0, width, conv_input):
        u_scr[:, c0:c0 + width] = _dot(xm, w_ref[:, c0:c0 + width])
        if conv_input:
            u_scr[0:8, c0:c0 + width] = jnp.where(seg_first, 0.0, u_scr[0:8, c0:c0 + width])
            u_scr[tm + 8:tm + 16, c0:c0 + width] = jnp.where(seg_last, 0.0, u_scr[tm + 8:tm + 16, c0:c0 + width])

    def conv(c0, width, w_ref_, w0):
        acc = u_scr[6:6 + tm, c0:c0 + width] * w_ref_[0:1, w0:w0 + width]
        for j in range(1, CONV_K):
            acc = acc + u_scr[6 + j:6 + j + tm, c0:c0 + width] * w_ref_[j:j + 1, w0:w0 + width]
        return acc

    project(0, RG_WIDTH, True)
    for grp in range(RG_WIDTH // 128):
        c0 = grp * 128
        ua_ref[:, c0:c0 + 128] = conv(c0, 128, acw_ref, c0) + acb_ref[0:1, c0:c0 + 128]
    project(512, 512, False)
    gay_ref[...] = _gelu_tanh(u_scr[8:8 + tm, 512:1024]).astype(BF16)

    for grp in range(3 * DN_HEADS):
        c0 = grp * 128
        if grp % DN_HEADS == 0:
            project(1024 + c0, DN_HEADS * DN_D, True)
        y = _silu(conv(1024 + c0, 128, bcw_ref, c0))
        if grp < 2 * DN_HEADS:
            y = y * lax.rsqrt(jnp.sum(y * y, axis=-1, keepdims=True) + EPS)
        if grp < DN_HEADS:
            q_ref[:, c0:c0 + 128] = (y * (DN_D ** -0.5)).astype(BF16)
        elif grp < 2 * DN_HEADS:
            k_ref[:, c0 - 512:c0 - 384] = y.astype(BF16)
        else:
            v_ref[:, c0 - 1024:c0 - 896] = y.astype(BF16)
    project(2560, 512, False)
    sz_ref[...] = _silu(u_scr[8:8 + tm, 2560:3072]).astype(BF16)

    project(3072, 128, False)
    xg = u_scr[8:8 + tm, 3072:3200]
    lane = lax.broadcasted_iota(jnp.int32, xg.shape, 1)
    g = -jnp.exp(gpar_ref[0:1, :]) * _softplus(xg + gpar_ref[1:2, :])
    gb_ref[...] = jnp.where(lane < 2 * DN_HEADS, _sigmoid(xg), g)


def _row_specs(tm, d, ctx_tiles, ctx_len, seq, halo):
    tb = tm // 8

    def specs(n_rows, tile_of):
        main = pl.BlockSpec((None, tm, d), lambda b, t: (b, jnp.clip(tile_of(t), 0, n_rows // tm - 1), 0))
        if not halo:
            return [main]
        prev = pl.BlockSpec((None, 8, d), lambda b, t: (b, jnp.clip(tile_of(t) * tb - 1, 0, n_rows // 8 - 1), 0))
        nxt = pl.BlockSpec((None, 8, d), lambda b, t: (b, jnp.clip((tile_of(t) + 1) * tb, 0, n_rows // 8 - 1), 0))
        return [prev, main, nxt]

    return specs(ctx_len, lambda t: t) + specs(seq, lambda t: t - ctx_tiles)


def _inproj0(ctx, x, modtab, g, w_pad, acw, acb, bcw, gpar, *, tm):
    bsz, ctx_len, d = ctx.shape
    seq = x.shape[1]
    t_all = ctx_len + seq
    n_tiles = t_all // tm
    ctx_tiles = ctx_len // tm
    kern = functools.partial(_inproj0_kernel, tm=tm, ctx_tiles=ctx_tiles, n_tiles=n_tiles)
    tok = lambda w, dt=BF16: jax.ShapeDtypeStruct((bsz, t_all, w), dt)
    tok_spec = lambda w: pl.BlockSpec((None, tm, w), lambda b, t: (b, t, 0))
    return pl.pallas_call(
        kern,
        grid=(bsz, n_tiles),
        in_specs=_row_specs(tm, d, ctx_tiles, ctx_len, seq, True) + [
            pl.BlockSpec((None, None, 6, d), lambda b, t: (b, jnp.where(t >= ctx_tiles, 1, 0), 0, 0)),
            _const_spec((1, d)),
            _const_spec((d, E_IN_PAD)),
            _const_spec((CONV_K, RG_WIDTH)),
            _const_spec((1, RG_WIDTH)),
            _const_spec((CONV_K, 3 * DN_HEADS * DN_D)),
            _const_spec((2, 128)),
        ],
        out_specs=[tok_spec(512), tok_spec(512), tok_spec(512), tok_spec(512), tok_spec(512), tok_spec(512),
                   tok_spec(128)],
        out_shape=[tok(512, F32), tok(512), tok(512), tok(512), tok(512), tok(512), tok(128, F32)],
        scratch_shapes=[pltpu.VMEM((tm + 16, E_IN_PAD), F32)],
        compiler_params=_params(("arbitrary", "arbitrary"), 40 << 20),
        name="l0_inproj",
    )(ctx, ctx, ctx, x, x, x, modtab, g, w_pad, acw, acb, bcw, gpar)


def _rglru_kernel(uf_ref, ub_ref, wg_ref, gbias_ref, lam_ref, hf_ref, hb_ref,
                  af_scr, xf_scr, ab_scr, xb_scr, h_scr, *, tt, bsz):
    s = pl.program_id(0)

    @pl.when(s == 0)
    def _():
        h_scr[...] = jnp.zeros_like(h_scr)

    def gates(u_ref, d, a_scr, x_scr):
        x = u_ref[...].reshape(tt * bsz, RG_WIDTH)
        xb = x.astype(BF16)
        for half in range(2):
            c0 = half * 256
            xh = xb[:, c0:c0 + 256]
            r = _sigmoid_tanh(_dot(xh, wg_ref[d, 0, half]) + gbias_ref[2 * d:2 * d + 1, c0:c0 + 256])
            i = _sigmoid_tanh(_dot(xh, wg_ref[d, 1, half]) + gbias_ref[2 * d + 1:2 * d + 2, c0:c0 + 256])
            log_a = (-RG_C) * r * _softplus(-lam_ref[d:d + 1, c0:c0 + 256])
            a = jnp.exp(log_a)
            mult = jnp.sqrt(-jnp.tanh(log_a) * (a * a + 1.0))
            xin = mult * (i * x[:, c0:c0 + 256])
            a_scr[:, :, c0:c0 + 256] = a.reshape(tt, bsz, 256)
            x_scr[:, :, c0:c0 + 256] = xin.reshape(tt, bsz, 256)

    gates(uf_ref, 0, af_scr, xf_scr)
    gates(ub_ref, 1, ab_scr, xb_scr)

    def step(t, carry):
        hf, hb = carry
        hf = af_scr[t] * hf + xf_scr[t]
        hf_ref[t] = hf
        tb = tt - 1 - t
        hb = ab_scr[tb] * hb + xb_scr[tb]
        hb_ref[tb] = hb
        return hf, hb

    hf, hb = lax.fori_loop(0, tt, step, (h_scr[0], h_scr[1]), unroll=8)
    h_scr[0] = hf
    h_scr[1] = hb


def _rglru(ua3, wg, gbias, lam, *, tt, ctx_len):
    t_all, bsz, w = ua3.shape
    n_steps = t_all // tt
    nc = ctx_len // tt

    def bwd(s):
        return jnp.where(s < nc, nc - 1 - s, n_steps + nc - 1 - s)

    blk = (tt, bsz, w)
    kern = functools.partial(_rglru_kernel, tt=tt, bsz=bsz)
    return pl.pallas_call(
        kern,
        grid=(n_steps,),
        in_specs=[pl.BlockSpec(blk, lambda s: (s, 0, 0)),
                  pl.BlockSpec(blk, lambda s: (bwd(s), 0, 0)),
                  _const_spec(wg.shape), _const_spec(gbias.shape), _const_spec(lam.shape)],
        out_specs=[pl.BlockSpec(blk, lambda s: (s, 0, 0)),
                   pl.BlockSpec(blk, lambda s: (bwd(s), 0, 0))],
        out_shape=[jax.ShapeDtypeStruct(ua3.shape, F32)] * 2,
        scratch_shapes=[pltpu.VMEM(blk, F32)] * 4 + [pltpu.VMEM((2, bsz, w), F32)],
        compiler_params=_params(("arbitrary",), 40 << 20),
        name="l0_rglru",
    )(ua3, ua3, wg, gbias, lam)


def _delta_kernel(qf_ref, kf_ref, vf_ref, gf_ref, qb_ref, kb_ref, vb_ref, gb_ref, of_ref, ob_ref, s_scr, *, n_sub):
    c = DN_CHUNK

    @pl.when(pl.program_id(1) == 0)
    def _():
        s_scr[...] = jnp.zeros_like(s_scr)

    dir_refs = ((qf_ref, kf_ref, vf_ref, gf_ref, of_ref), (qb_ref, kb_ref, vb_ref, gb_ref, ob_ref))
    masks = [_scan_masks(c, d) for d in range(2)]
    eye = jnp.where(masks[0][3], 1.0, 0.0)

    cums = {}
    for d in range(2):
        incl, _, incl_t, _ = masks[d]
        m_incl = jnp.where(incl, 1.0, 0.0)
        m_incl_t = jnp.where(incl_t, 1.0, 0.0)
        for ci in range(n_sub):
            g_all = dir_refs[d][3][ci * c:(ci + 1) * c, :]
            gc_all = _mask_dot(m_incl, g_all)
            gct_all = _mask_dot_tn(g_all, m_incl_t)
            cums[d, ci] = (g_all, gc_all, gct_all)

    chains = []
    for d in range(2):
        q_ref, k_ref, v_ref, _, _ = dir_refs[d]
        incl, strict, _, _ = masks[d]
        last = c - 1 if d == 0 else 0
        for ci in range(n_sub):
            g_all, gc_all, gct_all = cums[d, ci]
            rs = slice(ci * c, (ci + 1) * c)
            for h in range(DN_HEADS):
                hs = slice(h * DN_D, (h + 1) * DN_D)
                lane = 2 * DN_HEADS + d * DN_HEADS + h
                ch = dict(d=d, ci=ci, h=h, rs=rs, hs=hs, incl=incl, strict=strict)
                ch["beta"] = g_all[:, d * DN_HEADS + h:d * DN_HEADS + h + 1]
                gc = jnp.broadcast_to(gc_all[:, lane:lane + 1], (c, DN_D))
                gc_row = jnp.broadcast_to(gct_all[lane:lane + 1, :], (c, c))
                ch["gc"] = gc
                ch["gtot"] = gc[last:last + 1, :]
                ch["decay"] = jnp.where(incl, jnp.exp(jnp.minimum(gc[:, 0:c] - gc_row, 0.0)), 0.0)
                ch["e_gc"] = jnp.exp(gc)
                ch["q"] = q_ref[rs, hs].astype(F32)
                ch["k"] = k_ref[rs, hs].astype(F32)
                ch["v"] = v_ref[rs, hs].astype(F32)
                chains.append(ch)

    for ch in chains:
        ch["kb"] = ch["k"] * ch["beta"]
        qk = _dot_nt(jnp.concatenate([ch["kb"], ch["q"]], axis=0).astype(BF16), ch["k"].astype(BF16))
        ch["neg"] = -jnp.where(ch["strict"], qk[0:c] * ch["decay"], 0.0)
        ch["a_qk"] = (qk[c:2 * c] * ch["decay"]).astype(BF16)
    for ch in chains:
        negb = ch["neg"].astype(BF16)
        ch["t"] = eye + ch["neg"]
        ch["p"] = _dot(negb, negb)
    n_sq = max(1, (c - 1).bit_length() - 1)
    for it in range(n_sq):
        for ch in chains:
            tp = _dot(jnp.concatenate([ch["t"], ch["p"]], axis=0).astype(BF16), ch["p"].astype(BF16))
            ch["t"] = ch["t"] + tp[0:c]
            ch["p"] = tp[c:2 * c]
    for ch in chains:
        rhs = jnp.concatenate([ch["v"] * ch["beta"], ch["kb"] * ch["e_gc"]], axis=1).astype(BF16)
        sol = _dot(ch["t"].astype(BF16), rhs)
        ch["u"] = sol[:, 0:DN_D]
        ch["wq"] = jnp.concatenate([sol[:, DN_D:2 * DN_D], ch["q"] * ch["e_gc"]], axis=0).astype(BF16)
        ch["k_tail"] = (ch["k"] * jnp.exp(ch["gtot"] - ch["gc"])).astype(BF16)

    by_key = {(ch["d"], ch["ci"], ch["h"]): ch for ch in chains}
    for step in range(n_sub):
        live = [by_key[d, step if d == 0 else n_sub - 1 - step, h] for d in range(2) for h in range(DN_HEADS)]
        for ch in live:
            ch["st"] = s_scr[ch["d"], ch["h"]]
            ch["ws"] = _dot(ch["wq"], ch["st"].astype(BF16))
        for ch in live:
            vnb = (ch["u"] - ch["ws"][0:c]).astype(BF16)
            o = ch["ws"][c:2 * c] + _dot(ch["a_qk"], vnb)
            dir_refs[ch["d"]][4][ch["rs"], ch["hs"]] = o.astype(BF16)
            s_scr[ch["d"], ch["h"]] = ch["st"] * jnp.exp(ch["gtot"]) + _dot_tn(ch["k_tail"], vnb)


def _delta(q, k, v, gb, *, ctx_len, rows):
    bsz, t_all, w = q.shape
    n_steps = t_all // rows
    nc = ctx_len // rows

    def bwd(s):
        return jnp.where(s < nc, nc - 1 - s, n_steps + nc - 1 - s)

    fwd_spec = lambda width: pl.BlockSpec((None, rows, width), lambda b, s: (b, s, 0))
    bwd_spec = lambda width: pl.BlockSpec((None, rows, width), lambda b, s: (b, bwd(s), 0))
    return pl.pallas_call(
        functools.partial(_delta_kernel, n_sub=rows // DN_CHUNK),
        grid=(bsz, n_steps),
        in_specs=[fwd_spec(w), fwd_spec(w), fwd_spec(w), fwd_spec(128),
                  bwd_spec(w), bwd_spec(w), bwd_spec(w), bwd_spec(128)],
        out_specs=[fwd_spec(w), bwd_spec(w)],
        out_shape=[jax.ShapeDtypeStruct((bsz, t_all, w), BF16)] * 2,
        scratch_shapes=[pltpu.VMEM((2, DN_HEADS, DN_D, DN_D), F32)],
        compiler_params=_params(("arbitrary", "arbitrary"), 32 << 20),
        name="l0_deltanet",
    )(q, k, v, gb, q, k, v, gb)


def _head_norm(y, g):
    return y * lax.rsqrt(jnp.mean(y * y, axis=-1, keepdims=True) + EPS) * g


def _l0_tail_kernel(ha_ref, gay_ref, o0_ref, o1_ref, sz_ref, ctx_ref, x_ref, mod_ref, ng_ref, wo_ref, g_ref, wg_ref,
                    wu_ref, wd_ref, hc_ref, hl_ref, *, n_chunks, ctx_tiles):
    t = pl.program_id(1)
    tm = ha_ref.shape[0]
    halves = [dict(rs=slice(i * tm // 2, (i + 1) * tm // 2)) for i in range(2)]
    for hv in halves:
        rs = hv["rs"]
        parts = [(ha_ref[rs, :] * gay_ref[rs, :].astype(F32)).astype(BF16)]
        for hd in range(DN_HEADS):
            lo = hd * DN_D
            ob = o0_ref[rs, lo:lo + DN_D].astype(F32) + o1_ref[rs, lo:lo + DN_D].astype(F32)
            parts.append((_head_norm(ob, ng_ref[...]) * sz_ref[rs, lo:lo + DN_D].astype(F32)).astype(BF16))
        hv["ycat"] = jnp.concatenate(parts, axis=-1)
    for hv in halves:
        h = jnp.where(t < ctx_tiles, ctx_ref[hv["rs"], :], x_ref[hv["rs"], :])
        hv["x"] = h + mod_ref[2:3, :] * _dot(hv["ycat"], wo_ref[...])
    for hv in halves:
        hv["xm"] = _normmod(hv["x"], g_ref[...], mod_ref[3:4, :], mod_ref[4:5, :]).astype(BF16)
        hv["acc"] = jnp.zeros(hv["x"].shape, F32)
    cw = D_FF // n_chunks
    for ci in range(n_chunks):
        c0 = ci * cw
        for hv in halves:
            hv["act"] = (_silu(_dot(hv["xm"], wg_ref[:, c0:c0 + cw])) * _dot(hv["xm"], wu_ref[:, c0:c0 + cw])).astype(BF16)
        for hv in halves:
            hv["acc"] = hv["acc"] + _dot(hv["act"], wd_ref[c0:c0 + cw, :])
    out = jnp.concatenate([hv["x"] + mod_ref[5:6, :] * hv["acc"] for hv in halves], axis=0)

    @pl.when(t < ctx_tiles)
    def _():
        hc_ref[...] = out

    @pl.when(t >= ctx_tiles)
    def _():
        hl_ref[...] = out


def _l0_tail(ha, gay, o0, o1, sz, ctx, x, modtab, ng, wo, g, wg, wu, wd, *, tm):
    bsz, ctx_len, d = ctx.shape
    seq = x.shape[1]
    t_all = ctx_len + seq
    ctx_tiles = ctx_len // tm
    tok = lambda width: pl.BlockSpec((None, tm, width), lambda b, t: (b, t, 0))
    return pl.pallas_call(
        functools.partial(_l0_tail_kernel, n_chunks=2, ctx_tiles=ctx_tiles),
        grid=(bsz, t_all // tm),
        in_specs=[tok(512), tok(512), tok(512), tok(512), tok(512)] + _row_specs(tm, d, ctx_tiles, ctx_len, seq, False) + [
                  pl.BlockSpec((None, None, 6, d), lambda b, t: (b, jnp.where(t >= ctx_tiles, 1, 0), 0, 0)),
                  _const_spec((1, DN_D)), _const_spec((d, d)),
                  _const_spec((1, d)), _const_spec((d, D_FF)), _const_spec((d, D_FF)), _const_spec((D_FF, d))],
        out_specs=[pl.BlockSpec((None, tm, d), lambda b, t: (b, jnp.minimum(t, ctx_tiles - 1), 0)),
                   pl.BlockSpec((None, tm, d), lambda b, t: (b, jnp.maximum(t - ctx_tiles, 0), 0))],
        out_shape=[jax.ShapeDtypeStruct((bsz, ctx_len, d), F32), jax.ShapeDtypeStruct((bsz, t_all - ctx_len, d), F32)],
        compiler_params=_params(("arbitrary", "arbitrary"), 48 << 20),
        name="l0_tail",
    )(ha, gay, o0, o1, sz, ctx, x, modtab, ng, wo, g, wg, wu, wd)


def _inproj1_kernel(h_ref, mod_ref, g_ref, w_ref, lbl_ref, wlr_ref, b2_ref, pa_ref, pb_ref, u_scr, *, layer):
    x = h_ref[...]
    xm = _normmod(x, g_ref[...], mod_ref[0:1, :], mod_ref[1:2, :]).astype(BF16)

    lg = lbl_ref[...]
    ex = jnp.exp(lg - jnp.max(lg, axis=0, keepdims=True))
    lbw = ex / jnp.sum(ex, axis=0, keepdims=True)
    lb = jnp.sum(lbw[1:layer + 1], axis=0, keepdims=True)

    def project(c0, width):
        u_scr[:, c0:c0 + width] = _dot(xm, w_ref[:, c0:c0 + width])

    def put(seg, off, val):
        ref = (pa_ref, pb_ref)[seg[0]]
        ref[:, seg[1] * SEG + off:seg[1] * SEG + off + val.shape[1]] = val.astype(ref.dtype)

    def groups(fn):
        for grp in range(SEG // 128):
            fn(grp * 128)

    project(0, SEG)
    groups(lambda c0: put(S_HQ, c0, _silu(u_scr[:, c0:c0 + 128]) * (HG_D ** -0.5)))
    for dr, (sk, sf) in enumerate(((S_HK0, S_HLF0), (S_HK1, S_HLF1))):
        project(512 + dr * 512, SEG)

        def forget(c0, dr=dr, sk=sk, sf=sf):
            lbg = lb[:, c0:c0 + 128]
            fl = u_scr[:, 512 + dr * 512 + c0:512 + dr * 512 + c0 + 128]
            put(sf, c0, jnp.log(lbg + (1.0 - lbg) * _sigmoid(fl)))
            put(sk, c0, (1.0 - lbg) * _sigmoid(-fl))
        groups(forget)
    project(1536, SEG)
    groups(lambda c0: put(S_HV, c0, u_scr[:, 1536 + c0:1536 + c0 + 128]))
    project(2048, SEG)
    groups(lambda c0: put(S_CG, c0, _silu(u_scr[:, 2048 + c0:2048 + c0 + 128])))
    project(2560, SEG)
    put(S_GQK, 0, u_scr[:, 2560:2816] * (GLA_DK ** -0.5))
    put(S_GQK, 256, u_scr[:, 2816:3072])
    project(3072, SEG)
    groups(lambda c0: put(S_GV, c0, u_scr[:, 3072 + c0:3072 + c0 + 128]))
    project(3584, SEG)
    groups(lambda c0: put(S_DG, c0, _silu(u_scr[:, 3584 + c0:3584 + c0 + 128])))
    project(4096, 128)
    lr = u_scr[:, 4096:4224]
    put(S_GLD, 0, -_softplus(-(_dot_hilo(lr, wlr_ref[...]) + b2_ref[...])) * (1.0 / GLA_GATE_NORM))


def _inproj1(h, row0, n_rows, modtab, seg, g, w_pad, lbl, wlr, b2, *, tm, layer):
    bsz, _, d = h.shape
    t0 = row0 // tm
    return pl.pallas_call(
        functools.partial(_inproj1_kernel, layer=layer),
        grid=(bsz, n_rows // tm),
        in_specs=[pl.BlockSpec((None, tm, d), lambda b, t: (b, t0 + t, 0)),
                  pl.BlockSpec((None, None, 6, d), lambda b, t: (b, seg, 0, 0)),
                  _const_spec((1, d)), _const_spec((d, O_IN_PAD)), _const_spec(lbl.shape),
                  _const_spec((128, SEG)), _const_spec((1, SEG))],
        out_specs=[pl.BlockSpec((None, tm, n * SEG), lambda b, t: (b, t, 0)) for n in N_SEG],
        out_shape=[jax.ShapeDtypeStruct((bsz, n_rows, n * SEG), dt) for n, dt in zip(N_SEG, P1_DTYPES)],
        scratch_shapes=[pltpu.VMEM((tm, O_IN_PAD), F32)],
        compiler_params=_params(("arbitrary", "arbitrary"), 44 << 20),
        name="l1_inproj",
    )(h, modtab, g, w_pad, lbl, wlr, b2)


def _gla_stream(d, q_all, k_all, ld_all, v_all, st_ref, o_ref, o_lane0, r0, n_heads, dk, dv, incl, m_incl):
    c = k_all.shape[0]
    mid = c // 2 - 1 if d == 0 else c // 2
    last = c - 1 if d == 0 else 0
    bc = _mask_dot(m_incl, ld_all)
    m = bc[mid:mid + 1]
    btot = bc[last:last + 1]
    kn = k_all.astype(F32) * jnp.exp(m - bc)
    it = dict(d=d, r0=r0, st_ref=st_ref, o_ref=o_ref, o_lane0=o_lane0, n_heads=n_heads, dk=dk, dv=dv, incl=incl,
              c=c, kt=(kn * jnp.exp(btot - m)).astype(BF16), dec=jnp.exp(btot), v=v_all.astype(BF16),
              want_out=q_all is not None)
    if q_all is not None:
        qe = q_all.astype(F32) * jnp.exp(bc)
        it.update(qd=(qe * jnp.exp(-m)).astype(BF16), qe=qe.astype(BF16), knb=kn.astype(BF16))
    return it


def _gla_intra(it):
    dk = it["dk"]
    it["a"] = [jnp.where(it["incl"], _dot_nt(it["qd"][:, hd * dk:(hd + 1) * dk], it["knb"][:, hd * dk:(hd + 1) * dk]),
                         0.0).astype(BF16) for hd in range(it["n_heads"])]


def _gla_advance(it):
    d, dk, dv, c, st_ref = it["d"], it["dk"], it["dv"], it["c"], it["st_ref"]
    sts = [st_ref[d, hd] for hd in range(it["n_heads"])]
    if it["want_out"]:
        for hd in range(it["n_heads"]):
            v = it["v"][:, hd * dv:(hd + 1) * dv]
            o = _dot(it["a"][hd], v) + _dot_nt(it["qe"][:, hd * dk:(hd + 1) * dk], sts[hd].astype(BF16))
            it["o_ref"][it["r0"]:it["r0"] + c, it["o_lane0"] + hd * dv:it["o_lane0"] + (hd + 1) * dv] = o.astype(BF16)
    for hd in range(it["n_heads"]):
        ks = slice(hd * dk, (hd + 1) * dk)
        st_ref[d, hd] = sts[hd] * it["dec"][:, ks] + _dot_tn(it["v"][:, hd * dv:(hd + 1) * dv], it["kt"][:, ks])


def _mix1_body(dirs, n_sub, sh_ref, sg_ref):
    c = MIX1_CHUNK
    gw = GLA_HEADS * GLA_DK
    prepared = {}
    for d, (hq_ref, hv_ref, hk_ref, hlf_ref, gv_ref, gqk_ref, gld_ref, o_ref) in enumerate(dirs):
        incl = _scan_masks(c, d)[0]
        m_incl = jnp.where(incl, 1.0, 0.0)
        for ci in range(n_sub):
            r0 = ci * c
            rs = slice(r0, r0 + c)
            prepared[d, ci, 0] = _gla_stream(
                d, None if hq_ref is None else hq_ref[rs, :], hk_ref[rs, :], hlf_ref[rs, :], hv_ref[rs, :],
                sh_ref, o_ref, 0, r0, HG_HEADS, HG_D, HG_D, incl, m_incl)
            prepared[d, ci, 1] = _gla_stream(
                d, None if hq_ref is None else gqk_ref[rs, 0:gw], gqk_ref[rs, gw:2 * gw],
                gld_ref[rs, d * gw:(d + 1) * gw], gv_ref[rs, :],
                sg_ref, o_ref, HG_HEADS * HG_D, r0, GLA_HEADS, GLA_DK, GLA_DV, incl, m_incl)
    for it in prepared.values():
        if it["want_out"]:
            _gla_intra(it)
    for step in range(n_sub):
        for d in range(2):
            for stream in range(2):
                _gla_advance(prepared[d, step if d == 0 else n_sub - 1 - step, stream])


def _mix1_ctx_kernel(*refs, n_sub):
    fwd, bwd, (sh_ref, sg_ref) = refs[0:6], refs[6:12], refs[12:14]

    @pl.when(pl.program_id(1) == 0)
    def _():
        sh_ref[...] = jnp.zeros_like(sh_ref)
        sg_ref[...] = jnp.zeros_like(sg_ref)

    _mix1_body([(None,) + tuple(r) + (None,) for r in (fwd, bwd)], n_sub, sh_ref, sg_ref)


def _mix1_lat_kernel(*refs, n_sub):
    fwd, bwd = refs[0:7], refs[7:14]
    sh0_ref, sg0_ref, of_ref, ob_ref, sh_scr, sg_scr = refs[14:20]

    @pl.when(pl.program_id(1) == 0)
    def _():
        sh_scr[...] = sh0_ref[...]
        sg_scr[...] = sg0_ref[...]

    _mix1_body([tuple(fwd) + (of_ref,), tuple(bwd) + (ob_ref,)], n_sub, sh_scr, sg_scr)


def _mix1_specs(p1, rows, n_steps, segs_of_dir):
    specs, args = [], []
    for d in range(2):
        blk = (lambda b, s: s) if d == 0 else (lambda b, s: n_steps - 1 - s)
        for arr, sg in segs_of_dir(d):
            specs.append(pl.BlockSpec((None, rows, SEG), lambda b, s, blk=blk, sg=sg: (b, blk(b, s), sg)))
            args.append(p1[arr])
    return specs, args


_SH_SHAPE = (2, HG_HEADS, HG_D, HG_D)
_SG_SHAPE = (2, GLA_HEADS, GLA_DV, GLA_DK)


def _mix1_ctx(p1c, *, rows):
    bsz, ctx_len, _ = p1c[0].shape
    n_steps = ctx_len // rows
    segs = lambda d: (S_HV, (S_HK0, S_HK1)[d], (S_HLF0, S_HLF1)[d], S_GV, S_GQK, S_GLD)
    specs, args = _mix1_specs(p1c, rows, n_steps, segs)
    state = lambda shape: pl.BlockSpec((None,) + shape, lambda b, s: (b, 0, 0, 0, 0))
    return pl.pallas_call(
        functools.partial(_mix1_ctx_kernel, n_sub=rows // MIX1_CHUNK),
        grid=(bsz, n_steps),
        in_specs=specs,
        out_specs=[state(_SH_SHAPE), state(_SG_SHAPE)],
        out_shape=[jax.ShapeDtypeStruct((bsz,) + _SH_SHAPE, F32), jax.ShapeDtypeStruct((bsz,) + _SG_SHAPE, F32)],
        compiler_params=_params(("arbitrary", "arbitrary"), 32 << 20),
        name="l1_ctx_state",
    )(*args)


def _mix1_lat(p1l, sh0, sg0, *, rows):
    bsz, seq, _ = p1l[0].shape
    n_steps = seq // rows
    segs = lambda d: (S_HQ, S_HV, (S_HK0, S_HK1)[d], (S_HLF0, S_HLF1)[d], S_GV, S_GQK, S_GLD)
    specs, args = _mix1_specs(p1l, rows, n_steps, segs)
    state = lambda shape: pl.BlockSpec((None,) + shape, lambda b, s: (b, 0, 0, 0, 0))
    ow = HG_HEADS * HG_D + GLA_HEADS * GLA_DV
    return pl.pallas_call(
        functools.partial(_mix1_lat_kernel, n_sub=rows // MIX1_CHUNK),
        grid=(bsz, n_steps),
        in_specs=specs + [state(_SH_SHAPE), state(_SG_SHAPE)],
        out_specs=[pl.BlockSpec((None, rows, ow), lambda b, s: (b, s, 0)),
                   pl.BlockSpec((None, rows, ow), lambda b, s: (b, n_steps - 1 - s, 0))],
        out_shape=[jax.ShapeDtypeStruct((bsz, seq, ow), BF16)] * 2,
        scratch_shapes=[pltpu.VMEM(_SH_SHAPE, F32), pltpu.VMEM(_SG_SHAPE, F32)],
        compiler_params=_params(("arbitrary", "arbitrary"), 32 << 20),
        name="l1_scan",
    )(*args, sh0, sg0)


def _outproj1_kernel(o0_ref, o1_ref, gate_ref, h_ref, mod_ref, cng_ref, dng_ref, w_ref, out_ref):
    parts = []
    for hd in range(HG_HEADS + GLA_HEADS):
        lo = hd * 128
        y = o0_ref[:, lo:lo + 128].astype(F32) + o1_ref[:, lo:lo + 128].astype(F32)
        ng = cng_ref[...] if hd < HG_HEADS else dng_ref[...]
        parts.append((_head_norm(y, ng) * gate_ref[:, lo:lo + 128].astype(F32)).astype(BF16))
    y = _dot(jnp.concatenate(parts, axis=-1), w_ref[...])
    out_ref[...] = h_ref[...] + mod_ref[2:3, :] * y


def _outproj1(o0, o1, p1l, h, modtab, cng, dng, w, *, tm):
    bsz, seq, d = h.shape
    tok = pl.BlockSpec((None, tm, d), lambda b, t: (b, t, 0))
    return pl.pallas_call(
        _outproj1_kernel,
        grid=(bsz, seq // tm),
        in_specs=[tok, tok,
                  pl.BlockSpec((None, tm, 2 * SEG), lambda b, t: (b, t, S_CG[1] // 2)),
                  tok,
                  pl.BlockSpec((None, None, 6, d), lambda b, t: (b, 1, 0, 0)),
                  _const_spec((1, 128)), _const_spec((1, 128)), _const_spec((d, d))],
        out_specs=tok,
        out_shape=jax.ShapeDtypeStruct((bsz, seq, d), F32),
        compiler_params=_params(("arbitrary", "arbitrary"), 32 << 20),
        name="l1_outproj",
    )(o0, o1, p1l[S_CG[0]], h, modtab, cng, dng, w)


def _for_pieces(length, fn):
    done = 0
    for rows in MOE_SEG_PIECES:
        n = (length - done) // rows

        def body(p, carry, rows=rows, done=done):
            fn(done + p * rows, rows)
            return carry
        lax.fori_loop(0, n, body, 0)
        done = done + n * rows


def _moe_route_kernel(h_ref, mod_ref, ng_ref, rw_ref, rb_ref, slot_ref, seg_ref, xs_hbm,
                      xn_scr, xg_scr, zero_scr, base_smem, sem, *, tk):
    i = pl.program_id(0)
    pc = MOE_PIECE
    gr = MOE_GATHER_ROWS

    @pl.when(i == 0)
    def _():
        for e in range(N_EXPERTS):
            base_smem[e] = 0
        zero_scr[...] = jnp.zeros_like(zero_scr)

    xm = _normmod(h_ref[...], ng_ref[...], mod_ref[3:4, :], mod_ref[4:5, :])
    xn_scr[...] = xm.astype(BF16)
    lg = jnp.transpose(_dot_hilo(xm, rw_ref[...]))[0:N_EXPERTS, :] + rb_ref[...]
    eidx = lax.broadcasted_iota(jnp.int32, lg.shape, 0).astype(F32)
    m1 = jnp.max(lg, axis=0, keepdims=True)
    i1 = jnp.min(jnp.where(lg == m1, eidx, float(N_EXPERTS)), axis=0, keepdims=True)
    lg2 = jnp.where(eidx == i1, -jnp.inf, lg)
    m2 = jnp.max(lg2, axis=0, keepdims=True)
    i2 = jnp.min(jnp.where(lg2 == m2, eidx, float(N_EXPERTS)), axis=0, keepdims=True)
    ex = jnp.exp(m2 - m1)
    p1 = 1.0 / (1.0 + ex)
    sel = jnp.where(eidx == i1, 1.0, 0.0) + jnp.where(eidx == i2, 1.0, 0.0)
    lane = lax.broadcasted_iota(jnp.int32, lg.shape, 1)
    cum = sel
    sh = 1
    while sh < tk:
        cum = cum + jnp.where(lane >= sh, pltpu.roll(cum, sh, 1), 0.0)
        sh *= 2
    padded = jnp.floor((cum[:, tk - 1:tk] + (pc - 1.0)) * (1.0 / pc)) * pc
    padded = jnp.broadcast_to(padded, (N_EXPERTS, 128))
    er = lax.broadcasted_iota(jnp.int32, (N_EXPERTS, N_EXPERTS), 0)
    ec = lax.broadcasted_iota(jnp.int32, (N_EXPERTS, N_EXPERTS), 1)
    off = _dot_hi(jnp.where(er > ec, 1.0, 0.0), padded)
    slot = off[:, 0:1] + cum - 1.0
    slot_a = jnp.sum(jnp.where(eidx == i1, slot, 0.0), axis=0, keepdims=True)
    slot_b = jnp.sum(jnp.where(eidx == i2, slot, 0.0), axis=0, keepdims=True)
    slot_ref[...] = jnp.concatenate([slot_a, slot_b, p1, ex * p1, jnp.zeros((4, tk), F32)], axis=0)

    total = jnp.max(off[N_EXPERTS - 1:N_EXPERTS, :] + padded[N_EXPERTS - 1:N_EXPERTS, :]).astype(jnp.int32)

    def gather(ci, carry):
        r0 = pl.multiple_of(ci * gr, gr)
        rid = (lax.broadcasted_iota(jnp.int32, (gr, tk), 0) + r0).astype(F32)
        p = jnp.where(rid == slot_a, 1.0, 0.0) + jnp.where(rid == slot_b, 1.0, 0.0)
        xg_scr[pl.ds(r0, gr), :] = _dot(p.astype(BF16), xn_scr[...]).astype(BF16)
        return carry
    lax.fori_loop(0, (total + gr - 1) // gr, gather, 0)

    erow = lax.broadcasted_iota(jnp.int32, (N_EXPERTS, 128), 0)
    base_vec = jnp.zeros((N_EXPERTS, 128), F32)
    segs = []
    for e in range(N_EXPERTS):
        off_e = jnp.max(off[e:e + 1, :]).astype(jnp.int32)
        len_e = jnp.max(padded[e:e + 1, :]).astype(jnp.int32)
        base_e = base_smem[e]
        base_vec = jnp.where(erow == e, base_e.astype(F32), base_vec)
        segs.append((e, off_e, len_e, base_e))
    seg_ref[0] = off
    seg_ref[1] = padded
    seg_ref[2] = base_vec

    def seg_copy(e, off_e, base_e, r, rows):
        return pltpu.make_async_copy(
            xg_scr.at[pl.ds(pl.multiple_of(off_e + r, pc), rows), :],
            xs_hbm.at[e, pl.ds(pl.multiple_of(base_e + r, pc), rows), :], sem)

    for e, off_e, len_e, base_e in segs:
        _for_pieces(len_e, lambda r, rows, e=e, off_e=off_e, base_e=base_e: seg_copy(e, off_e, base_e, r, rows).start())
    for e, off_e, len_e, base_e in segs:
        _for_pieces(len_e, lambda r, rows, e=e, off_e=off_e, base_e=base_e: seg_copy(e, off_e, base_e, r, rows).wait())
        base_smem[e] = base_e + len_e

    @pl.when(i == pl.num_programs(0) - 1)
    def _():
        def tail_copy(e, p):
            end = base_smem[e]
            return pltpu.make_async_copy(zero_scr, xs_hbm.at[e, pl.ds(pl.multiple_of(end + p * pc, pc), pc), :], sem)

        def n_tail(e):
            rem = lax.rem(base_smem[e], MOE_BLOCK)
            return jnp.where(rem == 0, 0, MOE_BLOCK - rem) // pc

        for e in range(N_EXPERTS):
            def start(p, carry, e=e):
                tail_copy(e, p).start()
                return carry
            lax.fori_loop(0, n_tail(e), start, 0)
        for e in range(N_EXPERTS):
            def wait(p, carry, e=e):
                tail_copy(e, p).wait()
                return carry
            lax.fori_loop(0, n_tail(e), wait, 0)


def _moe_ffn_kernel(eid_ref, blk_ref, nv_ref, x_ref, wg_ref, wu_ref, wd_ref, o_ref, *, n_chunks):
    del eid_ref, blk_ref

    @pl.when(pl.program_id(0) < nv_ref[0])
    def _():
        x = x_ref[...]
        cw = D_FF // n_chunks
        acc = jnp.zeros(x.shape, F32)
        for ci in range(n_chunks):
            c0 = ci * cw
            act = (_silu(_dot(x, wg_ref[:, c0:c0 + cw])) * _dot(x, wu_ref[:, c0:c0 + cw])).astype(BF16)
            acc = acc + _dot(act, wd_ref[c0:c0 + cw, :])
        o_ref[...] = acc.astype(BF16)


def _moe_combine_kernel(base_ref, len_ref, off_ref, h_ref, mod_ref, fg_ref, slot_ref, og_hbm, out_ref,
                        og_scr, sem, *, tk):
    i = pl.program_id(0)
    pc = MOE_PIECE
    gr = MOE_GATHER_ROWS
    n_rows = og_scr.shape[0]

    def seg_copy(e, r, rows):
        return pltpu.make_async_copy(
            og_hbm.at[e, pl.ds(pl.multiple_of(base_ref[i * N_EXPERTS + e] + r, pc), rows), :],
            og_scr.at[pl.ds(pl.multiple_of(off_ref[i * N_EXPERTS + e] + r, pc), rows), :], sem)

    for e in range(N_EXPERTS):
        _for_pieces(len_ref[i * N_EXPERTS + e], lambda r, rows, e=e: seg_copy(e, r, rows).start())

    last = i * N_EXPERTS + N_EXPERTS - 1
    total = off_ref[last] + len_ref[last]

    def clear(p, carry):
        og_scr[pl.ds(pl.multiple_of(p * pc, pc), pc), :] = jnp.zeros((pc, og_scr.shape[1]), BF16)
        return carry
    lax.fori_loop(total // pc, n_rows // pc, clear, 0)

    for e in range(N_EXPERTS):
        _for_pieces(len_ref[i * N_EXPERTS + e], lambda r, rows, e=e: seg_copy(e, r, rows).wait())

    out_ref[...] = jnp.zeros_like(out_ref)

    def scatter(ci, carry):
        r0 = pl.multiple_of(ci * gr, gr)
        rid = (lax.broadcasted_iota(jnp.int32, (gr, tk), 0) + r0).astype(F32)
        pg = (jnp.where(rid == slot_ref[0:1, :], slot_ref[2:3, :], 0.0)
              + jnp.where(rid == slot_ref[1:2, :], slot_ref[3:4, :], 0.0))
        out_ref[...] = out_ref[...] + _dot_tn(pg.astype(BF16), og_scr[pl.ds(r0, gr), :])
        return carry
    lax.fori_loop(0, (total + gr - 1) // gr, scatter, 0)
    h3 = h_ref[...] + mod_ref[5:6, :] * out_ref[...]
    out_ref[...] = h3 * lax.rsqrt(jnp.mean(h3 * h3, axis=-1, keepdims=True) + EPS) * fg_ref[...]


def _moe_block_table(seg, n_blocks):
    ends = (seg[-1, 2, :, 0] + seg[-1, 1, :, 0]).astype(jnp.int32)
    nblk = (ends + MOE_BLOCK - 1) // MOE_BLOCK
    cum = jnp.cumsum(nblk)
    n_valid = cum[-1]
    g = jnp.minimum(jnp.arange(n_blocks, dtype=jnp.int32), n_valid - 1)
    eid = jnp.sum((g[:, None] >= cum[None, :]).astype(jnp.int32), axis=1)
    blk = g - (cum - nblk)[eid]
    return eid, blk, n_valid.reshape(1)


def _moe(h, modtab, ng, fg, rw, rb, wg, wu, wd, *, tk):
    bsz, seq, d = h.shape
    tpb = seq // tk
    n_tiles = bsz * tpb
    n_tok = bsz * seq
    tile_rows = -(-(2 * tk + N_EXPERTS * MOE_PIECE) // MOE_GATHER_ROWS) * MOE_GATHER_ROWS
    cap = -(-(n_tok + n_tiles * MOE_PIECE) // MOE_BLOCK) * MOE_BLOCK
    n_blocks = -(-(2 * n_tok + n_tiles * N_EXPERTS * MOE_PIECE) // MOE_BLOCK) + N_EXPERTS
    tok = lambda i, *_: (i // tpb, i % tpb, 0)
    mod = lambda i, *_: (i // tpb, 1, 0, 0)

    slots, seg, xs = pl.pallas_call(
        functools.partial(_moe_route_kernel, tk=tk),
        grid=(n_tiles,),
        in_specs=[pl.BlockSpec((None, tk, d), tok), pl.BlockSpec((None, None, 6, d), mod),
                  _const_spec((1, d)), _const_spec((d, 128)), _const_spec((N_EXPERTS, 1))],
        out_specs=[pl.BlockSpec((None, 8, tk), lambda i: (i, 0, 0)),
                   pl.BlockSpec((None, 3, N_EXPERTS, 128), lambda i: (i, 0, 0, 0)),
                   pl.BlockSpec(memory_space=pl.ANY)],
        out_shape=[jax.ShapeDtypeStruct((n_tiles, 8, tk), F32),
                   jax.ShapeDtypeStruct((n_tiles, 3, N_EXPERTS, 128), F32),
                   jax.ShapeDtypeStruct((N_EXPERTS, cap, d), BF16)],
        scratch_shapes=[pltpu.VMEM((tk, d), BF16), pltpu.VMEM((tile_rows, d), BF16), pltpu.VMEM((MOE_PIECE, d), BF16),
                        pltpu.SMEM((N_EXPERTS,), jnp.int32), pltpu.SemaphoreType.DMA(())],
        compiler_params=_params(("arbitrary",), 40 << 20),
        name="l1_moe_route",
    )(h, modtab, ng, _pad_cols(rw, 128), rb)

    eid, blk, n_valid = _moe_block_table(seg, n_blocks)
    x_spec = pl.BlockSpec((None, MOE_BLOCK, d), lambda g, eid, blk, nv: (eid[g], blk[g], 0))
    og = pl.pallas_call(
        functools.partial(_moe_ffn_kernel, n_chunks=11),
        grid_spec=pltpu.PrefetchScalarGridSpec(
            num_scalar_prefetch=3, grid=(n_blocks,),
            in_specs=[x_spec,
                      pl.BlockSpec((None, d, D_FF), lambda g, eid, blk, nv: (eid[g], 0, 0)),
                      pl.BlockSpec((None, d, D_FF), lambda g, eid, blk, nv: (eid[g], 0, 0)),
                      pl.BlockSpec((None, D_FF, d), lambda g, eid, blk, nv: (eid[g], 0, 0))],
            out_specs=x_spec),
        out_shape=jax.ShapeDtypeStruct((N_EXPERTS, cap, d), BF16),
        compiler_params=_params(("arbitrary",), 52 << 20),
        name="l1_moe_experts",
    )(eid, blk, n_valid, xs, wg, wu, wd)

    tab = lambda k: seg[:, k, :, 0].astype(jnp.int32).reshape(-1)
    return pl.pallas_call(
        functools.partial(_moe_combine_kernel, tk=tk),
        grid_spec=pltpu.PrefetchScalarGridSpec(
            num_scalar_prefetch=3, grid=(n_tiles,),
            in_specs=[pl.BlockSpec((None, tk, d), tok), pl.BlockSpec((None, None, 6, d), mod),
                      pl.BlockSpec((1, d), lambda i, *_: (0, 0)),
                      pl.BlockSpec((None, 8, tk), lambda i, *_: (i, 0, 0)),
                      pl.BlockSpec(memory_space=pl.ANY)],
            out_specs=pl.BlockSpec((None, tk, d), tok),
            scratch_shapes=[pltpu.VMEM((tile_rows, d), BF16), pltpu.SemaphoreType.DMA(())]),
        out_shape=jax.ShapeDtypeStruct(h.shape, F32),
        compiler_params=_params(("arbitrary",), 40 << 20),
        name="l1_moe_combine",
    )(tab(2), tab(1), tab(0), h, modtab, fg, slots, og)


def _block_diag_gate(gate_w):
    w = gate_w.reshape(2, 2, 2, 4, RG_BLOCK, RG_BLOCK)
    eye = jnp.eye(4, dtype=gate_w.dtype)
    return jnp.einsum('dghbij,bc->dghbicj', w, eye).reshape(2, 2, 2, 256, 256)


def _pad_cols(w, n):
    return jnp.pad(w, ((0, 0), (0, n - w.shape[1])))


def _layer0(ctx, x, modtab, norm_mix_g, norm_ffn_g, e_w_in, e_w_out, e_a_conv_w, e_a_conv_b, e_a_gate_w, e_a_gate_b,
            e_a_lambda, e_b_conv_w, e_b_a_log, e_b_dt_bias, e_b_norm_g, e_ffn_w_gate, e_ffn_w_up, e_ffn_w_down,
            *, tm, tt):
    bsz, ctx_len, d = ctx.shape
    w_in = _pad_cols(e_w_in, E_IN_PAD).astype(BF16)
    gpar = jnp.zeros((2, 128), F32)
    gpar = gpar.at[0, 2 * DN_HEADS:4 * DN_HEADS].set(e_b_a_log.reshape(-1))
    gpar = gpar.at[1, 2 * DN_HEADS:4 * DN_HEADS].set(e_b_dt_bias.reshape(-1))
    ua, gay, q, k, v, sz, gb = _inproj0(ctx, x, modtab, norm_mix_g.reshape(1, d), w_in, e_a_conv_w,
                                        e_a_conv_b.reshape(1, -1), e_b_conv_w, gpar, tm=tm)
    wg = _block_diag_gate(e_a_gate_w).astype(BF16)
    hf, hb = _rglru(jnp.transpose(ua, (1, 0, 2)), wg, e_a_gate_b.reshape(4, RG_WIDTH), e_a_lambda,
                    tt=tt, ctx_len=ctx_len)
    ha = jnp.transpose(hf + hb, (1, 0, 2))
    o0, o1 = _delta(q, k, v, gb, ctx_len=ctx_len, rows=SCAN_ROWS)
    return _l0_tail(ha, gay, o0, o1, sz, ctx, x, modtab, e_b_norm_g.reshape(1, -1), e_w_out.astype(BF16),
                    norm_ffn_g.reshape(1, d), e_ffn_w_gate.astype(BF16), e_ffn_w_up.astype(BF16),
                    e_ffn_w_down.astype(BF16), tm=tm)


def _layer1(hc, hl, modtab, norm_mix_g, norm_ffn_g, final_norm_g, o_w_in, o_w_out, o_lb_logits, o_c_norm_g,
            o_d_gate_w2, o_d_gate_b2, o_d_norm_g, o_router_w, o_router_b, o_moe_w_gate, o_moe_w_up, o_moe_w_down,
            *, tm, tk, layer):
    bsz, seq, d = hl.shape
    ctx_len = hc.shape[1]
    rows = seq // GRID_W
    hl = hl.reshape(bsz, rows, GRID_W, d).swapaxes(1, 2).reshape(bsz, seq, d)
    w_in = _pad_cols(o_w_in, O_IN_PAD).astype(BF16)
    wlr = jnp.zeros((128, SEG), F32)
    wlr = wlr.at[0:GLA_RANK, 0:256].set(o_d_gate_w2[0]).at[GLA_RANK:2 * GLA_RANK, 256:512].set(o_d_gate_w2[1])
    proj = functools.partial(_inproj1, g=norm_mix_g.reshape(1, d), w_pad=w_in, lbl=o_lb_logits, wlr=wlr,
                             b2=o_d_gate_b2.reshape(1, SEG), layer=layer)
    p1c = proj(hc, 0, ctx_len, modtab, 0, tm=tm)
    p1l = proj(hl, 0, seq, modtab, 1, tm=2 * tm)
    sh0, sg0 = _mix1_ctx(p1c, rows=SCAN_ROWS)
    o0, o1 = _mix1_lat(p1l, sh0, sg0, rows=SCAN_ROWS)
    h2 = _outproj1(o0, o1, p1l, hl, modtab, o_c_norm_g.reshape(1, -1), o_d_norm_g.reshape(1, -1),
                   o_w_out.astype(BF16), tm=tm)
    return _moe(h2, modtab, norm_ffn_g.reshape(1, d), final_norm_g.reshape(1, d), o_router_w,
                o_router_b.reshape(N_EXPERTS, 1), o_moe_w_gate.astype(BF16), o_moe_w_up.astype(BF16),
                o_moe_w_down.astype(BF16), tk=tk)


def kernel(x, c, ctx, c_ctx, ada_w, ada_b, norm_mix_g, norm_ffn_g, final_norm_g, e_w_in, e_w_out, e_a_conv_w, e_a_conv_b, e_a_gate_w, e_a_gate_b, e_a_lambda, e_b_conv_w, e_b_a_log, e_b_dt_bias, e_b_norm_g, e_ffn_w_gate, e_ffn_w_up, e_ffn_w_down, o_w_in, o_w_out, o_lb_logits, o_c_norm_g, o_d_gate_w2, o_d_gate_b2, o_d_norm_g, o_router_w, o_router_b, o_moe_w_gate, o_moe_w_up, o_moe_w_down):
    bsz, seq, d = x.shape
    ctx_len = ctx.shape[1]
    assert bsz == 8 and d == D_MODEL and ada_w.shape[0] == 2
    tm = min(256, ctx_len)
    tt = min(128, ctx_len)
    tk = min(512, seq)
    assert ctx_len % tm == 0 and seq % tm == 0 and ctx_len % SCAN_ROWS == 0 and seq % SCAN_ROWS == 0
    assert seq % GRID_W == 0 and seq % tk == 0

    mods = _ada(c, c_ctx, ada_w, ada_b)
    hc, hl = _layer0(ctx, x, _modtab(mods[0], bsz), norm_mix_g[0], norm_ffn_g[0], e_w_in[0], e_w_out[0],
                     e_a_conv_w[0], e_a_conv_b[0], e_a_gate_w[0], e_a_gate_b[0], e_a_lambda[0], e_b_conv_w[0],
                     e_b_a_log[0], e_b_dt_bias[0], e_b_norm_g[0], e_ffn_w_gate[0], e_ffn_w_up[0], e_ffn_w_down[0],
                     tm=tm, tt=tt)
    out_cm = _layer1(hc, hl, _modtab(mods[1], bsz), norm_mix_g[1], norm_ffn_g[1], final_norm_g, o_w_in[0],
                     o_w_out[0], o_lb_logits, o_c_norm_g[0], o_d_gate_w2[0], o_d_gate_b2[0], o_d_norm_g[0],
                     o_router_w[0], o_router_b[0], o_moe_w_gate[0], o_moe_w_up[0], o_moe_w_down[0],
                     tm=tm, tk=tk, layer=1)
    rows = seq // GRID_W
    return out_cm.reshape(bsz, GRID_W, rows, d).swapaxes(1, 2).reshape(bsz, seq, d)
```

```python
import functools

import jax
import jax.numpy as jnp
from jax import lax
from jax.experimental import pallas as pl
from jax.experimental.pallas import tpu as pltpu

F32 = jnp.float32
BF16 = jnp.bfloat16
HI = lax.Precision.HIGHEST

EPS = 1e-6
D_MODEL = 1024
GRID_W = 64
CONV_K = 4
RG_WIDTH = 512
RG_BLOCK = 64
RG_C = 8.0
DN_HEADS = 4
DN_D = 128
DN_CHUNK = 64
HG_HEADS = 4
HG_D = 128
GLA_HEADS = 4
GLA_DK = 64
GLA_DV = 128
GLA_RANK = 16
GLA_GATE_NORM = 16.0
MIX1_CHUNK = 64
SCAN_ROWS = 256
D_FF = 2816
N_EXPERTS = 8

E_IN_PAD = 3200
O_IN_PAD = 4224
SEG = 512
S_HQ, S_HV, S_HK0, S_HK1, S_GV, S_GQK, S_CG, S_DG = [(0, i) for i in range(8)]
S_HLF0, S_HLF1, S_GLD = [(1, i) for i in range(3)]
N_SEG = (8, 3)
P1_DTYPES = (BF16, F32)

V7X_VMEM_BYTES = 64 * 1024 * 1024
VMEM_HEADROOM_BYTES = 8 * 1024 * 1024
MOE_PIECE = 16
MOE_SEG_PIECES = (64, MOE_PIECE)
MOE_BLOCK = 512
MOE_GATHER_ROWS = 256


def _vmem(nbytes):
    return int(min(V7X_VMEM_BYTES - VMEM_HEADROOM_BYTES, nbytes))


def _params(sem, vmem_bytes):
    return pltpu.CompilerParams(dimension_semantics=sem, vmem_limit_bytes=_vmem(vmem_bytes))


def _sigmoid(x):
    return jax.nn.sigmoid(x)


def _sigmoid_tanh(x):
    return 0.5 * jnp.tanh(0.5 * x) + 0.5


def _silu(x):
    return x * jax.nn.sigmoid(x)


def _softplus(x):
    return jnp.maximum(x, 0.0) + jnp.log1p(jnp.exp(-jnp.abs(x)))


def _gelu_tanh(x):
    return 0.5 * x * (1.0 + jnp.tanh(0.7978845608028654 * (x + 0.044715 * (x * x * x))))


def _normmod(x, g, shift, scale):
    y = x * lax.rsqrt(jnp.mean(x * x, axis=-1, keepdims=True) + EPS)
    return (y * g) * (1.0 + scale) + shift


def _dot(a, b):
    return jnp.dot(a, b, preferred_element_type=F32)


def _dot_nt(a, b):
    return lax.dot_general(a, b, (((1,), (1,)), ((), ())), preferred_element_type=F32)


def _dot_tn(a, b):
    return lax.dot_general(a, b, (((0,), (0,)), ((), ())), preferred_element_type=F32)


def _dot_hi(a, b):
    return jnp.dot(a, b, precision=HI, preferred_element_type=F32)


def _split3(x):
    hi = x.astype(BF16)
    r1 = x - hi.astype(F32)
    mid = r1.astype(BF16)
    return hi, mid, (r1 - mid.astype(F32)).astype(BF16)


def _dot_hilo(a, b):
    ah = a.astype(BF16)
    al = (a - ah.astype(F32)).astype(BF16)
    bh = b.astype(BF16)
    bl = (b - bh.astype(F32)).astype(BF16)
    return _dot(ah, bh) + (_dot(ah, bl) + _dot(al, bh))


def _mask_dot(mask, x):
    mb = mask.astype(BF16)
    hi, mid, lo = _split3(x)
    return _dot(mb, hi) + (_dot(mb, mid) + _dot(mb, lo))


def _mask_dot_tn(x, mask):
    mb = mask.astype(BF16)
    hi, mid, lo = _split3(x)
    return _dot_tn(hi, mb) + (_dot_tn(mid, mb) + _dot_tn(lo, mb))


def _const_spec(shape):
    nd = len(shape)
    return pl.BlockSpec(shape, lambda *_: (0,) * nd, pipeline_mode=pl.Buffered(1))


def _scan_masks(c, d):
    row = lax.broadcasted_iota(jnp.int32, (c, c), 0)
    col = lax.broadcasted_iota(jnp.int32, (c, c), 1)
    dlt = row - col if d == 0 else col - row
    return dlt >= 0, dlt > 0, dlt <= 0, row == col


def _ada_kernel(cv_ref, w_ref, b_ref, o_ref):
    s = _silu(cv_ref[...]).astype(BF16)
    o_ref[...] = _dot(s, w_ref[...].astype(BF16)) + b_ref[...]


def _ada(c, c_ctx, ada_w, ada_b):
    depth, d, n6 = ada_w.shape
    bsz = c.shape[0]
    rows = 16
    cv = jnp.zeros((rows, d), F32).at[:bsz].set(c).at[bsz].set(c_ctx)
    tn = 1536
    return pl.pallas_call(
        _ada_kernel,
        grid=(depth, n6 // tn),
        in_specs=[pl.BlockSpec((rows, d), lambda l, j: (0, 0)),
                  pl.BlockSpec((None, d, tn), lambda l, j: (l, 0, j)),
                  pl.BlockSpec((None, 1, tn), lambda l, j: (l, 0, j))],
        out_specs=pl.BlockSpec((None, rows, tn), lambda l, j: (l, 0, j)),
        out_shape=jax.ShapeDtypeStruct((depth, rows, n6), F32),
        compiler_params=_params(("arbitrary", "arbitrary"), 32 << 20),
        name="ada_mod",
    )(cv, ada_w, ada_b.reshape(depth, 1, n6))


def _modtab(mods_l, bsz):
    m = mods_l.reshape(mods_l.shape[0], 6, D_MODEL)
    lat = m[:bsz]
    ctx = jnp.broadcast_to(m[bsz][None], (bsz, 6, D_MODEL))
    return jnp.stack([ctx, lat], axis=1)


def _inproj0_kernel(cp_ref, cm_ref, cn_ref, xp_ref, xm_ref, xn_ref, mod_ref, g_ref, w_ref, acw_ref, acb_ref, bcw_ref,
                    gpar_ref, ua_ref, gay_ref, q_ref, k_ref, v_ref, sz_ref, gb_ref, u_scr, *, tm, ctx_tiles, n_tiles):
    t = pl.program_id(1)
    pick = lambda c_ref, x_ref: jnp.where(t < ctx_tiles, c_ref[...], x_ref[...])
    x = jnp.concatenate([pick(cp_ref, xp_ref), pick(cm_ref, xm_ref), pick(cn_ref, xn_ref)], axis=0)
    xm = _normmod(x, g_ref[...], mod_ref[0:1, :], mod_ref[1:2, :]).astype(BF16)
    seg_first = jnp.logical_or(t == 0, t == ctx_tiles)
    seg_last = jnp.logical_or(t == ctx_tiles - 1, t == n_tiles - 1)

    def project(c0, width, conv_input):
        u_scr[:, c0:c0 + width] = _dot(xm, w_ref[:, c0:c0 + width])
        if conv_input:
            u_scr[0:8, c0:c0 + width] = jnp.where(seg_first, 0.0, u_scr[0:8, c0:c0 + width])
            u_scr[tm + 8:tm + 16, c0:c0 + width] = jnp.where(seg_last, 0.0, u_scr[tm + 8:tm + 16, c0:c0 + width])

    def conv(c0, width, w_ref_, w0):
        acc = u_scr[6:6 + tm, c0:c0 + width] * w_ref_[0:1, w0:w0 + width]
        for j in range(1, CONV_K):
            acc = acc + u_scr[6 + j:6 + j + tm, c0:c0 + width] * w_ref_[j:j + 1, w0:w0 + width]
        return acc

    project(0, RG_WIDTH, True)
    for grp in range(RG_WIDTH // 128):
        c0 = grp * 128
        ua_ref[:, c0:c0 + 128] = conv(c0, 128, acw_ref, c0) + acb_ref[0:1, c0:c0 + 128]
    project(512, 512, False)
    gay_ref[...] = _gelu_tanh(u_scr[8:8 + tm, 512:1024]).astype(BF16)

    for grp in range(3 * DN_HEADS):
        c0 = grp * 128
        if grp % DN_HEADS == 0:
            project(1024 + c0, DN_HEADS * DN_D, True)
        y = _silu(conv(1024 + c0, 128, bcw_ref, c0))
        if grp < 2 * DN_HEADS:
            y = y * lax.rsqrt(jnp.sum(y * y, axis=-1, keepdims=True) + EPS)
        if grp < DN_HEADS:
            q_ref[:, c0:c0 + 128] = (y * (DN_D ** -0.5)).astype(BF16)
        elif grp < 2 * DN_HEADS:
            k_ref[:, c0 - 512:c0 - 384] = y.astype(BF16)
        else:
            v_ref[:, c0 - 1024:c0 - 896] = y.astype(BF16)
    project(2560, 512, False)
    sz_ref[...] = _silu(u_scr[8:8 + tm, 2560:3072]).astype(BF16)

    project(3072, 128, False)
    xg = u_scr[8:8 + tm, 3072:3200]
    lane = lax.broadcasted_iota(jnp.int32, xg.shape, 1)
    g = -jnp.exp(gpar_ref[0:1, :]) * _softplus(xg + gpar_ref[1:2, :])
    gb_ref[...] = jnp.where(lane < 2 * DN_HEADS, _sigmoid(xg), g)


def _row_specs(tm, d, ctx_tiles, ctx_len, seq, halo):
    tb = tm // 8

    def specs(n_rows, tile_of):
        main = pl.BlockSpec((None, tm, d), lambda b, t: (b, jnp.clip(tile_of(t), 0, n_rows // tm - 1), 0))
        if not halo:
            return [main]
        prev = pl.BlockSpec((None, 8, d), lambda b, t: (b, jnp.clip(tile_of(t) * tb - 1, 0, n_rows // 8 - 1), 0))
        nxt = pl.BlockSpec((None, 8, d), lambda b, t: (b, jnp.clip((tile_of(t) + 1) * tb, 0, n_rows // 8 - 1), 0))
        return [prev, main, nxt]

    return specs(ctx_len, lambda t: t) + specs(seq, lambda t: t - ctx_tiles)


def _inproj0(ctx, x, modtab, g, w_pad, acw, acb, bcw, gpar, *, tm):
    bsz, ctx_len, d = ctx.shape
    seq = x.shape[1]
    t_all = ctx_len + seq
    n_tiles = t_all // tm
    ctx_tiles = ctx_len // tm
    kern = functools.partial(_inproj0_kernel, tm=tm, ctx_tiles=ctx_tiles, n_tiles=n_tiles)
    tok = lambda w, dt=BF16: jax.ShapeDtypeStruct((bsz, t_all, w), dt)
    tok_spec = lambda w: pl.BlockSpec((None, tm, w), lambda b, t: (b, t, 0))
    return pl.pallas_call(
        kern,
        grid=(bsz, n_tiles),
        in_specs=_row_specs(tm, d, ctx_tiles, ctx_len, seq, True) + [
            pl.BlockSpec((None, None, 6, d), lambda b, t: (b, jnp.where(t >= ctx_tiles, 1, 0), 0, 0)),
            _const_spec((1, d)),
            _const_spec((d, E_IN_PAD)),
            _const_spec((CONV_K, RG_WIDTH)),
            _const_spec((1, RG_WIDTH)),
            _const_spec((CONV_K, 3 * DN_HEADS * DN_D)),
            _const_spec((2, 128)),
        ],
        out_specs=[tok_spec(512), tok_spec(512), tok_spec(512), tok_spec(512), tok_spec(512), tok_spec(512),
                   tok_spec(128)],
        out_shape=[tok(512, F32), tok(512), tok(512), tok(512), tok(512), tok(512), tok(128, F32)],
        scratch_shapes=[pltpu.VMEM((tm + 16, E_IN_PAD), F32)],
        compiler_params=_params(("arbitrary", "arbitrary"), 40 << 20),
        name="l0_inproj",
    )(ctx, ctx, ctx, x, x, x, modtab, g, w_pad, acw, acb, bcw, gpar)


def _rglru_kernel(uf_ref, ub_ref, wg_ref, gbias_ref, lam_ref, hf_ref, hb_ref,
                  af_scr, xf_scr, ab_scr, xb_scr, h_scr, *, tt, bsz):
    s = pl.program_id(0)

    @pl.when(s == 0)
    def _():
        h_scr[...] = jnp.zeros_like(h_scr)

    def gates(u_ref, d, a_scr, x_scr):
        x = u_ref[...].reshape(tt * bsz, RG_WIDTH)
        xb = x.astype(BF16)
        for half in range(2):
            c0 = half * 256
            xh = xb[:, c0:c0 + 256]
            r = _sigmoid_tanh(_dot(xh, wg_ref[d, 0, half]) + gbias_ref[2 * d:2 * d + 1, c0:c0 + 256])
            i = _sigmoid_tanh(_dot(xh, wg_ref[d, 1, half]) + gbias_ref[2 * d + 1:2 * d + 2, c0:c0 + 256])
            log_a = (-RG_C) * r * _softplus(-lam_ref[d:d + 1, c0:c0 + 256])
            a = jnp.exp(log_a)
            mult = jnp.sqrt(-jnp.tanh(log_a) * (a * a + 1.0))
            xin = mult * (i * x[:, c0:c0 + 256])
            a_scr[:, :, c0:c0 + 256] = a.reshape(tt, bsz, 256)
            x_scr[:, :, c0:c0 + 256] = xin.reshape(tt, bsz, 256)

    gates(uf_ref, 0, af_scr, xf_scr)
    gates(ub_ref, 1, ab_scr, xb_scr)

    def step(t, carry):
        hf, hb = carry
        hf = af_scr[t] * hf + xf_scr[t]
        hf_ref[t] = hf
        tb = tt - 1 - t
        hb = ab_scr[tb] * hb + xb_scr[tb]
        hb_ref[tb] = hb
        return hf, hb

    hf, hb = lax.fori_loop(0, tt, step, (h_scr[0], h_scr[1]), unroll=8)
    h_scr[0] = hf
    h_scr[1] = hb


def _rglru(ua3, wg, gbias, lam, *, tt, ctx_len):
    t_all, bsz, w = ua3.shape
    n_steps = t_all // tt
    nc = ctx_len // tt

    def bwd(s):
        return jnp.where(s < nc, nc - 1 - s, n_steps + nc - 1 - s)

    blk = (tt, bsz, w)
    kern = functools.partial(_rglru_kernel, tt=tt, bsz=bsz)
    return pl.pallas_call(
        kern,
        grid=(n_steps,),
        in_specs=[pl.BlockSpec(blk, lambda s: (s, 0, 0)),
                  pl.BlockSpec(blk, lambda s: (bwd(s), 0, 0)),
                  _const_spec(wg.shape), _const_spec(gbias.shape), _const_spec(lam.shape)],
        out_specs=[pl.BlockSpec(blk, lambda s: (s, 0, 0)),
                   pl.BlockSpec(blk, lambda s: (bwd(s), 0, 0))],
        out_shape=[jax.ShapeDtypeStruct(ua3.shape, F32)] * 2,
        scratch_shapes=[pltpu.VMEM(blk, F32)] * 4 + [pltpu.VMEM((2, bsz, w), F32)],
        compiler_params=_params(("arbitrary",), 40 << 20),
        name="l0_rglru",
    )(ua3, ua3, wg, gbias, lam)


def _delta_kernel(qf_ref, kf_ref, vf_ref, gf_ref, qb_ref, kb_ref, vb_ref, gb_ref, of_ref, ob_ref, s_scr, *, n_sub):
    c = DN_CHUNK

    @pl.when(pl.program_id(1) == 0)
    def _():
        s_scr[...] = jnp.zeros_like(s_scr)

    dir_refs = ((qf_ref, kf_ref, vf_ref, gf_ref, of_ref), (qb_ref, kb_ref, vb_ref, gb_ref, ob_ref))
    masks = [_scan_masks(c, d) for d in range(2)]
    eye = jnp.where(masks[0][3], 1.0, 0.0)

    cums = {}
    for d in range(2):
        incl, _, incl_t, _ = masks[d]
        m_incl = jnp.where(incl, 1.0, 0.0)
        m_incl_t = jnp.where(incl_t, 1.0, 0.0)
        for ci in range(n_sub):
            g_all = dir_refs[d][3][ci * c:(ci + 1) * c, :]
            gc_all = _mask_dot(m_incl, g_all)
            gct_all = _mask_dot_tn(g_all, m_incl_t)
            cums[d, ci] = (g_all, gc_all, gct_all)

    chains = []
    for d in range(2):
        q_ref, k_ref, v_ref, _, _ = dir_refs[d]
        incl, strict, _, _ = masks[d]
        last = c - 1 if d == 0 else 0
        for ci in range(n_sub):
            g_all, gc_all, gct_all = cums[d, ci]
            rs = slice(ci * c, (ci + 1) * c)
            for h in range(DN_HEADS):
                hs = slice(h * DN_D, (h + 1) * DN_D)
                lane = 2 * DN_HEADS + d * DN_HEADS + h
                ch = dict(d=d, ci=ci, h=h, rs=rs, hs=hs, incl=incl, strict=strict)
                ch["beta"] = g_all[:, d * DN_HEADS + h:d * DN_HEADS + h + 1]
                gc = jnp.broadcast_to(gc_all[:, lane:lane + 1], (c, DN_D))
                gc_row = jnp.broadcast_to(gct_all[lane:lane + 1, :], (c, c))
                ch["gc"] = gc
                ch["gtot"] = gc[last:last + 1, :]
                ch["decay"] = jnp.where(incl, jnp.exp(jnp.minimum(gc[:, 0:c] - gc_row, 0.0)), 0.0)
                ch["e_gc"] = jnp.exp(gc)
                ch["q"] = q_ref[rs, hs].astype(F32)
                ch["k"] = k_ref[rs, hs].astype(F32)
                ch["v"] = v_ref[rs, hs].astype(F32)
                chains.append(ch)

    for ch in chains:
        ch["kb"] = ch["k"] * ch["beta"]
        qk = _dot_nt(jnp.concatenate([ch["kb"], ch["q"]], axis=0).astype(BF16), ch["k"].astype(BF16))
        ch["neg"] = -jnp.where(ch["strict"], qk[0:c] * ch["decay"], 0.0)
        ch["a_qk"] = (qk[c:2 * c] * ch["decay"]).astype(BF16)
    for ch in chains:
        negb = ch["neg"].astype(BF16)
        ch["t"] = eye + ch["neg"]
        ch["p"] = _dot(negb, negb)
    n_sq = max(1, (c - 1).bit_length() - 1)
    for it in range(n_sq):
        for ch in chains:
            tp = _dot(jnp.concatenate([ch["t"], ch["p"]], axis=0).astype(BF16), ch["p"].astype(BF16))
            ch["t"] = ch["t"] + tp[0:c]
            ch["p"] = tp[c:2 * c]
    for ch in chains:
        rhs = jnp.concatenate([ch["v"] * ch["beta"], ch["kb"] * ch["e_gc"]], axis=1).astype(BF16)
        sol = _dot(ch["t"].astype(BF16), rhs)
        ch["u"] = sol[:, 0:DN_D]
        ch["wq"] = jnp.concatenate([sol[:, DN_D:2 * DN_D], ch["q"] * ch["e_gc"]], axis=0).astype(BF16)
        ch["k_tail"] = (ch["k"] * jnp.exp(ch["gtot"] - ch["gc"])).astype(BF16)

    by_key = {(ch["d"], ch["ci"], ch["h"]): ch for ch in chains}
    for step in range(n_sub):
        live = [by_key[d, step if d == 0 else n_sub - 1 - step, h] for d in range(2) for h in range(DN_HEADS)]
        for ch in live:
            ch["st"] = s_scr[ch["d"], ch["h"]]
            ch["ws"] = _dot(ch["wq"], ch["st"].astype(BF16))
        for ch in live:
            vnb = (ch["u"] - ch["ws"][0:c]).astype(BF16)
            o = ch["ws"][c:2 * c] + _dot(ch["a_qk"], vnb)
            dir_refs[ch["d"]][4][ch["rs"], ch["hs"]] = o.astype(BF16)
            s_scr[ch["d"], ch["h"]] = ch["st"] * jnp.exp(ch["gtot"]) + _dot_tn(ch["k_tail"], vnb)


def _delta(q, k, v, gb, *, ctx_len, rows):
    bsz, t_all, w = q.shape
    n_steps = t_all // rows
    nc = ctx_len // rows

    def bwd(s):
        return jnp.where(s < nc, nc - 1 - s, n_steps + nc - 1 - s)

    fwd_spec = lambda width: pl.BlockSpec((None, rows, width), lambda b, s: (b, s, 0))
    bwd_spec = lambda width: pl.BlockSpec((None, rows, width), lambda b, s: (b, bwd(s), 0))
    return pl.pallas_call(
        functools.partial(_delta_kernel, n_sub=rows // DN_CHUNK),
        grid=(bsz, n_steps),
        in_specs=[fwd_spec(w), fwd_spec(w), fwd_spec(w), fwd_spec(128),
                  bwd_spec(w), bwd_spec(w), bwd_spec(w), bwd_spec(128)],
        out_specs=[fwd_spec(w), bwd_spec(w)],
        out_shape=[jax.ShapeDtypeStruct((bsz, t_all, w), BF16)] * 2,
        scratch_shapes=[pltpu.VMEM((2, DN_HEADS, DN_D, DN_D), F32)],
        compiler_params=_params(("arbitrary", "arbitrary"), 32 << 20),
        name="l0_deltanet",
    )(q, k, v, gb, q, k, v, gb)


def _head_norm(y, g):
    return y * lax.rsqrt(jnp.mean(y * y, axis=-1, keepdims=True) + EPS) * g


def _l0_tail_kernel(ha_ref, gay_ref, o0_ref, o1_ref, sz_ref, ctx_ref, x_ref, mod_ref, ng_ref, wo_ref, g_ref, wg_ref,
                    wu_ref, wd_ref, hc_ref, hl_ref, *, n_chunks, ctx_tiles):
    t = pl.program_id(1)
    tm = ha_ref.shape[0]
    halves = [dict(rs=slice(i * tm // 2, (i + 1) * tm // 2)) for i in range(2)]
    for hv in halves:
        rs = hv["rs"]
        parts = [(ha_ref[rs, :] * gay_ref[rs, :].astype(F32)).astype(BF16)]
        for hd in range(DN_HEADS):
            lo = hd * DN_D
            ob = o0_ref[rs, lo:lo + DN_D].astype(F32) + o1_ref[rs, lo:lo + DN_D].astype(F32)
            parts.append((_head_norm(ob, ng_ref[...]) * sz_ref[rs, lo:lo + DN_D].astype(F32)).astype(BF16))
        hv["ycat"] = jnp.concatenate(parts, axis=-1)
    for hv in halves:
        h = jnp.where(t < ctx_tiles, ctx_ref[hv["rs"], :], x_ref[hv["rs"], :])
        hv["x"] = h + mod_ref[2:3, :] * _dot(hv["ycat"], wo_ref[...])
    for hv in halves:
        hv["xm"] = _normmod(hv["x"], g_ref[...], mod_ref[3:4, :], mod_ref[4:5, :]).astype(BF16)
        hv["acc"] = jnp.zeros(hv["x"].shape, F32)
    cw = D_FF // n_chunks
    for ci in range(n_chunks):
        c0 = ci * cw
        for hv in halves:
            hv["act"] = (_silu(_dot(hv["xm"], wg_ref[:, c0:c0 + cw])) * _dot(hv["xm"], wu_ref[:, c0:c0 + cw])).astype(BF16)
        for hv in halves:
            hv["acc"] = hv["acc"] + _dot(hv["act"], wd_ref[c0:c0 + cw, :])
    out = jnp.concatenate([hv["x"] + mod_ref[5:6, :] * hv["acc"] for hv in halves], axis=0)

    @pl.when(t < ctx_tiles)
    def _():
        hc_ref[...] = out

    @pl.when(t >= ctx_tiles)
    def _():
        hl_ref[...] = out


def _l0_tail(ha, gay, o0, o1, sz, ctx, x, modtab, ng, wo, g, wg, wu, wd, *, tm):
    bsz, ctx_len, d = ctx.shape
    seq = x.shape[1]
    t_all = ctx_len + seq
    ctx_tiles = ctx_len // tm
    tok = lambda width: pl.BlockSpec((None, tm, width), lambda b, t: (b, t, 0))
    return pl.pallas_call(
        functools.partial(_l0_tail_kernel, n_chunks=2, ctx_tiles=ctx_tiles),
        grid=(bsz, t_all // tm),
        in_specs=[tok(512), tok(512), tok(512), tok(512), tok(512)] + _row_specs(tm, d, ctx_tiles, ctx_len, seq, False) + [
                  pl.BlockSpec((None, None, 6, d), lambda b, t: (b, jnp.where(t >= ctx_tiles, 1, 0), 0, 0)),
                  _const_spec((1, DN_D)), _const_spec((d, d)),
                  _const_spec((1, d)), _const_spec((d, D_FF)), _const_spec((d, D_FF)), _const_spec((D_FF, d))],
        out_specs=[pl.BlockSpec((None, tm, d), lambda b, t: (b, jnp.minimum(t, ctx_tiles - 1), 0)),
                   pl.BlockSpec((None, tm, d), lambda b, t: (b, jnp.maximum(t - ctx_tiles, 0), 0))],
        out_shape=[jax.ShapeDtypeStruct((bsz, ctx_len, d), F32), jax.ShapeDtypeStruct((bsz, t_all - ctx_len, d), F32)],
        compiler_params=_params(("arbitrary", "arbitrary"), 48 << 20),
        name="l0_tail",
    )(ha, gay, o0, o1, sz, ctx, x, modtab, ng, wo, g, wg, wu, wd)


def _inproj1_kernel(h_ref, mod_ref, g_ref, w_ref, lbl_ref, wlr_ref, b2_ref, pa_ref, pb_ref, u_scr, *, layer):
    x = h_ref[...]
    xm = _normmod(x, g_ref[...], mod_ref[0:1, :], mod_ref[1:2, :]).astype(BF16)

    lg = lbl_ref[...]
    ex = jnp.exp(lg - jnp.max(lg, axis=0, keepdims=True))
    lbw = ex / jnp.sum(ex, axis=0, keepdims=True)
    lb = jnp.sum(lbw[1:layer + 1], axis=0, keepdims=True)

    def project(c0, width):
        u_scr[:, c0:c0 + width] = _dot(xm, w_ref[:, c0:c0 + width])

    def put(seg, off, val):
        ref = (pa_ref, pb_ref)[seg[0]]
        ref[:, seg[1] * SEG + off:seg[1] * SEG + off + val.shape[1]] = val.astype(ref.dtype)

    def groups(fn):
        for grp in range(SEG // 128):
            fn(grp * 128)

    project(0, SEG)
    groups(lambda c0: put(S_HQ, c0, _silu(u_scr[:, c0:c0 + 128]) * (HG_D ** -0.5)))
    for dr, (sk, sf) in enumerate(((S_HK0, S_HLF0), (S_HK1, S_HLF1))):
        project(512 + dr * 512, SEG)

        def forget(c0, dr=dr, sk=sk, sf=sf):
            lbg = lb[:, c0:c0 + 128]
            fl = u_scr[:, 512 + dr * 512 + c0:512 + dr * 512 + c0 + 128]
            put(sf, c0, jnp.log(lbg + (1.0 - lbg) * _sigmoid(fl)))
            put(sk, c0, (1.0 - lbg) * _sigmoid(-fl))
        groups(forget)
    project(1536, SEG)
    groups(lambda c0: put(S_HV, c0, u_scr[:, 1536 + c0:1536 + c0 + 128]))
    project(2048, SEG)
    groups(lambda c0: put(S_CG, c0, _silu(u_scr[:, 2048 + c0:2048 + c0 + 128])))
    project(2560, SEG)
    put(S_GQK, 0, u_scr[:, 2560:2816] * (GLA_DK ** -0.5))
    put(S_GQK, 256, u_scr[:, 2816:3072])
    project(3072, SEG)
    groups(lambda c0: put(S_GV, c0, u_scr[:, 3072 + c0:3072 + c0 + 128]))
    project(3584, SEG)
    groups(lambda c0: put(S_DG, c0, _silu(u_scr[:, 3584 + c0:3584 + c0 + 128])))
    project(4096, 128)
    lr = u_scr[:, 4096:4224]
    put(S_GLD, 0, -_softplus(-(_dot_hilo(lr, wlr_ref[...]) + b2_ref[...])) * (1.0 / GLA_GATE_NORM))


def _inproj1(h, row0, n_rows, modtab, seg, g, w_pad, lbl, wlr, b2, *, tm, layer):
    bsz, _, d = h.shape
    t0 = row0 // tm
    return pl.pallas_call(
        functools.partial(_inproj1_kernel, layer=layer),
        grid=(bsz, n_rows // tm),
        in_specs=[pl.BlockSpec((None, tm, d), lambda b, t: (b, t0 + t, 0)),
                  pl.BlockSpec((None, None, 6, d), lambda b, t: (b, seg, 0, 0)),
                  _const_spec((1, d)), _const_spec((d, O_IN_PAD)), _const_spec(lbl.shape),
                  _const_spec((128, SEG)), _const_spec((1, SEG))],
        out_specs=[pl.BlockSpec((None, tm, n * SEG), lambda b, t: (b, t, 0)) for n in N_SEG],
        out_shape=[jax.ShapeDtypeStruct((bsz, n_rows, n * SEG), dt) for n, dt in zip(N_SEG, P1_DTYPES)],
        scratch_shapes=[pltpu.VMEM((tm, O_IN_PAD), F32)],
        compiler_params=_params(("arbitrary", "arbitrary"), 44 << 20),
        name="l1_inproj",
    )(h, modtab, g, w_pad, lbl, wlr, b2)


def _gla_stream(d, q_all, k_all, ld_all, v_all, st_ref, o_ref, o_lane0, r0, n_heads, dk, dv, incl, m_incl):
    c = k_all.shape[0]
    mid = c // 2 - 1 if d == 0 else c // 2
    last = c - 1 if d == 0 else 0
    bc = _mask_dot(m_incl, ld_all)
    m = bc[mid:mid + 1]
    btot = bc[last:last + 1]
    kn = k_all.astype(F32) * jnp.exp(m - bc)
    it = dict(d=d, r0=r0, st_ref=st_ref, o_ref=o_ref, o_lane0=o_lane0, n_heads=n_heads, dk=dk, dv=dv, incl=incl,
              c=c, kt=(kn * jnp.exp(btot - m)).astype(BF16), dec=jnp.exp(btot), v=v_all.astype(BF16),
              want_out=q_all is not None)
    if q_all is not None:
        qe = q_all.astype(F32) * jnp.exp(bc)
        it.update(qd=(qe * jnp.exp(-m)).astype(BF16), qe=qe.astype(BF16), knb=kn.astype(BF16))
    return it


def _gla_intra(it):
    dk = it["dk"]
    it["a"] = [jnp.where(it["incl"], _dot_nt(it["qd"][:, hd * dk:(hd + 1) * dk], it["knb"][:, hd * dk:(hd + 1) * dk]),
                         0.0).astype(BF16) for hd in range(it["n_heads"])]


def _gla_advance(it):
    d, dk, dv, c, st_ref = it["d"], it["dk"], it["dv"], it["c"], it["st_ref"]
    sts = [st_ref[d, hd] for hd in range(it["n_heads"])]
    if it["want_out"]:
        for hd in range(it["n_heads"]):
            v = it["v"][:, hd * dv:(hd + 1) * dv]
            o = _dot(it["a"][hd], v) + _dot_nt(it["qe"][:, hd * dk:(hd + 1) * dk], sts[hd].astype(BF16))
            it["o_ref"][it["r0"]:it["r0"] + c, it["o_lane0"] + hd * dv:it["o_lane0"] + (hd + 1) * dv] = o.astype(BF16)
    for hd in range(it["n_heads"]):
        ks = slice(hd * dk, (hd + 1) * dk)
        st_ref[d, hd] = sts[hd] * it["dec"][:, ks] + _dot_tn(it["v"][:, hd * dv:(hd + 1) * dv], it["kt"][:, ks])


def _mix1_body(dirs, n_sub, sh_ref, sg_ref):
    c = MIX1_CHUNK
    gw = GLA_HEADS * GLA_DK
    prepared = {}
    for d, (hq_ref, hv_ref, hk_ref, hlf_ref, gv_ref, gqk_ref, gld_ref, o_ref) in enumerate(dirs):
        incl = _scan_masks(c, d)[0]
        m_incl = jnp.where(incl, 1.0, 0.0)
        for ci in range(n_sub):
            r0 = ci * c
            rs = slice(r0, r0 + c)
            prepared[d, ci, 0] = _gla_stream(
                d, None if hq_ref is None else hq_ref[rs, :], hk_ref[rs, :], hlf_ref[rs, :], hv_ref[rs, :],
                sh_ref, o_ref, 0, r0, HG_HEADS, HG_D, HG_D, incl, m_incl)
            prepared[d, ci, 1] = _gla_stream(
                d, None if hq_ref is None else gqk_ref[rs, 0:gw], gqk_ref[rs, gw:2 * gw],
                gld_ref[rs, d * gw:(d + 1) * gw], gv_ref[rs, :],
                sg_ref, o_ref, HG_HEADS * HG_D, r0, GLA_HEADS, GLA_DK, GLA_DV, incl, m_incl)
    for it in prepared.values():
        if it["want_out"]:
            _gla_intra(it)
    for step in range(n_sub):
        for d in range(2):
            for stream in range(2):
                _gla_advance(prepared[d, step if d == 0 else n_sub - 1 - step, stream])


def _mix1_ctx_kernel(*refs, n_sub):
    fwd, bwd, (sh_ref, sg_ref) = refs[0:6], refs[6:12], refs[12:14]

    @pl.when(pl.program_id(1) == 0)
    def _():
        sh_ref[...] = jnp.zeros_like(sh_ref)
        sg_ref[...] = jnp.zeros_like(sg_ref)

    _mix1_body([(None,) + tuple(r) + (None,) for r in (fwd, bwd)], n_sub, sh_ref, sg_ref)


def _mix1_lat_kernel(*refs, n_sub):
    fwd, bwd = refs[0:7], refs[7:14]
    sh0_ref, sg0_ref, of_ref, ob_ref, sh_scr, sg_scr = refs[14:20]

    @pl.when(pl.program_id(1) == 0)
    def _():
        sh_scr[...] = sh0_ref[...]
        sg_scr[...] = sg0_ref[...]

    _mix1_body([tuple(fwd) + (of_ref,), tuple(bwd) + (ob_ref,)], n_sub, sh_scr, sg_scr)


def _mix1_specs(p1, rows, n_steps, segs_of_dir):
    specs, args = [], []
    for d in range(2):
        blk = (lambda b, s: s) if d == 0 else (lambda b, s: n_steps - 1 - s)
        for arr, sg in segs_of_dir(d):
            specs.append(pl.BlockSpec((None, rows, SEG), lambda b, s, blk=blk, sg=sg: (b, blk(b, s), sg)))
            args.append(p1[arr])
    return specs, args


_SH_SHAPE = (2, HG_HEADS, HG_D, HG_D)
_SG_SHAPE = (2, GLA_HEADS, GLA_DV, GLA_DK)


def _mix1_ctx(p1c, *, rows):
    bsz, ctx_len, _ = p1c[0].shape
    n_steps = ctx_len // rows
    segs = lambda d: (S_HV, (S_HK0, S_HK1)[d], (S_HLF0, S_HLF1)[d], S_GV, S_GQK, S_GLD)
    specs, args = _mix1_specs(p1c, rows, n_steps, segs)
    state = lambda shape: pl.BlockSpec((None,) + shape, lambda b, s: (b, 0, 0, 0, 0))
    return pl.pallas_call(
        functools.partial(_mix1_ctx_kernel, n_sub=rows // MIX1_CHUNK),
        grid=(bsz, n_steps),
        in_specs=specs,
        out_specs=[state(_SH_SHAPE), state(_SG_SHAPE)],
        out_shape=[jax.ShapeDtypeStruct((bsz,) + _SH_SHAPE, F32), jax.ShapeDtypeStruct((bsz,) + _SG_SHAPE, F32)],
        compiler_params=_params(("arbitrary", "arbitrary"), 32 << 20),
        name="l1_ctx_state",
    )(*args)


def _mix1_lat(p1l, sh0, sg0, *, rows):
    bsz, seq, _ = p1l[0].shape
    n_steps = seq // rows
    segs = lambda d: (S_HQ, S_HV, (S_HK0, S_HK1)[d], (S_HLF0, S_HLF1)[d], S_GV, S_GQK, S_GLD)
    specs, args = _mix1_specs(p1l, rows, n_steps, segs)
    state = lambda shape: pl.BlockSpec((None,) + shape, lambda b, s: (b, 0, 0, 0, 0))
    ow = HG_HEADS * HG_D + GLA_HEADS * GLA_DV
    return pl.pallas_call(
        functools.partial(_mix1_lat_kernel, n_sub=rows // MIX1_CHUNK),
        grid=(bsz, n_steps),
        in_specs=specs + [state(_SH_SHAPE), state(_SG_SHAPE)],
        out_specs=[pl.BlockSpec((None, rows, ow), lambda b, s: (b, s, 0)),
                   pl.BlockSpec((None, rows, ow), lambda b, s: (b, n_steps - 1 - s, 0))],
        out_shape=[jax.ShapeDtypeStruct((bsz, seq, ow), BF16)] * 2,
        scratch_shapes=[pltpu.VMEM(_SH_SHAPE, F32), pltpu.VMEM(_SG_SHAPE, F32)],
        compiler_params=_params(("arbitrary", "arbitrary"), 32 << 20),
        name="l1_scan",
    )(*args, sh0, sg0)


def _outproj1(o0_ref, o1_ref, gate_ref, h_ref, mod_ref, cng_ref, dng_ref, w_ref):
    parts = []
    for hd in range(HG_HEADS + GLA_HEADS):
        lo = hd * 128
        y = o0_ref[:, lo:lo + 128].astype(F32) + o1_ref[:, lo:lo + 128].astype(F32)
        ng = cng_ref[...] if hd < HG_HEADS else dng_ref[...]
        parts.append((_head_norm(y, ng) * gate_ref[:, lo:lo + 128].astype(F32)).astype(BF16))
    return h_ref[...] + mod_ref[2:3, :] * _dot(jnp.concatenate(parts, axis=-1), w_ref[...])


def _for_pieces(length, fn):
    done = 0
    for rows in MOE_SEG_PIECES:
        n = (length - done) // rows

        def body(p, carry, rows=rows, done=done):
            fn(done + p * rows, rows)
            return carry
        lax.fori_loop(0, n, body, 0)
        done = done + n * rows


def _moe_route_kernel(o0_ref, o1_ref, gate_ref, h_ref, mod_ref, cng_ref, dng_ref, wo_ref, ng_ref, rw_ref, rb_ref,
                      h2_ref, slot_ref, seg_ref, xs_hbm, xn_scr, xg_scr, zero_scr, base_smem, sem, *, tk):
    i = pl.program_id(0)
    pc = MOE_PIECE
    gr = MOE_GATHER_ROWS

    @pl.when(i == 0)
    def _():
        for e in range(N_EXPERTS):
            base_smem[e] = 0
        zero_scr[...] = jnp.zeros_like(zero_scr)

    h2 = _outproj1(o0_ref, o1_ref, gate_ref, h_ref, mod_ref, cng_ref, dng_ref, wo_ref)
    h2_ref[...] = h2
    xm = _normmod(h2, ng_ref[...], mod_ref[3:4, :], mod_ref[4:5, :])
    xn_scr[...] = xm.astype(BF16)
    lg = jnp.transpose(_dot_hilo(xm, rw_ref[...]))[0:N_EXPERTS, :] + rb_ref[...]
    eidx = lax.broadcasted_iota(jnp.int32, lg.shape, 0).astype(F32)
    m1 = jnp.max(lg, axis=0, keepdims=True)
    i1 = jnp.min(jnp.where(lg == m1, eidx, float(N_EXPERTS)), axis=0, keepdims=True)
    lg2 = jnp.where(eidx == i1, -jnp.inf, lg)
    m2 = jnp.max(lg2, axis=0, keepdims=True)
    i2 = jnp.min(jnp.where(lg2 == m2, eidx, float(N_EXPERTS)), axis=0, keepdims=True)
    ex = jnp.exp(m2 - m1)
    p1 = 1.0 / (1.0 + ex)
    sel = jnp.where(eidx == i1, 1.0, 0.0) + jnp.where(eidx == i2, 1.0, 0.0)
    lane = lax.broadcasted_iota(jnp.int32, lg.shape, 1)
    cum = sel
    sh = 1
    while sh < tk:
        cum = cum + jnp.where(lane >= sh, pltpu.roll(cum, sh, 1), 0.0)
        sh *= 2
    padded = jnp.floor((cum[:, tk - 1:tk] + (pc - 1.0)) * (1.0 / pc)) * pc
    padded = jnp.broadcast_to(padded, (N_EXPERTS, 128))
    er = lax.broadcasted_iota(jnp.int32, (N_EXPERTS, N_EXPERTS), 0)
    ec = lax.broadcasted_iota(jnp.int32, (N_EXPERTS, N_EXPERTS), 1)
    off = _dot_hi(jnp.where(er > ec, 1.0, 0.0), padded)
    slot = off[:, 0:1] + cum - 1.0
    slot_a = jnp.sum(jnp.where(eidx == i1, slot, 0.0), axis=0, keepdims=True)
    slot_b = jnp.sum(jnp.where(eidx == i2, slot, 0.0), axis=0, keepdims=True)
    slot_ref[...] = jnp.concatenate([slot_a, slot_b, p1, ex * p1, jnp.zeros((4, tk), F32)], axis=0)

    total = jnp.max(off[N_EXPERTS - 1:N_EXPERTS, :] + padded[N_EXPERTS - 1:N_EXPERTS, :]).astype(jnp.int32)

    def gather(ci, carry):
        r0 = pl.multiple_of(ci * gr, gr)
        rid = (lax.broadcasted_iota(jnp.int32, (gr, tk), 0) + r0).astype(F32)
        p = jnp.where(rid == slot_a, 1.0, 0.0) + jnp.where(rid == slot_b, 1.0, 0.0)
        xg_scr[pl.ds(r0, gr), :] = _dot(p.astype(BF16), xn_scr[...]).astype(BF16)
        return carry
    lax.fori_loop(0, (total + gr - 1) // gr, gather, 0)

    erow = lax.broadcasted_iota(jnp.int32, (N_EXPERTS, 128), 0)
    base_vec = jnp.zeros((N_EXPERTS, 128), F32)
    segs = []
    for e in range(N_EXPERTS):
        off_e = jnp.max(off[e:e + 1, :]).astype(jnp.int32)
        len_e = jnp.max(padded[e:e + 1, :]).astype(jnp.int32)
        base_e = base_smem[e]
        base_vec = jnp.where(erow == e, base_e.astype(F32), base_vec)
        segs.append((e, off_e, len_e, base_e))
    seg_ref[0] = off
    seg_ref[1] = padded
    seg_ref[2] = base_vec

    def seg_copy(e, off_e, base_e, r, rows):
        return pltpu.make_async_copy(
            xg_scr.at[pl.ds(pl.multiple_of(off_e + r, pc), rows), :],
            xs_hbm.at[e, pl.ds(pl.multiple_of(base_e + r, pc), rows), :], sem)

    for e, off_e, len_e, base_e in segs:
        _for_pieces(len_e, lambda r, rows, e=e, off_e=off_e, base_e=base_e: seg_copy(e, off_e, base_e, r, rows).start())
    for e, off_e, len_e, base_e in segs:
        _for_pieces(len_e, lambda r, rows, e=e, off_e=off_e, base_e=base_e: seg_copy(e, off_e, base_e, r, rows).wait())
        base_smem[e] = base_e + len_e

    @pl.when(i == pl.num_programs(0) - 1)
    def _():
        def tail_copy(e, p):
            end = base_smem[e]
            return pltpu.make_async_copy(zero_scr, xs_hbm.at[e, pl.ds(pl.multiple_of(end + p * pc, pc), pc), :], sem)

        def n_tail(e):
            rem = lax.rem(base_smem[e], MOE_BLOCK)
            return jnp.where(rem == 0, 0, MOE_BLOCK - rem) // pc

        for e in range(N_EXPERTS):
            def start(p, carry, e=e):
                tail_copy(e, p).start()
                return carry
            lax.fori_loop(0, n_tail(e), start, 0)
        for e in range(N_EXPERTS):
            def wait(p, carry, e=e):
                tail_copy(e, p).wait()
                return carry
            lax.fori_loop(0, n_tail(e), wait, 0)


def _moe_ffn_kernel(eid_ref, blk_ref, nv_ref, x_ref, wg_ref, wu_ref, wd_ref, o_ref, *, n_chunks):
    del eid_ref, blk_ref

    @pl.when(pl.program_id(0) < nv_ref[0])
    def _():
        x = x_ref[...]
        cw = D_FF // n_chunks
        acc = jnp.zeros(x.shape, F32)
        for ci in range(n_chunks):
            c0 = ci * cw
            act = (_silu(_dot(x, wg_ref[:, c0:c0 + cw])) * _dot(x, wu_ref[:, c0:c0 + cw])).astype(BF16)
            acc = acc + _dot(act, wd_ref[c0:c0 + cw, :])
        o_ref[...] = acc.astype(BF16)


def _moe_combine_kernel(base_ref, len_ref, off_ref, h_ref, mod_ref, fg_ref, slot_ref, og_hbm, out_ref,
                        og_scr, sem, *, tk):
    i = pl.program_id(0)
    pc = MOE_PIECE
    gr = MOE_GATHER_ROWS
    n_rows = og_scr.shape[0]

    def seg_copy(e, r, rows):
        return pltpu.make_async_copy(
            og_hbm.at[e, pl.ds(pl.multiple_of(base_ref[i * N_EXPERTS + e] + r, pc), rows), :],
            og_scr.at[pl.ds(pl.multiple_of(off_ref[i * N_EXPERTS + e] + r, pc), rows), :], sem)

    for e in range(N_EXPERTS):
        _for_pieces(len_ref[i * N_EXPERTS + e], lambda r, rows, e=e: seg_copy(e, r, rows).start())

    last = i * N_EXPERTS + N_EXPERTS - 1
    total = off_ref[last] + len_ref[last]

    def clear(p, carry):
        og_scr[pl.ds(pl.multiple_of(p * pc, pc), pc), :] = jnp.zeros((pc, og_scr.shape[1]), BF16)
        return carry
    lax.fori_loop(total // pc, n_rows // pc, clear, 0)

    for e in range(N_EXPERTS):
        _for_pieces(len_ref[i * N_EXPERTS + e], lambda r, rows, e=e: seg_copy(e, r, rows).wait())

    out_ref[...] = jnp.zeros_like(out_ref)

    def scatter(ci, carry):
        r0 = pl.multiple_of(ci * gr, gr)
        rid = (lax.broadcasted_iota(jnp.int32, (gr, tk), 0) + r0).astype(F32)
        pg = (jnp.where(rid == slot_ref[0:1, :], slot_ref[2:3, :], 0.0)
              + jnp.where(rid == slot_ref[1:2, :], slot_ref[3:4, :], 0.0))
        out_ref[...] = out_ref[...] + _dot_tn(pg.astype(BF16), og_scr[pl.ds(r0, gr), :])
        return carry
    lax.fori_loop(0, (total + gr - 1) // gr, scatter, 0)
    h3 = h_ref[...] + mod_ref[5:6, :] * out_ref[...]
    out_ref[...] = h3 * lax.rsqrt(jnp.mean(h3 * h3, axis=-1, keepdims=True) + EPS) * fg_ref[...]


def _moe_block_table(seg, n_blocks):
    ends = (seg[-1, 2, :, 0] + seg[-1, 1, :, 0]).astype(jnp.int32)
    nblk = (ends + MOE_BLOCK - 1) // MOE_BLOCK
    cum = jnp.cumsum(nblk)
    n_valid = cum[-1]
    g = jnp.minimum(jnp.arange(n_blocks, dtype=jnp.int32), n_valid - 1)
    eid = jnp.sum((g[:, None] >= cum[None, :]).astype(jnp.int32), axis=1)
    blk = g - (cum - nblk)[eid]
    return eid, blk, n_valid.reshape(1)


def _moe(o0, o1, gates, gate_blk, h, modtab, cng, dng, wo, ng, fg, rw, rb, wg, wu, wd, *, tk):
    bsz, seq, d = h.shape
    tpb = seq // tk
    n_tiles = bsz * tpb
    n_tok = bsz * seq
    tile_rows = -(-(2 * tk + N_EXPERTS * MOE_PIECE) // MOE_GATHER_ROWS) * MOE_GATHER_ROWS
    cap = -(-(n_tok + n_tiles * MOE_PIECE) // MOE_BLOCK) * MOE_BLOCK
    n_blocks = -(-(2 * n_tok + n_tiles * N_EXPERTS * MOE_PIECE) // MOE_BLOCK) + N_EXPERTS
    tok = lambda i, *_: (i // tpb, i % tpb, 0)
    mod = lambda i, *_: (i // tpb, 1, 0, 0)

    tok_spec = pl.BlockSpec((None, tk, d), tok)
    h2, slots, seg, xs = pl.pallas_call(
        functools.partial(_moe_route_kernel, tk=tk),
        grid=(n_tiles,),
        in_specs=[tok_spec, tok_spec,
                  pl.BlockSpec((None, tk, d), lambda i: (i // tpb, i % tpb, gate_blk)),
                  tok_spec, pl.BlockSpec((None, None, 6, d), mod),
                  _const_spec((1, 128)), _const_spec((1, 128)), _const_spec((d, d)),
                  _const_spec((1, d)), _const_spec((d, 128)), _const_spec((N_EXPERTS, 1))],
        out_specs=[tok_spec,
                   pl.BlockSpec((None, 8, tk), lambda i: (i, 0, 0)),
                   pl.BlockSpec((None, 3, N_EXPERTS, 128), lambda i: (i, 0, 0, 0)),
                   pl.BlockSpec(memory_space=pl.ANY)],
        out_shape=[jax.ShapeDtypeStruct(h.shape, F32),
                   jax.ShapeDtypeStruct((n_tiles, 8, tk), F32),
                   jax.ShapeDtypeStruct((n_tiles, 3, N_EXPERTS, 128), F32),
                   jax.ShapeDtypeStruct((N_EXPERTS, cap, d), BF16)],
        scratch_shapes=[pltpu.VMEM((tk, d), BF16), pltpu.VMEM((tile_rows, d), BF16), pltpu.VMEM((MOE_PIECE, d), BF16),
                        pltpu.SMEM((N_EXPERTS,), jnp.int32), pltpu.SemaphoreType.DMA(())],
        compiler_params=_params(("arbitrary",), 40 << 20),
        name="l1_moe_route",
    )(o0, o1, gates, h, modtab, cng, dng, wo, ng, _pad_cols(rw, 128), rb)

    eid, blk, n_valid = _moe_block_table(seg, n_blocks)
    x_spec = pl.BlockSpec((None, MOE_BLOCK, d), lambda g, eid, blk, nv: (eid[g], blk[g], 0))
    og = pl.pallas_call(
        functools.partial(_moe_ffn_kernel, n_chunks=11),
        grid_spec=pltpu.PrefetchScalarGridSpec(
            num_scalar_prefetch=3, grid=(n_blocks,),
            in_specs=[x_spec,
                      pl.BlockSpec((None, d, D_FF), lambda g, eid, blk, nv: (eid[g], 0, 0)),
                      pl.BlockSpec((None, d, D_FF), lambda g, eid, blk, nv: (eid[g], 0, 0)),
                      pl.BlockSpec((None, D_FF, d), lambda g, eid, blk, nv: (eid[g], 0, 0))],
            out_specs=x_spec),
        out_shape=jax.ShapeDtypeStruct((N_EXPERTS, cap, d), BF16),
        compiler_params=_params(("arbitrary",), 52 << 20),
        name="l1_moe_experts",
    )(eid, blk, n_valid, xs, wg, wu, wd)

    tab = lambda k: seg[:, k, :, 0].astype(jnp.int32).reshape(-1)
    return pl.pallas_call(
        functools.partial(_moe_combine_kernel, tk=tk),
        grid_spec=pltpu.PrefetchScalarGridSpec(
            num_scalar_prefetch=3, grid=(n_tiles,),
            in_specs=[pl.BlockSpec((None, tk, d), tok), pl.BlockSpec((None, None, 6, d), mod),
                      pl.BlockSpec((1, d), lambda i, *_: (0, 0)),
                      pl.BlockSpec((None, 8, tk), lambda i, *_: (i, 0, 0)),
                      pl.BlockSpec(memory_space=pl.ANY)],
            out_specs=pl.BlockSpec((None, tk, d), tok),
            scratch_shapes=[pltpu.VMEM((tile_rows, d), BF16), pltpu.SemaphoreType.DMA(())]),
        out_shape=jax.ShapeDtypeStruct(h.shape, F32),
        compiler_params=_params(("arbitrary",), 40 << 20),
        name="l1_moe_combine",
    )(tab(2), tab(1), tab(0), h2, modtab, fg, slots, og)


def _block_diag_gate(gate_w):
    w = gate_w.reshape(2, 2, 2, 4, RG_BLOCK, RG_BLOCK)
    eye = jnp.eye(4, dtype=gate_w.dtype)
    return jnp.einsum('dghbij,bc->dghbicj', w, eye).reshape(2, 2, 2, 256, 256)


def _pad_cols(w, n):
    return jnp.pad(w, ((0, 0), (0, n - w.shape[1])))


def _layer0(ctx, x, modtab, norm_mix_g, norm_ffn_g, e_w_in, e_w_out, e_a_conv_w, e_a_conv_b, e_a_gate_w, e_a_gate_b,
            e_a_lambda, e_b_conv_w, e_b_a_log, e_b_dt_bias, e_b_norm_g, e_ffn_w_gate, e_ffn_w_up, e_ffn_w_down,
            *, tm, tt):
    bsz, ctx_len, d = ctx.shape
    w_in = _pad_cols(e_w_in, E_IN_PAD).astype(BF16)
    gpar = jnp.zeros((2, 128), F32)
    gpar = gpar.at[0, 2 * DN_HEADS:4 * DN_HEADS].set(e_b_a_log.reshape(-1))
    gpar = gpar.at[1, 2 * DN_HEADS:4 * DN_HEADS].set(e_b_dt_bias.reshape(-1))
    ua, gay, q, k, v, sz, gb = _inproj0(ctx, x, modtab, norm_mix_g.reshape(1, d), w_in, e_a_conv_w,
                                        e_a_conv_b.reshape(1, -1), e_b_conv_w, gpar, tm=tm)
    wg = _block_diag_gate(e_a_gate_w).astype(BF16)
    hf, hb = _rglru(jnp.transpose(ua, (1, 0, 2)), wg, e_a_gate_b.reshape(4, RG_WIDTH), e_a_lambda,
                    tt=tt, ctx_len=ctx_len)
    ha = jnp.transpose(hf + hb, (1, 0, 2))
    o0, o1 = _delta(q, k, v, gb, ctx_len=ctx_len, rows=SCAN_ROWS)
    return _l0_tail(ha, gay, o0, o1, sz, ctx, x, modtab, e_b_norm_g.reshape(1, -1), e_w_out.astype(BF16),
                    norm_ffn_g.reshape(1, d), e_ffn_w_gate.astype(BF16), e_ffn_w_up.astype(BF16),
                    e_ffn_w_down.astype(BF16), tm=tm)


def _layer1(hc, hl, modtab, norm_mix_g, norm_ffn_g, final_norm_g, o_w_in, o_w_out, o_lb_logits, o_c_norm_g,
            o_d_gate_w2, o_d_gate_b2, o_d_norm_g, o_router_w, o_router_b, o_moe_w_gate, o_moe_w_up, o_moe_w_down,
            *, tm, tk, layer):
    bsz, seq, d = hl.shape
    ctx_len = hc.shape[1]
    rows = seq // GRID_W
    hl = hl.reshape(bsz, rows, GRID_W, d).swapaxes(1, 2).reshape(bsz, seq, d)
    w_in = _pad_cols(o_w_in, O_IN_PAD).astype(BF16)
    wlr = jnp.zeros((128, SEG), F32)
    wlr = wlr.at[0:GLA_RANK, 0:256].set(o_d_gate_w2[0]).at[GLA_RANK:2 * GLA_RANK, 256:512].set(o_d_gate_w2[1])
    proj = functools.partial(_inproj1, g=norm_mix_g.reshape(1, d), w_pad=w_in, lbl=o_lb_logits, wlr=wlr,
                             b2=o_d_gate_b2.reshape(1, SEG), layer=layer)
    p1c = proj(hc, 0, ctx_len, modtab, 0, tm=tm)
    p1l = proj(hl, 0, seq, modtab, 1, tm=2 * tm)
    sh0, sg0 = _mix1_ctx(p1c, rows=SCAN_ROWS)
    o0, o1 = _mix1_lat(p1l, sh0, sg0, rows=SCAN_ROWS)
    return _moe(o0, o1, p1l[S_CG[0]], S_CG[1] * SEG // d, hl, modtab, o_c_norm_g.reshape(1, -1),
                o_d_norm_g.reshape(1, -1), o_w_out.astype(BF16), norm_ffn_g.reshape(1, d),
                final_norm_g.reshape(1, d), o_router_w, o_router_b.reshape(N_EXPERTS, 1),
                o_moe_w_gate.astype(BF16), o_moe_w_up.astype(BF16), o_moe_w_down.astype(BF16), tk=tk)


def kernel(x, c, ctx, c_ctx, ada_w, ada_b, norm_mix_g, norm_ffn_g, final_norm_g, e_w_in, e_w_out, e_a_conv_w, e_a_conv_b, e_a_gate_w, e_a_gate_b, e_a_lambda, e_b_conv_w, e_b_a_log, e_b_dt_bias, e_b_norm_g, e_ffn_w_gate, e_ffn_w_up, e_ffn_w_down, o_w_in, o_w_out, o_lb_logits, o_c_norm_g, o_d_gate_w2, o_d_gate_b2, o_d_norm_g, o_router_w, o_router_b, o_moe_w_gate, o_moe_w_up, o_moe_w_down):
    bsz, seq, d = x.shape
    ctx_len = ctx.shape[1]
    assert bsz == 8 and d == D_MODEL and ada_w.shape[0] == 2
    tm = min(256, ctx_len)
    tt = min(128, ctx_len)
    tk = min(512, seq)
    assert ctx_len % tm == 0 and seq % tm == 0 and ctx_len % SCAN_ROWS == 0 and seq % SCAN_ROWS == 0
    assert seq % GRID_W == 0 and seq % tk == 0

    mods = _ada(c, c_ctx, ada_w, ada_b)
    hc, hl = _layer0(ctx, x, _modtab(mods[0], bsz), norm_mix_g[0], norm_ffn_g[0], e_w_in[0], e_w_out[0],
                     e_a_conv_w[0], e_a_conv_b[0], e_a_gate_w[0], e_a_gate_b[0], e_a_lambda[0], e_b_conv_w[0],
                     e_b_a_log[0], e_b_dt_bias[0], e_b_norm_g[0], e_ffn_w_gate[0], e_ffn_w_up[0], e_ffn_w_down[0],
                     tm=tm, tt=tt)
    out_cm = _layer1(hc, hl, _modtab(mods[1], bsz), norm_mix_g[1], norm_ffn_g[1], final_norm_g, o_w_in[0],
                     o_w_out[0], o_lb_logits, o_c_norm_g[0], o_d_gate_w2[0], o_d_gate_b2[0], o_d_norm_g[0],
                     o_router_w[0], o_router_b[0], o_moe_w_gate[0], o_moe_w_up[0], o_moe_w_down[0],
                     tm=tm, tk=tk, layer=1)
    rows = seq // GRID_W
    return out_cm.reshape(bsz, GRID_W, rows, d).swapaxes(1, 2).reshape(bsz, seq, d)
```

```python
import functools

import jax
import jax.numpy as jnp
from jax import lax
from jax.experimental import pallas as pl
from jax.experimental.pallas import tpu as pltpu

F32 = jnp.float32
BF16 = jnp.bfloat16
HI = lax.Precision.HIGHEST

EPS = 1e-6
D_MODEL = 1024
GRID_W = 64
CONV_K = 4
RG_WIDTH = 512
RG_BLOCK = 64
RG_C = 8.0
DN_HEADS = 4
DN_D = 128
DN_CHUNK = 64
HG_HEADS = 4
HG_D = 128
GLA_HEADS = 4
GLA_DK = 64
GLA_DV = 128
GLA_RANK = 16
GLA_GATE_NORM = 16.0
MIX1_CHUNK = 64
SCAN_ROWS = 256
D_FF = 2816
N_EXPERTS = 8

E_IN_MAIN = 3072
E_IN_PAD = E_IN_MAIN + 128
O_IN_MAIN = 4096
O_IN_PAD = O_IN_MAIN + 128
SEG = 512
S_HQ, S_HV, S_GV, S_GQK, S_HK0, S_HK1, S_CG, S_DG = [(0, i) for i in range(8)]
N_SHARED = 4
S_HLF0, S_HLF1, S_GLD = [(1, i) for i in range(3)]
N_SEG = (8, 3)
P1_DTYPES = (BF16, F32)

V7X_VMEM_BYTES = 64 * 1024 * 1024
VMEM_HEADROOM_BYTES = 8 * 1024 * 1024
MOE_PIECE = 16
MOE_SEG_PIECES = (64, MOE_PIECE)
MOE_BLOCK = 512
MOE_GATHER_ROWS = 256


def _vmem(nbytes):
    return int(min(V7X_VMEM_BYTES - VMEM_HEADROOM_BYTES, nbytes))


def _params(sem, vmem_bytes):
    return pltpu.CompilerParams(dimension_semantics=sem, vmem_limit_bytes=_vmem(vmem_bytes))


def _sigmoid(x):
    return jax.nn.sigmoid(x)


def _sigmoid_tanh(x):
    return 0.5 * jnp.tanh(0.5 * x) + 0.5


def _silu(x):
    return x * jax.nn.sigmoid(x)


def _softplus(x):
    return jnp.maximum(x, 0.0) + jnp.log1p(jnp.exp(-jnp.abs(x)))


def _gelu_tanh(x):
    return 0.5 * x * (1.0 + jnp.tanh(0.7978845608028654 * (x + 0.044715 * (x * x * x))))


def _normmod(x, g, shift, scale):
    y = x * lax.rsqrt(jnp.mean(x * x, axis=-1, keepdims=True) + EPS)
    return (y * g) * (1.0 + scale) + shift


def _dot(a, b):
    return jnp.dot(a, b, preferred_element_type=F32)


def _dot_nt(a, b):
    return lax.dot_general(a, b, (((1,), (1,)), ((), ())), preferred_element_type=F32)


def _dot_tn(a, b):
    return lax.dot_general(a, b, (((0,), (0,)), ((), ())), preferred_element_type=F32)


def _dot_hi(a, b):
    return jnp.dot(a, b, precision=HI, preferred_element_type=F32)


def _split3(x):
    hi = x.astype(BF16)
    r1 = x - hi.astype(F32)
    mid = r1.astype(BF16)
    return hi, mid, (r1 - mid.astype(F32)).astype(BF16)


def _dot_hilo(a, b):
    ah = a.astype(BF16)
    al = (a - ah.astype(F32)).astype(BF16)
    bh = b.astype(BF16)
    bl = (b - bh.astype(F32)).astype(BF16)
    return _dot(ah, bh) + (_dot(ah, bl) + _dot(al, bh))


def _mask_dot(mask, x):
    mb = mask.astype(BF16)
    hi, mid, lo = _split3(x)
    return _dot(mb, hi) + (_dot(mb, mid) + _dot(mb, lo))


def _mask_dot_tn(x, mask):
    mb = mask.astype(BF16)
    hi, mid, lo = _split3(x)
    return _dot_tn(hi, mb) + (_dot_tn(mid, mb) + _dot_tn(lo, mb))


def _const_spec(shape):
    nd = len(shape)
    return pl.BlockSpec(shape, lambda *_: (0,) * nd, pipeline_mode=pl.Buffered(1))


def _scan_masks(c, d):
    row = lax.broadcasted_iota(jnp.int32, (c, c), 0)
    col = lax.broadcasted_iota(jnp.int32, (c, c), 1)
    dlt = row - col if d == 0 else col - row
    return dlt >= 0, dlt > 0, dlt <= 0, row == col


def _ada_kernel(cv_ref, w_ref, b_ref, o_ref):
    s = _silu(cv_ref[...]).astype(BF16)
    o_ref[...] = _dot(s, w_ref[...].astype(BF16)) + b_ref[...]


def _ada(c, c_ctx, ada_w, ada_b):
    depth, d, n6 = ada_w.shape
    bsz = c.shape[0]
    rows = 16
    cv = jnp.zeros((rows, d), F32).at[:bsz].set(c).at[bsz].set(c_ctx)
    tn = 1536
    return pl.pallas_call(
        _ada_kernel,
        grid=(depth, n6 // tn),
        in_specs=[pl.BlockSpec((rows, d), lambda l, j: (0, 0)),
                  pl.BlockSpec((None, d, tn), lambda l, j: (l, 0, j)),
                  pl.BlockSpec((None, 1, tn), lambda l, j: (l, 0, j))],
        out_specs=pl.BlockSpec((None, rows, tn), lambda l, j: (l, 0, j)),
        out_shape=jax.ShapeDtypeStruct((depth, rows, n6), F32),
        compiler_params=_params(("arbitrary", "arbitrary"), 32 << 20),
        name="ada_mod",
    )(cv, ada_w, ada_b.reshape(depth, 1, n6))


def _modtab(mods_l, bsz):
    m = mods_l.reshape(mods_l.shape[0], 6, D_MODEL)
    lat = m[:bsz]
    ctx = jnp.broadcast_to(m[bsz][None], (bsz, 6, D_MODEL))
    return jnp.stack([ctx, lat], axis=1)


def _inproj0_kernel(cp_ref, cm_ref, cn_ref, xp_ref, xm_ref, xn_ref, mod_ref, g_ref, w_ref, wt_ref, acw_ref, acb_ref,
                    bcw_ref, gpar_ref, ua_ref, gay_ref, q_ref, k_ref, v_ref, sz_ref, gb_ref, u_scr,
                    *, tm, ctx_tiles, n_tiles):
    t = pl.program_id(1)
    pick = lambda c_ref, x_ref: jnp.where(t < ctx_tiles, c_ref[...], x_ref[...])
    x = jnp.concatenate([pick(cp_ref, xp_ref), pick(cm_ref, xm_ref), pick(cn_ref, xn_ref)], axis=0)
    xm = _normmod(x, g_ref[...], mod_ref[0:1, :], mod_ref[1:2, :]).astype(BF16)
    seg_first = jnp.logical_or(t == 0, t == ctx_tiles)
    seg_last = jnp.logical_or(t == ctx_tiles - 1, t == n_tiles - 1)

    def project(c0, width, conv_input):
        u_scr[:, c0:c0 + width] = _dot(xm, wt_ref[...] if c0 == E_IN_MAIN else w_ref[:, c0:c0 + width])
        if conv_input:
            u_scr[0:8, c0:c0 + width] = jnp.where(seg_first, 0.0, u_scr[0:8, c0:c0 + width])
            u_scr[tm + 8:tm + 16, c0:c0 + width] = jnp.where(seg_last, 0.0, u_scr[tm + 8:tm + 16, c0:c0 + width])

    def conv(c0, width, w_ref_, w0):
        acc = u_scr[6:6 + tm, c0:c0 + width] * w_ref_[0:1, w0:w0 + width]
        for j in range(1, CONV_K):
            acc = acc + u_scr[6 + j:6 + j + tm, c0:c0 + width] * w_ref_[j:j + 1, w0:w0 + width]
        return acc

    project(0, RG_WIDTH, True)
    for grp in range(RG_WIDTH // 128):
        c0 = grp * 128
        ua_ref[:, c0:c0 + 128] = conv(c0, 128, acw_ref, c0) + acb_ref[0:1, c0:c0 + 128]
    project(512, 512, False)
    gay_ref[...] = _gelu_tanh(u_scr[8:8 + tm, 512:1024]).astype(BF16)

    for grp in range(3 * DN_HEADS):
        c0 = grp * 128
        if grp % DN_HEADS == 0:
            project(1024 + c0, DN_HEADS * DN_D, True)
        y = _silu(conv(1024 + c0, 128, bcw_ref, c0))
        if grp < 2 * DN_HEADS:
            y = y * lax.rsqrt(jnp.sum(y * y, axis=-1, keepdims=True) + EPS)
        if grp < DN_HEADS:
            q_ref[:, c0:c0 + 128] = (y * (DN_D ** -0.5)).astype(BF16)
        elif grp < 2 * DN_HEADS:
            k_ref[:, c0 - 512:c0 - 384] = y.astype(BF16)
        else:
            v_ref[:, c0 - 1024:c0 - 896] = y.astype(BF16)
    project(2560, 512, False)
    sz_ref[...] = _silu(u_scr[8:8 + tm, 2560:3072]).astype(BF16)

    project(3072, 128, False)
    xg = u_scr[8:8 + tm, 3072:3200]
    lane = lax.broadcasted_iota(jnp.int32, xg.shape, 1)
    g = -jnp.exp(gpar_ref[0:1, :]) * _softplus(xg + gpar_ref[1:2, :])
    gb_ref[...] = jnp.where(lane < 2 * DN_HEADS, _sigmoid(xg), g)


def _row_specs(tm, d, ctx_tiles, ctx_len, seq, halo):
    tb = tm // 8

    def specs(n_rows, tile_of):
        main = pl.BlockSpec((None, tm, d), lambda b, t: (b, jnp.clip(tile_of(t), 0, n_rows // tm - 1), 0))
        if not halo:
            return [main]
        prev = pl.BlockSpec((None, 8, d), lambda b, t: (b, jnp.clip(tile_of(t) * tb - 1, 0, n_rows // 8 - 1), 0))
        nxt = pl.BlockSpec((None, 8, d), lambda b, t: (b, jnp.clip((tile_of(t) + 1) * tb, 0, n_rows // 8 - 1), 0))
        return [prev, main, nxt]

    return specs(ctx_len, lambda t: t) + specs(seq, lambda t: t - ctx_tiles)


def _inproj0(ctx, x, modtab, g, w, w_tail, acw, acb, bcw, gpar, *, tm):
    bsz, ctx_len, d = ctx.shape
    seq = x.shape[1]
    t_all = ctx_len + seq
    n_tiles = t_all // tm
    ctx_tiles = ctx_len // tm
    kern = functools.partial(_inproj0_kernel, tm=tm, ctx_tiles=ctx_tiles, n_tiles=n_tiles)
    tok = lambda w, dt=BF16: jax.ShapeDtypeStruct((bsz, t_all, w), dt)
    tok_spec = lambda w: pl.BlockSpec((None, tm, w), lambda b, t: (b, t, 0))
    return pl.pallas_call(
        kern,
        grid=(bsz, n_tiles),
        in_specs=_row_specs(tm, d, ctx_tiles, ctx_len, seq, True) + [
            pl.BlockSpec((None, None, 6, d), lambda b, t: (b, jnp.where(t >= ctx_tiles, 1, 0), 0, 0)),
            _const_spec((1, d)),
            _const_spec(w.shape), _const_spec((d, 128)),
            _const_spec((CONV_K, RG_WIDTH)),
            _const_spec((1, RG_WIDTH)),
            _const_spec((CONV_K, 3 * DN_HEADS * DN_D)),
            _const_spec((2, 128)),
        ],
        out_specs=[tok_spec(512), tok_spec(512), tok_spec(512), tok_spec(512), tok_spec(512), tok_spec(512),
                   tok_spec(128)],
        out_shape=[tok(512, F32), tok(512), tok(512), tok(512), tok(512), tok(512), tok(128, F32)],
        scratch_shapes=[pltpu.VMEM((tm + 16, E_IN_PAD), F32)],
        compiler_params=_params(("arbitrary", "arbitrary"), 40 << 20),
        name="l0_inproj",
    )(ctx, ctx, ctx, x, x, x, modtab, g, w, w_tail, acw, acb, bcw, gpar)


def _rglru_kernel(uf_ref, ub_ref, wg_ref, gbias_ref, lam_ref, hf_ref, hb_ref,
                  af_scr, xf_scr, ab_scr, xb_scr, h_scr, *, tt, bsz):
    s = pl.program_id(0)

    @pl.when(s == 0)
    def _():
        h_scr[...] = jnp.zeros_like(h_scr)

    def gates(u_ref, d, a_scr, x_scr):
        x = u_ref[...].reshape(tt * bsz, RG_WIDTH)
        xb = x.astype(BF16)
        for half in range(2):
            c0 = half * 256
            xh = xb[:, c0:c0 + 256]
            r = _sigmoid_tanh(_dot(xh, wg_ref[d, 0, half]) + gbias_ref[2 * d:2 * d + 1, c0:c0 + 256])
            i = _sigmoid_tanh(_dot(xh, wg_ref[d, 1, half]) + gbias_ref[2 * d + 1:2 * d + 2, c0:c0 + 256])
            log_a = (-RG_C) * r * _softplus(-lam_ref[d:d + 1, c0:c0 + 256])
            a = jnp.exp(log_a)
            mult = jnp.sqrt(-jnp.tanh(log_a) * (a * a + 1.0))
            xin = mult * (i * x[:, c0:c0 + 256])
            a_scr[:, :, c0:c0 + 256] = a.reshape(tt, bsz, 256)
            x_scr[:, :, c0:c0 + 256] = xin.reshape(tt, bsz, 256)

    gates(uf_ref, 0, af_scr, xf_scr)
    gates(ub_ref, 1, ab_scr, xb_scr)

    def step(t, carry):
        hf, hb = carry
        hf = af_scr[t] * hf + xf_scr[t]
        hf_ref[t] = hf
        tb = tt - 1 - t
        hb = ab_scr[tb] * hb + xb_scr[tb]
        hb_ref[tb] = hb
        return hf, hb

    hf, hb = lax.fori_loop(0, tt, step, (h_scr[0], h_scr[1]), unroll=8)
    h_scr[0] = hf
    h_scr[1] = hb


def _rglru(ua3, wg, gbias, lam, *, tt, ctx_len):
    t_all, bsz, w = ua3.shape
    n_steps = t_all // tt
    nc = ctx_len // tt

    def bwd(s):
        return jnp.where(s < nc, nc - 1 - s, n_steps + nc - 1 - s)

    blk = (tt, bsz, w)
    kern = functools.partial(_rglru_kernel, tt=tt, bsz=bsz)
    return pl.pallas_call(
        kern,
        grid=(n_steps,),
        in_specs=[pl.BlockSpec(blk, lambda s: (s, 0, 0)),
                  pl.BlockSpec(blk, lambda s: (bwd(s), 0, 0)),
                  _const_spec(wg.shape), _const_spec(gbias.shape), _const_spec(lam.shape)],
        out_specs=[pl.BlockSpec(blk, lambda s: (s, 0, 0)),
                   pl.BlockSpec(blk, lambda s: (bwd(s), 0, 0))],
        out_shape=[jax.ShapeDtypeStruct(ua3.shape, F32)] * 2,
        scratch_shapes=[pltpu.VMEM(blk, F32)] * 4 + [pltpu.VMEM((2, bsz, w), F32)],
        compiler_params=_params(("arbitrary",), 40 << 20),
        name="l0_rglru",
    )(ua3, ua3, wg, gbias, lam)


def _delta_kernel(qf_ref, kf_ref, vf_ref, gf_ref, qb_ref, kb_ref, vb_ref, gb_ref, of_ref, ob_ref, s_scr, *, n_sub):
    c = DN_CHUNK

    @pl.when(pl.program_id(1) == 0)
    def _():
        s_scr[...] = jnp.zeros_like(s_scr)

    dir_refs = ((qf_ref, kf_ref, vf_ref, gf_ref, of_ref), (qb_ref, kb_ref, vb_ref, gb_ref, ob_ref))
    masks = [_scan_masks(c, d) for d in range(2)]
    eye = jnp.where(masks[0][3], 1.0, 0.0)

    cums = {}
    for d in range(2):
        incl, _, incl_t, _ = masks[d]
        m_incl = jnp.where(incl, 1.0, 0.0)
        m_incl_t = jnp.where(incl_t, 1.0, 0.0)
        for ci in range(n_sub):
            g_all = dir_refs[d][3][ci * c:(ci + 1) * c, :]
            gc_all = _mask_dot(m_incl, g_all)
            gct_all = _mask_dot_tn(g_all, m_incl_t)
            cums[d, ci] = (g_all, gc_all, gct_all)

    chains = []
    for d in range(2):
        q_ref, k_ref, v_ref, _, _ = dir_refs[d]
        incl, strict, _, _ = masks[d]
        last = c - 1 if d == 0 else 0
        for ci in range(n_sub):
            g_all, gc_all, gct_all = cums[d, ci]
            rs = slice(ci * c, (ci + 1) * c)
            for h in range(DN_HEADS):
                hs = slice(h * DN_D, (h + 1) * DN_D)
                lane = 2 * DN_HEADS + d * DN_HEADS + h
                ch = dict(d=d, ci=ci, h=h, rs=rs, hs=hs, incl=incl, strict=strict)
                ch["beta"] = g_all[:, d * DN_HEADS + h:d * DN_HEADS + h + 1]
                gc = jnp.broadcast_to(gc_all[:, lane:lane + 1], (c, DN_D))
                gc_row = jnp.broadcast_to(gct_all[lane:lane + 1, :], (c, c))
                ch["gc"] = gc
                ch["gtot"] = gc[last:last + 1, :]
                ch["decay"] = jnp.where(incl, jnp.exp(jnp.minimum(gc[:, 0:c] - gc_row, 0.0)), 0.0)
                ch["e_gc"] = jnp.exp(gc)
                ch["q"] = q_ref[rs, hs].astype(F32)
                ch["k"] = k_ref[rs, hs].astype(F32)
                ch["v"] = v_ref[rs, hs].astype(F32)
                chains.append(ch)

    for ch in chains:
        ch["kb"] = ch["k"] * ch["beta"]
        qk = _dot_nt(jnp.concatenate([ch["kb"], ch["q"]], axis=0).astype(BF16), ch["k"].astype(BF16))
        ch["neg"] = -jnp.where(ch["strict"], qk[0:c] * ch["decay"], 0.0)
        ch["a_qk"] = (qk[c:2 * c] * ch["decay"]).astype(BF16)
    for ch in chains:
        negb = ch["neg"].astype(BF16)
        ch["t"] = eye + ch["neg"]
        ch["p"] = _dot(negb, negb)
    n_sq = max(1, (c - 1).bit_length() - 1)
    for it in range(n_sq):
        for ch in chains:
            tp = _dot(jnp.concatenate([ch["t"], ch["p"]], axis=0).astype(BF16), ch["p"].astype(BF16))
            ch["t"] = ch["t"] + tp[0:c]
            ch["p"] = tp[c:2 * c]
    for ch in chains:
        rhs = jnp.concatenate([ch["v"] * ch["beta"], ch["kb"] * ch["e_gc"]], axis=1).astype(BF16)
        sol = _dot(ch["t"].astype(BF16), rhs)
        ch["u"] = sol[:, 0:DN_D]
        ch["wq"] = jnp.concatenate([sol[:, DN_D:2 * DN_D], ch["q"] * ch["e_gc"]], axis=0).astype(BF16)
        ch["k_tail"] = (ch["k"] * jnp.exp(ch["gtot"] - ch["gc"])).astype(BF16)

    by_key = {(ch["d"], ch["ci"], ch["h"]): ch for ch in chains}
    for step in range(n_sub):
        live = [by_key[d, step if d == 0 else n_sub - 1 - step, h] for d in range(2) for h in range(DN_HEADS)]
        for ch in live:
            ch["st"] = s_scr[ch["d"], ch["h"]]
            ch["ws"] = _dot(ch["wq"], ch["st"].astype(BF16))
        for ch in live:
            vnb = (ch["u"] - ch["ws"][0:c]).astype(BF16)
            o = ch["ws"][c:2 * c] + _dot(ch["a_qk"], vnb)
            dir_refs[ch["d"]][4][ch["rs"], ch["hs"]] = o.astype(BF16)
            s_scr[ch["d"], ch["h"]] = ch["st"] * jnp.exp(ch["gtot"]) + _dot_tn(ch["k_tail"], vnb)


def _delta(q, k, v, gb, *, ctx_len, rows):
    bsz, t_all, w = q.shape
    n_steps = t_all // rows
    nc = ctx_len // rows

    def bwd(s):
        return jnp.where(s < nc, nc - 1 - s, n_steps + nc - 1 - s)

    fwd_spec = lambda width: pl.BlockSpec((None, rows, width), lambda b, s: (b, s, 0))
    bwd_spec = lambda width: pl.BlockSpec((None, rows, width), lambda b, s: (b, bwd(s), 0))
    return pl.pallas_call(
        functools.partial(_delta_kernel, n_sub=rows // DN_CHUNK),
        grid=(bsz, n_steps),
        in_specs=[fwd_spec(w), fwd_spec(w), fwd_spec(w), fwd_spec(128),
                  bwd_spec(w), bwd_spec(w), bwd_spec(w), bwd_spec(128)],
        out_specs=[fwd_spec(w), bwd_spec(w)],
        out_shape=[jax.ShapeDtypeStruct((bsz, t_all, w), BF16)] * 2,
        scratch_shapes=[pltpu.VMEM((2, DN_HEADS, DN_D, DN_D), F32)],
        compiler_params=_params(("arbitrary", "arbitrary"), 32 << 20),
        name="l0_deltanet",
    )(q, k, v, gb, q, k, v, gb)


def _head_norm(y, g):
    return y * lax.rsqrt(jnp.mean(y * y, axis=-1, keepdims=True) + EPS) * g


def _l0_tail_kernel(ha_ref, gay_ref, o0_ref, o1_ref, sz_ref, ctx_ref, x_ref, mod_ref, ng_ref, wo_ref, g_ref, wg_ref,
                    wu_ref, wd_ref, hc_ref, hl_ref, *, n_chunks, ctx_tiles):
    t = pl.program_id(1)
    tm = ha_ref.shape[0]
    halves = [dict(rs=slice(i * tm // 2, (i + 1) * tm // 2)) for i in range(2)]
    for hv in halves:
        rs = hv["rs"]
        parts = [(ha_ref[rs, :] * gay_ref[rs, :].astype(F32)).astype(BF16)]
        for hd in range(DN_HEADS):
            lo = hd * DN_D
            ob = o0_ref[rs, lo:lo + DN_D].astype(F32) + o1_ref[rs, lo:lo + DN_D].astype(F32)
            parts.append((_head_norm(ob, ng_ref[...]) * sz_ref[rs, lo:lo + DN_D].astype(F32)).astype(BF16))
        hv["ycat"] = jnp.concatenate(parts, axis=-1)
    for hv in halves:
        h = jnp.where(t < ctx_tiles, ctx_ref[hv["rs"], :], x_ref[hv["rs"], :])
        hv["x"] = h + mod_ref[2:3, :] * _dot(hv["ycat"], wo_ref[...])
    for hv in halves:
        hv["xm"] = _normmod(hv["x"], g_ref[...], mod_ref[3:4, :], mod_ref[4:5, :]).astype(BF16)
        hv["acc"] = jnp.zeros(hv["x"].shape, F32)
    cw = D_FF // n_chunks
    for ci in range(n_chunks):
        c0 = ci * cw
        for hv in halves:
            hv["act"] = (_silu(_dot(hv["xm"], wg_ref[:, c0:c0 + cw])) * _dot(hv["xm"], wu_ref[:, c0:c0 + cw])).astype(BF16)
        for hv in halves:
            hv["acc"] = hv["acc"] + _dot(hv["act"], wd_ref[c0:c0 + cw, :])
    out = jnp.concatenate([hv["x"] + mod_ref[5:6, :] * hv["acc"] for hv in halves], axis=0)

    @pl.when(t < ctx_tiles)
    def _():
        hc_ref[...] = out

    @pl.when(t >= ctx_tiles)
    def _():
        hl_ref[...] = out


def _l0_tail(ha, gay, o0, o1, sz, ctx, x, modtab, ng, wo, g, wg, wu, wd, *, tm):
    bsz, ctx_len, d = ctx.shape
    seq = x.shape[1]
    t_all = ctx_len + seq
    ctx_tiles = ctx_len // tm
    tok = lambda width: pl.BlockSpec((None, tm, width), lambda b, t: (b, t, 0))
    return pl.pallas_call(
        functools.partial(_l0_tail_kernel, n_chunks=2, ctx_tiles=ctx_tiles),
        grid=(bsz, t_all // tm),
        in_specs=[tok(512), tok(512), tok(512), tok(512), tok(512)] + _row_specs(tm, d, ctx_tiles, ctx_len, seq, False) + [
                  pl.BlockSpec((None, None, 6, d), lambda b, t: (b, jnp.where(t >= ctx_tiles, 1, 0), 0, 0)),
                  _const_spec((1, DN_D)), _const_spec((d, d)),
                  _const_spec((1, d)), _const_spec((d, D_FF)), _const_spec((d, D_FF)), _const_spec((D_FF, d))],
        out_specs=[pl.BlockSpec((None, tm, d), lambda b, t: (b, jnp.minimum(t, ctx_tiles - 1), 0)),
                   pl.BlockSpec((None, tm, d), lambda b, t: (b, jnp.maximum(t - ctx_tiles, 0), 0))],
        out_shape=[jax.ShapeDtypeStruct((bsz, ctx_len, d), F32), jax.ShapeDtypeStruct((bsz, t_all - ctx_len, d), F32)],
        compiler_params=_params(("arbitrary", "arbitrary"), 48 << 20),
        name="l0_tail",
    )(ha, gay, o0, o1, sz, ctx, x, modtab, ng, wo, g, wg, wu, wd)


def _inproj1_kernel(h_ref, mod_ref, g_ref, w_ref, wt_ref, lbl_ref, wlr_ref, b2_ref, pa_ref, pb_ref, u_scr, *, layer):
    x = h_ref[...]
    xm = _normmod(x, g_ref[...], mod_ref[0:1, :], mod_ref[1:2, :]).astype(BF16)

    lg = lbl_ref[...]
    ex = jnp.exp(lg - jnp.max(lg, axis=0, keepdims=True))
    lbw = ex / jnp.sum(ex, axis=0, keepdims=True)
    lb = jnp.sum(lbw[1:layer + 1], axis=0, keepdims=True)

    def project(c0, width):
        u_scr[:, c0:c0 + width] = _dot(xm, wt_ref[...] if c0 == O_IN_MAIN else w_ref[:, c0:c0 + width])

    def put(seg, off, val):
        ref = (pa_ref, pb_ref)[seg[0]]
        ref[:, seg[1] * SEG + off:seg[1] * SEG + off + val.shape[1]] = val.astype(ref.dtype)

    def groups(fn):
        for grp in range(SEG // 128):
            fn(grp * 128)

    project(0, SEG)
    groups(lambda c0: put(S_HQ, c0, _silu(u_scr[:, c0:c0 + 128]) * (HG_D ** -0.5)))
    for dr, (sk, sf) in enumerate(((S_HK0, S_HLF0), (S_HK1, S_HLF1))):
        project(512 + dr * 512, SEG)

        def forget(c0, dr=dr, sk=sk, sf=sf):
            lbg = lb[:, c0:c0 + 128]
            fl = u_scr[:, 512 + dr * 512 + c0:512 + dr * 512 + c0 + 128]
            sg = _sigmoid(fl)
            put(sf, c0, jnp.log(lbg + (1.0 - lbg) * sg))
            put(sk, c0, (1.0 - lbg) * (1.0 - sg))
        groups(forget)
    project(1536, SEG)
    groups(lambda c0: put(S_HV, c0, u_scr[:, 1536 + c0:1536 + c0 + 128]))
    project(2048, SEG)
    groups(lambda c0: put(S_CG, c0, _silu(u_scr[:, 2048 + c0:2048 + c0 + 128])))
    project(2560, SEG)
    put(S_GQK, 0, u_scr[:, 2560:2816] * (GLA_DK ** -0.5))
    put(S_GQK, 256, u_scr[:, 2816:3072])
    project(3072, SEG)
    groups(lambda c0: put(S_GV, c0, u_scr[:, 3072 + c0:3072 + c0 + 128]))
    project(3584, SEG)
    groups(lambda c0: put(S_DG, c0, _silu(u_scr[:, 3584 + c0:3584 + c0 + 128])))
    project(4096, 128)
    lr = u_scr[:, 4096:4224]
    put(S_GLD, 0, -_softplus(-(_dot_hilo(lr, wlr_ref[...]) + b2_ref[...])) * (1.0 / GLA_GATE_NORM))


def _inproj1(h, row0, n_rows, modtab, seg, g, w, w_tail, lbl, wlr, b2, *, tm, layer):
    bsz, _, d = h.shape
    t0 = row0 // tm
    return pl.pallas_call(
        functools.partial(_inproj1_kernel, layer=layer),
        grid=(bsz, n_rows // tm),
        in_specs=[pl.BlockSpec((None, tm, d), lambda b, t: (b, t0 + t, 0)),
                  pl.BlockSpec((None, None, 6, d), lambda b, t: (b, seg, 0, 0)),
                  _const_spec((1, d)), _const_spec(w.shape), _const_spec((d, 128)), _const_spec(lbl.shape),
                  _const_spec((128, SEG)), _const_spec((1, SEG))],
        out_specs=[pl.BlockSpec((None, tm, n * SEG), lambda b, t: (b, t, 0)) for n in N_SEG],
        out_shape=[jax.ShapeDtypeStruct((bsz, n_rows, n * SEG), dt) for n, dt in zip(N_SEG, P1_DTYPES)],
        scratch_shapes=[pltpu.VMEM((tm, O_IN_PAD), F32)],
        compiler_params=_params(("arbitrary", "arbitrary"), 44 << 20),
        name="l1_inproj",
    )(h, modtab, g, w, w_tail, lbl, wlr, b2)


def _gla_stream(d, q_all, k_all, ld_all, v_all, st_ref, o_ref, o_lane0, r0, n_heads, dk, dv, incl, m_incl):
    c = k_all.shape[0]
    mid = c // 2 - 1 if d == 0 else c // 2
    last = c - 1 if d == 0 else 0
    bc = _mask_dot(m_incl, ld_all)
    m = bc[mid:mid + 1]
    btot = bc[last:last + 1]
    kn = k_all.astype(F32) * jnp.exp(m - bc)
    it = dict(d=d, r0=r0, st_ref=st_ref, o_ref=o_ref, o_lane0=o_lane0, n_heads=n_heads, dk=dk, dv=dv, incl=incl,
              c=c, kt=(kn * jnp.exp(btot - m)).astype(BF16), dec=jnp.exp(btot), v=v_all.astype(BF16),
              want_out=q_all is not None)
    if q_all is not None:
        qe = q_all.astype(F32) * jnp.exp(bc)
        it.update(qd=(qe * jnp.exp(-m)).astype(BF16), qe=qe.astype(BF16), knb=kn.astype(BF16))
    return it


def _gla_intra(it):
    dk = it["dk"]
    it["a"] = [jnp.where(it["incl"], _dot_nt(it["qd"][:, hd * dk:(hd + 1) * dk], it["knb"][:, hd * dk:(hd + 1) * dk]),
                         0.0).astype(BF16) for hd in range(it["n_heads"])]


def _gla_advance(it):
    d, dk, dv, c, st_ref = it["d"], it["dk"], it["dv"], it["c"], it["st_ref"]
    sts = [st_ref[d, hd] for hd in range(it["n_heads"])]
    if it["want_out"]:
        for hd in range(it["n_heads"]):
            v = it["v"][:, hd * dv:(hd + 1) * dv]
            o = _dot(it["a"][hd], v) + _dot_nt(it["qe"][:, hd * dk:(hd + 1) * dk], sts[hd].astype(BF16))
            it["o_ref"][it["r0"]:it["r0"] + c, it["o_lane0"] + hd * dv:it["o_lane0"] + (hd + 1) * dv] = o.astype(BF16)
    for hd in range(it["n_heads"]):
        ks = slice(hd * dk, (hd + 1) * dk)
        st_ref[d, hd] = sts[hd] * it["dec"][:, ks] + _dot_tn(it["v"][:, hd * dv:(hd + 1) * dv], it["kt"][:, ks])


def _mix1_body(dirs, n_sub, sh_ref, sg_ref):
    c = MIX1_CHUNK
    gw = GLA_HEADS * GLA_DK
    lane = lambda seg: slice(seg[1] * SEG, (seg[1] + 1) * SEG)
    prepared = {}
    for d, (sh_in, hk_ref, hlf_ref, gld_ref, o_ref) in enumerate(dirs):
        incl = _scan_masks(c, d)[0]
        m_incl = jnp.where(incl, 1.0, 0.0)
        g0 = S_GQK[1] * SEG
        for ci in range(n_sub):
            r0 = ci * c
            rs = slice(r0, r0 + c)
            prepared[d, ci, 0] = _gla_stream(
                d, None if o_ref is None else sh_in[rs, lane(S_HQ)], hk_ref[rs, :], hlf_ref[rs, :],
                sh_in[rs, lane(S_HV)], sh_ref, o_ref, 0, r0, HG_HEADS, HG_D, HG_D, incl, m_incl)
            prepared[d, ci, 1] = _gla_stream(
                d, None if o_ref is None else sh_in[rs, g0:g0 + gw], sh_in[rs, g0 + gw:g0 + 2 * gw],
                gld_ref[rs, d * gw:(d + 1) * gw], sh_in[rs, lane(S_GV)],
                sg_ref, o_ref, HG_HEADS * HG_D, r0, GLA_HEADS, GLA_DK, GLA_DV, incl, m_incl)
    for it in prepared.values():
        if it["want_out"]:
            _gla_intra(it)
    for step in range(n_sub):
        for d in range(2):
            for stream in range(2):
                _gla_advance(prepared[d, step if d == 0 else n_sub - 1 - step, stream])


def _mix1_ctx_kernel(*refs, n_sub):
    fwd, bwd, (sh_ref, sg_ref) = refs[0:4], refs[4:8], refs[8:10]

    @pl.when(pl.program_id(1) == 0)
    def _():
        sh_ref[...] = jnp.zeros_like(sh_ref)
        sg_ref[...] = jnp.zeros_like(sg_ref)

    _mix1_body([tuple(r) + (None,) for r in (fwd, bwd)], n_sub, sh_ref, sg_ref)


def _mix1_lat_kernel(*refs, n_sub):
    fwd, bwd = refs[0:4], refs[4:8]
    sh0_ref, sg0_ref, of_ref, ob_ref, sh_scr, sg_scr = refs[8:14]

    @pl.when(pl.program_id(1) == 0)
    def _():
        sh_scr[...] = sh0_ref[...]
        sg_scr[...] = sg0_ref[...]

    _mix1_body([tuple(fwd) + (of_ref,), tuple(bwd) + (ob_ref,)], n_sub, sh_scr, sg_scr)


def _mix1_specs(p1, rows, n_steps):
    specs, args = [], []
    for d in range(2):
        blk = (lambda b, s: s) if d == 0 else (lambda b, s: n_steps - 1 - s)
        for (arr, sg), n_seg in ((S_HQ, N_SHARED), ((S_HK0, S_HK1)[d], 1), ((S_HLF0, S_HLF1)[d], 1), (S_GLD, 1)):
            assert sg % n_seg == 0
            specs.append(pl.BlockSpec((None, rows, n_seg * SEG),
                                      lambda b, s, blk=blk, sg=sg // n_seg: (b, blk(b, s), sg)))
            args.append(p1[arr])
    return specs, args


_SH_SHAPE = (2, HG_HEADS, HG_D, HG_D)
_SG_SHAPE = (2, GLA_HEADS, GLA_DV, GLA_DK)


def _mix1_ctx(p1c, *, rows):
    bsz, ctx_len, _ = p1c[0].shape
    n_steps = ctx_len // rows
    specs, args = _mix1_specs(p1c, rows, n_steps)
    state = lambda shape: pl.BlockSpec((None,) + shape, lambda b, s: (b, 0, 0, 0, 0))
    return pl.pallas_call(
        functools.partial(_mix1_ctx_kernel, n_sub=rows // MIX1_CHUNK),
        grid=(bsz, n_steps),
        in_specs=specs,
        out_specs=[state(_SH_SHAPE), state(_SG_SHAPE)],
        out_shape=[jax.ShapeDtypeStruct((bsz,) + _SH_SHAPE, F32), jax.ShapeDtypeStruct((bsz,) + _SG_SHAPE, F32)],
        compiler_params=_params(("arbitrary", "arbitrary"), 32 << 20),
        name="l1_ctx_state",
    )(*args)


def _mix1_lat(p1l, sh0, sg0, *, rows):
    bsz, seq, _ = p1l[0].shape
    n_steps = seq // rows
    specs, args = _mix1_specs(p1l, rows, n_steps)
    state = lambda shape: pl.BlockSpec((None,) + shape, lambda b, s: (b, 0, 0, 0, 0))
    ow = HG_HEADS * HG_D + GLA_HEADS * GLA_DV
    return pl.pallas_call(
        functools.partial(_mix1_lat_kernel, n_sub=rows // MIX1_CHUNK),
        grid=(bsz, n_steps),
        in_specs=specs + [state(_SH_SHAPE), state(_SG_SHAPE)],
        out_specs=[pl.BlockSpec((None, rows, ow), lambda b, s: (b, s, 0)),
                   pl.BlockSpec((None, rows, ow), lambda b, s: (b, n_steps - 1 - s, 0))],
        out_shape=[jax.ShapeDtypeStruct((bsz, seq, ow), BF16)] * 2,
        scratch_shapes=[pltpu.VMEM(_SH_SHAPE, F32), pltpu.VMEM(_SG_SHAPE, F32)],
        compiler_params=_params(("arbitrary", "arbitrary"), 32 << 20),
        name="l1_scan",
    )(*args, sh0, sg0)


def _outproj1(o0_ref, o1_ref, gate_ref, h_ref, mod_ref, cng_ref, dng_ref, w_ref):
    parts = []
    for hd in range(HG_HEADS + GLA_HEADS):
        lo = hd * 128
        y = o0_ref[:, lo:lo + 128].astype(F32) + o1_ref[:, lo:lo + 128].astype(F32)
        ng = cng_ref[...] if hd < HG_HEADS else dng_ref[...]
        parts.append((_head_norm(y, ng) * gate_ref[:, lo:lo + 128].astype(F32)).astype(BF16))
    return h_ref[...] + mod_ref[2:3, :] * _dot(jnp.concatenate(parts, axis=-1), w_ref[...])


def _for_pieces(length, fn):
    done = 0
    for rows in MOE_SEG_PIECES:
        n = (length - done) // rows

        def body(p, carry, rows=rows, done=done):
            fn(done + p * rows, rows)
            return carry
        lax.fori_loop(0, n, body, 0)
        done = done + n * rows


def _moe_route_kernel(o0_ref, o1_ref, gate_ref, h_ref, mod_ref, cng_ref, dng_ref, wo_ref, ng_ref, rw_ref, rb_ref,
                      h2_ref, slot_ref, seg_ref, xs_hbm, xn_scr, xg_scr, zero_scr, base_smem, sem, *, tk):
    i = pl.program_id(0)
    pc = MOE_PIECE
    gr = MOE_GATHER_ROWS

    @pl.when(i == 0)
    def _():
        for e in range(N_EXPERTS):
            base_smem[e] = 0
        zero_scr[...] = jnp.zeros_like(zero_scr)

    h2 = _outproj1(o0_ref, o1_ref, gate_ref, h_ref, mod_ref, cng_ref, dng_ref, wo_ref)
    h2_ref[...] = h2
    xm = _normmod(h2, ng_ref[...], mod_ref[3:4, :], mod_ref[4:5, :])
    xn_scr[...] = xm.astype(BF16)
    lg = jnp.transpose(_dot_hilo(xm, rw_ref[...]))[0:N_EXPERTS, :] + rb_ref[...]
    eidx = lax.broadcasted_iota(jnp.int32, lg.shape, 0).astype(F32)
    m1 = jnp.max(lg, axis=0, keepdims=True)
    i1 = jnp.min(jnp.where(lg == m1, eidx, float(N_EXPERTS)), axis=0, keepdims=True)
    lg2 = jnp.where(eidx == i1, -jnp.inf, lg)
    m2 = jnp.max(lg2, axis=0, keepdims=True)
    i2 = jnp.min(jnp.where(lg2 == m2, eidx, float(N_EXPERTS)), axis=0, keepdims=True)
    ex = jnp.exp(m2 - m1)
    p1 = 1.0 / (1.0 + ex)
    sel = jnp.where(eidx == i1, 1.0, 0.0) + jnp.where(eidx == i2, 1.0, 0.0)
    lane = lax.broadcasted_iota(jnp.int32, lg.shape, 1)
    cum = sel
    sh = 1
    while sh < tk:
        cum = cum + jnp.where(lane >= sh, pltpu.roll(cum, sh, 1), 0.0)
        sh *= 2
    padded = jnp.floor((cum[:, tk - 1:tk] + (pc - 1.0)) * (1.0 / pc)) * pc
    padded = jnp.broadcast_to(padded, (N_EXPERTS, 128))
    er = lax.broadcasted_iota(jnp.int32, (N_EXPERTS, N_EXPERTS), 0)
    ec = lax.broadcasted_iota(jnp.int32, (N_EXPERTS, N_EXPERTS), 1)
    off = _dot_hi(jnp.where(er > ec, 1.0, 0.0), padded)
    slot = off[:, 0:1] + cum - 1.0
    slot_a = jnp.sum(jnp.where(eidx == i1, slot, 0.0), axis=0, keepdims=True)
    slot_b = jnp.sum(jnp.where(eidx == i2, slot, 0.0), axis=0, keepdims=True)
    slot_ref[...] = jnp.concatenate([slot_a, slot_b, p1, ex * p1, jnp.zeros((4, tk), F32)], axis=0)

    total = jnp.max(off[N_EXPERTS - 1:N_EXPERTS, :] + padded[N_EXPERTS - 1:N_EXPERTS, :]).astype(jnp.int32)

    def gather(ci, carry):
        r0 = pl.multiple_of(ci * gr, gr)
        rid = (lax.broadcasted_iota(jnp.int32, (gr, tk), 0) + r0).astype(F32)
        p = jnp.where(rid == slot_a, 1.0, 0.0) + jnp.where(rid == slot_b, 1.0, 0.0)
        xg_scr[pl.ds(r0, gr), :] = _dot(p.astype(BF16), xn_scr[...]).astype(BF16)
        return carry
    lax.fori_loop(0, (total + gr - 1) // gr, gather, 0)

    erow = lax.broadcasted_iota(jnp.int32, (N_EXPERTS, 128), 0)
    base_vec = jnp.zeros((N_EXPERTS, 128), F32)
    segs = []
    for e in range(N_EXPERTS):
        off_e = jnp.max(off[e:e + 1, :]).astype(jnp.int32)
        len_e = jnp.max(padded[e:e + 1, :]).astype(jnp.int32)
        base_e = base_smem[e]
        base_vec = jnp.where(erow == e, base_e.astype(F32), base_vec)
        segs.append((e, off_e, len_e, base_e))
    seg_ref[0] = off
    seg_ref[1] = padded
    seg_ref[2] = base_vec

    def seg_copy(e, off_e, base_e, r, rows):
        return pltpu.make_async_copy(
            xg_scr.at[pl.ds(pl.multiple_of(off_e + r, pc), rows), :],
            xs_hbm.at[e, pl.ds(pl.multiple_of(base_e + r, pc), rows), :], sem)

    for e, off_e, len_e, base_e in segs:
        _for_pieces(len_e, lambda r, rows, e=e, off_e=off_e, base_e=base_e: seg_copy(e, off_e, base_e, r, rows).start())
    for e, off_e, len_e, base_e in segs:
        _for_pieces(len_e, lambda r, rows, e=e, off_e=off_e, base_e=base_e: seg_copy(e, off_e, base_e, r, rows).wait())
        base_smem[e] = base_e + len_e

    @pl.when(i == pl.num_programs(0) - 1)
    def _():
        def tail_copy(e, p):
            end = base_smem[e]
            return pltpu.make_async_copy(zero_scr, xs_hbm.at[e, pl.ds(pl.multiple_of(end + p * pc, pc), pc), :], sem)

        def n_tail(e):
            rem = lax.rem(base_smem[e], MOE_BLOCK)
            return jnp.where(rem == 0, 0, MOE_BLOCK - rem) // pc

        for e in range(N_EXPERTS):
            def start(p, carry, e=e):
                tail_copy(e, p).start()
                return carry
            lax.fori_loop(0, n_tail(e), start, 0)
        for e in range(N_EXPERTS):
            def wait(p, carry, e=e):
                tail_copy(e, p).wait()
                return carry
            lax.fori_loop(0, n_tail(e), wait, 0)


def _moe_ffn_kernel(eid_ref, blk_ref, nv_ref, x_ref, wg_ref, wu_ref, wd_ref, o_ref, *, n_chunks):
    del eid_ref, blk_ref

    @pl.when(pl.program_id(0) < nv_ref[0])
    def _():
        x = x_ref[...]
        cw = D_FF // n_chunks
        acc = jnp.zeros(x.shape, F32)
        for ci in range(n_chunks):
            c0 = ci * cw
            act = (_silu(_dot(x, wg_ref[:, c0:c0 + cw])) * _dot(x, wu_ref[:, c0:c0 + cw])).astype(BF16)
            acc = acc + _dot(act, wd_ref[c0:c0 + cw, :])
        o_ref[...] = acc.astype(BF16)


def _moe_combine_kernel(base_ref, len_ref, off_ref, h_ref, mod_ref, fg_ref, slot_ref, og_hbm, out_ref,
                        og_scr, sem, *, tk):
    i = pl.program_id(0)
    pc = MOE_PIECE
    gr = MOE_GATHER_ROWS
    n_rows = og_scr.shape[0]

    def seg_copy(e, r, rows):
        return pltpu.make_async_copy(
            og_hbm.at[e, pl.ds(pl.multiple_of(base_ref[i * N_EXPERTS + e] + r, pc), rows), :],
            og_scr.at[pl.ds(pl.multiple_of(off_ref[i * N_EXPERTS + e] + r, pc), rows), :], sem)

    for e in range(N_EXPERTS):
        _for_pieces(len_ref[i * N_EXPERTS + e], lambda r, rows, e=e: seg_copy(e, r, rows).start())

    last = i * N_EXPERTS + N_EXPERTS - 1
    total = off_ref[last] + len_ref[last]

    def clear(p, carry):
        og_scr[pl.ds(pl.multiple_of(p * pc, pc), pc), :] = jnp.zeros((pc, og_scr.shape[1]), BF16)
        return carry
    lax.fori_loop(total // pc, n_rows // pc, clear, 0)

    for e in range(N_EXPERTS):
        _for_pieces(len_ref[i * N_EXPERTS + e], lambda r, rows, e=e: seg_copy(e, r, rows).wait())

    out_ref[...] = jnp.zeros_like(out_ref)

    def scatter(ci, carry):
        r0 = pl.multiple_of(ci * gr, gr)
        rid = (lax.broadcasted_iota(jnp.int32, (gr, tk), 0) + r0).astype(F32)
        pg = (jnp.where(rid == slot_ref[0:1, :], slot_ref[2:3, :], 0.0)
              + jnp.where(rid == slot_ref[1:2, :], slot_ref[3:4, :], 0.0))
        out_ref[...] = out_ref[...] + _dot_tn(pg.astype(BF16), og_scr[pl.ds(r0, gr), :])
        return carry
    lax.fori_loop(0, (total + gr - 1) // gr, scatter, 0)
    h3 = h_ref[...] + mod_ref[5:6, :] * out_ref[...]
    out_ref[...] = h3 * lax.rsqrt(jnp.mean(h3 * h3, axis=-1, keepdims=True) + EPS) * fg_ref[...]


def _moe_block_table(seg, n_blocks):
    ends = (seg[-1, 2, :, 0] + seg[-1, 1, :, 0]).astype(jnp.int32)
    nblk = (ends + MOE_BLOCK - 1) // MOE_BLOCK
    cum = jnp.cumsum(nblk)
    n_valid = cum[-1]
    g = jnp.minimum(jnp.arange(n_blocks, dtype=jnp.int32), n_valid - 1)
    eid = jnp.sum((g[:, None] >= cum[None, :]).astype(jnp.int32), axis=1)
    blk = g - (cum - nblk)[eid]
    return eid, blk, n_valid.reshape(1)


def _moe(o0, o1, gates, gate_blk, h, modtab, cng, dng, wo, ng, fg, rw, rb, wg, wu, wd, *, tk):
    bsz, seq, d = h.shape
    tpb = seq // tk
    n_tiles = bsz * tpb
    n_tok = bsz * seq
    tile_rows = -(-(2 * tk + N_EXPERTS * MOE_PIECE) // MOE_GATHER_ROWS) * MOE_GATHER_ROWS
    cap = -(-(n_tok + n_tiles * MOE_PIECE) // MOE_BLOCK) * MOE_BLOCK
    n_blocks = -(-(2 * n_tok + n_tiles * N_EXPERTS * MOE_PIECE) // MOE_BLOCK) + N_EXPERTS
    tok = lambda i, *_: (i // tpb, i % tpb, 0)
    mod = lambda i, *_: (i // tpb, 1, 0, 0)

    tok_spec = pl.BlockSpec((None, tk, d), tok)
    h2, slots, seg, xs = pl.pallas_call(
        functools.partial(_moe_route_kernel, tk=tk),
        grid=(n_tiles,),
        in_specs=[tok_spec, tok_spec,
                  pl.BlockSpec((None, tk, d), lambda i: (i // tpb, i % tpb, gate_blk)),
                  tok_spec, pl.BlockSpec((None, None, 6, d), mod),
                  _const_spec((1, 128)), _const_spec((1, 128)), _const_spec((d, d)),
                  _const_spec((1, d)), _const_spec((d, 128)), _const_spec((N_EXPERTS, 1))],
        out_specs=[tok_spec,
                   pl.BlockSpec((None, 8, tk), lambda i: (i, 0, 0)),
                   pl.BlockSpec((None, 3, N_EXPERTS, 128), lambda i: (i, 0, 0, 0)),
                   pl.BlockSpec(memory_space=pl.ANY)],
        out_shape=[jax.ShapeDtypeStruct(h.shape, F32),
                   jax.ShapeDtypeStruct((n_tiles, 8, tk), F32),
                   jax.ShapeDtypeStruct((n_tiles, 3, N_EXPERTS, 128), F32),
                   jax.ShapeDtypeStruct((N_EXPERTS, cap, d), BF16)],
        scratch_shapes=[pltpu.VMEM((tk, d), BF16), pltpu.VMEM((tile_rows, d), BF16), pltpu.VMEM((MOE_PIECE, d), BF16),
                        pltpu.SMEM((N_EXPERTS,), jnp.int32), pltpu.SemaphoreType.DMA(())],
        compiler_params=_params(("arbitrary",), 40 << 20),
        name="l1_moe_route",
    )(o0, o1, gates, h, modtab, cng, dng, wo, ng, _pad_cols(rw, 128), rb)

    eid, blk, n_valid = _moe_block_table(seg, n_blocks)
    x_spec = pl.BlockSpec((None, MOE_BLOCK, d), lambda g, eid, blk, nv: (eid[g], blk[g], 0))
    og = pl.pallas_call(
        functools.partial(_moe_ffn_kernel, n_chunks=11),
        grid_spec=pltpu.PrefetchScalarGridSpec(
            num_scalar_prefetch=3, grid=(n_blocks,),
            in_specs=[x_spec,
                      pl.BlockSpec((None, d, D_FF), lambda g, eid, blk, nv: (eid[g], 0, 0)),
                      pl.BlockSpec((None, d, D_FF), lambda g, eid, blk, nv: (eid[g], 0, 0)),
                      pl.BlockSpec((None, D_FF, d), lambda g, eid, blk, nv: (eid[g], 0, 0))],
            out_specs=x_spec),
        out_shape=jax.ShapeDtypeStruct((N_EXPERTS, cap, d), BF16),
        compiler_params=_params(("arbitrary",), 52 << 20),
        name="l1_moe_experts",
    )(eid, blk, n_valid, xs, wg, wu, wd)

    tab = lambda k: seg[:, k, :, 0].astype(jnp.int32).reshape(-1)
    return pl.pallas_call(
        functools.partial(_moe_combine_kernel, tk=tk),
        grid_spec=pltpu.PrefetchScalarGridSpec(
            num_scalar_prefetch=3, grid=(n_tiles,),
            in_specs=[pl.BlockSpec((None, tk, d), tok), pl.BlockSpec((None, None, 6, d), mod),
                      pl.BlockSpec((1, d), lambda i, *_: (0, 0)),
                      pl.BlockSpec((None, 8, tk), lambda i, *_: (i, 0, 0)),
                      pl.BlockSpec(memory_space=pl.ANY)],
            out_specs=pl.BlockSpec((None, tk, d), tok),
            scratch_shapes=[pltpu.VMEM((tile_rows, d), BF16), pltpu.SemaphoreType.DMA(())]),
        out_shape=jax.ShapeDtypeStruct(h.shape, F32),
        compiler_params=_params(("arbitrary",), 40 << 20),
        name="l1_moe_combine",
    )(tab(2), tab(1), tab(0), h2, modtab, fg, slots, og)


def _block_diag_gate(gate_w):
    w = gate_w.reshape(2, 2, 2, 4, RG_BLOCK, RG_BLOCK)
    eye = jnp.eye(4, dtype=gate_w.dtype)
    return jnp.einsum('dghbij,bc->dghbicj', w, eye).reshape(2, 2, 2, 256, 256)


def _pad_cols(w, n):
    return jnp.pad(w, ((0, 0), (0, n - w.shape[1])))


def _layer0(ctx, x, modtab, norm_mix_g, norm_ffn_g, e_w_in, e_w_out, e_a_conv_w, e_a_conv_b, e_a_gate_w, e_a_gate_b,
            e_a_lambda, e_b_conv_w, e_b_a_log, e_b_dt_bias, e_b_norm_g, e_ffn_w_gate, e_ffn_w_up, e_ffn_w_down,
            *, tm, tt):
    bsz, ctx_len, d = ctx.shape
    w_in, w_tail = e_w_in.astype(BF16), _pad_cols(e_w_in[:, E_IN_MAIN:], 128).astype(BF16)
    gpar = jnp.zeros((2, 128), F32)
    gpar = gpar.at[0, 2 * DN_HEADS:4 * DN_HEADS].set(e_b_a_log.reshape(-1))
    gpar = gpar.at[1, 2 * DN_HEADS:4 * DN_HEADS].set(e_b_dt_bias.reshape(-1))
    ua, gay, q, k, v, sz, gb = _inproj0(ctx, x, modtab, norm_mix_g.reshape(1, d), w_in, w_tail, e_a_conv_w,
                                        e_a_conv_b.reshape(1, -1), e_b_conv_w, gpar, tm=tm)
    wg = _block_diag_gate(e_a_gate_w).astype(BF16)
    hf, hb = _rglru(jnp.transpose(ua, (1, 0, 2)), wg, e_a_gate_b.reshape(4, RG_WIDTH), e_a_lambda,
                    tt=tt, ctx_len=ctx_len)
    ha = jnp.transpose(hf + hb, (1, 0, 2))
    o0, o1 = _delta(q, k, v, gb, ctx_len=ctx_len, rows=SCAN_ROWS)
    return _l0_tail(ha, gay, o0, o1, sz, ctx, x, modtab, e_b_norm_g.reshape(1, -1), e_w_out.astype(BF16),
                    norm_ffn_g.reshape(1, d), e_ffn_w_gate.astype(BF16), e_ffn_w_up.astype(BF16),
                    e_ffn_w_down.astype(BF16), tm=tm)


def _layer1(hc, hl, modtab, norm_mix_g, norm_ffn_g, final_norm_g, o_w_in, o_w_out, o_lb_logits, o_c_norm_g,
            o_d_gate_w2, o_d_gate_b2, o_d_norm_g, o_router_w, o_router_b, o_moe_w_gate, o_moe_w_up, o_moe_w_down,
            *, tm, tk, layer):
    bsz, seq, d = hl.shape
    ctx_len = hc.shape[1]
    rows = seq // GRID_W
    hl = hl.reshape(bsz, rows, GRID_W, d).swapaxes(1, 2).reshape(bsz, seq, d)
    w_in, w_tail = o_w_in.astype(BF16), _pad_cols(o_w_in[:, O_IN_MAIN:], 128).astype(BF16)
    wlr = jnp.zeros((128, SEG), F32)
    wlr = wlr.at[0:GLA_RANK, 0:256].set(o_d_gate_w2[0]).at[GLA_RANK:2 * GLA_RANK, 256:512].set(o_d_gate_w2[1])
    proj = functools.partial(_inproj1, g=norm_mix_g.reshape(1, d), w=w_in, w_tail=w_tail, lbl=o_lb_logits, wlr=wlr,
                             b2=o_d_gate_b2.reshape(1, SEG), layer=layer)
    p1c = proj(hc, 0, ctx_len, modtab, 0, tm=tm)
    p1l = proj(hl, 0, seq, modtab, 1, tm=2 * tm)
    sh0, sg0 = _mix1_ctx(p1c, rows=SCAN_ROWS)
    o0, o1 = _mix1_lat(p1l, sh0, sg0, rows=SCAN_ROWS)
    return _moe(o0, o1, p1l[S_CG[0]], S_CG[1] * SEG // d, hl, modtab, o_c_norm_g.reshape(1, -1),
                o_d_norm_g.reshape(1, -1), o_w_out.astype(BF16), norm_ffn_g.reshape(1, d),
                final_norm_g.reshape(1, d), o_router_w, o_router_b.reshape(N_EXPERTS, 1),
                o_moe_w_gate.astype(BF16), o_moe_w_up.astype(BF16), o_moe_w_down.astype(BF16), tk=tk)


def kernel(x, c, ctx, c_ctx, ada_w, ada_b, norm_mix_g, norm_ffn_g, final_norm_g, e_w_in, e_w_out, e_a_conv_w, e_a_conv_b, e_a_gate_w, e_a_gate_b, e_a_lambda, e_b_conv_w, e_b_a_log, e_b_dt_bias, e_b_norm_g, e_ffn_w_gate, e_ffn_w_up, e_ffn_w_down, o_w_in, o_w_out, o_lb_logits, o_c_norm_g, o_d_gate_w2, o_d_gate_b2, o_d_norm_g, o_router_w, o_router_b, o_moe_w_gate, o_moe_w_up, o_moe_w_down):
    bsz, seq, d = x.shape
    ctx_len = ctx.shape[1]
    assert bsz == 8 and d == D_MODEL and ada_w.shape[0] == 2
    tm = min(256, ctx_len)
    tt = min(128, ctx_len)
    tk = min(512, seq)
    assert ctx_len % tm == 0 and seq % tm == 0 and ctx_len % SCAN_ROWS == 0 and seq % SCAN_ROWS == 0
    assert seq % GRID_W == 0 and seq % tk == 0

    mods = _ada(c, c_ctx, ada_w, ada_b)
    hc, hl = _layer0(ctx, x, _modtab(mods[0], bsz), norm_mix_g[0], norm_ffn_g[0], e_w_in[0], e_w_out[0],
                     e_a_conv_w[0], e_a_conv_b[0], e_a_gate_w[0], e_a_gate_b[0], e_a_lambda[0], e_b_conv_w[0],
                     e_b_a_log[0], e_b_dt_bias[0], e_b_norm_g[0], e_ffn_w_gate[0], e_ffn_w_up[0], e_ffn_w_down[0],
                     tm=tm, tt=tt)
    out_cm = _layer1(hc, hl, _modtab(mods[1], bsz), norm_mix_g[1], norm_ffn_g[1], final_norm_g, o_w_in[0],
                     o_w_out[0], o_lb_logits, o_c_norm_g[0], o_d_gate_w2[0], o_d_gate_b2[0], o_d_norm_g[0],
                     o_router_w[0], o_router_b[0], o_moe_w_gate[0], o_moe_w_up[0], o_moe_w_down[0],
                     tm=tm, tk=tk, layer=1)
    rows = seq // GRID_W
    return out_cm.reshape(bsz, GRID_W, rows, d).swapaxes(1, 2).reshape(bsz, seq, d)
```

```python
import functools

import jax
import jax.numpy as jnp
from jax import lax
from jax.experimental import pallas as pl
from jax.experimental.pallas import tpu as pltpu

F32 = jnp.float32
BF16 = jnp.bfloat16
HI = lax.Precision.HIGHEST

EPS = 1e-6
D_MODEL = 1024
GRID_W = 64
CONV_K = 4
RG_WIDTH = 512
RG_BLOCK = 64
RG_C = 8.0
DN_HEADS = 4
DN_D = 128
DN_CHUNK = 64
HG_HEADS = 4
HG_D = 128
GLA_HEADS = 4
GLA_DK = 64
GLA_DV = 128
GLA_RANK = 16
GLA_GATE_NORM = 16.0
MIX1_CHUNK = 64
SCAN_ROWS = 256
D_FF = 2816
N_EXPERTS = 8

E_IN_MAIN = 3072
E_IN_PAD = E_IN_MAIN + 128
O_IN_MAIN = 4096
O_IN_PAD = O_IN_MAIN + 128
SEG = 512
S_HQ, S_HV, S_GV, S_GQK, S_HK0, S_HK1, S_CG, S_DG = [(0, i) for i in range(8)]
N_SHARED = 4
S_HLF0, S_HLF1, S_GLD = [(1, i) for i in range(3)]
N_SEG = (8, 3)
P1_DTYPES = (BF16, F32)

V7X_VMEM_BYTES = 64 * 1024 * 1024
VMEM_HEADROOM_BYTES = 8 * 1024 * 1024
MOE_PIECE = 16
MOE_SEG_PIECES = (64, MOE_PIECE)
MOE_BLOCK = 512
MOE_GATHER_ROWS = 256


def _vmem(nbytes):
    return int(min(V7X_VMEM_BYTES - VMEM_HEADROOM_BYTES, nbytes))


def _params(sem, vmem_bytes):
    return pltpu.CompilerParams(dimension_semantics=sem, vmem_limit_bytes=_vmem(vmem_bytes))


def _sigmoid(x):
    return jax.nn.sigmoid(x)


def _sigmoid_tanh(x):
    return 0.5 * jnp.tanh(0.5 * x) + 0.5


def _silu(x):
    return x * jax.nn.sigmoid(x)


def _softplus(x):
    return jnp.maximum(x, 0.0) + jnp.log1p(jnp.exp(-jnp.abs(x)))


def _gelu_tanh(x):
    return 0.5 * x * (1.0 + jnp.tanh(0.7978845608028654 * (x + 0.044715 * (x * x * x))))


def _normmod(x, g, shift, scale):
    y = x * lax.rsqrt(jnp.mean(x * x, axis=-1, keepdims=True) + EPS)
    return (y * g) * (1.0 + scale) + shift


def _dot(a, b):
    return jnp.dot(a, b, preferred_element_type=F32)


def _dot_nt(a, b):
    return lax.dot_general(a, b, (((1,), (1,)), ((), ())), preferred_element_type=F32)


def _dot_tn(a, b):
    return lax.dot_general(a, b, (((0,), (0,)), ((), ())), preferred_element_type=F32)


def _dot_hi(a, b):
    return jnp.dot(a, b, precision=HI, preferred_element_type=F32)


def _split3(x):
    hi = x.astype(BF16)
    r1 = x - hi.astype(F32)
    mid = r1.astype(BF16)
    return hi, mid, (r1 - mid.astype(F32)).astype(BF16)


def _dot_hilo(a, b):
    ah = a.astype(BF16)
    al = (a - ah.astype(F32)).astype(BF16)
    bh = b.astype(BF16)
    bl = (b - bh.astype(F32)).astype(BF16)
    return _dot(ah, bh) + (_dot(ah, bl) + _dot(al, bh))


def _mask_dot(mask, x):
    mb = mask.astype(BF16)
    hi, mid, lo = _split3(x)
    return _dot(mb, hi) + (_dot(mb, mid) + _dot(mb, lo))


def _mask_dot_tn(x, mask):
    mb = mask.astype(BF16)
    hi, mid, lo = _split3(x)
    return _dot_tn(hi, mb) + (_dot_tn(mid, mb) + _dot_tn(lo, mb))


def _const_spec(shape):
    nd = len(shape)
    return pl.BlockSpec(shape, lambda *_: (0,) * nd, pipeline_mode=pl.Buffered(1))


def _scan_masks(c, d):
    row = lax.broadcasted_iota(jnp.int32, (c, c), 0)
    col = lax.broadcasted_iota(jnp.int32, (c, c), 1)
    dlt = row - col if d == 0 else col - row
    return dlt >= 0, dlt > 0, dlt <= 0, row == col


def _ada_kernel(cv_ref, w_ref, b_ref, o_ref):
    s = _silu(cv_ref[...]).astype(BF16)
    o_ref[...] = _dot(s, w_ref[...].astype(BF16)) + b_ref[...]


def _ada(c, c_ctx, ada_w, ada_b):
    depth, d, n6 = ada_w.shape
    bsz = c.shape[0]
    rows = 16
    cv = jnp.zeros((rows, d), F32).at[:bsz].set(c).at[bsz].set(c_ctx)
    tn = 1536
    return pl.pallas_call(
        _ada_kernel,
        grid=(depth, n6 // tn),
        in_specs=[pl.BlockSpec((rows, d), lambda l, j: (0, 0)),
                  pl.BlockSpec((None, d, tn), lambda l, j: (l, 0, j)),
                  pl.BlockSpec((None, 1, tn), lambda l, j: (l, 0, j))],
        out_specs=pl.BlockSpec((None, rows, tn), lambda l, j: (l, 0, j)),
        out_shape=jax.ShapeDtypeStruct((depth, rows, n6), F32),
        compiler_params=_params(("arbitrary", "arbitrary"), 32 << 20),
        name="ada_mod",
    )(cv, ada_w, ada_b.reshape(depth, 1, n6))


def _modtab(mods_l, bsz):
    m = mods_l.reshape(mods_l.shape[0], 6, D_MODEL)
    lat = m[:bsz]
    ctx = jnp.broadcast_to(m[bsz][None], (bsz, 6, D_MODEL))
    return jnp.stack([ctx, lat], axis=1)


def _inproj0_kernel(cp_ref, cm_ref, cn_ref, xp_ref, xm_ref, xn_ref, mod_ref, g_ref, w_ref, wt_ref, acw_ref, acb_ref,
                    bcw_ref, gpar_ref, ua_ref, gay_ref, q_ref, k_ref, v_ref, sz_ref, gb_ref, u_scr,
                    *, tm, ctx_tiles, n_tiles):
    t = pl.program_id(1)
    pick = lambda c_ref, x_ref: jnp.where(t < ctx_tiles, c_ref[...], x_ref[...])
    x = jnp.concatenate([pick(cp_ref, xp_ref), pick(cm_ref, xm_ref), pick(cn_ref, xn_ref)], axis=0)
    xm = _normmod(x, g_ref[...], mod_ref[0:1, :], mod_ref[1:2, :]).astype(BF16)
    seg_first = jnp.logical_or(t == 0, t == ctx_tiles)
    seg_last = jnp.logical_or(t == ctx_tiles - 1, t == n_tiles - 1)

    def project(c0, width, conv_input):
        u_scr[:, c0:c0 + width] = _dot(xm, wt_ref[...] if c0 == E_IN_MAIN else w_ref[:, c0:c0 + width])
        if conv_input:
            u_scr[0:8, c0:c0 + width] = jnp.where(seg_first, 0.0, u_scr[0:8, c0:c0 + width])
            u_scr[tm + 8:tm + 16, c0:c0 + width] = jnp.where(seg_last, 0.0, u_scr[tm + 8:tm + 16, c0:c0 + width])

    def conv(c0, width, w_ref_, w0):
        acc = u_scr[6:6 + tm, c0:c0 + width] * w_ref_[0:1, w0:w0 + width]
        for j in range(1, CONV_K):
            acc = acc + u_scr[6 + j:6 + j + tm, c0:c0 + width] * w_ref_[j:j + 1, w0:w0 + width]
        return acc

    project(0, RG_WIDTH, True)
    for grp in range(RG_WIDTH // 128):
        c0 = grp * 128
        ua_ref[:, c0:c0 + 128] = conv(c0, 128, acw_ref, c0) + acb_ref[0:1, c0:c0 + 128]
    project(512, 512, False)
    gay_ref[...] = _gelu_tanh(u_scr[8:8 + tm, 512:1024]).astype(BF16)

    for grp in range(3 * DN_HEADS):
        c0 = grp * 128
        if grp % DN_HEADS == 0:
            project(1024 + c0, DN_HEADS * DN_D, True)
        y = _silu(conv(1024 + c0, 128, bcw_ref, c0))
        if grp < 2 * DN_HEADS:
            y = y * lax.rsqrt(jnp.sum(y * y, axis=-1, keepdims=True) + EPS)
        if grp < DN_HEADS:
            q_ref[:, c0:c0 + 128] = (y * (DN_D ** -0.5)).astype(BF16)
        elif grp < 2 * DN_HEADS:
            k_ref[:, c0 - 512:c0 - 384] = y.astype(BF16)
        else:
            v_ref[:, c0 - 1024:c0 - 896] = y.astype(BF16)
    project(2560, 512, False)
    sz_ref[...] = _silu(u_scr[8:8 + tm, 2560:3072]).astype(BF16)

    project(3072, 128, False)
    xg = u_scr[8:8 + tm, 3072:3200]
    lane = lax.broadcasted_iota(jnp.int32, xg.shape, 1)
    g = -jnp.exp(gpar_ref[0:1, :]) * _softplus(xg + gpar_ref[1:2, :])
    gb_ref[...] = jnp.where(lane < 2 * DN_HEADS, _sigmoid(xg), g)


def _row_specs(tm, d, ctx_tiles, ctx_len, seq, halo):
    tb = tm // 8

    def specs(n_rows, tile_of):
        main = pl.BlockSpec((None, tm, d), lambda b, t: (b, jnp.clip(tile_of(t), 0, n_rows // tm - 1), 0))
        if not halo:
            return [main]
        prev = pl.BlockSpec((None, 8, d), lambda b, t: (b, jnp.clip(tile_of(t) * tb - 1, 0, n_rows // 8 - 1), 0))
        nxt = pl.BlockSpec((None, 8, d), lambda b, t: (b, jnp.clip((tile_of(t) + 1) * tb, 0, n_rows // 8 - 1), 0))
        return [prev, main, nxt]

    return specs(ctx_len, lambda t: t) + specs(seq, lambda t: t - ctx_tiles)


def _inproj0(ctx, x, modtab, g, w, w_tail, acw, acb, bcw, gpar, *, tm):
    bsz, ctx_len, d = ctx.shape
    seq = x.shape[1]
    t_all = ctx_len + seq
    n_tiles = t_all // tm
    ctx_tiles = ctx_len // tm
    kern = functools.partial(_inproj0_kernel, tm=tm, ctx_tiles=ctx_tiles, n_tiles=n_tiles)
    tok = lambda w, dt=BF16: jax.ShapeDtypeStruct((bsz, t_all, w), dt)
    tok_spec = lambda w: pl.BlockSpec((None, tm, w), lambda b, t: (b, t, 0))
    return pl.pallas_call(
        kern,
        grid=(bsz, n_tiles),
        in_specs=_row_specs(tm, d, ctx_tiles, ctx_len, seq, True) + [
            pl.BlockSpec((None, None, 6, d), lambda b, t: (b, jnp.where(t >= ctx_tiles, 1, 0), 0, 0)),
            _const_spec((1, d)),
            _const_spec(w.shape), _const_spec((d, 128)),
            _const_spec((CONV_K, RG_WIDTH)),
            _const_spec((1, RG_WIDTH)),
            _const_spec((CONV_K, 3 * DN_HEADS * DN_D)),
            _const_spec((2, 128)),
        ],
        out_specs=[tok_spec(512), tok_spec(512), tok_spec(512), tok_spec(512), tok_spec(512), tok_spec(512),
                   tok_spec(128)],
        out_shape=[tok(512, F32), tok(512), tok(512), tok(512), tok(512), tok(512), tok(128, F32)],
        scratch_shapes=[pltpu.VMEM((tm + 16, E_IN_PAD), F32)],
        compiler_params=_params(("arbitrary", "arbitrary"), 40 << 20),
        name="l0_inproj",
    )(ctx, ctx, ctx, x, x, x, modtab, g, w, w_tail, acw, acb, bcw, gpar)


def _rglru_kernel(uf_ref, ub_ref, wg_ref, gbias_ref, lam_ref, hf_ref, hb_ref,
                  af_scr, xf_scr, ab_scr, xb_scr, stg_scr, of_scr, ob_scr, h_scr, *, tt, bsz):
    s = pl.program_id(0)
    n_slab = RG_WIDTH // 128

    @pl.when(s == 0)
    def _():
        h_scr[...] = jnp.zeros_like(h_scr)

    def gates(u_ref, d, a_scr, x_scr):
        for b in range(bsz):
            for j in range(n_slab):
                stg_scr[j, pl.ds(b, tt, stride=bsz), :] = u_ref[b, :, j * 128:(j + 1) * 128]
        x = jnp.concatenate([stg_scr[j] for j in range(n_slab)], axis=1)
        xb = x.astype(BF16)
        for half in range(2):
            c0 = half * 256
            xh = xb[:, c0:c0 + 256]
            r = _sigmoid_tanh(_dot(xh, wg_ref[d, 0, half]) + gbias_ref[2 * d:2 * d + 1, c0:c0 + 256])
            i = _sigmoid_tanh(_dot(xh, wg_ref[d, 1, half]) + gbias_ref[2 * d + 1:2 * d + 2, c0:c0 + 256])
            log_a = (-RG_C) * r * _softplus(-lam_ref[d:d + 1, c0:c0 + 256])
            a = jnp.exp(log_a)
            mult = jnp.sqrt(-jnp.tanh(log_a) * (a * a + 1.0))
            xin = mult * (i * x[:, c0:c0 + 256])
            a_scr[:, :, c0:c0 + 256] = a.reshape(tt, bsz, 256)
            x_scr[:, :, c0:c0 + 256] = xin.reshape(tt, bsz, 256)

    gates(uf_ref, 0, af_scr, xf_scr)
    gates(ub_ref, 1, ab_scr, xb_scr)

    def put(o_scr, t, h):
        rows = pl.ds(pl.multiple_of(t * bsz, bsz), bsz)
        for j in range(n_slab):
            o_scr[j, rows, :] = h[:, j * 128:(j + 1) * 128]

    def step(t, carry):
        hf, hb = carry
        hf = af_scr[t] * hf + xf_scr[t]
        put(of_scr, t, hf)
        tb = tt - 1 - t
        hb = ab_scr[tb] * hb + xb_scr[tb]
        put(ob_scr, tb, hb)
        return hf, hb

    hf, hb = lax.fori_loop(0, tt, step, (h_scr[0], h_scr[1]), unroll=8)
    h_scr[0] = hf
    h_scr[1] = hb
    for o_scr, o_ref in ((of_scr, hf_ref), (ob_scr, hb_ref)):
        for b in range(bsz):
            for j in range(n_slab):
                o_ref[b, :, j * 128:(j + 1) * 128] = o_scr[j, pl.ds(b, tt, stride=bsz), :]


def _rglru(ua, wg, gbias, lam, *, tt, ctx_len):
    bsz, t_all, w = ua.shape
    n_steps = t_all // tt
    nc = ctx_len // tt

    def bwd(s):
        return jnp.where(s < nc, nc - 1 - s, n_steps + nc - 1 - s)

    blk = (bsz, tt, w)
    tm_blk = (tt, bsz, w)
    slabs = (w // 128, tt * bsz, 128)
    kern = functools.partial(_rglru_kernel, tt=tt, bsz=bsz)
    return pl.pallas_call(
        kern,
        grid=(n_steps,),
        in_specs=[pl.BlockSpec(blk, lambda s: (0, s, 0)),
                  pl.BlockSpec(blk, lambda s: (0, bwd(s), 0)),
                  _const_spec(wg.shape), _const_spec(gbias.shape), _const_spec(lam.shape)],
        out_specs=[pl.BlockSpec(blk, lambda s: (0, s, 0)),
                   pl.BlockSpec(blk, lambda s: (0, bwd(s), 0))],
        out_shape=[jax.ShapeDtypeStruct(ua.shape, F32)] * 2,
        scratch_shapes=[pltpu.VMEM(tm_blk, F32)] * 4 + [pltpu.VMEM(slabs, F32)] * 3 + [pltpu.VMEM((2, bsz, w), F32)],
        compiler_params=_params(("arbitrary",), 48 << 20),
        name="l0_rglru",
    )(ua, ua, wg, gbias, lam)


def _delta_kernel(qf_ref, kf_ref, vf_ref, gf_ref, qb_ref, kb_ref, vb_ref, gb_ref, of_ref, ob_ref, s_scr, *, n_sub):
    c = DN_CHUNK

    @pl.when(pl.program_id(1) == 0)
    def _():
        s_scr[...] = jnp.zeros_like(s_scr)

    dir_refs = ((qf_ref, kf_ref, vf_ref, gf_ref, of_ref), (qb_ref, kb_ref, vb_ref, gb_ref, ob_ref))
    masks = [_scan_masks(c, d) for d in range(2)]
    eye = jnp.where(masks[0][3], 1.0, 0.0)

    cums = {}
    for d in range(2):
        incl, _, incl_t, _ = masks[d]
        m_incl = jnp.where(incl, 1.0, 0.0)
        m_incl_t = jnp.where(incl_t, 1.0, 0.0)
        for ci in range(n_sub):
            g_all = dir_refs[d][3][ci * c:(ci + 1) * c, :]
            gc_all = _mask_dot(m_incl, g_all)
            gct_all = _mask_dot_tn(g_all, m_incl_t)
            cums[d, ci] = (g_all, gc_all, gct_all)

    chains = []
    for d in range(2):
        q_ref, k_ref, v_ref, _, _ = dir_refs[d]
        incl, strict, _, _ = masks[d]
        last = c - 1 if d == 0 else 0
        for ci in range(n_sub):
            g_all, gc_all, gct_all = cums[d, ci]
            rs = slice(ci * c, (ci + 1) * c)
            for h in range(DN_HEADS):
                hs = slice(h * DN_D, (h + 1) * DN_D)
                lane = 2 * DN_HEADS + d * DN_HEADS + h
                ch = dict(d=d, ci=ci, h=h, rs=rs, hs=hs, incl=incl, strict=strict)
                ch["beta"] = g_all[:, d * DN_HEADS + h:d * DN_HEADS + h + 1]
                gc = jnp.broadcast_to(gc_all[:, lane:lane + 1], (c, DN_D))
                gc_row = jnp.broadcast_to(gct_all[lane:lane + 1, :], (c, c))
                ch["gc"] = gc
                ch["gtot"] = gc[last:last + 1, :]
                ch["decay"] = jnp.where(incl, jnp.exp(jnp.minimum(gc[:, 0:c] - gc_row, 0.0)), 0.0)
                ch["e_gc"] = jnp.exp(gc)
                ch["q"] = q_ref[rs, hs].astype(F32)
                ch["k"] = k_ref[rs, hs].astype(F32)
                ch["v"] = v_ref[rs, hs].astype(F32)
                chains.append(ch)

    for ch in chains:
        ch["kb"] = ch["k"] * ch["beta"]
        qk = _dot_nt(jnp.concatenate([ch["kb"], ch["q"]], axis=0).astype(BF16), ch["k"].astype(BF16))
        ch["neg"] = -jnp.where(ch["strict"], qk[0:c] * ch["decay"], 0.0)
        ch["a_qk"] = (qk[c:2 * c] * ch["decay"]).astype(BF16)
    for ch in chains:
        negb = ch["neg"].astype(BF16)
        ch["t"] = eye + ch["neg"]
        ch["p"] = _dot(negb, negb)
    n_sq = max(1, (c - 1).bit_length() - 1)
    for it in range(n_sq):
        for ch in chains:
            tp = _dot(jnp.concatenate([ch["t"], ch["p"]], axis=0).astype(BF16), ch["p"].astype(BF16))
            ch["t"] = ch["t"] + tp[0:c]
            ch["p"] = tp[c:2 * c]
    for ch in chains:
        rhs = jnp.concatenate([ch["v"] * ch["beta"], ch["kb"] * ch["e_gc"]], axis=1).astype(BF16)
        sol = _dot(ch["t"].astype(BF16), rhs)
        ch["u"] = sol[:, 0:DN_D]
        ch["wq"] = jnp.concatenate([sol[:, DN_D:2 * DN_D], ch["q"] * ch["e_gc"]], axis=0).astype(BF16)
        ch["k_tail"] = (ch["k"] * jnp.exp(ch["gtot"] - ch["gc"])).astype(BF16)

    by_key = {(ch["d"], ch["ci"], ch["h"]): ch for ch in chains}
    for step in range(n_sub):
        live = [by_key[d, step if d == 0 else n_sub - 1 - step, h] for d in range(2) for h in range(DN_HEADS)]
        for ch in live:
            ch["st"] = s_scr[ch["d"], ch["h"]]
            ch["ws"] = _dot(ch["wq"], ch["st"].astype(BF16))
        for ch in live:
            vnb = (ch["u"] - ch["ws"][0:c]).astype(BF16)
            o = ch["ws"][c:2 * c] + _dot(ch["a_qk"], vnb)
            dir_refs[ch["d"]][4][ch["rs"], ch["hs"]] = o.astype(BF16)
            s_scr[ch["d"], ch["h"]] = ch["st"] * jnp.exp(ch["gtot"]) + _dot_tn(ch["k_tail"], vnb)


def _delta(q, k, v, gb, *, ctx_len, rows):
    bsz, t_all, w = q.shape
    n_steps = t_all // rows
    nc = ctx_len // rows

    def bwd(s):
        return jnp.where(s < nc, nc - 1 - s, n_steps + nc - 1 - s)

    fwd_spec = lambda width: pl.BlockSpec((None, rows, width), lambda b, s: (b, s, 0))
    bwd_spec = lambda width: pl.BlockSpec((None, rows, width), lambda b, s: (b, bwd(s), 0))
    return pl.pallas_call(
        functools.partial(_delta_kernel, n_sub=rows // DN_CHUNK),
        grid=(bsz, n_steps),
        in_specs=[fwd_spec(w), fwd_spec(w), fwd_spec(w), fwd_spec(128),
                  bwd_spec(w), bwd_spec(w), bwd_spec(w), bwd_spec(128)],
        out_specs=[fwd_spec(w), bwd_spec(w)],
        out_shape=[jax.ShapeDtypeStruct((bsz, t_all, w), BF16)] * 2,
        scratch_shapes=[pltpu.VMEM((2, DN_HEADS, DN_D, DN_D), F32)],
        compiler_params=_params(("arbitrary", "arbitrary"), 32 << 20),
        name="l0_deltanet",
    )(q, k, v, gb, q, k, v, gb)


def _head_norm(y, g):
    return y * lax.rsqrt(jnp.mean(y * y, axis=-1, keepdims=True) + EPS) * g


def _l0_tail_kernel(hf_ref, hb_ref, gay_ref, o0_ref, o1_ref, sz_ref, ctx_ref, x_ref, mod_ref, ng_ref, wo_ref, g_ref,
                    wg_ref, wu_ref, wd_ref, hc_ref, hl_ref, *, n_chunks, ctx_tiles):
    t = pl.program_id(1)
    tm = hf_ref.shape[0]
    halves = [dict(rs=slice(i * tm // 2, (i + 1) * tm // 2)) for i in range(2)]
    for hv in halves:
        rs = hv["rs"]
        parts = [((hf_ref[rs, :] + hb_ref[rs, :]) * gay_ref[rs, :].astype(F32)).astype(BF16)]
        for hd in range(DN_HEADS):
            lo = hd * DN_D
            ob = o0_ref[rs, lo:lo + DN_D].astype(F32) + o1_ref[rs, lo:lo + DN_D].astype(F32)
            parts.append((_head_norm(ob, ng_ref[...]) * sz_ref[rs, lo:lo + DN_D].astype(F32)).astype(BF16))
        hv["ycat"] = jnp.concatenate(parts, axis=-1)
    for hv in halves:
        h = jnp.where(t < ctx_tiles, ctx_ref[hv["rs"], :], x_ref[hv["rs"], :])
        hv["x"] = h + mod_ref[2:3, :] * _dot(hv["ycat"], wo_ref[...])
    for hv in halves:
        hv["xm"] = _normmod(hv["x"], g_ref[...], mod_ref[3:4, :], mod_ref[4:5, :]).astype(BF16)
        hv["acc"] = jnp.zeros(hv["x"].shape, F32)
    cw = D_FF // n_chunks
    for ci in range(n_chunks):
        c0 = ci * cw
        for hv in halves:
            hv["act"] = (_silu(_dot(hv["xm"], wg_ref[:, c0:c0 + cw])) * _dot(hv["xm"], wu_ref[:, c0:c0 + cw])).astype(BF16)
        for hv in halves:
            hv["acc"] = hv["acc"] + _dot(hv["act"], wd_ref[c0:c0 + cw, :])
    out = jnp.concatenate([hv["x"] + mod_ref[5:6, :] * hv["acc"] for hv in halves], axis=0)

    @pl.when(t < ctx_tiles)
    def _():
        hc_ref[...] = out

    @pl.when(t >= ctx_tiles)
    def _():
        hl_ref[...] = out


def _l0_tail(hf, hb, gay, o0, o1, sz, ctx, x, modtab, ng, wo, g, wg, wu, wd, *, tm):
    bsz, ctx_len, d = ctx.shape
    seq = x.shape[1]
    t_all = ctx_len + seq
    ctx_tiles = ctx_len // tm
    tok = lambda width: pl.BlockSpec((None, tm, width), lambda b, t: (b, t, 0))
    return pl.pallas_call(
        functools.partial(_l0_tail_kernel, n_chunks=2, ctx_tiles=ctx_tiles),
        grid=(bsz, t_all // tm),
        in_specs=[tok(512)] * 6 + _row_specs(tm, d, ctx_tiles, ctx_len, seq, False) + [
                  pl.BlockSpec((None, None, 6, d), lambda b, t: (b, jnp.where(t >= ctx_tiles, 1, 0), 0, 0)),
                  _const_spec((1, DN_D)), _const_spec((d, d)),
                  _const_spec((1, d)), _const_spec((d, D_FF)), _const_spec((d, D_FF)), _const_spec((D_FF, d))],
        out_specs=[pl.BlockSpec((None, tm, d), lambda b, t: (b, jnp.minimum(t, ctx_tiles - 1), 0)),
                   pl.BlockSpec((None, tm, d), lambda b, t: (b, jnp.maximum(t - ctx_tiles, 0), 0))],
        out_shape=[jax.ShapeDtypeStruct((bsz, ctx_len, d), F32), jax.ShapeDtypeStruct((bsz, t_all - ctx_len, d), F32)],
        compiler_params=_params(("arbitrary", "arbitrary"), 48 << 20),
        name="l0_tail",
    )(hf, hb, gay, o0, o1, sz, ctx, x, modtab, ng, wo, g, wg, wu, wd)


def _inproj1_kernel(h_ref, mod_ref, g_ref, w_ref, wt_ref, lbl_ref, wlr_ref, b2_ref, pa_ref, pb_ref, u_scr, *, layer):
    x = h_ref[...]
    xm = _normmod(x, g_ref[...], mod_ref[0:1, :], mod_ref[1:2, :]).astype(BF16)

    lg = lbl_ref[...]
    ex = jnp.exp(lg - jnp.max(lg, axis=0, keepdims=True))
    lbw = ex / jnp.sum(ex, axis=0, keepdims=True)
    lb = jnp.sum(lbw[1:layer + 1], axis=0, keepdims=True)

    def project(c0, width):
        u_scr[:, c0:c0 + width] = _dot(xm, wt_ref[...] if c0 == O_IN_MAIN else w_ref[:, c0:c0 + width])

    def put(seg, off, val):
        ref = (pa_ref, pb_ref)[seg[0]]
        ref[:, seg[1] * SEG + off:seg[1] * SEG + off + val.shape[1]] = val.astype(ref.dtype)

    def groups(fn):
        for grp in range(SEG // 128):
            fn(grp * 128)

    project(0, SEG)
    groups(lambda c0: put(S_HQ, c0, _silu(u_scr[:, c0:c0 + 128]) * (HG_D ** -0.5)))
    for dr, (sk, sf) in enumerate(((S_HK0, S_HLF0), (S_HK1, S_HLF1))):
        project(512 + dr * 512, SEG)

        def forget(c0, dr=dr, sk=sk, sf=sf):
            lbg = lb[:, c0:c0 + 128]
            fl = u_scr[:, 512 + dr * 512 + c0:512 + dr * 512 + c0 + 128]
            sg = _sigmoid(fl)
            put(sf, c0, jnp.log(lbg + (1.0 - lbg) * sg))
            put(sk, c0, (1.0 - lbg) * (1.0 - sg))
        groups(forget)
    project(1536, SEG)
    groups(lambda c0: put(S_HV, c0, u_scr[:, 1536 + c0:1536 + c0 + 128]))
    project(2048, SEG)
    groups(lambda c0: put(S_CG, c0, _silu(u_scr[:, 2048 + c0:2048 + c0 + 128])))
    project(2560, SEG)
    put(S_GQK, 0, u_scr[:, 2560:2816] * (GLA_DK ** -0.5))
    put(S_GQK, 256, u_scr[:, 2816:3072])
    project(3072, SEG)
    groups(lambda c0: put(S_GV, c0, u_scr[:, 3072 + c0:3072 + c0 + 128]))
    project(3584, SEG)
    groups(lambda c0: put(S_DG, c0, _silu(u_scr[:, 3584 + c0:3584 + c0 + 128])))
    project(4096, 128)
    lr = u_scr[:, 4096:4224]
    put(S_GLD, 0, -_softplus(-(_dot_hilo(lr, wlr_ref[...]) + b2_ref[...])) * (1.0 / GLA_GATE_NORM))


def _inproj1(h, row0, n_rows, modtab, seg, g, w, w_tail, lbl, wlr, b2, *, tm, layer):
    bsz, _, d = h.shape
    t0 = row0 // tm
    return pl.pallas_call(
        functools.partial(_inproj1_kernel, layer=layer),
        grid=(bsz, n_rows // tm),
        in_specs=[pl.BlockSpec((None, tm, d), lambda b, t: (b, t0 + t, 0)),
                  pl.BlockSpec((None, None, 6, d), lambda b, t: (b, seg, 0, 0)),
                  _const_spec((1, d)), _const_spec(w.shape), _const_spec((d, 128)), _const_spec(lbl.shape),
                  _const_spec((128, SEG)), _const_spec((1, SEG))],
        out_specs=[pl.BlockSpec((None, tm, n * SEG), lambda b, t: (b, t, 0)) for n in N_SEG],
        out_shape=[jax.ShapeDtypeStruct((bsz, n_rows, n * SEG), dt) for n, dt in zip(N_SEG, P1_DTYPES)],
        scratch_shapes=[pltpu.VMEM((tm, O_IN_PAD), F32)],
        compiler_params=_params(("arbitrary", "arbitrary"), 44 << 20),
        name="l1_inproj",
    )(h, modtab, g, w, w_tail, lbl, wlr, b2)


def _gla_stream(d, q_all, k_all, ld_all, v_all, st_ref, o_ref, o_lane0, r0, n_heads, dk, dv, incl, m_incl):
    c = k_all.shape[0]
    mid = c // 2 - 1 if d == 0 else c // 2
    last = c - 1 if d == 0 else 0
    bc = _mask_dot(m_incl, ld_all)
    m = bc[mid:mid + 1]
    btot = bc[last:last + 1]
    kn = k_all.astype(F32) * jnp.exp(m - bc)
    it = dict(d=d, r0=r0, st_ref=st_ref, o_ref=o_ref, o_lane0=o_lane0, n_heads=n_heads, dk=dk, dv=dv, incl=incl,
              c=c, kt=(kn * jnp.exp(btot - m)).astype(BF16), dec=jnp.exp(btot), v=v_all.astype(BF16),
              want_out=q_all is not None)
    if q_all is not None:
        qe = q_all.astype(F32) * jnp.exp(bc)
        it.update(qd=(qe * jnp.exp(-m)).astype(BF16), qe=qe.astype(BF16), knb=kn.astype(BF16))
    return it


def _gla_intra(it):
    dk = it["dk"]
    it["a"] = [jnp.where(it["incl"], _dot_nt(it["qd"][:, hd * dk:(hd + 1) * dk], it["knb"][:, hd * dk:(hd + 1) * dk]),
                         0.0).astype(BF16) for hd in range(it["n_heads"])]


def _gla_advance(it):
    d, dk, dv, c, st_ref = it["d"], it["dk"], it["dv"], it["c"], it["st_ref"]
    sts = [st_ref[d, hd] for hd in range(it["n_heads"])]
    if it["want_out"]:
        for hd in range(it["n_heads"]):
            v = it["v"][:, hd * dv:(hd + 1) * dv]
            o = _dot(it["a"][hd], v) + _dot_nt(it["qe"][:, hd * dk:(hd + 1) * dk], sts[hd].astype(BF16))
            it["o_ref"][it["r0"]:it["r0"] + c, it["o_lane0"] + hd * dv:it["o_lane0"] + (hd + 1) * dv] = o.astype(BF16)
    for hd in range(it["n_heads"]):
        ks = slice(hd * dk, (hd + 1) * dk)
        st_ref[d, hd] = sts[hd] * it["dec"][:, ks] + _dot_tn(it["v"][:, hd * dv:(hd + 1) * dv], it["kt"][:, ks])


def _mix1_body(dirs, n_sub, sh_ref, sg_ref):
    c = MIX1_CHUNK
    gw = GLA_HEADS * GLA_DK
    lane = lambda seg: slice(seg[1] * SEG, (seg[1] + 1) * SEG)
    prepared = {}
    for d, (sh_in, hk_ref, hlf_ref, gld_ref, o_ref) in enumerate(dirs):
        incl = _scan_masks(c, d)[0]
        m_incl = jnp.where(incl, 1.0, 0.0)
        g0 = S_GQK[1] * SEG
        for ci in range(n_sub):
            r0 = ci * c
            rs = slice(r0, r0 + c)
            prepared[d, ci, 0] = _gla_stream(
                d, None if o_ref is None else sh_in[rs, lane(S_HQ)], hk_ref[rs, :], hlf_ref[rs, :],
                sh_in[rs, lane(S_HV)], sh_ref, o_ref, 0, r0, HG_HEADS, HG_D, HG_D, incl, m_incl)
            prepared[d, ci, 1] = _gla_stream(
                d, None if o_ref is None else sh_in[rs, g0:g0 + gw], sh_in[rs, g0 + gw:g0 + 2 * gw],
                gld_ref[rs, d * gw:(d + 1) * gw], sh_in[rs, lane(S_GV)],
                sg_ref, o_ref, HG_HEADS * HG_D, r0, GLA_HEADS, GLA_DK, GLA_DV, incl, m_incl)
    for it in prepared.values():
        if it["want_out"]:
            _gla_intra(it)
    for step in range(n_sub):
        for d in range(2):
            for stream in range(2):
                _gla_advance(prepared[d, step if d == 0 else n_sub - 1 - step, stream])


def _mix1_ctx_kernel(*refs, n_sub):
    fwd, bwd, (sh_ref, sg_ref) = refs[0:4], refs[4:8], refs[8:10]

    @pl.when(pl.program_id(1) == 0)
    def _():
        sh_ref[...] = jnp.zeros_like(sh_ref)
        sg_ref[...] = jnp.zeros_like(sg_ref)

    _mix1_body([tuple(r) + (None,) for r in (fwd, bwd)], n_sub, sh_ref, sg_ref)


def _mix1_lat_kernel(*refs, n_sub):
    fwd, bwd = refs[0:4], refs[4:8]
    sh0_ref, sg0_ref, of_ref, ob_ref, sh_scr, sg_scr = refs[8:14]

    @pl.when(pl.program_id(1) == 0)
    def _():
        sh_scr[...] = sh0_ref[...]
        sg_scr[...] = sg0_ref[...]

    _mix1_body([tuple(fwd) + (of_ref,), tuple(bwd) + (ob_ref,)], n_sub, sh_scr, sg_scr)


def _mix1_specs(p1, rows, n_steps):
    specs, args = [], []
    for d in range(2):
        blk = (lambda b, s: s) if d == 0 else (lambda b, s: n_steps - 1 - s)
        for (arr, sg), n_seg in ((S_HQ, N_SHARED), ((S_HK0, S_HK1)[d], 1), ((S_HLF0, S_HLF1)[d], 1), (S_GLD, 1)):
            assert sg % n_seg == 0
            specs.append(pl.BlockSpec((None, rows, n_seg * SEG),
                                      lambda b, s, blk=blk, sg=sg // n_seg: (b, blk(b, s), sg)))
            args.append(p1[arr])
    return specs, args


_SH_SHAPE = (2, HG_HEADS, HG_D, HG_D)
_SG_SHAPE = (2, GLA_HEADS, GLA_DV, GLA_DK)


def _mix1_ctx(p1c, *, rows):
    bsz, ctx_len, _ = p1c[0].shape
    n_steps = ctx_len // rows
    specs, args = _mix1_specs(p1c, rows, n_steps)
    state = lambda shape: pl.BlockSpec((None,) + shape, lambda b, s: (b, 0, 0, 0, 0))
    return pl.pallas_call(
        functools.partial(_mix1_ctx_kernel, n_sub=rows // MIX1_CHUNK),
        grid=(bsz, n_steps),
        in_specs=specs,
        out_specs=[state(_SH_SHAPE), state(_SG_SHAPE)],
        out_shape=[jax.ShapeDtypeStruct((bsz,) + _SH_SHAPE, F32), jax.ShapeDtypeStruct((bsz,) + _SG_SHAPE, F32)],
        compiler_params=_params(("arbitrary", "arbitrary"), 32 << 20),
        name="l1_ctx_state",
    )(*args)


def _mix1_lat(p1l, sh0, sg0, *, rows):
    bsz, seq, _ = p1l[0].shape
    n_steps = seq // rows
    specs, args = _mix1_specs(p1l, rows, n_steps)
    state = lambda shape: pl.BlockSpec((None,) + shape, lambda b, s: (b, 0, 0, 0, 0))
    ow = HG_HEADS * HG_D + GLA_HEADS * GLA_DV
    return pl.pallas_call(
        functools.partial(_mix1_lat_kernel, n_sub=rows // MIX1_CHUNK),
        grid=(bsz, n_steps),
        in_specs=specs + [state(_SH_SHAPE), state(_SG_SHAPE)],
        out_specs=[pl.BlockSpec((None, rows, ow), lambda b, s: (b, s, 0)),
                   pl.BlockSpec((None, rows, ow), lambda b, s: (b, n_steps - 1 - s, 0))],
        out_shape=[jax.ShapeDtypeStruct((bsz, seq, ow), BF16)] * 2,
        scratch_shapes=[pltpu.VMEM(_SH_SHAPE, F32), pltpu.VMEM(_SG_SHAPE, F32)],
        compiler_params=_params(("arbitrary", "arbitrary"), 32 << 20),
        name="l1_scan",
    )(*args, sh0, sg0)


def _outproj1(o0_ref, o1_ref, gate_ref, h_ref, mod_ref, cng_ref, dng_ref, w_ref):
    parts = []
    for hd in range(HG_HEADS + GLA_HEADS):
        lo = hd * 128
        y = o0_ref[:, lo:lo + 128].astype(F32) + o1_ref[:, lo:lo + 128].astype(F32)
        ng = cng_ref[...] if hd < HG_HEADS else dng_ref[...]
        parts.append((_head_norm(y, ng) * gate_ref[:, lo:lo + 128].astype(F32)).astype(BF16))
    return h_ref[...] + mod_ref[2:3, :] * _dot(jnp.concatenate(parts, axis=-1), w_ref[...])


def _for_pieces(length, fn):
    done = 0
    for rows in MOE_SEG_PIECES:
        n = (length - done) // rows

        def body(p, carry, rows=rows, done=done):
            fn(done + p * rows, rows)
            return carry
        lax.fori_loop(0, n, body, 0)
        done = done + n * rows


def _moe_route_kernel(o0_ref, o1_ref, gate_ref, h_ref, mod_ref, cng_ref, dng_ref, wo_ref, ng_ref, rw_ref, rb_ref,
                      h2_ref, slot_ref, seg_ref, xs_hbm, xn_scr, xg_scr, zero_scr, base_smem, sem, *, tk):
    i = pl.program_id(0)
    pc = MOE_PIECE
    gr = MOE_GATHER_ROWS

    @pl.when(i == 0)
    def _():
        for e in range(N_EXPERTS):
            base_smem[e] = 0
        zero_scr[...] = jnp.zeros_like(zero_scr)

    h2 = _outproj1(o0_ref, o1_ref, gate_ref, h_ref, mod_ref, cng_ref, dng_ref, wo_ref)
    h2_ref[...] = h2
    xm = _normmod(h2, ng_ref[...], mod_ref[3:4, :], mod_ref[4:5, :])
    xn_scr[...] = xm.astype(BF16)
    lg = jnp.transpose(_dot_hilo(xm, rw_ref[...]))[0:N_EXPERTS, :] + rb_ref[...]
    eidx = lax.broadcasted_iota(jnp.int32, lg.shape, 0).astype(F32)
    m1 = jnp.max(lg, axis=0, keepdims=True)
    i1 = jnp.min(jnp.where(lg == m1, eidx, float(N_EXPERTS)), axis=0, keepdims=True)
    lg2 = jnp.where(eidx == i1, -jnp.inf, lg)
    m2 = jnp.max(lg2, axis=0, keepdims=True)
    i2 = jnp.min(jnp.where(lg2 == m2, eidx, float(N_EXPERTS)), axis=0, keepdims=True)
    ex = jnp.exp(m2 - m1)
    p1 = 1.0 / (1.0 + ex)
    sel = jnp.where(eidx == i1, 1.0, 0.0) + jnp.where(eidx == i2, 1.0, 0.0)
    lane = lax.broadcasted_iota(jnp.int32, lg.shape, 1)
    cum = sel
    sh = 1
    while sh < tk:
        cum = cum + jnp.where(lane >= sh, pltpu.roll(cum, sh, 1), 0.0)
        sh *= 2
    padded = jnp.floor((cum[:, tk - 1:tk] + (pc - 1.0)) * (1.0 / pc)) * pc
    padded = jnp.broadcast_to(padded, (N_EXPERTS, 128))
    er = lax.broadcasted_iota(jnp.int32, (N_EXPERTS, N_EXPERTS), 0)
    ec = lax.broadcasted_iota(jnp.int32, (N_EXPERTS, N_EXPERTS), 1)
    off = _dot_hi(jnp.where(er > ec, 1.0, 0.0), padded)
    slot = off[:, 0:1] + cum - 1.0
    slot_a = jnp.sum(jnp.where(eidx == i1, slot, 0.0), axis=0, keepdims=True)
    slot_b = jnp.sum(jnp.where(eidx == i2, slot, 0.0), axis=0, keepdims=True)
    slot_ref[...] = jnp.concatenate([slot_a, slot_b, p1, ex * p1, jnp.zeros((4, tk), F32)], axis=0)

    total = jnp.max(off[N_EXPERTS - 1:N_EXPERTS, :] + padded[N_EXPERTS - 1:N_EXPERTS, :]).astype(jnp.int32)

    def gather(ci, carry):
        r0 = pl.multiple_of(ci * gr, gr)
        rid = (lax.broadcasted_iota(jnp.int32, (gr, tk), 0) + r0).astype(F32)
        p = jnp.where(rid == slot_a, 1.0, 0.0) + jnp.where(rid == slot_b, 1.0, 0.0)
        xg_scr[pl.ds(r0, gr), :] = _dot(p.astype(BF16), xn_scr[...]).astype(BF16)
        return carry
    lax.fori_loop(0, (total + gr - 1) // gr, gather, 0)

    erow = lax.broadcasted_iota(jnp.int32, (N_EXPERTS, 128), 0)
    base_vec = jnp.zeros((N_EXPERTS, 128), F32)
    segs = []
    for e in range(N_EXPERTS):
        off_e = jnp.max(off[e:e + 1, :]).astype(jnp.int32)
        len_e = jnp.max(padded[e:e + 1, :]).astype(jnp.int32)
        base_e = base_smem[e]
        base_vec = jnp.where(erow == e, base_e.astype(F32), base_vec)
        segs.append((e, off_e, len_e, base_e))
    seg_ref[0] = off
    seg_ref[1] = padded
    seg_ref[2] = base_vec

    def seg_copy(e, off_e, base_e, r, rows):
        return pltpu.make_async_copy(
            xg_scr.at[pl.ds(pl.multiple_of(off_e + r, pc), rows), :],
            xs_hbm.at[e, pl.ds(pl.multiple_of(base_e + r, pc), rows), :], sem)

    for e, off_e, len_e, base_e in segs:
        _for_pieces(len_e, lambda r, rows, e=e, off_e=off_e, base_e=base_e: seg_copy(e, off_e, base_e, r, rows).start())
    for e, off_e, len_e, base_e in segs:
        _for_pieces(len_e, lambda r, rows, e=e, off_e=off_e, base_e=base_e: seg_copy(e, off_e, base_e, r, rows).wait())
        base_smem[e] = base_e + len_e

    @pl.when(i == pl.num_programs(0) - 1)
    def _():
        def tail_copy(e, p):
            end = base_smem[e]
            return pltpu.make_async_copy(zero_scr, xs_hbm.at[e, pl.ds(pl.multiple_of(end + p * pc, pc), pc), :], sem)

        def n_tail(e):
            rem = lax.rem(base_smem[e], MOE_BLOCK)
            return jnp.where(rem == 0, 0, MOE_BLOCK - rem) // pc

        for e in range(N_EXPERTS):
            def start(p, carry, e=e):
                tail_copy(e, p).start()
                return carry
            lax.fori_loop(0, n_tail(e), start, 0)
        for e in range(N_EXPERTS):
            def wait(p, carry, e=e):
                tail_copy(e, p).wait()
                return carry
            lax.fori_loop(0, n_tail(e), wait, 0)


def _moe_ffn_kernel(eid_ref, blk_ref, nv_ref, x_ref, wg_ref, wu_ref, wd_ref, o_ref, *, n_chunks):
    del eid_ref, blk_ref

    @pl.when(pl.program_id(0) < nv_ref[0])
    def _():
        x = x_ref[...]
        cw = D_FF // n_chunks
        acc = jnp.zeros(x.shape, F32)
        for ci in range(n_chunks):
            c0 = ci * cw
            act = (_silu(_dot(x, wg_ref[:, c0:c0 + cw])) * _dot(x, wu_ref[:, c0:c0 + cw])).astype(BF16)
            acc = acc + _dot(act, wd_ref[c0:c0 + cw, :])
        o_ref[...] = acc.astype(BF16)


def _moe_combine_kernel(base_ref, len_ref, off_ref, h_ref, mod_ref, fg_ref, slot_ref, og_hbm, out_ref,
                        og_scr, sem, *, tk):
    i = pl.program_id(0)
    pc = MOE_PIECE
    gr = MOE_GATHER_ROWS
    n_rows = og_scr.shape[0]

    def seg_copy(e, r, rows):
        return pltpu.make_async_copy(
            og_hbm.at[e, pl.ds(pl.multiple_of(base_ref[i * N_EXPERTS + e] + r, pc), rows), :],
            og_scr.at[pl.ds(pl.multiple_of(off_ref[i * N_EXPERTS + e] + r, pc), rows), :], sem)

    for e in range(N_EXPERTS):
        _for_pieces(len_ref[i * N_EXPERTS + e], lambda r, rows, e=e: seg_copy(e, r, rows).start())

    last = i * N_EXPERTS + N_EXPERTS - 1
    total = off_ref[last] + len_ref[last]

    def clear(p, carry):
        og_scr[pl.ds(pl.multiple_of(p * pc, pc), pc), :] = jnp.zeros((pc, og_scr.shape[1]), BF16)
        return carry
    lax.fori_loop(total // pc, n_rows // pc, clear, 0)

    for e in range(N_EXPERTS):
        _for_pieces(len_ref[i * N_EXPERTS + e], lambda r, rows, e=e: seg_copy(e, r, rows).wait())

    out_ref[...] = jnp.zeros_like(out_ref)

    def scatter(ci, carry):
        r0 = pl.multiple_of(ci * gr, gr)
        rid = (lax.broadcasted_iota(jnp.int32, (gr, tk), 0) + r0).astype(F32)
        pg = (jnp.where(rid == slot_ref[0:1, :], slot_ref[2:3, :], 0.0)
              + jnp.where(rid == slot_ref[1:2, :], slot_ref[3:4, :], 0.0))
        out_ref[...] = out_ref[...] + _dot_tn(pg.astype(BF16), og_scr[pl.ds(r0, gr), :])
        return carry
    lax.fori_loop(0, (total + gr - 1) // gr, scatter, 0)
    h3 = h_ref[...] + mod_ref[5:6, :] * out_ref[...]
    out_ref[...] = h3 * lax.rsqrt(jnp.mean(h3 * h3, axis=-1, keepdims=True) + EPS) * fg_ref[...]


def _moe_block_table(seg, n_blocks):
    ends = (seg[-1, 2, :, 0] + seg[-1, 1, :, 0]).astype(jnp.int32)
    nblk = (ends + MOE_BLOCK - 1) // MOE_BLOCK
    cum = jnp.cumsum(nblk)
    n_valid = cum[-1]
    g = jnp.minimum(jnp.arange(n_blocks, dtype=jnp.int32), n_valid - 1)
    eid = jnp.sum((g[:, None] >= cum[None, :]).astype(jnp.int32), axis=1)
    blk = g - (cum - nblk)[eid]
    return eid, blk, n_valid.reshape(1)


def _moe(o0, o1, gates, gate_blk, h, modtab, cng, dng, wo, ng, fg, rw, rb, wg, wu, wd, *, tk):
    bsz, seq, d = h.shape
    tpb = seq // tk
    n_tiles = bsz * tpb
    n_tok = bsz * seq
    tile_rows = -(-(2 * tk + N_EXPERTS * MOE_PIECE) // MOE_GATHER_ROWS) * MOE_GATHER_ROWS
    cap = -(-(n_tok + n_tiles * MOE_PIECE) // MOE_BLOCK) * MOE_BLOCK
    n_blocks = -(-(2 * n_tok + n_tiles * N_EXPERTS * MOE_PIECE) // MOE_BLOCK) + N_EXPERTS
    tok = lambda i, *_: (i // tpb, i % tpb, 0)
    mod = lambda i, *_: (i // tpb, 1, 0, 0)

    tok_spec = pl.BlockSpec((None, tk, d), tok)
    h2, slots, seg, xs = pl.pallas_call(
        functools.partial(_moe_route_kernel, tk=tk),
        grid=(n_tiles,),
        in_specs=[tok_spec, tok_spec,
                  pl.BlockSpec((None, tk, d), lambda i: (i // tpb, i % tpb, gate_blk)),
                  tok_spec, pl.BlockSpec((None, None, 6, d), mod),
                  _const_spec((1, 128)), _const_spec((1, 128)), _const_spec((d, d)),
                  _const_spec((1, d)), _const_spec((d, 128)), _const_spec((N_EXPERTS, 1))],
        out_specs=[tok_spec,
                   pl.BlockSpec((None, 8, tk), lambda i: (i, 0, 0)),
                   pl.BlockSpec((None, 3, N_EXPERTS, 128), lambda i: (i, 0, 0, 0)),
                   pl.BlockSpec(memory_space=pl.ANY)],
        out_shape=[jax.ShapeDtypeStruct(h.shape, F32),
                   jax.ShapeDtypeStruct((n_tiles, 8, tk), F32),
                   jax.ShapeDtypeStruct((n_tiles, 3, N_EXPERTS, 128), F32),
                   jax.ShapeDtypeStruct((N_EXPERTS, cap, d), BF16)],
        scratch_shapes=[pltpu.VMEM((tk, d), BF16), pltpu.VMEM((tile_rows, d), BF16), pltpu.VMEM((MOE_PIECE, d), BF16),
                        pltpu.SMEM((N_EXPERTS,), jnp.int32), pltpu.SemaphoreType.DMA(())],
        compiler_params=_params(("arbitrary",), 40 << 20),
        name="l1_moe_route",
    )(o0, o1, gates, h, modtab, cng, dng, wo, ng, _pad_cols(rw, 128), rb)

    eid, blk, n_valid = _moe_block_table(seg, n_blocks)
    x_spec = pl.BlockSpec((None, MOE_BLOCK, d), lambda g, eid, blk, nv: (eid[g], blk[g], 0))
    og = pl.pallas_call(
        functools.partial(_moe_ffn_kernel, n_chunks=11),
        grid_spec=pltpu.PrefetchScalarGridSpec(
            num_scalar_prefetch=3, grid=(n_blocks,),
            in_specs=[x_spec,
                      pl.BlockSpec((None, d, D_FF), lambda g, eid, blk, nv: (eid[g], 0, 0)),
                      pl.BlockSpec((None, d, D_FF), lambda g, eid, blk, nv: (eid[g], 0, 0)),
                      pl.BlockSpec((None, D_FF, d), lambda g, eid, blk, nv: (eid[g], 0, 0))],
            out_specs=x_spec),
        out_shape=jax.ShapeDtypeStruct((N_EXPERTS, cap, d), BF16),
        compiler_params=_params(("arbitrary",), 52 << 20),
        name="l1_moe_experts",
    )(eid, blk, n_valid, xs, wg, wu, wd)

    tab = lambda k: seg[:, k, :, 0].astype(jnp.int32).reshape(-1)
    return pl.pallas_call(
        functools.partial(_moe_combine_kernel, tk=tk),
        grid_spec=pltpu.PrefetchScalarGridSpec(
            num_scalar_prefetch=3, grid=(n_tiles,),
            in_specs=[pl.BlockSpec((None, tk, d), tok), pl.BlockSpec((None, None, 6, d), mod),
                      pl.BlockSpec((1, d), lambda i, *_: (0, 0)),
                      pl.BlockSpec((None, 8, tk), lambda i, *_: (i, 0, 0)),
                      pl.BlockSpec(memory_space=pl.ANY)],
            out_specs=pl.BlockSpec((None, tk, d), tok),
            scratch_shapes=[pltpu.VMEM((tile_rows, d), BF16), pltpu.SemaphoreType.DMA(())]),
        out_shape=jax.ShapeDtypeStruct(h.shape, F32),
        compiler_params=_params(("arbitrary",), 40 << 20),
        name="l1_moe_combine",
    )(tab(2), tab(1), tab(0), h2, modtab, fg, slots, og)


def _block_diag_gate(gate_w):
    w = gate_w.reshape(2, 2, 2, 4, RG_BLOCK, RG_BLOCK)
    eye = jnp.eye(4, dtype=gate_w.dtype)
    return jnp.einsum('dghbij,bc->dghbicj', w, eye).reshape(2, 2, 2, 256, 256)


def _pad_cols(w, n):
    return jnp.pad(w, ((0, 0), (0, n - w.shape[1])))


def _layer0(ctx, x, modtab, norm_mix_g, norm_ffn_g, e_w_in, e_w_out, e_a_conv_w, e_a_conv_b, e_a_gate_w, e_a_gate_b,
            e_a_lambda, e_b_conv_w, e_b_a_log, e_b_dt_bias, e_b_norm_g, e_ffn_w_gate, e_ffn_w_up, e_ffn_w_down,
            *, tm, tt):
    bsz, ctx_len, d = ctx.shape
    w_in, w_tail = e_w_in.astype(BF16), _pad_cols(e_w_in[:, E_IN_MAIN:], 128).astype(BF16)
    gpar = jnp.zeros((2, 128), F32)
    gpar = gpar.at[0, 2 * DN_HEADS:4 * DN_HEADS].set(e_b_a_log.reshape(-1))
    gpar = gpar.at[1, 2 * DN_HEADS:4 * DN_HEADS].set(e_b_dt_bias.reshape(-1))
    ua, gay, q, k, v, sz, gb = _inproj0(ctx, x, modtab, norm_mix_g.reshape(1, d), w_in, w_tail, e_a_conv_w,
                                        e_a_conv_b.reshape(1, -1), e_b_conv_w, gpar, tm=tm)
    wg = _block_diag_gate(e_a_gate_w).astype(BF16)
    hf, hb = _rglru(ua, wg, e_a_gate_b.reshape(4, RG_WIDTH), e_a_lambda, tt=tt, ctx_len=ctx_len)
    o0, o1 = _delta(q, k, v, gb, ctx_len=ctx_len, rows=SCAN_ROWS)
    return _l0_tail(hf, hb, gay, o0, o1, sz, ctx, x, modtab, e_b_norm_g.reshape(1, -1), e_w_out.astype(BF16),
                    norm_ffn_g.reshape(1, d), e_ffn_w_gate.astype(BF16), e_ffn_w_up.astype(BF16),
                    e_ffn_w_down.astype(BF16), tm=tm)


def _layer1(hc, hl, modtab, norm_mix_g, norm_ffn_g, final_norm_g, o_w_in, o_w_out, o_lb_logits, o_c_norm_g,
            o_d_gate_w2, o_d_gate_b2, o_d_norm_g, o_router_w, o_router_b, o_moe_w_gate, o_moe_w_up, o_moe_w_down,
            *, tm, tk, layer):
    bsz, seq, d = hl.shape
    ctx_len = hc.shape[1]
    rows = seq // GRID_W
    hl = hl.reshape(bsz, rows, GRID_W, d).swapaxes(1, 2).reshape(bsz, seq, d)
    w_in, w_tail = o_w_in.astype(BF16), _pad_cols(o_w_in[:, O_IN_MAIN:], 128).astype(BF16)
    wlr = jnp.zeros((128, SEG), F32)
    wlr = wlr.at[0:GLA_RANK, 0:256].set(o_d_gate_w2[0]).at[GLA_RANK:2 * GLA_RANK, 256:512].set(o_d_gate_w2[1])
    proj = functools.partial(_inproj1, g=norm_mix_g.reshape(1, d), w=w_in, w_tail=w_tail, lbl=o_lb_logits, wlr=wlr,
                             b2=o_d_gate_b2.reshape(1, SEG), layer=layer)
    p1c = proj(hc, 0, ctx_len, modtab, 0, tm=tm)
    p1l = proj(hl, 0, seq, modtab, 1, tm=2 * tm)
    sh0, sg0 = _mix1_ctx(p1c, rows=SCAN_ROWS)
    o0, o1 = _mix1_lat(p1l, sh0, sg0, rows=SCAN_ROWS)
    return _moe(o0, o1, p1l[S_CG[0]], S_CG[1] * SEG // d, hl, modtab, o_c_norm_g.reshape(1, -1),
                o_d_norm_g.reshape(1, -1), o_w_out.astype(BF16), norm_ffn_g.reshape(1, d),
                final_norm_g.reshape(1, d), o_router_w, o_router_b.reshape(N_EXPERTS, 1),
                o_moe_w_gate.astype(BF16), o_moe_w_up.astype(BF16), o_moe_w_down.astype(BF16), tk=tk)


def kernel(x, c, ctx, c_ctx, ada_w, ada_b, norm_mix_g, norm_ffn_g, final_norm_g, e_w_in, e_w_out, e_a_conv_w, e_a_conv_b, e_a_gate_w, e_a_gate_b, e_a_lambda, e_b_conv_w, e_b_a_log, e_b_dt_bias, e_b_norm_g, e_ffn_w_gate, e_ffn_w_up, e_ffn_w_down, o_w_in, o_w_out, o_lb_logits, o_c_norm_g, o_d_gate_w2, o_d_gate_b2, o_d_norm_g, o_router_w, o_router_b, o_moe_w_gate, o_moe_w_up, o_moe_w_down):
    bsz, seq, d = x.shape
    ctx_len = ctx.shape[1]
    assert bsz == 8 and d == D_MODEL and ada_w.shape[0] == 2
    tm = min(256, ctx_len)
    tt = min(128, ctx_len)
    tk = min(512, seq)
    assert ctx_len % tm == 0 and seq % tm == 0 and ctx_len % SCAN_ROWS == 0 and seq % SCAN_ROWS == 0
    assert seq % GRID_W == 0 and seq % tk == 0

    mods = _ada(c, c_ctx, ada_w, ada_b)
    hc, hl = _layer0(ctx, x, _modtab(mods[0], bsz), norm_mix_g[0], norm_ffn_g[0], e_w_in[0], e_w_out[0],
                     e_a_conv_w[0], e_a_conv_b[0], e_a_gate_w[0], e_a_gate_b[0], e_a_lambda[0], e_b_conv_w[0],
                     e_b_a_log[0], e_b_dt_bias[0], e_b_norm_g[0], e_ffn_w_gate[0], e_ffn_w_up[0], e_ffn_w_down[0],
                     tm=tm, tt=tt)
    out_cm = _layer1(hc, hl, _modtab(mods[1], bsz), norm_mix_g[1], norm_ffn_g[1], final_norm_g, o_w_in[0],
                     o_w_out[0], o_lb_logits, o_c_norm_g[0], o_d_gate_w2[0], o_d_gate_b2[0], o_d_norm_g[0],
                     o_router_w[0], o_router_b[0], o_moe_w_gate[0], o_moe_w_up[0], o_moe_w_down[0],
                     tm=tm, tk=tk, layer=1)
    rows = seq // GRID_W
    return out_cm.reshape(bsz, GRID_W, rows, d).swapaxes(1, 2).reshape(bsz, seq, d)
```

```python
import functools

import jax
import jax.numpy as jnp
from jax import lax
from jax.experimental import pallas as pl
from jax.experimental.pallas import tpu as pltpu

F32 = jnp.float32
BF16 = jnp.bfloat16
HI = lax.Precision.HIGHEST

EPS = 1e-6
D_MODEL = 1024
GRID_W = 64
CONV_K = 4
RG_WIDTH = 512
RG_BLOCK = 64
RG_C = 8.0
DN_HEADS = 4
DN_D = 128
DN_CHUNK = 64
HG_HEADS = 4
HG_D = 128
GLA_HEADS = 4
GLA_DK = 64
GLA_DV = 128
GLA_RANK = 16
GLA_GATE_NORM = 16.0
MIX1_CHUNK = 64
SCAN_ROWS = 256
D_FF = 2816
N_EXPERTS = 8

E_IN_MAIN = 3072
E_IN_PAD = E_IN_MAIN + 128
O_IN_MAIN = 4096
O_IN_PAD = O_IN_MAIN + 128
SEG = 512
S_HQ, S_HV, S_GV, S_GQK, S_HK0, S_HK1, S_CG, S_DG = [(0, i) for i in range(8)]
N_SHARED = 4
S_HLF0, S_HLF1, S_GLD = [(1, i) for i in range(3)]
N_SEG = (8, 3)
P1_DTYPES = (BF16, F32)

V7X_VMEM_BYTES = 64 * 1024 * 1024
VMEM_HEADROOM_BYTES = 8 * 1024 * 1024
MOE_PIECE = 16
MOE_SEG_PIECES = (64, MOE_PIECE)
MOE_BLOCK = 512
MOE_GATHER_ROWS = 256


def _vmem(nbytes):
    return int(min(V7X_VMEM_BYTES - VMEM_HEADROOM_BYTES, nbytes))


def _params(sem, vmem_bytes):
    return pltpu.CompilerParams(dimension_semantics=sem, vmem_limit_bytes=_vmem(vmem_bytes))


def _sigmoid(x):
    return jax.nn.sigmoid(x)


def _sigmoid_tanh(x):
    return 0.5 * jnp.tanh(0.5 * x) + 0.5


def _silu(x):
    return x * jax.nn.sigmoid(x)


def _softplus(x):
    return jnp.maximum(x, 0.0) + jnp.log1p(jnp.exp(-jnp.abs(x)))


def _gelu_tanh(x):
    return 0.5 * x * (1.0 + jnp.tanh(0.7978845608028654 * (x + 0.044715 * (x * x * x))))


def _normmod(x, g, shift, scale):
    y = x * lax.rsqrt(jnp.mean(x * x, axis=-1, keepdims=True) + EPS)
    return (y * g) * (1.0 + scale) + shift


def _dot(a, b):
    return jnp.dot(a, b, preferred_element_type=F32)


def _dot_nt(a, b):
    return lax.dot_general(a, b, (((1,), (1,)), ((), ())), preferred_element_type=F32)


def _dot_tn(a, b):
    return lax.dot_general(a, b, (((0,), (0,)), ((), ())), preferred_element_type=F32)


def _dot_hi(a, b):
    return jnp.dot(a, b, precision=HI, preferred_element_type=F32)


def _split3(x):
    hi = x.astype(BF16)
    r1 = x - hi.astype(F32)
    mid = r1.astype(BF16)
    return hi, mid, (r1 - mid.astype(F32)).astype(BF16)


def _dot_hilo(a, b):
    ah = a.astype(BF16)
    al = (a - ah.astype(F32)).astype(BF16)
    bh = b.astype(BF16)
    bl = (b - bh.astype(F32)).astype(BF16)
    return _dot(ah, bh) + (_dot(ah, bl) + _dot(al, bh))


def _mask_dot(mask, x):
    mb = mask.astype(BF16)
    hi, mid, lo = _split3(x)
    return _dot(mb, hi) + (_dot(mb, mid) + _dot(mb, lo))


def _mask_dot_tn(x, mask):
    mb = mask.astype(BF16)
    hi, mid, lo = _split3(x)
    return _dot_tn(hi, mb) + (_dot_tn(mid, mb) + _dot_tn(lo, mb))


def _const_spec(shape):
    nd = len(shape)
    return pl.BlockSpec(shape, lambda *_: (0,) * nd, pipeline_mode=pl.Buffered(1))


def _scan_masks(c, d):
    row = lax.broadcasted_iota(jnp.int32, (c, c), 0)
    col = lax.broadcasted_iota(jnp.int32, (c, c), 1)
    dlt = row - col if d == 0 else col - row
    return dlt >= 0, dlt > 0, dlt <= 0, row == col


def _ada_kernel(cv_ref, w_ref, b_ref, o_ref):
    s = _silu(cv_ref[...]).astype(BF16)
    o_ref[...] = _dot(s, w_ref[...].astype(BF16)) + b_ref[...]


def _ada(c, c_ctx, ada_w, ada_b):
    depth, d, n6 = ada_w.shape
    bsz = c.shape[0]
    rows = 16
    cv = jnp.zeros((rows, d), F32).at[:bsz].set(c).at[bsz].set(c_ctx)
    tn = 1536
    return pl.pallas_call(
        _ada_kernel,
        grid=(depth, n6 // tn),
        in_specs=[pl.BlockSpec((rows, d), lambda l, j: (0, 0)),
                  pl.BlockSpec((None, d, tn), lambda l, j: (l, 0, j)),
                  pl.BlockSpec((None, 1, tn), lambda l, j: (l, 0, j))],
        out_specs=pl.BlockSpec((None, rows, tn), lambda l, j: (l, 0, j)),
        out_shape=jax.ShapeDtypeStruct((depth, rows, n6), F32),
        compiler_params=_params(("arbitrary", "arbitrary"), 32 << 20),
        name="ada_mod",
    )(cv, ada_w, ada_b.reshape(depth, 1, n6))


def _modtab(mods_l, bsz):
    m = mods_l.reshape(mods_l.shape[0], 6, D_MODEL)
    lat = m[:bsz]
    ctx = jnp.broadcast_to(m[bsz][None], (bsz, 6, D_MODEL))
    return jnp.stack([ctx, lat], axis=1)


def _inproj0_kernel(cp_ref, cm_ref, cn_ref, xp_ref, xm_ref, xn_ref, mod_ref, g_ref, w_ref, wt_ref, acw_ref, acb_ref,
                    bcw_ref, gpar_ref, ua_ref, gay_ref, q_ref, k_ref, v_ref, sz_ref, gb_ref, u_scr,
                    *, tm, ctx_tiles, n_tiles):
    t = pl.program_id(1)
    pick = lambda c_ref, x_ref: jnp.where(t < ctx_tiles, c_ref[...], x_ref[...])
    x = jnp.concatenate([pick(cp_ref, xp_ref), pick(cm_ref, xm_ref), pick(cn_ref, xn_ref)], axis=0)
    xm = _normmod(x, g_ref[...], mod_ref[0:1, :], mod_ref[1:2, :]).astype(BF16)
    seg_first = jnp.logical_or(t == 0, t == ctx_tiles)
    seg_last = jnp.logical_or(t == ctx_tiles - 1, t == n_tiles - 1)

    def project(c0, width, conv_input):
        u_scr[:, c0:c0 + width] = _dot(xm, wt_ref[...] if c0 == E_IN_MAIN else w_ref[:, c0:c0 + width])
        if conv_input:
            u_scr[0:8, c0:c0 + width] = jnp.where(seg_first, 0.0, u_scr[0:8, c0:c0 + width])
            u_scr[tm + 8:tm + 16, c0:c0 + width] = jnp.where(seg_last, 0.0, u_scr[tm + 8:tm + 16, c0:c0 + width])

    def conv(c0, width, w_ref_, w0):
        acc = u_scr[6:6 + tm, c0:c0 + width] * w_ref_[0:1, w0:w0 + width]
        for j in range(1, CONV_K):
            acc = acc + u_scr[6 + j:6 + j + tm, c0:c0 + width] * w_ref_[j:j + 1, w0:w0 + width]
        return acc

    project(0, RG_WIDTH, True)
    for grp in range(RG_WIDTH // 128):
        c0 = grp * 128
        ua_ref[:, c0:c0 + 128] = conv(c0, 128, acw_ref, c0) + acb_ref[0:1, c0:c0 + 128]
    project(512, 512, False)
    gay_ref[...] = _gelu_tanh(u_scr[8:8 + tm, 512:1024]).astype(BF16)

    for grp in range(3 * DN_HEADS):
        c0 = grp * 128
        if grp % DN_HEADS == 0:
            project(1024 + c0, DN_HEADS * DN_D, True)
        y = _silu(conv(1024 + c0, 128, bcw_ref, c0))
        if grp < 2 * DN_HEADS:
            y = y * lax.rsqrt(jnp.sum(y * y, axis=-1, keepdims=True) + EPS)
        if grp < DN_HEADS:
            q_ref[:, c0:c0 + 128] = (y * (DN_D ** -0.5)).astype(BF16)
        elif grp < 2 * DN_HEADS:
            k_ref[:, c0 - 512:c0 - 384] = y.astype(BF16)
        else:
            v_ref[:, c0 - 1024:c0 - 896] = y.astype(BF16)
    project(2560, 512, False)
    sz_ref[...] = _silu(u_scr[8:8 + tm, 2560:3072]).astype(BF16)

    project(3072, 128, False)
    xg = u_scr[8:8 + tm, 3072:3200]
    lane = lax.broadcasted_iota(jnp.int32, xg.shape, 1)
    g = -jnp.exp(gpar_ref[0:1, :]) * _softplus(xg + gpar_ref[1:2, :])
    gb_ref[...] = jnp.where(lane < 2 * DN_HEADS, _sigmoid(xg), g)


def _row_specs(tm, d, ctx_tiles, ctx_len, seq, halo):
    tb = tm // 8

    def specs(n_rows, tile_of):
        main = pl.BlockSpec((None, tm, d), lambda b, t: (b, jnp.clip(tile_of(t), 0, n_rows // tm - 1), 0))
        if not halo:
            return [main]
        prev = pl.BlockSpec((None, 8, d), lambda b, t: (b, jnp.clip(tile_of(t) * tb - 1, 0, n_rows // 8 - 1), 0))
        nxt = pl.BlockSpec((None, 8, d), lambda b, t: (b, jnp.clip((tile_of(t) + 1) * tb, 0, n_rows // 8 - 1), 0))
        return [prev, main, nxt]

    return specs(ctx_len, lambda t: t) + specs(seq, lambda t: t - ctx_tiles)


def _inproj0(ctx, x, modtab, g, w, w_tail, acw, acb, bcw, gpar, *, tm):
    bsz, ctx_len, d = ctx.shape
    seq = x.shape[1]
    t_all = ctx_len + seq
    n_tiles = t_all // tm
    ctx_tiles = ctx_len // tm
    kern = functools.partial(_inproj0_kernel, tm=tm, ctx_tiles=ctx_tiles, n_tiles=n_tiles)
    tok = lambda w, dt=BF16: jax.ShapeDtypeStruct((bsz, t_all, w), dt)
    tok_spec = lambda w: pl.BlockSpec((None, tm, w), lambda b, t: (b, t, 0))
    return pl.pallas_call(
        kern,
        grid=(bsz, n_tiles),
        in_specs=_row_specs(tm, d, ctx_tiles, ctx_len, seq, True) + [
            pl.BlockSpec((None, None, 6, d), lambda b, t: (b, jnp.where(t >= ctx_tiles, 1, 0), 0, 0)),
            _const_spec((1, d)),
            _const_spec(w.shape), _const_spec((d, 128)),
            _const_spec((CONV_K, RG_WIDTH)),
            _const_spec((1, RG_WIDTH)),
            _const_spec((CONV_K, 3 * DN_HEADS * DN_D)),
            _const_spec((2, 128)),
        ],
        out_specs=[tok_spec(512), tok_spec(512), tok_spec(512), tok_spec(512), tok_spec(512), tok_spec(512),
                   tok_spec(128)],
        out_shape=[tok(512, F32), tok(512), tok(512), tok(512), tok(512), tok(512), tok(128, F32)],
        scratch_shapes=[pltpu.VMEM((tm + 16, E_IN_PAD), F32)],
        compiler_params=_params(("arbitrary", "arbitrary"), 40 << 20),
        name="l0_inproj",
    )(ctx, ctx, ctx, x, x, x, modtab, g, w, w_tail, acw, acb, bcw, gpar)


def _rglru_kernel(uf_ref, ub_ref, wg_ref, gbias_ref, lam_ref, hf_ref, hb_ref,
                  af_scr, xf_scr, ab_scr, xb_scr, stg_scr, of_scr, ob_scr, h_scr, *, tt, bsz):
    s = pl.program_id(0)
    n_slab = RG_WIDTH // 128

    @pl.when(s == 0)
    def _():
        h_scr[...] = jnp.zeros_like(h_scr)

    def gates(u_ref, d, a_scr, x_scr):
        for b in range(bsz):
            for j in range(n_slab):
                stg_scr[j, pl.ds(b, tt, stride=bsz), :] = u_ref[b, :, j * 128:(j + 1) * 128]
        x = jnp.concatenate([stg_scr[j] for j in range(n_slab)], axis=1)
        xb = x.astype(BF16)
        for half in range(2):
            c0 = half * 256
            xh = xb[:, c0:c0 + 256]
            r = _sigmoid_tanh(_dot(xh, wg_ref[d, 0, half]) + gbias_ref[2 * d:2 * d + 1, c0:c0 + 256])
            i = _sigmoid_tanh(_dot(xh, wg_ref[d, 1, half]) + gbias_ref[2 * d + 1:2 * d + 2, c0:c0 + 256])
            log_a = (-RG_C) * r * _softplus(-lam_ref[d:d + 1, c0:c0 + 256])
            a = jnp.exp(log_a)
            mult = jnp.sqrt(-jnp.tanh(log_a) * (a * a + 1.0))
            xin = mult * (i * x[:, c0:c0 + 256])
            a_scr[:, :, c0:c0 + 256] = a.reshape(tt, bsz, 256)
            x_scr[:, :, c0:c0 + 256] = xin.reshape(tt, bsz, 256)

    gates(uf_ref, 0, af_scr, xf_scr)
    gates(ub_ref, 1, ab_scr, xb_scr)

    def put(o_scr, t, h):
        rows = pl.ds(pl.multiple_of(t * bsz, bsz), bsz)
        for j in range(n_slab):
            o_scr[j, rows, :] = h[:, j * 128:(j + 1) * 128]

    def step(t, carry):
        hf, hb = carry
        hf = af_scr[t] * hf + xf_scr[t]
        put(of_scr, t, hf)
        tb = tt - 1 - t
        hb = ab_scr[tb] * hb + xb_scr[tb]
        put(ob_scr, tb, hb)
        return hf, hb

    hf, hb = lax.fori_loop(0, tt, step, (h_scr[0], h_scr[1]), unroll=8)
    h_scr[0] = hf
    h_scr[1] = hb
    for o_scr, o_ref in ((of_scr, hf_ref), (ob_scr, hb_ref)):
        for b in range(bsz):
            for j in range(n_slab):
                o_ref[b, :, j * 128:(j + 1) * 128] = o_scr[j, pl.ds(b, tt, stride=bsz), :]


def _rglru(ua, wg, gbias, lam, *, tt, ctx_len):
    bsz, t_all, w = ua.shape
    n_steps = t_all // tt
    nc = ctx_len // tt

    def bwd(s):
        return jnp.where(s < nc, nc - 1 - s, n_steps + nc - 1 - s)

    blk = (bsz, tt, w)
    tm_blk = (tt, bsz, w)
    slabs = (w // 128, tt * bsz, 128)
    kern = functools.partial(_rglru_kernel, tt=tt, bsz=bsz)
    return pl.pallas_call(
        kern,
        grid=(n_steps,),
        in_specs=[pl.BlockSpec(blk, lambda s: (0, s, 0)),
                  pl.BlockSpec(blk, lambda s: (0, bwd(s), 0)),
                  _const_spec(wg.shape), _const_spec(gbias.shape), _const_spec(lam.shape)],
        out_specs=[pl.BlockSpec(blk, lambda s: (0, s, 0)),
                   pl.BlockSpec(blk, lambda s: (0, bwd(s), 0))],
        out_shape=[jax.ShapeDtypeStruct(ua.shape, F32)] * 2,
        scratch_shapes=[pltpu.VMEM(tm_blk, F32)] * 4 + [pltpu.VMEM(slabs, F32)] * 3 + [pltpu.VMEM((2, bsz, w), F32)],
        compiler_params=_params(("arbitrary",), 48 << 20),
        name="l0_rglru",
    )(ua, ua, wg, gbias, lam)


def _delta_kernel(qf_ref, kf_ref, vf_ref, gf_ref, qb_ref, kb_ref, vb_ref, gb_ref, of_ref, ob_ref, s_scr, *, n_sub):
    c = DN_CHUNK

    @pl.when(pl.program_id(1) == 0)
    def _():
        s_scr[...] = jnp.zeros_like(s_scr)

    dir_refs = ((qf_ref, kf_ref, vf_ref, gf_ref, of_ref), (qb_ref, kb_ref, vb_ref, gb_ref, ob_ref))
    masks = [_scan_masks(c, d) for d in range(2)]
    eye = jnp.where(masks[0][3], 1.0, 0.0)

    cums = {}
    for d in range(2):
        incl, _, incl_t, _ = masks[d]
        m_incl = jnp.where(incl, 1.0, 0.0)
        m_incl_t = jnp.where(incl_t, 1.0, 0.0)
        for ci in range(n_sub):
            g_all = dir_refs[d][3][ci * c:(ci + 1) * c, :]
            gc_all = _mask_dot(m_incl, g_all)
            gct_all = _mask_dot_tn(g_all, m_incl_t)
            cums[d, ci] = (g_all, gc_all, gct_all)

    chains = []
    for d in range(2):
        q_ref, k_ref, v_ref, _, _ = dir_refs[d]
        incl, strict, _, _ = masks[d]
        last = c - 1 if d == 0 else 0
        for ci in range(n_sub):
            g_all, gc_all, gct_all = cums[d, ci]
            rs = slice(ci * c, (ci + 1) * c)
            for h in range(DN_HEADS):
                hs = slice(h * DN_D, (h + 1) * DN_D)
                lane = 2 * DN_HEADS + d * DN_HEADS + h
                ch = dict(d=d, ci=ci, h=h, rs=rs, hs=hs, incl=incl, strict=strict)
                ch["beta"] = g_all[:, d * DN_HEADS + h:d * DN_HEADS + h + 1]
                gc = jnp.broadcast_to(gc_all[:, lane:lane + 1], (c, DN_D))
                gc_row = jnp.broadcast_to(gct_all[lane:lane + 1, :], (c, c))
                ch["gc"] = gc
                ch["gtot"] = gc[last:last + 1, :]
                ch["decay"] = jnp.where(incl, jnp.exp(jnp.minimum(gc[:, 0:c] - gc_row, 0.0)), 0.0)
                ch["e_gc"] = jnp.exp(gc)
                ch["q"] = q_ref[rs, hs].astype(F32)
                ch["k"] = k_ref[rs, hs].astype(F32)
                ch["v"] = v_ref[rs, hs].astype(F32)
                chains.append(ch)

    for ch in chains:
        ch["kb"] = ch["k"] * ch["beta"]
        qk = _dot_nt(jnp.concatenate([ch["kb"], ch["q"]], axis=0).astype(BF16), ch["k"].astype(BF16))
        ch["neg"] = -jnp.where(ch["strict"], qk[0:c] * ch["decay"], 0.0)
        ch["a_qk"] = (qk[c:2 * c] * ch["decay"]).astype(BF16)
    for ch in chains:
        negb = ch["neg"].astype(BF16)
        ch["t"] = eye + ch["neg"]
        ch["p"] = _dot(negb, negb)
    n_sq = max(1, (c - 1).bit_length() - 1)
    for it in range(n_sq):
        for ch in chains:
            tp = _dot(jnp.concatenate([ch["t"], ch["p"]], axis=0).astype(BF16), ch["p"].astype(BF16))
            ch["t"] = ch["t"] + tp[0:c]
            ch["p"] = tp[c:2 * c]
    for ch in chains:
        rhs = jnp.concatenate([ch["v"] * ch["beta"], ch["kb"] * ch["e_gc"]], axis=1).astype(BF16)
        sol = _dot(ch["t"].astype(BF16), rhs)
        ch["u"] = sol[:, 0:DN_D]
        ch["wq"] = jnp.concatenate([sol[:, DN_D:2 * DN_D], ch["q"] * ch["e_gc"]], axis=0).astype(BF16)
        ch["k_tail"] = (ch["k"] * jnp.exp(ch["gtot"] - ch["gc"])).astype(BF16)

    by_key = {(ch["d"], ch["ci"], ch["h"]): ch for ch in chains}
    for step in range(n_sub):
        live = [by_key[d, step if d == 0 else n_sub - 1 - step, h] for d in range(2) for h in range(DN_HEADS)]
        for ch in live:
            ch["st"] = s_scr[ch["d"], ch["h"]]
            ch["ws"] = _dot(ch["wq"], ch["st"].astype(BF16))
        for ch in live:
            vnb = (ch["u"] - ch["ws"][0:c]).astype(BF16)
            o = ch["ws"][c:2 * c] + _dot(ch["a_qk"], vnb)
            dir_refs[ch["d"]][4][ch["rs"], ch["hs"]] = o.astype(BF16)
            s_scr[ch["d"], ch["h"]] = ch["st"] * jnp.exp(ch["gtot"]) + _dot_tn(ch["k_tail"], vnb)


def _delta(q, k, v, gb, *, ctx_len, rows):
    bsz, t_all, w = q.shape
    n_steps = t_all // rows
    nc = ctx_len // rows

    def bwd(s):
        return jnp.where(s < nc, nc - 1 - s, n_steps + nc - 1 - s)

    fwd_spec = lambda width: pl.BlockSpec((None, rows, width), lambda b, s: (b, s, 0))
    bwd_spec = lambda width: pl.BlockSpec((None, rows, width), lambda b, s: (b, bwd(s), 0))
    return pl.pallas_call(
        functools.partial(_delta_kernel, n_sub=rows // DN_CHUNK),
        grid=(bsz, n_steps),
        in_specs=[fwd_spec(w), fwd_spec(w), fwd_spec(w), fwd_spec(128),
                  bwd_spec(w), bwd_spec(w), bwd_spec(w), bwd_spec(128)],
        out_specs=[fwd_spec(w), bwd_spec(w)],
        out_shape=[jax.ShapeDtypeStruct((bsz, t_all, w), BF16)] * 2,
        scratch_shapes=[pltpu.VMEM((2, DN_HEADS, DN_D, DN_D), F32)],
        compiler_params=_params(("arbitrary", "arbitrary"), 32 << 20),
        name="l0_deltanet",
    )(q, k, v, gb, q, k, v, gb)


def _head_norm(y, g):
    return y * lax.rsqrt(jnp.mean(y * y, axis=-1, keepdims=True) + EPS) * g


def _l0_tail_kernel(hf_ref, hb_ref, gay_ref, o0_ref, o1_ref, sz_ref, ctx_ref, x_ref, mod_ref, ng_ref, wo_ref, g_ref,
                    wg_ref, wu_ref, wd_ref, hc_ref, hl_ref, *, n_chunks, ctx_tiles):
    t = pl.program_id(1)
    tm = hf_ref.shape[0]
    halves = [dict(rs=slice(i * tm // 2, (i + 1) * tm // 2)) for i in range(2)]
    for hv in halves:
        rs = hv["rs"]
        parts = [((hf_ref[rs, :] + hb_ref[rs, :]) * gay_ref[rs, :].astype(F32)).astype(BF16)]
        for hd in range(DN_HEADS):
            lo = hd * DN_D
            ob = o0_ref[rs, lo:lo + DN_D].astype(F32) + o1_ref[rs, lo:lo + DN_D].astype(F32)
            parts.append((_head_norm(ob, ng_ref[...]) * sz_ref[rs, lo:lo + DN_D].astype(F32)).astype(BF16))
        hv["ycat"] = jnp.concatenate(parts, axis=-1)
    for hv in halves:
        h = jnp.where(t < ctx_tiles, ctx_ref[hv["rs"], :], x_ref[hv["rs"], :])
        hv["x"] = h + mod_ref[2:3, :] * _dot(hv["ycat"], wo_ref[...])
    for hv in halves:
        hv["xm"] = _normmod(hv["x"], g_ref[...], mod_ref[3:4, :], mod_ref[4:5, :]).astype(BF16)
        hv["acc"] = jnp.zeros(hv["x"].shape, F32)
    cw = D_FF // n_chunks
    for ci in range(n_chunks):
        c0 = ci * cw
        for hv in halves:
            hv["act"] = (_silu(_dot(hv["xm"], wg_ref[:, c0:c0 + cw])) * _dot(hv["xm"], wu_ref[:, c0:c0 + cw])).astype(BF16)
        for hv in halves:
            hv["acc"] = hv["acc"] + _dot(hv["act"], wd_ref[c0:c0 + cw, :])
    out = jnp.concatenate([hv["x"] + mod_ref[5:6, :] * hv["acc"] for hv in halves], axis=0)

    @pl.when(t < ctx_tiles)
    def _():
        hc_ref[...] = out

    @pl.when(t >= ctx_tiles)
    def _():
        hl_ref[...] = out


def _l0_tail(hf, hb, gay, o0, o1, sz, ctx, x, modtab, ng, wo, g, wg, wu, wd, *, tm):
    bsz, ctx_len, d = ctx.shape
    seq = x.shape[1]
    t_all = ctx_len + seq
    ctx_tiles = ctx_len // tm
    tok = lambda width: pl.BlockSpec((None, tm, width), lambda b, t: (b, t, 0))
    return pl.pallas_call(
        functools.partial(_l0_tail_kernel, n_chunks=2, ctx_tiles=ctx_tiles),
        grid=(bsz, t_all // tm),
        in_specs=[tok(512)] * 6 + _row_specs(tm, d, ctx_tiles, ctx_len, seq, False) + [
                  pl.BlockSpec((None, None, 6, d), lambda b, t: (b, jnp.where(t >= ctx_tiles, 1, 0), 0, 0)),
                  _const_spec((1, DN_D)), _const_spec((d, d)),
                  _const_spec((1, d)), _const_spec((d, D_FF)), _const_spec((d, D_FF)), _const_spec((D_FF, d))],
        out_specs=[pl.BlockSpec((None, tm, d), lambda b, t: (b, jnp.minimum(t, ctx_tiles - 1), 0)),
                   pl.BlockSpec((None, tm, d), lambda b, t: (b, jnp.maximum(t - ctx_tiles, 0), 0))],
        out_shape=[jax.ShapeDtypeStruct((bsz, ctx_len, d), F32), jax.ShapeDtypeStruct((bsz, t_all - ctx_len, d), F32)],
        compiler_params=_params(("arbitrary", "arbitrary"), 48 << 20),
        name="l0_tail",
    )(hf, hb, gay, o0, o1, sz, ctx, x, modtab, ng, wo, g, wg, wu, wd)


def _inproj1_kernel(h_ref, mod_ref, g_ref, w_ref, wt_ref, lbl_ref, wlr_ref, b2_ref, pa_ref, pb_ref, u_scr, *, layer):
    x = h_ref[...]
    xm = _normmod(x, g_ref[...], mod_ref[0:1, :], mod_ref[1:2, :]).astype(BF16)

    lg = lbl_ref[...]
    ex = jnp.exp(lg - jnp.max(lg, axis=0, keepdims=True))
    lbw = ex / jnp.sum(ex, axis=0, keepdims=True)
    lb = jnp.sum(lbw[1:layer + 1], axis=0, keepdims=True)

    def project(c0, width):
        u_scr[:, c0:c0 + width] = _dot(xm, wt_ref[...] if c0 == O_IN_MAIN else w_ref[:, c0:c0 + width])

    def put(seg, off, val):
        ref = (pa_ref, pb_ref)[seg[0]]
        ref[:, seg[1] * SEG + off:seg[1] * SEG + off + val.shape[1]] = val.astype(ref.dtype)

    def groups(fn):
        for grp in range(SEG // 128):
            fn(grp * 128)

    project(0, SEG)
    groups(lambda c0: put(S_HQ, c0, _silu(u_scr[:, c0:c0 + 128]) * (HG_D ** -0.5)))
    for dr, (sk, sf) in enumerate(((S_HK0, S_HLF0), (S_HK1, S_HLF1))):
        project(512 + dr * 512, SEG)

        def forget(c0, dr=dr, sk=sk, sf=sf):
            lbg = lb[:, c0:c0 + 128]
            fl = u_scr[:, 512 + dr * 512 + c0:512 + dr * 512 + c0 + 128]
            sg = _sigmoid(fl)
            put(sf, c0, jnp.log(lbg + (1.0 - lbg) * sg))
            put(sk, c0, (1.0 - lbg) * (1.0 - sg))
        groups(forget)
    project(1536, SEG)
    groups(lambda c0: put(S_HV, c0, u_scr[:, 1536 + c0:1536 + c0 + 128]))
    project(2048, SEG)
    groups(lambda c0: put(S_CG, c0, _silu(u_scr[:, 2048 + c0:2048 + c0 + 128])))
    project(2560, SEG)
    put(S_GQK, 0, u_scr[:, 2560:2816] * (GLA_DK ** -0.5))
    put(S_GQK, 256, u_scr[:, 2816:3072])
    project(3072, SEG)
    groups(lambda c0: put(S_GV, c0, u_scr[:, 3072 + c0:3072 + c0 + 128]))
    project(3584, SEG)
    groups(lambda c0: put(S_DG, c0, _silu(u_scr[:, 3584 + c0:3584 + c0 + 128])))
    project(4096, 128)
    lr = u_scr[:, 4096:4224]
    put(S_GLD, 0, -_softplus(-(_dot_hilo(lr, wlr_ref[...]) + b2_ref[...])) * (1.0 / GLA_GATE_NORM))


def _inproj1(h, row0, n_rows, modtab, seg, g, w, w_tail, lbl, wlr, b2, *, tm, layer):
    bsz, _, d = h.shape
    t0 = row0 // tm
    return pl.pallas_call(
        functools.partial(_inproj1_kernel, layer=layer),
        grid=(bsz, n_rows // tm),
        in_specs=[pl.BlockSpec((None, tm, d), lambda b, t: (b, t0 + t, 0)),
                  pl.BlockSpec((None, None, 6, d), lambda b, t: (b, seg, 0, 0)),
                  _const_spec((1, d)), _const_spec(w.shape), _const_spec((d, 128)), _const_spec(lbl.shape),
                  _const_spec((128, SEG)), _const_spec((1, SEG))],
        out_specs=[pl.BlockSpec((None, tm, n * SEG), lambda b, t: (b, t, 0)) for n in N_SEG],
        out_shape=[jax.ShapeDtypeStruct((bsz, n_rows, n * SEG), dt) for n, dt in zip(N_SEG, P1_DTYPES)],
        scratch_shapes=[pltpu.VMEM((tm, O_IN_PAD), F32)],
        compiler_params=_params(("arbitrary", "arbitrary"), 44 << 20),
        name="l1_inproj",
    )(h, modtab, g, w, w_tail, lbl, wlr, b2)


def _gla_stream(d, q_all, k_all, ld_all, v_all, st_ref, o_ref, o_lane0, r0, n_heads, dk, dv, incl, m_incl):
    c = k_all.shape[0]
    mid = c // 2 - 1 if d == 0 else c // 2
    last = c - 1 if d == 0 else 0
    bc = _mask_dot(m_incl, ld_all)
    m = bc[mid:mid + 1]
    btot = bc[last:last + 1]
    kn = k_all.astype(F32) * jnp.exp(m - bc)
    it = dict(d=d, r0=r0, st_ref=st_ref, o_ref=o_ref, o_lane0=o_lane0, n_heads=n_heads, dk=dk, dv=dv, incl=incl,
              c=c, kt=(kn * jnp.exp(btot - m)).astype(BF16), dec=jnp.exp(btot), v=v_all.astype(BF16),
              want_out=q_all is not None)
    if q_all is not None:
        qe = q_all.astype(F32) * jnp.exp(bc)
        it.update(qd=(qe * jnp.exp(-m)).astype(BF16), qe=qe.astype(BF16), knb=kn.astype(BF16))
    return it


def _gla_intra(it):
    dk = it["dk"]
    it["a"] = [jnp.where(it["incl"], _dot_nt(it["qd"][:, hd * dk:(hd + 1) * dk], it["knb"][:, hd * dk:(hd + 1) * dk]),
                         0.0).astype(BF16) for hd in range(it["n_heads"])]


def _gla_advance(it):
    d, dk, dv, c, st_ref = it["d"], it["dk"], it["dv"], it["c"], it["st_ref"]
    sts = [st_ref[d, hd] for hd in range(it["n_heads"])]
    if it["want_out"]:
        for hd in range(it["n_heads"]):
            v = it["v"][:, hd * dv:(hd + 1) * dv]
            o = _dot(it["a"][hd], v) + _dot_nt(it["qe"][:, hd * dk:(hd + 1) * dk], sts[hd].astype(BF16))
            it["o_ref"][it["r0"]:it["r0"] + c, it["o_lane0"] + hd * dv:it["o_lane0"] + (hd + 1) * dv] = o.astype(BF16)
    for hd in range(it["n_heads"]):
        ks = slice(hd * dk, (hd + 1) * dk)
        st_ref[d, hd] = sts[hd] * it["dec"][:, ks] + _dot_tn(it["v"][:, hd * dv:(hd + 1) * dv], it["kt"][:, ks])


def _mix1_body(dirs, n_sub, sh_ref, sg_ref):
    c = MIX1_CHUNK
    gw = GLA_HEADS * GLA_DK
    lane = lambda seg: slice(seg[1] * SEG, (seg[1] + 1) * SEG)
    prepared = {}
    for d, (sh_in, hk_ref, hlf_ref, gld_ref, o_ref) in enumerate(dirs):
        incl = _scan_masks(c, d)[0]
        m_incl = jnp.where(incl, 1.0, 0.0)
        g0 = S_GQK[1] * SEG
        for ci in range(n_sub):
            r0 = ci * c
            rs = slice(r0, r0 + c)
            prepared[d, ci, 0] = _gla_stream(
                d, None if o_ref is None else sh_in[rs, lane(S_HQ)], hk_ref[rs, :], hlf_ref[rs, :],
                sh_in[rs, lane(S_HV)], sh_ref, o_ref, 0, r0, HG_HEADS, HG_D, HG_D, incl, m_incl)
            prepared[d, ci, 1] = _gla_stream(
                d, None if o_ref is None else sh_in[rs, g0:g0 + gw], sh_in[rs, g0 + gw:g0 + 2 * gw],
                gld_ref[rs, d * gw:(d + 1) * gw], sh_in[rs, lane(S_GV)],
                sg_ref, o_ref, HG_HEADS * HG_D, r0, GLA_HEADS, GLA_DK, GLA_DV, incl, m_incl)
    for it in prepared.values():
        if it["want_out"]:
            _gla_intra(it)
    for step in range(n_sub):
        for d in range(2):
            for stream in range(2):
                _gla_advance(prepared[d, step if d == 0 else n_sub - 1 - step, stream])


def _mix1_ctx_kernel(*refs, n_sub):
    fwd, bwd, (sh_ref, sg_ref) = refs[0:4], refs[4:8], refs[8:10]

    @pl.when(pl.program_id(1) == 0)
    def _():
        sh_ref[...] = jnp.zeros_like(sh_ref)
        sg_ref[...] = jnp.zeros_like(sg_ref)

    _mix1_body([tuple(r) + (None,) for r in (fwd, bwd)], n_sub, sh_ref, sg_ref)


def _mix1_lat_kernel(*refs, n_sub):
    fwd, bwd = refs[0:4], refs[4:8]
    sh0_ref, sg0_ref, of_ref, ob_ref, sh_scr, sg_scr = refs[8:14]

    @pl.when(pl.program_id(1) == 0)
    def _():
        sh_scr[...] = sh0_ref[...]
        sg_scr[...] = sg0_ref[...]

    _mix1_body([tuple(fwd) + (of_ref,), tuple(bwd) + (ob_ref,)], n_sub, sh_scr, sg_scr)


def _mix1_specs(p1, rows, n_steps):
    specs, args = [], []
    for d in range(2):
        blk = (lambda b, s: s) if d == 0 else (lambda b, s: n_steps - 1 - s)
        for (arr, sg), n_seg in ((S_HQ, N_SHARED), ((S_HK0, S_HK1)[d], 1), ((S_HLF0, S_HLF1)[d], 1), (S_GLD, 1)):
            assert sg % n_seg == 0
            specs.append(pl.BlockSpec((None, rows, n_seg * SEG),
                                      lambda b, s, blk=blk, sg=sg // n_seg: (b, blk(b, s), sg)))
            args.append(p1[arr])
    return specs, args


_SH_SHAPE = (2, HG_HEADS, HG_D, HG_D)
_SG_SHAPE = (2, GLA_HEADS, GLA_DV, GLA_DK)


def _mix1_ctx(p1c, *, rows):
    bsz, ctx_len, _ = p1c[0].shape
    n_steps = ctx_len // rows
    specs, args = _mix1_specs(p1c, rows, n_steps)
    state = lambda shape: pl.BlockSpec((None,) + shape, lambda b, s: (b, 0, 0, 0, 0))
    return pl.pallas_call(
        functools.partial(_mix1_ctx_kernel, n_sub=rows // MIX1_CHUNK),
        grid=(bsz, n_steps),
        in_specs=specs,
        out_specs=[state(_SH_SHAPE), state(_SG_SHAPE)],
        out_shape=[jax.ShapeDtypeStruct((bsz,) + _SH_SHAPE, F32), jax.ShapeDtypeStruct((bsz,) + _SG_SHAPE, F32)],
        compiler_params=_params(("arbitrary", "arbitrary"), 32 << 20),
        name="l1_ctx_state",
    )(*args)


def _mix1_lat(p1l, sh0, sg0, *, rows):
    bsz, seq, _ = p1l[0].shape
    n_steps = seq // rows
    specs, args = _mix1_specs(p1l, rows, n_steps)
    state = lambda shape: pl.BlockSpec((None,) + shape, lambda b, s: (b, 0, 0, 0, 0))
    ow = HG_HEADS * HG_D + GLA_HEADS * GLA_DV
    return pl.pallas_call(
        functools.partial(_mix1_lat_kernel, n_sub=rows // MIX1_CHUNK),
        grid=(bsz, n_steps),
        in_specs=specs + [state(_SH_SHAPE), state(_SG_SHAPE)],
        out_specs=[pl.BlockSpec((None, rows, ow), lambda b, s: (b, s, 0)),
                   pl.BlockSpec((None, rows, ow), lambda b, s: (b, n_steps - 1 - s, 0))],
        out_shape=[jax.ShapeDtypeStruct((bsz, seq, ow), BF16)] * 2,
        scratch_shapes=[pltpu.VMEM(_SH_SHAPE, F32), pltpu.VMEM(_SG_SHAPE, F32)],
        compiler_params=_params(("arbitrary", "arbitrary"), 32 << 20),
        name="l1_scan",
    )(*args, sh0, sg0)


def _outproj1(o0_ref, o1_ref, gate_ref, h_ref, mod_ref, cng_ref, dng_ref, w_ref):
    parts = []
    for hd in range(HG_HEADS + GLA_HEADS):
        lo = hd * 128
        y = o0_ref[:, lo:lo + 128].astype(F32) + o1_ref[:, lo:lo + 128].astype(F32)
        ng = cng_ref[...] if hd < HG_HEADS else dng_ref[...]
        parts.append((_head_norm(y, ng) * gate_ref[:, lo:lo + 128].astype(F32)).astype(BF16))
    return h_ref[...] + mod_ref[2:3, :] * _dot(jnp.concatenate(parts, axis=-1), w_ref[...])


def _for_pieces(length, fn):
    done = 0
    for rows in MOE_SEG_PIECES:
        n = (length - done) // rows

        def body(p, carry, rows=rows, done=done):
            fn(done + p * rows, rows)
            return carry
        lax.fori_loop(0, n, body, 0)
        done = done + n * rows


def _moe_route_kernel(o0_ref, o1_ref, gate_ref, h_ref, mod_ref, cng_ref, dng_ref, wo_ref, ng_ref, rw_ref, rb_ref,
                      h2_ref, slot_ref, seg_ref, xs_hbm, xn_scr, xg_scr, zero_scr, base_smem, seg_smem, sem, *, tk):
    i = pl.program_id(0)
    n_tiles = pl.num_programs(0)
    pc = MOE_PIECE
    gr = MOE_GATHER_ROWS
    buf = lax.rem(i, 2)

    def segment_copies(slot, issue):
        for e in range(N_EXPERTS):
            off_e = seg_smem[slot, e]
            base_e = seg_smem[slot, 2 * N_EXPERTS + e]
            _for_pieces(seg_smem[slot, N_EXPERTS + e], lambda r, rows, e=e, off_e=off_e, base_e=base_e: issue(
                pltpu.make_async_copy(xg_scr.at[slot, pl.ds(pl.multiple_of(off_e + r, pc), rows), :],
                                      xs_hbm.at[e, pl.ds(pl.multiple_of(base_e + r, pc), rows), :], sem.at[slot])))

    @pl.when(i == 0)
    def _():
        for e in range(N_EXPERTS):
            base_smem[e] = 0
        zero_scr[...] = jnp.zeros_like(zero_scr)

    @pl.when(i >= 2)
    def _():
        segment_copies(buf, lambda cp: cp.wait())

    h2 = _outproj1(o0_ref, o1_ref, gate_ref, h_ref, mod_ref, cng_ref, dng_ref, wo_ref)
    h2_ref[...] = h2
    xm = _normmod(h2, ng_ref[...], mod_ref[3:4, :], mod_ref[4:5, :])
    xn_scr[...] = xm.astype(BF16)
    lg = jnp.transpose(_dot_hilo(xm, rw_ref[...]))[0:N_EXPERTS, :] + rb_ref[...]
    eidx = lax.broadcasted_iota(jnp.int32, lg.shape, 0).astype(F32)
    m1 = jnp.max(lg, axis=0, keepdims=True)
    i1 = jnp.min(jnp.where(lg == m1, eidx, float(N_EXPERTS)), axis=0, keepdims=True)
    lg2 = jnp.where(eidx == i1, -jnp.inf, lg)
    m2 = jnp.max(lg2, axis=0, keepdims=True)
    i2 = jnp.min(jnp.where(lg2 == m2, eidx, float(N_EXPERTS)), axis=0, keepdims=True)
    ex = jnp.exp(m2 - m1)
    p1 = 1.0 / (1.0 + ex)
    sel = jnp.where(eidx == i1, 1.0, 0.0) + jnp.where(eidx == i2, 1.0, 0.0)
    lane = lax.broadcasted_iota(jnp.int32, lg.shape, 1)
    cum = sel
    sh = 1
    while sh < tk:
        cum = cum + jnp.where(lane >= sh, pltpu.roll(cum, sh, 1), 0.0)
        sh *= 2
    padded = jnp.floor((cum[:, tk - 1:tk] + (pc - 1.0)) * (1.0 / pc)) * pc
    padded = jnp.broadcast_to(padded, (N_EXPERTS, 128))
    er = lax.broadcasted_iota(jnp.int32, (N_EXPERTS, N_EXPERTS), 0)
    ec = lax.broadcasted_iota(jnp.int32, (N_EXPERTS, N_EXPERTS), 1)
    off = _dot_hi(jnp.where(er > ec, 1.0, 0.0), padded)
    slot = off[:, 0:1] + cum - 1.0
    slot_a = jnp.sum(jnp.where(eidx == i1, slot, 0.0), axis=0, keepdims=True)
    slot_b = jnp.sum(jnp.where(eidx == i2, slot, 0.0), axis=0, keepdims=True)
    slot_ref[...] = jnp.concatenate([slot_a, slot_b, p1, ex * p1, jnp.zeros((4, tk), F32)], axis=0)

    total = jnp.max(off[N_EXPERTS - 1:N_EXPERTS, :] + padded[N_EXPERTS - 1:N_EXPERTS, :]).astype(jnp.int32)

    def gather(ci, carry):
        r0 = pl.multiple_of(ci * gr, gr)
        rid = (lax.broadcasted_iota(jnp.int32, (gr, tk), 0) + r0).astype(F32)
        p = jnp.where(rid == slot_a, 1.0, 0.0) + jnp.where(rid == slot_b, 1.0, 0.0)
        xg_scr[buf, pl.ds(r0, gr), :] = _dot(p.astype(BF16), xn_scr[...]).astype(BF16)
        return carry
    lax.fori_loop(0, (total + gr - 1) // gr, gather, 0)

    erow = lax.broadcasted_iota(jnp.int32, (N_EXPERTS, 128), 0)
    base_vec = jnp.zeros((N_EXPERTS, 128), F32)
    for e in range(N_EXPERTS):
        len_e = jnp.max(padded[e:e + 1, :]).astype(jnp.int32)
        base_e = base_smem[e]
        base_vec = jnp.where(erow == e, base_e.astype(F32), base_vec)
        seg_smem[buf, e] = jnp.max(off[e:e + 1, :]).astype(jnp.int32)
        seg_smem[buf, N_EXPERTS + e] = len_e
        seg_smem[buf, 2 * N_EXPERTS + e] = base_e
        base_smem[e] = base_e + len_e
    seg_ref[0] = off
    seg_ref[1] = padded
    seg_ref[2] = base_vec
    segment_copies(buf, lambda cp: cp.start())

    @pl.when(i == n_tiles - 1)
    def _():
        @pl.when(i >= 1)
        def _():
            segment_copies(1 - buf, lambda cp: cp.wait())
        segment_copies(buf, lambda cp: cp.wait())

        def tail_copy(e, p):
            end = base_smem[e]
            return pltpu.make_async_copy(zero_scr, xs_hbm.at[e, pl.ds(pl.multiple_of(end + p * pc, pc), pc), :],
                                         sem.at[0])

        def n_tail(e):
            rem = lax.rem(base_smem[e], MOE_BLOCK)
            return jnp.where(rem == 0, 0, MOE_BLOCK - rem) // pc

        for e in range(N_EXPERTS):
            def start(p, carry, e=e):
                tail_copy(e, p).start()
                return carry
            lax.fori_loop(0, n_tail(e), start, 0)
        for e in range(N_EXPERTS):
            def wait(p, carry, e=e):
                tail_copy(e, p).wait()
                return carry
            lax.fori_loop(0, n_tail(e), wait, 0)


def _moe_ffn_kernel(eid_ref, blk_ref, nv_ref, x_ref, wg_ref, wu_ref, wd_ref, o_ref, *, n_chunks):
    del eid_ref, blk_ref

    @pl.when(pl.program_id(0) < nv_ref[0])
    def _():
        x = x_ref[...]
        cw = D_FF // n_chunks
        acc = jnp.zeros(x.shape, F32)
        for ci in range(n_chunks):
            c0 = ci * cw
            act = (_silu(_dot(x, wg_ref[:, c0:c0 + cw])) * _dot(x, wu_ref[:, c0:c0 + cw])).astype(BF16)
            acc = acc + _dot(act, wd_ref[c0:c0 + cw, :])
        o_ref[...] = acc.astype(BF16)


def _moe_combine_kernel(base_ref, len_ref, off_ref, h_ref, mod_ref, fg_ref, slot_ref, og_hbm, out_ref,
                        og_scr, sem, *, tk):
    i = pl.program_id(0)
    pc = MOE_PIECE
    gr = MOE_GATHER_ROWS
    n_rows = og_scr.shape[1]
    buf = lax.rem(i, 2)

    def segments(tile, slot, issue):
        def seg_copy(e, r, rows):
            return pltpu.make_async_copy(
                og_hbm.at[e, pl.ds(pl.multiple_of(base_ref[tile * N_EXPERTS + e] + r, pc), rows), :],
                og_scr.at[slot, pl.ds(pl.multiple_of(off_ref[tile * N_EXPERTS + e] + r, pc), rows), :], sem.at[slot])
        for e in range(N_EXPERTS):
            _for_pieces(len_ref[tile * N_EXPERTS + e], lambda r, rows, e=e: issue(seg_copy(e, r, rows)))

    @pl.when(i == 0)
    def _():
        segments(0, 0, lambda cp: cp.start())

    @pl.when(i + 1 < pl.num_programs(0))
    def _():
        segments(i + 1, 1 - buf, lambda cp: cp.start())

    last = i * N_EXPERTS + N_EXPERTS - 1
    total = off_ref[last] + len_ref[last]

    def clear(p, carry):
        og_scr[buf, pl.ds(pl.multiple_of(p * pc, pc), pc), :] = jnp.zeros((pc, og_scr.shape[2]), BF16)
        return carry
    lax.fori_loop(total // pc, n_rows // pc, clear, 0)
    segments(i, buf, lambda cp: cp.wait())

    out_ref[...] = jnp.zeros_like(out_ref)

    def scatter(ci, carry):
        r0 = pl.multiple_of(ci * gr, gr)
        rid = (lax.broadcasted_iota(jnp.int32, (gr, tk), 0) + r0).astype(F32)
        pg = (jnp.where(rid == slot_ref[0:1, :], slot_ref[2:3, :], 0.0)
              + jnp.where(rid == slot_ref[1:2, :], slot_ref[3:4, :], 0.0))
        out_ref[...] = out_ref[...] + _dot_tn(pg.astype(BF16), og_scr[buf, pl.ds(r0, gr), :])
        return carry
    lax.fori_loop(0, (total + gr - 1) // gr, scatter, 0)
    h3 = h_ref[...] + mod_ref[5:6, :] * out_ref[...]
    out_ref[...] = h3 * lax.rsqrt(jnp.mean(h3 * h3, axis=-1, keepdims=True) + EPS) * fg_ref[...]


def _moe_block_table(seg, n_blocks):
    ends = (seg[-1, 2, :, 0] + seg[-1, 1, :, 0]).astype(jnp.int32)
    nblk = (ends + MOE_BLOCK - 1) // MOE_BLOCK
    cum = jnp.cumsum(nblk)
    n_valid = cum[-1]
    g = jnp.minimum(jnp.arange(n_blocks, dtype=jnp.int32), n_valid - 1)
    eid = jnp.sum((g[:, None] >= cum[None, :]).astype(jnp.int32), axis=1)
    blk = g - (cum - nblk)[eid]
    return eid, blk, n_valid.reshape(1)


def _moe(o0, o1, gates, gate_blk, h, modtab, cng, dng, wo, ng, fg, rw, rb, wg, wu, wd, *, tk):
    bsz, seq, d = h.shape
    tpb = seq // tk
    n_tiles = bsz * tpb
    n_tok = bsz * seq
    tile_rows = -(-(2 * tk + N_EXPERTS * MOE_PIECE) // MOE_GATHER_ROWS) * MOE_GATHER_ROWS
    cap = -(-(n_tok + n_tiles * MOE_PIECE) // MOE_BLOCK) * MOE_BLOCK
    n_blocks = -(-(2 * n_tok + n_tiles * N_EXPERTS * MOE_PIECE) // MOE_BLOCK) + N_EXPERTS
    tok = lambda i, *_: (i // tpb, i % tpb, 0)
    mod = lambda i, *_: (i // tpb, 1, 0, 0)

    tok_spec = pl.BlockSpec((None, tk, d), tok)
    h2, slots, seg, xs = pl.pallas_call(
        functools.partial(_moe_route_kernel, tk=tk),
        grid=(n_tiles,),
        in_specs=[tok_spec, tok_spec,
                  pl.BlockSpec((None, tk, d), lambda i: (i // tpb, i % tpb, gate_blk)),
                  tok_spec, pl.BlockSpec((None, None, 6, d), mod),
                  _const_spec((1, 128)), _const_spec((1, 128)), _const_spec((d, d)),
                  _const_spec((1, d)), _const_spec((d, 128)), _const_spec((N_EXPERTS, 1))],
        out_specs=[tok_spec,
                   pl.BlockSpec((None, 8, tk), lambda i: (i, 0, 0)),
                   pl.BlockSpec((None, 3, N_EXPERTS, 128), lambda i: (i, 0, 0, 0)),
                   pl.BlockSpec(memory_space=pl.ANY)],
        out_shape=[jax.ShapeDtypeStruct(h.shape, F32),
                   jax.ShapeDtypeStruct((n_tiles, 8, tk), F32),
                   jax.ShapeDtypeStruct((n_tiles, 3, N_EXPERTS, 128), F32),
                   jax.ShapeDtypeStruct((N_EXPERTS, cap, d), BF16)],
        scratch_shapes=[pltpu.VMEM((tk, d), BF16), pltpu.VMEM((2, tile_rows, d), BF16), pltpu.VMEM((MOE_PIECE, d), BF16),
                        pltpu.SMEM((N_EXPERTS,), jnp.int32), pltpu.SMEM((2, 3 * N_EXPERTS), jnp.int32),
                        pltpu.SemaphoreType.DMA((2,))],
        compiler_params=_params(("arbitrary",), 40 << 20),
        name="l1_moe_route",
    )(o0, o1, gates, h, modtab, cng, dng, wo, ng, _pad_cols(rw, 128), rb)

    eid, blk, n_valid = _moe_block_table(seg, n_blocks)
    x_spec = pl.BlockSpec((None, MOE_BLOCK, d), lambda g, eid, blk, nv: (eid[g], blk[g], 0))
    og = pl.pallas_call(
        functools.partial(_moe_ffn_kernel, n_chunks=11),
        grid_spec=pltpu.PrefetchScalarGridSpec(
            num_scalar_prefetch=3, grid=(n_blocks,),
            in_specs=[x_spec,
                      pl.BlockSpec((None, d, D_FF), lambda g, eid, blk, nv: (eid[g], 0, 0)),
                      pl.BlockSpec((None, d, D_FF), lambda g, eid, blk, nv: (eid[g], 0, 0)),
                      pl.BlockSpec((None, D_FF, d), lambda g, eid, blk, nv: (eid[g], 0, 0))],
            out_specs=x_spec),
        out_shape=jax.ShapeDtypeStruct((N_EXPERTS, cap, d), BF16),
        compiler_params=_params(("arbitrary",), 52 << 20),
        name="l1_moe_experts",
    )(eid, blk, n_valid, xs, wg, wu, wd)

    tab = lambda k: seg[:, k, :, 0].astype(jnp.int32).reshape(-1)
    return pl.pallas_call(
        functools.partial(_moe_combine_kernel, tk=tk),
        grid_spec=pltpu.PrefetchScalarGridSpec(
            num_scalar_prefetch=3, grid=(n_tiles,),
            in_specs=[pl.BlockSpec((None, tk, d), tok), pl.BlockSpec((None, None, 6, d), mod),
                      pl.BlockSpec((1, d), lambda i, *_: (0, 0)),
                      pl.BlockSpec((None, 8, tk), lambda i, *_: (i, 0, 0)),
                      pl.BlockSpec(memory_space=pl.ANY)],
            out_specs=pl.BlockSpec((None, tk, d), tok),
            scratch_shapes=[pltpu.VMEM((2, tile_rows, d), BF16), pltpu.SemaphoreType.DMA((2,))]),
        out_shape=jax.ShapeDtypeStruct(h.shape, F32),
        compiler_params=_params(("arbitrary",), 40 << 20),
        name="l1_moe_combine",
    )(tab(2), tab(1), tab(0), h2, modtab, fg, slots, og)


def _block_diag_gate(gate_w):
    w = gate_w.reshape(2, 2, 2, 4, RG_BLOCK, RG_BLOCK)
    eye = jnp.eye(4, dtype=gate_w.dtype)
    return jnp.einsum('dghbij,bc->dghbicj', w, eye).reshape(2, 2, 2, 256, 256)


def _pad_cols(w, n):
    return jnp.pad(w, ((0, 0), (0, n - w.shape[1])))


def _layer0(ctx, x, modtab, norm_mix_g, norm_ffn_g, e_w_in, e_w_out, e_a_conv_w, e_a_conv_b, e_a_gate_w, e_a_gate_b,
            e_a_lambda, e_b_conv_w, e_b_a_log, e_b_dt_bias, e_b_norm_g, e_ffn_w_gate, e_ffn_w_up, e_ffn_w_down,
            *, tm, tt):
    bsz, ctx_len, d = ctx.shape
    w_in, w_tail = e_w_in.astype(BF16), _pad_cols(e_w_in[:, E_IN_MAIN:], 128).astype(BF16)
    gpar = jnp.zeros((2, 128), F32)
    gpar = gpar.at[0, 2 * DN_HEADS:4 * DN_HEADS].set(e_b_a_log.reshape(-1))
    gpar = gpar.at[1, 2 * DN_HEADS:4 * DN_HEADS].set(e_b_dt_bias.reshape(-1))
    ua, gay, q, k, v, sz, gb = _inproj0(ctx, x, modtab, norm_mix_g.reshape(1, d), w_in, w_tail, e_a_conv_w,
                                        e_a_conv_b.reshape(1, -1), e_b_conv_w, gpar, tm=tm)
    wg = _block_diag_gate(e_a_gate_w).astype(BF16)
    hf, hb = _rglru(ua, wg, e_a_gate_b.reshape(4, RG_WIDTH), e_a_lambda, tt=tt, ctx_len=ctx_len)
    o0, o1 = _delta(q, k, v, gb, ctx_len=ctx_len, rows=SCAN_ROWS)
    return _l0_tail(hf, hb, gay, o0, o1, sz, ctx, x, modtab, e_b_norm_g.reshape(1, -1), e_w_out.astype(BF16),
                    norm_ffn_g.reshape(1, d), e_ffn_w_gate.astype(BF16), e_ffn_w_up.astype(BF16),
                    e_ffn_w_down.astype(BF16), tm=tm)


def _layer1(hc, hl, modtab, norm_mix_g, norm_ffn_g, final_norm_g, o_w_in, o_w_out, o_lb_logits, o_c_norm_g,
            o_d_gate_w2, o_d_gate_b2, o_d_norm_g, o_router_w, o_router_b, o_moe_w_gate, o_moe_w_up, o_moe_w_down,
            *, tm, tk, layer):
    bsz, seq, d = hl.shape
    ctx_len = hc.shape[1]
    rows = seq // GRID_W
    hl = hl.reshape(bsz, rows, GRID_W, d).swapaxes(1, 2).reshape(bsz, seq, d)
    w_in, w_tail = o_w_in.astype(BF16), _pad_cols(o_w_in[:, O_IN_MAIN:], 128).astype(BF16)
    wlr = jnp.zeros((128, SEG), F32)
    wlr = wlr.at[0:GLA_RANK, 0:256].set(o_d_gate_w2[0]).at[GLA_RANK:2 * GLA_RANK, 256:512].set(o_d_gate_w2[1])
    proj = functools.partial(_inproj1, g=norm_mix_g.reshape(1, d), w=w_in, w_tail=w_tail, lbl=o_lb_logits, wlr=wlr,
                             b2=o_d_gate_b2.reshape(1, SEG), layer=layer)
    p1c = proj(hc, 0, ctx_len, modtab, 0, tm=tm)
    p1l = proj(hl, 0, seq, modtab, 1, tm=2 * tm)
    sh0, sg0 = _mix1_ctx(p1c, rows=SCAN_ROWS)
    o0, o1 = _mix1_lat(p1l, sh0, sg0, rows=SCAN_ROWS)
    return _moe(o0, o1, p1l[S_CG[0]], S_CG[1] * SEG // d, hl, modtab, o_c_norm_g.reshape(1, -1),
                o_d_norm_g.reshape(1, -1), o_w_out.astype(BF16), norm_ffn_g.reshape(1, d),
                final_norm_g.reshape(1, d), o_router_w, o_router_b.reshape(N_EXPERTS, 1),
                o_moe_w_gate.astype(BF16), o_moe_w_up.astype(BF16), o_moe_w_down.astype(BF16), tk=tk)


def kernel(x, c, ctx, c_ctx, ada_w, ada_b, norm_mix_g, norm_ffn_g, final_norm_g, e_w_in, e_w_out, e_a_conv_w, e_a_conv_b, e_a_gate_w, e_a_gate_b, e_a_lambda, e_b_conv_w, e_b_a_log, e_b_dt_bias, e_b_norm_g, e_ffn_w_gate, e_ffn_w_up, e_ffn_w_down, o_w_in, o_w_out, o_lb_logits, o_c_norm_g, o_d_gate_w2, o_d_gate_b2, o_d_norm_g, o_router_w, o_router_b, o_moe_w_gate, o_moe_w_up, o_moe_w_down):
    bsz, seq, d = x.shape
    ctx_len = ctx.shape[1]
    assert bsz == 8 and d == D_MODEL and ada_w.shape[0] == 2
    tm = min(256, ctx_len)
    tt = min(128, ctx_len)
    tk = min(512, seq)
    assert ctx_len % tm == 0 and seq % tm == 0 and ctx_len % SCAN_ROWS == 0 and seq % SCAN_ROWS == 0
    assert seq % GRID_W == 0 and seq % tk == 0

    mods = _ada(c, c_ctx, ada_w, ada_b)
    hc, hl = _layer0(ctx, x, _modtab(mods[0], bsz), norm_mix_g[0], norm_ffn_g[0], e_w_in[0], e_w_out[0],
                     e_a_conv_w[0], e_a_conv_b[0], e_a_gate_w[0], e_a_gate_b[0], e_a_lambda[0], e_b_conv_w[0],
                     e_b_a_log[0], e_b_dt_bias[0], e_b_norm_g[0], e_ffn_w_gate[0], e_ffn_w_up[0], e_ffn_w_down[0],
                     tm=tm, tt=tt)
    out_cm = _layer1(hc, hl, _modtab(mods[1], bsz), norm_mix_g[1], norm_ffn_g[1], final_norm_g, o_w_in[0],
                     o_w_out[0], o_lb_logits, o_c_norm_g[0], o_d_gate_w2[0], o_d_gate_b2[0], o_d_norm_g[0],
                     o_router_w[0], o_router_b[0], o_moe_w_gate[0], o_moe_w_up[0], o_moe_w_down[0],
                     tm=tm, tk=tk, layer=1)
    rows = seq // GRID_W
    return out_cm.reshape(bsz, GRID_W, rows, d).swapaxes(1, 2).reshape(bsz, seq, d)
```

```python
import functools

import jax
import jax.numpy as jnp
from jax import lax
from jax.experimental import pallas as pl
from jax.experimental.pallas import tpu as pltpu

F32 = jnp.float32
BF16 = jnp.bfloat16
HI = lax.Precision.HIGHEST

EPS = 1e-6
D_MODEL = 1024
GRID_W = 64
CONV_K = 4
RG_WIDTH = 512
RG_BLOCK = 64
RG_C = 8.0
DN_HEADS = 4
DN_D = 128
DN_CHUNK = 64
HG_HEADS = 4
HG_D = 128
GLA_HEADS = 4
GLA_DK = 64
GLA_DV = 128
GLA_RANK = 16
GLA_GATE_NORM = 16.0
MIX1_CHUNK = 64
SCAN_ROWS = 256
D_FF = 2816
N_EXPERTS = 8

E_IN_MAIN = 3072
E_IN_PAD = E_IN_MAIN + 128
O_IN_MAIN = 4096
O_IN_PAD = O_IN_MAIN + 128
SEG = 512
S_HQ, S_HV, S_GV, S_GQK, S_HK0, S_HK1, S_CG, S_DG = [(0, i) for i in range(8)]
N_SHARED = 4
S_HLF0, S_HLF1, S_GLD = [(1, i) for i in range(3)]
N_SEG = (8, 3)
P1_DTYPES = (BF16, F32)

V7X_VMEM_BYTES = 64 * 1024 * 1024
VMEM_HEADROOM_BYTES = 8 * 1024 * 1024
MOE_PIECE = 16
MOE_SEG_PIECES = (64, MOE_PIECE)
MOE_BLOCK = 512
MOE_GATHER_ROWS = 256


def _vmem(nbytes):
    return int(min(V7X_VMEM_BYTES - VMEM_HEADROOM_BYTES, nbytes))


def _params(sem, vmem_bytes):
    return pltpu.CompilerParams(dimension_semantics=sem, vmem_limit_bytes=_vmem(vmem_bytes))


def _sigmoid(x):
    return jax.nn.sigmoid(x)


def _sigmoid_tanh(x):
    return 0.5 * jnp.tanh(0.5 * x) + 0.5


def _silu(x):
    return x * jax.nn.sigmoid(x)


def _softplus(x):
    return jnp.maximum(x, 0.0) + jnp.log1p(jnp.exp(-jnp.abs(x)))


def _gelu_tanh(x):
    return 0.5 * x * (1.0 + jnp.tanh(0.7978845608028654 * (x + 0.044715 * (x * x * x))))


def _normmod(x, g, shift, scale):
    y = x * lax.rsqrt(jnp.mean(x * x, axis=-1, keepdims=True) + EPS)
    return (y * g) * (1.0 + scale) + shift


def _dot(a, b):
    return jnp.dot(a, b, preferred_element_type=F32)


def _dot_nt(a, b):
    return lax.dot_general(a, b, (((1,), (1,)), ((), ())), preferred_element_type=F32)


def _dot_tn(a, b):
    return lax.dot_general(a, b, (((0,), (0,)), ((), ())), preferred_element_type=F32)


def _dot_hi(a, b):
    return jnp.dot(a, b, precision=HI, preferred_element_type=F32)


def _split3(x):
    hi = x.astype(BF16)
    r1 = x - hi.astype(F32)
    mid = r1.astype(BF16)
    return hi, mid, (r1 - mid.astype(F32)).astype(BF16)


def _dot_hilo(a, b):
    ah = a.astype(BF16)
    al = (a - ah.astype(F32)).astype(BF16)
    bh = b.astype(BF16)
    bl = (b - bh.astype(F32)).astype(BF16)
    return _dot(ah, bh) + (_dot(ah, bl) + _dot(al, bh))


def _mask_dot(mask, x):
    mb = mask.astype(BF16)
    hi, mid, lo = _split3(x)
    return _dot(mb, hi) + (_dot(mb, mid) + _dot(mb, lo))


def _mask_dot_tn(x, mask):
    mb = mask.astype(BF16)
    hi, mid, lo = _split3(x)
    return _dot_tn(hi, mb) + (_dot_tn(mid, mb) + _dot_tn(lo, mb))


def _const_spec(shape):
    nd = len(shape)
    return pl.BlockSpec(shape, lambda *_: (0,) * nd, pipeline_mode=pl.Buffered(1))


def _scan_masks(c, d):
    row = lax.broadcasted_iota(jnp.int32, (c, c), 0)
    col = lax.broadcasted_iota(jnp.int32, (c, c), 1)
    dlt = row - col if d == 0 else col - row
    return dlt >= 0, dlt > 0, dlt <= 0, row == col


def _ada_kernel(cv_ref, w_ref, b_ref, o_ref):
    s = _silu(cv_ref[...]).astype(BF16)
    o_ref[...] = _dot(s, w_ref[...].astype(BF16)) + b_ref[...]


def _ada(c, c_ctx, ada_w, ada_b):
    depth, d, n6 = ada_w.shape
    bsz = c.shape[0]
    rows = 16
    cv = jnp.zeros((rows, d), F32).at[:bsz].set(c).at[bsz].set(c_ctx)
    tn = 1536
    return pl.pallas_call(
        _ada_kernel,
        grid=(depth, n6 // tn),
        in_specs=[pl.BlockSpec((rows, d), lambda l, j: (0, 0)),
                  pl.BlockSpec((None, d, tn), lambda l, j: (l, 0, j)),
                  pl.BlockSpec((None, 1, tn), lambda l, j: (l, 0, j))],
        out_specs=pl.BlockSpec((None, rows, tn), lambda l, j: (l, 0, j)),
        out_shape=jax.ShapeDtypeStruct((depth, rows, n6), F32),
        compiler_params=_params(("arbitrary", "arbitrary"), 32 << 20),
        name="ada_mod",
    )(cv, ada_w, ada_b.reshape(depth, 1, n6))


def _modtab(mods_l, bsz):
    m = mods_l.reshape(mods_l.shape[0], 6, D_MODEL)
    lat = m[:bsz]
    ctx = jnp.broadcast_to(m[bsz][None], (bsz, 6, D_MODEL))
    return jnp.stack([ctx, lat], axis=1)


def _inproj0_kernel(cp_ref, cm_ref, cn_ref, xp_ref, xm_ref, xn_ref, mod_ref, g_ref, w_ref, wt_ref, acw_ref, acb_ref,
                    bcw_ref, gpar_ref, ua_ref, gay_ref, q_ref, k_ref, v_ref, sz_ref, gb_ref, u_scr,
                    *, tm, ctx_tiles, n_tiles):
    t = pl.program_id(1)
    pick = lambda c_ref, x_ref: jnp.where(t < ctx_tiles, c_ref[...], x_ref[...])
    x = jnp.concatenate([pick(cp_ref, xp_ref), pick(cm_ref, xm_ref), pick(cn_ref, xn_ref)], axis=0)
    xm = _normmod(x, g_ref[...], mod_ref[0:1, :], mod_ref[1:2, :]).astype(BF16)
    seg_first = jnp.logical_or(t == 0, t == ctx_tiles)
    seg_last = jnp.logical_or(t == ctx_tiles - 1, t == n_tiles - 1)

    def project(c0, width, conv_input):
        u_scr[:, c0:c0 + width] = _dot(xm, wt_ref[...] if c0 == E_IN_MAIN else w_ref[:, c0:c0 + width])
        if conv_input:
            u_scr[0:8, c0:c0 + width] = jnp.where(seg_first, 0.0, u_scr[0:8, c0:c0 + width])
            u_scr[tm + 8:tm + 16, c0:c0 + width] = jnp.where(seg_last, 0.0, u_scr[tm + 8:tm + 16, c0:c0 + width])

    def conv(c0, width, w_ref_, w0):
        acc = u_scr[6:6 + tm, c0:c0 + width] * w_ref_[0:1, w0:w0 + width]
        for j in range(1, CONV_K):
            acc = acc + u_scr[6 + j:6 + j + tm, c0:c0 + width] * w_ref_[j:j + 1, w0:w0 + width]
        return acc

    project(0, RG_WIDTH, True)
    for grp in range(RG_WIDTH // 128):
        c0 = grp * 128
        ua_ref[:, c0:c0 + 128] = conv(c0, 128, acw_ref, c0) + acb_ref[0:1, c0:c0 + 128]
    project(512, 512, False)
    gay_ref[...] = _gelu_tanh(u_scr[8:8 + tm, 512:1024]).astype(BF16)

    for grp in range(3 * DN_HEADS):
        c0 = grp * 128
        if grp % DN_HEADS == 0:
            project(1024 + c0, DN_HEADS * DN_D, True)
        y = _silu(conv(1024 + c0, 128, bcw_ref, c0))
        if grp < 2 * DN_HEADS:
            y = y * lax.rsqrt(jnp.sum(y * y, axis=-1, keepdims=True) + EPS)
        if grp < DN_HEADS:
            q_ref[:, c0:c0 + 128] = (y * (DN_D ** -0.5)).astype(BF16)
        elif grp < 2 * DN_HEADS:
            k_ref[:, c0 - 512:c0 - 384] = y.astype(BF16)
        else:
            v_ref[:, c0 - 1024:c0 - 896] = y.astype(BF16)
    project(2560, 512, False)
    sz_ref[...] = _silu(u_scr[8:8 + tm, 2560:3072]).astype(BF16)

    project(3072, 128, False)
    xg = u_scr[8:8 + tm, 3072:3200]
    lane = lax.broadcasted_iota(jnp.int32, xg.shape, 1)
    g = -jnp.exp(gpar_ref[0:1, :]) * _softplus(xg + gpar_ref[1:2, :])
    gb_ref[...] = jnp.where(lane < 2 * DN_HEADS, _sigmoid(xg), g)


def _row_specs(tm, d, ctx_tiles, ctx_len, seq, halo):
    tb = tm // 8

    def specs(n_rows, tile_of):
        main = pl.BlockSpec((None, tm, d), lambda b, t: (b, jnp.clip(tile_of(t), 0, n_rows // tm - 1), 0))
        if not halo:
            return [main]
        prev = pl.BlockSpec((None, 8, d), lambda b, t: (b, jnp.clip(tile_of(t) * tb - 1, 0, n_rows // 8 - 1), 0))
        nxt = pl.BlockSpec((None, 8, d), lambda b, t: (b, jnp.clip((tile_of(t) + 1) * tb, 0, n_rows // 8 - 1), 0))
        return [prev, main, nxt]

    return specs(ctx_len, lambda t: t) + specs(seq, lambda t: t - ctx_tiles)


def _inproj0(ctx, x, modtab, g, w, w_tail, acw, acb, bcw, gpar, *, tm):
    bsz, ctx_len, d = ctx.shape
    seq = x.shape[1]
    t_all = ctx_len + seq
    n_tiles = t_all // tm
    ctx_tiles = ctx_len // tm
    kern = functools.partial(_inproj0_kernel, tm=tm, ctx_tiles=ctx_tiles, n_tiles=n_tiles)
    tok = lambda w, dt=BF16: jax.ShapeDtypeStruct((bsz, t_all, w), dt)
    tok_spec = lambda w: pl.BlockSpec((None, tm, w), lambda b, t: (b, t, 0))
    return pl.pallas_call(
        kern,
        grid=(bsz, n_tiles),
        in_specs=_row_specs(tm, d, ctx_tiles, ctx_len, seq, True) + [
            pl.BlockSpec((None, None, 6, d), lambda b, t: (b, jnp.where(t >= ctx_tiles, 1, 0), 0, 0)),
            _const_spec((1, d)),
            _const_spec(w.shape), _const_spec((d, 128)),
            _const_spec((CONV_K, RG_WIDTH)),
            _const_spec((1, RG_WIDTH)),
            _const_spec((CONV_K, 3 * DN_HEADS * DN_D)),
            _const_spec((2, 128)),
        ],
        out_specs=[tok_spec(512), tok_spec(512), tok_spec(512), tok_spec(512), tok_spec(512), tok_spec(512),
                   tok_spec(128)],
        out_shape=[tok(512, F32), tok(512), tok(512), tok(512), tok(512), tok(512), tok(128, F32)],
        scratch_shapes=[pltpu.VMEM((tm + 16, E_IN_PAD), F32)],
        compiler_params=_params(("arbitrary", "arbitrary"), 40 << 20),
        name="l0_inproj",
    )(ctx, ctx, ctx, x, x, x, modtab, g, w, w_tail, acw, acb, bcw, gpar)


def _rglru_kernel(uf_ref, ub_ref, wg_ref, gbias_ref, lam_ref, hf_ref, hb_ref,
                  af_scr, xf_scr, ab_scr, xb_scr, stg_scr, of_scr, ob_scr, h_scr, *, tt, bsz):
    s = pl.program_id(0)
    n_slab = RG_WIDTH // 128

    @pl.when(s == 0)
    def _():
        h_scr[...] = jnp.zeros_like(h_scr)

    def gates(u_ref, d, a_scr, x_scr):
        for b in range(bsz):
            for j in range(n_slab):
                stg_scr[j, pl.ds(b, tt, stride=bsz), :] = u_ref[b, :, j * 128:(j + 1) * 128]
        x = jnp.concatenate([stg_scr[j] for j in range(n_slab)], axis=1)
        xb = x.astype(BF16)
        for half in range(2):
            c0 = half * 256
            xh = xb[:, c0:c0 + 256]
            r = _sigmoid_tanh(_dot(xh, wg_ref[d, 0, half]) + gbias_ref[2 * d:2 * d + 1, c0:c0 + 256])
            i = _sigmoid_tanh(_dot(xh, wg_ref[d, 1, half]) + gbias_ref[2 * d + 1:2 * d + 2, c0:c0 + 256])
            log_a = (-RG_C) * r * _softplus(-lam_ref[d:d + 1, c0:c0 + 256])
            a = jnp.exp(log_a)
            mult = jnp.sqrt(-jnp.tanh(log_a) * (a * a + 1.0))
            xin = mult * (i * x[:, c0:c0 + 256])
            a_scr[:, :, c0:c0 + 256] = a.reshape(tt, bsz, 256)
            x_scr[:, :, c0:c0 + 256] = xin.reshape(tt, bsz, 256)

    gates(uf_ref, 0, af_scr, xf_scr)
    gates(ub_ref, 1, ab_scr, xb_scr)

    def put(o_scr, t, h):
        rows = pl.ds(pl.multiple_of(t * bsz, bsz), bsz)
        for j in range(n_slab):
            o_scr[j, rows, :] = h[:, j * 128:(j + 1) * 128]

    def step(t, carry):
        hf, hb = carry
        hf = af_scr[t] * hf + xf_scr[t]
        put(of_scr, t, hf)
        tb = tt - 1 - t
        hb = ab_scr[tb] * hb + xb_scr[tb]
        put(ob_scr, tb, hb)
        return hf, hb

    hf, hb = lax.fori_loop(0, tt, step, (h_scr[0], h_scr[1]), unroll=8)
    h_scr[0] = hf
    h_scr[1] = hb
    for o_scr, o_ref in ((of_scr, hf_ref), (ob_scr, hb_ref)):
        for b in range(bsz):
            for j in range(n_slab):
                o_ref[b, :, j * 128:(j + 1) * 128] = o_scr[j, pl.ds(b, tt, stride=bsz), :]


def _rglru(ua, wg, gbias, lam, *, tt, ctx_len):
    bsz, t_all, w = ua.shape
    n_steps = t_all // tt
    nc = ctx_len // tt

    def bwd(s):
        return jnp.where(s < nc, nc - 1 - s, n_steps + nc - 1 - s)

    blk = (bsz, tt, w)
    tm_blk = (tt, bsz, w)
    slabs = (w // 128, tt * bsz, 128)
    kern = functools.partial(_rglru_kernel, tt=tt, bsz=bsz)
    return pl.pallas_call(
        kern,
        grid=(n_steps,),
        in_specs=[pl.BlockSpec(blk, lambda s: (0, s, 0)),
                  pl.BlockSpec(blk, lambda s: (0, bwd(s), 0)),
                  _const_spec(wg.shape), _const_spec(gbias.shape), _const_spec(lam.shape)],
        out_specs=[pl.BlockSpec(blk, lambda s: (0, s, 0)),
                   pl.BlockSpec(blk, lambda s: (0, bwd(s), 0))],
        out_shape=[jax.ShapeDtypeStruct(ua.shape, F32)] * 2,
        scratch_shapes=[pltpu.VMEM(tm_blk, F32)] * 4 + [pltpu.VMEM(slabs, F32)] * 3 + [pltpu.VMEM((2, bsz, w), F32)],
        compiler_params=_params(("arbitrary",), 48 << 20),
        name="l0_rglru",
    )(ua, ua, wg, gbias, lam)


def _delta_kernel(qf_ref, kf_ref, vf_ref, gf_ref, qb_ref, kb_ref, vb_ref, gb_ref, of_ref, ob_ref, s_scr, *, n_sub):
    c = DN_CHUNK

    @pl.when(pl.program_id(1) == 0)
    def _():
        s_scr[...] = jnp.zeros_like(s_scr)

    dir_refs = ((qf_ref, kf_ref, vf_ref, gf_ref, of_ref), (qb_ref, kb_ref, vb_ref, gb_ref, ob_ref))
    masks = [_scan_masks(c, d) for d in range(2)]
    eye = jnp.where(masks[0][3], 1.0, 0.0)

    cums = {}
    for d in range(2):
        incl, _, incl_t, _ = masks[d]
        m_incl = jnp.where(incl, 1.0, 0.0)
        m_incl_t = jnp.where(incl_t, 1.0, 0.0)
        for ci in range(n_sub):
            g_all = dir_refs[d][3][ci * c:(ci + 1) * c, :]
            gc_all = _mask_dot(m_incl, g_all)
            gct_all = _mask_dot_tn(g_all, m_incl_t)
            cums[d, ci] = (g_all, gc_all, gct_all)

    chains = []
    for d in range(2):
        q_ref, k_ref, v_ref, _, _ = dir_refs[d]
        incl, strict, _, _ = masks[d]
        last = c - 1 if d == 0 else 0
        for ci in range(n_sub):
            g_all, gc_all, gct_all = cums[d, ci]
            rs = slice(ci * c, (ci + 1) * c)
            for h in range(DN_HEADS):
                hs = slice(h * DN_D, (h + 1) * DN_D)
                lane = 2 * DN_HEADS + d * DN_HEADS + h
                ch = dict(d=d, ci=ci, h=h, rs=rs, hs=hs, incl=incl, strict=strict)
                ch["beta"] = g_all[:, d * DN_HEADS + h:d * DN_HEADS + h + 1]
                gc = jnp.broadcast_to(gc_all[:, lane:lane + 1], (c, DN_D))
                gc_row = jnp.broadcast_to(gct_all[lane:lane + 1, :], (c, c))
                ch["gc"] = gc
                ch["gtot"] = gc[last:last + 1, :]
                ch["decay"] = jnp.where(incl, jnp.exp(jnp.minimum(gc[:, 0:c] - gc_row, 0.0)), 0.0)
                ch["e_gc"] = jnp.exp(gc)
                ch["q"] = q_ref[rs, hs].astype(F32)
                ch["k"] = k_ref[rs, hs].astype(F32)
                ch["v"] = v_ref[rs, hs].astype(F32)
                chains.append(ch)

    for ch in chains:
        ch["kb"] = ch["k"] * ch["beta"]
        qk = _dot_nt(jnp.concatenate([ch["kb"], ch["q"]], axis=0).astype(BF16), ch["k"].astype(BF16))
        ch["neg"] = -jnp.where(ch["strict"], qk[0:c] * ch["decay"], 0.0)
        ch["a_qk"] = (qk[c:2 * c] * ch["decay"]).astype(BF16)
    for ch in chains:
        negb = ch["neg"].astype(BF16)
        ch["t"] = eye + ch["neg"]
        ch["p"] = _dot(negb, negb)
    n_sq = max(1, (c - 1).bit_length() - 1)
    for it in range(n_sq):
        for ch in chains:
            tp = _dot(jnp.concatenate([ch["t"], ch["p"]], axis=0).astype(BF16), ch["p"].astype(BF16))
            ch["t"] = ch["t"] + tp[0:c]
            ch["p"] = tp[c:2 * c]
    for ch in chains:
        rhs = jnp.concatenate([ch["v"] * ch["beta"], ch["kb"] * ch["e_gc"]], axis=1).astype(BF16)
        sol = _dot(ch["t"].astype(BF16), rhs)
        ch["u"] = sol[:, 0:DN_D]
        ch["wq"] = jnp.concatenate([sol[:, DN_D:2 * DN_D], ch["q"] * ch["e_gc"]], axis=0).astype(BF16)
        ch["k_tail"] = (ch["k"] * jnp.exp(ch["gtot"] - ch["gc"])).astype(BF16)

    by_key = {(ch["d"], ch["ci"], ch["h"]): ch for ch in chains}
    for step in range(n_sub):
        live = [by_key[d, step if d == 0 else n_sub - 1 - step, h] for d in range(2) for h in range(DN_HEADS)]
        for ch in live:
            ch["st"] = s_scr[ch["d"], ch["h"]]
            ch["ws"] = _dot(ch["wq"], ch["st"].astype(BF16))
        for ch in live:
            vnb = (ch["u"] - ch["ws"][0:c]).astype(BF16)
            o = ch["ws"][c:2 * c] + _dot(ch["a_qk"], vnb)
            dir_refs[ch["d"]][4][ch["rs"], ch["hs"]] = o.astype(BF16)
            s_scr[ch["d"], ch["h"]] = ch["st"] * jnp.exp(ch["gtot"]) + _dot_tn(ch["k_tail"], vnb)


def _delta(q, k, v, gb, *, ctx_len, rows):
    bsz, t_all, w = q.shape
    n_steps = t_all // rows
    nc = ctx_len // rows

    def bwd(s):
        return jnp.where(s < nc, nc - 1 - s, n_steps + nc - 1 - s)

    fwd_spec = lambda width: pl.BlockSpec((None, rows, width), lambda b, s: (b, s, 0))
    bwd_spec = lambda width: pl.BlockSpec((None, rows, width), lambda b, s: (b, bwd(s), 0))
    return pl.pallas_call(
        functools.partial(_delta_kernel, n_sub=rows // DN_CHUNK),
        grid=(bsz, n_steps),
        in_specs=[fwd_spec(w), fwd_spec(w), fwd_spec(w), fwd_spec(128),
                  bwd_spec(w), bwd_spec(w), bwd_spec(w), bwd_spec(128)],
        out_specs=[fwd_spec(w), bwd_spec(w)],
        out_shape=[jax.ShapeDtypeStruct((bsz, t_all, w), BF16)] * 2,
        scratch_shapes=[pltpu.VMEM((2, DN_HEADS, DN_D, DN_D), F32)],
        compiler_params=_params(("arbitrary", "arbitrary"), 32 << 20),
        name="l0_deltanet",
    )(q, k, v, gb, q, k, v, gb)


def _head_norm(y, g):
    return y * lax.rsqrt(jnp.mean(y * y, axis=-1, keepdims=True) + EPS) * g


def _l0_tail_kernel(hf_ref, hb_ref, gay_ref, o0_ref, o1_ref, sz_ref, ctx_ref, x_ref, mod_ref, ng_ref, wo_ref, g_ref,
                    wg_ref, wu_ref, wd_ref, hc_ref, hl_ref, *, n_chunks, ctx_tiles):
    t = pl.program_id(1)
    tm = hf_ref.shape[0]
    halves = [dict(rs=slice(i * tm // 2, (i + 1) * tm // 2)) for i in range(2)]
    for hv in halves:
        rs = hv["rs"]
        parts = [((hf_ref[rs, :] + hb_ref[rs, :]) * gay_ref[rs, :].astype(F32)).astype(BF16)]
        for hd in range(DN_HEADS):
            lo = hd * DN_D
            ob = o0_ref[rs, lo:lo + DN_D].astype(F32) + o1_ref[rs, lo:lo + DN_D].astype(F32)
            parts.append((_head_norm(ob, ng_ref[...]) * sz_ref[rs, lo:lo + DN_D].astype(F32)).astype(BF16))
        hv["ycat"] = jnp.concatenate(parts, axis=-1)
    for hv in halves:
        h = jnp.where(t < ctx_tiles, ctx_ref[hv["rs"], :], x_ref[hv["rs"], :])
        hv["x"] = h + mod_ref[2:3, :] * _dot(hv["ycat"], wo_ref[...])
    for hv in halves:
        hv["xm"] = _normmod(hv["x"], g_ref[...], mod_ref[3:4, :], mod_ref[4:5, :]).astype(BF16)
        hv["acc"] = jnp.zeros(hv["x"].shape, F32)
    cw = D_FF // n_chunks
    for ci in range(n_chunks):
        c0 = ci * cw
        for hv in halves:
            hv["act"] = (_silu(_dot(hv["xm"], wg_ref[:, c0:c0 + cw])) * _dot(hv["xm"], wu_ref[:, c0:c0 + cw])).astype(BF16)
        for hv in halves:
            hv["acc"] = hv["acc"] + _dot(hv["act"], wd_ref[c0:c0 + cw, :])
    out = jnp.concatenate([hv["x"] + mod_ref[5:6, :] * hv["acc"] for hv in halves], axis=0)

    @pl.when(t < ctx_tiles)
    def _():
        hc_ref[...] = out

    @pl.when(t >= ctx_tiles)
    def _():
        hl_ref[...] = out


def _l0_tail(hf, hb, gay, o0, o1, sz, ctx, x, modtab, ng, wo, g, wg, wu, wd, *, tm):
    bsz, ctx_len, d = ctx.shape
    seq = x.shape[1]
    t_all = ctx_len + seq
    ctx_tiles = ctx_len // tm
    tok = lambda width: pl.BlockSpec((None, tm, width), lambda b, t: (b, t, 0))
    return pl.pallas_call(
        functools.partial(_l0_tail_kernel, n_chunks=2, ctx_tiles=ctx_tiles),
        grid=(bsz, t_all // tm),
        in_specs=[tok(512)] * 6 + _row_specs(tm, d, ctx_tiles, ctx_len, seq, False) + [
                  pl.BlockSpec((None, None, 6, d), lambda b, t: (b, jnp.where(t >= ctx_tiles, 1, 0), 0, 0)),
                  _const_spec((1, DN_D)), _const_spec((d, d)),
                  _const_spec((1, d)), _const_spec((d, D_FF)), _const_spec((d, D_FF)), _const_spec((D_FF, d))],
        out_specs=[pl.BlockSpec((None, tm, d), lambda b, t: (b, jnp.minimum(t, ctx_tiles - 1), 0)),
                   pl.BlockSpec((None, tm, d), lambda b, t: (b, jnp.maximum(t - ctx_tiles, 0), 0))],
        out_shape=[jax.ShapeDtypeStruct((bsz, ctx_len, d), F32), jax.ShapeDtypeStruct((bsz, t_all - ctx_len, d), F32)],
        compiler_params=_params(("arbitrary", "arbitrary"), 48 << 20),
        name="l0_tail",
    )(hf, hb, gay, o0, o1, sz, ctx, x, modtab, ng, wo, g, wg, wu, wd)


def _inproj1_kernel(h_ref, mod_ref, g_ref, w_ref, wt_ref, lbl_ref, wlr_ref, b2_ref, pa_ref, pb_ref, u_scr, *, layer):
    x = h_ref[...]
    xm = _normmod(x, g_ref[...], mod_ref[0:1, :], mod_ref[1:2, :]).astype(BF16)

    lg = lbl_ref[...]
    ex = jnp.exp(lg - jnp.max(lg, axis=0, keepdims=True))
    lbw = ex / jnp.sum(ex, axis=0, keepdims=True)
    lb = jnp.sum(lbw[1:layer + 1], axis=0, keepdims=True)

    def project(c0, width):
        u_scr[:, c0:c0 + width] = _dot(xm, wt_ref[...] if c0 == O_IN_MAIN else w_ref[:, c0:c0 + width])

    def put(seg, off, val):
        ref = (pa_ref, pb_ref)[seg[0]]
        ref[:, seg[1] * SEG + off:seg[1] * SEG + off + val.shape[1]] = val.astype(ref.dtype)

    def groups(fn):
        for grp in range(SEG // 128):
            fn(grp * 128)

    project(0, SEG)
    groups(lambda c0: put(S_HQ, c0, _silu(u_scr[:, c0:c0 + 128]) * (HG_D ** -0.5)))
    for dr, (sk, sf) in enumerate(((S_HK0, S_HLF0), (S_HK1, S_HLF1))):
        project(512 + dr * 512, SEG)

        def forget(c0, dr=dr, sk=sk, sf=sf):
            lbg = lb[:, c0:c0 + 128]
            fl = u_scr[:, 512 + dr * 512 + c0:512 + dr * 512 + c0 + 128]
            sg = _sigmoid(fl)
            put(sf, c0, jnp.log(lbg + (1.0 - lbg) * sg))
            put(sk, c0, (1.0 - lbg) * (1.0 - sg))
        groups(forget)
    project(1536, SEG)
    groups(lambda c0: put(S_HV, c0, u_scr[:, 1536 + c0:1536 + c0 + 128]))
    project(2048, SEG)
    groups(lambda c0: put(S_CG, c0, _silu(u_scr[:, 2048 + c0:2048 + c0 + 128])))
    project(2560, SEG)
    put(S_GQK, 0, u_scr[:, 2560:2816] * (GLA_DK ** -0.5))
    put(S_GQK, 256, u_scr[:, 2816:3072])
    project(3072, SEG)
    groups(lambda c0: put(S_GV, c0, u_scr[:, 3072 + c0:3072 + c0 + 128]))
    project(3584, SEG)
    groups(lambda c0: put(S_DG, c0, _silu(u_scr[:, 3584 + c0:3584 + c0 + 128])))
    project(4096, 128)
    lr = u_scr[:, 4096:4224]
    put(S_GLD, 0, -_softplus(-(_dot_hilo(lr, wlr_ref[...]) + b2_ref[...])) * (1.0 / GLA_GATE_NORM))


def _inproj1(h, row0, n_rows, modtab, seg, g, w, w_tail, lbl, wlr, b2, *, tm, layer):
    bsz, _, d = h.shape
    t0 = row0 // tm
    return pl.pallas_call(
        functools.partial(_inproj1_kernel, layer=layer),
        grid=(bsz, n_rows // tm),
        in_specs=[pl.BlockSpec((None, tm, d), lambda b, t: (b, t0 + t, 0)),
                  pl.BlockSpec((None, None, 6, d), lambda b, t: (b, seg, 0, 0)),
                  _const_spec((1, d)), _const_spec(w.shape), _const_spec((d, 128)), _const_spec(lbl.shape),
                  _const_spec((128, SEG)), _const_spec((1, SEG))],
        out_specs=[pl.BlockSpec((None, tm, n * SEG), lambda b, t: (b, t, 0)) for n in N_SEG],
        out_shape=[jax.ShapeDtypeStruct((bsz, n_rows, n * SEG), dt) for n, dt in zip(N_SEG, P1_DTYPES)],
        scratch_shapes=[pltpu.VMEM((tm, O_IN_PAD), F32)],
        compiler_params=_params(("arbitrary", "arbitrary"), 44 << 20),
        name="l1_inproj",
    )(h, modtab, g, w, w_tail, lbl, wlr, b2)


def _gla_stream(d, q_all, k_all, ld_all, v_all, st_ref, o_ref, o_lane0, r0, n_heads, dk, dv, incl, m_incl):
    c = k_all.shape[0]
    mid = c // 2 - 1 if d == 0 else c // 2
    last = c - 1 if d == 0 else 0
    bc = _mask_dot(m_incl, ld_all)
    m = bc[mid:mid + 1]
    btot = bc[last:last + 1]
    kn = k_all.astype(F32) * jnp.exp(m - bc)
    it = dict(d=d, r0=r0, st_ref=st_ref, o_ref=o_ref, o_lane0=o_lane0, n_heads=n_heads, dk=dk, dv=dv, incl=incl,
              c=c, kt=(kn * jnp.exp(btot - m)).astype(BF16), dec=jnp.exp(btot), v=v_all.astype(BF16),
              want_out=q_all is not None)
    if q_all is not None:
        qe = q_all.astype(F32) * jnp.exp(bc)
        it.update(qd=(qe * jnp.exp(-m)).astype(BF16), qe=qe.astype(BF16), knb=kn.astype(BF16))
    return it


def _gla_intra(it):
    dk = it["dk"]
    it["a"] = [jnp.where(it["incl"], _dot_nt(it["qd"][:, hd * dk:(hd + 1) * dk], it["knb"][:, hd * dk:(hd + 1) * dk]),
                         0.0).astype(BF16) for hd in range(it["n_heads"])]


def _gla_advance(it):
    d, dk, dv, c, st_ref = it["d"], it["dk"], it["dv"], it["c"], it["st_ref"]
    sts = [st_ref[d, hd] for hd in range(it["n_heads"])]
    if it["want_out"]:
        for hd in range(it["n_heads"]):
            v = it["v"][:, hd * dv:(hd + 1) * dv]
            o = _dot(it["a"][hd], v) + _dot_nt(it["qe"][:, hd * dk:(hd + 1) * dk], sts[hd].astype(BF16))
            it["o_ref"][it["r0"]:it["r0"] + c, it["o_lane0"] + hd * dv:it["o_lane0"] + (hd + 1) * dv] = o.astype(BF16)
    for hd in range(it["n_heads"]):
        ks = slice(hd * dk, (hd + 1) * dk)
        st_ref[d, hd] = sts[hd] * it["dec"][:, ks] + _dot_tn(it["v"][:, hd * dv:(hd + 1) * dv], it["kt"][:, ks])


def _mix1_body(dirs, n_sub, sh_ref, sg_ref):
    c = MIX1_CHUNK
    gw = GLA_HEADS * GLA_DK
    lane = lambda seg: slice(seg[1] * SEG, (seg[1] + 1) * SEG)
    prepared = {}
    for d, (sh_in, hk_ref, hlf_ref, gld_ref, o_ref) in enumerate(dirs):
        incl = _scan_masks(c, d)[0]
        m_incl = jnp.where(incl, 1.0, 0.0)
        g0 = S_GQK[1] * SEG
        for ci in range(n_sub):
            r0 = ci * c
            rs = slice(r0, r0 + c)
            prepared[d, ci, 0] = _gla_stream(
                d, None if o_ref is None else sh_in[rs, lane(S_HQ)], hk_ref[rs, :], hlf_ref[rs, :],
                sh_in[rs, lane(S_HV)], sh_ref, o_ref, 0, r0, HG_HEADS, HG_D, HG_D, incl, m_incl)
            prepared[d, ci, 1] = _gla_stream(
                d, None if o_ref is None else sh_in[rs, g0:g0 + gw], sh_in[rs, g0 + gw:g0 + 2 * gw],
                gld_ref[rs, d * gw:(d + 1) * gw], sh_in[rs, lane(S_GV)],
                sg_ref, o_ref, HG_HEADS * HG_D, r0, GLA_HEADS, GLA_DK, GLA_DV, incl, m_incl)
    for it in prepared.values():
        if it["want_out"]:
            _gla_intra(it)
    for step in range(n_sub):
        for d in range(2):
            for stream in range(2):
                _gla_advance(prepared[d, step if d == 0 else n_sub - 1 - step, stream])


def _mix1_ctx_kernel(*refs, n_sub):
    fwd, bwd, (sh_ref, sg_ref) = refs[0:4], refs[4:8], refs[8:10]

    @pl.when(pl.program_id(1) == 0)
    def _():
        sh_ref[...] = jnp.zeros_like(sh_ref)
        sg_ref[...] = jnp.zeros_like(sg_ref)

    _mix1_body([tuple(r) + (None,) for r in (fwd, bwd)], n_sub, sh_ref, sg_ref)


def _mix1_lat_kernel(*refs, n_sub):
    fwd, bwd = refs[0:4], refs[4:8]
    sh0_ref, sg0_ref, of_ref, ob_ref, sh_scr, sg_scr = refs[8:14]

    @pl.when(pl.program_id(1) == 0)
    def _():
        sh_scr[...] = sh0_ref[...]
        sg_scr[...] = sg0_ref[...]

    _mix1_body([tuple(fwd) + (of_ref,), tuple(bwd) + (ob_ref,)], n_sub, sh_scr, sg_scr)


def _mix1_specs(p1, rows, n_steps):
    specs, args = [], []
    for d in range(2):
        blk = (lambda b, s: s) if d == 0 else (lambda b, s: n_steps - 1 - s)
        for (arr, sg), n_seg in ((S_HQ, N_SHARED), ((S_HK0, S_HK1)[d], 1), ((S_HLF0, S_HLF1)[d], 1), (S_GLD, 1)):
            assert sg % n_seg == 0
            specs.append(pl.BlockSpec((None, rows, n_seg * SEG),
                                      lambda b, s, blk=blk, sg=sg // n_seg: (b, blk(b, s), sg)))
            args.append(p1[arr])
    return specs, args


_SH_SHAPE = (2, HG_HEADS, HG_D, HG_D)
_SG_SHAPE = (2, GLA_HEADS, GLA_DV, GLA_DK)


def _mix1_ctx(p1c, *, rows):
    bsz, ctx_len, _ = p1c[0].shape
    n_steps = ctx_len // rows
    specs, args = _mix1_specs(p1c, rows, n_steps)
    state = lambda shape: pl.BlockSpec((None,) + shape, lambda b, s: (b, 0, 0, 0, 0))
    return pl.pallas_call(
        functools.partial(_mix1_ctx_kernel, n_sub=rows // MIX1_CHUNK),
        grid=(bsz, n_steps),
        in_specs=specs,
        out_specs=[state(_SH_SHAPE), state(_SG_SHAPE)],
        out_shape=[jax.ShapeDtypeStruct((bsz,) + _SH_SHAPE, F32), jax.ShapeDtypeStruct((bsz,) + _SG_SHAPE, F32)],
        compiler_params=_params(("arbitrary", "arbitrary"), 32 << 20),
        name="l1_ctx_state",
    )(*args)


def _mix1_lat(p1l, sh0, sg0, *, rows):
    bsz, seq, _ = p1l[0].shape
    n_steps = seq // rows
    specs, args = _mix1_specs(p1l, rows, n_steps)
    state = lambda shape: pl.BlockSpec((None,) + shape, lambda b, s: (b, 0, 0, 0, 0))
    ow = HG_HEADS * HG_D + GLA_HEADS * GLA_DV
    return pl.pallas_call(
        functools.partial(_mix1_lat_kernel, n_sub=rows // MIX1_CHUNK),
        grid=(bsz, n_steps),
        in_specs=specs + [state(_SH_SHAPE), state(_SG_SHAPE)],
        out_specs=[pl.BlockSpec((None, rows, ow), lambda b, s: (b, s, 0)),
                   pl.BlockSpec((None, rows, ow), lambda b, s: (b, n_steps - 1 - s, 0))],
        out_shape=[jax.ShapeDtypeStruct((bsz, seq, ow), BF16)] * 2,
        scratch_shapes=[pltpu.VMEM(_SH_SHAPE, F32), pltpu.VMEM(_SG_SHAPE, F32)],
        compiler_params=_params(("arbitrary", "arbitrary"), 52 << 20),
        name="l1_scan",
    )(*args, sh0, sg0)


def _outproj1(o0_ref, o1_ref, gate_ref, h_ref, mod_ref, cng_ref, dng_ref, w_ref):
    parts = []
    for hd in range(HG_HEADS + GLA_HEADS):
        lo = hd * 128
        y = o0_ref[:, lo:lo + 128].astype(F32) + o1_ref[:, lo:lo + 128].astype(F32)
        ng = cng_ref[...] if hd < HG_HEADS else dng_ref[...]
        parts.append((_head_norm(y, ng) * gate_ref[:, lo:lo + 128].astype(F32)).astype(BF16))
    return h_ref[...] + mod_ref[2:3, :] * _dot(jnp.concatenate(parts, axis=-1), w_ref[...])


def _for_pieces(length, fn):
    done = 0
    for rows in MOE_SEG_PIECES:
        n = (length - done) // rows

        def body(p, carry, rows=rows, done=done):
            fn(done + p * rows, rows)
            return carry
        lax.fori_loop(0, n, body, 0)
        done = done + n * rows


def _moe_route_kernel(o0_ref, o1_ref, gate_ref, h_ref, mod_ref, cng_ref, dng_ref, wo_ref, ng_ref, rw_ref, rb_ref,
                      h2_ref, slot_ref, seg_ref, xs_hbm, xn_scr, xg_scr, zero_scr, base_smem, seg_smem, sem, *, tk):
    i = pl.program_id(0)
    n_tiles = pl.num_programs(0)
    pc = MOE_PIECE
    gr = MOE_GATHER_ROWS
    buf = lax.rem(i, 2)

    def segment_copies(slot, issue):
        for e in range(N_EXPERTS):
            off_e = seg_smem[slot, e]
            base_e = seg_smem[slot, 2 * N_EXPERTS + e]
            _for_pieces(seg_smem[slot, N_EXPERTS + e], lambda r, rows, e=e, off_e=off_e, base_e=base_e: issue(
                pltpu.make_async_copy(xg_scr.at[slot, pl.ds(pl.multiple_of(off_e + r, pc), rows), :],
                                      xs_hbm.at[e, pl.ds(pl.multiple_of(base_e + r, pc), rows), :], sem.at[slot])))

    @pl.when(i == 0)
    def _():
        for e in range(N_EXPERTS):
            base_smem[e] = 0
        zero_scr[...] = jnp.zeros_like(zero_scr)

    @pl.when(i >= 2)
    def _():
        segment_copies(buf, lambda cp: cp.wait())

    h2 = _outproj1(o0_ref, o1_ref, gate_ref, h_ref, mod_ref, cng_ref, dng_ref, wo_ref)
    h2_ref[...] = h2
    xm = _normmod(h2, ng_ref[...], mod_ref[3:4, :], mod_ref[4:5, :])
    xn_scr[...] = xm.astype(BF16)
    lg = jnp.transpose(_dot_hilo(xm, rw_ref[...]))[0:N_EXPERTS, :] + rb_ref[...]
    eidx = lax.broadcasted_iota(jnp.int32, lg.shape, 0).astype(F32)
    m1 = jnp.max(lg, axis=0, keepdims=True)
    i1 = jnp.min(jnp.where(lg == m1, eidx, float(N_EXPERTS)), axis=0, keepdims=True)
    lg2 = jnp.where(eidx == i1, -jnp.inf, lg)
    m2 = jnp.max(lg2, axis=0, keepdims=True)
    i2 = jnp.min(jnp.where(lg2 == m2, eidx, float(N_EXPERTS)), axis=0, keepdims=True)
    ex = jnp.exp(m2 - m1)
    p1 = 1.0 / (1.0 + ex)
    sel = jnp.where(eidx == i1, 1.0, 0.0) + jnp.where(eidx == i2, 1.0, 0.0)
    lane = lax.broadcasted_iota(jnp.int32, lg.shape, 1)
    cum = sel
    sh = 1
    while sh < tk:
        cum = cum + jnp.where(lane >= sh, pltpu.roll(cum, sh, 1), 0.0)
        sh *= 2
    padded = jnp.floor((cum[:, tk - 1:tk] + (pc - 1.0)) * (1.0 / pc)) * pc
    padded = jnp.broadcast_to(padded, (N_EXPERTS, 128))
    er = lax.broadcasted_iota(jnp.int32, (N_EXPERTS, N_EXPERTS), 0)
    ec = lax.broadcasted_iota(jnp.int32, (N_EXPERTS, N_EXPERTS), 1)
    off = _dot_hi(jnp.where(er > ec, 1.0, 0.0), padded)
    slot = off[:, 0:1] + cum - 1.0
    slot_a = jnp.sum(jnp.where(eidx == i1, slot, 0.0), axis=0, keepdims=True)
    slot_b = jnp.sum(jnp.where(eidx == i2, slot, 0.0), axis=0, keepdims=True)
    slot_ref[...] = jnp.concatenate([slot_a, slot_b, p1, ex * p1, jnp.zeros((4, tk), F32)], axis=0)

    total = jnp.max(off[N_EXPERTS - 1:N_EXPERTS, :] + padded[N_EXPERTS - 1:N_EXPERTS, :]).astype(jnp.int32)

    def gather(ci, carry):
        r0 = pl.multiple_of(ci * gr, gr)
        rid = (lax.broadcasted_iota(jnp.int32, (gr, tk), 0) + r0).astype(F32)
        p = jnp.where(rid == slot_a, 1.0, 0.0) + jnp.where(rid == slot_b, 1.0, 0.0)
        xg_scr[buf, pl.ds(r0, gr), :] = _dot(p.astype(BF16), xn_scr[...]).astype(BF16)
        return carry
    lax.fori_loop(0, (total + gr - 1) // gr, gather, 0)

    erow = lax.broadcasted_iota(jnp.int32, (N_EXPERTS, 128), 0)
    base_vec = jnp.zeros((N_EXPERTS, 128), F32)
    for e in range(N_EXPERTS):
        len_e = jnp.max(padded[e:e + 1, :]).astype(jnp.int32)
        base_e = base_smem[e]
        base_vec = jnp.where(erow == e, base_e.astype(F32), base_vec)
        seg_smem[buf, e] = jnp.max(off[e:e + 1, :]).astype(jnp.int32)
        seg_smem[buf, N_EXPERTS + e] = len_e
        seg_smem[buf, 2 * N_EXPERTS + e] = base_e
        base_smem[e] = base_e + len_e
    seg_ref[0] = off
    seg_ref[1] = padded
    seg_ref[2] = base_vec
    segment_copies(buf, lambda cp: cp.start())

    @pl.when(i == n_tiles - 1)
    def _():
        @pl.when(i >= 1)
        def _():
            segment_copies(1 - buf, lambda cp: cp.wait())
        segment_copies(buf, lambda cp: cp.wait())

        def tail_copy(e, p):
            end = base_smem[e]
            return pltpu.make_async_copy(zero_scr, xs_hbm.at[e, pl.ds(pl.multiple_of(end + p * pc, pc), pc), :],
                                         sem.at[0])

        def n_tail(e):
            rem = lax.rem(base_smem[e], MOE_BLOCK)
            return jnp.where(rem == 0, 0, MOE_BLOCK - rem) // pc

        for e in range(N_EXPERTS):
            def start(p, carry, e=e):
                tail_copy(e, p).start()
                return carry
            lax.fori_loop(0, n_tail(e), start, 0)
        for e in range(N_EXPERTS):
            def wait(p, carry, e=e):
                tail_copy(e, p).wait()
                return carry
            lax.fori_loop(0, n_tail(e), wait, 0)


def _moe_ffn_kernel(eid_ref, blk_ref, nv_ref, x_ref, wg_ref, wu_ref, wd_ref, o_ref, *, n_chunks):
    del eid_ref, blk_ref

    @pl.when(pl.program_id(0) < nv_ref[0])
    def _():
        x = x_ref[...]
        cw = D_FF // n_chunks
        acc = jnp.zeros(x.shape, F32)
        for ci in range(n_chunks):
            c0 = ci * cw
            act = (_silu(_dot(x, wg_ref[:, c0:c0 + cw])) * _dot(x, wu_ref[:, c0:c0 + cw])).astype(BF16)
            acc = acc + _dot(act, wd_ref[c0:c0 + cw, :])
        o_ref[...] = acc.astype(BF16)


def _moe_combine_kernel(base_ref, len_ref, off_ref, h_ref, mod_ref, fg_ref, slot_ref, og_hbm, out_ref,
                        og_scr, sem, *, tk):
    i = pl.program_id(0)
    pc = MOE_PIECE
    gr = MOE_GATHER_ROWS
    n_rows = og_scr.shape[1]
    buf = lax.rem(i, 2)

    def segments(tile, slot, issue):
        def seg_copy(e, r, rows):
            return pltpu.make_async_copy(
                og_hbm.at[e, pl.ds(pl.multiple_of(base_ref[tile * N_EXPERTS + e] + r, pc), rows), :],
                og_scr.at[slot, pl.ds(pl.multiple_of(off_ref[tile * N_EXPERTS + e] + r, pc), rows), :], sem.at[slot])
        for e in range(N_EXPERTS):
            _for_pieces(len_ref[tile * N_EXPERTS + e], lambda r, rows, e=e: issue(seg_copy(e, r, rows)))

    @pl.when(i == 0)
    def _():
        segments(0, 0, lambda cp: cp.start())

    @pl.when(i + 1 < pl.num_programs(0))
    def _():
        segments(i + 1, 1 - buf, lambda cp: cp.start())

    last = i * N_EXPERTS + N_EXPERTS - 1
    total = off_ref[last] + len_ref[last]

    def clear(p, carry):
        og_scr[buf, pl.ds(pl.multiple_of(p * pc, pc), pc), :] = jnp.zeros((pc, og_scr.shape[2]), BF16)
        return carry
    lax.fori_loop(total // pc, n_rows // pc, clear, 0)
    segments(i, buf, lambda cp: cp.wait())

    out_ref[...] = jnp.zeros_like(out_ref)

    def scatter(ci, carry):
        r0 = pl.multiple_of(ci * gr, gr)
        rid = (lax.broadcasted_iota(jnp.int32, (gr, tk), 0) + r0).astype(F32)
        pg = (jnp.where(rid == slot_ref[0:1, :], slot_ref[2:3, :], 0.0)
              + jnp.where(rid == slot_ref[1:2, :], slot_ref[3:4, :], 0.0))
        out_ref[...] = out_ref[...] + _dot_tn(pg.astype(BF16), og_scr[buf, pl.ds(r0, gr), :])
        return carry
    lax.fori_loop(0, (total + gr - 1) // gr, scatter, 0)
    h3 = h_ref[...] + mod_ref[5:6, :] * out_ref[...]
    out_ref[...] = h3 * lax.rsqrt(jnp.mean(h3 * h3, axis=-1, keepdims=True) + EPS) * fg_ref[...]


def _moe_block_table(seg, n_blocks):
    ends = (seg[-1, 2, :, 0] + seg[-1, 1, :, 0]).astype(jnp.int32)
    nblk = (ends + MOE_BLOCK - 1) // MOE_BLOCK
    cum = jnp.cumsum(nblk)
    n_valid = cum[-1]
    g = jnp.minimum(jnp.arange(n_blocks, dtype=jnp.int32), n_valid - 1)
    eid = jnp.sum((g[:, None] >= cum[None, :]).astype(jnp.int32), axis=1)
    blk = g - (cum - nblk)[eid]
    return eid, blk, n_valid.reshape(1)


def _moe(o0, o1, gates, gate_blk, h, modtab, cng, dng, wo, ng, fg, rw, rb, wg, wu, wd, *, tk):
    bsz, seq, d = h.shape
    tpb = seq // tk
    n_tiles = bsz * tpb
    n_tok = bsz * seq
    tile_rows = -(-(2 * tk + N_EXPERTS * MOE_PIECE) // MOE_GATHER_ROWS) * MOE_GATHER_ROWS
    cap = -(-(n_tok + n_tiles * MOE_PIECE) // MOE_BLOCK) * MOE_BLOCK
    n_blocks = -(-(2 * n_tok + n_tiles * N_EXPERTS * MOE_PIECE) // MOE_BLOCK) + N_EXPERTS
    tok = lambda i, *_: (i // tpb, i % tpb, 0)
    mod = lambda i, *_: (i // tpb, 1, 0, 0)

    tok_spec = pl.BlockSpec((None, tk, d), tok)
    h2, slots, seg, xs = pl.pallas_call(
        functools.partial(_moe_route_kernel, tk=tk),
        grid=(n_tiles,),
        in_specs=[tok_spec, tok_spec,
                  pl.BlockSpec((None, tk, d), lambda i: (i // tpb, i % tpb, gate_blk)),
                  tok_spec, pl.BlockSpec((None, None, 6, d), mod),
                  _const_spec((1, 128)), _const_spec((1, 128)), _const_spec((d, d)),
                  _const_spec((1, d)), _const_spec((d, 128)), _const_spec((N_EXPERTS, 1))],
        out_specs=[tok_spec,
                   pl.BlockSpec((None, 8, tk), lambda i: (i, 0, 0)),
                   pl.BlockSpec((None, 3, N_EXPERTS, 128), lambda i: (i, 0, 0, 0)),
                   pl.BlockSpec(memory_space=pl.ANY)],
        out_shape=[jax.ShapeDtypeStruct(h.shape, F32),
                   jax.ShapeDtypeStruct((n_tiles, 8, tk), F32),
                   jax.ShapeDtypeStruct((n_tiles, 3, N_EXPERTS, 128), F32),
                   jax.ShapeDtypeStruct((N_EXPERTS, cap, d), BF16)],
        scratch_shapes=[pltpu.VMEM((tk, d), BF16), pltpu.VMEM((2, tile_rows, d), BF16), pltpu.VMEM((MOE_PIECE, d), BF16),
                        pltpu.SMEM((N_EXPERTS,), jnp.int32), pltpu.SMEM((2, 3 * N_EXPERTS), jnp.int32),
                        pltpu.SemaphoreType.DMA((2,))],
        compiler_params=_params(("arbitrary",), 40 << 20),
        name="l1_moe_route",
    )(o0, o1, gates, h, modtab, cng, dng, wo, ng, _pad_cols(rw, 128), rb)

    eid, blk, n_valid = _moe_block_table(seg, n_blocks)
    x_spec = pl.BlockSpec((None, MOE_BLOCK, d), lambda g, eid, blk, nv: (eid[g], blk[g], 0))
    og = pl.pallas_call(
        functools.partial(_moe_ffn_kernel, n_chunks=11),
        grid_spec=pltpu.PrefetchScalarGridSpec(
            num_scalar_prefetch=3, grid=(n_blocks,),
            in_specs=[x_spec,
                      pl.BlockSpec((None, d, D_FF), lambda g, eid, blk, nv: (eid[g], 0, 0)),
                      pl.BlockSpec((None, d, D_FF), lambda g, eid, blk, nv: (eid[g], 0, 0)),
                      pl.BlockSpec((None, D_FF, d), lambda g, eid, blk, nv: (eid[g], 0, 0))],
            out_specs=x_spec),
        out_shape=jax.ShapeDtypeStruct((N_EXPERTS, cap, d), BF16),
        compiler_params=_params(("arbitrary",), 52 << 20),
        name="l1_moe_experts",
    )(eid, blk, n_valid, xs, wg, wu, wd)

    tab = lambda k: seg[:, k, :, 0].astype(jnp.int32).reshape(-1)
    return pl.pallas_call(
        functools.partial(_moe_combine_kernel, tk=tk),
        grid_spec=pltpu.PrefetchScalarGridSpec(
            num_scalar_prefetch=3, grid=(n_tiles,),
            in_specs=[pl.BlockSpec((None, tk, d), tok), pl.BlockSpec((None, None, 6, d), mod),
                      pl.BlockSpec((1, d), lambda i, *_: (0, 0)),
                      pl.BlockSpec((None, 8, tk), lambda i, *_: (i, 0, 0)),
                      pl.BlockSpec(memory_space=pl.ANY)],
            out_specs=pl.BlockSpec((None, tk, d), tok),
            scratch_shapes=[pltpu.VMEM((2, tile_rows, d), BF16), pltpu.SemaphoreType.DMA((2,))]),
        out_shape=jax.ShapeDtypeStruct(h.shape, F32),
        compiler_params=_params(("arbitrary",), 40 << 20),
        name="l1_moe_combine",
    )(tab(2), tab(1), tab(0), h2, modtab, fg, slots, og)


def _block_diag_gate(gate_w):
    w = gate_w.reshape(2, 2, 2, 4, RG_BLOCK, RG_BLOCK)
    eye = jnp.eye(4, dtype=gate_w.dtype)
    return jnp.einsum('dghbij,bc->dghbicj', w, eye).reshape(2, 2, 2, 256, 256)


def _pad_cols(w, n):
    return jnp.pad(w, ((0, 0), (0, n - w.shape[1])))


def _layer0(ctx, x, modtab, norm_mix_g, norm_ffn_g, e_w_in, e_w_out, e_a_conv_w, e_a_conv_b, e_a_gate_w, e_a_gate_b,
            e_a_lambda, e_b_conv_w, e_b_a_log, e_b_dt_bias, e_b_norm_g, e_ffn_w_gate, e_ffn_w_up, e_ffn_w_down,
            *, tm, tt):
    bsz, ctx_len, d = ctx.shape
    w_in, w_tail = e_w_in.astype(BF16), _pad_cols(e_w_in[:, E_IN_MAIN:], 128).astype(BF16)
    gpar = jnp.zeros((2, 128), F32)
    gpar = gpar.at[0, 2 * DN_HEADS:4 * DN_HEADS].set(e_b_a_log.reshape(-1))
    gpar = gpar.at[1, 2 * DN_HEADS:4 * DN_HEADS].set(e_b_dt_bias.reshape(-1))
    ua, gay, q, k, v, sz, gb = _inproj0(ctx, x, modtab, norm_mix_g.reshape(1, d), w_in, w_tail, e_a_conv_w,
                                        e_a_conv_b.reshape(1, -1), e_b_conv_w, gpar, tm=tm)
    wg = _block_diag_gate(e_a_gate_w).astype(BF16)
    hf, hb = _rglru(ua, wg, e_a_gate_b.reshape(4, RG_WIDTH), e_a_lambda, tt=tt, ctx_len=ctx_len)
    o0, o1 = _delta(q, k, v, gb, ctx_len=ctx_len, rows=SCAN_ROWS)
    return _l0_tail(hf, hb, gay, o0, o1, sz, ctx, x, modtab, e_b_norm_g.reshape(1, -1), e_w_out.astype(BF16),
                    norm_ffn_g.reshape(1, d), e_ffn_w_gate.astype(BF16), e_ffn_w_up.astype(BF16),
                    e_ffn_w_down.astype(BF16), tm=tm)


def _layer1(hc, hl, modtab, norm_mix_g, norm_ffn_g, final_norm_g, o_w_in, o_w_out, o_lb_logits, o_c_norm_g,
            o_d_gate_w2, o_d_gate_b2, o_d_norm_g, o_router_w, o_router_b, o_moe_w_gate, o_moe_w_up, o_moe_w_down,
            *, tm, tk, layer):
    bsz, seq, d = hl.shape
    ctx_len = hc.shape[1]
    rows = seq // GRID_W
    hl = hl.reshape(bsz, rows, GRID_W, d).swapaxes(1, 2).reshape(bsz, seq, d)
    w_in, w_tail = o_w_in.astype(BF16), _pad_cols(o_w_in[:, O_IN_MAIN:], 128).astype(BF16)
    wlr = jnp.zeros((128, SEG), F32)
    wlr = wlr.at[0:GLA_RANK, 0:256].set(o_d_gate_w2[0]).at[GLA_RANK:2 * GLA_RANK, 256:512].set(o_d_gate_w2[1])
    proj = functools.partial(_inproj1, g=norm_mix_g.reshape(1, d), w=w_in, w_tail=w_tail, lbl=o_lb_logits, wlr=wlr,
                             b2=o_d_gate_b2.reshape(1, SEG), layer=layer)
    p1c = proj(hc, 0, ctx_len, modtab, 0, tm=tm)
    p1l = proj(hl, 0, seq, modtab, 1, tm=2 * tm)
    sh0, sg0 = _mix1_ctx(p1c, rows=SCAN_ROWS)
    o0, o1 = _mix1_lat(p1l, sh0, sg0, rows=2 * SCAN_ROWS)
    return _moe(o0, o1, p1l[S_CG[0]], S_CG[1] * SEG // d, hl, modtab, o_c_norm_g.reshape(1, -1),
                o_d_norm_g.reshape(1, -1), o_w_out.astype(BF16), norm_ffn_g.reshape(1, d),
                final_norm_g.reshape(1, d), o_router_w, o_router_b.reshape(N_EXPERTS, 1),
                o_moe_w_gate.astype(BF16), o_moe_w_up.astype(BF16), o_moe_w_down.astype(BF16), tk=tk)


def kernel(x, c, ctx, c_ctx, ada_w, ada_b, norm_mix_g, norm_ffn_g, final_norm_g, e_w_in, e_w_out, e_a_conv_w, e_a_conv_b, e_a_gate_w, e_a_gate_b, e_a_lambda, e_b_conv_w, e_b_a_log, e_b_dt_bias, e_b_norm_g, e_ffn_w_gate, e_ffn_w_up, e_ffn_w_down, o_w_in, o_w_out, o_lb_logits, o_c_norm_g, o_d_gate_w2, o_d_gate_b2, o_d_norm_g, o_router_w, o_router_b, o_moe_w_gate, o_moe_w_up, o_moe_w_down):
    bsz, seq, d = x.shape
    ctx_len = ctx.shape[1]
    assert bsz == 8 and d == D_MODEL and ada_w.shape[0] == 2
    tm = min(256, ctx_len)
    tt = min(128, ctx_len)
    tk = min(512, seq)
    assert ctx_len % tm == 0 and seq % tm == 0 and ctx_len % SCAN_ROWS == 0 and seq % SCAN_ROWS == 0
    assert seq % GRID_W == 0 and seq % tk == 0

    mods = _ada(c, c_ctx, ada_w, ada_b)
    hc, hl = _layer0(ctx, x, _modtab(mods[0], bsz), norm_mix_g[0], norm_ffn_g[0], e_w_in[0], e_w_out[0],
                     e_a_conv_w[0], e_a_conv_b[0], e_a_gate_w[0], e_a_gate_b[0], e_a_lambda[0], e_b_conv_w[0],
                     e_b_a_log[0], e_b_dt_bias[0], e_b_norm_g[0], e_ffn_w_gate[0], e_ffn_w_up[0], e_ffn_w_down[0],
                     tm=tm, tt=tt)
    out_cm = _layer1(hc, hl, _modtab(mods[1], bsz), norm_mix_g[1], norm_ffn_g[1], final_norm_g, o_w_in[0],
                     o_w_out[0], o_lb_logits, o_c_norm_g[0], o_d_gate_w2[0], o_d_gate_b2[0], o_d_norm_g[0],
                     o_router_w[0], o_router_b[0], o_moe_w_gate[0], o_moe_w_up[0], o_moe_w_down[0],
                     tm=tm, tk=tk, layer=1)
    rows = seq // GRID_W
    return out_cm.reshape(bsz, GRID_W, rows, d).swapaxes(1, 2).reshape(bsz, seq, d)
```

```python
import functools

import jax
import jax.numpy as jnp
from jax import lax
from jax.experimental import pallas as pl
from jax.experimental.pallas import tpu as pltpu

F32 = jnp.float32
BF16 = jnp.bfloat16
HI = lax.Precision.HIGHEST

EPS = 1e-6
D_MODEL = 1024
GRID_W = 64
CONV_K = 4
RG_WIDTH = 512
RG_BLOCK = 64
RG_C = 8.0
DN_HEADS = 4
DN_D = 128
DN_CHUNK = 64
HG_HEADS = 4
HG_D = 128
GLA_HEADS = 4
GLA_DK = 64
GLA_DV = 128
GLA_RANK = 16
GLA_GATE_NORM = 16.0
MIX1_CHUNK = 64
SCAN_ROWS = 256
D_FF = 2816
N_EXPERTS = 8

E_IN_MAIN = 3072
E_IN_PAD = E_IN_MAIN + 128
O_IN_MAIN = 4096
O_IN_PAD = O_IN_MAIN + 128
SEG = 512
S_HQ, S_HV, S_GV, S_GQK, S_HK0, S_HK1, S_CG, S_DG = [(0, i) for i in range(8)]
N_SHARED = 4
S_HLF0, S_HLF1, S_GLD = [(1, i) for i in range(3)]
N_SEG = (8, 3)
P1_DTYPES = (BF16, F32)

V7X_VMEM_BYTES = 64 * 1024 * 1024
VMEM_HEADROOM_BYTES = 8 * 1024 * 1024
MOE_PIECE = 16
MOE_SEG_PIECES = (64, MOE_PIECE)
MOE_BLOCK = 512
MOE_GATHER_ROWS = 256


def _vmem(nbytes):
    return int(min(V7X_VMEM_BYTES - VMEM_HEADROOM_BYTES, nbytes))


def _params(sem, vmem_bytes):
    return pltpu.CompilerParams(dimension_semantics=sem, vmem_limit_bytes=_vmem(vmem_bytes))


def _sigmoid(x):
    return jax.nn.sigmoid(x)


def _sigmoid_tanh(x):
    return 0.5 * jnp.tanh(0.5 * x) + 0.5


def _silu(x):
    return x * jax.nn.sigmoid(x)


def _softplus(x):
    return jnp.maximum(x, 0.0) + jnp.log1p(jnp.exp(-jnp.abs(x)))


def _gelu_tanh(x):
    return 0.5 * x * (1.0 + jnp.tanh(0.7978845608028654 * (x + 0.044715 * (x * x * x))))


def _normmod(x, g, shift, scale):
    y = x * lax.rsqrt(jnp.mean(x * x, axis=-1, keepdims=True) + EPS)
    return (y * g) * (1.0 + scale) + shift


def _dot(a, b):
    return jnp.dot(a, b, preferred_element_type=F32)


def _dot_nt(a, b):
    return lax.dot_general(a, b, (((1,), (1,)), ((), ())), preferred_element_type=F32)


def _dot_tn(a, b):
    return lax.dot_general(a, b, (((0,), (0,)), ((), ())), preferred_element_type=F32)


def _dot_hi(a, b):
    return jnp.dot(a, b, precision=HI, preferred_element_type=F32)


def _split3(x):
    hi = x.astype(BF16)
    r1 = x - hi.astype(F32)
    mid = r1.astype(BF16)
    return hi, mid, (r1 - mid.astype(F32)).astype(BF16)


def _dot_hilo(a, b):
    ah = a.astype(BF16)
    al = (a - ah.astype(F32)).astype(BF16)
    bh = b.astype(BF16)
    bl = (b - bh.astype(F32)).astype(BF16)
    return _dot(ah, bh) + (_dot(ah, bl) + _dot(al, bh))


def _mask_dot(mask, x):
    mb = mask.astype(BF16)
    hi, mid, lo = _split3(x)
    return _dot(mb, hi) + (_dot(mb, mid) + _dot(mb, lo))


def _mask_dot_tn(x, mask):
    mb = mask.astype(BF16)
    hi, mid, lo = _split3(x)
    return _dot_tn(hi, mb) + (_dot_tn(mid, mb) + _dot_tn(lo, mb))


def _const_spec(shape):
    nd = len(shape)
    return pl.BlockSpec(shape, lambda *_: (0,) * nd, pipeline_mode=pl.Buffered(1))


def _scan_masks(c, d):
    row = lax.broadcasted_iota(jnp.int32, (c, c), 0)
    col = lax.broadcasted_iota(jnp.int32, (c, c), 1)
    dlt = row - col if d == 0 else col - row
    return dlt >= 0, dlt > 0, dlt <= 0, row == col


def _ada_kernel(cv_ref, w_ref, b_ref, o_ref):
    s = _silu(cv_ref[...]).astype(BF16)
    o_ref[...] = _dot(s, w_ref[...].astype(BF16)) + b_ref[...]


def _ada(c, c_ctx, ada_w, ada_b):
    depth, d, n6 = ada_w.shape
    bsz = c.shape[0]
    rows = 16
    cv = jnp.zeros((rows, d), F32).at[:bsz].set(c).at[bsz].set(c_ctx)
    tn = 1536
    return pl.pallas_call(
        _ada_kernel,
        grid=(depth, n6 // tn),
        in_specs=[pl.BlockSpec((rows, d), lambda l, j: (0, 0)),
                  pl.BlockSpec((None, d, tn), lambda l, j: (l, 0, j)),
                  pl.BlockSpec((None, 1, tn), lambda l, j: (l, 0, j))],
        out_specs=pl.BlockSpec((None, rows, tn), lambda l, j: (l, 0, j)),
        out_shape=jax.ShapeDtypeStruct((depth, rows, n6), F32),
        compiler_params=_params(("arbitrary", "arbitrary"), 32 << 20),
        name="ada_mod",
    )(cv, ada_w, ada_b.reshape(depth, 1, n6))


def _modtab(mods_l, bsz):
    m = mods_l.reshape(mods_l.shape[0], 6, D_MODEL)
    lat = m[:bsz]
    ctx = jnp.broadcast_to(m[bsz][None], (bsz, 6, D_MODEL))
    return jnp.stack([ctx, lat], axis=1)


def _inproj0_kernel(cp_ref, cm_ref, cn_ref, xp_ref, xm_ref, xn_ref, mod_ref, g_ref, w_ref, wt_ref, acw_ref, acb_ref,
                    bcw_ref, gpar_ref, ua_ref, gay_ref, q_ref, k_ref, v_ref, sz_ref, gb_ref, u_scr,
                    *, tm, ctx_tiles, n_tiles):
    t = pl.program_id(1)
    pick = lambda c_ref, x_ref: jnp.where(t < ctx_tiles, c_ref[...], x_ref[...])
    x = jnp.concatenate([pick(cp_ref, xp_ref), pick(cm_ref, xm_ref), pick(cn_ref, xn_ref)], axis=0)
    xm = _normmod(x, g_ref[...], mod_ref[0:1, :], mod_ref[1:2, :]).astype(BF16)
    seg_first = jnp.logical_or(t == 0, t == ctx_tiles)
    seg_last = jnp.logical_or(t == ctx_tiles - 1, t == n_tiles - 1)

    def project(c0, width, conv_input):
        u_scr[:, c0:c0 + width] = _dot(xm, wt_ref[...] if c0 == E_IN_MAIN else w_ref[:, c0:c0 + width])
        if conv_input:
            u_scr[0:8, c0:c0 + width] = jnp.where(seg_first, 0.0, u_scr[0:8, c0:c0 + width])
            u_scr[tm + 8:tm + 16, c0:c0 + width] = jnp.where(seg_last, 0.0, u_scr[tm + 8:tm + 16, c0:c0 + width])

    def conv(c0, width, w_ref_, w0):
        acc = u_scr[6:6 + tm, c0:c0 + width] * w_ref_[0:1, w0:w0 + width]
        for j in range(1, CONV_K):
            acc = acc + u_scr[6 + j:6 + j + tm, c0:c0 + width] * w_ref_[j:j + 1, w0:w0 + width]
        return acc

    project(0, RG_WIDTH, True)
    for grp in range(RG_WIDTH // 128):
        c0 = grp * 128
        ua_ref[:, c0:c0 + 128] = conv(c0, 128, acw_ref, c0) + acb_ref[0:1, c0:c0 + 128]
    project(512, 512, False)
    gay_ref[...] = _gelu_tanh(u_scr[8:8 + tm, 512:1024]).astype(BF16)

    for grp in range(3 * DN_HEADS):
        c0 = grp * 128
        if grp % DN_HEADS == 0:
            project(1024 + c0, DN_HEADS * DN_D, True)
        y = _silu(conv(1024 + c0, 128, bcw_ref, c0))
        if grp < 2 * DN_HEADS:
            y = y * lax.rsqrt(jnp.sum(y * y, axis=-1, keepdims=True) + EPS)
        if grp < DN_HEADS:
            q_ref[:, c0:c0 + 128] = (y * (DN_D ** -0.5)).astype(BF16)
        elif grp < 2 * DN_HEADS:
            k_ref[:, c0 - 512:c0 - 384] = y.astype(BF16)
        else:
            v_ref[:, c0 - 1024:c0 - 896] = y.astype(BF16)
    project(2560, 512, False)
    sz_ref[...] = _silu(u_scr[8:8 + tm, 2560:3072]).astype(BF16)

    project(3072, 128, False)
    xg = u_scr[8:8 + tm, 3072:3200]
    lane = lax.broadcasted_iota(jnp.int32, xg.shape, 1)
    g = -jnp.exp(gpar_ref[0:1, :]) * _softplus(xg + gpar_ref[1:2, :])
    gb_ref[...] = jnp.where(lane < 2 * DN_HEADS, _sigmoid(xg), g)


def _row_specs(tm, d, ctx_tiles, ctx_len, seq, halo):
    tb = tm // 8

    def specs(n_rows, tile_of):
        main = pl.BlockSpec((None, tm, d), lambda b, t: (b, jnp.clip(tile_of(t), 0, n_rows // tm - 1), 0))
        if not halo:
            return [main]
        prev = pl.BlockSpec((None, 8, d), lambda b, t: (b, jnp.clip(tile_of(t) * tb - 1, 0, n_rows // 8 - 1), 0))
        nxt = pl.BlockSpec((None, 8, d), lambda b, t: (b, jnp.clip((tile_of(t) + 1) * tb, 0, n_rows // 8 - 1), 0))
        return [prev, main, nxt]

    return specs(ctx_len, lambda t: t) + specs(seq, lambda t: t - ctx_tiles)


def _inproj0(ctx, x, modtab, g, w, w_tail, acw, acb, bcw, gpar, *, tm):
    bsz, ctx_len, d = ctx.shape
    seq = x.shape[1]
    t_all = ctx_len + seq
    n_tiles = t_all // tm
    ctx_tiles = ctx_len // tm
    kern = functools.partial(_inproj0_kernel, tm=tm, ctx_tiles=ctx_tiles, n_tiles=n_tiles)
    tok = lambda w, dt=BF16: jax.ShapeDtypeStruct((bsz, t_all, w), dt)
    tok_spec = lambda w: pl.BlockSpec((None, tm, w), lambda b, t: (b, t, 0))
    return pl.pallas_call(
        kern,
        grid=(bsz, n_tiles),
        in_specs=_row_specs(tm, d, ctx_tiles, ctx_len, seq, True) + [
            pl.BlockSpec((None, None, 6, d), lambda b, t: (b, jnp.where(t >= ctx_tiles, 1, 0), 0, 0)),
            _const_spec((1, d)),
            _const_spec(w.shape), _const_spec((d, 128)),
            _const_spec((CONV_K, RG_WIDTH)),
            _const_spec((1, RG_WIDTH)),
            _const_spec((CONV_K, 3 * DN_HEADS * DN_D)),
            _const_spec((2, 128)),
        ],
        out_specs=[tok_spec(512), tok_spec(512), tok_spec(512), tok_spec(512), tok_spec(512), tok_spec(512),
                   tok_spec(128)],
        out_shape=[tok(512, F32), tok(512), tok(512), tok(512), tok(512), tok(512), tok(128, F32)],
        scratch_shapes=[pltpu.VMEM((tm + 16, E_IN_PAD), F32)],
        compiler_params=_params(("arbitrary", "arbitrary"), 40 << 20),
        name="l0_inproj",
    )(ctx, ctx, ctx, x, x, x, modtab, g, w, w_tail, acw, acb, bcw, gpar)


def _rglru_kernel(uf_ref, ub_ref, wg_ref, gbias_ref, lam_ref, hf_ref, hb_ref,
                  af_scr, xf_scr, ab_scr, xb_scr, stg_scr, of_scr, ob_scr, h_scr, *, tt, bsz):
    s = pl.program_id(0)
    n_slab = RG_WIDTH // 128

    @pl.when(s == 0)
    def _():
        h_scr[...] = jnp.zeros_like(h_scr)

    def gates(u_ref, d, a_scr, x_scr):
        for b in range(bsz):
            for j in range(n_slab):
                stg_scr[j, pl.ds(b, tt, stride=bsz), :] = u_ref[b, :, j * 128:(j + 1) * 128]
        x = jnp.concatenate([stg_scr[j] for j in range(n_slab)], axis=1)
        xb = x.astype(BF16)
        for half in range(2):
            c0 = half * 256
            xh = xb[:, c0:c0 + 256]
            r = _sigmoid_tanh(_dot(xh, wg_ref[d, 0, half]) + gbias_ref[2 * d:2 * d + 1, c0:c0 + 256])
            i = _sigmoid_tanh(_dot(xh, wg_ref[d, 1, half]) + gbias_ref[2 * d + 1:2 * d + 2, c0:c0 + 256])
            log_a = (-RG_C) * r * _softplus(-lam_ref[d:d + 1, c0:c0 + 256])
            a = jnp.exp(log_a)
            mult = jnp.sqrt(-jnp.tanh(log_a) * (a * a + 1.0))
            xin = mult * (i * x[:, c0:c0 + 256])
            a_scr[:, :, c0:c0 + 256] = a.reshape(tt, bsz, 256)
            x_scr[:, :, c0:c0 + 256] = xin.reshape(tt, bsz, 256)

    gates(uf_ref, 0, af_scr, xf_scr)
    gates(ub_ref, 1, ab_scr, xb_scr)

    def put(o_scr, t, h):
        rows = pl.ds(pl.multiple_of(t * bsz, bsz), bsz)
        for j in range(n_slab):
            o_scr[j, rows, :] = h[:, j * 128:(j + 1) * 128]

    def step(t, carry):
        hf, hb = carry
        hf = af_scr[t] * hf + xf_scr[t]
        put(of_scr, t, hf)
        tb = tt - 1 - t
        hb = ab_scr[tb] * hb + xb_scr[tb]
        put(ob_scr, tb, hb)
        return hf, hb

    hf, hb = lax.fori_loop(0, tt, step, (h_scr[0], h_scr[1]), unroll=8)
    h_scr[0] = hf
    h_scr[1] = hb
    for o_scr, o_ref in ((of_scr, hf_ref), (ob_scr, hb_ref)):
        for b in range(bsz):
            for j in range(n_slab):
                o_ref[b, :, j * 128:(j + 1) * 128] = o_scr[j, pl.ds(b, tt, stride=bsz), :]


def _rglru(ua, wg, gbias, lam, *, tt, ctx_len):
    bsz, t_all, w = ua.shape
    n_steps = t_all // tt
    nc = ctx_len // tt

    def bwd(s):
        return jnp.where(s < nc, nc - 1 - s, n_steps + nc - 1 - s)

    blk = (bsz, tt, w)
    tm_blk = (tt, bsz, w)
    slabs = (w // 128, tt * bsz, 128)
    kern = functools.partial(_rglru_kernel, tt=tt, bsz=bsz)
    return pl.pallas_call(
        kern,
        grid=(n_steps,),
        in_specs=[pl.BlockSpec(blk, lambda s: (0, s, 0)),
                  pl.BlockSpec(blk, lambda s: (0, bwd(s), 0)),
                  _const_spec(wg.shape), _const_spec(gbias.shape), _const_spec(lam.shape)],
        out_specs=[pl.BlockSpec(blk, lambda s: (0, s, 0)),
                   pl.BlockSpec(blk, lambda s: (0, bwd(s), 0))],
        out_shape=[jax.ShapeDtypeStruct(ua.shape, F32)] * 2,
        scratch_shapes=[pltpu.VMEM(tm_blk, F32)] * 4 + [pltpu.VMEM(slabs, F32)] * 3 + [pltpu.VMEM((2, bsz, w), F32)],
        compiler_params=_params(("arbitrary",), 48 << 20),
        name="l0_rglru",
    )(ua, ua, wg, gbias, lam)


def _delta_kernel(qf_ref, kf_ref, vf_ref, gf_ref, qb_ref, kb_ref, vb_ref, gb_ref, of_ref, ob_ref, s_scr, *, n_sub):
    c = DN_CHUNK

    @pl.when(pl.program_id(1) == 0)
    def _():
        s_scr[...] = jnp.zeros_like(s_scr)

    dir_refs = ((qf_ref, kf_ref, vf_ref, gf_ref, of_ref), (qb_ref, kb_ref, vb_ref, gb_ref, ob_ref))
    masks = [_scan_masks(c, d) for d in range(2)]
    eye = jnp.where(masks[0][3], 1.0, 0.0)

    cums = {}
    for d in range(2):
        incl, _, incl_t, _ = masks[d]
        m_incl = jnp.where(incl, 1.0, 0.0)
        m_incl_t = jnp.where(incl_t, 1.0, 0.0)
        for ci in range(n_sub):
            g_all = dir_refs[d][3][ci * c:(ci + 1) * c, :]
            gc_all = _mask_dot(m_incl, g_all)
            gct_all = _mask_dot_tn(g_all, m_incl_t)
            cums[d, ci] = (g_all, gc_all, gct_all)

    chains = []
    for d in range(2):
        q_ref, k_ref, v_ref, _, _ = dir_refs[d]
        incl, strict, _, _ = masks[d]
        last = c - 1 if d == 0 else 0
        for ci in range(n_sub):
            g_all, gc_all, gct_all = cums[d, ci]
            rs = slice(ci * c, (ci + 1) * c)
            for h in range(DN_HEADS):
                hs = slice(h * DN_D, (h + 1) * DN_D)
                lane = 2 * DN_HEADS + d * DN_HEADS + h
                ch = dict(d=d, ci=ci, h=h, rs=rs, hs=hs, incl=incl, strict=strict)
                ch["beta"] = g_all[:, d * DN_HEADS + h:d * DN_HEADS + h + 1]
                gc = jnp.broadcast_to(gc_all[:, lane:lane + 1], (c, DN_D))
                gc_row = jnp.broadcast_to(gct_all[lane:lane + 1, :], (c, c))
                ch["gc"] = gc
                ch["gtot"] = gc[last:last + 1, :]
                ch["decay"] = jnp.where(incl, jnp.exp(jnp.minimum(gc[:, 0:c] - gc_row, 0.0)), 0.0)
                ch["e_gc"] = jnp.exp(gc)
                ch["q"] = q_ref[rs, hs].astype(F32)
                ch["k"] = k_ref[rs, hs].astype(F32)
                ch["v"] = v_ref[rs, hs].astype(F32)
                chains.append(ch)

    for ch in chains:
        ch["kb"] = ch["k"] * ch["beta"]
        qk = _dot_nt(jnp.concatenate([ch["kb"], ch["q"]], axis=0).astype(BF16), ch["k"].astype(BF16))
        ch["neg"] = -jnp.where(ch["strict"], qk[0:c] * ch["decay"], 0.0)
        ch["a_qk"] = (qk[c:2 * c] * ch["decay"]).astype(BF16)
    for ch in chains:
        negb = ch["neg"].astype(BF16)
        ch["t"] = eye + ch["neg"]
        ch["p"] = _dot(negb, negb)
    n_sq = max(1, (c - 1).bit_length() - 1)
    for it in range(n_sq):
        for ch in chains:
            tp = _dot(jnp.concatenate([ch["t"], ch["p"]], axis=0).astype(BF16), ch["p"].astype(BF16))
            ch["t"] = ch["t"] + tp[0:c]
            ch["p"] = tp[c:2 * c]
    for ch in chains:
        rhs = jnp.concatenate([ch["v"] * ch["beta"], ch["kb"] * ch["e_gc"]], axis=1).astype(BF16)
        sol = _dot(ch["t"].astype(BF16), rhs)
        ch["u"] = sol[:, 0:DN_D]
        ch["wq"] = jnp.concatenate([sol[:, DN_D:2 * DN_D], ch["q"] * ch["e_gc"]], axis=0).astype(BF16)
        ch["k_tail"] = (ch["k"] * jnp.exp(ch["gtot"] - ch["gc"])).astype(BF16)

    by_key = {(ch["d"], ch["ci"], ch["h"]): ch for ch in chains}
    for step in range(n_sub):
        live = [by_key[d, step if d == 0 else n_sub - 1 - step, h] for d in range(2) for h in range(DN_HEADS)]
        for ch in live:
            ch["st"] = s_scr[ch["d"], ch["h"]]
            ch["ws"] = _dot(ch["wq"], ch["st"].astype(BF16))
        for ch in live:
            vnb = (ch["u"] - ch["ws"][0:c]).astype(BF16)
            o = ch["ws"][c:2 * c] + _dot(ch["a_qk"], vnb)
            dir_refs[ch["d"]][4][ch["rs"], ch["hs"]] = o.astype(BF16)
            s_scr[ch["d"], ch["h"]] = ch["st"] * jnp.exp(ch["gtot"]) + _dot_tn(ch["k_tail"], vnb)


def _delta(q, k, v, gb, *, ctx_len, rows):
    bsz, t_all, w = q.shape
    n_steps = t_all // rows
    nc = ctx_len // rows

    def bwd(s):
        return jnp.where(s < nc, nc - 1 - s, n_steps + nc - 1 - s)

    fwd_spec = lambda width: pl.BlockSpec((None, rows, width), lambda b, s: (b, s, 0))
    bwd_spec = lambda width: pl.BlockSpec((None, rows, width), lambda b, s: (b, bwd(s), 0))
    return pl.pallas_call(
        functools.partial(_delta_kernel, n_sub=rows // DN_CHUNK),
        grid=(bsz, n_steps),
        in_specs=[fwd_spec(w), fwd_spec(w), fwd_spec(w), fwd_spec(128),
                  bwd_spec(w), bwd_spec(w), bwd_spec(w), bwd_spec(128)],
        out_specs=[fwd_spec(w), bwd_spec(w)],
        out_shape=[jax.ShapeDtypeStruct((bsz, t_all, w), BF16)] * 2,
        scratch_shapes=[pltpu.VMEM((2, DN_HEADS, DN_D, DN_D), F32)],
        compiler_params=_params(("arbitrary", "arbitrary"), 32 << 20),
        name="l0_deltanet",
    )(q, k, v, gb, q, k, v, gb)


def _head_norm(y, g):
    return y * lax.rsqrt(jnp.mean(y * y, axis=-1, keepdims=True) + EPS) * g


def _l0_tail_kernel(hf_ref, hb_ref, gay_ref, o0_ref, o1_ref, sz_ref, ctx_ref, x_ref, mod_ref, ng_ref, wo_ref, g_ref,
                    wg_ref, wu_ref, wd_ref, hc_ref, hl_ref, *, n_chunks, ctx_tiles):
    t = pl.program_id(1)
    tm = hf_ref.shape[0]
    halves = [dict(rs=slice(i * tm // 2, (i + 1) * tm // 2)) for i in range(2)]
    for hv in halves:
        rs = hv["rs"]
        parts = [((hf_ref[rs, :] + hb_ref[rs, :]) * gay_ref[rs, :].astype(F32)).astype(BF16)]
        for hd in range(DN_HEADS):
            lo = hd * DN_D
            ob = o0_ref[rs, lo:lo + DN_D].astype(F32) + o1_ref[rs, lo:lo + DN_D].astype(F32)
            parts.append((_head_norm(ob, ng_ref[...]) * sz_ref[rs, lo:lo + DN_D].astype(F32)).astype(BF16))
        hv["ycat"] = jnp.concatenate(parts, axis=-1)
    for hv in halves:
        h = jnp.where(t < ctx_tiles, ctx_ref[hv["rs"], :], x_ref[hv["rs"], :])
        hv["x"] = h + mod_ref[2:3, :] * _dot(hv["ycat"], wo_ref[...])
    for hv in halves:
        hv["xm"] = _normmod(hv["x"], g_ref[...], mod_ref[3:4, :], mod_ref[4:5, :]).astype(BF16)
        hv["acc"] = jnp.zeros(hv["x"].shape, F32)
    cw = D_FF // n_chunks
    for ci in range(n_chunks):
        c0 = ci * cw
        for hv in halves:
            hv["act"] = (_silu(_dot(hv["xm"], wg_ref[:, c0:c0 + cw])) * _dot(hv["xm"], wu_ref[:, c0:c0 + cw])).astype(BF16)
        for hv in halves:
            hv["acc"] = hv["acc"] + _dot(hv["act"], wd_ref[c0:c0 + cw, :])
    out = jnp.concatenate([hv["x"] + mod_ref[5:6, :] * hv["acc"] for hv in halves], axis=0)

    @pl.when(t < ctx_tiles)
    def _():
        hc_ref[...] = out

    @pl.when(t >= ctx_tiles)
    def _():
        hl_ref[...] = out


def _l0_tail(hf, hb, gay, o0, o1, sz, ctx, x, modtab, ng, wo, g, wg, wu, wd, *, tm):
    bsz, ctx_len, d = ctx.shape
    seq = x.shape[1]
    t_all = ctx_len + seq
    ctx_tiles = ctx_len // tm
    tok = lambda width: pl.BlockSpec((None, tm, width), lambda b, t: (b, t, 0))
    return pl.pallas_call(
        functools.partial(_l0_tail_kernel, n_chunks=2, ctx_tiles=ctx_tiles),
        grid=(bsz, t_all // tm),
        in_specs=[tok(512)] * 6 + _row_specs(tm, d, ctx_tiles, ctx_len, seq, False) + [
                  pl.BlockSpec((None, None, 6, d), lambda b, t: (b, jnp.where(t >= ctx_tiles, 1, 0), 0, 0)),
                  _const_spec((1, DN_D)), _const_spec((d, d)),
                  _const_spec((1, d)), _const_spec((d, D_FF)), _const_spec((d, D_FF)), _const_spec((D_FF, d))],
        out_specs=[pl.BlockSpec((None, tm, d), lambda b, t: (b, jnp.minimum(t, ctx_tiles - 1), 0)),
                   pl.BlockSpec((None, tm, d), lambda b, t: (b, jnp.maximum(t - ctx_tiles, 0), 0))],
        out_shape=[jax.ShapeDtypeStruct((bsz, ctx_len, d), F32), jax.ShapeDtypeStruct((bsz, t_all - ctx_len, d), F32)],
        compiler_params=_params(("arbitrary", "arbitrary"), 48 << 20),
        name="l0_tail",
    )(hf, hb, gay, o0, o1, sz, ctx, x, modtab, ng, wo, g, wg, wu, wd)


def _inproj1_kernel(h_ref, mod_ref, g_ref, w_ref, wt_ref, lbl_ref, wlr_ref, b2_ref, pa_ref, pb_ref, u_scr, *, layer):
    x = h_ref[...]
    xm = _normmod(x, g_ref[...], mod_ref[0:1, :], mod_ref[1:2, :]).astype(BF16)

    lg = lbl_ref[...]
    ex = jnp.exp(lg - jnp.max(lg, axis=0, keepdims=True))
    lbw = ex / jnp.sum(ex, axis=0, keepdims=True)
    lb = jnp.sum(lbw[1:layer + 1], axis=0, keepdims=True)

    def project(c0, width):
        u_scr[:, c0:c0 + width] = _dot(xm, wt_ref[...] if c0 == O_IN_MAIN else w_ref[:, c0:c0 + width])

    def put(seg, off, val):
        ref = (pa_ref, pb_ref)[seg[0]]
        ref[:, seg[1] * SEG + off:seg[1] * SEG + off + val.shape[1]] = val.astype(ref.dtype)

    def groups(fn):
        for grp in range(SEG // 128):
            fn(grp * 128)

    project(0, SEG)
    groups(lambda c0: put(S_HQ, c0, _silu(u_scr[:, c0:c0 + 128]) * (HG_D ** -0.5)))
    for dr, (sk, sf) in enumerate(((S_HK0, S_HLF0), (S_HK1, S_HLF1))):
        project(512 + dr * 512, SEG)

        def forget(c0, dr=dr, sk=sk, sf=sf):
            lbg = lb[:, c0:c0 + 128]
            fl = u_scr[:, 512 + dr * 512 + c0:512 + dr * 512 + c0 + 128]
            sg = _sigmoid(fl)
            put(sf, c0, jnp.log(lbg + (1.0 - lbg) * sg))
            put(sk, c0, (1.0 - lbg) * (1.0 - sg))
        groups(forget)
    project(1536, SEG)
    groups(lambda c0: put(S_HV, c0, u_scr[:, 1536 + c0:1536 + c0 + 128]))
    project(2048, SEG)
    groups(lambda c0: put(S_CG, c0, _silu(u_scr[:, 2048 + c0:2048 + c0 + 128])))
    project(2560, SEG)
    put(S_GQK, 0, u_scr[:, 2560:2816] * (GLA_DK ** -0.5))
    put(S_GQK, 256, u_scr[:, 2816:3072])
    project(3072, SEG)
    groups(lambda c0: put(S_GV, c0, u_scr[:, 3072 + c0:3072 + c0 + 128]))
    project(3584, SEG)
    groups(lambda c0: put(S_DG, c0, _silu(u_scr[:, 3584 + c0:3584 + c0 + 128])))
    project(4096, 128)
    lr = u_scr[:, 4096:4224]
    put(S_GLD, 0, -_softplus(-(_dot_hilo(lr, wlr_ref[...]) + b2_ref[...])) * (1.0 / GLA_GATE_NORM))


def _inproj1(h, row0, n_rows, modtab, seg, g, w, w_tail, lbl, wlr, b2, *, tm, layer):
    bsz, _, d = h.shape
    t0 = row0 // tm
    return pl.pallas_call(
        functools.partial(_inproj1_kernel, layer=layer),
        grid=(bsz, n_rows // tm),
        in_specs=[pl.BlockSpec((None, tm, d), lambda b, t: (b, t0 + t, 0)),
                  pl.BlockSpec((None, None, 6, d), lambda b, t: (b, seg, 0, 0)),
                  _const_spec((1, d)), _const_spec(w.shape), _const_spec((d, 128)), _const_spec(lbl.shape),
                  _const_spec((128, SEG)), _const_spec((1, SEG))],
        out_specs=[pl.BlockSpec((None, tm, n * SEG), lambda b, t: (b, t, 0)) for n in N_SEG],
        out_shape=[jax.ShapeDtypeStruct((bsz, n_rows, n * SEG), dt) for n, dt in zip(N_SEG, P1_DTYPES)],
        scratch_shapes=[pltpu.VMEM((tm, O_IN_PAD), F32)],
        compiler_params=_params(("arbitrary", "arbitrary"), 44 << 20),
        name="l1_inproj",
    )(h, modtab, g, w, w_tail, lbl, wlr, b2)


def _gla_stream(d, q_all, k_all, ld_all, v_all, st_ref, o_ref, o_lane0, r0, n_heads, dk, dv, incl, m_incl):
    c = k_all.shape[0]
    mid = c // 2 - 1 if d == 0 else c // 2
    last = c - 1 if d == 0 else 0
    bc = _mask_dot(m_incl, ld_all)
    m = bc[mid:mid + 1]
    btot = bc[last:last + 1]
    kn = k_all.astype(F32) * jnp.exp(m - bc)
    it = dict(d=d, r0=r0, st_ref=st_ref, o_ref=o_ref, o_lane0=o_lane0, n_heads=n_heads, dk=dk, dv=dv, incl=incl,
              c=c, kt=(kn * jnp.exp(btot - m)).astype(BF16), dec=jnp.exp(btot), v=v_all.astype(BF16),
              want_out=q_all is not None)
    if q_all is not None:
        qe = q_all.astype(F32) * jnp.exp(bc)
        it.update(qd=(qe * jnp.exp(-m)).astype(BF16), qe=qe.astype(BF16), knb=kn.astype(BF16))
    return it


def _gla_intra(it):
    dk = it["dk"]
    it["a"] = [jnp.where(it["incl"], _dot_nt(it["qd"][:, hd * dk:(hd + 1) * dk], it["knb"][:, hd * dk:(hd + 1) * dk]),
                         0.0).astype(BF16) for hd in range(it["n_heads"])]


def _gla_advance(it):
    d, dk, dv, c, st_ref = it["d"], it["dk"], it["dv"], it["c"], it["st_ref"]
    sts = [st_ref[d, hd] for hd in range(it["n_heads"])]
    if it["want_out"]:
        for hd in range(it["n_heads"]):
            v = it["v"][:, hd * dv:(hd + 1) * dv]
            o = _dot(it["a"][hd], v) + _dot_nt(it["qe"][:, hd * dk:(hd + 1) * dk], sts[hd].astype(BF16))
            it["o_ref"][it["r0"]:it["r0"] + c, it["o_lane0"] + hd * dv:it["o_lane0"] + (hd + 1) * dv] = o.astype(BF16)
    for hd in range(it["n_heads"]):
        ks = slice(hd * dk, (hd + 1) * dk)
        st_ref[d, hd] = sts[hd] * it["dec"][:, ks] + _dot_tn(it["v"][:, hd * dv:(hd + 1) * dv], it["kt"][:, ks])


def _mix1_body(dirs, n_sub, sh_ref, sg_ref):
    c = MIX1_CHUNK
    gw = GLA_HEADS * GLA_DK
    lane = lambda seg: slice(seg[1] * SEG, (seg[1] + 1) * SEG)
    prepared = {}
    for d, (sh_in, hk_ref, hlf_ref, gld_ref, o_ref) in enumerate(dirs):
        incl = _scan_masks(c, d)[0]
        m_incl = jnp.where(incl, 1.0, 0.0)
        g0 = S_GQK[1] * SEG
        for ci in range(n_sub):
            r0 = ci * c
            rs = slice(r0, r0 + c)
            prepared[d, ci, 0] = _gla_stream(
                d, None if o_ref is None else sh_in[rs, lane(S_HQ)], hk_ref[rs, :], hlf_ref[rs, :],
                sh_in[rs, lane(S_HV)], sh_ref, o_ref, 0, r0, HG_HEADS, HG_D, HG_D, incl, m_incl)
            prepared[d, ci, 1] = _gla_stream(
                d, None if o_ref is None else sh_in[rs, g0:g0 + gw], sh_in[rs, g0 + gw:g0 + 2 * gw],
                gld_ref[rs, d * gw:(d + 1) * gw], sh_in[rs, lane(S_GV)],
                sg_ref, o_ref, HG_HEADS * HG_D, r0, GLA_HEADS, GLA_DK, GLA_DV, incl, m_incl)
    for it in prepared.values():
        if it["want_out"]:
            _gla_intra(it)
    for step in range(n_sub):
        for d in range(2):
            for stream in range(2):
                _gla_advance(prepared[d, step if d == 0 else n_sub - 1 - step, stream])


def _mix1_ctx_kernel(*refs, n_sub):
    fwd, bwd, (sh_ref, sg_ref) = refs[0:4], refs[4:8], refs[8:10]

    @pl.when(pl.program_id(1) == 0)
    def _():
        sh_ref[...] = jnp.zeros_like(sh_ref)
        sg_ref[...] = jnp.zeros_like(sg_ref)

    _mix1_body([tuple(r) + (None,) for r in (fwd, bwd)], n_sub, sh_ref, sg_ref)


def _mix1_lat_kernel(*refs, n_sub):
    fwd, bwd = refs[0:4], refs[4:8]
    sh0_ref, sg0_ref, of_ref, ob_ref, sh_scr, sg_scr = refs[8:14]

    @pl.when(pl.program_id(1) == 0)
    def _():
        sh_scr[...] = sh0_ref[...]
        sg_scr[...] = sg0_ref[...]

    _mix1_body([tuple(fwd) + (of_ref,), tuple(bwd) + (ob_ref,)], n_sub, sh_scr, sg_scr)


def _mix1_specs(p1, rows, n_steps):
    specs, args = [], []
    for d in range(2):
        blk = (lambda b, s: s) if d == 0 else (lambda b, s: n_steps - 1 - s)
        for (arr, sg), n_seg in ((S_HQ, N_SHARED), ((S_HK0, S_HK1)[d], 1), ((S_HLF0, S_HLF1)[d], 1), (S_GLD, 1)):
            assert sg % n_seg == 0
            specs.append(pl.BlockSpec((None, rows, n_seg * SEG),
                                      lambda b, s, blk=blk, sg=sg // n_seg: (b, blk(b, s), sg)))
            args.append(p1[arr])
    return specs, args


_SH_SHAPE = (2, HG_HEADS, HG_D, HG_D)
_SG_SHAPE = (2, GLA_HEADS, GLA_DV, GLA_DK)


def _mix1_ctx(p1c, *, rows):
    bsz, ctx_len, _ = p1c[0].shape
    n_steps = ctx_len // rows
    specs, args = _mix1_specs(p1c, rows, n_steps)
    state = lambda shape: pl.BlockSpec((None,) + shape, lambda b, s: (b, 0, 0, 0, 0))
    return pl.pallas_call(
        functools.partial(_mix1_ctx_kernel, n_sub=rows // MIX1_CHUNK),
        grid=(bsz, n_steps),
        in_specs=specs,
        out_specs=[state(_SH_SHAPE), state(_SG_SHAPE)],
        out_shape=[jax.ShapeDtypeStruct((bsz,) + _SH_SHAPE, F32), jax.ShapeDtypeStruct((bsz,) + _SG_SHAPE, F32)],
        compiler_params=_params(("arbitrary", "arbitrary"), 32 << 20),
        name="l1_ctx_state",
    )(*args)


def _mix1_lat(p1l, sh0, sg0, *, rows):
    bsz, seq, _ = p1l[0].shape
    n_steps = seq // rows
    specs, args = _mix1_specs(p1l, rows, n_steps)
    state = lambda shape: pl.BlockSpec((None,) + shape, lambda b, s: (b, 0, 0, 0, 0))
    ow = HG_HEADS * HG_D + GLA_HEADS * GLA_DV
    return pl.pallas_call(
        functools.partial(_mix1_lat_kernel, n_sub=rows // MIX1_CHUNK),
        grid=(bsz, n_steps),
        in_specs=specs + [state(_SH_SHAPE), state(_SG_SHAPE)],
        out_specs=[pl.BlockSpec((None, rows, ow), lambda b, s: (b, s, 0)),
                   pl.BlockSpec((None, rows, ow), lambda b, s: (b, n_steps - 1 - s, 0))],
        out_shape=[jax.ShapeDtypeStruct((bsz, seq, ow), BF16)] * 2,
        scratch_shapes=[pltpu.VMEM(_SH_SHAPE, F32), pltpu.VMEM(_SG_SHAPE, F32)],
        compiler_params=_params(("arbitrary", "arbitrary"), 32 << 20),
        name="l1_scan",
    )(*args, sh0, sg0)


def _outproj1(o0_ref, o1_ref, gate_ref, h_ref, mod_ref, cng_ref, dng_ref, w_ref):
    parts = []
    for hd in range(HG_HEADS + GLA_HEADS):
        lo = hd * 128
        y = o0_ref[:, lo:lo + 128].astype(F32) + o1_ref[:, lo:lo + 128].astype(F32)
        ng = cng_ref[...] if hd < HG_HEADS else dng_ref[...]
        parts.append((_head_norm(y, ng) * gate_ref[:, lo:lo + 128].astype(F32)).astype(BF16))
    return h_ref[...] + mod_ref[2:3, :] * _dot(jnp.concatenate(parts, axis=-1), w_ref[...])


def _for_pieces(length, fn):
    done = 0
    for rows in MOE_SEG_PIECES:
        n = (length - done) // rows

        def body(p, carry, rows=rows, done=done):
            fn(done + p * rows, rows)
            return carry
        lax.fori_loop(0, n, body, 0)
        done = done + n * rows


def _moe_route_kernel(o0_ref, o1_ref, gate_ref, h_ref, mod_ref, cng_ref, dng_ref, wo_ref, ng_ref, rw_ref, rb_ref,
                      h2_ref, slot_ref, seg_ref, xs_hbm, xn_scr, lg_scr, xg_scr, zero_scr, base_smem, seg_smem, sem,
                      *, tk):
    i = pl.program_id(0)
    last_step = pl.num_programs(0) - 1
    pc = MOE_PIECE
    gr = MOE_GATHER_ROWS
    cur = lax.rem(i, 2)
    buf = 1 - cur

    def segment_copies(slot, issue):
        for e in range(N_EXPERTS):
            off_e = seg_smem[slot, e]
            base_e = seg_smem[slot, 2 * N_EXPERTS + e]
            _for_pieces(seg_smem[slot, N_EXPERTS + e], lambda r, rows, e=e, off_e=off_e, base_e=base_e: issue(
                pltpu.make_async_copy(xg_scr.at[slot, pl.ds(pl.multiple_of(off_e + r, pc), rows), :],
                                      xs_hbm.at[e, pl.ds(pl.multiple_of(base_e + r, pc), rows), :], sem.at[slot])))

    @pl.when(i == 0)
    def _():
        for e in range(N_EXPERTS):
            base_smem[e] = 0
        zero_scr[...] = jnp.zeros_like(zero_scr)
        lg_scr[1] = jnp.zeros_like(lg_scr[1])

    @pl.when(i >= 3)
    def _():
        segment_copies(buf, lambda cp: cp.wait())

    lg = lg_scr[buf]
    h2 = _outproj1(o0_ref, o1_ref, gate_ref, h_ref, mod_ref, cng_ref, dng_ref, wo_ref)
    h2_ref[...] = h2
    xm = _normmod(h2, ng_ref[...], mod_ref[3:4, :], mod_ref[4:5, :])
    xn_scr[cur] = xm.astype(BF16)
    lg_scr[cur] = jnp.transpose(_dot_hilo(xm, rw_ref[...]))[0:N_EXPERTS, :] + rb_ref[...]

    routed = jnp.where(i >= 1, 1.0, 0.0)
    eidx = lax.broadcasted_iota(jnp.int32, lg.shape, 0).astype(F32)
    m1 = jnp.max(lg, axis=0, keepdims=True)
    i1 = jnp.min(jnp.where(lg == m1, eidx, float(N_EXPERTS)), axis=0, keepdims=True)
    lg2 = jnp.where(eidx == i1, -jnp.inf, lg)
    m2 = jnp.max(lg2, axis=0, keepdims=True)
    i2 = jnp.min(jnp.where(lg2 == m2, eidx, float(N_EXPERTS)), axis=0, keepdims=True)
    ex = jnp.exp(m2 - m1)
    p1 = 1.0 / (1.0 + ex)
    sel = (jnp.where(eidx == i1, 1.0, 0.0) + jnp.where(eidx == i2, 1.0, 0.0)) * routed
    lane = lax.broadcasted_iota(jnp.int32, lg.shape, 1)
    cum = sel
    sh = 1
    while sh < tk:
        cum = cum + jnp.where(lane >= sh, pltpu.roll(cum, sh, 1), 0.0)
        sh *= 2
    padded = jnp.floor((cum[:, tk - 1:tk] + (pc - 1.0)) * (1.0 / pc)) * pc
    padded = jnp.broadcast_to(padded, (N_EXPERTS, 128))
    er = lax.broadcasted_iota(jnp.int32, (N_EXPERTS, N_EXPERTS), 0)
    ec = lax.broadcasted_iota(jnp.int32, (N_EXPERTS, N_EXPERTS), 1)
    off = _dot_hi(jnp.where(er > ec, 1.0, 0.0), padded)
    slot = off[:, 0:1] + cum - 1.0
    slot_a = jnp.sum(jnp.where(eidx == i1, slot, 0.0), axis=0, keepdims=True)
    slot_b = jnp.sum(jnp.where(eidx == i2, slot, 0.0), axis=0, keepdims=True)
    slot_ref[...] = jnp.concatenate([slot_a, slot_b, p1, ex * p1, jnp.zeros((4, tk), F32)], axis=0)

    total = jnp.max(off[N_EXPERTS - 1:N_EXPERTS, :] + padded[N_EXPERTS - 1:N_EXPERTS, :]).astype(jnp.int32)

    def gather(ci, carry):
        r0 = pl.multiple_of(ci * gr, gr)
        rid = (lax.broadcasted_iota(jnp.int32, (gr, tk), 0) + r0).astype(F32)
        p = jnp.where(rid == slot_a, 1.0, 0.0) + jnp.where(rid == slot_b, 1.0, 0.0)
        xg_scr[buf, pl.ds(r0, gr), :] = _dot(p.astype(BF16), xn_scr[buf]).astype(BF16)
        return carry
    lax.fori_loop(0, (total + gr - 1) // gr, gather, 0)

    erow = lax.broadcasted_iota(jnp.int32, (N_EXPERTS, 128), 0)
    base_vec = jnp.zeros((N_EXPERTS, 128), F32)
    for e in range(N_EXPERTS):
        len_e = jnp.max(padded[e:e + 1, :]).astype(jnp.int32)
        base_e = base_smem[e]
        base_vec = jnp.where(erow == e, base_e.astype(F32), base_vec)
        seg_smem[buf, e] = jnp.max(off[e:e + 1, :]).astype(jnp.int32)
        seg_smem[buf, N_EXPERTS + e] = len_e
        seg_smem[buf, 2 * N_EXPERTS + e] = base_e
        base_smem[e] = base_e + len_e
    seg_ref[0] = off
    seg_ref[1] = padded
    seg_ref[2] = base_vec
    segment_copies(buf, lambda cp: cp.start())

    @pl.when(i == last_step)
    def _():
        @pl.when(i >= 2)
        def _():
            segment_copies(1 - buf, lambda cp: cp.wait())
        segment_copies(buf, lambda cp: cp.wait())

        def tail_copy(e, p):
            end = base_smem[e]
            return pltpu.make_async_copy(zero_scr, xs_hbm.at[e, pl.ds(pl.multiple_of(end + p * pc, pc), pc), :],
                                         sem.at[0])

        def n_tail(e):
            rem = lax.rem(base_smem[e], MOE_BLOCK)
            return jnp.where(rem == 0, 0, MOE_BLOCK - rem) // pc

        for e in range(N_EXPERTS):
            def start(p, carry, e=e):
                tail_copy(e, p).start()
                return carry
            lax.fori_loop(0, n_tail(e), start, 0)
        for e in range(N_EXPERTS):
            def wait(p, carry, e=e):
                tail_copy(e, p).wait()
                return carry
            lax.fori_loop(0, n_tail(e), wait, 0)


def _moe_ffn_kernel(eid_ref, blk_ref, nv_ref, x_ref, wg_ref, wu_ref, wd_ref, o_ref, *, n_chunks):
    del eid_ref, blk_ref

    @pl.when(pl.program_id(0) < nv_ref[0])
    def _():
        x = x_ref[...]
        cw = D_FF // n_chunks
        acc = jnp.zeros(x.shape, F32)
        for ci in range(n_chunks):
            c0 = ci * cw
            act = (_silu(_dot(x, wg_ref[:, c0:c0 + cw])) * _dot(x, wu_ref[:, c0:c0 + cw])).astype(BF16)
            acc = acc + _dot(act, wd_ref[c0:c0 + cw, :])
        o_ref[...] = acc.astype(BF16)


def _moe_combine_kernel(base_ref, len_ref, off_ref, h_ref, mod_ref, fg_ref, slot_ref, og_hbm, out_ref,
                        og_scr, sem, *, tk):
    i = pl.program_id(0)
    pc = MOE_PIECE
    gr = MOE_GATHER_ROWS
    n_rows = og_scr.shape[1]
    buf = lax.rem(i, 2)

    def segments(tile, slot, issue):
        def seg_copy(e, r, rows):
            return pltpu.make_async_copy(
                og_hbm.at[e, pl.ds(pl.multiple_of(base_ref[tile * N_EXPERTS + e] + r, pc), rows), :],
                og_scr.at[slot, pl.ds(pl.multiple_of(off_ref[tile * N_EXPERTS + e] + r, pc), rows), :], sem.at[slot])
        for e in range(N_EXPERTS):
            _for_pieces(len_ref[tile * N_EXPERTS + e], lambda r, rows, e=e: issue(seg_copy(e, r, rows)))

    @pl.when(i == 0)
    def _():
        segments(0, 0, lambda cp: cp.start())

    @pl.when(i + 1 < pl.num_programs(0))
    def _():
        segments(i + 1, 1 - buf, lambda cp: cp.start())

    last = i * N_EXPERTS + N_EXPERTS - 1
    total = off_ref[last] + len_ref[last]

    def clear(p, carry):
        og_scr[buf, pl.ds(pl.multiple_of(p * pc, pc), pc), :] = jnp.zeros((pc, og_scr.shape[2]), BF16)
        return carry
    lax.fori_loop(total // pc, n_rows // pc, clear, 0)
    segments(i, buf, lambda cp: cp.wait())

    out_ref[...] = jnp.zeros_like(out_ref)

    def scatter(ci, carry):
        r0 = pl.multiple_of(ci * gr, gr)
        rid = (lax.broadcasted_iota(jnp.int32, (gr, tk), 0) + r0).astype(F32)
        pg = (jnp.where(rid == slot_ref[0:1, :], slot_ref[2:3, :], 0.0)
              + jnp.where(rid == slot_ref[1:2, :], slot_ref[3:4, :], 0.0))
        out_ref[...] = out_ref[...] + _dot_tn(pg.astype(BF16), og_scr[buf, pl.ds(r0, gr), :])
        return carry
    lax.fori_loop(0, (total + gr - 1) // gr, scatter, 0)
    h3 = h_ref[...] + mod_ref[5:6, :] * out_ref[...]
    out_ref[...] = h3 * lax.rsqrt(jnp.mean(h3 * h3, axis=-1, keepdims=True) + EPS) * fg_ref[...]


def _moe_block_table(seg, n_blocks):
    ends = (seg[-1, 2, :, 0] + seg[-1, 1, :, 0]).astype(jnp.int32)
    nblk = (ends + MOE_BLOCK - 1) // MOE_BLOCK
    cum = jnp.cumsum(nblk)
    n_valid = cum[-1]
    g = jnp.minimum(jnp.arange(n_blocks, dtype=jnp.int32), n_valid - 1)
    eid = jnp.sum((g[:, None] >= cum[None, :]).astype(jnp.int32), axis=1)
    blk = g - (cum - nblk)[eid]
    return eid, blk, n_valid.reshape(1)


def _moe(o0, o1, gates, gate_blk, h, modtab, cng, dng, wo, ng, fg, rw, rb, wg, wu, wd, *, tk):
    bsz, seq, d = h.shape
    tpb = seq // tk
    n_tiles = bsz * tpb
    n_tok = bsz * seq
    tile_rows = -(-(2 * tk + N_EXPERTS * MOE_PIECE) // MOE_GATHER_ROWS) * MOE_GATHER_ROWS
    cap = -(-(n_tok + n_tiles * MOE_PIECE) // MOE_BLOCK) * MOE_BLOCK
    n_blocks = -(-(2 * n_tok + n_tiles * N_EXPERTS * MOE_PIECE) // MOE_BLOCK) + N_EXPERTS
    tok = lambda i, *_: (i // tpb, i % tpb, 0)
    mod = lambda i, *_: (i // tpb, 1, 0, 0)

    proj = lambda i: jnp.minimum(i, n_tiles - 1)
    routed = lambda i: jnp.maximum(i - 1, 0)
    tok_spec = pl.BlockSpec((None, tk, d), lambda i: tok(proj(i)))
    h2, slots, seg, xs = pl.pallas_call(
        functools.partial(_moe_route_kernel, tk=tk),
        grid=(n_tiles + 1,),
        in_specs=[tok_spec, tok_spec,
                  pl.BlockSpec((None, tk, d), lambda i: (proj(i) // tpb, proj(i) % tpb, gate_blk)),
                  tok_spec, pl.BlockSpec((None, None, 6, d), lambda i: mod(proj(i))),
                  _const_spec((1, 128)), _const_spec((1, 128)), _const_spec((d, d)),
                  _const_spec((1, d)), _const_spec((d, 128)), _const_spec((N_EXPERTS, 1))],
        out_specs=[tok_spec,
                   pl.BlockSpec((None, 8, tk), lambda i: (routed(i), 0, 0)),
                   pl.BlockSpec((None, 3, N_EXPERTS, 128), lambda i: (routed(i), 0, 0, 0)),
                   pl.BlockSpec(memory_space=pl.ANY)],
        out_shape=[jax.ShapeDtypeStruct(h.shape, F32),
                   jax.ShapeDtypeStruct((n_tiles, 8, tk), F32),
                   jax.ShapeDtypeStruct((n_tiles, 3, N_EXPERTS, 128), F32),
                   jax.ShapeDtypeStruct((N_EXPERTS, cap, d), BF16)],
        scratch_shapes=[pltpu.VMEM((2, tk, d), BF16), pltpu.VMEM((2, N_EXPERTS, tk), F32),
                        pltpu.VMEM((2, tile_rows, d), BF16), pltpu.VMEM((MOE_PIECE, d), BF16),
                        pltpu.SMEM((N_EXPERTS,), jnp.int32), pltpu.SMEM((2, 3 * N_EXPERTS), jnp.int32),
                        pltpu.SemaphoreType.DMA((2,))],
        compiler_params=_params(("arbitrary",), 40 << 20),
        name="l1_moe_route",
    )(o0, o1, gates, h, modtab, cng, dng, wo, ng, _pad_cols(rw, 128), rb)

    eid, blk, n_valid = _moe_block_table(seg, n_blocks)
    x_spec = pl.BlockSpec((None, MOE_BLOCK, d), lambda g, eid, blk, nv: (eid[g], blk[g], 0))
    og = pl.pallas_call(
        functools.partial(_moe_ffn_kernel, n_chunks=11),
        grid_spec=pltpu.PrefetchScalarGridSpec(
            num_scalar_prefetch=3, grid=(n_blocks,),
            in_specs=[x_spec,
                      pl.BlockSpec((None, d, D_FF), lambda g, eid, blk, nv: (eid[g], 0, 0)),
                      pl.BlockSpec((None, d, D_FF), lambda g, eid, blk, nv: (eid[g], 0, 0)),
                      pl.BlockSpec((None, D_FF, d), lambda g, eid, blk, nv: (eid[g], 0, 0))],
            out_specs=x_spec),
        out_shape=jax.ShapeDtypeStruct((N_EXPERTS, cap, d), BF16),
        compiler_params=_params(("arbitrary",), 52 << 20),
        name="l1_moe_experts",
    )(eid, blk, n_valid, xs, wg, wu, wd)

    tab = lambda k: seg[:, k, :, 0].astype(jnp.int32).reshape(-1)
    return pl.pallas_call(
        functools.partial(_moe_combine_kernel, tk=tk),
        grid_spec=pltpu.PrefetchScalarGridSpec(
            num_scalar_prefetch=3, grid=(n_tiles,),
            in_specs=[pl.BlockSpec((None, tk, d), tok), pl.BlockSpec((None, None, 6, d), mod),
                      pl.BlockSpec((1, d), lambda i, *_: (0, 0)),
                      pl.BlockSpec((None, 8, tk), lambda i, *_: (i, 0, 0)),
                      pl.BlockSpec(memory_space=pl.ANY)],
            out_specs=pl.BlockSpec((None, tk, d), tok),
            scratch_shapes=[pltpu.VMEM((2, tile_rows, d), BF16), pltpu.SemaphoreType.DMA((2,))]),
        out_shape=jax.ShapeDtypeStruct(h.shape, F32),
        compiler_params=_params(("arbitrary",), 40 << 20),
        name="l1_moe_combine",
    )(tab(2), tab(1), tab(0), h2, modtab, fg, slots, og)


def _block_diag_gate(gate_w):
    w = gate_w.reshape(2, 2, 2, 4, RG_BLOCK, RG_BLOCK)
    eye = jnp.eye(4, dtype=gate_w.dtype)
    return jnp.einsum('dghbij,bc->dghbicj', w, eye).reshape(2, 2, 2, 256, 256)


def _pad_cols(w, n):
    return jnp.pad(w, ((0, 0), (0, n - w.shape[1])))


def _layer0(ctx, x, modtab, norm_mix_g, norm_ffn_g, e_w_in, e_w_out, e_a_conv_w, e_a_conv_b, e_a_gate_w, e_a_gate_b,
            e_a_lambda, e_b_conv_w, e_b_a_log, e_b_dt_bias, e_b_norm_g, e_ffn_w_gate, e_ffn_w_up, e_ffn_w_down,
            *, tm, tt):
    bsz, ctx_len, d = ctx.shape
    w_in, w_tail = e_w_in.astype(BF16), _pad_cols(e_w_in[:, E_IN_MAIN:], 128).astype(BF16)
    gpar = jnp.zeros((2, 128), F32)
    gpar = gpar.at[0, 2 * DN_HEADS:4 * DN_HEADS].set(e_b_a_log.reshape(-1))
    gpar = gpar.at[1, 2 * DN_HEADS:4 * DN_HEADS].set(e_b_dt_bias.reshape(-1))
    ua, gay, q, k, v, sz, gb = _inproj0(ctx, x, modtab, norm_mix_g.reshape(1, d), w_in, w_tail, e_a_conv_w,
                                        e_a_conv_b.reshape(1, -1), e_b_conv_w, gpar, tm=tm)
    wg = _block_diag_gate(e_a_gate_w).astype(BF16)
    hf, hb = _rglru(ua, wg, e_a_gate_b.reshape(4, RG_WIDTH), e_a_lambda, tt=tt, ctx_len=ctx_len)
    o0, o1 = _delta(q, k, v, gb, ctx_len=ctx_len, rows=SCAN_ROWS)
    return _l0_tail(hf, hb, gay, o0, o1, sz, ctx, x, modtab, e_b_norm_g.reshape(1, -1), e_w_out.astype(BF16),
                    norm_ffn_g.reshape(1, d), e_ffn_w_gate.astype(BF16), e_ffn_w_up.astype(BF16),
                    e_ffn_w_down.astype(BF16), tm=tm)


def _layer1(hc, hl, modtab, norm_mix_g, norm_ffn_g, final_norm_g, o_w_in, o_w_out, o_lb_logits, o_c_norm_g,
            o_d_gate_w2, o_d_gate_b2, o_d_norm_g, o_router_w, o_router_b, o_moe_w_gate, o_moe_w_up, o_moe_w_down,
            *, tm, tk, layer):
    bsz, seq, d = hl.shape
    ctx_len = hc.shape[1]
    rows = seq // GRID_W
    hl = hl.reshape(bsz, rows, GRID_W, d).swapaxes(1, 2).reshape(bsz, seq, d)
    w_in, w_tail = o_w_in.astype(BF16), _pad_cols(o_w_in[:, O_IN_MAIN:], 128).astype(BF16)
    wlr = jnp.zeros((128, SEG), F32)
    wlr = wlr.at[0:GLA_RANK, 0:256].set(o_d_gate_w2[0]).at[GLA_RANK:2 * GLA_RANK, 256:512].set(o_d_gate_w2[1])
    proj = functools.partial(_inproj1, g=norm_mix_g.reshape(1, d), w=w_in, w_tail=w_tail, lbl=o_lb_logits, wlr=wlr,
                             b2=o_d_gate_b2.reshape(1, SEG), layer=layer)
    p1c = proj(hc, 0, ctx_len, modtab, 0, tm=tm)
    p1l = proj(hl, 0, seq, modtab, 1, tm=2 * tm)
    sh0, sg0 = _mix1_ctx(p1c, rows=SCAN_ROWS)
    o0, o1 = _mix1_lat(p1l, sh0, sg0, rows=SCAN_ROWS)
    return _moe(o0, o1, p1l[S_CG[0]], S_CG[1] * SEG // d, hl, modtab, o_c_norm_g.reshape(1, -1),
                o_d_norm_g.reshape(1, -1), o_w_out.astype(BF16), norm_ffn_g.reshape(1, d),
                final_norm_g.reshape(1, d), o_router_w, o_router_b.reshape(N_EXPERTS, 1),
                o_moe_w_gate.astype(BF16), o_moe_w_up.astype(BF16), o_moe_w_down.astype(BF16), tk=tk)


def kernel(x, c, ctx, c_ctx, ada_w, ada_b, norm_mix_g, norm_ffn_g, final_norm_g, e_w_in, e_w_out, e_a_conv_w, e_a_conv_b, e_a_gate_w, e_a_gate_b, e_a_lambda, e_b_conv_w, e_b_a_log, e_b_dt_bias, e_b_norm_g, e_ffn_w_gate, e_ffn_w_up, e_ffn_w_down, o_w_in, o_w_out, o_lb_logits, o_c_norm_g, o_d_gate_w2, o_d_gate_b2, o_d_norm_g, o_router_w, o_router_b, o_moe_w_gate, o_moe_w_up, o_moe_w_down):
    bsz, seq, d = x.shape
    ctx_len = ctx.shape[1]
    assert bsz == 8 and d == D_MODEL and ada_w.shape[0] == 2
    tm = min(256, ctx_len)
    tt = min(128, ctx_len)
    tk = min(512, seq)
    assert ctx_len % tm == 0 and seq % tm == 0 and ctx_len % SCAN_ROWS == 0 and seq % SCAN_ROWS == 0
    assert seq % GRID_W == 0 and seq % tk == 0

    mods = _ada(c, c_ctx, ada_w, ada_b)
    hc, hl = _layer0(ctx, x, _modtab(mods[0], bsz), norm_mix_g[0], norm_ffn_g[0], e_w_in[0], e_w_out[0],
                     e_a_conv_w[0], e_a_conv_b[0], e_a_gate_w[0], e_a_gate_b[0], e_a_lambda[0], e_b_conv_w[0],
                     e_b_a_log[0], e_b_dt_bias[0], e_b_norm_g[0], e_ffn_w_gate[0], e_ffn_w_up[0], e_ffn_w_down[0],
                     tm=tm, tt=tt)
    out_cm = _layer1(hc, hl, _modtab(mods[1], bsz), norm_mix_g[1], norm_ffn_g[1], final_norm_g, o_w_in[0],
                     o_w_out[0], o_lb_logits, o_c_norm_g[0], o_d_gate_w2[0], o_d_gate_b2[0], o_d_norm_g[0],
                     o_router_w[0], o_router_b[0], o_moe_w_gate[0], o_moe_w_up[0], o_moe_w_down[0],
                     tm=tm, tk=tk, layer=1)
    rows = seq // GRID_W
    return out_cm.reshape(bsz, GRID_W, rows, d).swapaxes(1, 2).reshape(bsz, seq, d)
```

```python
import functools

import jax
import jax.numpy as jnp
from jax import lax
from jax.experimental import pallas as pl
from jax.experimental.pallas import tpu as pltpu

F32 = jnp.float32
BF16 = jnp.bfloat16
HI = lax.Precision.HIGHEST

EPS = 1e-6
D_MODEL = 1024
GRID_W = 64
CONV_K = 4
RG_WIDTH = 512
RG_BLOCK = 64
RG_C = 8.0
DN_HEADS = 4
DN_D = 128
DN_CHUNK = 64
HG_HEADS = 4
HG_D = 128
GLA_HEADS = 4
GLA_DK = 64
GLA_DV = 128
GLA_RANK = 16
GLA_GATE_NORM = 16.0
MIX1_CHUNK = 64
SCAN_ROWS = 256
D_FF = 2816
N_EXPERTS = 8

E_IN_MAIN = 3072
E_IN_PAD = E_IN_MAIN + 128
O_IN_MAIN = 4096
O_IN_PAD = O_IN_MAIN + 128
SEG = 512
S_HQ, S_HV, S_GV, S_GQK, S_HK0, S_HK1, S_CG, S_DG = [(0, i) for i in range(8)]
N_SHARED = 4
S_HLF0, S_HLF1, S_GLD = [(1, i) for i in range(3)]
N_SEG = (8, 3)
P1_DTYPES = (BF16, F32)

V7X_VMEM_BYTES = 64 * 1024 * 1024
VMEM_HEADROOM_BYTES = 8 * 1024 * 1024
MOE_PIECE = 16
MOE_SEG_PIECES = (64, MOE_PIECE)
MOE_BLOCK = 512
MOE_GATHER_ROWS = 256


def _vmem(nbytes):
    return int(min(V7X_VMEM_BYTES - VMEM_HEADROOM_BYTES, nbytes))


def _params(sem, vmem_bytes):
    return pltpu.CompilerParams(dimension_semantics=sem, vmem_limit_bytes=_vmem(vmem_bytes))


def _sigmoid(x):
    return jax.nn.sigmoid(x)


def _sigmoid_tanh(x):
    return 0.5 * jnp.tanh(0.5 * x) + 0.5


def _silu(x):
    return x * jax.nn.sigmoid(x)


def _softplus(x):
    return jnp.maximum(x, 0.0) + jnp.log1p(jnp.exp(-jnp.abs(x)))


def _gelu_tanh(x):
    return 0.5 * x * (1.0 + jnp.tanh(0.7978845608028654 * (x + 0.044715 * (x * x * x))))


def _normmod(x, g, shift, scale):
    y = x * lax.rsqrt(jnp.mean(x * x, axis=-1, keepdims=True) + EPS)
    return (y * g) * (1.0 + scale) + shift


def _dot(a, b):
    return jnp.dot(a, b, preferred_element_type=F32)


def _dot_nt(a, b):
    return lax.dot_general(a, b, (((1,), (1,)), ((), ())), preferred_element_type=F32)


def _dot_tn(a, b):
    return lax.dot_general(a, b, (((0,), (0,)), ((), ())), preferred_element_type=F32)


def _dot_hi(a, b):
    return jnp.dot(a, b, precision=HI, preferred_element_type=F32)


def _split3(x):
    hi = x.astype(BF16)
    r1 = x - hi.astype(F32)
    mid = r1.astype(BF16)
    return hi, mid, (r1 - mid.astype(F32)).astype(BF16)


def _dot_hilo(a, b):
    ah = a.astype(BF16)
    al = (a - ah.astype(F32)).astype(BF16)
    bh = b.astype(BF16)
    bl = (b - bh.astype(F32)).astype(BF16)
    return _dot(ah, bh) + (_dot(ah, bl) + _dot(al, bh))


def _mask_dot(mask, x):
    mb = mask.astype(BF16)
    hi, mid, lo = _split3(x)
    return _dot(mb, hi) + (_dot(mb, mid) + _dot(mb, lo))


def _mask_dot_tn(x, mask):
    mb = mask.astype(BF16)
    hi, mid, lo = _split3(x)
    return _dot_tn(hi, mb) + (_dot_tn(mid, mb) + _dot_tn(lo, mb))


def _const_spec(shape):
    nd = len(shape)
    return pl.BlockSpec(shape, lambda *_: (0,) * nd, pipeline_mode=pl.Buffered(1))


def _scan_masks(c, d):
    row = lax.broadcasted_iota(jnp.int32, (c, c), 0)
    col = lax.broadcasted_iota(jnp.int32, (c, c), 1)
    dlt = row - col if d == 0 else col - row
    return dlt >= 0, dlt > 0, dlt <= 0, row == col


def _ada_kernel(cv_ref, w_ref, b_ref, o_ref):
    s = _silu(cv_ref[...]).astype(BF16)
    o_ref[...] = _dot(s, w_ref[...].astype(BF16)) + b_ref[...]


def _ada(c, c_ctx, ada_w, ada_b):
    depth, d, n6 = ada_w.shape
    bsz = c.shape[0]
    rows = 16
    cv = jnp.zeros((rows, d), F32).at[:bsz].set(c).at[bsz].set(c_ctx)
    tn = 1536
    return pl.pallas_call(
        _ada_kernel,
        grid=(depth, n6 // tn),
        in_specs=[pl.BlockSpec((rows, d), lambda l, j: (0, 0)),
                  pl.BlockSpec((None, d, tn), lambda l, j: (l, 0, j)),
                  pl.BlockSpec((None, 1, tn), lambda l, j: (l, 0, j))],
        out_specs=pl.BlockSpec((None, rows, tn), lambda l, j: (l, 0, j)),
        out_shape=jax.ShapeDtypeStruct((depth, rows, n6), F32),
        compiler_params=_params(("arbitrary", "arbitrary"), 32 << 20),
        name="ada_mod",
    )(cv, ada_w, ada_b.reshape(depth, 1, n6))


def _modtab(mods_l, bsz):
    m = mods_l.reshape(mods_l.shape[0], 6, D_MODEL)
    lat = m[:bsz]
    ctx = jnp.broadcast_to(m[bsz][None], (bsz, 6, D_MODEL))
    return jnp.stack([ctx, lat], axis=1)


def _inproj0_kernel(cp_ref, cm_ref, cn_ref, xp_ref, xm_ref, xn_ref, mod_ref, g_ref, w_ref, wt_ref, acw_ref, acb_ref,
                    bcw_ref, gpar_ref, ua_ref, gay_ref, q_ref, k_ref, v_ref, sz_ref, gb_ref, u_scr,
                    *, tm, ctx_tiles, n_tiles):
    t = pl.program_id(1)
    pick = lambda c_ref, x_ref: jnp.where(t < ctx_tiles, c_ref[...], x_ref[...])
    x = jnp.concatenate([pick(cp_ref, xp_ref), pick(cm_ref, xm_ref), pick(cn_ref, xn_ref)], axis=0)
    xm = _normmod(x, g_ref[...], mod_ref[0:1, :], mod_ref[1:2, :]).astype(BF16)
    seg_first = jnp.logical_or(t == 0, t == ctx_tiles)
    seg_last = jnp.logical_or(t == ctx_tiles - 1, t == n_tiles - 1)

    def project(c0, width, conv_input):
        u_scr[:, c0:c0 + width] = _dot(xm, wt_ref[...] if c0 == E_IN_MAIN else w_ref[:, c0:c0 + width])
        if conv_input:
            u_scr[0:8, c0:c0 + width] = jnp.where(seg_first, 0.0, u_scr[0:8, c0:c0 + width])
            u_scr[tm + 8:tm + 16, c0:c0 + width] = jnp.where(seg_last, 0.0, u_scr[tm + 8:tm + 16, c0:c0 + width])

    def conv(c0, width, w_ref_, w0):
        acc = u_scr[6:6 + tm, c0:c0 + width] * w_ref_[0:1, w0:w0 + width]
        for j in range(1, CONV_K):
            acc = acc + u_scr[6 + j:6 + j + tm, c0:c0 + width] * w_ref_[j:j + 1, w0:w0 + width]
        return acc

    project(0, RG_WIDTH, True)
    for grp in range(RG_WIDTH // 128):
        c0 = grp * 128
        ua_ref[:, c0:c0 + 128] = conv(c0, 128, acw_ref, c0) + acb_ref[0:1, c0:c0 + 128]
    project(512, 512, False)
    gay_ref[...] = _gelu_tanh(u_scr[8:8 + tm, 512:1024]).astype(BF16)

    for grp in range(3 * DN_HEADS):
        c0 = grp * 128
        if grp % DN_HEADS == 0:
            project(1024 + c0, DN_HEADS * DN_D, True)
        y = _silu(conv(1024 + c0, 128, bcw_ref, c0))
        if grp < 2 * DN_HEADS:
            y = y * lax.rsqrt(jnp.sum(y * y, axis=-1, keepdims=True) + EPS)
        if grp < DN_HEADS:
            q_ref[:, c0:c0 + 128] = (y * (DN_D ** -0.5)).astype(BF16)
        elif grp < 2 * DN_HEADS:
            k_ref[:, c0 - 512:c0 - 384] = y.astype(BF16)
        else:
            v_ref[:, c0 - 1024:c0 - 896] = y.astype(BF16)
    project(2560, 512, False)
    sz_ref[...] = _silu(u_scr[8:8 + tm, 2560:3072]).astype(BF16)

    project(3072, 128, False)
    xg = u_scr[8:8 + tm, 3072:3200]
    lane = lax.broadcasted_iota(jnp.int32, xg.shape, 1)
    g = -jnp.exp(gpar_ref[0:1, :]) * _softplus(xg + gpar_ref[1:2, :])
    gb_ref[...] = jnp.where(lane < 2 * DN_HEADS, _sigmoid(xg), g)


def _row_specs(tm, d, ctx_tiles, ctx_len, seq, halo):
    tb = tm // 8

    def specs(n_rows, tile_of):
        main = pl.BlockSpec((None, tm, d), lambda b, t: (b, jnp.clip(tile_of(t), 0, n_rows // tm - 1), 0))
        if not halo:
            return [main]
        prev = pl.BlockSpec((None, 8, d), lambda b, t: (b, jnp.clip(tile_of(t) * tb - 1, 0, n_rows // 8 - 1), 0))
        nxt = pl.BlockSpec((None, 8, d), lambda b, t: (b, jnp.clip((tile_of(t) + 1) * tb, 0, n_rows // 8 - 1), 0))
        return [prev, main, nxt]

    return specs(ctx_len, lambda t: t) + specs(seq, lambda t: t - ctx_tiles)


def _inproj0(ctx, x, modtab, g, w, w_tail, acw, acb, bcw, gpar, *, tm):
    bsz, ctx_len, d = ctx.shape
    seq = x.shape[1]
    t_all = ctx_len + seq
    n_tiles = t_all // tm
    ctx_tiles = ctx_len // tm
    kern = functools.partial(_inproj0_kernel, tm=tm, ctx_tiles=ctx_tiles, n_tiles=n_tiles)
    tok = lambda w, dt=BF16: jax.ShapeDtypeStruct((bsz, t_all, w), dt)
    tok_spec = lambda w: pl.BlockSpec((None, tm, w), lambda b, t: (b, t, 0))
    return pl.pallas_call(
        kern,
        grid=(bsz, n_tiles),
        in_specs=_row_specs(tm, d, ctx_tiles, ctx_len, seq, True) + [
            pl.BlockSpec((None, None, 6, d), lambda b, t: (b, jnp.where(t >= ctx_tiles, 1, 0), 0, 0)),
            _const_spec((1, d)),
            _const_spec(w.shape), _const_spec((d, 128)),
            _const_spec((CONV_K, RG_WIDTH)),
            _const_spec((1, RG_WIDTH)),
            _const_spec((CONV_K, 3 * DN_HEADS * DN_D)),
            _const_spec((2, 128)),
        ],
        out_specs=[tok_spec(512), tok_spec(512), tok_spec(512), tok_spec(512), tok_spec(512), tok_spec(512),
                   tok_spec(128)],
        out_shape=[tok(512, F32), tok(512), tok(512), tok(512), tok(512), tok(512), tok(128, F32)],
        scratch_shapes=[pltpu.VMEM((tm + 16, E_IN_PAD), F32)],
        compiler_params=_params(("arbitrary", "arbitrary"), 40 << 20),
        name="l0_inproj",
    )(ctx, ctx, ctx, x, x, x, modtab, g, w, w_tail, acw, acb, bcw, gpar)


def _rglru_kernel(uf_ref, ub_ref, wg_ref, gbias_ref, lam_ref, hf_ref, hb_ref,
                  af_scr, xf_scr, ab_scr, xb_scr, stg_scr, of_scr, ob_scr, h_scr, *, tt, bsz):
    s = pl.program_id(0)
    n_slab = RG_WIDTH // 128

    @pl.when(s == 0)
    def _():
        h_scr[...] = jnp.zeros_like(h_scr)

    def gates(u_ref, d, a_scr, x_scr):
        for b in range(bsz):
            for j in range(n_slab):
                stg_scr[j, pl.ds(b, tt, stride=bsz), :] = u_ref[b, :, j * 128:(j + 1) * 128]
        x = jnp.concatenate([stg_scr[j] for j in range(n_slab)], axis=1)
        xb = x.astype(BF16)
        for half in range(2):
            c0 = half * 256
            xh = xb[:, c0:c0 + 256]
            r = _sigmoid_tanh(_dot(xh, wg_ref[d, 0, half]) + gbias_ref[2 * d:2 * d + 1, c0:c0 + 256])
            i = _sigmoid_tanh(_dot(xh, wg_ref[d, 1, half]) + gbias_ref[2 * d + 1:2 * d + 2, c0:c0 + 256])
            log_a = (-RG_C) * r * _softplus(-lam_ref[d:d + 1, c0:c0 + 256])
            a = jnp.exp(log_a)
            mult = jnp.sqrt(-jnp.tanh(log_a) * (a * a + 1.0))
            xin = mult * (i * x[:, c0:c0 + 256])
            a_scr[:, :, c0:c0 + 256] = a.reshape(tt, bsz, 256)
            x_scr[:, :, c0:c0 + 256] = xin.reshape(tt, bsz, 256)

    gates(uf_ref, 0, af_scr, xf_scr)
    gates(ub_ref, 1, ab_scr, xb_scr)

    def put(o_scr, t, h):
        rows = pl.ds(pl.multiple_of(t * bsz, bsz), bsz)
        for j in range(n_slab):
            o_scr[j, rows, :] = h[:, j * 128:(j + 1) * 128]

    def step(t, carry):
        hf, hb = carry
        hf = af_scr[t] * hf + xf_scr[t]
        put(of_scr, t, hf)
        tb = tt - 1 - t
        hb = ab_scr[tb] * hb + xb_scr[tb]
        put(ob_scr, tb, hb)
        return hf, hb

    hf, hb = lax.fori_loop(0, tt, step, (h_scr[0], h_scr[1]), unroll=8)
    h_scr[0] = hf
    h_scr[1] = hb
    for o_scr, o_ref in ((of_scr, hf_ref), (ob_scr, hb_ref)):
        for b in range(bsz):
            for j in range(n_slab):
                o_ref[b, :, j * 128:(j + 1) * 128] = o_scr[j, pl.ds(b, tt, stride=bsz), :]


def _rglru(ua, wg, gbias, lam, *, tt, ctx_len):
    bsz, t_all, w = ua.shape
    n_steps = t_all // tt
    nc = ctx_len // tt

    def bwd(s):
        return jnp.where(s < nc, nc - 1 - s, n_steps + nc - 1 - s)

    blk = (bsz, tt, w)
    tm_blk = (tt, bsz, w)
    slabs = (w // 128, tt * bsz, 128)
    kern = functools.partial(_rglru_kernel, tt=tt, bsz=bsz)
    return pl.pallas_call(
        kern,
        grid=(n_steps,),
        in_specs=[pl.BlockSpec(blk, lambda s: (0, s, 0)),
                  pl.BlockSpec(blk, lambda s: (0, bwd(s), 0)),
                  _const_spec(wg.shape), _const_spec(gbias.shape), _const_spec(lam.shape)],
        out_specs=[pl.BlockSpec(blk, lambda s: (0, s, 0)),
                   pl.BlockSpec(blk, lambda s: (0, bwd(s), 0))],
        out_shape=[jax.ShapeDtypeStruct(ua.shape, F32)] * 2,
        scratch_shapes=[pltpu.VMEM(tm_blk, F32)] * 4 + [pltpu.VMEM(slabs, F32)] * 3 + [pltpu.VMEM((2, bsz, w), F32)],
        compiler_params=_params(("arbitrary",), 48 << 20),
        name="l0_rglru",
    )(ua, ua, wg, gbias, lam)


def _delta_kernel(qf_ref, kf_ref, vf_ref, gf_ref, qb_ref, kb_ref, vb_ref, gb_ref, of_ref, ob_ref, s_scr, *, n_sub):
    c = DN_CHUNK

    @pl.when(pl.program_id(1) == 0)
    def _():
        s_scr[...] = jnp.zeros_like(s_scr)

    dir_refs = ((qf_ref, kf_ref, vf_ref, gf_ref, of_ref), (qb_ref, kb_ref, vb_ref, gb_ref, ob_ref))
    masks = [_scan_masks(c, d) for d in range(2)]
    eye = jnp.where(masks[0][3], 1.0, 0.0)

    cums = {}
    for d in range(2):
        incl, _, incl_t, _ = masks[d]
        m_incl = jnp.where(incl, 1.0, 0.0)
        m_incl_t = jnp.where(incl_t, 1.0, 0.0)
        for ci in range(n_sub):
            g_all = dir_refs[d][3][ci * c:(ci + 1) * c, :]
            gc_all = _mask_dot(m_incl, g_all)
            gct_all = _mask_dot_tn(g_all, m_incl_t)
            cums[d, ci] = (g_all, gc_all, gct_all)

    chains = []
    for d in range(2):
        q_ref, k_ref, v_ref, _, _ = dir_refs[d]
        incl, strict, _, _ = masks[d]
        last = c - 1 if d == 0 else 0
        for ci in range(n_sub):
            g_all, gc_all, gct_all = cums[d, ci]
            rs = slice(ci * c, (ci + 1) * c)
            for h in range(DN_HEADS):
                hs = slice(h * DN_D, (h + 1) * DN_D)
                lane = 2 * DN_HEADS + d * DN_HEADS + h
                ch = dict(d=d, ci=ci, h=h, rs=rs, hs=hs, incl=incl, strict=strict)
                ch["beta"] = g_all[:, d * DN_HEADS + h:d * DN_HEADS + h + 1]
                gc = jnp.broadcast_to(gc_all[:, lane:lane + 1], (c, DN_D))
                gc_row = jnp.broadcast_to(gct_all[lane:lane + 1, :], (c, c))
                ch["gc"] = gc
                ch["gtot"] = gc[last:last + 1, :]
                ch["decay"] = jnp.where(incl, jnp.exp(jnp.minimum(gc[:, 0:c] - gc_row, 0.0)), 0.0)
                ch["e_gc"] = jnp.exp(gc)
                ch["q"] = q_ref[rs, hs].astype(F32)
                ch["k"] = k_ref[rs, hs].astype(F32)
                ch["v"] = v_ref[rs, hs].astype(F32)
                chains.append(ch)

    for ch in chains:
        ch["kb"] = ch["k"] * ch["beta"]
        qk = _dot_nt(jnp.concatenate([ch["kb"], ch["q"]], axis=0).astype(BF16), ch["k"].astype(BF16))
        ch["neg"] = -jnp.where(ch["strict"], qk[0:c] * ch["decay"], 0.0)
        ch["a_qk"] = (qk[c:2 * c] * ch["decay"]).astype(BF16)
    for ch in chains:
        negb = ch["neg"].astype(BF16)
        ch["t"] = eye + ch["neg"]
        ch["p"] = _dot(negb, negb)
    n_sq = max(1, (c - 1).bit_length() - 1)
    for it in range(n_sq):
        for ch in chains:
            tp = _dot(jnp.concatenate([ch["t"], ch["p"]], axis=0).astype(BF16), ch["p"].astype(BF16))
            ch["t"] = ch["t"] + tp[0:c]
            ch["p"] = tp[c:2 * c]
    for ch in chains:
        rhs = jnp.concatenate([ch["v"] * ch["beta"], ch["kb"] * ch["e_gc"]], axis=1).astype(BF16)
        sol = _dot(ch["t"].astype(BF16), rhs)
        ch["u"] = sol[:, 0:DN_D]
        ch["wq"] = jnp.concatenate([sol[:, DN_D:2 * DN_D], ch["q"] * ch["e_gc"]], axis=0).astype(BF16)
        ch["k_tail"] = (ch["k"] * jnp.exp(ch["gtot"] - ch["gc"])).astype(BF16)

    by_key = {(ch["d"], ch["ci"], ch["h"]): ch for ch in chains}
    for step in range(n_sub):
        live = [by_key[d, step if d == 0 else n_sub - 1 - step, h] for d in range(2) for h in range(DN_HEADS)]
        for ch in live:
            ch["st"] = s_scr[ch["d"], ch["h"]]
            ch["ws"] = _dot(ch["wq"], ch["st"].astype(BF16))
        for ch in live:
            vnb = (ch["u"] - ch["ws"][0:c]).astype(BF16)
            o = ch["ws"][c:2 * c] + _dot(ch["a_qk"], vnb)
            dir_refs[ch["d"]][4][ch["rs"], ch["hs"]] = o.astype(BF16)
            s_scr[ch["d"], ch["h"]] = ch["st"] * jnp.exp(ch["gtot"]) + _dot_tn(ch["k_tail"], vnb)


def _delta(q, k, v, gb, *, ctx_len, rows):
    bsz, t_all, w = q.shape
    n_steps = t_all // rows
    nc = ctx_len // rows

    def bwd(s):
        return jnp.where(s < nc, nc - 1 - s, n_steps + nc - 1 - s)

    fwd_spec = lambda width: pl.BlockSpec((None, rows, width), lambda b, s: (b, s, 0))
    bwd_spec = lambda width: pl.BlockSpec((None, rows, width), lambda b, s: (b, bwd(s), 0))
    return pl.pallas_call(
        functools.partial(_delta_kernel, n_sub=rows // DN_CHUNK),
        grid=(bsz, n_steps),
        in_specs=[fwd_spec(w), fwd_spec(w), fwd_spec(w), fwd_spec(128),
                  bwd_spec(w), bwd_spec(w), bwd_spec(w), bwd_spec(128)],
        out_specs=[fwd_spec(w), bwd_spec(w)],
        out_shape=[jax.ShapeDtypeStruct((bsz, t_all, w), BF16)] * 2,
        scratch_shapes=[pltpu.VMEM((2, DN_HEADS, DN_D, DN_D), F32)],
        compiler_params=_params(("arbitrary", "arbitrary"), 32 << 20),
        name="l0_deltanet",
    )(q, k, v, gb, q, k, v, gb)


def _head_norm(y, g):
    return y * lax.rsqrt(jnp.mean(y * y, axis=-1, keepdims=True) + EPS) * g


def _l0_tail_kernel(hf_ref, hb_ref, gay_ref, o0_ref, o1_ref, sz_ref, ctx_ref, x_ref, mod_ref, ng_ref, wo_ref, g_ref,
                    wg_ref, wu_ref, wd_ref, hc_ref, hl_ref, *, n_chunks, ctx_tiles):
    t = pl.program_id(1)
    tm = hf_ref.shape[0]
    halves = [dict(rs=slice(i * tm // 2, (i + 1) * tm // 2)) for i in range(2)]
    for hv in halves:
        rs = hv["rs"]
        parts = [((hf_ref[rs, :] + hb_ref[rs, :]) * gay_ref[rs, :].astype(F32)).astype(BF16)]
        for hd in range(DN_HEADS):
            lo = hd * DN_D
            ob = o0_ref[rs, lo:lo + DN_D].astype(F32) + o1_ref[rs, lo:lo + DN_D].astype(F32)
            parts.append((_head_norm(ob, ng_ref[...]) * sz_ref[rs, lo:lo + DN_D].astype(F32)).astype(BF16))
        hv["ycat"] = jnp.concatenate(parts, axis=-1)
    for hv in halves:
        h = jnp.where(t < ctx_tiles, ctx_ref[hv["rs"], :], x_ref[hv["rs"], :])
        hv["x"] = h + mod_ref[2:3, :] * _dot(hv["ycat"], wo_ref[...])
    for hv in halves:
        hv["xm"] = _normmod(hv["x"], g_ref[...], mod_ref[3:4, :], mod_ref[4:5, :]).astype(BF16)
        hv["acc"] = jnp.zeros(hv["x"].shape, F32)
    cw = D_FF // n_chunks
    for ci in range(n_chunks):
        c0 = ci * cw
        for hv in halves:
            hv["act"] = (_silu(_dot(hv["xm"], wg_ref[:, c0:c0 + cw])) * _dot(hv["xm"], wu_ref[:, c0:c0 + cw])).astype(BF16)
        for hv in halves:
            hv["acc"] = hv["acc"] + _dot(hv["act"], wd_ref[c0:c0 + cw, :])
    out = jnp.concatenate([hv["x"] + mod_ref[5:6, :] * hv["acc"] for hv in halves], axis=0)

    @pl.when(t < ctx_tiles)
    def _():
        hc_ref[...] = out

    @pl.when(t >= ctx_tiles)
    def _():
        hl_ref[...] = out


def _l0_tail(hf, hb, gay, o0, o1, sz, ctx, x, modtab, ng, wo, g, wg, wu, wd, *, tm):
    bsz, ctx_len, d = ctx.shape
    seq = x.shape[1]
    t_all = ctx_len + seq
    ctx_tiles = ctx_len // tm
    tok = lambda width: pl.BlockSpec((None, tm, width), lambda b, t: (b, t, 0))
    return pl.pallas_call(
        functools.partial(_l0_tail_kernel, n_chunks=2, ctx_tiles=ctx_tiles),
        grid=(bsz, t_all // tm),
        in_specs=[tok(512)] * 6 + _row_specs(tm, d, ctx_tiles, ctx_len, seq, False) + [
                  pl.BlockSpec((None, None, 6, d), lambda b, t: (b, jnp.where(t >= ctx_tiles, 1, 0), 0, 0)),
                  _const_spec((1, DN_D)), _const_spec((d, d)),
                  _const_spec((1, d)), _const_spec((d, D_FF)), _const_spec((d, D_FF)), _const_spec((D_FF, d))],
        out_specs=[pl.BlockSpec((None, tm, d), lambda b, t: (b, jnp.minimum(t, ctx_tiles - 1), 0)),
                   pl.BlockSpec((None, tm, d), lambda b, t: (b, jnp.maximum(t - ctx_tiles, 0), 0))],
        out_shape=[jax.ShapeDtypeStruct((bsz, ctx_len, d), F32), jax.ShapeDtypeStruct((bsz, t_all - ctx_len, d), F32)],
        compiler_params=_params(("arbitrary", "arbitrary"), 48 << 20),
        name="l0_tail",
    )(hf, hb, gay, o0, o1, sz, ctx, x, modtab, ng, wo, g, wg, wu, wd)


def _inproj1_kernel(h_ref, mod_ref, g_ref, w_ref, wt_ref, lbl_ref, wlr_ref, b2_ref, pa_ref, pb_ref, u_scr, *, layer):
    x = h_ref[...]
    xm = _normmod(x, g_ref[...], mod_ref[0:1, :], mod_ref[1:2, :]).astype(BF16)

    lg = lbl_ref[...]
    ex = jnp.exp(lg - jnp.max(lg, axis=0, keepdims=True))
    lbw = ex / jnp.sum(ex, axis=0, keepdims=True)
    lb = jnp.sum(lbw[1:layer + 1], axis=0, keepdims=True)

    def project(c0, width):
        u_scr[:, c0:c0 + width] = _dot(xm, wt_ref[...] if c0 == O_IN_MAIN else w_ref[:, c0:c0 + width])

    def put(seg, off, val):
        ref = (pa_ref, pb_ref)[seg[0]]
        ref[:, seg[1] * SEG + off:seg[1] * SEG + off + val.shape[1]] = val.astype(ref.dtype)

    def groups(fn):
        for grp in range(SEG // 128):
            fn(grp * 128)

    project(0, SEG)
    groups(lambda c0: put(S_HQ, c0, _silu(u_scr[:, c0:c0 + 128]) * (HG_D ** -0.5)))
    for dr, (sk, sf) in enumerate(((S_HK0, S_HLF0), (S_HK1, S_HLF1))):
        project(512 + dr * 512, SEG)

        def forget(c0, dr=dr, sk=sk, sf=sf):
            lbg = lb[:, c0:c0 + 128]
            fl = u_scr[:, 512 + dr * 512 + c0:512 + dr * 512 + c0 + 128]
            sg = _sigmoid(fl)
            put(sf, c0, jnp.log(lbg + (1.0 - lbg) * sg))
            put(sk, c0, (1.0 - lbg) * (1.0 - sg))
        groups(forget)
    project(1536, SEG)
    groups(lambda c0: put(S_HV, c0, u_scr[:, 1536 + c0:1536 + c0 + 128]))
    project(2048, SEG)
    groups(lambda c0: put(S_CG, c0, _silu(u_scr[:, 2048 + c0:2048 + c0 + 128])))
    project(2560, SEG)
    put(S_GQK, 0, u_scr[:, 2560:2816] * (GLA_DK ** -0.5))
    put(S_GQK, 256, u_scr[:, 2816:3072])
    project(3072, SEG)
    groups(lambda c0: put(S_GV, c0, u_scr[:, 3072 + c0:3072 + c0 + 128]))
    project(3584, SEG)
    groups(lambda c0: put(S_DG, c0, _silu(u_scr[:, 3584 + c0:3584 + c0 + 128])))
    project(4096, 128)
    lr = u_scr[:, 4096:4224]
    put(S_GLD, 0, -_softplus(-(_dot_hilo(lr, wlr_ref[...]) + b2_ref[...])) * (1.0 / GLA_GATE_NORM))


def _inproj1(h, row0, n_rows, modtab, seg, g, w, w_tail, lbl, wlr, b2, *, tm, layer):
    bsz, _, d = h.shape
    t0 = row0 // tm
    return pl.pallas_call(
        functools.partial(_inproj1_kernel, layer=layer),
        grid=(bsz, n_rows // tm),
        in_specs=[pl.BlockSpec((None, tm, d), lambda b, t: (b, t0 + t, 0)),
                  pl.BlockSpec((None, None, 6, d), lambda b, t: (b, seg, 0, 0)),
                  _const_spec((1, d)), _const_spec(w.shape), _const_spec((d, 128)), _const_spec(lbl.shape),
                  _const_spec((128, SEG)), _const_spec((1, SEG))],
        out_specs=[pl.BlockSpec((None, tm, n * SEG), lambda b, t: (b, t, 0)) for n in N_SEG],
        out_shape=[jax.ShapeDtypeStruct((bsz, n_rows, n * SEG), dt) for n, dt in zip(N_SEG, P1_DTYPES)],
        scratch_shapes=[pltpu.VMEM((tm, O_IN_PAD), F32)],
        compiler_params=_params(("arbitrary", "arbitrary"), 44 << 20),
        name="l1_inproj",
    )(h, modtab, g, w, w_tail, lbl, wlr, b2)


def _gla_stream(d, q_all, k_all, ld_all, v_all, st_ref, o_ref, o_lane0, r0, n_heads, dk, dv, incl, m_incl):
    c = k_all.shape[0]
    mid = c // 2 - 1 if d == 0 else c // 2
    last = c - 1 if d == 0 else 0
    bc = _mask_dot(m_incl, ld_all)
    m = bc[mid:mid + 1]
    btot = bc[last:last + 1]
    kn = k_all.astype(F32) * jnp.exp(m - bc)
    it = dict(d=d, r0=r0, st_ref=st_ref, o_ref=o_ref, o_lane0=o_lane0, n_heads=n_heads, dk=dk, dv=dv, incl=incl,
              c=c, kt=(kn * jnp.exp(btot - m)).astype(BF16), dec=jnp.exp(btot), v=v_all.astype(BF16),
              want_out=q_all is not None)
    if q_all is not None:
        qe = q_all.astype(F32) * jnp.exp(bc)
        it.update(qd=(qe * jnp.exp(-m)).astype(BF16), qe=qe.astype(BF16), knb=kn.astype(BF16))
    return it


def _gla_intra(it):
    dk = it["dk"]
    it["a"] = [jnp.where(it["incl"], _dot_nt(it["qd"][:, hd * dk:(hd + 1) * dk], it["knb"][:, hd * dk:(hd + 1) * dk]),
                         0.0).astype(BF16) for hd in range(it["n_heads"])]


def _gla_advance(it):
    d, dk, dv, c, st_ref = it["d"], it["dk"], it["dv"], it["c"], it["st_ref"]
    sts = [st_ref[d, hd] for hd in range(it["n_heads"])]
    if it["want_out"]:
        for hd in range(it["n_heads"]):
            v = it["v"][:, hd * dv:(hd + 1) * dv]
            o = _dot(it["a"][hd], v) + _dot_nt(it["qe"][:, hd * dk:(hd + 1) * dk], sts[hd].astype(BF16))
            it["o_ref"][it["r0"]:it["r0"] + c, it["o_lane0"] + hd * dv:it["o_lane0"] + (hd + 1) * dv] = o.astype(BF16)
    for hd in range(it["n_heads"]):
        ks = slice(hd * dk, (hd + 1) * dk)
        st_ref[d, hd] = sts[hd] * it["dec"][:, ks] + _dot_tn(it["v"][:, hd * dv:(hd + 1) * dv], it["kt"][:, ks])


def _mix1_body(dirs, n_sub, sh_ref, sg_ref):
    c = MIX1_CHUNK
    gw = GLA_HEADS * GLA_DK
    lane = lambda seg: slice(seg[1] * SEG, (seg[1] + 1) * SEG)
    prepared = {}
    for d, (sh_in, hk_ref, hlf_ref, gld_ref, o_ref) in enumerate(dirs):
        incl = _scan_masks(c, d)[0]
        m_incl = jnp.where(incl, 1.0, 0.0)
        g0 = S_GQK[1] * SEG
        for ci in range(n_sub):
            r0 = ci * c
            rs = slice(r0, r0 + c)
            prepared[d, ci, 0] = _gla_stream(
                d, None if o_ref is None else sh_in[rs, lane(S_HQ)], hk_ref[rs, :], hlf_ref[rs, :],
                sh_in[rs, lane(S_HV)], sh_ref, o_ref, 0, r0, HG_HEADS, HG_D, HG_D, incl, m_incl)
            prepared[d, ci, 1] = _gla_stream(
                d, None if o_ref is None else sh_in[rs, g0:g0 + gw], sh_in[rs, g0 + gw:g0 + 2 * gw],
                gld_ref[rs, d * gw:(d + 1) * gw], sh_in[rs, lane(S_GV)],
                sg_ref, o_ref, HG_HEADS * HG_D, r0, GLA_HEADS, GLA_DK, GLA_DV, incl, m_incl)
    for it in prepared.values():
        if it["want_out"]:
            _gla_intra(it)
    for step in range(n_sub):
        for d in range(2):
            for stream in range(2):
                _gla_advance(prepared[d, step if d == 0 else n_sub - 1 - step, stream])


def _mix1_ctx_kernel(*refs, n_sub):
    fwd, bwd, (sh_ref, sg_ref) = refs[0:4], refs[4:8], refs[8:10]

    @pl.when(pl.program_id(1) == 0)
    def _():
        sh_ref[...] = jnp.zeros_like(sh_ref)
        sg_ref[...] = jnp.zeros_like(sg_ref)

    _mix1_body([tuple(r) + (None,) for r in (fwd, bwd)], n_sub, sh_ref, sg_ref)


def _mix1_lat_kernel(*refs, n_sub):
    fwd, bwd = refs[0:4], refs[4:8]
    sh0_ref, sg0_ref, of_ref, ob_ref, sh_scr, sg_scr = refs[8:14]

    @pl.when(pl.program_id(1) == 0)
    def _():
        sh_scr[...] = sh0_ref[...]
        sg_scr[...] = sg0_ref[...]

    _mix1_body([tuple(fwd) + (of_ref,), tuple(bwd) + (ob_ref,)], n_sub, sh_scr, sg_scr)


def _mix1_specs(p1, rows, n_steps):
    specs, args = [], []
    for d in range(2):
        blk = (lambda b, s: s) if d == 0 else (lambda b, s: n_steps - 1 - s)
        for (arr, sg), n_seg in ((S_HQ, N_SHARED), ((S_HK0, S_HK1)[d], 1), ((S_HLF0, S_HLF1)[d], 1), (S_GLD, 1)):
            assert sg % n_seg == 0
            specs.append(pl.BlockSpec((None, rows, n_seg * SEG),
                                      lambda b, s, blk=blk, sg=sg // n_seg: (b, blk(b, s), sg)))
            args.append(p1[arr])
    return specs, args


_SH_SHAPE = (2, HG_HEADS, HG_D, HG_D)
_SG_SHAPE = (2, GLA_HEADS, GLA_DV, GLA_DK)


def _mix1_ctx(p1c, *, rows):
    bsz, ctx_len, _ = p1c[0].shape
    n_steps = ctx_len // rows
    specs, args = _mix1_specs(p1c, rows, n_steps)
    state = lambda shape: pl.BlockSpec((None,) + shape, lambda b, s: (b, 0, 0, 0, 0))
    return pl.pallas_call(
        functools.partial(_mix1_ctx_kernel, n_sub=rows // MIX1_CHUNK),
        grid=(bsz, n_steps),
        in_specs=specs,
        out_specs=[state(_SH_SHAPE), state(_SG_SHAPE)],
        out_shape=[jax.ShapeDtypeStruct((bsz,) + _SH_SHAPE, F32), jax.ShapeDtypeStruct((bsz,) + _SG_SHAPE, F32)],
        compiler_params=_params(("arbitrary", "arbitrary"), 32 << 20),
        name="l1_ctx_state",
    )(*args)


def _mix1_lat(p1l, sh0, sg0, *, rows):
    bsz, seq, _ = p1l[0].shape
    n_steps = seq // rows
    specs, args = _mix1_specs(p1l, rows, n_steps)
    state = lambda shape: pl.BlockSpec((None,) + shape, lambda b, s: (b, 0, 0, 0, 0))
    ow = HG_HEADS * HG_D + GLA_HEADS * GLA_DV
    return pl.pallas_call(
        functools.partial(_mix1_lat_kernel, n_sub=rows // MIX1_CHUNK),
        grid=(bsz, n_steps),
        in_specs=specs + [state(_SH_SHAPE), state(_SG_SHAPE)],
        out_specs=[pl.BlockSpec((None, rows, ow), lambda b, s: (b, s, 0)),
                   pl.BlockSpec((None, rows, ow), lambda b, s: (b, n_steps - 1 - s, 0))],
        out_shape=[jax.ShapeDtypeStruct((bsz, seq, ow), BF16)] * 2,
        scratch_shapes=[pltpu.VMEM(_SH_SHAPE, F32), pltpu.VMEM(_SG_SHAPE, F32)],
        compiler_params=_params(("arbitrary", "arbitrary"), 32 << 20),
        name="l1_scan",
    )(*args, sh0, sg0)


def _outproj1(o0_ref, o1_ref, gate_ref, h_ref, mod_ref, cng_ref, dng_ref, w_ref, r0, rows):
    rs = slice(r0, r0 + rows)
    parts = []
    for hd in range(HG_HEADS + GLA_HEADS):
        lo = hd * 128
        y = o0_ref[rs, lo:lo + 128].astype(F32) + o1_ref[rs, lo:lo + 128].astype(F32)
        ng = cng_ref[...] if hd < HG_HEADS else dng_ref[...]
        parts.append((_head_norm(y, ng) * gate_ref[rs, lo:lo + 128].astype(F32)).astype(BF16))
    return h_ref[rs, :] + mod_ref[2:3, :] * _dot(jnp.concatenate(parts, axis=-1), w_ref[...])


def _for_pieces(length, fn):
    done = 0
    for rows in MOE_SEG_PIECES:
        n = (length - done) // rows

        def body(p, carry, rows=rows, done=done):
            fn(done + p * rows, rows)
            return carry
        lax.fori_loop(0, n, body, 0)
        done = done + n * rows


def _moe_route_kernel(o0_ref, o1_ref, gate_ref, h_ref, mod_ref, cng_ref, dng_ref, wo_ref, ng_ref, rw_ref, rb_ref,
                      h2_ref, slot_ref, seg_ref, xs_hbm, xn_scr, lg_scr, xg_scr, zero_scr, base_smem, seg_smem, sem,
                      *, tk):
    i = pl.program_id(0)
    last_step = pl.num_programs(0) - 1
    pc = MOE_PIECE
    gr = MOE_GATHER_ROWS
    cur = lax.rem(i, 2)
    buf = 1 - cur

    def segment_copies(slot, issue):
        for e in range(N_EXPERTS):
            off_e = seg_smem[slot, e]
            base_e = seg_smem[slot, 2 * N_EXPERTS + e]
            _for_pieces(seg_smem[slot, N_EXPERTS + e], lambda r, rows, e=e, off_e=off_e, base_e=base_e: issue(
                pltpu.make_async_copy(xg_scr.at[slot, pl.ds(pl.multiple_of(off_e + r, pc), rows), :],
                                      xs_hbm.at[e, pl.ds(pl.multiple_of(base_e + r, pc), rows), :], sem.at[slot])))

    @pl.when(i == 0)
    def _():
        for e in range(N_EXPERTS):
            base_smem[e] = 0
        zero_scr[...] = jnp.zeros_like(zero_scr)
        lg_scr[1] = jnp.zeros_like(lg_scr[1])

    @pl.when(i >= 3)
    def _():
        segment_copies(buf, lambda cp: cp.wait())

    lg = lg_scr[buf]
    half = tk // 2
    for r0 in (0, half):
        h2 = _outproj1(o0_ref, o1_ref, gate_ref, h_ref, mod_ref, cng_ref, dng_ref, wo_ref, r0, half)
        h2_ref[r0:r0 + half, :] = h2
        xm = _normmod(h2, ng_ref[...], mod_ref[3:4, :], mod_ref[4:5, :])
        xn_scr[cur, r0:r0 + half, :] = xm.astype(BF16)
        lg_scr[cur, :, r0:r0 + half] = jnp.transpose(_dot_hilo(xm, rw_ref[...]))[0:N_EXPERTS, :] + rb_ref[...]

    routed = jnp.where(i >= 1, 1.0, 0.0)
    eidx = lax.broadcasted_iota(jnp.int32, lg.shape, 0).astype(F32)
    m1 = jnp.max(lg, axis=0, keepdims=True)
    i1 = jnp.min(jnp.where(lg == m1, eidx, float(N_EXPERTS)), axis=0, keepdims=True)
    lg2 = jnp.where(eidx == i1, -jnp.inf, lg)
    m2 = jnp.max(lg2, axis=0, keepdims=True)
    i2 = jnp.min(jnp.where(lg2 == m2, eidx, float(N_EXPERTS)), axis=0, keepdims=True)
    ex = jnp.exp(m2 - m1)
    p1 = 1.0 / (1.0 + ex)
    sel = (jnp.where(eidx == i1, 1.0, 0.0) + jnp.where(eidx == i2, 1.0, 0.0)) * routed
    lane = lax.broadcasted_iota(jnp.int32, lg.shape, 1)
    cum = sel
    sh = 1
    while sh < tk:
        cum = cum + jnp.where(lane >= sh, pltpu.roll(cum, sh, 1), 0.0)
        sh *= 2
    padded = jnp.floor((cum[:, tk - 1:tk] + (pc - 1.0)) * (1.0 / pc)) * pc
    padded = jnp.broadcast_to(padded, (N_EXPERTS, 128))
    er = lax.broadcasted_iota(jnp.int32, (N_EXPERTS, N_EXPERTS), 0)
    ec = lax.broadcasted_iota(jnp.int32, (N_EXPERTS, N_EXPERTS), 1)
    off = _dot_hi(jnp.where(er > ec, 1.0, 0.0), padded)
    slot = off[:, 0:1] + cum - 1.0
    slot_a = jnp.sum(jnp.where(eidx == i1, slot, 0.0), axis=0, keepdims=True)
    slot_b = jnp.sum(jnp.where(eidx == i2, slot, 0.0), axis=0, keepdims=True)
    slot_ref[...] = jnp.concatenate([slot_a, slot_b, p1, ex * p1, jnp.zeros((4, tk), F32)], axis=0)

    total = jnp.max(off[N_EXPERTS - 1:N_EXPERTS, :] + padded[N_EXPERTS - 1:N_EXPERTS, :]).astype(jnp.int32)

    def gather(ci, carry):
        r0 = pl.multiple_of(ci * gr, gr)
        rid = (lax.broadcasted_iota(jnp.int32, (gr, tk), 0) + r0).astype(F32)
        p = jnp.where(rid == slot_a, 1.0, 0.0) + jnp.where(rid == slot_b, 1.0, 0.0)
        xg_scr[buf, pl.ds(r0, gr), :] = _dot(p.astype(BF16), xn_scr[buf]).astype(BF16)
        return carry
    lax.fori_loop(0, (total + gr - 1) // gr, gather, 0)

    erow = lax.broadcasted_iota(jnp.int32, (N_EXPERTS, 128), 0)
    base_vec = jnp.zeros((N_EXPERTS, 128), F32)
    for e in range(N_EXPERTS):
        len_e = jnp.max(padded[e:e + 1, :]).astype(jnp.int32)
        base_e = base_smem[e]
        base_vec = jnp.where(erow == e, base_e.astype(F32), base_vec)
        seg_smem[buf, e] = jnp.max(off[e:e + 1, :]).astype(jnp.int32)
        seg_smem[buf, N_EXPERTS + e] = len_e
        seg_smem[buf, 2 * N_EXPERTS + e] = base_e
        base_smem[e] = base_e + len_e
    seg_ref[0] = off
    seg_ref[1] = padded
    seg_ref[2] = base_vec
    segment_copies(buf, lambda cp: cp.start())

    @pl.when(i == last_step)
    def _():
        @pl.when(i >= 2)
        def _():
            segment_copies(1 - buf, lambda cp: cp.wait())
        segment_copies(buf, lambda cp: cp.wait())

        def tail_copy(e, p):
            end = base_smem[e]
            return pltpu.make_async_copy(zero_scr, xs_hbm.at[e, pl.ds(pl.multiple_of(end + p * pc, pc), pc), :],
                                         sem.at[0])

        def n_tail(e):
            rem = lax.rem(base_smem[e], MOE_BLOCK)
            return jnp.where(rem == 0, 0, MOE_BLOCK - rem) // pc

        for e in range(N_EXPERTS):
            def start(p, carry, e=e):
                tail_copy(e, p).start()
                return carry
            lax.fori_loop(0, n_tail(e), start, 0)
        for e in range(N_EXPERTS):
            def wait(p, carry, e=e):
                tail_copy(e, p).wait()
                return carry
            lax.fori_loop(0, n_tail(e), wait, 0)


def _moe_ffn_kernel(eid_ref, blk_ref, nv_ref, x_ref, wg_ref, wu_ref, wd_ref, o_ref, *, n_chunks):
    del eid_ref, blk_ref

    @pl.when(pl.program_id(0) < nv_ref[0])
    def _():
        x = x_ref[...]
        cw = D_FF // n_chunks
        acc = jnp.zeros(x.shape, F32)
        for ci in range(n_chunks):
            c0 = ci * cw
            act = (_silu(_dot(x, wg_ref[:, c0:c0 + cw])) * _dot(x, wu_ref[:, c0:c0 + cw])).astype(BF16)
            acc = acc + _dot(act, wd_ref[c0:c0 + cw, :])
        o_ref[...] = acc.astype(BF16)


def _moe_combine_kernel(base_ref, len_ref, off_ref, h_ref, mod_ref, fg_ref, slot_ref, og_hbm, out_ref,
                        og_scr, sem, *, tk):
    i = pl.program_id(0)
    pc = MOE_PIECE
    gr = MOE_GATHER_ROWS
    n_rows = og_scr.shape[1]
    buf = lax.rem(i, 2)

    def segments(tile, slot, issue):
        def seg_copy(e, r, rows):
            return pltpu.make_async_copy(
                og_hbm.at[e, pl.ds(pl.multiple_of(base_ref[tile * N_EXPERTS + e] + r, pc), rows), :],
                og_scr.at[slot, pl.ds(pl.multiple_of(off_ref[tile * N_EXPERTS + e] + r, pc), rows), :], sem.at[slot])
        for e in range(N_EXPERTS):
            _for_pieces(len_ref[tile * N_EXPERTS + e], lambda r, rows, e=e: issue(seg_copy(e, r, rows)))

    @pl.when(i == 0)
    def _():
        segments(0, 0, lambda cp: cp.start())

    @pl.when(i + 1 < pl.num_programs(0))
    def _():
        segments(i + 1, 1 - buf, lambda cp: cp.start())

    last = i * N_EXPERTS + N_EXPERTS - 1
    total = off_ref[last] + len_ref[last]

    def clear(p, carry):
        og_scr[buf, pl.ds(pl.multiple_of(p * pc, pc), pc), :] = jnp.zeros((pc, og_scr.shape[2]), BF16)
        return carry
    lax.fori_loop(total // pc, n_rows // pc, clear, 0)
    segments(i, buf, lambda cp: cp.wait())

    out_ref[...] = jnp.zeros_like(out_ref)

    def scatter(ci, carry):
        r0 = pl.multiple_of(ci * gr, gr)
        rid = (lax.broadcasted_iota(jnp.int32, (gr, tk), 0) + r0).astype(F32)
        pg = (jnp.where(rid == slot_ref[0:1, :], slot_ref[2:3, :], 0.0)
              + jnp.where(rid == slot_ref[1:2, :], slot_ref[3:4, :], 0.0))
        out_ref[...] = out_ref[...] + _dot_tn(pg.astype(BF16), og_scr[buf, pl.ds(r0, gr), :])
        return carry
    lax.fori_loop(0, (total + gr - 1) // gr, scatter, 0)
    h3 = h_ref[...] + mod_ref[5:6, :] * out_ref[...]
    out_ref[...] = h3 * lax.rsqrt(jnp.mean(h3 * h3, axis=-1, keepdims=True) + EPS) * fg_ref[...]


def _moe_block_table(seg, n_blocks):
    ends = (seg[-1, 2, :, 0] + seg[-1, 1, :, 0]).astype(jnp.int32)
    nblk = (ends + MOE_BLOCK - 1) // MOE_BLOCK
    cum = jnp.cumsum(nblk)
    n_valid = cum[-1]
    g = jnp.minimum(jnp.arange(n_blocks, dtype=jnp.int32), n_valid - 1)
    eid = jnp.sum((g[:, None] >= cum[None, :]).astype(jnp.int32), axis=1)
    blk = g - (cum - nblk)[eid]
    return eid, blk, n_valid.reshape(1)


def _moe(o0, o1, gates, gate_blk, h, modtab, cng, dng, wo, ng, fg, rw, rb, wg, wu, wd, *, tk):
    bsz, seq, d = h.shape
    tpb = seq // tk
    n_tiles = bsz * tpb
    n_tok = bsz * seq
    tile_rows = -(-(2 * tk + N_EXPERTS * MOE_PIECE) // MOE_GATHER_ROWS) * MOE_GATHER_ROWS
    cap = -(-(n_tok + n_tiles * MOE_PIECE) // MOE_BLOCK) * MOE_BLOCK
    n_blocks = -(-(2 * n_tok + n_tiles * N_EXPERTS * MOE_PIECE) // MOE_BLOCK) + N_EXPERTS
    tok = lambda i, *_: (i // tpb, i % tpb, 0)
    mod = lambda i, *_: (i // tpb, 1, 0, 0)

    proj = lambda i: jnp.minimum(i, n_tiles - 1)
    routed = lambda i: jnp.maximum(i - 1, 0)
    tok_spec = pl.BlockSpec((None, tk, d), lambda i: tok(proj(i)))
    h2, slots, seg, xs = pl.pallas_call(
        functools.partial(_moe_route_kernel, tk=tk),
        grid=(n_tiles + 1,),
        in_specs=[tok_spec, tok_spec,
                  pl.BlockSpec((None, tk, d), lambda i: (proj(i) // tpb, proj(i) % tpb, gate_blk)),
                  tok_spec, pl.BlockSpec((None, None, 6, d), lambda i: mod(proj(i))),
                  _const_spec((1, 128)), _const_spec((1, 128)), _const_spec((d, d)),
                  _const_spec((1, d)), _const_spec((d, 128)), _const_spec((N_EXPERTS, 1))],
        out_specs=[tok_spec,
                   pl.BlockSpec((None, 8, tk), lambda i: (routed(i), 0, 0)),
                   pl.BlockSpec((None, 3, N_EXPERTS, 128), lambda i: (routed(i), 0, 0, 0)),
                   pl.BlockSpec(memory_space=pl.ANY)],
        out_shape=[jax.ShapeDtypeStruct(h.shape, F32),
                   jax.ShapeDtypeStruct((n_tiles, 8, tk), F32),
                   jax.ShapeDtypeStruct((n_tiles, 3, N_EXPERTS, 128), F32),
                   jax.ShapeDtypeStruct((N_EXPERTS, cap, d), BF16)],
        scratch_shapes=[pltpu.VMEM((2, tk, d), BF16), pltpu.VMEM((2, N_EXPERTS, tk), F32),
                        pltpu.VMEM((2, tile_rows, d), BF16), pltpu.VMEM((MOE_PIECE, d), BF16),
                        pltpu.SMEM((N_EXPERTS,), jnp.int32), pltpu.SMEM((2, 3 * N_EXPERTS), jnp.int32),
                        pltpu.SemaphoreType.DMA((2,))],
        compiler_params=_params(("arbitrary",), 40 << 20),
        name="l1_moe_route",
    )(o0, o1, gates, h, modtab, cng, dng, wo, ng, _pad_cols(rw, 128), rb)

    eid, blk, n_valid = _moe_block_table(seg, n_blocks)
    x_spec = pl.BlockSpec((None, MOE_BLOCK, d), lambda g, eid, blk, nv: (eid[g], blk[g], 0))
    og = pl.pallas_call(
        functools.partial(_moe_ffn_kernel, n_chunks=11),
        grid_spec=pltpu.PrefetchScalarGridSpec(
            num_scalar_prefetch=3, grid=(n_blocks,),
            in_specs=[x_spec,
                      pl.BlockSpec((None, d, D_FF), lambda g, eid, blk, nv: (eid[g], 0, 0)),
                      pl.BlockSpec((None, d, D_FF), lambda g, eid, blk, nv: (eid[g], 0, 0)),
                      pl.BlockSpec((None, D_FF, d), lambda g, eid, blk, nv: (eid[g], 0, 0))],
            out_specs=x_spec),
        out_shape=jax.ShapeDtypeStruct((N_EXPERTS, cap, d), BF16),
        compiler_params=_params(("arbitrary",), 52 << 20),
        name="l1_moe_experts",
    )(eid, blk, n_valid, xs, wg, wu, wd)

    tab = lambda k: seg[:, k, :, 0].astype(jnp.int32).reshape(-1)
    return pl.pallas_call(
        functools.partial(_moe_combine_kernel, tk=tk),
        grid_spec=pltpu.PrefetchScalarGridSpec(
            num_scalar_prefetch=3, grid=(n_tiles,),
            in_specs=[pl.BlockSpec((None, tk, d), tok), pl.BlockSpec((None, None, 6, d), mod),
                      pl.BlockSpec((1, d), lambda i, *_: (0, 0)),
                      pl.BlockSpec((None, 8, tk), lambda i, *_: (i, 0, 0)),
                      pl.BlockSpec(memory_space=pl.ANY)],
            out_specs=pl.BlockSpec((None, tk, d), tok),
            scratch_shapes=[pltpu.VMEM((2, tile_rows, d), BF16), pltpu.SemaphoreType.DMA((2,))]),
        out_shape=jax.ShapeDtypeStruct(h.shape, F32),
        compiler_params=_params(("arbitrary",), 40 << 20),
        name="l1_moe_combine",
    )(tab(2), tab(1), tab(0), h2, modtab, fg, slots, og)


def _block_diag_gate(gate_w):
    w = gate_w.reshape(2, 2, 2, 4, RG_BLOCK, RG_BLOCK)
    eye = jnp.eye(4, dtype=gate_w.dtype)
    return jnp.einsum('dghbij,bc->dghbicj', w, eye).reshape(2, 2, 2, 256, 256)


def _pad_cols(w, n):
    return jnp.pad(w, ((0, 0), (0, n - w.shape[1])))


def _layer0(ctx, x, modtab, norm_mix_g, norm_ffn_g, e_w_in, e_w_out, e_a_conv_w, e_a_conv_b, e_a_gate_w, e_a_gate_b,
            e_a_lambda, e_b_conv_w, e_b_a_log, e_b_dt_bias, e_b_norm_g, e_ffn_w_gate, e_ffn_w_up, e_ffn_w_down,
            *, tm, tt):
    bsz, ctx_len, d = ctx.shape
    w_in, w_tail = e_w_in.astype(BF16), _pad_cols(e_w_in[:, E_IN_MAIN:], 128).astype(BF16)
    gpar = jnp.zeros((2, 128), F32)
    gpar = gpar.at[0, 2 * DN_HEADS:4 * DN_HEADS].set(e_b_a_log.reshape(-1))
    gpar = gpar.at[1, 2 * DN_HEADS:4 * DN_HEADS].set(e_b_dt_bias.reshape(-1))
    ua, gay, q, k, v, sz, gb = _inproj0(ctx, x, modtab, norm_mix_g.reshape(1, d), w_in, w_tail, e_a_conv_w,
                                        e_a_conv_b.reshape(1, -1), e_b_conv_w, gpar, tm=tm)
    wg = _block_diag_gate(e_a_gate_w).astype(BF16)
    hf, hb = _rglru(ua, wg, e_a_gate_b.reshape(4, RG_WIDTH), e_a_lambda, tt=tt, ctx_len=ctx_len)
    o0, o1 = _delta(q, k, v, gb, ctx_len=ctx_len, rows=SCAN_ROWS)
    return _l0_tail(hf, hb, gay, o0, o1, sz, ctx, x, modtab, e_b_norm_g.reshape(1, -1), e_w_out.astype(BF16),
                    norm_ffn_g.reshape(1, d), e_ffn_w_gate.astype(BF16), e_ffn_w_up.astype(BF16),
                    e_ffn_w_down.astype(BF16), tm=tm)


def _layer1(hc, hl, modtab, norm_mix_g, norm_ffn_g, final_norm_g, o_w_in, o_w_out, o_lb_logits, o_c_norm_g,
            o_d_gate_w2, o_d_gate_b2, o_d_norm_g, o_router_w, o_router_b, o_moe_w_gate, o_moe_w_up, o_moe_w_down,
            *, tm, tk, layer):
    bsz, seq, d = hl.shape
    ctx_len = hc.shape[1]
    rows = seq // GRID_W
    hl = hl.reshape(bsz, rows, GRID_W, d).swapaxes(1, 2).reshape(bsz, seq, d)
    w_in, w_tail = o_w_in.astype(BF16), _pad_cols(o_w_in[:, O_IN_MAIN:], 128).astype(BF16)
    wlr = jnp.zeros((128, SEG), F32)
    wlr = wlr.at[0:GLA_RANK, 0:256].set(o_d_gate_w2[0]).at[GLA_RANK:2 * GLA_RANK, 256:512].set(o_d_gate_w2[1])
    proj = functools.partial(_inproj1, g=norm_mix_g.reshape(1, d), w=w_in, w_tail=w_tail, lbl=o_lb_logits, wlr=wlr,
                             b2=o_d_gate_b2.reshape(1, SEG), layer=layer)
    p1c = proj(hc, 0, ctx_len, modtab, 0, tm=tm)
    p1l = proj(hl, 0, seq, modtab, 1, tm=2 * tm)
    sh0, sg0 = _mix1_ctx(p1c, rows=SCAN_ROWS)
    o0, o1 = _mix1_lat(p1l, sh0, sg0, rows=SCAN_ROWS)
    return _moe(o0, o1, p1l[S_CG[0]], S_CG[1] * SEG // d, hl, modtab, o_c_norm_g.reshape(1, -1),
                o_d_norm_g.reshape(1, -1), o_w_out.astype(BF16), norm_ffn_g.reshape(1, d),
                final_norm_g.reshape(1, d), o_router_w, o_router_b.reshape(N_EXPERTS, 1),
                o_moe_w_gate.astype(BF16), o_moe_w_up.astype(BF16), o_moe_w_down.astype(BF16), tk=tk)


def kernel(x, c, ctx, c_ctx, ada_w, ada_b, norm_mix_g, norm_ffn_g, final_norm_g, e_w_in, e_w_out, e_a_conv_w, e_a_conv_b, e_a_gate_w, e_a_gate_b, e_a_lambda, e_b_conv_w, e_b_a_log, e_b_dt_bias, e_b_norm_g, e_ffn_w_gate, e_ffn_w_up, e_ffn_w_down, o_w_in, o_w_out, o_lb_logits, o_c_norm_g, o_d_gate_w2, o_d_gate_b2, o_d_norm_g, o_router_w, o_router_b, o_moe_w_gate, o_moe_w_up, o_moe_w_down):
    bsz, seq, d = x.shape
    ctx_len = ctx.shape[1]
    assert bsz == 8 and d == D_MODEL and ada_w.shape[0] == 2
    tm = min(256, ctx_len)
    tt = min(128, ctx_len)
    tk = min(512, seq)
    assert ctx_len % tm == 0 and seq % tm == 0 and ctx_len % SCAN_ROWS == 0 and seq % SCAN_ROWS == 0
    assert seq % GRID_W == 0 and seq % tk == 0

    mods = _ada(c, c_ctx, ada_w, ada_b)
    hc, hl = _layer0(ctx, x, _modtab(mods[0], bsz), norm_mix_g[0], norm_ffn_g[0], e_w_in[0], e_w_out[0],
                     e_a_conv_w[0], e_a_conv_b[0], e_a_gate_w[0], e_a_gate_b[0], e_a_lambda[0], e_b_conv_w[0],
                     e_b_a_log[0], e_b_dt_bias[0], e_b_norm_g[0], e_ffn_w_gate[0], e_ffn_w_up[0], e_ffn_w_down[0],
                     tm=tm, tt=tt)
    out_cm = _layer1(hc, hl, _modtab(mods[1], bsz), norm_mix_g[1], norm_ffn_g[1], final_norm_g, o_w_in[0],
                     o_w_out[0], o_lb_logits, o_c_norm_g[0], o_d_gate_w2[0], o_d_gate_b2[0], o_d_norm_g[0],
                     o_router_w[0], o_router_b[0], o_moe_w_gate[0], o_moe_w_up[0], o_moe_w_down[0],
                     tm=tm, tk=tk, layer=1)
    rows = seq // GRID_W
    return out_cm.reshape(bsz, GRID_W, rows, d).swapaxes(1, 2).reshape(bsz, seq, d)
```

```python
import functools

import jax
import jax.numpy as jnp
from jax import lax
from jax.experimental import pallas as pl
from jax.experimental.pallas import tpu as pltpu

F32 = jnp.float32
BF16 = jnp.bfloat16
HI = lax.Precision.HIGHEST

EPS = 1e-6
D_MODEL = 1024
GRID_W = 64
CONV_K = 4
RG_WIDTH = 512
RG_BLOCK = 64
RG_C = 8.0
DN_HEADS = 4
DN_D = 128
DN_CHUNK = 64
HG_HEADS = 4
HG_D = 128
GLA_HEADS = 4
GLA_DK = 64
GLA_DV = 128
GLA_RANK = 16
GLA_GATE_NORM = 16.0
MIX1_CHUNK = 64
SCAN_ROWS = 256
D_FF = 2816
N_EXPERTS = 8

E_IN_MAIN = 3072
E_IN_PAD = E_IN_MAIN + 128
O_IN_MAIN = 4096
O_IN_PAD = O_IN_MAIN + 128
SEG = 512
S_HQ, S_HV, S_GV, S_GQK, S_HK0, S_HK1, S_CG, S_DG = [(0, i) for i in range(8)]
N_SHARED = 4
S_HLF0, S_HLF1, S_GLD = [(1, i) for i in range(3)]
N_SEG = (8, 3)
P1_DTYPES = (BF16, F32)

V7X_VMEM_BYTES = 64 * 1024 * 1024
VMEM_HEADROOM_BYTES = 8 * 1024 * 1024
MOE_PIECE = 16
MOE_SEG_PIECES = (64, MOE_PIECE)
MOE_BLOCK = 512
MOE_GATHER_ROWS = 256


def _vmem(nbytes):
    return int(min(V7X_VMEM_BYTES - VMEM_HEADROOM_BYTES, nbytes))


def _params(sem, vmem_bytes):
    return pltpu.CompilerParams(dimension_semantics=sem, vmem_limit_bytes=_vmem(vmem_bytes))


def _sigmoid(x):
    return jax.nn.sigmoid(x)


def _sigmoid_tanh(x):
    return 0.5 * jnp.tanh(0.5 * x) + 0.5


def _silu(x):
    return x * jax.nn.sigmoid(x)


def _softplus(x):
    return jnp.maximum(x, 0.0) + jnp.log1p(jnp.exp(-jnp.abs(x)))


def _gelu_tanh(x):
    return 0.5 * x * (1.0 + jnp.tanh(0.7978845608028654 * (x + 0.044715 * (x * x * x))))


def _normmod(x, g, shift, scale):
    y = x * lax.rsqrt(jnp.mean(x * x, axis=-1, keepdims=True) + EPS)
    return (y * g) * (1.0 + scale) + shift


def _dot(a, b):
    return jnp.dot(a, b, preferred_element_type=F32)


def _dot_nt(a, b):
    return lax.dot_general(a, b, (((1,), (1,)), ((), ())), preferred_element_type=F32)


def _dot_tn(a, b):
    return lax.dot_general(a, b, (((0,), (0,)), ((), ())), preferred_element_type=F32)


def _dot_hi(a, b):
    return jnp.dot(a, b, precision=HI, preferred_element_type=F32)


def _split3(x):
    hi = x.astype(BF16)
    r1 = x - hi.astype(F32)
    mid = r1.astype(BF16)
    return hi, mid, (r1 - mid.astype(F32)).astype(BF16)


def _dot_hilo(a, b):
    ah = a.astype(BF16)
    al = (a - ah.astype(F32)).astype(BF16)
    bh = b.astype(BF16)
    bl = (b - bh.astype(F32)).astype(BF16)
    return _dot(ah, bh) + (_dot(ah, bl) + _dot(al, bh))


def _mask_dot(mask, x):
    mb = mask.astype(BF16)
    hi, mid, lo = _split3(x)
    return _dot(mb, hi) + (_dot(mb, mid) + _dot(mb, lo))


def _mask_dot_tn(x, mask):
    mb = mask.astype(BF16)
    hi, mid, lo = _split3(x)
    return _dot_tn(hi, mb) + (_dot_tn(mid, mb) + _dot_tn(lo, mb))


def _const_spec(shape):
    nd = len(shape)
    return pl.BlockSpec(shape, lambda *_: (0,) * nd, pipeline_mode=pl.Buffered(1))


def _scan_masks(c, d):
    row = lax.broadcasted_iota(jnp.int32, (c, c), 0)
    col = lax.broadcasted_iota(jnp.int32, (c, c), 1)
    dlt = row - col if d == 0 else col - row
    return dlt >= 0, dlt > 0, dlt <= 0, row == col


def _ada_kernel(cv_ref, w_ref, b_ref, o_ref):
    s = _silu(cv_ref[...]).astype(BF16)
    o_ref[...] = _dot(s, w_ref[...].astype(BF16)) + b_ref[...]


def _ada(c, c_ctx, ada_w, ada_b):
    depth, d, n6 = ada_w.shape
    bsz = c.shape[0]
    rows = 16
    cv = jnp.zeros((rows, d), F32).at[:bsz].set(c).at[bsz].set(c_ctx)
    tn = 1536
    return pl.pallas_call(
        _ada_kernel,
        grid=(depth, n6 // tn),
        in_specs=[pl.BlockSpec((rows, d), lambda l, j: (0, 0)),
                  pl.BlockSpec((None, d, tn), lambda l, j: (l, 0, j)),
                  pl.BlockSpec((None, 1, tn), lambda l, j: (l, 0, j))],
        out_specs=pl.BlockSpec((None, rows, tn), lambda l, j: (l, 0, j)),
        out_shape=jax.ShapeDtypeStruct((depth, rows, n6), F32),
        compiler_params=_params(("arbitrary", "arbitrary"), 32 << 20),
        name="ada_mod",
    )(cv, ada_w, ada_b.reshape(depth, 1, n6))


def _modtab(mods_l, bsz):
    m = mods_l.reshape(mods_l.shape[0], 6, D_MODEL)
    lat = m[:bsz]
    ctx = jnp.broadcast_to(m[bsz][None], (bsz, 6, D_MODEL))
    return jnp.stack([ctx, lat], axis=1)


def _inproj0_kernel(cp_ref, cm_ref, cn_ref, xp_ref, xm_ref, xn_ref, mod_ref, g_ref, w_ref, wt_ref, acw_ref, acb_ref,
                    bcw_ref, gpar_ref, ua_ref, gay_ref, q_ref, k_ref, v_ref, sz_ref, gb_ref, u_scr,
                    *, tm, ctx_tiles, n_tiles):
    t = pl.program_id(1)
    pick = lambda c_ref, x_ref: jnp.where(t < ctx_tiles, c_ref[...], x_ref[...])
    x = jnp.concatenate([pick(cp_ref, xp_ref), pick(cm_ref, xm_ref), pick(cn_ref, xn_ref)], axis=0)
    xm = _normmod(x, g_ref[...], mod_ref[0:1, :], mod_ref[1:2, :]).astype(BF16)
    seg_first = jnp.logical_or(t == 0, t == ctx_tiles)
    seg_last = jnp.logical_or(t == ctx_tiles - 1, t == n_tiles - 1)

    def project(c0, width, conv_input):
        u_scr[:, c0:c0 + width] = _dot(xm, wt_ref[...] if c0 == E_IN_MAIN else w_ref[:, c0:c0 + width])
        if conv_input:
            u_scr[0:8, c0:c0 + width] = jnp.where(seg_first, 0.0, u_scr[0:8, c0:c0 + width])
            u_scr[tm + 8:tm + 16, c0:c0 + width] = jnp.where(seg_last, 0.0, u_scr[tm + 8:tm + 16, c0:c0 + width])

    def conv(c0, width, w_ref_, w0):
        acc = u_scr[6:6 + tm, c0:c0 + width] * w_ref_[0:1, w0:w0 + width]
        for j in range(1, CONV_K):
            acc = acc + u_scr[6 + j:6 + j + tm, c0:c0 + width] * w_ref_[j:j + 1, w0:w0 + width]
        return acc

    project(0, RG_WIDTH, True)
    for grp in range(RG_WIDTH // 128):
        c0 = grp * 128
        ua_ref[:, c0:c0 + 128] = conv(c0, 128, acw_ref, c0) + acb_ref[0:1, c0:c0 + 128]
    project(512, 512, False)
    gay_ref[...] = _gelu_tanh(u_scr[8:8 + tm, 512:1024]).astype(BF16)

    for grp in range(3 * DN_HEADS):
        c0 = grp * 128
        if grp % DN_HEADS == 0:
            project(1024 + c0, DN_HEADS * DN_D, True)
        y = _silu(conv(1024 + c0, 128, bcw_ref, c0))
        if grp < 2 * DN_HEADS:
            y = y * lax.rsqrt(jnp.sum(y * y, axis=-1, keepdims=True) + EPS)
        if grp < DN_HEADS:
            q_ref[:, c0:c0 + 128] = (y * (DN_D ** -0.5)).astype(BF16)
        elif grp < 2 * DN_HEADS:
            k_ref[:, c0 - 512:c0 - 384] = y.astype(BF16)
        else:
            v_ref[:, c0 - 1024:c0 - 896] = y.astype(BF16)
    project(2560, 512, False)
    sz_ref[...] = _silu(u_scr[8:8 + tm, 2560:3072]).astype(BF16)

    project(3072, 128, False)
    xg = u_scr[8:8 + tm, 3072:3200]
    lane = lax.broadcasted_iota(jnp.int32, xg.shape, 1)
    g = -jnp.exp(gpar_ref[0:1, :]) * _softplus(xg + gpar_ref[1:2, :])
    gb_ref[...] = jnp.where(lane < 2 * DN_HEADS, _sigmoid(xg), g)


def _row_specs(tm, d, ctx_tiles, ctx_len, seq, halo):
    tb = tm // 8

    def specs(n_rows, tile_of):
        main = pl.BlockSpec((None, tm, d), lambda b, t: (b, jnp.clip(tile_of(t), 0, n_rows // tm - 1), 0))
        if not halo:
            return [main]
        prev = pl.BlockSpec((None, 8, d), lambda b, t: (b, jnp.clip(tile_of(t) * tb - 1, 0, n_rows // 8 - 1), 0))
        nxt = pl.BlockSpec((None, 8, d), lambda b, t: (b, jnp.clip((tile_of(t) + 1) * tb, 0, n_rows // 8 - 1), 0))
        return [prev, main, nxt]

    return specs(ctx_len, lambda t: t) + specs(seq, lambda t: t - ctx_tiles)


def _inproj0(ctx, x, modtab, g, w, w_tail, acw, acb, bcw, gpar, *, tm):
    bsz, ctx_len, d = ctx.shape
    seq = x.shape[1]
    t_all = ctx_len + seq
    n_tiles = t_all // tm
    ctx_tiles = ctx_len // tm
    kern = functools.partial(_inproj0_kernel, tm=tm, ctx_tiles=ctx_tiles, n_tiles=n_tiles)
    tok = lambda w, dt=BF16: jax.ShapeDtypeStruct((bsz, t_all, w), dt)
    tok_spec = lambda w: pl.BlockSpec((None, tm, w), lambda b, t: (b, t, 0))
    return pl.pallas_call(
        kern,
        grid=(bsz, n_tiles),
        in_specs=_row_specs(tm, d, ctx_tiles, ctx_len, seq, True) + [
            pl.BlockSpec((None, None, 6, d), lambda b, t: (b, jnp.where(t >= ctx_tiles, 1, 0), 0, 0)),
            _const_spec((1, d)),
            _const_spec(w.shape), _const_spec((d, 128)),
            _const_spec((CONV_K, RG_WIDTH)),
            _const_spec((1, RG_WIDTH)),
            _const_spec((CONV_K, 3 * DN_HEADS * DN_D)),
            _const_spec((2, 128)),
        ],
        out_specs=[tok_spec(512), tok_spec(512), tok_spec(512), tok_spec(512), tok_spec(512), tok_spec(512),
                   tok_spec(128)],
        out_shape=[tok(512, F32), tok(512), tok(512), tok(512), tok(512), tok(512), tok(128, F32)],
        scratch_shapes=[pltpu.VMEM((tm + 16, E_IN_PAD), F32)],
        compiler_params=_params(("arbitrary", "arbitrary"), 40 << 20),
        name="l0_inproj",
    )(ctx, ctx, ctx, x, x, x, modtab, g, w, w_tail, acw, acb, bcw, gpar)


def _rglru_kernel(uf_ref, ub_ref, wg_ref, gbias_ref, lam_ref, hf_ref, hb_ref,
                  af_scr, xf_scr, ab_scr, xb_scr, stg_scr, of_scr, ob_scr, h_scr, *, tt, bsz):
    s = pl.program_id(0)
    n_slab = RG_WIDTH // 128

    @pl.when(s == 0)
    def _():
        h_scr[...] = jnp.zeros_like(h_scr)

    def gates(u_ref, d, a_scr, x_scr):
        for b in range(bsz):
            for j in range(n_slab):
                stg_scr[j, pl.ds(b, tt, stride=bsz), :] = u_ref[b, :, j * 128:(j + 1) * 128]
        x = jnp.concatenate([stg_scr[j] for j in range(n_slab)], axis=1)
        xb = x.astype(BF16)
        for half in range(2):
            c0 = half * 256
            xh = xb[:, c0:c0 + 256]
            r = _sigmoid_tanh(_dot(xh, wg_ref[d, 0, half]) + gbias_ref[2 * d:2 * d + 1, c0:c0 + 256])
            i = _sigmoid_tanh(_dot(xh, wg_ref[d, 1, half]) + gbias_ref[2 * d + 1:2 * d + 2, c0:c0 + 256])
            log_a = (-RG_C) * r * _softplus(-lam_ref[d:d + 1, c0:c0 + 256])
            a = jnp.exp(log_a)
            mult = jnp.sqrt(-jnp.tanh(log_a) * (a * a + 1.0))
            xin = mult * (i * x[:, c0:c0 + 256])
            a_scr[:, :, c0:c0 + 256] = a.reshape(tt, bsz, 256)
            x_scr[:, :, c0:c0 + 256] = xin.reshape(tt, bsz, 256)

    gates(uf_ref, 0, af_scr, xf_scr)
    gates(ub_ref, 1, ab_scr, xb_scr)

    def put(o_scr, t, h):
        rows = pl.ds(pl.multiple_of(t * bsz, bsz), bsz)
        for j in range(n_slab):
            o_scr[j, rows, :] = h[:, j * 128:(j + 1) * 128]

    def step(t, carry):
        hf, hb = carry
        hf = af_scr[t] * hf + xf_scr[t]
        put(of_scr, t, hf)
        tb = tt - 1 - t
        hb = ab_scr[tb] * hb + xb_scr[tb]
        put(ob_scr, tb, hb)
        return hf, hb

    hf, hb = lax.fori_loop(0, tt, step, (h_scr[0], h_scr[1]), unroll=8)
    h_scr[0] = hf
    h_scr[1] = hb
    for o_scr, o_ref in ((of_scr, hf_ref), (ob_scr, hb_ref)):
        for b in range(bsz):
            for j in range(n_slab):
                o_ref[b, :, j * 128:(j + 1) * 128] = o_scr[j, pl.ds(b, tt, stride=bsz), :]


def _rglru(ua, wg, gbias, lam, *, tt, ctx_len):
    bsz, t_all, w = ua.shape
    n_steps = t_all // tt
    nc = ctx_len // tt

    def bwd(s):
        return jnp.where(s < nc, nc - 1 - s, n_steps + nc - 1 - s)

    blk = (bsz, tt, w)
    tm_blk = (tt, bsz, w)
    slabs = (w // 128, tt * bsz, 128)
    kern = functools.partial(_rglru_kernel, tt=tt, bsz=bsz)
    return pl.pallas_call(
        kern,
        grid=(n_steps,),
        in_specs=[pl.BlockSpec(blk, lambda s: (0, s, 0)),
                  pl.BlockSpec(blk, lambda s: (0, bwd(s), 0)),
                  _const_spec(wg.shape), _const_spec(gbias.shape), _const_spec(lam.shape)],
        out_specs=[pl.BlockSpec(blk, lambda s: (0, s, 0)),
                   pl.BlockSpec(blk, lambda s: (0, bwd(s), 0))],
        out_shape=[jax.ShapeDtypeStruct(ua.shape, F32)] * 2,
        scratch_shapes=[pltpu.VMEM(tm_blk, F32)] * 4 + [pltpu.VMEM(slabs, F32)] * 3 + [pltpu.VMEM((2, bsz, w), F32)],
        compiler_params=_params(("arbitrary",), 48 << 20),
        name="l0_rglru",
    )(ua, ua, wg, gbias, lam)


def _delta_kernel(qf_ref, kf_ref, vf_ref, gf_ref, qb_ref, kb_ref, vb_ref, gb_ref, of_ref, ob_ref, s_scr, *, n_sub):
    c = DN_CHUNK

    @pl.when(pl.program_id(1) == 0)
    def _():
        s_scr[...] = jnp.zeros_like(s_scr)

    dir_refs = ((qf_ref, kf_ref, vf_ref, gf_ref, of_ref), (qb_ref, kb_ref, vb_ref, gb_ref, ob_ref))
    masks = [_scan_masks(c, d) for d in range(2)]
    eye = jnp.where(masks[0][3], 1.0, 0.0)

    cums = {}
    for d in range(2):
        incl, _, incl_t, _ = masks[d]
        m_incl = jnp.where(incl, 1.0, 0.0)
        m_incl_t = jnp.where(incl_t, 1.0, 0.0)
        for ci in range(n_sub):
            g_all = dir_refs[d][3][ci * c:(ci + 1) * c, :]
            gc_all = _mask_dot(m_incl, g_all)
            gct_all = _mask_dot_tn(g_all, m_incl_t)
            cums[d, ci] = (g_all, gc_all, gct_all)

    chains = []
    for d in range(2):
        q_ref, k_ref, v_ref, _, _ = dir_refs[d]
        incl, strict, _, _ = masks[d]
        last = c - 1 if d == 0 else 0
        for ci in range(n_sub):
            g_all, gc_all, gct_all = cums[d, ci]
            rs = slice(ci * c, (ci + 1) * c)
            for h in range(DN_HEADS):
                hs = slice(h * DN_D, (h + 1) * DN_D)
                lane = 2 * DN_HEADS + d * DN_HEADS + h
                ch = dict(d=d, ci=ci, h=h, rs=rs, hs=hs, incl=incl, strict=strict)
                ch["beta"] = g_all[:, d * DN_HEADS + h:d * DN_HEADS + h + 1]
                gc = jnp.broadcast_to(gc_all[:, lane:lane + 1], (c, DN_D))
                gc_row = jnp.broadcast_to(gct_all[lane:lane + 1, :], (c, c))
                ch["gc"] = gc
                ch["gtot"] = gc[last:last + 1, :]
                ch["decay"] = jnp.where(incl, jnp.exp(jnp.minimum(gc[:, 0:c] - gc_row, 0.0)), 0.0)
                ch["e_gc"] = jnp.exp(gc)
                ch["q"] = q_ref[rs, hs].astype(F32)
                ch["k"] = k_ref[rs, hs].astype(F32)
                ch["v"] = v_ref[rs, hs].astype(F32)
                chains.append(ch)

    for ch in chains:
        ch["kb"] = ch["k"] * ch["beta"]
        qk = _dot_nt(jnp.concatenate([ch["kb"], ch["q"]], axis=0).astype(BF16), ch["k"].astype(BF16))
        ch["neg"] = -jnp.where(ch["strict"], qk[0:c] * ch["decay"], 0.0)
        ch["a_qk"] = (qk[c:2 * c] * ch["decay"]).astype(BF16)
    for ch in chains:
        negb = ch["neg"].astype(BF16)
        ch["t"] = eye + ch["neg"]
        ch["p"] = _dot(negb, negb)
    n_sq = max(1, (c - 1).bit_length() - 1)
    for it in range(n_sq):
        for ch in chains:
            tp = _dot(jnp.concatenate([ch["t"], ch["p"]], axis=0).astype(BF16), ch["p"].astype(BF16))
            ch["t"] = ch["t"] + tp[0:c]
            ch["p"] = tp[c:2 * c]
    for ch in chains:
        rhs = jnp.concatenate([ch["v"] * ch["beta"], ch["kb"] * ch["e_gc"]], axis=1).astype(BF16)
        sol = _dot(ch["t"].astype(BF16), rhs)
        ch["u"] = sol[:, 0:DN_D]
        ch["wq"] = jnp.concatenate([sol[:, DN_D:2 * DN_D], ch["q"] * ch["e_gc"]], axis=0).astype(BF16)
        ch["k_tail"] = (ch["k"] * jnp.exp(ch["gtot"] - ch["gc"])).astype(BF16)

    by_key = {(ch["d"], ch["ci"], ch["h"]): ch for ch in chains}
    for step in range(n_sub):
        live = [by_key[d, step if d == 0 else n_sub - 1 - step, h] for d in range(2) for h in range(DN_HEADS)]
        for ch in live:
            ch["st"] = s_scr[ch["d"], ch["h"]]
            ch["ws"] = _dot(ch["wq"], ch["st"].astype(BF16))
        for ch in live:
            vnb = (ch["u"] - ch["ws"][0:c]).astype(BF16)
            o = ch["ws"][c:2 * c] + _dot(ch["a_qk"], vnb)
            dir_refs[ch["d"]][4][ch["rs"], ch["hs"]] = o.astype(BF16)
            s_scr[ch["d"], ch["h"]] = ch["st"] * jnp.exp(ch["gtot"]) + _dot_tn(ch["k_tail"], vnb)


def _delta(q, k, v, gb, *, ctx_len, rows):
    bsz, t_all, w = q.shape
    n_steps = t_all // rows
    nc = ctx_len // rows

    def bwd(s):
        return jnp.where(s < nc, nc - 1 - s, n_steps + nc - 1 - s)

    fwd_spec = lambda width: pl.BlockSpec((None, rows, width), lambda b, s: (b, s, 0))
    bwd_spec = lambda width: pl.BlockSpec((None, rows, width), lambda b, s: (b, bwd(s), 0))
    return pl.pallas_call(
        functools.partial(_delta_kernel, n_sub=rows // DN_CHUNK),
        grid=(bsz, n_steps),
        in_specs=[fwd_spec(w), fwd_spec(w), fwd_spec(w), fwd_spec(128),
                  bwd_spec(w), bwd_spec(w), bwd_spec(w), bwd_spec(128)],
        out_specs=[fwd_spec(w), bwd_spec(w)],
        out_shape=[jax.ShapeDtypeStruct((bsz, t_all, w), BF16)] * 2,
        scratch_shapes=[pltpu.VMEM((2, DN_HEADS, DN_D, DN_D), F32)],
        compiler_params=_params(("arbitrary", "arbitrary"), 32 << 20),
        name="l0_deltanet",
    )(q, k, v, gb, q, k, v, gb)


def _head_norm(y, g):
    return y * lax.rsqrt(jnp.mean(y * y, axis=-1, keepdims=True) + EPS) * g


def _l0_tail_kernel(hf_ref, hb_ref, gay_ref, o0_ref, o1_ref, sz_ref, ctx_ref, x_ref, mod_ref, ng_ref, wo_ref, g_ref,
                    wg_ref, wu_ref, wd_ref, hc_ref, hl_ref, *, n_chunks, ctx_tiles):
    t = pl.program_id(1)
    tm = hf_ref.shape[0]
    halves = [dict(rs=slice(i * tm // 2, (i + 1) * tm // 2)) for i in range(2)]
    for hv in halves:
        rs = hv["rs"]
        parts = [((hf_ref[rs, :] + hb_ref[rs, :]) * gay_ref[rs, :].astype(F32)).astype(BF16)]
        for hd in range(DN_HEADS):
            lo = hd * DN_D
            ob = o0_ref[rs, lo:lo + DN_D].astype(F32) + o1_ref[rs, lo:lo + DN_D].astype(F32)
            parts.append((_head_norm(ob, ng_ref[...]) * sz_ref[rs, lo:lo + DN_D].astype(F32)).astype(BF16))
        hv["ycat"] = jnp.concatenate(parts, axis=-1)
    for hv in halves:
        h = jnp.where(t < ctx_tiles, ctx_ref[hv["rs"], :], x_ref[hv["rs"], :])
        hv["x"] = h + mod_ref[2:3, :] * _dot(hv["ycat"], wo_ref[...])
    for hv in halves:
        hv["xm"] = _normmod(hv["x"], g_ref[...], mod_ref[3:4, :], mod_ref[4:5, :]).astype(BF16)
        hv["acc"] = jnp.zeros(hv["x"].shape, F32)
    cw = D_FF // n_chunks
    for ci in range(n_chunks):
        c0 = ci * cw
        for hv in halves:
            hv["act"] = (_silu(_dot(hv["xm"], wg_ref[:, c0:c0 + cw])) * _dot(hv["xm"], wu_ref[:, c0:c0 + cw])).astype(BF16)
        for hv in halves:
            hv["acc"] = hv["acc"] + _dot(hv["act"], wd_ref[c0:c0 + cw, :])
    out = jnp.concatenate([hv["x"] + mod_ref[5:6, :] * hv["acc"] for hv in halves], axis=0)

    @pl.when(t < ctx_tiles)
    def _():
        hc_ref[...] = out

    @pl.when(t >= ctx_tiles)
    def _():
        hl_ref[...] = out


def _l0_tail(hf, hb, gay, o0, o1, sz, ctx, x, modtab, ng, wo, g, wg, wu, wd, *, tm):
    bsz, ctx_len, d = ctx.shape
    seq = x.shape[1]
    t_all = ctx_len + seq
    ctx_tiles = ctx_len // tm
    tok = lambda width: pl.BlockSpec((None, tm, width), lambda b, t: (b, t, 0))
    return pl.pallas_call(
        functools.partial(_l0_tail_kernel, n_chunks=2, ctx_tiles=ctx_tiles),
        grid=(bsz, t_all // tm),
        in_specs=[tok(512)] * 6 + _row_specs(tm, d, ctx_tiles, ctx_len, seq, False) + [
                  pl.BlockSpec((None, None, 6, d), lambda b, t: (b, jnp.where(t >= ctx_tiles, 1, 0), 0, 0)),
                  _const_spec((1, DN_D)), _const_spec((d, d)),
                  _const_spec((1, d)), _const_spec((d, D_FF)), _const_spec((d, D_FF)), _const_spec((D_FF, d))],
        out_specs=[pl.BlockSpec((None, tm, d), lambda b, t: (b, jnp.minimum(t, ctx_tiles - 1), 0)),
                   pl.BlockSpec((None, tm, d), lambda b, t: (b, jnp.maximum(t - ctx_tiles, 0), 0))],
        out_shape=[jax.ShapeDtypeStruct((bsz, ctx_len, d), F32), jax.ShapeDtypeStruct((bsz, t_all - ctx_len, d), F32)],
        compiler_params=_params(("arbitrary", "arbitrary"), 48 << 20),
        name="l0_tail",
    )(hf, hb, gay, o0, o1, sz, ctx, x, modtab, ng, wo, g, wg, wu, wd)


def _inproj1_kernel(h_ref, mod_ref, g_ref, w_ref, wt_ref, lbl_ref, wlr_ref, b2_ref, pa_ref, pb_ref, u_scr, *, layer):
    x = h_ref[...]
    xm = _normmod(x, g_ref[...], mod_ref[0:1, :], mod_ref[1:2, :]).astype(BF16)

    lg = lbl_ref[...]
    ex = jnp.exp(lg - jnp.max(lg, axis=0, keepdims=True))
    lbw = ex / jnp.sum(ex, axis=0, keepdims=True)
    lb = jnp.sum(lbw[1:layer + 1], axis=0, keepdims=True)

    def project(c0, width):
        u_scr[:, c0:c0 + width] = _dot(xm, wt_ref[...] if c0 == O_IN_MAIN else w_ref[:, c0:c0 + width])

    def put(seg, off, val):
        ref = (pa_ref, pb_ref)[seg[0]]
        ref[:, seg[1] * SEG + off:seg[1] * SEG + off + val.shape[1]] = val.astype(ref.dtype)

    def groups(fn):
        for grp in range(SEG // 128):
            fn(grp * 128)

    project(0, SEG)
    groups(lambda c0: put(S_HQ, c0, _silu(u_scr[:, c0:c0 + 128]) * (HG_D ** -0.5)))
    for dr, (sk, sf) in enumerate(((S_HK0, S_HLF0), (S_HK1, S_HLF1))):
        project(512 + dr * 512, SEG)

        def forget(c0, dr=dr, sk=sk, sf=sf):
            lbg = lb[:, c0:c0 + 128]
            fl = u_scr[:, 512 + dr * 512 + c0:512 + dr * 512 + c0 + 128]
            sg = _sigmoid(fl)
            put(sf, c0, jnp.log(lbg + (1.0 - lbg) * sg))
            put(sk, c0, (1.0 - lbg) * (1.0 - sg))
        groups(forget)
    project(1536, SEG)
    groups(lambda c0: put(S_HV, c0, u_scr[:, 1536 + c0:1536 + c0 + 128]))
    project(2048, SEG)
    groups(lambda c0: put(S_CG, c0, _silu(u_scr[:, 2048 + c0:2048 + c0 + 128])))
    project(2560, SEG)
    put(S_GQK, 0, u_scr[:, 2560:2816] * (GLA_DK ** -0.5))
    put(S_GQK, 256, u_scr[:, 2816:3072])
    project(3072, SEG)
    groups(lambda c0: put(S_GV, c0, u_scr[:, 3072 + c0:3072 + c0 + 128]))
    project(3584, SEG)
    groups(lambda c0: put(S_DG, c0, _silu(u_scr[:, 3584 + c0:3584 + c0 + 128])))
    project(4096, 128)
    lr = u_scr[:, 4096:4224]
    put(S_GLD, 0, -_softplus(-(_dot_hilo(lr, wlr_ref[...]) + b2_ref[...])) * (1.0 / GLA_GATE_NORM))


def _inproj1(h, row0, n_rows, modtab, seg, g, w, w_tail, lbl, wlr, b2, *, tm, layer):
    bsz, _, d = h.shape
    t0 = row0 // tm
    return pl.pallas_call(
        functools.partial(_inproj1_kernel, layer=layer),
        grid=(bsz, n_rows // tm),
        in_specs=[pl.BlockSpec((None, tm, d), lambda b, t: (b, t0 + t, 0)),
                  pl.BlockSpec((None, None, 6, d), lambda b, t: (b, seg, 0, 0)),
                  _const_spec((1, d)), _const_spec(w.shape), _const_spec((d, 128)), _const_spec(lbl.shape),
                  _const_spec((128, SEG)), _const_spec((1, SEG))],
        out_specs=[pl.BlockSpec((None, tm, n * SEG), lambda b, t: (b, t, 0)) for n in N_SEG],
        out_shape=[jax.ShapeDtypeStruct((bsz, n_rows, n * SEG), dt) for n, dt in zip(N_SEG, P1_DTYPES)],
        scratch_shapes=[pltpu.VMEM((tm, O_IN_PAD), F32)],
        compiler_params=_params(("arbitrary", "arbitrary"), 44 << 20),
        name="l1_inproj",
    )(h, modtab, g, w, w_tail, lbl, wlr, b2)


def _gla_stream(d, q_all, k_all, ld_all, v_all, st_ref, o_ref, o_lane0, r0, n_heads, dk, dv, incl, m_incl):
    c = k_all.shape[0]
    mid = c // 2 - 1 if d == 0 else c // 2
    last = c - 1 if d == 0 else 0
    bc = _mask_dot(m_incl, ld_all)
    m = bc[mid:mid + 1]
    btot = bc[last:last + 1]
    kn = k_all.astype(F32) * jnp.exp(m - bc)
    it = dict(d=d, r0=r0, st_ref=st_ref, o_ref=o_ref, o_lane0=o_lane0, n_heads=n_heads, dk=dk, dv=dv, incl=incl,
              c=c, kt=(kn * jnp.exp(btot - m)).astype(BF16), dec=jnp.exp(btot), v=v_all.astype(BF16),
              want_out=q_all is not None)
    if q_all is not None:
        qe = q_all.astype(F32) * jnp.exp(bc)
        it.update(qd=(qe * jnp.exp(-m)).astype(BF16), qe=qe.astype(BF16), knb=kn.astype(BF16))
    return it


def _gla_intra(it):
    dk = it["dk"]
    it["a"] = [jnp.where(it["incl"], _dot_nt(it["qd"][:, hd * dk:(hd + 1) * dk], it["knb"][:, hd * dk:(hd + 1) * dk]),
                         0.0).astype(BF16) for hd in range(it["n_heads"])]


def _gla_advance(it):
    d, dk, dv, c, st_ref = it["d"], it["dk"], it["dv"], it["c"], it["st_ref"]
    sts = [st_ref[d, hd] for hd in range(it["n_heads"])]
    if it["want_out"]:
        for hd in range(it["n_heads"]):
            v = it["v"][:, hd * dv:(hd + 1) * dv]
            o = _dot(it["a"][hd], v) + _dot_nt(it["qe"][:, hd * dk:(hd + 1) * dk], sts[hd].astype(BF16))
            it["o_ref"][it["r0"]:it["r0"] + c, it["o_lane0"] + hd * dv:it["o_lane0"] + (hd + 1) * dv] = o.astype(BF16)
    for hd in range(it["n_heads"]):
        ks = slice(hd * dk, (hd + 1) * dk)
        st_ref[d, hd] = sts[hd] * it["dec"][:, ks] + _dot_tn(it["v"][:, hd * dv:(hd + 1) * dv], it["kt"][:, ks])


def _mix1_body(dirs, n_sub, sh_ref, sg_ref):
    c = MIX1_CHUNK
    gw = GLA_HEADS * GLA_DK
    lane = lambda seg: slice(seg[1] * SEG, (seg[1] + 1) * SEG)
    prepared = {}
    for d, (sh_in, hk_ref, hlf_ref, gld_ref, o_ref) in enumerate(dirs):
        incl = _scan_masks(c, d)[0]
        m_incl = jnp.where(incl, 1.0, 0.0)
        g0 = S_GQK[1] * SEG
        for ci in range(n_sub):
            r0 = ci * c
            rs = slice(r0, r0 + c)
            prepared[d, ci, 0] = _gla_stream(
                d, None if o_ref is None else sh_in[rs, lane(S_HQ)], hk_ref[rs, :], hlf_ref[rs, :],
                sh_in[rs, lane(S_HV)], sh_ref, o_ref, 0, r0, HG_HEADS, HG_D, HG_D, incl, m_incl)
            prepared[d, ci, 1] = _gla_stream(
                d, None if o_ref is None else sh_in[rs, g0:g0 + gw], sh_in[rs, g0 + gw:g0 + 2 * gw],
                gld_ref[rs, d * gw:(d + 1) * gw], sh_in[rs, lane(S_GV)],
                sg_ref, o_ref, HG_HEADS * HG_D, r0, GLA_HEADS, GLA_DK, GLA_DV, incl, m_incl)
    for it in prepared.values():
        if it["want_out"]:
            _gla_intra(it)
    for step in range(n_sub):
        for d in range(2):
            for stream in range(2):
                _gla_advance(prepared[d, step if d == 0 else n_sub - 1 - step, stream])


def _mix1_ctx_kernel(*refs, n_sub):
    fwd, bwd, (sh_ref, sg_ref) = refs[0:4], refs[4:8], refs[8:10]

    @pl.when(pl.program_id(1) == 0)
    def _():
        sh_ref[...] = jnp.zeros_like(sh_ref)
        sg_ref[...] = jnp.zeros_like(sg_ref)

    _mix1_body([tuple(r) + (None,) for r in (fwd, bwd)], n_sub, sh_ref, sg_ref)


def _mix1_lat_kernel(*refs, n_sub):
    fwd, bwd = refs[0:4], refs[4:8]
    sh0_ref, sg0_ref, of_ref, ob_ref, sh_scr, sg_scr = refs[8:14]

    @pl.when(pl.program_id(1) == 0)
    def _():
        sh_scr[...] = sh0_ref[...]
        sg_scr[...] = sg0_ref[...]

    _mix1_body([tuple(fwd) + (of_ref,), tuple(bwd) + (ob_ref,)], n_sub, sh_scr, sg_scr)


def _mix1_specs(p1, rows, n_steps):
    specs, args = [], []
    for d in range(2):
        blk = (lambda b, s: s) if d == 0 else (lambda b, s: n_steps - 1 - s)
        for (arr, sg), n_seg in ((S_HQ, N_SHARED), ((S_HK0, S_HK1)[d], 1), ((S_HLF0, S_HLF1)[d], 1), (S_GLD, 1)):
            assert sg % n_seg == 0
            specs.append(pl.BlockSpec((None, rows, n_seg * SEG),
                                      lambda b, s, blk=blk, sg=sg // n_seg: (b, blk(b, s), sg)))
            args.append(p1[arr])
    return specs, args


_SH_SHAPE = (2, HG_HEADS, HG_D, HG_D)
_SG_SHAPE = (2, GLA_HEADS, GLA_DV, GLA_DK)


def _mix1_ctx(p1c, *, rows):
    bsz, ctx_len, _ = p1c[0].shape
    n_steps = ctx_len // rows
    specs, args = _mix1_specs(p1c, rows, n_steps)
    state = lambda shape: pl.BlockSpec((None,) + shape, lambda b, s: (b, 0, 0, 0, 0))
    return pl.pallas_call(
        functools.partial(_mix1_ctx_kernel, n_sub=rows // MIX1_CHUNK),
        grid=(bsz, n_steps),
        in_specs=specs,
        out_specs=[state(_SH_SHAPE), state(_SG_SHAPE)],
        out_shape=[jax.ShapeDtypeStruct((bsz,) + _SH_SHAPE, F32), jax.ShapeDtypeStruct((bsz,) + _SG_SHAPE, F32)],
        compiler_params=_params(("arbitrary", "arbitrary"), 32 << 20),
        name="l1_ctx_state",
    )(*args)


def _mix1_lat(p1l, sh0, sg0, *, rows):
    bsz, seq, _ = p1l[0].shape
    n_steps = seq // rows
    specs, args = _mix1_specs(p1l, rows, n_steps)
    state = lambda shape: pl.BlockSpec((None,) + shape, lambda b, s: (b, 0, 0, 0, 0))
    ow = HG_HEADS * HG_D + GLA_HEADS * GLA_DV
    return pl.pallas_call(
        functools.partial(_mix1_lat_kernel, n_sub=rows // MIX1_CHUNK),
        grid=(bsz, n_steps),
        in_specs=specs + [state(_SH_SHAPE), state(_SG_SHAPE)],
        out_specs=[pl.BlockSpec((None, rows, ow), lambda b, s: (b, s, 0)),
                   pl.BlockSpec((None, rows, ow), lambda b, s: (b, n_steps - 1 - s, 0))],
        out_shape=[jax.ShapeDtypeStruct((bsz, seq, ow), BF16)] * 2,
        scratch_shapes=[pltpu.VMEM(_SH_SHAPE, F32), pltpu.VMEM(_SG_SHAPE, F32)],
        compiler_params=_params(("arbitrary", "arbitrary"), 32 << 20),
        name="l1_scan",
    )(*args, sh0, sg0)


def _outproj1(o0_ref, o1_ref, gate_ref, h_ref, mod_ref, cng_ref, dng_ref, w_ref, r0, rows):
    rs = slice(r0, r0 + rows)
    parts = []
    for hd in range(HG_HEADS + GLA_HEADS):
        lo = hd * 128
        y = o0_ref[rs, lo:lo + 128].astype(F32) + o1_ref[rs, lo:lo + 128].astype(F32)
        ng = cng_ref[...] if hd < HG_HEADS else dng_ref[...]
        parts.append((_head_norm(y, ng) * gate_ref[rs, lo:lo + 128].astype(F32)).astype(BF16))
    return h_ref[rs, :] + mod_ref[2:3, :] * _dot(jnp.concatenate(parts, axis=-1), w_ref[...])


def _for_pieces(length, fn):
    done = 0
    for rows in MOE_SEG_PIECES:
        n = (length - done) // rows

        def body(p, carry, rows=rows, done=done):
            fn(done + p * rows, rows)
            return carry
        lax.fori_loop(0, n, body, 0)
        done = done + n * rows


def _moe_route_kernel(o0_ref, o1_ref, gate_ref, h_ref, mod_ref, cng_ref, dng_ref, wo_ref, ng_ref, rw_ref, rb_ref,
                      h2_ref, slot_ref, seg_ref, xs_hbm, xn_scr, lg_scr, xg_scr, zero_scr, base_smem, seg_smem, sem,
                      *, tk):
    i = pl.program_id(0)
    last_step = pl.num_programs(0) - 1
    pc = MOE_PIECE
    gr = MOE_GATHER_ROWS
    cur = lax.rem(i, 2)
    buf = 1 - cur

    def segment_copies(slot, issue):
        for e in range(N_EXPERTS):
            off_e = seg_smem[slot, e]
            base_e = seg_smem[slot, 2 * N_EXPERTS + e]
            _for_pieces(seg_smem[slot, N_EXPERTS + e], lambda r, rows, e=e, off_e=off_e, base_e=base_e: issue(
                pltpu.make_async_copy(xg_scr.at[slot, pl.ds(pl.multiple_of(off_e + r, pc), rows), :],
                                      xs_hbm.at[e, pl.ds(pl.multiple_of(base_e + r, pc), rows), :], sem.at[slot])))

    @pl.when(i == 0)
    def _():
        for e in range(N_EXPERTS):
            base_smem[e] = 0
        zero_scr[...] = jnp.zeros_like(zero_scr)
        lg_scr[1] = jnp.zeros_like(lg_scr[1])

    @pl.when(i >= 3)
    def _():
        segment_copies(buf, lambda cp: cp.wait())

    lg = lg_scr[buf]
    half = tk // 2
    for r0 in (0, half):
        h2 = _outproj1(o0_ref, o1_ref, gate_ref, h_ref, mod_ref, cng_ref, dng_ref, wo_ref, r0, half)
        h2_ref[r0:r0 + half, :] = h2
        xm = _normmod(h2, ng_ref[...], mod_ref[3:4, :], mod_ref[4:5, :])
        xn_scr[cur, r0:r0 + half, :] = xm.astype(BF16)
        lg_scr[cur, :, r0:r0 + half] = jnp.transpose(_dot_hilo(xm, rw_ref[...]))[0:N_EXPERTS, :] + rb_ref[...]

    routed = jnp.where(i >= 1, 1.0, 0.0)
    eidx = lax.broadcasted_iota(jnp.int32, lg.shape, 0).astype(F32)
    m1 = jnp.max(lg, axis=0, keepdims=True)
    i1 = jnp.min(jnp.where(lg == m1, eidx, float(N_EXPERTS)), axis=0, keepdims=True)
    lg2 = jnp.where(eidx == i1, -jnp.inf, lg)
    m2 = jnp.max(lg2, axis=0, keepdims=True)
    i2 = jnp.min(jnp.where(lg2 == m2, eidx, float(N_EXPERTS)), axis=0, keepdims=True)
    ex = jnp.exp(m2 - m1)
    p1 = 1.0 / (1.0 + ex)
    sel = (jnp.where(eidx == i1, 1.0, 0.0) + jnp.where(eidx == i2, 1.0, 0.0)) * routed
    lane = lax.broadcasted_iota(jnp.int32, lg.shape, 1)
    cum = sel
    sh = 1
    while sh < tk:
        cum = cum + jnp.where(lane >= sh, pltpu.roll(cum, sh, 1), 0.0)
        sh *= 2
    padded = jnp.floor((cum[:, tk - 1:tk] + (pc - 1.0)) * (1.0 / pc)) * pc
    padded = jnp.broadcast_to(padded, (N_EXPERTS, 128))
    er = lax.broadcasted_iota(jnp.int32, (N_EXPERTS, N_EXPERTS), 0)
    ec = lax.broadcasted_iota(jnp.int32, (N_EXPERTS, N_EXPERTS), 1)
    off = _dot_hi(jnp.where(er > ec, 1.0, 0.0), padded)
    slot = off[:, 0:1] + cum - 1.0
    slot_a = jnp.sum(jnp.where(eidx == i1, slot, 0.0), axis=0, keepdims=True)
    slot_b = jnp.sum(jnp.where(eidx == i2, slot, 0.0), axis=0, keepdims=True)
    slot_ref[...] = jnp.concatenate([slot_a, slot_b, p1, ex * p1, jnp.zeros((4, tk), F32)], axis=0)

    total = jnp.max(off[N_EXPERTS - 1:N_EXPERTS, :] + padded[N_EXPERTS - 1:N_EXPERTS, :]).astype(jnp.int32)

    def gather(ci, carry):
        r0 = pl.multiple_of(ci * gr, gr)
        rid = (lax.broadcasted_iota(jnp.int32, (gr, tk), 0) + r0).astype(F32)
        p = jnp.where(rid == slot_a, 1.0, 0.0) + jnp.where(rid == slot_b, 1.0, 0.0)
        xg_scr[buf, pl.ds(r0, gr), :] = _dot(p.astype(BF16), xn_scr[buf]).astype(BF16)
        return carry
    lax.fori_loop(0, (total + gr - 1) // gr, gather, 0)

    erow = lax.broadcasted_iota(jnp.int32, (N_EXPERTS, 128), 0)
    base_vec = jnp.zeros((N_EXPERTS, 128), F32)
    for e in range(N_EXPERTS):
        len_e = jnp.max(padded[e:e + 1, :]).astype(jnp.int32)
        base_e = base_smem[e]
        base_vec = jnp.where(erow == e, base_e.astype(F32), base_vec)
        seg_smem[buf, e] = jnp.max(off[e:e + 1, :]).astype(jnp.int32)
        seg_smem[buf, N_EXPERTS + e] = len_e
        seg_smem[buf, 2 * N_EXPERTS + e] = base_e
        base_smem[e] = base_e + len_e
    seg_ref[0] = off
    seg_ref[1] = padded
    seg_ref[2] = base_vec
    segment_copies(buf, lambda cp: cp.start())

    @pl.when(i == last_step)
    def _():
        @pl.when(i >= 2)
        def _():
            segment_copies(1 - buf, lambda cp: cp.wait())
        segment_copies(buf, lambda cp: cp.wait())

        def tail_copy(e, p):
            end = base_smem[e]
            return pltpu.make_async_copy(zero_scr, xs_hbm.at[e, pl.ds(pl.multiple_of(end + p * pc, pc), pc), :],
                                         sem.at[0])

        def n_tail(e):
            rem = lax.rem(base_smem[e], MOE_BLOCK)
            return jnp.where(rem == 0, 0, MOE_BLOCK - rem) // pc

        for e in range(N_EXPERTS):
            def start(p, carry, e=e):
                tail_copy(e, p).start()
                return carry
            lax.fori_loop(0, n_tail(e), start, 0)
        for e in range(N_EXPERTS):
            def wait(p, carry, e=e):
                tail_copy(e, p).wait()
                return carry
            lax.fori_loop(0, n_tail(e), wait, 0)


def _moe_ffn_kernel(eid_ref, blk_ref, nv_ref, x_ref, wg_ref, wu_ref, wd_ref, o_ref, *, n_chunks):
    del eid_ref, blk_ref

    @pl.when(pl.program_id(0) < nv_ref[0])
    def _():
        x = x_ref[...]
        cw = D_FF // n_chunks
        acc = jnp.zeros(x.shape, F32)
        for ci in range(n_chunks):
            c0 = ci * cw
            act = (_silu(_dot(x, wg_ref[:, c0:c0 + cw])) * _dot(x, wu_ref[:, c0:c0 + cw])).astype(BF16)
            acc = acc + _dot(act, wd_ref[c0:c0 + cw, :])
        o_ref[...] = acc.astype(BF16)


def _moe_combine_kernel(base_ref, len_ref, off_ref, h_ref, mod_ref, fg_ref, slot_ref, og_hbm, out_ref,
                        og_scr, sem, *, tk):
    i = pl.program_id(0)
    pc = MOE_PIECE
    gr = MOE_GATHER_ROWS
    n_rows = og_scr.shape[1]
    buf = lax.rem(i, 2)

    def segments(tile, slot, issue):
        def seg_copy(e, r, rows):
            return pltpu.make_async_copy(
                og_hbm.at[e, pl.ds(pl.multiple_of(base_ref[tile * N_EXPERTS + e] + r, pc), rows), :],
                og_scr.at[slot, pl.ds(pl.multiple_of(off_ref[tile * N_EXPERTS + e] + r, pc), rows), :], sem.at[slot])
        for e in range(N_EXPERTS):
            _for_pieces(len_ref[tile * N_EXPERTS + e], lambda r, rows, e=e: issue(seg_copy(e, r, rows)))

    @pl.when(i == 0)
    def _():
        segments(0, 0, lambda cp: cp.start())

    @pl.when(i + 1 < pl.num_programs(0))
    def _():
        segments(i + 1, 1 - buf, lambda cp: cp.start())

    last = i * N_EXPERTS + N_EXPERTS - 1
    total = off_ref[last] + len_ref[last]

    def clear(p, carry):
        og_scr[buf, pl.ds(pl.multiple_of(p * pc, pc), pc), :] = jnp.zeros((pc, og_scr.shape[2]), BF16)
        return carry
    lax.fori_loop(total // pc, n_rows // pc, clear, 0)
    segments(i, buf, lambda cp: cp.wait())

    def chunk(r0):
        rid = (lax.broadcasted_iota(jnp.int32, (gr, tk), 0) + r0).astype(F32)
        pg = (jnp.where(rid == slot_ref[0:1, :], slot_ref[2:3, :], 0.0)
              + jnp.where(rid == slot_ref[1:2, :], slot_ref[3:4, :], 0.0))
        return _dot_tn(pg.astype(BF16), og_scr[buf, pl.ds(r0, gr), :])

    n_always = (2 * tk) // gr
    acc = chunk(0)
    for ci in range(1, n_always):
        acc = acc + chunk(ci * gr)
    out_ref[...] = acc

    def scatter(ci, carry):
        out_ref[...] = out_ref[...] + chunk(pl.multiple_of(ci * gr, gr))
        return carry
    lax.fori_loop(n_always, (total + gr - 1) // gr, scatter, 0)
    h3 = h_ref[...] + mod_ref[5:6, :] * out_ref[...]
    out_ref[...] = h3 * lax.rsqrt(jnp.mean(h3 * h3, axis=-1, keepdims=True) + EPS) * fg_ref[...]


def _moe_block_table(seg, n_blocks):
    ends = (seg[-1, 2, :, 0] + seg[-1, 1, :, 0]).astype(jnp.int32)
    nblk = (ends + MOE_BLOCK - 1) // MOE_BLOCK
    cum = jnp.cumsum(nblk)
    n_valid = cum[-1]
    g = jnp.minimum(jnp.arange(n_blocks, dtype=jnp.int32), n_valid - 1)
    eid = jnp.sum((g[:, None] >= cum[None, :]).astype(jnp.int32), axis=1)
    blk = g - (cum - nblk)[eid]
    return eid, blk, n_valid.reshape(1)


def _moe(o0, o1, gates, gate_blk, h, modtab, cng, dng, wo, ng, fg, rw, rb, wg, wu, wd, *, tk):
    bsz, seq, d = h.shape
    tpb = seq // tk
    n_tiles = bsz * tpb
    n_tok = bsz * seq
    tile_rows = -(-(2 * tk + N_EXPERTS * MOE_PIECE) // MOE_GATHER_ROWS) * MOE_GATHER_ROWS
    cap = -(-(n_tok + n_tiles * MOE_PIECE) // MOE_BLOCK) * MOE_BLOCK
    n_blocks = -(-(2 * n_tok + n_tiles * N_EXPERTS * MOE_PIECE) // MOE_BLOCK) + N_EXPERTS
    tok = lambda i, *_: (i // tpb, i % tpb, 0)
    mod = lambda i, *_: (i // tpb, 1, 0, 0)

    proj = lambda i: jnp.minimum(i, n_tiles - 1)
    routed = lambda i: jnp.maximum(i - 1, 0)
    tok_spec = pl.BlockSpec((None, tk, d), lambda i: tok(proj(i)))
    h2, slots, seg, xs = pl.pallas_call(
        functools.partial(_moe_route_kernel, tk=tk),
        grid=(n_tiles + 1,),
        in_specs=[tok_spec, tok_spec,
                  pl.BlockSpec((None, tk, d), lambda i: (proj(i) // tpb, proj(i) % tpb, gate_blk)),
                  tok_spec, pl.BlockSpec((None, None, 6, d), lambda i: mod(proj(i))),
                  _const_spec((1, 128)), _const_spec((1, 128)), _const_spec((d, d)),
                  _const_spec((1, d)), _const_spec((d, 128)), _const_spec((N_EXPERTS, 1))],
        out_specs=[tok_spec,
                   pl.BlockSpec((None, 8, tk), lambda i: (routed(i), 0, 0)),
                   pl.BlockSpec((None, 3, N_EXPERTS, 128), lambda i: (routed(i), 0, 0, 0)),
                   pl.BlockSpec(memory_space=pl.ANY)],
        out_shape=[jax.ShapeDtypeStruct(h.shape, F32),
                   jax.ShapeDtypeStruct((n_tiles, 8, tk), F32),
                   jax.ShapeDtypeStruct((n_tiles, 3, N_EXPERTS, 128), F32),
                   jax.ShapeDtypeStruct((N_EXPERTS, cap, d), BF16)],
        scratch_shapes=[pltpu.VMEM((2, tk, d), BF16), pltpu.VMEM((2, N_EXPERTS, tk), F32),
                        pltpu.VMEM((2, tile_rows, d), BF16), pltpu.VMEM((MOE_PIECE, d), BF16),
                        pltpu.SMEM((N_EXPERTS,), jnp.int32), pltpu.SMEM((2, 3 * N_EXPERTS), jnp.int32),
                        pltpu.SemaphoreType.DMA((2,))],
        compiler_params=_params(("arbitrary",), 40 << 20),
        name="l1_moe_route",
    )(o0, o1, gates, h, modtab, cng, dng, wo, ng, _pad_cols(rw, 128), rb)

    eid, blk, n_valid = _moe_block_table(seg, n_blocks)
    x_spec = pl.BlockSpec((None, MOE_BLOCK, d), lambda g, eid, blk, nv: (eid[g], blk[g], 0))
    og = pl.pallas_call(
        functools.partial(_moe_ffn_kernel, n_chunks=11),
        grid_spec=pltpu.PrefetchScalarGridSpec(
            num_scalar_prefetch=3, grid=(n_blocks,),
            in_specs=[x_spec,
                      pl.BlockSpec((None, d, D_FF), lambda g, eid, blk, nv: (eid[g], 0, 0)),
                      pl.BlockSpec((None, d, D_FF), lambda g, eid, blk, nv: (eid[g], 0, 0)),
                      pl.BlockSpec((None, D_FF, d), lambda g, eid, blk, nv: (eid[g], 0, 0))],
            out_specs=x_spec),
        out_shape=jax.ShapeDtypeStruct((N_EXPERTS, cap, d), BF16),
        compiler_params=_params(("arbitrary",), 52 << 20),
        name="l1_moe_experts",
    )(eid, blk, n_valid, xs, wg, wu, wd)

    tab = lambda k: seg[:, k, :, 0].astype(jnp.int32).reshape(-1)
    return pl.pallas_call(
        functools.partial(_moe_combine_kernel, tk=tk),
        grid_spec=pltpu.PrefetchScalarGridSpec(
            num_scalar_prefetch=3, grid=(n_tiles,),
            in_specs=[pl.BlockSpec((None, tk, d), tok), pl.BlockSpec((None, None, 6, d), mod),
                      pl.BlockSpec((1, d), lambda i, *_: (0, 0)),
                      pl.BlockSpec((None, 8, tk), lambda i, *_: (i, 0, 0)),
                      pl.BlockSpec(memory_space=pl.ANY)],
            out_specs=pl.BlockSpec((None, tk, d), tok),
            scratch_shapes=[pltpu.VMEM((2, tile_rows, d), BF16), pltpu.SemaphoreType.DMA((2,))]),
        out_shape=jax.ShapeDtypeStruct(h.shape, F32),
        compiler_params=_params(("arbitrary",), 40 << 20),
        name="l1_moe_combine",
    )(tab(2), tab(1), tab(0), h2, modtab, fg, slots, og)


def _block_diag_gate(gate_w):
    w = gate_w.reshape(2, 2, 2, 4, RG_BLOCK, RG_BLOCK)
    eye = jnp.eye(4, dtype=gate_w.dtype)
    return jnp.einsum('dghbij,bc->dghbicj', w, eye).reshape(2, 2, 2, 256, 256)


def _pad_cols(w, n):
    return jnp.pad(w, ((0, 0), (0, n - w.shape[1])))


def _layer0(ctx, x, modtab, norm_mix_g, norm_ffn_g, e_w_in, e_w_out, e_a_conv_w, e_a_conv_b, e_a_gate_w, e_a_gate_b,
            e_a_lambda, e_b_conv_w, e_b_a_log, e_b_dt_bias, e_b_norm_g, e_ffn_w_gate, e_ffn_w_up, e_ffn_w_down,
            *, tm, tt):
    bsz, ctx_len, d = ctx.shape
    w_in, w_tail = e_w_in.astype(BF16), _pad_cols(e_w_in[:, E_IN_MAIN:], 128).astype(BF16)
    gpar = jnp.zeros((2, 128), F32)
    gpar = gpar.at[0, 2 * DN_HEADS:4 * DN_HEADS].set(e_b_a_log.reshape(-1))
    gpar = gpar.at[1, 2 * DN_HEADS:4 * DN_HEADS].set(e_b_dt_bias.reshape(-1))
    ua, gay, q, k, v, sz, gb = _inproj0(ctx, x, modtab, norm_mix_g.reshape(1, d), w_in, w_tail, e_a_conv_w,
                                        e_a_conv_b.reshape(1, -1), e_b_conv_w, gpar, tm=tm)
    wg = _block_diag_gate(e_a_gate_w).astype(BF16)
    hf, hb = _rglru(ua, wg, e_a_gate_b.reshape(4, RG_WIDTH), e_a_lambda, tt=tt, ctx_len=ctx_len)
    o0, o1 = _delta(q, k, v, gb, ctx_len=ctx_len, rows=SCAN_ROWS)
    return _l0_tail(hf, hb, gay, o0, o1, sz, ctx, x, modtab, e_b_norm_g.reshape(1, -1), e_w_out.astype(BF16),
                    norm_ffn_g.reshape(1, d), e_ffn_w_gate.astype(BF16), e_ffn_w_up.astype(BF16),
                    e_ffn_w_down.astype(BF16), tm=tm)


def _layer1(hc, hl, modtab, norm_mix_g, norm_ffn_g, final_norm_g, o_w_in, o_w_out, o_lb_logits, o_c_norm_g,
            o_d_gate_w2, o_d_gate_b2, o_d_norm_g, o_router_w, o_router_b, o_moe_w_gate, o_moe_w_up, o_moe_w_down,
            *, tm, tk, layer):
    bsz, seq, d = hl.shape
    ctx_len = hc.shape[1]
    rows = seq // GRID_W
    hl = hl.reshape(bsz, rows, GRID_W, d).swapaxes(1, 2).reshape(bsz, seq, d)
    w_in, w_tail = o_w_in.astype(BF16), _pad_cols(o_w_in[:, O_IN_MAIN:], 128).astype(BF16)
    wlr = jnp.zeros((128, SEG), F32)
    wlr = wlr.at[0:GLA_RANK, 0:256].set(o_d_gate_w2[0]).at[GLA_RANK:2 * GLA_RANK, 256:512].set(o_d_gate_w2[1])
    proj = functools.partial(_inproj1, g=norm_mix_g.reshape(1, d), w=w_in, w_tail=w_tail, lbl=o_lb_logits, wlr=wlr,
                             b2=o_d_gate_b2.reshape(1, SEG), layer=layer)
    p1c = proj(hc, 0, ctx_len, modtab, 0, tm=tm)
    p1l = proj(hl, 0, seq, modtab, 1, tm=2 * tm)
    sh0, sg0 = _mix1_ctx(p1c, rows=SCAN_ROWS)
    o0, o1 = _mix1_lat(p1l, sh0, sg0, rows=SCAN_ROWS)
    return _moe(o0, o1, p1l[S_CG[0]], S_CG[1] * SEG // d, hl, modtab, o_c_norm_g.reshape(1, -1),
                o_d_norm_g.reshape(1, -1), o_w_out.astype(BF16), norm_ffn_g.reshape(1, d),
                final_norm_g.reshape(1, d), o_router_w, o_router_b.reshape(N_EXPERTS, 1),
                o_moe_w_gate.astype(BF16), o_moe_w_up.astype(BF16), o_moe_w_down.astype(BF16), tk=tk)


def kernel(x, c, ctx, c_ctx, ada_w, ada_b, norm_mix_g, norm_ffn_g, final_norm_g, e_w_in, e_w_out, e_a_conv_w, e_a_conv_b, e_a_gate_w, e_a_gate_b, e_a_lambda, e_b_conv_w, e_b_a_log, e_b_dt_bias, e_b_norm_g, e_ffn_w_gate, e_ffn_w_up, e_ffn_w_down, o_w_in, o_w_out, o_lb_logits, o_c_norm_g, o_d_gate_w2, o_d_gate_b2, o_d_norm_g, o_router_w, o_router_b, o_moe_w_gate, o_moe_w_up, o_moe_w_down):
    bsz, seq, d = x.shape
    ctx_len = ctx.shape[1]
    assert bsz == 8 and d == D_MODEL and ada_w.shape[0] == 2
    tm = min(256, ctx_len)
    tt = min(128, ctx_len)
    tk = min(512, seq)
    assert ctx_len % tm == 0 and seq % tm == 0 and ctx_len % SCAN_ROWS == 0 and seq % SCAN_ROWS == 0
    assert seq % GRID_W == 0 and seq % tk == 0

    mods = _ada(c, c_ctx, ada_w, ada_b)
    hc, hl = _layer0(ctx, x, _modtab(mods[0], bsz), norm_mix_g[0], norm_ffn_g[0], e_w_in[0], e_w_out[0],
                     e_a_conv_w[0], e_a_conv_b[0], e_a_gate_w[0], e_a_gate_b[0], e_a_lambda[0], e_b_conv_w[0],
                     e_b_a_log[0], e_b_dt_bias[0], e_b_norm_g[0], e_ffn_w_gate[0], e_ffn_w_up[0], e_ffn_w_down[0],
                     tm=tm, tt=tt)
    out_cm = _layer1(hc, hl, _modtab(mods[1], bsz), norm_mix_g[1], norm_ffn_g[1], final_norm_g, o_w_in[0],
                     o_w_out[0], o_lb_logits, o_c_norm_g[0], o_d_gate_w2[0], o_d_gate_b2[0], o_d_norm_g[0],
                     o_router_w[0], o_router_b[0], o_moe_w_gate[0], o_moe_w_up[0], o_moe_w_down[0],
                     tm=tm, tk=tk, layer=1)
    rows = seq // GRID_W
    return out_cm.reshape(bsz, GRID_W, rows, d).swapaxes(1, 2).reshape(bsz, seq, d)
```
